```python
import jax, jax.numpy as jnp
from jax import lax
import numpy as np

D_MODEL = 1024
BATCH = 16
SEQ = 4096
DEPTH = 2

FOX_HEADS = 8
FOX_HEAD_DIM = 64
FOX_WIDTH = FOX_HEADS * FOX_HEAD_DIM
GDN_HEADS = 4
GDN_HEAD_DIM = 128
GDN_WIDTH = GDN_HEADS * GDN_HEAD_DIM
MIX_WIDTH = FOX_WIDTH + GDN_WIDTH
CONV_WIDTH = 4
CHUNK = 64
Q_BLOCK = 128
D_FF = 2816
EPS = 1e-6
FORGET_BIAS_INIT = 3.0
SPLIT_SIZES = (FOX_WIDTH, FOX_WIDTH, FOX_WIDTH, FOX_HEADS,
               GDN_WIDTH, GDN_WIDTH, GDN_WIDTH, GDN_HEADS, GDN_HEADS, GDN_WIDTH)
N_IN = 3 * FOX_WIDTH + FOX_HEADS + 4 * GDN_WIDTH + 2 * GDN_HEADS

kernel_name = "macaron_fox_gdn_hybrid"


def rms_norm(x, w):
    xf = x.astype(jnp.float32)
    y = xf * lax.rsqrt(jnp.mean(xf * xf, axis=-1, keepdims=True) + EPS)
    return (y * w.astype(jnp.float32)).astype(x.dtype)


def l2_norm(x):
    xf = x.astype(jnp.float32)
    return xf * lax.rsqrt(jnp.sum(xf * xf, axis=-1, keepdims=True) + EPS)


def swiglu_ffn(x, w_in, w_out):
    gate, up = jnp.split(x @ w_in, 2, axis=-1)
    return (jax.nn.silu(gate) * up) @ w_out


def causal_depthwise_conv(x, w):
    c = x.shape[-1]
    return lax.conv_general_dilated(
        x, w[:, None, :].astype(x.dtype), window_strides=(1,), padding=[(CONV_WIDTH - 1, 0)],
        dimension_numbers=("NWC", "WIO", "NWC"), feature_group_count=c)


def forgetting_attention(q, k, v, f_logit):
    seq = q.shape[2]
    scale = FOX_HEAD_DIM ** -0.5
    cum = jnp.cumsum(jax.nn.log_sigmoid(f_logit.astype(jnp.float32)), axis=-1)
    outs = []
    for blk in range(seq // Q_BLOCK):
        start, end = blk * Q_BLOCK, (blk + 1) * Q_BLOCK
        qb = q[:, :, start:end]
        kb = k[:, :, :end]
        vb = v[:, :, :end]
        s = jnp.einsum("bhqd,bhkd->bhqk", qb, kb).astype(jnp.float32) * scale
        s = s + cum[:, :, start:end, None] - cum[:, :, None, :end]
        causal = (start + jnp.arange(Q_BLOCK))[:, None] >= jnp.arange(end)[None, :]
        s = jnp.where(causal, s, -jnp.inf)
        p = jax.nn.softmax(s, axis=-1)
        outs.append(jnp.einsum("bhqk,bhkd->bhqd", p.astype(v.dtype), vb))
    return jnp.concatenate(outs, axis=2)


def gated_delta_rule_chunked(q, k, v, g, beta):
    out_dtype = v.dtype
    b, h, seq, dk = q.shape
    dv = v.shape[-1]
    n = seq // CHUNK
    q = q.astype(jnp.float32) * dk ** -0.5
    k = k.astype(jnp.float32)
    v = v.astype(jnp.float32)
    g = g.astype(jnp.float32).reshape(b, h, n, CHUNK)
    beta = beta.astype(jnp.float32).reshape(b, h, n, CHUNK)
    q = q.reshape(b, h, n, CHUNK, dk)
    k = k.reshape(b, h, n, CHUNK, dk)
    v = v.reshape(b, h, n, CHUNK, dv)

    g = jnp.cumsum(g, axis=-1)
    tri_incl = jnp.tril(jnp.ones((CHUNK, CHUNK), dtype=bool))
    tri_strict = jnp.tril(jnp.ones((CHUNK, CHUNK), dtype=bool), k=-1)
    decay = jnp.exp(jnp.where(tri_incl, g[..., :, None] - g[..., None, :], -jnp.inf))

    k_beta = k * beta[..., None]
    v_beta = v * beta[..., None]
    a_strict = jnp.where(tri_strict, jnp.einsum("bhnid,bhnjd->bhnij", k_beta, k) * decay, 0.0)
    eye = jnp.eye(CHUNK, dtype=jnp.float32)
    t_mat = lax.linalg.triangular_solve(eye + a_strict, jnp.broadcast_to(eye, a_strict.shape),
                                        left_side=True, lower=True, unit_diagonal=True)
    u = jnp.einsum("bhnij,bhnjd->bhnid", t_mat, v_beta)
    w = jnp.einsum("bhnij,bhnjd->bhnid", t_mat, k_beta * jnp.exp(g)[..., None])
    attn_intra = jnp.where(tri_incl, jnp.einsum("bhnid,bhnjd->bhnij", q, k) * decay, 0.0)
    g_last = g[..., -1]
    k_dec = k * jnp.exp(g_last[..., None] - g)[..., None]
    q_dec = q * jnp.exp(g)[..., None]

    def step(state, inp):
        qd, kd, uc, wc, ac, gl = inp
        v_new = uc - jnp.einsum("bhck,bhkv->bhcv", wc, state)
        o = jnp.einsum("bhck,bhkv->bhcv", qd, state) + jnp.einsum("bhij,bhjv->bhiv", ac, v_new)
        state = state * jnp.exp(gl)[..., None, None] + jnp.einsum("bhck,bhcv->bhkv", kd, v_new)
        return state, o

    xs = tuple(jnp.moveaxis(t, 2, 0) for t in (q_dec, k_dec, u, w, attn_intra, g_last))
    state0 = jnp.zeros((b, h, dk, dv), jnp.float32)
    _, o = lax.scan(step, state0, xs)
    o = jnp.moveaxis(o, 0, 2).reshape(b, h, seq, dv)
    return o.astype(out_dtype)


def hybrid_mixer(hn, w_in, fox_q_norm, fox_k_norm, fox_f_bias,
                 gdn_conv, gdn_a_log, gdn_dt_bias, gdn_out_norm, w_out):
    b, s, _ = hn.shape
    proj = hn @ w_in
    offsets = []
    acc = 0
    for size in SPLIT_SIZES[:-1]:
        acc += size
        offsets.append(acc)
    fq, fk, fv, ff, gq, gk, gv, ga, gb, gg = jnp.split(proj, offsets, axis=-1)

    def heads(t, n_h, d_h):
        return t.reshape(b, s, n_h, d_h).transpose(0, 2, 1, 3)

    fq = rms_norm(heads(fq, FOX_HEADS, FOX_HEAD_DIM), fox_q_norm)
    fk = rms_norm(heads(fk, FOX_HEADS, FOX_HEAD_DIM), fox_k_norm)
    fv = heads(fv, FOX_HEADS, FOX_HEAD_DIM)
    f_logit = (ff + fox_f_bias).transpose(0, 2, 1)
    y_fox = forgetting_attention(fq, fk, fv, f_logit)
    y_fox = y_fox.transpose(0, 2, 1, 3).reshape(b, s, FOX_WIDTH)

    qkv = jax.nn.silu(causal_depthwise_conv(jnp.concatenate([gq, gk, gv], axis=-1), gdn_conv))
    gq, gk, gv = jnp.split(qkv, 3, axis=-1)
    gq = l2_norm(heads(gq, GDN_HEADS, GDN_HEAD_DIM))
    gk = l2_norm(heads(gk, GDN_HEADS, GDN_HEAD_DIM))
    gv = heads(gv, GDN_HEADS, GDN_HEAD_DIM)
    beta = jax.nn.sigmoid(gb.astype(jnp.float32)).transpose(0, 2, 1)
    log_decay = (-jnp.exp(gdn_a_log.astype(jnp.float32))
                 * jax.nn.softplus(ga.astype(jnp.float32) + gdn_dt_bias.astype(jnp.float32))).transpose(0, 2, 1)
    y_gdn = gated_delta_rule_chunked(gq, gk, gv, log_decay, beta)
    y_gdn = rms_norm(y_gdn, gdn_out_norm) * jax.nn.silu(heads(gg, GDN_HEADS, GDN_HEAD_DIM))
    y_gdn = y_gdn.transpose(0, 2, 1, 3).reshape(b, s, GDN_WIDTH)

    return jnp.concatenate([y_fox, y_gdn], axis=-1) @ w_out


def _fwd_setup_inputs(seed: int = 0) -> dict:
    key = jax.random.key(seed)
    ks = jax.random.split(key, 20)
    L, D = DEPTH, D_MODEL

    def normal(k, shape, scale):
        return jax.random.normal(k, shape, jnp.float32) * scale

    def gain(k, shape):
        return 1.0 + 0.1 * jax.random.normal(k, shape, jnp.float32)

    return {
        "x": normal(ks[0], (BATCH, SEQ, D), 1.0),
        "ffn1_norm": gain(ks[1], (L, D)),
        "ffn1_w_in": normal(ks[2], (L, D, 2 * D_FF), D ** -0.5),
        "ffn1_w_out": normal(ks[3], (L, D_FF, D), D_FF ** -0.5),
        "mix_norm": gain(ks[4], (L, D)),
        "w_in": normal(ks[5], (L, D, N_IN), D ** -0.5),
        "fox_q_norm": gain(ks[6], (L, FOX_HEAD_DIM)),
        "fox_k_norm": gain(ks[7], (L, FOX_HEAD_DIM)),
        "fox_f_bias": FORGET_BIAS_INIT + 0.1 * jax.random.normal(ks[8], (L, FOX_HEADS), jnp.float32),
        "gdn_conv": normal(ks[9], (L, CONV_WIDTH, 3 * GDN_WIDTH), CONV_WIDTH ** -0.5),
        "gdn_a_log": jnp.log(jax.random.uniform(ks[10], (L, GDN_HEADS), jnp.float32, 1.0, 16.0)),
        "gdn_dt_bias": jnp.log(jnp.expm1(jax.random.uniform(ks[11], (L, GDN_HEADS), jnp.float32, 0.001, 0.1))),
        "gdn_out_norm": gain(ks[12], (L, GDN_HEAD_DIM)),
        "w_out": normal(ks[13], (L, MIX_WIDTH, D), MIX_WIDTH ** -0.5),
        "ffn2_norm": gain(ks[14], (L, D)),
        "ffn2_w_in": normal(ks[15], (L, D, 2 * D_FF), D ** -0.5),
        "ffn2_w_out": normal(ks[16], (L, D_FF, D), D_FF ** -0.5),
    }


def _fwd_reference(x, ffn1_norm, ffn1_w_in, ffn1_w_out, mix_norm, w_in, fox_q_norm, fox_k_norm,
              fox_f_bias, gdn_conv, gdn_a_log, gdn_dt_bias, gdn_out_norm, w_out,
              ffn2_norm, ffn2_w_in, ffn2_w_out):
    for l in range(DEPTH):
        x = x + 0.5 * swiglu_ffn(rms_norm(x, ffn1_norm[l]), ffn1_w_in[l], ffn1_w_out[l])
        x = x + hybrid_mixer(rms_norm(x, mix_norm[l]), w_in[l], fox_q_norm[l], fox_k_norm[l],
                             fox_f_bias[l], gdn_conv[l], gdn_a_log[l], gdn_dt_bias[l],
                             gdn_out_norm[l], w_out[l])
        x = x + 0.5 * swiglu_ffn(rms_norm(x, ffn2_norm[l]), ffn2_w_in[l], ffn2_w_out[l])
    return x


import jax as _jax
import jax.numpy as _jnp

TWIN_FORMAT = 'train_step'
FWD_PARAMS = ['x', 'ffn1_norm', 'ffn1_w_in', 'ffn1_w_out', 'mix_norm', 'w_in', 'fox_q_norm', 'fox_k_norm', 'fox_f_bias', 'gdn_conv', 'gdn_a_log', 'gdn_dt_bias', 'gdn_out_norm', 'w_out', 'ffn2_norm', 'ffn2_w_in', 'ffn2_w_out']
TWIN_WEIGHTS = ['ffn1_norm', 'ffn1_w_in', 'ffn1_w_out', 'mix_norm', 'w_in', 'fox_q_norm', 'fox_k_norm', 'fox_f_bias', 'gdn_conv', 'gdn_a_log', 'gdn_dt_bias', 'gdn_out_norm', 'w_out', 'ffn2_norm', 'ffn2_w_in', 'ffn2_w_out']
TWIN_DIFF_INPUT = 'x'
TWIN_INPUTS = ['x', 'ffn1_norm', 'ffn1_w_in', 'ffn1_w_out', 'mix_norm', 'w_in', 'fox_q_norm', 'fox_k_norm', 'fox_f_bias', 'gdn_conv', 'gdn_a_log', 'gdn_dt_bias', 'gdn_out_norm', 'w_out', 'ffn2_norm', 'ffn2_w_in', 'ffn2_w_out', 'loss_target', 'm_ffn1_norm', 'm_ffn1_w_in', 'm_ffn1_w_out', 'm_mix_norm', 'm_w_in', 'm_fox_q_norm', 'm_fox_k_norm', 'm_fox_f_bias', 'm_gdn_conv', 'm_gdn_a_log', 'm_gdn_dt_bias', 'm_gdn_out_norm', 'm_w_out', 'm_ffn2_norm', 'm_ffn2_w_in', 'm_ffn2_w_out', 'v_ffn1_norm', 'v_ffn1_w_in', 'v_ffn1_w_out', 'v_mix_norm', 'v_w_in', 'v_fox_q_norm', 'v_fox_k_norm', 'v_fox_f_bias', 'v_gdn_conv', 'v_gdn_a_log', 'v_gdn_dt_bias', 'v_gdn_out_norm', 'v_w_out', 'v_ffn2_norm', 'v_ffn2_w_in', 'v_ffn2_w_out']
TWIN_OUTPUTS = ['loss', 'grad_x', 'grad_ffn1_norm', 'grad_ffn1_w_in', 'grad_ffn1_w_out', 'grad_mix_norm', 'grad_w_in', 'grad_fox_q_norm', 'grad_fox_k_norm', 'grad_fox_f_bias', 'grad_gdn_conv', 'grad_gdn_a_log', 'grad_gdn_dt_bias', 'grad_gdn_out_norm', 'grad_w_out', 'grad_ffn2_norm', 'grad_ffn2_w_in', 'grad_ffn2_w_out', 'delta_ffn1_norm', 'delta_ffn1_w_in', 'delta_ffn1_w_out', 'delta_mix_norm', 'delta_w_in', 'delta_fox_q_norm', 'delta_fox_k_norm', 'delta_fox_f_bias', 'delta_gdn_conv', 'delta_gdn_a_log', 'delta_gdn_dt_bias', 'delta_gdn_out_norm', 'delta_w_out', 'delta_ffn2_norm', 'delta_ffn2_w_in', 'delta_ffn2_w_out', 'new_m_ffn1_norm', 'new_m_ffn1_w_in', 'new_m_ffn1_w_out', 'new_m_mix_norm', 'new_m_w_in', 'new_m_fox_q_norm', 'new_m_fox_k_norm', 'new_m_fox_f_bias', 'new_m_gdn_conv', 'new_m_gdn_a_log', 'new_m_gdn_dt_bias', 'new_m_gdn_out_norm', 'new_m_w_out', 'new_m_ffn2_norm', 'new_m_ffn2_w_in', 'new_m_ffn2_w_out', 'new_v_ffn1_norm', 'new_v_ffn1_w_in', 'new_v_ffn1_w_out', 'new_v_mix_norm', 'new_v_w_in', 'new_v_fox_q_norm', 'new_v_fox_k_norm', 'new_v_fox_f_bias', 'new_v_gdn_conv', 'new_v_gdn_a_log', 'new_v_gdn_dt_bias', 'new_v_gdn_out_norm', 'new_v_w_out', 'new_v_ffn2_norm', 'new_v_ffn2_w_in', 'new_v_ffn2_w_out']
TWIN_LEAF_KINDS = {'loss': 'loss', 'grad_x': 'grad_x', 'grad_ffn1_norm': 'grad_w', 'grad_ffn1_w_in': 'grad_w', 'grad_ffn1_w_out': 'grad_w', 'grad_mix_norm': 'grad_w', 'grad_w_in': 'grad_w', 'grad_fox_q_norm': 'grad_w', 'grad_fox_k_norm': 'grad_w', 'grad_fox_f_bias': 'grad_w', 'grad_gdn_conv': 'grad_w', 'grad_gdn_a_log': 'grad_w', 'grad_gdn_dt_bias': 'grad_w', 'grad_gdn_out_norm': 'grad_w', 'grad_w_out': 'grad_w', 'grad_ffn2_norm': 'grad_w', 'grad_ffn2_w_in': 'grad_w', 'grad_ffn2_w_out': 'grad_w', 'delta_ffn1_norm': 'delta_w', 'delta_ffn1_w_in': 'delta_w', 'delta_ffn1_w_out': 'delta_w', 'delta_mix_norm': 'delta_w', 'delta_w_in': 'delta_w', 'delta_fox_q_norm': 'delta_w', 'delta_fox_k_norm': 'delta_w', 'delta_fox_f_bias': 'delta_w', 'delta_gdn_conv': 'delta_w', 'delta_gdn_a_log': 'delta_w', 'delta_gdn_dt_bias': 'delta_w', 'delta_gdn_out_norm': 'delta_w', 'delta_w_out': 'delta_w', 'delta_ffn2_norm': 'delta_w', 'delta_ffn2_w_in': 'delta_w', 'delta_ffn2_w_out': 'delta_w', 'new_m_ffn1_norm': 'new_m', 'new_m_ffn1_w_in': 'new_m', 'new_m_ffn1_w_out': 'new_m', 'new_m_mix_norm': 'new_m', 'new_m_w_in': 'new_m', 'new_m_fox_q_norm': 'new_m', 'new_m_fox_k_norm': 'new_m', 'new_m_fox_f_bias': 'new_m', 'new_m_gdn_conv': 'new_m', 'new_m_gdn_a_log': 'new_m', 'new_m_gdn_dt_bias': 'new_m', 'new_m_gdn_out_norm': 'new_m', 'new_m_w_out': 'new_m', 'new_m_ffn2_norm': 'new_m', 'new_m_ffn2_w_in': 'new_m', 'new_m_ffn2_w_out': 'new_m', 'new_v_ffn1_norm': 'new_v', 'new_v_ffn1_w_in': 'new_v', 'new_v_ffn1_w_out': 'new_v', 'new_v_mix_norm': 'new_v', 'new_v_w_in': 'new_v', 'new_v_fox_q_norm': 'new_v', 'new_v_fox_k_norm': 'new_v', 'new_v_fox_f_bias': 'new_v', 'new_v_gdn_conv': 'new_v', 'new_v_gdn_a_log': 'new_v', 'new_v_gdn_dt_bias': 'new_v', 'new_v_gdn_out_norm': 'new_v', 'new_v_w_out': 'new_v', 'new_v_ffn2_norm': 'new_v', 'new_v_ffn2_w_in': 'new_v', 'new_v_ffn2_w_out': 'new_v'}


def _forward(args):
    return _fwd_reference(*[args[k] for k in FWD_PARAMS])


def _output_shape():
    out = _jax.eval_shape(lambda: _forward(_fwd_setup_inputs(0)))
    return out.shape, out.dtype

N_MICROBATCH = 1
ADAM_LR = 0.001
ADAM_B1 = 0.9
ADAM_B2 = 0.999
ADAM_EPS = 1e-08
ADAM_WD = 0.01
ADAM_STEP = 10
PER_EXAMPLE_BATCH_AXIS = {'x': 0, 'loss_target': 0}
SHARED_INPUTS = []
_WEIGHT_DTYPES = {'ffn1_norm': _jnp.float32, 'ffn1_w_in': _jnp.float32, 'ffn1_w_out': _jnp.float32, 'mix_norm': _jnp.float32, 'w_in': _jnp.float32, 'fox_q_norm': _jnp.float32, 'fox_k_norm': _jnp.float32, 'fox_f_bias': _jnp.float32, 'gdn_conv': _jnp.float32, 'gdn_a_log': _jnp.float32, 'gdn_dt_bias': _jnp.float32, 'gdn_out_norm': _jnp.float32, 'w_out': _jnp.float32, 'ffn2_norm': _jnp.float32, 'ffn2_w_in': _jnp.float32, 'ffn2_w_out': _jnp.float32}
MOMENT_SCALE = {'ffn1_norm': 1.218801e+01, 'ffn1_w_in': 1.911381e-01, 'ffn1_w_out': 3.215264e-01, 'mix_norm': 1.695861e+01, 'w_in': 5.288097e-01, 'fox_q_norm': 2.248354e+01, 'fox_k_norm': 2.295467e+01, 'fox_f_bias': 1.193831e+02, 'gdn_conv': 1.326310e+00, 'gdn_a_log': 1.205565e+02, 'gdn_dt_bias': 1.136227e+02, 'gdn_out_norm': 9.138318e+01, 'w_out': 8.869221e-01, 'ffn2_norm': 1.232089e+01, 'ffn2_w_in': 1.407415e-01, 'ffn2_w_out': 2.374246e-01}


def _to_microbatches(a, axis):
    t = _jnp.moveaxis(a, axis, 0)
    t = t.reshape((N_MICROBATCH, t.shape[0] // N_MICROBATCH) + t.shape[1:])
    return _jnp.moveaxis(t, 1, axis + 1)


def setup_inputs(seed: int = 0) -> dict:
    inp = _fwd_setup_inputs(seed)
    key = _jax.random.fold_in(_jax.random.key(seed), 7919)
    shape, _ = _output_shape()
    out = dict(inp)
    out["loss_target"] = _jax.random.normal(_jax.random.fold_in(key, 0), shape, _jnp.float32)
    for i, name in enumerate(TWIN_WEIGHTS):
        w = inp[name].astype(_jnp.float32)
        if MOMENT_SCALE is None:
            s = _jnp.sqrt(_jnp.mean(_jnp.square(w)) + 1e-30)
        else:
            s = MOMENT_SCALE[name]
        km, kv = _jax.random.split(_jax.random.fold_in(key, i + 1))
        out[name] = w
        out["m_" + name] = s * _jax.random.normal(km, w.shape, _jnp.float32)
        out["v_" + name] = (s * s) * _jax.random.uniform(kv, w.shape, _jnp.float32, 0.5, 1.5)
    if N_MICROBATCH > 1:
        for name, axis in PER_EXAMPLE_BATCH_AXIS.items():
            out[name] = _to_microbatches(out[name], axis)
    return {'x': out['x'], 'ffn1_norm': out['ffn1_norm'], 'ffn1_w_in': out['ffn1_w_in'], 'ffn1_w_out': out['ffn1_w_out'], 'mix_norm': out['mix_norm'], 'w_in': out['w_in'], 'fox_q_norm': out['fox_q_norm'], 'fox_k_norm': out['fox_k_norm'], 'fox_f_bias': out['fox_f_bias'], 'gdn_conv': out['gdn_conv'], 'gdn_a_log': out['gdn_a_log'], 'gdn_dt_bias': out['gdn_dt_bias'], 'gdn_out_norm': out['gdn_out_norm'], 'w_out': out['w_out'], 'ffn2_norm': out['ffn2_norm'], 'ffn2_w_in': out['ffn2_w_in'], 'ffn2_w_out': out['ffn2_w_out'], 'loss_target': out['loss_target'], 'm_ffn1_norm': out['m_ffn1_norm'], 'm_ffn1_w_in': out['m_ffn1_w_in'], 'm_ffn1_w_out': out['m_ffn1_w_out'], 'm_mix_norm': out['m_mix_norm'], 'm_w_in': out['m_w_in'], 'm_fox_q_norm': out['m_fox_q_norm'], 'm_fox_k_norm': out['m_fox_k_norm'], 'm_fox_f_bias': out['m_fox_f_bias'], 'm_gdn_conv': out['m_gdn_conv'], 'm_gdn_a_log': out['m_gdn_a_log'], 'm_gdn_dt_bias': out['m_gdn_dt_bias'], 'm_gdn_out_norm': out['m_gdn_out_norm'], 'm_w_out': out['m_w_out'], 'm_ffn2_norm': out['m_ffn2_norm'], 'm_ffn2_w_in': out['m_ffn2_w_in'], 'm_ffn2_w_out': out['m_ffn2_w_out'], 'v_ffn1_norm': out['v_ffn1_norm'], 'v_ffn1_w_in': out['v_ffn1_w_in'], 'v_ffn1_w_out': out['v_ffn1_w_out'], 'v_mix_norm': out['v_mix_norm'], 'v_w_in': out['v_w_in'], 'v_fox_q_norm': out['v_fox_q_norm'], 'v_fox_k_norm': out['v_fox_k_norm'], 'v_fox_f_bias': out['v_fox_f_bias'], 'v_gdn_conv': out['v_gdn_conv'], 'v_gdn_a_log': out['v_gdn_a_log'], 'v_gdn_dt_bias': out['v_gdn_dt_bias'], 'v_gdn_out_norm': out['v_gdn_out_norm'], 'v_w_out': out['v_w_out'], 'v_ffn2_norm': out['v_ffn2_norm'], 'v_ffn2_w_in': out['v_ffn2_w_in'], 'v_ffn2_w_out': out['v_ffn2_w_out']}


def _loss(weights, diff, rest, loss_target):
    with _jax.named_scope("forward"):
        args = {**rest, TWIN_DIFF_INPUT: diff, **{k: w.astype(_WEIGHT_DTYPES[k]) for k, w in weights.items()}}
        y = _forward(args)
    with _jax.named_scope("loss_head"):
        err = _jnp.square(y.astype(_jnp.float32) - loss_target)
        return 0.5 * _jnp.sum(_jnp.mean(err, axis=-1)) if err.ndim else 0.5 * err


def _adamw(w, g, m, v):
    m = ADAM_B1 * m + (1.0 - ADAM_B1) * g
    v = ADAM_B2 * v + (1.0 - ADAM_B2) * _jnp.square(g)
    m_hat = m / (1.0 - ADAM_B1 ** ADAM_STEP)
    v_hat = v / (1.0 - ADAM_B2 ** ADAM_STEP)
    delta = -ADAM_LR * (m_hat / (_jnp.sqrt(v_hat) + ADAM_EPS) + ADAM_WD * w)
    return delta, m, v


def reference(x, ffn1_norm, ffn1_w_in, ffn1_w_out, mix_norm, w_in, fox_q_norm, fox_k_norm, fox_f_bias, gdn_conv, gdn_a_log, gdn_dt_bias, gdn_out_norm, w_out, ffn2_norm, ffn2_w_in, ffn2_w_out, loss_target, m_ffn1_norm, m_ffn1_w_in, m_ffn1_w_out, m_mix_norm, m_w_in, m_fox_q_norm, m_fox_k_norm, m_fox_f_bias, m_gdn_conv, m_gdn_a_log, m_gdn_dt_bias, m_gdn_out_norm, m_w_out, m_ffn2_norm, m_ffn2_w_in, m_ffn2_w_out, v_ffn1_norm, v_ffn1_w_in, v_ffn1_w_out, v_mix_norm, v_w_in, v_fox_q_norm, v_fox_k_norm, v_fox_f_bias, v_gdn_conv, v_gdn_a_log, v_gdn_dt_bias, v_gdn_out_norm, v_w_out, v_ffn2_norm, v_ffn2_w_in, v_ffn2_w_out):
    given = dict(x=x, ffn1_norm=ffn1_norm, ffn1_w_in=ffn1_w_in, ffn1_w_out=ffn1_w_out, mix_norm=mix_norm, w_in=w_in, fox_q_norm=fox_q_norm, fox_k_norm=fox_k_norm, fox_f_bias=fox_f_bias, gdn_conv=gdn_conv, gdn_a_log=gdn_a_log, gdn_dt_bias=gdn_dt_bias, gdn_out_norm=gdn_out_norm, w_out=w_out, ffn2_norm=ffn2_norm, ffn2_w_in=ffn2_w_in, ffn2_w_out=ffn2_w_out, loss_target=loss_target, m_ffn1_norm=m_ffn1_norm, m_ffn1_w_in=m_ffn1_w_in, m_ffn1_w_out=m_ffn1_w_out, m_mix_norm=m_mix_norm, m_w_in=m_w_in, m_fox_q_norm=m_fox_q_norm, m_fox_k_norm=m_fox_k_norm, m_fox_f_bias=m_fox_f_bias, m_gdn_conv=m_gdn_conv, m_gdn_a_log=m_gdn_a_log, m_gdn_dt_bias=m_gdn_dt_bias, m_gdn_out_norm=m_gdn_out_norm, m_w_out=m_w_out, m_ffn2_norm=m_ffn2_norm, m_ffn2_w_in=m_ffn2_w_in, m_ffn2_w_out=m_ffn2_w_out, v_ffn1_norm=v_ffn1_norm, v_ffn1_w_in=v_ffn1_w_in, v_ffn1_w_out=v_ffn1_w_out, v_mix_norm=v_mix_norm, v_w_in=v_w_in, v_fox_q_norm=v_fox_q_norm, v_fox_k_norm=v_fox_k_norm, v_fox_f_bias=v_fox_f_bias, v_gdn_conv=v_gdn_conv, v_gdn_a_log=v_gdn_a_log, v_gdn_dt_bias=v_gdn_dt_bias, v_gdn_out_norm=v_gdn_out_norm, v_w_out=v_w_out, v_ffn2_norm=v_ffn2_norm, v_ffn2_w_in=v_ffn2_w_in, v_ffn2_w_out=v_ffn2_w_out)
    weights = {n: given[n] for n in TWIN_WEIGHTS}
    shared = {n: given[n] for n in SHARED_INPUTS}
    per_example = {n: given[n] for n in ['x']}
    grad_fn = _jax.value_and_grad(_loss, argnums=(0, 1))

    def one_microbatch(ex, loss_target):
        ex = dict(ex)
        diff = ex.pop(TWIN_DIFF_INPUT)
        return grad_fn(weights, diff, {**shared, **ex}, loss_target)

    if N_MICROBATCH == 1:
        loss, (grad_w, grad_x) = one_microbatch(per_example, given["loss_target"])
    else:
        def body(carry, xs):
            loss_sum, grad_sum = carry
            l_k, (gw_k, gx_k) = one_microbatch(xs[0], xs[1])
            with _jax.named_scope("update"):
                return (loss_sum + l_k, _jax.tree.map(_jnp.add, grad_sum, gw_k)), gx_k

        init = (_jnp.zeros((), _jnp.float32), _jax.tree.map(_jnp.zeros_like, weights))
        (loss, grad_w), grad_x = _jax.lax.scan(body, init, (per_example, given["loss_target"]))
    with _jax.named_scope("update"):
        delta_w, new_m, new_v = {}, {}, {}
        for n in TWIN_WEIGHTS:
            delta_w[n], new_m[n], new_v[n] = _adamw(weights[n], grad_w[n], given["m_" + n], given["v_" + n])
    return (loss, grad_x, *[grad_w[n] for n in TWIN_WEIGHTS], *[delta_w[n] for n in TWIN_WEIGHTS],
            *[new_m[n] for n in TWIN_WEIGHTS], *[new_v[n] for n in TWIN_WEIGHTS])
```

```python
import functools

import jax
import jax.numpy as jnp
from jax import lax
from jax.experimental import pallas as pl
from jax.experimental.pallas import tpu as pltpu

F32 = jnp.float32
BF16 = jnp.bfloat16
EPS = 1e-6
N_DEV = 8
MESH = pl.DeviceIdType.MESH
HIGHEST = lax.Precision.HIGHEST
VMEM_LIMIT = 56 * 1024 * 1024

FOX_HEADS, FOX_DH = 8, 64
GDN_HEADS, GDN_DH = 4, 128
CHUNK = 64
CONV_W = 4

ADAM_LR, ADAM_B1, ADAM_B2, ADAM_EPS, ADAM_WD, ADAM_STEP = 0.001, 0.9, 0.999, 1e-08, 0.01, 10


def _cp(*sem):
    return pltpu.CompilerParams(dimension_semantics=sem, vmem_limit_bytes=VMEM_LIMIT)


def _dot(a, b):
    return jnp.dot(a, b, preferred_element_type=F32)


def _dot_nt(a, b):
    return lax.dot_general(a, b, (((1,), (1,)), ((), ())), preferred_element_type=F32)


def _dot_tn(a, b):
    return lax.dot_general(a, b, (((0,), (0,)), ((), ())), preferred_element_type=F32)


def _rstd(xf):
    return lax.rsqrt(jnp.mean(xf * xf, axis=-1, keepdims=True) + EPS)


def _rms_bwd(xf, r, dyn):
    return r * dyn - xf * (r * r * r) * jnp.mean(dyn * xf, axis=-1, keepdims=True)


def ffn_fwd(x, nw, w_in, w_out, tm=512):
    t, d = x.shape
    nj, fb = w_out.shape[0], w_out.shape[1]

    def body(x_ref, nw_ref, wi_ref, wo_ref, o_ref, xn_ref, acc_ref):
        j = pl.program_id(1)

        @pl.when(j == 0)
        def _():
            xf = x_ref[...]
            xn_ref[...] = (xf * _rstd(xf) * nw_ref[...]).astype(BF16)
            acc_ref[...] = jnp.zeros_like(acc_ref)

        xn = xn_ref[...]
        g = _dot(xn, wi_ref[0])
        u = _dot(xn, wi_ref[1])
        h = (g * jax.nn.sigmoid(g) * u).astype(BF16)
        acc_ref[...] += _dot(h, wo_ref[...])

        @pl.when(j == nj - 1)
        def _():
            o_ref[...] = x_ref[...] + 0.5 * acc_ref[...]

    return pl.pallas_call(
        body, grid=(t // tm, nj),
        in_specs=[pl.BlockSpec((tm, d), lambda i, j: (i, 0)),
                  pl.BlockSpec((1, d), lambda i, j: (0, 0)),
                  pl.BlockSpec((2, None, d, fb), lambda i, j: (0, j, 0, 0)),
                  pl.BlockSpec((None, fb, d), lambda i, j: (j, 0, 0))],
        out_specs=pl.BlockSpec((tm, d), lambda i, j: (i, 0)),
        out_shape=jax.ShapeDtypeStruct((t, d), F32),
        scratch_shapes=[pltpu.VMEM((tm, d), BF16), pltpu.VMEM((tm, d), F32)],
        compiler_params=_cp("parallel", "arbitrary"), name="ffn_fwd")(x, nw, w_in, w_out)


def ffn_bwd(x, dy, nw, w_in, w_out, tm=512):
    t, d = x.shape
    nj, fb = w_out.shape[0], w_out.shape[1]

    def body(x_ref, dy_ref, nw_ref, wi_ref, wo_ref,
             dx_ref, dnw_ref, dgu_ref, h_ref, xn_ref, dyh_ref, acc_ref):
        i, j = pl.program_id(0), pl.program_id(1)

        @pl.when(j == 0)
        def _():
            xf = x_ref[...]
            xn_ref[...] = (xf * _rstd(xf) * nw_ref[...]).astype(BF16)
            dyh_ref[...] = (0.5 * dy_ref[...]).astype(BF16)
            acc_ref[...] = jnp.zeros_like(acc_ref)

        @pl.when((i == 0) & (j == 0))
        def _():
            dnw_ref[...] = jnp.zeros_like(dnw_ref)

        xn = xn_ref[...]
        g = _dot(xn, wi_ref[0])
        u = _dot(xn, wi_ref[1])
        sg = jax.nn.sigmoid(g)
        silu = g * sg
        dh = _dot_nt(dyh_ref[...], wo_ref[...])
        dg = (dh * u * (sg * (1.0 + g * (1.0 - sg)))).astype(BF16)
        du = (dh * silu).astype(BF16)
        dgu_ref[0] = dg
        dgu_ref[1] = du
        h_ref[...] = (silu * u).astype(BF16)
        acc_ref[...] += _dot_nt(dg, wi_ref[0]) + _dot_nt(du, wi_ref[1])

        @pl.when(j == nj - 1)
        def _():
            xf = x_ref[...]
            r = _rstd(xf)
            dxn = acc_ref[...]
            dnw_ref[...] += jnp.sum(dxn * xf * r, axis=0, keepdims=True)
            dx_ref[...] = _rms_bwd(xf, r, dxn * nw_ref[...]) + dy_ref[...]

    return pl.pallas_call(
        body, grid=(t // tm, nj),
        in_specs=[pl.BlockSpec((tm, d), lambda i, j: (i, 0)),
                  pl.BlockSpec((tm, d), lambda i, j: (i, 0)),
                  pl.BlockSpec((1, d), lambda i, j: (0, 0)),
                  pl.BlockSpec((2, None, d, fb), lambda i, j: (0, j, 0, 0)),
                  pl.BlockSpec((None, fb, d), lambda i, j: (j, 0, 0))],
        out_specs=[pl.BlockSpec((tm, d), lambda i, j: (i, 0)),
                   pl.BlockSpec((1, d), lambda i, j: (0, 0)),
                   pl.BlockSpec((2, None, tm, fb), lambda i, j: (0, j, i, 0)),
                   pl.BlockSpec((None, tm, fb), lambda i, j: (j, i, 0)),
                   pl.BlockSpec((tm, d), lambda i, j: (i, 0)),
                   pl.BlockSpec((tm, d), lambda i, j: (i, 0))],
        out_shape=[jax.ShapeDtypeStruct((t, d), F32),
                   jax.ShapeDtypeStruct((1, d), F32),
                   jax.ShapeDtypeStruct((2, nj, t, fb), BF16),
                   jax.ShapeDtypeStruct((nj, t, fb), BF16),
                   jax.ShapeDtypeStruct((t, d), BF16),
                   jax.ShapeDtypeStruct((t, d), BF16)],
        scratch_shapes=[pltpu.VMEM((tm, d), F32)],
        compiler_params=_cp("arbitrary", "arbitrary"), name="ffn_bwd")(x, dy, nw, w_in, w_out)


def _wgrad_call(a, b, a_spec, b_spec, out_shape, out_spec, grid, name):
    last = len(grid) - 1

    def body(a_ref, b_ref, o_ref):
        @pl.when(pl.program_id(last) == 0)
        def _():
            o_ref[...] = jnp.zeros_like(o_ref)

        o_ref[...] += _dot_tn(a_ref[...], b_ref[...])

    sem = ("parallel",) * last + ("arbitrary",)
    return pl.pallas_call(body, grid=grid, in_specs=[a_spec, b_spec], out_specs=out_spec,
                          out_shape=jax.ShapeDtypeStruct(out_shape, F32),
                          compiler_params=_cp(*sem), name=name)(a, b)


def wgrad_ffn_in(xn, dgu, tm=512):
    t, d = xn.shape
    _, nj, _, fb = dgu.shape
    return _wgrad_call(xn, dgu,
                       pl.BlockSpec((tm, d), lambda p, j, k: (k, 0)),
                       pl.BlockSpec((None, None, tm, fb), lambda p, j, k: (p, j, k, 0)),
                       (2, nj, d, fb), pl.BlockSpec((None, None, d, fb), lambda p, j, k: (p, j, 0, 0)),
                       (2, nj, t // tm), "wgrad_ffn_in")


def wgrad_ffn_out(h, dyh, tm=512):
    nj, t, fb = h.shape
    d = dyh.shape[1]
    return _wgrad_call(h, dyh,
                       pl.BlockSpec((None, tm, fb), lambda j, k: (j, k, 0)),
                       pl.BlockSpec((tm, d), lambda j, k: (k, 0)),
                       (nj, fb, d), pl.BlockSpec((None, fb, d), lambda j, k: (j, 0, 0)),
                       (nj, t // tm), "wgrad_ffn_out")


def wgrad_2d(a, b, tk, name, tm=512):
    t, k = a.shape
    n = b.shape[1]
    return _wgrad_call(a, b,
                       pl.BlockSpec((tm, tk), lambda c, s: (s, c)),
                       pl.BlockSpec((tm, n), lambda c, s: (s, 0)),
                       (k, n), pl.BlockSpec((tk, n), lambda c, s: (c, 0)),
                       (k // tk, t // tm), name)


N_BIG = 7 * 512
N_PROJ = N_BIG + 128
COL_SMALL = N_BIG // 128


def inproj_fwd(x, nw, w, tm=256):
    t, d = x.shape
    n = w.shape[1]

    def body(x_ref, nw_ref, w_ref, p_ref, hn_ref):
        xf = x_ref[...]
        hn = (xf * _rstd(xf) * nw_ref[...]).astype(BF16)
        hn_ref[...] = hn
        p_ref[...] = _dot(hn, w_ref[...])

    return pl.pallas_call(
        body, grid=(t // tm,),
        in_specs=[pl.BlockSpec((tm, d), lambda i: (i, 0)), pl.BlockSpec((1, d), lambda i: (0, 0)),
                  pl.BlockSpec((d, n), lambda i: (0, 0))],
        out_specs=[pl.BlockSpec((tm, n), lambda i: (i, 0)), pl.BlockSpec((tm, d), lambda i: (i, 0))],
        out_shape=[jax.ShapeDtypeStruct((t, n), F32), jax.ShapeDtypeStruct((t, d), BF16)],
        compiler_params=_cp("parallel"), name="inproj_fwd")(x, nw, w)


def inproj_bwd(x, dres, nw, w, dparts, tm=256):
    t, d = x.shape
    n = w.shape[1]
    widths = [p.shape[1] for p in dparts]
    assert sum(widths) == n

    def body(x_ref, dres_ref, nw_ref, w_ref, *rest):
        part_refs, (dx_ref, dnw_ref, dp_ref) = rest[:len(widths)], rest[len(widths):]

        @pl.when(pl.program_id(0) == 0)
        def _():
            dnw_ref[...] = jnp.zeros_like(dnw_ref)

        dp = jnp.concatenate([r[...].astype(BF16) for r in part_refs], axis=1)
        dp_ref[...] = dp
        dhn = _dot_nt(dp, w_ref[...])
        xf = x_ref[...]
        r = _rstd(xf)
        dnw_ref[...] += jnp.sum(dhn * xf * r, axis=0, keepdims=True)
        dx_ref[...] = _rms_bwd(xf, r, dhn * nw_ref[...]) + dres_ref[...]

    return pl.pallas_call(
        body, grid=(t // tm,),
        in_specs=[pl.BlockSpec((tm, d), lambda i: (i, 0)), pl.BlockSpec((tm, d), lambda i: (i, 0)),
                  pl.BlockSpec((1, d), lambda i: (0, 0)), pl.BlockSpec((d, n), lambda i: (0, 0))]
                 + [pl.BlockSpec((tm, wd), lambda i: (i, 0)) for wd in widths],
        out_specs=[pl.BlockSpec((tm, d), lambda i: (i, 0)), pl.BlockSpec((1, d), lambda i: (0, 0)),
                   pl.BlockSpec((tm, n), lambda i: (i, 0))],
        out_shape=[jax.ShapeDtypeStruct((t, d), F32), jax.ShapeDtypeStruct((1, d), F32),
                   jax.ShapeDtypeStruct((t, n), BF16)],
        compiler_params=_cp("arbitrary"), name="inproj_bwd")(x, dres, nw, w, *dparts)


def outproj_fwd(x, yf, yg, w, tm=512):
    t, d = x.shape
    hw = yf.shape[1]

    def body(x_ref, yf_ref, yg_ref, w_ref, o_ref, y_ref):
        y = jnp.concatenate([yf_ref[...], yg_ref[...]], axis=1).astype(BF16)
        y_ref[...] = y
        o_ref[...] = x_ref[...] + _dot(y, w_ref[...])

    return pl.pallas_call(
        body, grid=(t // tm,),
        in_specs=[pl.BlockSpec((tm, d), lambda i: (i, 0)), pl.BlockSpec((tm, hw), lambda i: (i, 0)),
                  pl.BlockSpec((tm, hw), lambda i: (i, 0)), pl.BlockSpec((2 * hw, d), lambda i: (0, 0))],
        out_specs=[pl.BlockSpec((tm, d), lambda i: (i, 0)), pl.BlockSpec((tm, 2 * hw), lambda i: (i, 0))],
        out_shape=[jax.ShapeDtypeStruct((t, d), F32), jax.ShapeDtypeStruct((t, 2 * hw), BF16)],
        compiler_params=_cp("parallel"), name="outproj_fwd")(x, yf, yg, w)


def outproj_bwd(dy, w, tm=512):
    t, d = dy.shape
    hw = w.shape[0] // 2

    def body(dy_ref, w_ref, df_ref, dg_ref, dyb_ref):
        dyb = dy_ref[...].astype(BF16)
        dyb_ref[...] = dyb
        dyy = _dot_nt(dyb, w_ref[...])
        df_ref[...] = dyy[:, :hw]
        dg_ref[...] = dyy[:, hw:]

    return pl.pallas_call(
        body, grid=(t // tm,),
        in_specs=[pl.BlockSpec((tm, d), lambda i: (i, 0)), pl.BlockSpec((2 * hw, d), lambda i: (0, 0))],
        out_specs=[pl.BlockSpec((tm, hw), lambda i: (i, 0)), pl.BlockSpec((tm, hw), lambda i: (i, 0)),
                   pl.BlockSpec((tm, d), lambda i: (i, 0))],
        out_shape=[jax.ShapeDtypeStruct((t, hw), F32), jax.ShapeDtypeStruct((t, hw), F32),
                   jax.ShapeDtypeStruct((t, d), BF16)],
        compiler_params=_cp("parallel"), name="outproj_bwd")(dy, w)


def _lane(shape):
    return lax.broadcasted_iota(jnp.int32, shape, 1)


def _row(shape):
    return lax.broadcasted_iota(jnp.int32, shape, 0)


def _gate_terms(val, gp_ref):
    z = val + gp_ref[0:1, :]
    sp = jnp.log(1.0 + jnp.exp(-jnp.abs(z)))
    return z, sp


def gates_fwd(proj, gp, seq, ts=512):
    t = proj.shape[0]
    nb, ns = t // seq, seq // ts

    def body(v_ref, gp_ref, o_ref, carry_ref):
        @pl.when(pl.program_id(1) == 0)
        def _():
            carry_ref[...] = jnp.zeros_like(carry_ref)

        z, sp = _gate_terms(v_ref[...], gp_ref)
        logsig = jnp.minimum(z, 0.0) - sp
        tri = (_row((ts, ts)) >= _lane((ts, ts))).astype(F32)
        cum = jnp.dot(tri, logsig, precision=HIGHEST, preferred_element_type=F32) + carry_ref[0:1, :]
        carry_ref[0:1, :] = cum[ts - 1:ts, :]
        g = -jnp.exp(gp_ref[1:2, :]) * (jnp.maximum(z, 0.0) + sp)
        beta = jax.nn.sigmoid(z)
        lane = _lane((ts, 128))
        o_ref[...] = jnp.where(lane < 8, cum, jnp.where(lane < 12, g, jnp.where(lane < 16, beta, 0.0)))

    return pl.pallas_call(
        body, grid=(nb, ns),
        in_specs=[pl.BlockSpec((ts, 128), lambda b, s: (b * ns + s, COL_SMALL)),
                  pl.BlockSpec((8, 128), lambda b, s: (0, 0))],
        out_specs=pl.BlockSpec((ts, 128), lambda b, s: (b * ns + s, 0)),
        out_shape=jax.ShapeDtypeStruct((t, 128), F32),
        scratch_shapes=[pltpu.VMEM((8, 128), F32)],
        compiler_params=_cp("parallel", "arbitrary"), name="gates_fwd")(proj, gp)


def gates_bwd(proj, gp, dga, dgb, seq, ts=512):
    t = proj.shape[0]
    nb, ns = t // seq, seq // ts

    def body(v_ref, gp_ref, da_ref, db_ref, ds_ref, dgp_ref, carry_ref):
        @pl.when(pl.program_id(1) == 0)
        def _():
            carry_ref[...] = jnp.zeros_like(carry_ref)

        @pl.when((pl.program_id(0) == 0) & (pl.program_id(1) == 0))
        def _():
            dgp_ref[...] = jnp.zeros_like(dgp_ref)

        lane = _lane((ts, 128))
        dgate = jnp.where(lane < 8, da_ref[...], jnp.where(lane < 16, db_ref[...], 0.0))
        z, sp = _gate_terms(v_ref[...], gp_ref)
        triu = (_row((ts, ts)) <= _lane((ts, ts))).astype(F32)
        dlog = jnp.dot(triu, dgate, precision=HIGHEST, preferred_element_type=F32) + carry_ref[0:1, :]
        carry_ref[0:1, :] = dlog[0:1, :]
        sig = jax.nn.sigmoid(z)
        nea = -jnp.exp(gp_ref[1:2, :])
        g = nea * (jnp.maximum(z, 0.0) + sp)
        dz = jnp.where(lane < 8, dlog * (1.0 - sig),
                       jnp.where(lane < 12, dgate * nea * sig, dgate * sig * (1.0 - sig)))
        dz = jnp.where(lane < 16, dz, 0.0)
        ds_ref[...] = dz
        dgp_ref[0:1, :] += jnp.where(lane[0:1] < 12, jnp.sum(dz, axis=0, keepdims=True), 0.0)
        dgp_ref[1:2, :] += jnp.where((lane[0:1] >= 8) & (lane[0:1] < 12), jnp.sum(dgate * g, axis=0, keepdims=True), 0.0)

    rev = lambda b, s: (b * ns + (ns - 1 - s), 0)
    return pl.pallas_call(
        body, grid=(nb, ns),
        in_specs=[pl.BlockSpec((ts, 128), lambda b, s: (b * ns + (ns - 1 - s), COL_SMALL)),
                  pl.BlockSpec((8, 128), lambda b, s: (0, 0)),
                  pl.BlockSpec((ts, 128), rev), pl.BlockSpec((ts, 128), rev)],
        out_specs=[pl.BlockSpec((ts, 128), rev), pl.BlockSpec((8, 128), lambda b, s: (0, 0))],
        out_shape=[jax.ShapeDtypeStruct((t, 128), F32), jax.ShapeDtypeStruct((8, 128), F32)],
        scratch_shapes=[pltpu.VMEM((8, 128), F32)],
        compiler_params=_cp("arbitrary", "arbitrary"), name="gates_bwd")(proj, gp, dga, dgb)


NEG = -1e30


def _pick_lane(tile, idx):
    return jnp.sum(jnp.where(_lane(tile.shape) == idx, tile, 0.0), axis=1, keepdims=True)


def _col_to_row(col, n):
    return jnp.sum(jnp.where(_row((n, n)) == _lane((n, n)), col, 0.0), axis=0, keepdims=True)


def _row_to_col(row, n):
    return jnp.sum(jnp.where(_row((n, n)) == _lane((n, n)), row, 0.0), axis=1, keepdims=True)


def _rows(i, n):
    return pl.ds(pl.multiple_of(i * n, n), n)


def _once(shape, index_map):
    return pl.BlockSpec(shape, index_map, pipeline_mode=pl.Buffered(1))


def attn_fwd(proj, gates, qw, kw, seq, tq=256):
    t = proj.shape[0]
    nb, nq, dh = t // seq, seq // tq, FOX_DH
    scale = dh ** -0.5

    def body(q_ref, k_ref, v_ref, g_ref, qw_ref, kw_ref, y_ref, lse_ref, qs, ks, vs, ccol, crow):
        p = pl.program_id(1)
        for hh in range(2):
            lanes = slice(hh * dh, (hh + 1) * dh)
            head = 2 * p + hh

            def prep(i, _):
                r = _rows(i, tq)
                qf, kf = q_ref[r, lanes], k_ref[r, lanes]
                qs[r, :] = (qf * _rstd(qf) * qw_ref[...] * scale).astype(BF16)
                ks[r, :] = (kf * _rstd(kf) * kw_ref[...]).astype(BF16)
                vs[r, :] = v_ref[r, lanes].astype(BF16)
                cc = _pick_lane(g_ref[r, :], head)
                ccol[r, :] = cc
                crow[i] = _col_to_row(cc, tq)
                return 0

            lax.fori_loop(0, nq, prep, 0)

            def q_tile(i, _):
                r = _rows(i, tq)
                qt, cc = qs[r, :], ccol[r, :]

                def kv_step(j, carry, masked):
                    m, l, acc = carry
                    kr = _rows(j, tq)
                    s = _dot_nt(qt, ks[kr, :]) + (cc - crow[j])
                    if masked:
                        s = jnp.where(_row((tq, tq)) >= _lane((tq, tq)), s, NEG)
                    m_new = jnp.maximum(m, jnp.max(s, axis=1, keepdims=True))
                    pe = jnp.exp(s - m_new)
                    a = jnp.exp(m - m_new)
                    return (m_new, a * l + jnp.sum(pe, axis=1, keepdims=True),
                            a * acc + _dot(pe.astype(BF16), vs[kr, :]))

                carry = (jnp.full((tq, 1), NEG, F32), jnp.zeros((tq, 1), F32), jnp.zeros((tq, dh), F32))
                carry = lax.fori_loop(0, i, lambda j, c: kv_step(j, c, False), carry)
                m, l, acc = kv_step(i, carry, True)
                y_ref[r, lanes] = acc / l
                lse_ref[r, lanes] = jnp.broadcast_to(m + jnp.log(l), (tq, dh))
                return 0

            lax.fori_loop(0, nq, q_tile, 0)

    blk = lambda off: _once((seq, 128), lambda b, p: (b, off + p))
    return pl.pallas_call(
        body, grid=(nb, 4),
        in_specs=[blk(0), blk(4), blk(8), _once((seq, 128), lambda b, p: (b, 0)),
                  pl.BlockSpec((1, dh), lambda b, p: (0, 0)), pl.BlockSpec((1, dh), lambda b, p: (0, 0))],
        out_specs=[pl.BlockSpec((seq, 128), lambda b, p: (b, p)), pl.BlockSpec((seq, 128), lambda b, p: (b, p))],
        out_shape=[jax.ShapeDtypeStruct((t, 512), F32), jax.ShapeDtypeStruct((t, 512), F32)],
        scratch_shapes=[pltpu.VMEM((seq, dh), BF16), pltpu.VMEM((seq, dh), BF16), pltpu.VMEM((seq, dh), BF16),
                        pltpu.VMEM((seq, 1), F32), pltpu.VMEM((nq, 1, tq), F32)],
        compiler_params=_cp("parallel", "arbitrary"), name="attn_fwd")(proj, proj, proj, gates, qw, kw)


def attn_bwd(proj, gates, qw, kw, y, lse, dy, seq, tq=256):
    t = proj.shape[0]
    nb, nq, dh = t // seq, seq // tq, FOX_DH
    scale = dh ** -0.5

    def body(q_ref, k_ref, v_ref, g_ref, qw_ref, kw_ref, y_ref, lse_ref, dy_ref,
             dq_ref, dk_ref, dv_ref, dg_ref, dqw_ref, dkw_ref,
             qs, ks, vs, dos, cols, crow, dqa, dka):
        b, p = pl.program_id(0), pl.program_id(1)

        @pl.when((b == 0) & (p == 0))
        def _():
            dqw_ref[...] = jnp.zeros_like(dqw_ref)
            dkw_ref[...] = jnp.zeros_like(dkw_ref)

        @pl.when(p == 0)
        def _():
            dg_ref[...] = jnp.zeros_like(dg_ref)

        for hh in range(2):
            lanes = slice(hh * dh, (hh + 1) * dh)
            head = 2 * p + hh

            def prep(i, _):
                r = _rows(i, tq)
                qf, kf = q_ref[r, lanes], k_ref[r, lanes]
                qs[r, :] = (qf * _rstd(qf) * qw_ref[...] * scale).astype(BF16)
                ks[r, :] = (kf * _rstd(kf) * kw_ref[...]).astype(BF16)
                vs[r, :] = v_ref[r, lanes].astype(BF16)
                dyf = dy_ref[r, lanes]
                dos[r, :] = dyf.astype(BF16)
                cc = _pick_lane(g_ref[r, :], head)
                crow[i] = _col_to_row(cc, tq)
                delta = jnp.sum(dyf * y_ref[r, lanes], axis=1, keepdims=True)
                lane = _lane((tq, 128))
                cols[r, :] = jnp.where(lane == 0, cc, jnp.where(lane == 1, lse_ref[r, hh * dh:hh * dh + 1],
                                                                 jnp.where(lane == 2, delta, 0.0)))
                dqa[r, :] = jnp.zeros((tq, dh), F32)
                return 0

            lax.fori_loop(0, nq, prep, 0)

            def kv_tile(j, _):
                kr = _rows(j, tq)
                kt, vt, cr = ks[kr, :], vs[kr, :], crow[j]

                def q_step(i, carry, masked):
                    dk, dv, dcr = carry
                    r = _rows(i, tq)
                    qt, dot, cl = qs[r, :], dos[r, :], cols[r, :]
                    s = _dot_nt(qt, kt) + (cl[:, 0:1] - cr)
                    if masked:
                        s = jnp.where(_row((tq, tq)) >= _lane((tq, tq)), s, NEG)
                    pe = jnp.exp(s - cl[:, 1:2])
                    ds = pe * (_dot_nt(dot, vt) - cl[:, 2:3])
                    dsb = ds.astype(BF16)
                    dqa[r, :] += _dot(dsb, kt)
                    cols[r, :] = cl + jnp.where(_lane((tq, 128)) == 3, jnp.sum(ds, axis=1, keepdims=True), 0.0)
                    return (dk + _dot_tn(dsb, qt), dv + _dot_tn(pe.astype(BF16), dot),
                            dcr - jnp.sum(ds, axis=0, keepdims=True))

                carry = (jnp.zeros((tq, dh), F32), jnp.zeros((tq, dh), F32), jnp.zeros((1, tq), F32))
                carry = q_step(j, carry, True)
                dk, dv, dcr = lax.fori_loop(j + 1, nq, lambda i, c: q_step(i, c, False), carry)
                dka[kr, :] = dk
                dv_ref[kr, lanes] = dv
                dg_ref[kr, :] = jnp.where(_lane((tq, 128)) == head, _row_to_col(dcr, tq), dg_ref[kr, :])
                return 0

            lax.fori_loop(0, nq, kv_tile, 0)

            def post(i, _):
                r = _rows(i, tq)
                qf, kf = q_ref[r, lanes], k_ref[r, lanes]
                rq, rk = _rstd(qf), _rstd(kf)
                dqn, dkn = dqa[r, :] * scale, dka[r, :]
                dqw_ref[...] += jnp.sum(dqn * qf * rq, axis=0, keepdims=True)
                dkw_ref[...] += jnp.sum(dkn * kf * rk, axis=0, keepdims=True)
                dq_ref[r, lanes] = _rms_bwd(qf, rq, dqn * qw_ref[...])
                dk_ref[r, lanes] = _rms_bwd(kf, rk, dkn * kw_ref[...])
                dg_ref[r, :] += jnp.where(_lane((tq, 128)) == head, cols[r, 3:4], 0.0)
                return 0

            lax.fori_loop(0, nq, post, 0)

    blk = lambda off: _once((seq, 128), lambda b, p: (b, off + p))
    own = lambda: _once((seq, 128), lambda b, p: (b, p))
    vec = lambda: pl.BlockSpec((1, dh), lambda b, p: (0, 0))
    return pl.pallas_call(
        body, grid=(nb, 4),
        in_specs=[blk(0), blk(4), blk(8), _once((seq, 128), lambda b, p: (b, 0)), vec(), vec(), own(), own(), own()],
        out_specs=[own(), own(), own(), _once((seq, 128), lambda b, p: (b, 0)), vec(), vec()],
        out_shape=[jax.ShapeDtypeStruct((t, 512), F32)] * 3
                  + [jax.ShapeDtypeStruct((t, 128), F32), jax.ShapeDtypeStruct((1, dh), F32), jax.ShapeDtypeStruct((1, dh), F32)],
        scratch_shapes=[pltpu.VMEM((seq, dh), BF16)] * 4
                       + [pltpu.VMEM((seq, 128), F32), pltpu.VMEM((nq, 1, tq), F32),
                          pltpu.VMEM((seq, dh), F32), pltpu.VMEM((seq, dh), F32)],
        compiler_params=_cp("arbitrary", "arbitrary"), name="attn_bwd")(proj, proj, proj, gates, qw, kw, y, lse, dy)


def _silu_grad(c, sg):
    return sg * (1.0 + c * (1.0 - sg))


def _conv(x, w, n):
    row = _row(x.shape)
    c = x * w[CONV_W - 1:CONV_W, :]
    for k in range(CONV_W - 1):
        sh = CONV_W - 1 - k
        c = c + w[k:k + 1, :] * jnp.where(row >= sh, pltpu.roll(x, sh, 0), 0.0)
    return c


def gdn_pre_fwd(proj, cw, seq):
    t = proj.shape[0]
    nb = t // seq
    scale = GDN_DH ** -0.5

    def body(xq_ref, xk_ref, xv_ref, wq_ref, wk_ref, wv_ref, q_ref, k_ref, v_ref):
        def act(x_ref, w_ref):
            c = _conv(x_ref[...], w_ref[...], seq)
            return c * jax.nn.sigmoid(c)

        aq, ak = act(xq_ref, wq_ref), act(xk_ref, wk_ref)
        q_ref[...] = aq * lax.rsqrt(jnp.sum(aq * aq, axis=1, keepdims=True) + EPS) * scale
        k_ref[...] = ak * lax.rsqrt(jnp.sum(ak * ak, axis=1, keepdims=True) + EPS)
        v_ref[...] = act(xv_ref, wv_ref)

    xb = lambda off: pl.BlockSpec((seq, 128), lambda b, h: (b, off + h))
    wb = lambda off: pl.BlockSpec((CONV_W, 128), lambda b, h: (0, off + h))
    ob = lambda: pl.BlockSpec((seq, 128), lambda b, h: (b, h))
    return pl.pallas_call(
        body, grid=(nb, GDN_HEADS),
        in_specs=[xb(12), xb(16), xb(20), wb(0), wb(4), wb(8)],
        out_specs=[ob(), ob(), ob()],
        out_shape=[jax.ShapeDtypeStruct((t, 512), F32)] * 3,
        compiler_params=_cp("parallel", "parallel"), name="gdn_pre_fwd")(proj, proj, proj, cw, cw, cw)


def gdn_pre_bwd(proj, cw, dq, dk, dv, seq):
    t = proj.shape[0]
    nb = t // seq
    scale = GDN_DH ** -0.5

    def body(xq_ref, xk_ref, xv_ref, wq_ref, wk_ref, wv_ref, dq_ref, dk_ref, dv_ref,
             dxq_ref, dxk_ref, dxv_ref, dwq_ref, dwk_ref, dwv_ref):
        first = pl.program_id(1) == 0
        row = _row((seq, 128))

        def one(x_ref, w_ref, dy_ref, dx_ref, dw_ref, norm, sc):
            x, w = x_ref[...], w_ref[...]
            c = _conv(x, w, seq)
            sg = jax.nn.sigmoid(c)
            dy = dy_ref[...]
            if norm:
                a = c * sg
                rs = lax.rsqrt(jnp.sum(a * a, axis=1, keepdims=True) + EPS)
                dy = dy * sc
                da = rs * dy - a * (rs * rs * rs) * jnp.sum(dy * a, axis=1, keepdims=True)
            else:
                da = dy
            dc = da * _silu_grad(c, sg)
            dx = dc * w[CONV_W - 1:CONV_W, :]
            dws = [None] * CONV_W
            dws[CONV_W - 1] = jnp.sum(dc * x, axis=0, keepdims=True)
            for k in range(CONV_W - 1):
                sh = CONV_W - 1 - k
                dx = dx + w[k:k + 1, :] * jnp.where(row < seq - sh, pltpu.roll(dc, seq - sh, 0), 0.0)
                dws[k] = jnp.sum(dc * jnp.where(row >= sh, pltpu.roll(x, sh, 0), 0.0), axis=0, keepdims=True)
            dx_ref[...] = dx
            dwn = jnp.concatenate(dws, axis=0)

            @pl.when(first)
            def _():
                dw_ref[...] = dwn

            @pl.when(jnp.logical_not(first))
            def _():
                dw_ref[...] += dwn

        one(xq_ref, wq_ref, dq_ref, dxq_ref, dwq_ref, True, scale)
        one(xk_ref, wk_ref, dk_ref, dxk_ref, dwk_ref, True, 1.0)
        one(xv_ref, wv_ref, dv_ref, dxv_ref, dwv_ref, False, 1.0)

    xb = lambda off: pl.BlockSpec((seq, 128), lambda h, b: (b, off + h))
    wb = lambda off: pl.BlockSpec((CONV_W, 128), lambda h, b: (0, off + h))
    ob = lambda: pl.BlockSpec((seq, 128), lambda h, b: (b, h))
    return pl.pallas_call(
        body, grid=(GDN_HEADS, nb),
        in_specs=[xb(12), xb(16), xb(20), wb(0), wb(4), wb(8), ob(), ob(), ob()],
        out_specs=[ob(), ob(), ob()] + [pl.BlockSpec((CONV_W, 128), lambda h, b: (0, h))] * 3,
        out_shape=[jax.ShapeDtypeStruct((t, 512), F32)] * 3 + [jax.ShapeDtypeStruct((CONV_W, 512), F32)] * 3,
        compiler_params=_cp("parallel", "arbitrary"), name="gdn_pre_bwd")(proj, proj, proj, cw, cw, cw, dq, dk, dv)


def _b16(x):
    return x.astype(BF16)


@jax.custom_vjp
def _mm(a, b):
    return _dot(_b16(a), _b16(b))


_mm.defvjp(lambda a, b: (_mm(a, b), (a, b)),
           lambda res, g: (_dot_nt(_b16(g), _b16(res[1])), _dot_tn(_b16(res[0]), _b16(g))))


@jax.custom_vjp
def _mm_nt(a, b):
    return _dot_nt(_b16(a), _b16(b))


_mm_nt.defvjp(lambda a, b: (_mm_nt(a, b), (a, b)),
              lambda res, g: (_dot(_b16(g), _b16(res[1])), _dot_tn(_b16(g), _b16(res[0]))))


@jax.custom_vjp
def _mm_tn(a, b):
    return _dot_tn(_b16(a), _b16(b))


_mm_tn.defvjp(lambda a, b: (_mm_tn(a, b), (a, b)),
              lambda res, g: (_dot_nt(_b16(res[1]), _b16(g)), _dot(_b16(res[0]), _b16(g))))


def _dot32(a, b, dims=(((1,), (0,)), ((), ()))):
    return lax.dot_general(a, b, dims, precision=HIGHEST, preferred_element_type=F32)


def _inv_fwd(a):
    n = a.shape[0]
    eye = (_row((n, n)) == _lane((n, n))).astype(F32)
    inv, pw = eye - a, a
    for _ in range(n.bit_length() - 2):
        pw = _dot32(pw, pw)
        inv = inv + _dot32(inv, pw)
    return inv


@jax.custom_vjp
def _inv_unit_lower(a):
    return _inv_fwd(a)


def _inv_bwd(inv, g):
    tg = _dot32(inv, g, (((0,), (0,)), ((), ())))
    return (-_dot32(tg, inv, (((1,), (1,)), ((), ()))),)


_inv_unit_lower.defvjp(lambda a: (lambda t: (t, t))(_inv_fwd(a)), _inv_bwd)


def _gdn_chunk(q, k, v, gcol, bcol, state, gg, nw):
    c = CHUNK
    ri, ci = _row((c, c)), _lane((c, c))
    incl, strict, eye = ri >= ci, ri > ci, ri == ci
    grow = jnp.sum(jnp.where(eye, gcol, 0.0), axis=0, keepdims=True)
    gc = jnp.sum(jnp.where(incl, grow, 0.0), axis=1, keepdims=True)
    gcr = jnp.sum(jnp.where(eye, gc, 0.0), axis=0, keepdims=True)
    gl = jnp.sum(jnp.where(_row((c, 1)) == c - 1, gc, 0.0), axis=0, keepdims=True)
    decay = jnp.exp(jnp.where(incl, gc - gcr, NEG))
    kb, vb = k * bcol, v * bcol
    a = jnp.where(strict, _mm_nt(kb, k) * decay, 0.0)
    inv = _inv_unit_lower(a)
    eg = jnp.exp(gc)
    u = _mm(inv, vb)
    w = _mm(inv, kb * eg)
    pm = jnp.where(incl, _mm_nt(q, k) * decay, 0.0)
    kd = k * jnp.exp(gl - gc)
    qd = q * eg
    v_new = u - _mm(w, state)
    o = _mm(qd, state) + _mm(pm, v_new)
    state_new = state * jnp.exp(gl) + _mm_tn(kd, v_new)
    y = o * _rstd(o) * nw * (gg * jax.nn.sigmoid(gg))
    return y, state_new


def gdn_fwd(q, k, v, gates, proj, nw, seq):
    t = q.shape[0]
    nb, nc = t // seq, seq // CHUNK

    def body(q_ref, k_ref, v_ref, g_ref, gg_ref, nw_ref, y_ref, st_ref):
        h = pl.program_id(1)

        def step(c, state):
            r = _rows(c, CHUNK)
            gt = g_ref[r, :]
            st_ref[c] = state
            y, state = _gdn_chunk(q_ref[r, :], k_ref[r, :], v_ref[r, :], _pick_lane(gt, 8 + h), _pick_lane(gt, 12 + h),
                                  state, gg_ref[r, :], nw_ref[...])
            y_ref[r, :] = y
            return state

        lax.fori_loop(0, nc, step, jnp.zeros((GDN_DH, GDN_DH), F32))

    hb = lambda: _once((seq, 128), lambda b, h: (b, h))
    return pl.pallas_call(
        body, grid=(nb, GDN_HEADS),
        in_specs=[hb(), hb(), hb(), _once((seq, 128), lambda b, h: (b, 0)),
                  _once((seq, 128), lambda b, h: (b, 24 + h)), pl.BlockSpec((1, 128), lambda b, h: (0, 0))],
        out_specs=[pl.BlockSpec((seq, 128), lambda b, h: (b, h)),
                   pl.BlockSpec((None, None, nc, GDN_DH, GDN_DH), lambda b, h: (b, h, 0, 0, 0))],
        out_shape=[jax.ShapeDtypeStruct((t, 512), F32),
                   jax.ShapeDtypeStruct((nb, GDN_HEADS, nc, GDN_DH, GDN_DH), F32)],
        compiler_params=_cp("parallel", "arbitrary"), name="gdn_fwd")(q, k, v, gates, proj, nw)


def gdn_bwd(q, k, v, gates, proj, nw, states, dy, seq):
    t = q.shape[0]
    nb, nc = t // seq, seq // CHUNK

    def body(q_ref, k_ref, v_ref, g_ref, gg_ref, nw_ref, st_ref, dy_ref,
             dq_ref, dk_ref, dv_ref, dgg_ref, dg_ref, dnw_ref):
        b, h = pl.program_id(0), pl.program_id(1)

        @pl.when((b == 0) & (h == 0))
        def _():
            dnw_ref[...] = jnp.zeros_like(dnw_ref)

        @pl.when(h == 0)
        def _():
            dg_ref[...] = jnp.zeros_like(dg_ref)

        def step(idx, dstate):
            c = nc - 1 - idx
            r = _rows(c, CHUNK)
            gt = g_ref[r, :]
            _, vjp = jax.vjp(_gdn_chunk, q_ref[r, :], k_ref[r, :], v_ref[r, :], _pick_lane(gt, 8 + h),
                             _pick_lane(gt, 12 + h), st_ref[c], gg_ref[r, :], nw_ref[...])
            dq, dk, dv, dgc, dbc, dstate, dgg, dnw = vjp((dy_ref[r, :], dstate))
            dq_ref[r, :] = dq
            dk_ref[r, :] = dk
            dv_ref[r, :] = dv
            dgg_ref[r, :] = dgg
            dnw_ref[...] += dnw
            lane = _lane((CHUNK, 128))
            dg_ref[r, :] = jnp.where(lane == 8 + h, dgc, jnp.where(lane == 12 + h, dbc, dg_ref[r, :]))
            return dstate

        lax.fori_loop(0, nc, step, jnp.zeros((GDN_DH, GDN_DH), F32))

    hb = lambda: _once((seq, 128), lambda b, h: (b, h))
    ob = lambda: pl.BlockSpec((seq, 128), lambda b, h: (b, h))
    return pl.pallas_call(
        body, grid=(nb, GDN_HEADS),
        in_specs=[hb(), hb(), hb(), _once((seq, 128), lambda b, h: (b, 0)),
                  _once((seq, 128), lambda b, h: (b, 24 + h)), pl.BlockSpec((1, 128), lambda b, h: (0, 0)),
                  _once((None, None, nc, GDN_DH, GDN_DH), lambda b, h: (b, h, 0, 0, 0)), hb()],
        out_specs=[ob(), ob(), ob(), ob(), pl.BlockSpec((seq, 128), lambda b, h: (b, 0)),
                   pl.BlockSpec((1, 128), lambda b, h: (0, 0))],
        out_shape=[jax.ShapeDtypeStruct((t, 512), F32)] * 4
                  + [jax.ShapeDtypeStruct((t, 128), F32), jax.ShapeDtypeStruct((1, 128), F32)],
        compiler_params=_cp("arbitrary", "arbitrary"), name="gdn_bwd")(q, k, v, gates, proj, nw, states, dy)


def loss_head(y, target, tm=512):
    t, d = y.shape

    def body(y_ref, t_ref, s_ref, dy_ref):
        @pl.when(pl.program_id(0) == 0)
        def _():
            s_ref[...] = jnp.zeros_like(s_ref)

        err = y_ref[...] - t_ref[...]
        s_ref[...] += jnp.sum(err * err, axis=0, keepdims=True)
        dy_ref[...] = err * (1.0 / d)

    return pl.pallas_call(
        body, grid=(t // tm,),
        in_specs=[pl.BlockSpec((tm, d), lambda i: (i, 0)), pl.BlockSpec((tm, d), lambda i: (i, 0))],
        out_specs=[pl.BlockSpec((1, d), lambda i: (0, 0)), pl.BlockSpec((tm, d), lambda i: (i, 0))],
        out_shape=[jax.ShapeDtypeStruct((1, d), F32), jax.ShapeDtypeStruct((t, d), F32)],
        compiler_params=_cp("arbitrary"), name="loss_head")(y, target)


def _place():
    return lax.axis_index("x"), lax.axis_index("y"), lax.axis_index("c")


def _peer(k):
    x, y, c = _place()
    px = 1 - x if (k >> 2) & 1 else x
    py = 1 - y if (k >> 1) & 1 else y
    pc = 1 - c if k & 1 else c
    return (px, py, pc), 4 * px + 2 * py + pc


def exchange(arrays, scatter, name):
    n = len(arrays)

    def body(*refs):
        ins, outs = refs[:n], refs[n:2 * n]
        send_sems, recv_sems, loc_sems = refs[2 * n:]
        x, y, c = _place()
        me = 4 * x + 2 * y + c

        def src(a, idx):
            return ins[a].at[idx] if scatter[a] else ins[a]

        def remote(a, k):
            pid, pidx = _peer(k)
            return pltpu.make_async_remote_copy(src_ref=src(a, pidx), dst_ref=outs[a].at[me],
                                                send_sem=send_sems.at[a, k], recv_sem=recv_sems.at[a, k],
                                                device_id=pid, device_id_type=MESH)

        def arrival(a, k):
            pid, pidx = _peer(k)
            return pltpu.make_async_remote_copy(src_ref=src(a, pidx), dst_ref=outs[a].at[pidx],
                                                send_sem=send_sems.at[a, k], recv_sem=recv_sems.at[a, k],
                                                device_id=pid, device_id_type=MESH)

        local = [pltpu.make_async_copy(src(a, me), outs[a].at[me], loc_sems.at[a]) for a in range(n)]
        for k in range(1, N_DEV):
            for a in range(n):
                remote(a, k).start()
        for cp in local:
            cp.start()
        for k in range(1, N_DEV):
            for a in range(n):
                arrival(a, k).wait_recv()
        for k in range(1, N_DEV):
            for a in range(n):
                remote(a, k).wait_send()
        for cp in local:
            cp.wait()

    out_shape = [jax.ShapeDtypeStruct(a.shape if s else (N_DEV,) + a.shape, a.dtype) for a, s in zip(arrays, scatter)]
    anyspec = pl.BlockSpec(memory_space=pl.ANY)
    return pl.pallas_call(
        body, in_specs=[anyspec] * n, out_specs=[anyspec] * n, out_shape=out_shape,
        scratch_shapes=[pltpu.SemaphoreType.DMA((n, N_DEV)), pltpu.SemaphoreType.DMA((n, N_DEV)),
                        pltpu.SemaphoreType.DMA((n,))],
        name=name)(*arrays)


def adamw_reduce(slots, w, m, v, name):
    r, c = w.shape
    tr = r
    while tr * c * 4 > (1 << 20) and tr % 16 == 0:
        tr //= 2
    bc1 = 1.0 - ADAM_B1 ** ADAM_STEP
    bc2 = 1.0 - ADAM_B2 ** ADAM_STEP

    def body(s_ref, w_ref, m_ref, v_ref, g_ref, d_ref, nm_ref, nv_ref):
        g = s_ref[0]
        for j in range(1, N_DEV):
            g = g + s_ref[j]
        nm = ADAM_B1 * m_ref[...] + (1.0 - ADAM_B1) * g
        nv = ADAM_B2 * v_ref[...] + (1.0 - ADAM_B2) * (g * g)
        g_ref[...] = g
        nm_ref[...] = nm
        nv_ref[...] = nv
        d_ref[...] = -ADAM_LR * ((nm / bc1) / (jnp.sqrt(nv / bc2) + ADAM_EPS) + ADAM_WD * w_ref[...])

    blk = lambda: pl.BlockSpec((tr, c), lambda i: (i, 0))
    return pl.pallas_call(
        body, grid=(r // tr,),
        in_specs=[pl.BlockSpec((N_DEV, tr, c), lambda i: (0, i, 0)), blk(), blk(), blk()],
        out_specs=[blk(), blk(), blk(), blk()],
        out_shape=[jax.ShapeDtypeStruct((r, c), F32)] * 4,
        compiler_params=_cp("parallel"), name=name)(slots, w, m, v)


BIG = ("ffn1_w_in", "ffn1_w_out", "w_in", "gdn_conv", "w_out", "ffn2_w_in", "ffn2_w_out")
SMALL = ("ffn1_norm", "mix_norm", "fox_q_norm", "fox_k_norm", "fox_f_bias", "gdn_a_log", "gdn_dt_bias",
         "gdn_out_norm", "ffn2_norm")
WEIGHTS = ("ffn1_norm", "ffn1_w_in", "ffn1_w_out", "mix_norm", "w_in", "fox_q_norm", "fox_k_norm", "fox_f_bias",
           "gdn_conv", "gdn_a_log", "gdn_dt_bias", "gdn_out_norm", "w_out", "ffn2_norm", "ffn2_w_in", "ffn2_w_out")
IN_COLS = (("fq", 512), ("fk", 512), ("fv", 512), ("ff", 8), ("gq", 512), ("gk", 512), ("gv", 512),
           ("ga", 4), ("gb", 4), ("gg", 512))
MY_BIG = ("fq", "fk", "fv", "gq", "gk", "gv", "gg")
MY_SMALL = ("ff", "ga", "gb")
SMALL_ROWS = 8 * 128


def _in_cols_to_mine(w):
    off, parts = 0, {}
    for nm, wd in IN_COLS:
        parts[nm] = w[:, off:off + wd]
        off += wd
    small = jnp.concatenate([parts[nm] for nm in MY_SMALL], axis=1)
    small = jnp.pad(small, ((0, 0), (0, 128 - small.shape[1])))
    return jnp.concatenate([parts[nm] for nm in MY_BIG] + [small], axis=1)


def _in_cols_from_mine(g):
    parts = {nm: g[:, i * 512:(i + 1) * 512] for i, nm in enumerate(MY_BIG)}
    off = N_BIG
    for nm in MY_SMALL:
        wd = dict(IN_COLS)[nm]
        parts[nm] = g[:, off:off + wd]
        off += wd
    return jnp.concatenate([parts[nm] for nm, _ in IN_COLS], axis=1)


def _pack_small(vals):
    rows = []
    nl = vals[SMALL[0]].shape[0]
    for l in range(nl):
        for nm in SMALL:
            v = vals[nm][l].reshape(-1)
            pad = (-v.shape[0]) % SMALL_ROWS
            rows.append(jnp.pad(v, (0, pad)).reshape(-1, 128))
    return jnp.concatenate(rows, axis=0)


def _unpack_small(packed, like):
    out = {nm: [] for nm in SMALL}
    row = 0
    nl = like[SMALL[0]].shape[0]
    for l in range(nl):
        for nm in SMALL:
            n = like[nm].shape[1]
            nr = -(-n // SMALL_ROWS) * 8
            out[nm].append(packed[row:row + nr].reshape(-1)[:n])
            row += nr
    return {nm: jnp.stack(v) for nm, v in out.items()}


def kernel(x, ffn1_norm, ffn1_w_in, ffn1_w_out, mix_norm, w_in, fox_q_norm, fox_k_norm, fox_f_bias, gdn_conv, gdn_a_log, gdn_dt_bias, gdn_out_norm, w_out, ffn2_norm, ffn2_w_in, ffn2_w_out, loss_target, m_ffn1_norm, m_ffn1_w_in, m_ffn1_w_out, m_mix_norm, m_w_in, m_fox_q_norm, m_fox_k_norm, m_fox_f_bias, m_gdn_conv, m_gdn_a_log, m_gdn_dt_bias, m_gdn_out_norm, m_w_out, m_ffn2_norm, m_ffn2_w_in, m_ffn2_w_out, v_ffn1_norm, v_ffn1_w_in, v_ffn1_w_out, v_mix_norm, v_w_in, v_fox_q_norm, v_fox_k_norm, v_fox_f_bias, v_gdn_conv, v_gdn_a_log, v_gdn_dt_bias, v_gdn_out_norm, v_w_out, v_ffn2_norm, v_ffn2_w_in, v_ffn2_w_out):
    wts = dict(ffn1_norm=ffn1_norm, ffn1_w_in=ffn1_w_in, ffn1_w_out=ffn1_w_out, mix_norm=mix_norm, w_in=w_in,
               fox_q_norm=fox_q_norm, fox_k_norm=fox_k_norm, fox_f_bias=fox_f_bias, gdn_conv=gdn_conv,
               gdn_a_log=gdn_a_log, gdn_dt_bias=gdn_dt_bias, gdn_out_norm=gdn_out_norm, w_out=w_out,
               ffn2_norm=ffn2_norm, ffn2_w_in=ffn2_w_in, ffn2_w_out=ffn2_w_out)
    mom = dict(ffn1_norm=m_ffn1_norm, ffn1_w_in=m_ffn1_w_in, ffn1_w_out=m_ffn1_w_out, mix_norm=m_mix_norm, w_in=m_w_in,
               fox_q_norm=m_fox_q_norm, fox_k_norm=m_fox_k_norm, fox_f_bias=m_fox_f_bias, gdn_conv=m_gdn_conv,
               gdn_a_log=m_gdn_a_log, gdn_dt_bias=m_gdn_dt_bias, gdn_out_norm=m_gdn_out_norm, w_out=m_w_out,
               ffn2_norm=m_ffn2_norm, ffn2_w_in=m_ffn2_w_in, ffn2_w_out=m_ffn2_w_out)
    var = dict(ffn1_norm=v_ffn1_norm, ffn1_w_in=v_ffn1_w_in, ffn1_w_out=v_ffn1_w_out, mix_norm=v_mix_norm, w_in=v_w_in,
               fox_q_norm=v_fox_q_norm, fox_k_norm=v_fox_k_norm, fox_f_bias=v_fox_f_bias, gdn_conv=v_gdn_conv,
               gdn_a_log=v_gdn_a_log, gdn_dt_bias=v_gdn_dt_bias, gdn_out_norm=v_gdn_out_norm, w_out=v_w_out,
               ffn2_norm=v_ffn2_norm, ffn2_w_in=v_ffn2_w_in, ffn2_w_out=v_ffn2_w_out)
    nb, seq, d = x.shape
    t = nb * seq
    depth = ffn1_norm.shape[0]

    shards = []
    for l in range(depth):
        for nm in BIG:
            s = wts[nm][l]
            shards.append(s if nm == "gdn_conv" else s.astype(BF16))
    full = exchange(shards, [False] * len(shards), "gather_weights")
    fw = [dict(zip(BIG, full[l * len(BIG):(l + 1) * len(BIG)])) for l in range(depth)]

    def layer_weights(l):
        g = fw[l]
        fb = g["ffn1_w_in"].shape[2]
        return dict(
            w1i=g["ffn1_w_in"].reshape(2, 4, d, fb), w1o=g["ffn1_w_out"].reshape(4, fb, d),
            w2i=g["ffn2_w_in"].reshape(2, 4, d, fb), w2o=g["ffn2_w_out"].reshape(4, fb, d),
            wi=_in_cols_to_mine(g["w_in"].transpose(1, 0, 2).reshape(d, -1)),
            cw=g["gdn_conv"].transpose(1, 0, 2).reshape(CONV_W, -1),
            wo=g["w_out"].reshape(d, d),
            n1=ffn1_norm[l][None], nmix=mix_norm[l][None], n2=ffn2_norm[l][None],
            qw=fox_q_norm[l][None], kw=fox_k_norm[l][None], onw=gdn_out_norm[l][None],
            gp=jnp.zeros((8, 128), F32).at[0, 0:8].set(fox_f_bias[l]).at[0, 8:12].set(gdn_dt_bias[l])
                  .at[1, 8:12].set(gdn_a_log[l]))

    h = x.reshape(t, d)
    saved = []
    for l in range(depth):
        p = layer_weights(l)
        x0 = h
        x1 = ffn_fwd(x0, p["n1"], p["w1i"], p["w1o"])
        proj, hn = inproj_fwd(x1, p["nmix"], p["wi"])
        gates = gates_fwd(proj, p["gp"], seq)
        yf, lse = attn_fwd(proj, gates, p["qw"], p["kw"], seq)
        qh, kh, vh = gdn_pre_fwd(proj, p["cw"], seq)
        yg, st = gdn_fwd(qh, kh, vh, gates, proj, p["onw"], seq)
        x2, ycat = outproj_fwd(x1, yf, yg, p["wo"])
        h = ffn_fwd(x2, p["n2"], p["w2i"], p["w2o"])
        saved.append(dict(p=p, x0=x0, x1=x1, x2=x2, proj=proj, hn=hn, gates=gates, yf=yf, lse=lse,
                          qh=qh, kh=kh, vh=vh, st=st, ycat=ycat))

    sq, dh = loss_head(h, loss_target.reshape(t, d))
    loss = lax.psum(0.5 * jnp.sum(sq) / d, ("x", "y", "c"))

    gbig = [None] * depth
    gsmall = {nm: [None] * depth for nm in SMALL}
    for l in reversed(range(depth)):
        s = saved[l]
        p = s["p"]
        dx2, dn2, dgu, hh, xn, dyh = ffn_bwd(s["x2"], dh, p["n2"], p["w2i"], p["w2o"])
        g_w2i = wgrad_ffn_in(xn, dgu)
        g_w2o = wgrad_ffn_out(hh, dyh)
        dyf, dyg, dyb = outproj_bwd(dx2, p["wo"])
        g_wo = wgrad_2d(s["ycat"], dyb, 512, "wgrad_w_out")
        dq, dk, dv, dga, dqw, dkw = attn_bwd(s["proj"], s["gates"], p["qw"], p["kw"], s["yf"], s["lse"], dyf, seq)
        dqh, dkh, dvh, dgg, dgb, donw = gdn_bwd(s["qh"], s["kh"], s["vh"], s["gates"], s["proj"], p["onw"],
                                                 s["st"], dyg, seq)
        dxq, dxk, dxv, dwq, dwk, dwv = gdn_pre_bwd(s["proj"], p["cw"], dqh, dkh, dvh, seq)
        dsm, dgp = gates_bwd(s["proj"], p["gp"], dga, dgb, seq)
        dx1, dnmix, dproj = inproj_bwd(s["x1"], dx2, p["nmix"], p["wi"], [dq, dk, dv, dxq, dxk, dxv, dgg, dsm])
        g_wi = wgrad_2d(s["hn"], dproj, 512, "wgrad_w_in")
        dh, dn1, dgu, hh, xn, dyh = ffn_bwd(s["x0"], dx1, p["n1"], p["w1i"], p["w1o"])
        g_w1i = wgrad_ffn_in(xn, dgu)
        g_w1o = wgrad_ffn_out(hh, dyh)
        fb = g_w1i.shape[3]
        g_cw = jnp.concatenate([dwq, dwk, dwv], axis=1)
        gbig[l] = dict(
            ffn1_w_in=g_w1i.reshape(N_DEV, d, fb), ffn1_w_out=g_w1o.reshape(N_DEV, -1, d),
            w_in=_in_cols_from_mine(g_wi).reshape(d, N_DEV, -1).transpose(1, 0, 2),
            gdn_conv=g_cw.reshape(CONV_W, N_DEV, -1).transpose(1, 0, 2),
            w_out=g_wo.reshape(N_DEV, -1, d),
            ffn2_w_in=g_w2i.reshape(N_DEV, d, fb), ffn2_w_out=g_w2o.reshape(N_DEV, -1, d))
        for nm, val in (("ffn1_norm", dn1[0]), ("mix_norm", dnmix[0]), ("fox_q_norm", dqw[0]), ("fox_k_norm", dkw[0]),
                        ("fox_f_bias", dgp[0, 0:8]), ("gdn_a_log", dgp[1, 8:12]), ("gdn_dt_bias", dgp[0, 8:12]),
                        ("gdn_out_norm", donw[0]), ("ffn2_norm", dn2[0])):
            gsmall[nm][l] = val
    grad_x = dh.reshape(nb, seq, d)

    small_pack = _pack_small({nm: jnp.stack(v) for nm, v in gsmall.items()})
    send = [gbig[l][nm] for l in range(depth) for nm in BIG] + [small_pack]
    got = exchange(send, [True] * (depth * len(BIG)) + [False], "exchange_grads")
    res = {}
    for i, nm in enumerate(BIG):
        per_layer = []
        for l in range(depth):
            slots = got[l * len(BIG) + i]
            shp = wts[nm].shape[1:]
            r, c = (shp[0], shp[1])
            outs = adamw_reduce(slots.reshape(N_DEV, r, c), wts[nm][l], mom[nm][l], var[nm][l], f"adamw_{nm}_{l}")
            per_layer.append(outs)
        res[nm] = [jnp.stack([per_layer[l][j] for l in range(depth)]) for j in range(4)]
    small_like = {nm: wts[nm] for nm in SMALL}
    sm = adamw_reduce(got[-1], _pack_small(small_like), _pack_small({nm: mom[nm] for nm in SMALL}),
                      _pack_small({nm: var[nm] for nm in SMALL}), "adamw_small")
    sm = [_unpack_small(a, small_like) for a in sm]
    for nm in SMALL:
        res[nm] = [sm[j][nm] for j in range(4)]
    return (loss, grad_x, *[res[nm][0] for nm in WEIGHTS], *[res[nm][1] for nm in WEIGHTS],
            *[res[nm][2] for nm in WEIGHTS], *[res[nm][3] for nm in WEIGHTS])
```

```python
import functools

import jax
import jax.numpy as jnp
from jax import lax
from jax.experimental import pallas as pl
from jax.experimental.pallas import tpu as pltpu

F32 = jnp.float32
BF16 = jnp.bfloat16
EPS = 1e-6
N_DEV = 8
MESH = pl.DeviceIdType.MESH
HIGHEST = lax.Precision.HIGHEST
VMEM_LIMIT = 56 * 1024 * 1024

FOX_HEADS, FOX_DH = 8, 64
GDN_HEADS, GDN_DH = 4, 128
CHUNK = 64
CONV_W = 4

ADAM_LR, ADAM_B1, ADAM_B2, ADAM_EPS, ADAM_WD, ADAM_STEP = 0.001, 0.9, 0.999, 1e-08, 0.01, 10


def _cp(*sem):
    return pltpu.CompilerParams(dimension_semantics=sem, vmem_limit_bytes=VMEM_LIMIT)


def _dot(a, b):
    return jnp.dot(a, b, preferred_element_type=F32)


def _dot_nt(a, b):
    return lax.dot_general(a, b, (((1,), (1,)), ((), ())), preferred_element_type=F32)


def _dot_tn(a, b):
    return lax.dot_general(a, b, (((0,), (0,)), ((), ())), preferred_element_type=F32)


def _rstd(xf):
    return lax.rsqrt(jnp.mean(xf * xf, axis=-1, keepdims=True) + EPS)


def _rms_bwd(xf, r, dyn):
    return r * dyn - xf * (r * r * r) * jnp.mean(dyn * xf, axis=-1, keepdims=True)


def ffn_fwd(x, nw, w_in, w_out, tm=512):
    t, d = x.shape
    nj, fb = w_out.shape[0], w_out.shape[1]

    def body(x_ref, nw_ref, wi_ref, wo_ref, o_ref, xn_ref, acc_ref):
        j = pl.program_id(1)

        @pl.when(j == 0)
        def _():
            xf = x_ref[...]
            xn_ref[...] = (xf * _rstd(xf) * nw_ref[...]).astype(BF16)
            acc_ref[...] = jnp.zeros_like(acc_ref)

        xn = xn_ref[...]
        g = _dot(xn, wi_ref[0])
        u = _dot(xn, wi_ref[1])
        h = (g * jax.nn.sigmoid(g) * u).astype(BF16)
        acc_ref[...] += _dot(h, wo_ref[...])

        @pl.when(j == nj - 1)
        def _():
            o_ref[...] = x_ref[...] + 0.5 * acc_ref[...]

    return pl.pallas_call(
        body, grid=(t // tm, nj),
        in_specs=[pl.BlockSpec((tm, d), lambda i, j: (i, 0)),
                  pl.BlockSpec((1, d), lambda i, j: (0, 0)),
                  pl.BlockSpec((2, None, d, fb), lambda i, j: (0, j, 0, 0)),
                  pl.BlockSpec((None, fb, d), lambda i, j: (j, 0, 0))],
        out_specs=pl.BlockSpec((tm, d), lambda i, j: (i, 0)),
        out_shape=jax.ShapeDtypeStruct((t, d), F32),
        scratch_shapes=[pltpu.VMEM((tm, d), BF16), pltpu.VMEM((tm, d), F32)],
        compiler_params=_cp("parallel", "arbitrary"), name="ffn_fwd")(x, nw, w_in, w_out)


def ffn_bwd(x, dy, nw, w_in, w_out, tm=512):
    t, d = x.shape
    nj, fb = w_out.shape[0], w_out.shape[1]

    def body(x_ref, dy_ref, nw_ref, wi_ref, wo_ref,
             dx_ref, dnw_ref, dgu_ref, h_ref, xn_ref, dyh_ref, acc_ref):
        i, j = pl.program_id(0), pl.program_id(1)

        @pl.when(j == 0)
        def _():
            xf = x_ref[...]
            xn_ref[...] = (xf * _rstd(xf) * nw_ref[...]).astype(BF16)
            dyh_ref[...] = (0.5 * dy_ref[...]).astype(BF16)
            acc_ref[...] = jnp.zeros_like(acc_ref)

        @pl.when((i == 0) & (j == 0))
        def _():
            dnw_ref[...] = jnp.zeros_like(dnw_ref)

        xn = xn_ref[...]
        g = _dot(xn, wi_ref[0])
        u = _dot(xn, wi_ref[1])
        sg = jax.nn.sigmoid(g)
        silu = g * sg
        dh = _dot_nt(dyh_ref[...], wo_ref[...])
        dg = (dh * u * (sg * (1.0 + g * (1.0 - sg)))).astype(BF16)
        du = (dh * silu).astype(BF16)
        dgu_ref[0] = dg
        dgu_ref[1] = du
        h_ref[...] = (silu * u).astype(BF16)
        acc_ref[...] += _dot_nt(dg, wi_ref[0]) + _dot_nt(du, wi_ref[1])

        @pl.when(j == nj - 1)
        def _():
            xf = x_ref[...]
            r = _rstd(xf)
            dxn = acc_ref[...]
            dnw_ref[...] += jnp.sum(dxn * xf * r, axis=0, keepdims=True)
            dx_ref[...] = _rms_bwd(xf, r, dxn * nw_ref[...]) + dy_ref[...]

    return pl.pallas_call(
        body, grid=(t // tm, nj),
        in_specs=[pl.BlockSpec((tm, d), lambda i, j: (i, 0)),
                  pl.BlockSpec((tm, d), lambda i, j: (i, 0)),
                  pl.BlockSpec((1, d), lambda i, j: (0, 0)),
                  pl.BlockSpec((2, None, d, fb), lambda i, j: (0, j, 0, 0)),
                  pl.BlockSpec((None, fb, d), lambda i, j: (j, 0, 0))],
        out_specs=[pl.BlockSpec((tm, d), lambda i, j: (i, 0)),
                   pl.BlockSpec((1, d), lambda i, j: (0, 0)),
                   pl.BlockSpec((2, None, tm, fb), lambda i, j: (0, j, i, 0)),
                   pl.BlockSpec((None, tm, fb), lambda i, j: (j, i, 0)),
                   pl.BlockSpec((tm, d), lambda i, j: (i, 0)),
                   pl.BlockSpec((tm, d), lambda i, j: (i, 0))],
        out_shape=[jax.ShapeDtypeStruct((t, d), F32),
                   jax.ShapeDtypeStruct((1, d), F32),
                   jax.ShapeDtypeStruct((2, nj, t, fb), BF16),
                   jax.ShapeDtypeStruct((nj, t, fb), BF16),
                   jax.ShapeDtypeStruct((t, d), BF16),
                   jax.ShapeDtypeStruct((t, d), BF16)],
        scratch_shapes=[pltpu.VMEM((tm, d), F32)],
        compiler_params=_cp("arbitrary", "arbitrary"), name="ffn_bwd")(x, dy, nw, w_in, w_out)


def _wgrad_call(a, b, a_spec, b_spec, out_shape, out_spec, grid, name):
    last = len(grid) - 1

    def body(a_ref, b_ref, o_ref):
        @pl.when(pl.program_id(last) == 0)
        def _():
            o_ref[...] = jnp.zeros_like(o_ref)

        o_ref[...] += _dot_tn(a_ref[...], b_ref[...])

    sem = ("parallel",) * last + ("arbitrary",)
    return pl.pallas_call(body, grid=grid, in_specs=[a_spec, b_spec], out_specs=out_spec,
                          out_shape=jax.ShapeDtypeStruct(out_shape, F32),
                          compiler_params=_cp(*sem), name=name)(a, b)


def wgrad_ffn_in(xn, dgu, tm=512):
    t, d = xn.shape
    _, nj, _, fb = dgu.shape
    return _wgrad_call(xn, dgu,
                       pl.BlockSpec((tm, d), lambda p, j, k: (k, 0)),
                       pl.BlockSpec((None, None, tm, fb), lambda p, j, k: (p, j, k, 0)),
                       (2, nj, d, fb), pl.BlockSpec((None, None, d, fb), lambda p, j, k: (p, j, 0, 0)),
                       (2, nj, t // tm), "wgrad_ffn_in")


def wgrad_ffn_out(h, dyh, tm=512):
    nj, t, fb = h.shape
    d = dyh.shape[1]
    return _wgrad_call(h, dyh,
                       pl.BlockSpec((None, tm, fb), lambda j, k: (j, k, 0)),
                       pl.BlockSpec((tm, d), lambda j, k: (k, 0)),
                       (nj, fb, d), pl.BlockSpec((None, fb, d), lambda j, k: (j, 0, 0)),
                       (nj, t // tm), "wgrad_ffn_out")


def wgrad_2d(a, b, tk, name, tm=512):
    t, k = a.shape
    n = b.shape[1]
    return _wgrad_call(a, b,
                       pl.BlockSpec((tm, tk), lambda c, s: (s, c)),
                       pl.BlockSpec((tm, n), lambda c, s: (s, 0)),
                       (k, n), pl.BlockSpec((tk, n), lambda c, s: (c, 0)),
                       (k // tk, t // tm), name)


N_BIG = 7 * 512
N_PROJ = N_BIG + 128
COL_SMALL = N_BIG // 128


def inproj_fwd(x, nw, w, tm=256):
    t, d = x.shape
    n = w.shape[1]

    def body(x_ref, nw_ref, w_ref, p_ref, hn_ref):
        xf = x_ref[...]
        hn = (xf * _rstd(xf) * nw_ref[...]).astype(BF16)
        hn_ref[...] = hn
        p_ref[...] = _dot(hn, w_ref[...])

    return pl.pallas_call(
        body, grid=(t // tm,),
        in_specs=[pl.BlockSpec((tm, d), lambda i: (i, 0)), pl.BlockSpec((1, d), lambda i: (0, 0)),
                  pl.BlockSpec((d, n), lambda i: (0, 0))],
        out_specs=[pl.BlockSpec((tm, n), lambda i: (i, 0)), pl.BlockSpec((tm, d), lambda i: (i, 0))],
        out_shape=[jax.ShapeDtypeStruct((t, n), F32), jax.ShapeDtypeStruct((t, d), BF16)],
        compiler_params=_cp("parallel"), name="inproj_fwd")(x, nw, w)


def inproj_bwd(x, dres, nw, w, dparts, tm=256):
    t, d = x.shape
    n = w.shape[1]
    widths = [p.shape[1] for p in dparts]
    assert sum(widths) == n

    def body(x_ref, dres_ref, nw_ref, w_ref, *rest):
        part_refs, (dx_ref, dnw_ref, dp_ref) = rest[:len(widths)], rest[len(widths):]

        @pl.when(pl.program_id(0) == 0)
        def _():
            dnw_ref[...] = jnp.zeros_like(dnw_ref)

        dp = jnp.concatenate([r[...].astype(BF16) for r in part_refs], axis=1)
        dp_ref[...] = dp
        dhn = _dot_nt(dp, w_ref[...])
        xf = x_ref[...]
        r = _rstd(xf)
        dnw_ref[...] += jnp.sum(dhn * xf * r, axis=0, keepdims=True)
        dx_ref[...] = _rms_bwd(xf, r, dhn * nw_ref[...]) + dres_ref[...]

    return pl.pallas_call(
        body, grid=(t // tm,),
        in_specs=[pl.BlockSpec((tm, d), lambda i: (i, 0)), pl.BlockSpec((tm, d), lambda i: (i, 0)),
                  pl.BlockSpec((1, d), lambda i: (0, 0)), pl.BlockSpec((d, n), lambda i: (0, 0))]
                 + [pl.BlockSpec((tm, wd), lambda i: (i, 0)) for wd in widths],
        out_specs=[pl.BlockSpec((tm, d), lambda i: (i, 0)), pl.BlockSpec((1, d), lambda i: (0, 0)),
                   pl.BlockSpec((tm, n), lambda i: (i, 0))],
        out_shape=[jax.ShapeDtypeStruct((t, d), F32), jax.ShapeDtypeStruct((1, d), F32),
                   jax.ShapeDtypeStruct((t, n), BF16)],
        compiler_params=_cp("arbitrary"), name="inproj_bwd")(x, dres, nw, w, *dparts)


def outproj_fwd(x, yf, yg, w, tm=512):
    t, d = x.shape
    hw = yf.shape[1]

    def body(x_ref, yf_ref, yg_ref, w_ref, o_ref, y_ref):
        y = jnp.concatenate([yf_ref[...], yg_ref[...]], axis=1).astype(BF16)
        y_ref[...] = y
        o_ref[...] = x_ref[...] + _dot(y, w_ref[...])

    return pl.pallas_call(
        body, grid=(t // tm,),
        in_specs=[pl.BlockSpec((tm, d), lambda i: (i, 0)), pl.BlockSpec((tm, hw), lambda i: (i, 0)),
                  pl.BlockSpec((tm, hw), lambda i: (i, 0)), pl.BlockSpec((2 * hw, d), lambda i: (0, 0))],
        out_specs=[pl.BlockSpec((tm, d), lambda i: (i, 0)), pl.BlockSpec((tm, 2 * hw), lambda i: (i, 0))],
        out_shape=[jax.ShapeDtypeStruct((t, d), F32), jax.ShapeDtypeStruct((t, 2 * hw), BF16)],
        compiler_params=_cp("parallel"), name="outproj_fwd")(x, yf, yg, w)


def outproj_bwd(dy, w, tm=512):
    t, d = dy.shape
    hw = w.shape[0] // 2

    def body(dy_ref, w_ref, df_ref, dg_ref, dyb_ref):
        dyb = dy_ref[...].astype(BF16)
        dyb_ref[...] = dyb
        dyy = _dot_nt(dyb, w_ref[...])
        df_ref[...] = dyy[:, :hw]
        dg_ref[...] = dyy[:, hw:]

    return pl.pallas_call(
        body, grid=(t // tm,),
        in_specs=[pl.BlockSpec((tm, d), lambda i: (i, 0)), pl.BlockSpec((2 * hw, d), lambda i: (0, 0))],
        out_specs=[pl.BlockSpec((tm, hw), lambda i: (i, 0)), pl.BlockSpec((tm, hw), lambda i: (i, 0)),
                   pl.BlockSpec((tm, d), lambda i: (i, 0))],
        out_shape=[jax.ShapeDtypeStruct((t, hw), F32), jax.ShapeDtypeStruct((t, hw), F32),
                   jax.ShapeDtypeStruct((t, d), BF16)],
        compiler_params=_cp("parallel"), name="outproj_bwd")(dy, w)


def _lane(shape):
    return lax.broadcasted_iota(jnp.int32, shape, 1)


def _row(shape):
    return lax.broadcasted_iota(jnp.int32, shape, 0)


def _gate_terms(val, gp_ref):
    z = val + gp_ref[0:1, :]
    sp = jnp.log(1.0 + jnp.exp(-jnp.abs(z)))
    return z, sp


def gates_fwd(proj, gp, seq, ts=512):
    t = proj.shape[0]
    nb, ns = t // seq, seq // ts

    def body(v_ref, gp_ref, o_ref, carry_ref):
        @pl.when(pl.program_id(1) == 0)
        def _():
            carry_ref[...] = jnp.zeros_like(carry_ref)

        z, sp = _gate_terms(v_ref[...], gp_ref)
        logsig = jnp.minimum(z, 0.0) - sp
        tri = (_row((ts, ts)) >= _lane((ts, ts))).astype(F32)
        cum = jnp.dot(tri, logsig, precision=HIGHEST, preferred_element_type=F32) + carry_ref[0:1, :]
        carry_ref[0:1, :] = cum[ts - 1:ts, :]
        g = -jnp.exp(gp_ref[1:2, :]) * (jnp.maximum(z, 0.0) + sp)
        beta = jax.nn.sigmoid(z)
        lane = _lane((ts, 128))
        o_ref[...] = jnp.where(lane < 8, cum, jnp.where(lane < 12, g, jnp.where(lane < 16, beta, 0.0)))

    return pl.pallas_call(
        body, grid=(nb, ns),
        in_specs=[pl.BlockSpec((ts, 128), lambda b, s: (b * ns + s, COL_SMALL)),
                  pl.BlockSpec((8, 128), lambda b, s: (0, 0))],
        out_specs=pl.BlockSpec((ts, 128), lambda b, s: (b * ns + s, 0)),
        out_shape=jax.ShapeDtypeStruct((t, 128), F32),
        scratch_shapes=[pltpu.VMEM((8, 128), F32)],
        compiler_params=_cp("parallel", "arbitrary"), name="gates_fwd")(proj, gp)


def gates_bwd(proj, gp, dga, dgb, seq, ts=512):
    t = proj.shape[0]
    nb, ns = t // seq, seq // ts

    def body(v_ref, gp_ref, da_ref, db_ref, ds_ref, dgp_ref, carry_ref):
        @pl.when(pl.program_id(1) == 0)
        def _():
            carry_ref[...] = jnp.zeros_like(carry_ref)

        @pl.when((pl.program_id(0) == 0) & (pl.program_id(1) == 0))
        def _():
            dgp_ref[...] = jnp.zeros_like(dgp_ref)

        lane = _lane((ts, 128))
        dgate = jnp.where(lane < 8, da_ref[...], jnp.where(lane < 16, db_ref[...], 0.0))
        z, sp = _gate_terms(v_ref[...], gp_ref)
        triu = (_row((ts, ts)) <= _lane((ts, ts))).astype(F32)
        dlog = jnp.dot(triu, dgate, precision=HIGHEST, preferred_element_type=F32) + carry_ref[0:1, :]
        carry_ref[0:1, :] = dlog[0:1, :]
        sig = jax.nn.sigmoid(z)
        nea = -jnp.exp(gp_ref[1:2, :])
        g = nea * (jnp.maximum(z, 0.0) + sp)
        dz = jnp.where(lane < 8, dlog * (1.0 - sig),
                       jnp.where(lane < 12, dgate * nea * sig, dgate * sig * (1.0 - sig)))
        dz = jnp.where(lane < 16, dz, 0.0)
        ds_ref[...] = dz
        dgp_ref[0:1, :] += jnp.where(lane[0:1] < 12, jnp.sum(dz, axis=0, keepdims=True), 0.0)
        dgp_ref[1:2, :] += jnp.where((lane[0:1] >= 8) & (lane[0:1] < 12), jnp.sum(dgate * g, axis=0, keepdims=True), 0.0)

    rev = lambda b, s: (b * ns + (ns - 1 - s), 0)
    return pl.pallas_call(
        body, grid=(nb, ns),
        in_specs=[pl.BlockSpec((ts, 128), lambda b, s: (b * ns + (ns - 1 - s), COL_SMALL)),
                  pl.BlockSpec((8, 128), lambda b, s: (0, 0)),
                  pl.BlockSpec((ts, 128), rev), pl.BlockSpec((ts, 128), rev)],
        out_specs=[pl.BlockSpec((ts, 128), rev), pl.BlockSpec((8, 128), lambda b, s: (0, 0))],
        out_shape=[jax.ShapeDtypeStruct((t, 128), F32), jax.ShapeDtypeStruct((8, 128), F32)],
        scratch_shapes=[pltpu.VMEM((8, 128), F32)],
        compiler_params=_cp("arbitrary", "arbitrary"), name="gates_bwd")(proj, gp, dga, dgb)


NEG = -1e30
ATTN_TQ_FWD = 1024
ATTN_TQ_BWD = 512


def _pick_lane(tile, idx):
    return jnp.sum(jnp.where(_lane(tile.shape) == idx, tile, 0.0), axis=1, keepdims=True)


def _col_to_row(col, n):
    return jnp.sum(jnp.where(_row((n, n)) == _lane((n, n)), col, 0.0), axis=0, keepdims=True)


def _row_to_col(row, n):
    return jnp.sum(jnp.where(_row((n, n)) == _lane((n, n)), row, 0.0), axis=1, keepdims=True)


def _rows(i, n):
    return pl.ds(pl.multiple_of(i * n, n), n)


def _once(shape, index_map):
    return pl.BlockSpec(shape, index_map, pipeline_mode=pl.Buffered(1))


def attn_fwd(proj, gates, qw, kw, seq, tq=256):
    t = proj.shape[0]
    nb, nq, dh = t // seq, seq // tq, FOX_DH
    scale = dh ** -0.5

    def body(q_ref, k_ref, v_ref, g_ref, qw_ref, kw_ref, y_ref, lse_ref, qs, ks, vs, ccol, crow):
        p = pl.program_id(1)
        heads = range(2)

        def prep(i, _):
            r = _rows(i, tq)
            for hh in heads:
                lanes = slice(hh * dh, (hh + 1) * dh)
                qf, kf = q_ref[r, lanes], k_ref[r, lanes]
                qs[hh, r, :] = (qf * _rstd(qf) * qw_ref[...] * scale).astype(BF16)
                ks[hh, r, :] = (kf * _rstd(kf) * kw_ref[...]).astype(BF16)
                vs[hh, r, :] = v_ref[r, lanes].astype(BF16)
                cc = _pick_lane(g_ref[r, :], 2 * p + hh)
                ccol[hh, r, :] = cc
                crow[hh * nq + i] = _col_to_row(cc, tq)
            return 0

        lax.fori_loop(0, nq, prep, 0)

        def q_tile(i, _):
            r = _rows(i, tq)
            qt = [qs[hh, r, :] for hh in heads]
            cc = [ccol[hh, r, :] for hh in heads]

            def kv_step(j, carry, masked):
                kr = _rows(j, tq)
                out = []
                for hh in heads:
                    m, l, acc = carry[hh]
                    s = _dot_nt(qt[hh], ks[hh, kr, :]) + (cc[hh] - crow[hh * nq + j])
                    if masked:
                        s = jnp.where(_row((tq, tq)) >= _lane((tq, tq)), s, NEG)
                    m_new = jnp.maximum(m, jnp.max(s, axis=1, keepdims=True))
                    pe = jnp.exp(s - m_new)
                    a = jnp.exp(m - m_new)
                    out.append((m_new, a * l + jnp.sum(pe, axis=1, keepdims=True),
                                a * acc + _dot(pe.astype(BF16), vs[hh, kr, :])))
                return tuple(out)

            one = (jnp.full((tq, 1), NEG, F32), jnp.zeros((tq, 1), F32), jnp.zeros((tq, dh), F32))
            carry = lax.fori_loop(0, i, lambda j, c: kv_step(j, c, False), (one, one))
            carry = kv_step(i, carry, True)
            for hh in heads:
                m, l, acc = carry[hh]
                lanes = slice(hh * dh, (hh + 1) * dh)
                y_ref[r, lanes] = acc / l
                lse_ref[r, lanes] = jnp.broadcast_to(m + jnp.log(l), (tq, dh))
            return 0

        lax.fori_loop(0, nq, q_tile, 0)

    blk = lambda off: _once((seq, 128), lambda b, p: (b, off + p))
    return pl.pallas_call(
        body, grid=(nb, 4),
        in_specs=[blk(0), blk(4), blk(8), _once((seq, 128), lambda b, p: (b, 0)),
                  pl.BlockSpec((1, dh), lambda b, p: (0, 0)), pl.BlockSpec((1, dh), lambda b, p: (0, 0))],
        out_specs=[pl.BlockSpec((seq, 128), lambda b, p: (b, p)), pl.BlockSpec((seq, 128), lambda b, p: (b, p))],
        out_shape=[jax.ShapeDtypeStruct((t, 512), F32), jax.ShapeDtypeStruct((t, 512), F32)],
        scratch_shapes=[pltpu.VMEM((2, seq, dh), BF16), pltpu.VMEM((2, seq, dh), BF16), pltpu.VMEM((2, seq, dh), BF16),
                        pltpu.VMEM((2, seq, 1), F32), pltpu.VMEM((2 * nq, 1, tq), F32)],
        compiler_params=_cp("parallel", "arbitrary"), name="attn_fwd")(proj, proj, proj, gates, qw, kw)


def attn_bwd(proj, gates, qw, kw, y, lse, dy, seq, tq=256):
    t = proj.shape[0]
    nb, nq, dh = t // seq, seq // tq, FOX_DH
    scale = dh ** -0.5

    def body(q_ref, k_ref, v_ref, g_ref, qw_ref, kw_ref, y_ref, lse_ref, dy_ref,
             dq_ref, dk_ref, dv_ref, dg_ref, dqw_ref, dkw_ref,
             qs, ks, vs, dos, cols, crow, dqa, dka):
        b, p = pl.program_id(0), pl.program_id(1)

        @pl.when((b == 0) & (p == 0))
        def _():
            dqw_ref[...] = jnp.zeros_like(dqw_ref)
            dkw_ref[...] = jnp.zeros_like(dkw_ref)

        @pl.when(p == 0)
        def _():
            dg_ref[...] = jnp.zeros_like(dg_ref)

        heads = range(2)
        hl = lambda hh: slice(hh * dh, (hh + 1) * dh)

        def prep(i, _):
            r = _rows(i, tq)
            for hh in heads:
                lanes = hl(hh)
                qf, kf = q_ref[r, lanes], k_ref[r, lanes]
                qs[hh, r, :] = (qf * _rstd(qf) * qw_ref[...] * scale).astype(BF16)
                ks[hh, r, :] = (kf * _rstd(kf) * kw_ref[...]).astype(BF16)
                vs[hh, r, :] = v_ref[r, lanes].astype(BF16)
                dyf = dy_ref[r, lanes]
                dos[hh, r, :] = dyf.astype(BF16)
                cc = _pick_lane(g_ref[r, :], 2 * p + hh)
                crow[hh * nq + i] = _col_to_row(cc, tq)
                delta = jnp.sum(dyf * y_ref[r, lanes], axis=1, keepdims=True)
                lane = _lane((tq, 128))
                cols[hh, r, :] = jnp.where(lane == 0, cc, jnp.where(lane == 1, lse_ref[r, hh * dh:hh * dh + 1],
                                                                     jnp.where(lane == 2, delta, 0.0)))
                dqa[hh, r, :] = jnp.zeros((tq, dh), F32)
            return 0

        lax.fori_loop(0, nq, prep, 0)

        def kv_tile(j, _):
            kr = _rows(j, tq)
            kt = [ks[hh, kr, :] for hh in heads]
            vt = [vs[hh, kr, :] for hh in heads]
            cr = [crow[hh * nq + j] for hh in heads]

            def q_step(i, carry, masked):
                r = _rows(i, tq)
                out = []
                for hh in heads:
                    dk, dv, dcr = carry[hh]
                    qt, dot, cl = qs[hh, r, :], dos[hh, r, :], cols[hh, r, :]
                    s = _dot_nt(qt, kt[hh]) + (cl[:, 0:1] - cr[hh])
                    if masked:
                        s = jnp.where(_row((tq, tq)) >= _lane((tq, tq)), s, NEG)
                    pe = jnp.exp(s - cl[:, 1:2])
                    ds = pe * (_dot_nt(dot, vt[hh]) - cl[:, 2:3])
                    dsb = ds.astype(BF16)
                    dqa[hh, r, :] += _dot(dsb, kt[hh])
                    cols[hh, r, :] = cl + jnp.where(_lane((tq, 128)) == 3, jnp.sum(ds, axis=1, keepdims=True), 0.0)
                    out.append((dk + _dot_tn(dsb, qt), dv + _dot_tn(pe.astype(BF16), dot),
                                dcr - jnp.sum(ds, axis=0, keepdims=True)))
                return tuple(out)

            one = (jnp.zeros((tq, dh), F32), jnp.zeros((tq, dh), F32), jnp.zeros((1, tq), F32))
            carry = q_step(j, (one, one), True)
            carry = lax.fori_loop(j + 1, nq, lambda i, c: q_step(i, c, False), carry)
            for hh in heads:
                dk, dv, dcr = carry[hh]
                dka[hh, kr, :] = dk
                dv_ref[kr, hl(hh)] = dv
                dg_ref[kr, :] = jnp.where(_lane((tq, 128)) == 2 * p + hh, _row_to_col(dcr, tq), dg_ref[kr, :])
            return 0

        lax.fori_loop(0, nq, kv_tile, 0)

        def post(i, _):
            r = _rows(i, tq)
            for hh in heads:
                lanes = hl(hh)
                qf, kf = q_ref[r, lanes], k_ref[r, lanes]
                rq, rk = _rstd(qf), _rstd(kf)
                dqn, dkn = dqa[hh, r, :] * scale, dka[hh, r, :]
                dqw_ref[...] += jnp.sum(dqn * qf * rq, axis=0, keepdims=True)
                dkw_ref[...] += jnp.sum(dkn * kf * rk, axis=0, keepdims=True)
                dq_ref[r, lanes] = _rms_bwd(qf, rq, dqn * qw_ref[...])
                dk_ref[r, lanes] = _rms_bwd(kf, rk, dkn * kw_ref[...])
                dg_ref[r, :] += jnp.where(_lane((tq, 128)) == 2 * p + hh, cols[hh, r, 3:4], 0.0)
            return 0

        lax.fori_loop(0, nq, post, 0)

    blk = lambda off: _once((seq, 128), lambda b, p: (b, off + p))
    own = lambda: _once((seq, 128), lambda b, p: (b, p))
    vec = lambda: pl.BlockSpec((1, dh), lambda b, p: (0, 0))
    return pl.pallas_call(
        body, grid=(nb, 4),
        in_specs=[blk(0), blk(4), blk(8), _once((seq, 128), lambda b, p: (b, 0)), vec(), vec(), own(), own(), own()],
        out_specs=[own(), own(), own(), _once((seq, 128), lambda b, p: (b, 0)), vec(), vec()],
        out_shape=[jax.ShapeDtypeStruct((t, 512), F32)] * 3
                  + [jax.ShapeDtypeStruct((t, 128), F32), jax.ShapeDtypeStruct((1, dh), F32), jax.ShapeDtypeStruct((1, dh), F32)],
        scratch_shapes=[pltpu.VMEM((2, seq, dh), BF16)] * 4
                       + [pltpu.VMEM((2, seq, 128), F32), pltpu.VMEM((2 * nq, 1, tq), F32),
                          pltpu.VMEM((2, seq, dh), F32), pltpu.VMEM((2, seq, dh), F32)],
        compiler_params=_cp("arbitrary", "arbitrary"), name="attn_bwd")(proj, proj, proj, gates, qw, kw, y, lse, dy)


def _silu_grad(c, sg):
    return sg * (1.0 + c * (1.0 - sg))


def _conv(x, w, n):
    row = _row(x.shape)
    c = x * w[CONV_W - 1:CONV_W, :]
    for k in range(CONV_W - 1):
        sh = CONV_W - 1 - k
        c = c + w[k:k + 1, :] * jnp.where(row >= sh, pltpu.roll(x, sh, 0), 0.0)
    return c


def gdn_pre_fwd(proj, cw, seq):
    t = proj.shape[0]
    nb = t // seq
    scale = GDN_DH ** -0.5

    def body(xq_ref, xk_ref, xv_ref, wq_ref, wk_ref, wv_ref, q_ref, k_ref, v_ref):
        def act(x_ref, w_ref):
            c = _conv(x_ref[...], w_ref[...], seq)
            return c * jax.nn.sigmoid(c)

        aq, ak = act(xq_ref, wq_ref), act(xk_ref, wk_ref)
        q_ref[...] = aq * lax.rsqrt(jnp.sum(aq * aq, axis=1, keepdims=True) + EPS) * scale
        k_ref[...] = ak * lax.rsqrt(jnp.sum(ak * ak, axis=1, keepdims=True) + EPS)
        v_ref[...] = act(xv_ref, wv_ref)

    xb = lambda off: pl.BlockSpec((seq, 128), lambda b, h: (b, off + h))
    wb = lambda off: pl.BlockSpec((CONV_W, 128), lambda b, h: (0, off + h))
    ob = lambda: pl.BlockSpec((seq, 128), lambda b, h: (b, h))
    return pl.pallas_call(
        body, grid=(nb, GDN_HEADS),
        in_specs=[xb(12), xb(16), xb(20), wb(0), wb(4), wb(8)],
        out_specs=[ob(), ob(), ob()],
        out_shape=[jax.ShapeDtypeStruct((t, 512), F32)] * 3,
        compiler_params=_cp("parallel", "parallel"), name="gdn_pre_fwd")(proj, proj, proj, cw, cw, cw)


def gdn_pre_bwd(proj, cw, dq, dk, dv, seq):
    t = proj.shape[0]
    nb = t // seq
    scale = GDN_DH ** -0.5

    def body(xq_ref, xk_ref, xv_ref, wq_ref, wk_ref, wv_ref, dq_ref, dk_ref, dv_ref,
             dxq_ref, dxk_ref, dxv_ref, dwq_ref, dwk_ref, dwv_ref):
        first = pl.program_id(1) == 0
        row = _row((seq, 128))

        def one(x_ref, w_ref, dy_ref, dx_ref, dw_ref, norm, sc):
            x, w = x_ref[...], w_ref[...]
            c = _conv(x, w, seq)
            sg = jax.nn.sigmoid(c)
            dy = dy_ref[...]
            if norm:
                a = c * sg
                rs = lax.rsqrt(jnp.sum(a * a, axis=1, keepdims=True) + EPS)
                dy = dy * sc
                da = rs * dy - a * (rs * rs * rs) * jnp.sum(dy * a, axis=1, keepdims=True)
            else:
                da = dy
            dc = da * _silu_grad(c, sg)
            dx = dc * w[CONV_W - 1:CONV_W, :]
            dws = [None] * CONV_W
            dws[CONV_W - 1] = jnp.sum(dc * x, axis=0, keepdims=True)
            for k in range(CONV_W - 1):
                sh = CONV_W - 1 - k
                dx = dx + w[k:k + 1, :] * jnp.where(row < seq - sh, pltpu.roll(dc, seq - sh, 0), 0.0)
                dws[k] = jnp.sum(dc * jnp.where(row >= sh, pltpu.roll(x, sh, 0), 0.0), axis=0, keepdims=True)
            dx_ref[...] = dx
            dwn = jnp.concatenate(dws, axis=0)

            @pl.when(first)
            def _():
                dw_ref[...] = dwn

            @pl.when(jnp.logical_not(first))
            def _():
                dw_ref[...] += dwn

        one(xq_ref, wq_ref, dq_ref, dxq_ref, dwq_ref, True, scale)
        one(xk_ref, wk_ref, dk_ref, dxk_ref, dwk_ref, True, 1.0)
        one(xv_ref, wv_ref, dv_ref, dxv_ref, dwv_ref, False, 1.0)

    xb = lambda off: pl.BlockSpec((seq, 128), lambda h, b: (b, off + h))
    wb = lambda off: pl.BlockSpec((CONV_W, 128), lambda h, b: (0, off + h))
    ob = lambda: pl.BlockSpec((seq, 128), lambda h, b: (b, h))
    return pl.pallas_call(
        body, grid=(GDN_HEADS, nb),
        in_specs=[xb(12), xb(16), xb(20), wb(0), wb(4), wb(8), ob(), ob(), ob()],
        out_specs=[ob(), ob(), ob()] + [pl.BlockSpec((CONV_W, 128), lambda h, b: (0, h))] * 3,
        out_shape=[jax.ShapeDtypeStruct((t, 512), F32)] * 3 + [jax.ShapeDtypeStruct((CONV_W, 512), F32)] * 3,
        compiler_params=_cp("parallel", "arbitrary"), name="gdn_pre_bwd")(proj, proj, proj, cw, cw, cw, dq, dk, dv)


def _b16(x):
    return x.astype(BF16)


@jax.custom_vjp
def _mm(a, b):
    return _dot(_b16(a), _b16(b))


_mm.defvjp(lambda a, b: (_mm(a, b), (a, b)),
           lambda res, g: (_dot_nt(_b16(g), _b16(res[1])), _dot_tn(_b16(res[0]), _b16(g))))


@jax.custom_vjp
def _mm_nt(a, b):
    return _dot_nt(_b16(a), _b16(b))


_mm_nt.defvjp(lambda a, b: (_mm_nt(a, b), (a, b)),
              lambda res, g: (_dot(_b16(g), _b16(res[1])), _dot_tn(_b16(g), _b16(res[0]))))


@jax.custom_vjp
def _mm_tn(a, b):
    return _dot_tn(_b16(a), _b16(b))


_mm_tn.defvjp(lambda a, b: (_mm_tn(a, b), (a, b)),
              lambda res, g: (_dot_nt(_b16(res[1]), _b16(g)), _dot(_b16(res[0]), _b16(g))))


def _dot32(a, b, dims=(((1,), (0,)), ((), ()))):
    def split(x):
        hi = x.astype(BF16)
        return hi, (x - hi.astype(F32)).astype(BF16)

    (ah, al), (bh, bl) = split(a), split(b)
    d = lambda x, y: lax.dot_general(x, y, dims, preferred_element_type=F32)
    return d(ah, bh) + (d(ah, bl) + d(al, bh))


def _inv_fwd_many(mats):
    n = mats[0].shape[0]
    eye = (_row((n, n)) == _lane((n, n))).astype(F32)
    invs, pws = [eye - a for a in mats], list(mats)
    for _ in range(n.bit_length() - 2):
        pws = [_dot32(pw, pw) for pw in pws]
        invs = [inv + _dot32(inv, pw) for inv, pw in zip(invs, pws)]
    return invs


@jax.custom_vjp
def _inv_saved(a, inv):
    return inv


def _inv_saved_bwd(inv, g):
    tg = _dot32(inv, g, (((0,), (0,)), ((), ())))
    return -_dot32(tg, inv, (((1,), (1,)), ((), ()))), jnp.zeros_like(inv)


_inv_saved.defvjp(lambda a, inv: (inv, inv), _inv_saved_bwd)


def _gdn_decay(gcol):
    c = CHUNK
    ri, ci = _row((c, c)), _lane((c, c))
    incl, eye = ri >= ci, ri == ci
    grow = jnp.sum(jnp.where(eye, gcol, 0.0), axis=0, keepdims=True)
    gc = jnp.sum(jnp.where(incl, grow, 0.0), axis=1, keepdims=True)
    gcr = jnp.sum(jnp.where(eye, gc, 0.0), axis=0, keepdims=True)
    gl = jnp.sum(jnp.where(_row((c, 1)) == c - 1, gc, 0.0), axis=0, keepdims=True)
    return gc, gl, jnp.exp(jnp.where(incl, gc - gcr, NEG))


def _gdn_a(k, bcol, decay):
    c = CHUNK
    return jnp.where(_row((c, c)) > _lane((c, c)), _mm_nt(k * bcol, k) * decay, 0.0)


def _gdn_chunk(q, k, v, gcol, bcol, state, gg, nw, inv_saved):
    c = CHUNK
    incl = _row((c, c)) >= _lane((c, c))
    gc, gl, decay = _gdn_decay(gcol)
    kb, vb = k * bcol, v * bcol
    inv = _inv_saved(_gdn_a(k, bcol, decay), inv_saved)
    eg = jnp.exp(gc)
    u = _mm(inv, vb)
    w = _mm(inv, kb * eg)
    pm = jnp.where(incl, _mm_nt(q, k) * decay, 0.0)
    kd = k * jnp.exp(gl - gc)
    qd = q * eg
    v_new = u - _mm(w, state)
    o = _mm(qd, state) + _mm(pm, v_new)
    state_new = state * jnp.exp(gl) + _mm_tn(kd, v_new)
    y = o * _rstd(o) * nw * (gg * jax.nn.sigmoid(gg))
    return y, state_new


_gdn_chunks = jax.vmap(_gdn_chunk, in_axes=(0, 0, 0, 0, 0, 0, 0, None, 0))


def _gdn_chain_inputs(chains, p, r, c, q_ref, k_ref, v_ref, g_ref, gg_ref, inv_ref):
    cols = {nm: [] for nm in ("q", "k", "v", "g", "b", "gg", "inv")}
    for b, hh in chains:
        h = 2 * p + hh
        ln = slice(hh * 128, (hh + 1) * 128)
        gt = g_ref[b, r, :]
        cols["q"].append(q_ref[b, r, ln])
        cols["k"].append(k_ref[b, r, ln])
        cols["v"].append(v_ref[b, r, ln])
        cols["g"].append(_pick_lane(gt, 8 + h))
        cols["b"].append(_pick_lane(gt, 12 + h))
        cols["gg"].append(gg_ref[b, r, ln])
        cols["inv"].append(inv_ref[b, hh, c])
    return [jnp.stack(cols[nm]) for nm in ("q", "k", "v", "g", "b", "gg", "inv")]


GDN_CB = 8


def gdn_inv(k, gates, seq):
    t = k.shape[0]
    nb, nc = t // seq, seq // CHUNK
    rb = GDN_CB * CHUNK
    nsb = seq // rb

    def body(k_ref, g_ref, o_ref):
        h = pl.program_id(1)
        mats = []
        for c in range(GDN_CB):
            r = slice(c * CHUNK, (c + 1) * CHUNK)
            gt = g_ref[r, :]
            _, _, decay = _gdn_decay(_pick_lane(gt, 8 + h))
            mats.append(_gdn_a(k_ref[r, :], _pick_lane(gt, 12 + h), decay))
        for c, inv in enumerate(_inv_fwd_many(mats)):
            o_ref[c] = inv

    return pl.pallas_call(
        body, grid=(nb, GDN_HEADS, nsb),
        in_specs=[pl.BlockSpec((rb, 128), lambda b, h, s: (b * nsb + s, h)),
                  pl.BlockSpec((rb, 128), lambda b, h, s: (b * nsb + s, 0))],
        out_specs=pl.BlockSpec((None, None, GDN_CB, CHUNK, CHUNK), lambda b, h, s: (b, h, s, 0, 0)),
        out_shape=jax.ShapeDtypeStruct((nb, GDN_HEADS, nc, CHUNK, CHUNK), F32),
        compiler_params=_cp("parallel", "parallel", "parallel"), name="gdn_inv")(k, gates)


def _gdn_specs(nb, nsb, rev):
    blk = (lambda s: nsb - 1 - s) if rev else (lambda s: s)
    rb = GDN_CB * CHUNK
    pair = lambda off=0: pl.BlockSpec((nb, rb, 256), lambda s, p: (0, blk(s), off + p))
    gate = lambda: pl.BlockSpec((nb, rb, 128), lambda s, p: (0, blk(s), 0))
    mats = lambda n: pl.BlockSpec((nb, 2, GDN_CB, n, n), lambda s, p: (0, p, blk(s), 0, 0))
    return pair, gate, mats


def gdn_fwd(q, k, v, gates, proj, nw, inv, seq):
    t = q.shape[0]
    nb, nc = t // seq, seq // CHUNK
    nsb = nc // GDN_CB
    chains = [(b, hh) for b in range(nb) for hh in range(2)]
    nch = len(chains)
    pair, gate, mats = _gdn_specs(nb, nsb, False)

    def body(q_ref, k_ref, v_ref, g_ref, gg_ref, inv_ref, nw_ref, y_ref, st_ref, carry):
        s, p = pl.program_id(0), pl.program_id(1)

        @pl.when(s == 0)
        def _():
            for ci in range(nch):
                carry[p * nch + ci] = jnp.zeros((GDN_DH, GDN_DH), F32)

        def step(c, states):
            r = _rows(c, CHUNK)
            for ci, (b, hh) in enumerate(chains):
                st_ref[b, hh, c] = states[ci]
            ins = _gdn_chain_inputs(chains, p, r, c, q_ref, k_ref, v_ref, g_ref, gg_ref, inv_ref)
            y, states = _gdn_chunks(*ins[:5], states, ins[5], nw_ref[...], ins[6])
            for ci, (b, hh) in enumerate(chains):
                y_ref[b, r, hh * 128:(hh + 1) * 128] = y[ci]
            return states

        states = lax.fori_loop(0, GDN_CB, step, jnp.stack([carry[p * nch + ci] for ci in range(nch)]))
        for ci in range(nch):
            carry[p * nch + ci] = states[ci]

    v3 = lambda a: a.reshape(nb, seq, a.shape[1])
    y, st = pl.pallas_call(
        body, grid=(nsb, 2),
        in_specs=[pair(), pair(), pair(), gate(), pair(12), mats(CHUNK), pl.BlockSpec((1, 128), lambda s, p: (0, 0))],
        out_specs=[pair(), mats(GDN_DH)],
        out_shape=[jax.ShapeDtypeStruct((nb, seq, 512), F32),
                   jax.ShapeDtypeStruct((nb, GDN_HEADS, nc, GDN_DH, GDN_DH), F32)],
        scratch_shapes=[pltpu.VMEM((2 * nch, GDN_DH, GDN_DH), F32)],
        compiler_params=_cp("arbitrary", "arbitrary"), name="gdn_fwd")(v3(q), v3(k), v3(v), v3(gates), v3(proj), inv, nw)
    return y.reshape(t, 512), st


def gdn_bwd(q, k, v, gates, proj, nw, inv, states, dy, seq):
    t = q.shape[0]
    nb, nc = t // seq, seq // CHUNK
    nsb = nc // GDN_CB
    chains = [(b, hh) for b in range(nb) for hh in range(2)]
    nch = len(chains)
    pair, gate, mats = _gdn_specs(nb, nsb, True)

    def body(q_ref, k_ref, v_ref, g_ref, gg_ref, inv_ref, st_ref, dy_ref, nw_ref,
             dq_ref, dk_ref, dv_ref, dgg_ref, dg_ref, dnw_ref, carry):
        s, p = pl.program_id(0), pl.program_id(1)

        @pl.when((s == 0) & (p == 0))
        def _():
            dnw_ref[...] = jnp.zeros_like(dnw_ref)

        @pl.when(p == 0)
        def _():
            dg_ref[...] = jnp.zeros_like(dg_ref)

        @pl.when(s == 0)
        def _():
            for ci in range(nch):
                carry[p * nch + ci] = jnp.zeros((GDN_DH, GDN_DH), F32)

        def step(idx, dstates):
            c = GDN_CB - 1 - idx
            r = _rows(c, CHUNK)
            ins = _gdn_chain_inputs(chains, p, r, c, q_ref, k_ref, v_ref, g_ref, gg_ref, inv_ref)
            st = jnp.stack([st_ref[b, hh, c] for b, hh in chains])
            dy = jnp.stack([dy_ref[b, r, hh * 128:(hh + 1) * 128] for b, hh in chains])
            _, vjp = jax.vjp(_gdn_chunks, *ins[:5], st, ins[5], nw_ref[...], ins[6])
            dq, dk, dv, dgc, dbc, dstates, dgg, dnw, _ = vjp((dy, dstates))
            dnw_ref[...] += dnw
            lane = _lane((CHUNK, 128))
            for ci, (b, hh) in enumerate(chains):
                h = 2 * p + hh
                ln = slice(hh * 128, (hh + 1) * 128)
                dq_ref[b, r, ln] = dq[ci]
                dk_ref[b, r, ln] = dk[ci]
                dv_ref[b, r, ln] = dv[ci]
                dgg_ref[b, r, ln] = dgg[ci]
                dg_ref[b, r, :] = jnp.where(lane == 8 + h, dgc[ci], jnp.where(lane == 12 + h, dbc[ci], dg_ref[b, r, :]))
            return dstates

        dstates = lax.fori_loop(0, GDN_CB, step, jnp.stack([carry[p * nch + ci] for ci in range(nch)]))
        for ci in range(nch):
            carry[p * nch + ci] = dstates[ci]

    v3 = lambda a: a.reshape(nb, seq, a.shape[1])
    res = pl.pallas_call(
        body, grid=(nsb, 2),
        in_specs=[pair(), pair(), pair(), gate(), pair(12), mats(CHUNK), mats(GDN_DH), pair(),
                  pl.BlockSpec((1, 128), lambda s, p: (0, 0))],
        out_specs=[pair(), pair(), pair(), pair(), gate(), pl.BlockSpec((1, 128), lambda s, p: (0, 0))],
        out_shape=[jax.ShapeDtypeStruct((nb, seq, 512), F32)] * 4
                  + [jax.ShapeDtypeStruct((nb, seq, 128), F32), jax.ShapeDtypeStruct((1, 128), F32)],
        scratch_shapes=[pltpu.VMEM((2 * nch, GDN_DH, GDN_DH), F32)],
        compiler_params=_cp("arbitrary", "arbitrary"),
        name="gdn_bwd")(v3(q), v3(k), v3(v), v3(gates), v3(proj), inv, states, v3(dy), nw)
    return [a.reshape(t, a.shape[2]) for a in res[:5]] + [res[5]]


def loss_head(y, target, tm=512):
    t, d = y.shape

    def body(y_ref, t_ref, s_ref, dy_ref):
        @pl.when(pl.program_id(0) == 0)
        def _():
            s_ref[...] = jnp.zeros_like(s_ref)

        err = y_ref[...] - t_ref[...]
        s_ref[...] += jnp.sum(err * err, axis=0, keepdims=True)
        dy_ref[...] = err * (1.0 / d)

    return pl.pallas_call(
        body, grid=(t // tm,),
        in_specs=[pl.BlockSpec((tm, d), lambda i: (i, 0)), pl.BlockSpec((tm, d), lambda i: (i, 0))],
        out_specs=[pl.BlockSpec((1, d), lambda i: (0, 0)), pl.BlockSpec((tm, d), lambda i: (i, 0))],
        out_shape=[jax.ShapeDtypeStruct((1, d), F32), jax.ShapeDtypeStruct((t, d), F32)],
        compiler_params=_cp("arbitrary"), name="loss_head")(y, target)


def _place():
    return lax.axis_index("x"), lax.axis_index("y"), lax.axis_index("c")


def _peer(k):
    x, y, c = _place()
    px = 1 - x if (k >> 2) & 1 else x
    py = 1 - y if (k >> 1) & 1 else y
    pc = 1 - c if k & 1 else c
    return (px, py, pc), 4 * px + 2 * py + pc


def exchange(arrays, scatter, name):
    n = len(arrays)

    def body(*refs):
        ins, outs = refs[:n], refs[n:2 * n]
        send_sems, recv_sems, loc_sems = refs[2 * n:]
        x, y, c = _place()
        me = 4 * x + 2 * y + c

        def src(a, idx):
            return ins[a].at[idx] if scatter[a] else ins[a]

        def remote(a, k):
            pid, pidx = _peer(k)
            return pltpu.make_async_remote_copy(src_ref=src(a, pidx), dst_ref=outs[a].at[me],
                                                send_sem=send_sems.at[a, k], recv_sem=recv_sems.at[a, k],
                                                device_id=pid, device_id_type=MESH)

        def arrival(a, k):
            pid, pidx = _peer(k)
            return pltpu.make_async_remote_copy(src_ref=src(a, pidx), dst_ref=outs[a].at[pidx],
                                                send_sem=send_sems.at[a, k], recv_sem=recv_sems.at[a, k],
                                                device_id=pid, device_id_type=MESH)

        local = [pltpu.make_async_copy(src(a, me), outs[a].at[me], loc_sems.at[a]) for a in range(n)]
        for k in range(1, N_DEV):
            for a in range(n):
                remote(a, k).start()
        for cp in local:
            cp.start()
        for k in range(1, N_DEV):
            for a in range(n):
                arrival(a, k).wait_recv()
        for k in range(1, N_DEV):
            for a in range(n):
                remote(a, k).wait_send()
        for cp in local:
            cp.wait()

    out_shape = [jax.ShapeDtypeStruct(a.shape if s else (N_DEV,) + a.shape, a.dtype) for a, s in zip(arrays, scatter)]
    anyspec = pl.BlockSpec(memory_space=pl.ANY)
    return pl.pallas_call(
        body, in_specs=[anyspec] * n, out_specs=[anyspec] * n, out_shape=out_shape,
        scratch_shapes=[pltpu.SemaphoreType.DMA((n, N_DEV)), pltpu.SemaphoreType.DMA((n, N_DEV)),
                        pltpu.SemaphoreType.DMA((n,))],
        name=name)(*arrays)


def adamw_reduce(slots, w, m, v, name):
    r, c = w.shape
    tr = r
    while tr * c * 4 > (1 << 20) and tr % 16 == 0:
        tr //= 2
    bc1 = 1.0 - ADAM_B1 ** ADAM_STEP
    bc2 = 1.0 - ADAM_B2 ** ADAM_STEP

    def body(s_ref, w_ref, m_ref, v_ref, g_ref, d_ref, nm_ref, nv_ref):
        g = s_ref[0]
        for j in range(1, N_DEV):
            g = g + s_ref[j]
        nm = ADAM_B1 * m_ref[...] + (1.0 - ADAM_B1) * g
        nv = ADAM_B2 * v_ref[...] + (1.0 - ADAM_B2) * (g * g)
        g_ref[...] = g
        nm_ref[...] = nm
        nv_ref[...] = nv
        d_ref[...] = -ADAM_LR * ((nm / bc1) / (jnp.sqrt(nv / bc2) + ADAM_EPS) + ADAM_WD * w_ref[...])

    blk = lambda: pl.BlockSpec((tr, c), lambda i: (i, 0))
    return pl.pallas_call(
        body, grid=(r // tr,),
        in_specs=[pl.BlockSpec((N_DEV, tr, c), lambda i: (0, i, 0)), blk(), blk(), blk()],
        out_specs=[blk(), blk(), blk(), blk()],
        out_shape=[jax.ShapeDtypeStruct((r, c), F32)] * 4,
        compiler_params=_cp("parallel"), name=name)(slots, w, m, v)


BIG = ("ffn1_w_in", "ffn1_w_out", "w_in", "gdn_conv", "w_out", "ffn2_w_in", "ffn2_w_out")
SMALL = ("ffn1_norm", "mix_norm", "fox_q_norm", "fox_k_norm", "fox_f_bias", "gdn_a_log", "gdn_dt_bias",
         "gdn_out_norm", "ffn2_norm")
WEIGHTS = ("ffn1_norm", "ffn1_w_in", "ffn1_w_out", "mix_norm", "w_in", "fox_q_norm", "fox_k_norm", "fox_f_bias",
           "gdn_conv", "gdn_a_log", "gdn_dt_bias", "gdn_out_norm", "w_out", "ffn2_norm", "ffn2_w_in", "ffn2_w_out")
IN_COLS = (("fq", 512), ("fk", 512), ("fv", 512), ("ff", 8), ("gq", 512), ("gk", 512), ("gv", 512),
           ("ga", 4), ("gb", 4), ("gg", 512))
MY_BIG = ("fq", "fk", "fv", "gq", "gk", "gv", "gg")
MY_SMALL = ("ff", "ga", "gb")
SMALL_ROWS = 8 * 128


def _in_cols_to_mine(w):
    off, parts = 0, {}
    for nm, wd in IN_COLS:
        parts[nm] = w[:, off:off + wd]
        off += wd
    small = jnp.concatenate([parts[nm] for nm in MY_SMALL], axis=1)
    small = jnp.pad(small, ((0, 0), (0, 128 - small.shape[1])))
    return jnp.concatenate([parts[nm] for nm in MY_BIG] + [small], axis=1)


def _in_cols_from_mine(g):
    parts = {nm: g[:, i * 512:(i + 1) * 512] for i, nm in enumerate(MY_BIG)}
    off = N_BIG
    for nm in MY_SMALL:
        wd = dict(IN_COLS)[nm]
        parts[nm] = g[:, off:off + wd]
        off += wd
    return jnp.concatenate([parts[nm] for nm, _ in IN_COLS], axis=1)


def _pack_small(vals):
    rows = []
    nl = vals[SMALL[0]].shape[0]
    for l in range(nl):
        for nm in SMALL:
            v = vals[nm][l].reshape(-1)
            pad = (-v.shape[0]) % SMALL_ROWS
            rows.append(jnp.pad(v, (0, pad)).reshape(-1, 128))
    return jnp.concatenate(rows, axis=0)


def _unpack_small(packed, like):
    out = {nm: [] for nm in SMALL}
    row = 0
    nl = like[SMALL[0]].shape[0]
    for l in range(nl):
        for nm in SMALL:
            n = like[nm].shape[1]
            nr = -(-n // SMALL_ROWS) * 8
            out[nm].append(packed[row:row + nr].reshape(-1)[:n])
            row += nr
    return {nm: jnp.stack(v) for nm, v in out.items()}


def kernel(x, ffn1_norm, ffn1_w_in, ffn1_w_out, mix_norm, w_in, fox_q_norm, fox_k_norm, fox_f_bias, gdn_conv, gdn_a_log, gdn_dt_bias, gdn_out_norm, w_out, ffn2_norm, ffn2_w_in, ffn2_w_out, loss_target, m_ffn1_norm, m_ffn1_w_in, m_ffn1_w_out, m_mix_norm, m_w_in, m_fox_q_norm, m_fox_k_norm, m_fox_f_bias, m_gdn_conv, m_gdn_a_log, m_gdn_dt_bias, m_gdn_out_norm, m_w_out, m_ffn2_norm, m_ffn2_w_in, m_ffn2_w_out, v_ffn1_norm, v_ffn1_w_in, v_ffn1_w_out, v_mix_norm, v_w_in, v_fox_q_norm, v_fox_k_norm, v_fox_f_bias, v_gdn_conv, v_gdn_a_log, v_gdn_dt_bias, v_gdn_out_norm, v_w_out, v_ffn2_norm, v_ffn2_w_in, v_ffn2_w_out):
    wts = dict(ffn1_norm=ffn1_norm, ffn1_w_in=ffn1_w_in, ffn1_w_out=ffn1_w_out, mix_norm=mix_norm, w_in=w_in,
               fox_q_norm=fox_q_norm, fox_k_norm=fox_k_norm, fox_f_bias=fox_f_bias, gdn_conv=gdn_conv,
               gdn_a_log=gdn_a_log, gdn_dt_bias=gdn_dt_bias, gdn_out_norm=gdn_out_norm, w_out=w_out,
               ffn2_norm=ffn2_norm, ffn2_w_in=ffn2_w_in, ffn2_w_out=ffn2_w_out)
    mom = dict(ffn1_norm=m_ffn1_norm, ffn1_w_in=m_ffn1_w_in, ffn1_w_out=m_ffn1_w_out, mix_norm=m_mix_norm, w_in=m_w_in,
               fox_q_norm=m_fox_q_norm, fox_k_norm=m_fox_k_norm, fox_f_bias=m_fox_f_bias, gdn_conv=m_gdn_conv,
               gdn_a_log=m_gdn_a_log, gdn_dt_bias=m_gdn_dt_bias, gdn_out_norm=m_gdn_out_norm, w_out=m_w_out,
               ffn2_norm=m_ffn2_norm, ffn2_w_in=m_ffn2_w_in, ffn2_w_out=m_ffn2_w_out)
    var = dict(ffn1_norm=v_ffn1_norm, ffn1_w_in=v_ffn1_w_in, ffn1_w_out=v_ffn1_w_out, mix_norm=v_mix_norm, w_in=v_w_in,
               fox_q_norm=v_fox_q_norm, fox_k_norm=v_fox_k_norm, fox_f_bias=v_fox_f_bias, gdn_conv=v_gdn_conv,
               gdn_a_log=v_gdn_a_log, gdn_dt_bias=v_gdn_dt_bias, gdn_out_norm=v_gdn_out_norm, w_out=v_w_out,
               ffn2_norm=v_ffn2_norm, ffn2_w_in=v_ffn2_w_in, ffn2_w_out=v_ffn2_w_out)
    nb, seq, d = x.shape
    t = nb * seq
    depth = ffn1_norm.shape[0]

    shards = []
    for l in range(depth):
        for nm in BIG:
            s = wts[nm][l]
            shards.append(s if nm == "gdn_conv" else s.astype(BF16))
    full = exchange(shards, [False] * len(shards), "gather_weights")
    fw = [dict(zip(BIG, full[l * len(BIG):(l + 1) * len(BIG)])) for l in range(depth)]

    def layer_weights(l):
        g = fw[l]
        fb = g["ffn1_w_in"].shape[2]
        return dict(
            w1i=g["ffn1_w_in"].reshape(2, 4, d, fb), w1o=g["ffn1_w_out"].reshape(4, fb, d),
            w2i=g["ffn2_w_in"].reshape(2, 4, d, fb), w2o=g["ffn2_w_out"].reshape(4, fb, d),
            wi=_in_cols_to_mine(g["w_in"].transpose(1, 0, 2).reshape(d, -1)),
            cw=g["gdn_conv"].transpose(1, 0, 2).reshape(CONV_W, -1),
            wo=g["w_out"].reshape(d, d),
            n1=ffn1_norm[l][None], nmix=mix_norm[l][None], n2=ffn2_norm[l][None],
            qw=fox_q_norm[l][None], kw=fox_k_norm[l][None], onw=gdn_out_norm[l][None],
            gp=jnp.concatenate([
                jnp.concatenate([fox_f_bias[l], gdn_dt_bias[l], jnp.zeros((116,), F32)])[None],
                jnp.concatenate([jnp.zeros((8,), F32), gdn_a_log[l], jnp.zeros((116,), F32)])[None],
                jnp.zeros((6, 128), F32)], axis=0))

    h = x.reshape(t, d)
    saved = []
    for l in range(depth):
        p = layer_weights(l)
        x0 = h
        x1 = ffn_fwd(x0, p["n1"], p["w1i"], p["w1o"])
        proj, hn = inproj_fwd(x1, p["nmix"], p["wi"])
        gates = gates_fwd(proj, p["gp"], seq)
        yf, lse = attn_fwd(proj, gates, p["qw"], p["kw"], seq, tq=min(seq, ATTN_TQ_FWD))
        qh, kh, vh = gdn_pre_fwd(proj, p["cw"], seq)
        inv = gdn_inv(kh, gates, seq)
        yg, st = gdn_fwd(qh, kh, vh, gates, proj, p["onw"], inv, seq)
        x2, ycat = outproj_fwd(x1, yf, yg, p["wo"])
        h = ffn_fwd(x2, p["n2"], p["w2i"], p["w2o"])
        saved.append(dict(p=p, x0=x0, x1=x1, x2=x2, proj=proj, hn=hn, gates=gates, yf=yf, lse=lse,
                          qh=qh, kh=kh, vh=vh, st=st, inv=inv, ycat=ycat))

    sq, dh = loss_head(h, loss_target.reshape(t, d))
    loss = lax.psum(0.5 * jnp.sum(sq) / d, ("x", "y", "c"))

    gbig = [None] * depth
    gsmall = {nm: [None] * depth for nm in SMALL}
    for l in reversed(range(depth)):
        s = saved[l]
        p = s["p"]
        dx2, dn2, dgu, hh, xn, dyh = ffn_bwd(s["x2"], dh, p["n2"], p["w2i"], p["w2o"])
        g_w2i = wgrad_ffn_in(xn, dgu)
        g_w2o = wgrad_ffn_out(hh, dyh)
        dyf, dyg, dyb = outproj_bwd(dx2, p["wo"])
        g_wo = wgrad_2d(s["ycat"], dyb, 512, "wgrad_w_out")
        dq, dk, dv, dga, dqw, dkw = attn_bwd(s["proj"], s["gates"], p["qw"], p["kw"], s["yf"], s["lse"], dyf, seq,
                                             tq=min(seq, ATTN_TQ_BWD))
        dqh, dkh, dvh, dgg, dgb, donw = gdn_bwd(s["qh"], s["kh"], s["vh"], s["gates"], s["proj"], p["onw"],
                                                 s["inv"], s["st"], dyg, seq)
        dxq, dxk, dxv, dwq, dwk, dwv = gdn_pre_bwd(s["proj"], p["cw"], dqh, dkh, dvh, seq)
        dsm, dgp = gates_bwd(s["proj"], p["gp"], dga, dgb, seq)
        dx1, dnmix, dproj = inproj_bwd(s["x1"], dx2, p["nmix"], p["wi"], [dq, dk, dv, dxq, dxk, dxv, dgg, dsm])
        g_wi = wgrad_2d(s["hn"], dproj, 512, "wgrad_w_in")
        dh, dn1, dgu, hh, xn, dyh = ffn_bwd(s["x0"], dx1, p["n1"], p["w1i"], p["w1o"])
        g_w1i = wgrad_ffn_in(xn, dgu)
        g_w1o = wgrad_ffn_out(hh, dyh)
        fb = g_w1i.shape[3]
        g_cw = jnp.concatenate([dwq, dwk, dwv], axis=1)
        gbig[l] = dict(
            ffn1_w_in=g_w1i.reshape(N_DEV, d, fb), ffn1_w_out=g_w1o.reshape(N_DEV, -1, d),
            w_in=_in_cols_from_mine(g_wi).reshape(d, N_DEV, -1).transpose(1, 0, 2),
            gdn_conv=g_cw.reshape(CONV_W, N_DEV, -1).transpose(1, 0, 2),
            w_out=g_wo.reshape(N_DEV, -1, d),
            ffn2_w_in=g_w2i.reshape(N_DEV, d, fb), ffn2_w_out=g_w2o.reshape(N_DEV, -1, d))
        for nm, val in (("ffn1_norm", dn1[0]), ("mix_norm", dnmix[0]), ("fox_q_norm", dqw[0]), ("fox_k_norm", dkw[0]),
                        ("fox_f_bias", dgp[0, 0:8]), ("gdn_a_log", dgp[1, 8:12]), ("gdn_dt_bias", dgp[0, 8:12]),
                        ("gdn_out_norm", donw[0]), ("ffn2_norm", dn2[0])):
            gsmall[nm][l] = val
    grad_x = dh.reshape(nb, seq, d)

    small_pack = _pack_small({nm: jnp.stack(v) for nm, v in gsmall.items()})
    send = [gbig[l][nm] for l in range(depth) for nm in BIG] + [small_pack]
    got = exchange(send, [True] * (depth * len(BIG)) + [False], "exchange_grads")
    res = {}
    for i, nm in enumerate(BIG):
        per_layer = []
        for l in range(depth):
            slots = got[l * len(BIG) + i]
            shp = wts[nm].shape[1:]
            r, c = (shp[0], shp[1])
            outs = adamw_reduce(slots.reshape(N_DEV, r, c), wts[nm][l], mom[nm][l], var[nm][l], f"adamw_{nm}_{l}")
            per_layer.append(outs)
        res[nm] = [jnp.stack([per_layer[l][j] for l in range(depth)]) for j in range(4)]
    small_like = {nm: wts[nm] for nm in SMALL}
    sm = adamw_reduce(got[-1], _pack_small(small_like), _pack_small({nm: mom[nm] for nm in SMALL}),
                      _pack_small({nm: var[nm] for nm in SMALL}), "adamw_small")
    sm = [_unpack_small(a, small_like) for a in sm]
    for nm in SMALL:
        res[nm] = [sm[j][nm] for j in range(4)]
    return (loss, grad_x, *[res[nm][0] for nm in WEIGHTS], *[res[nm][1] for nm in WEIGHTS],
            *[res[nm][2] for nm in WEIGHTS], *[res[nm][3] for nm in WEIGHTS])
```

```python
import functools

import jax
import jax.numpy as jnp
from jax import lax
from jax.experimental import pallas as pl
from jax.experimental.pallas import tpu as pltpu

F32 = jnp.float32
BF16 = jnp.bfloat16
EPS = 1e-6
N_DEV = 8
MESH = pl.DeviceIdType.MESH
HIGHEST = lax.Precision.HIGHEST
VMEM_LIMIT = 56 * 1024 * 1024

FOX_HEADS, FOX_DH = 8, 64
GDN_HEADS, GDN_DH = 4, 128
CHUNK = 64
CONV_W = 4

ADAM_LR, ADAM_B1, ADAM_B2, ADAM_EPS, ADAM_WD, ADAM_STEP = 0.001, 0.9, 0.999, 1e-08, 0.01, 10


def _cp(*sem):
    return pltpu.CompilerParams(dimension_semantics=sem, vmem_limit_bytes=VMEM_LIMIT)


def _dot(a, b):
    return jnp.dot(a, b, preferred_element_type=F32)


def _dot_nt(a, b):
    return lax.dot_general(a, b, (((1,), (1,)), ((), ())), preferred_element_type=F32)


def _dot_tn(a, b):
    return lax.dot_general(a, b, (((0,), (0,)), ((), ())), preferred_element_type=F32)


def _rstd(xf):
    return lax.rsqrt(jnp.mean(xf * xf, axis=-1, keepdims=True) + EPS)


def _rms_bwd(xf, r, dyn):
    return r * dyn - xf * (r * r * r) * jnp.mean(dyn * xf, axis=-1, keepdims=True)


def ffn_fwd(x, nw, w_in, w_out, tm=512):
    t, d = x.shape
    nj, fb = w_out.shape[0], w_out.shape[1]

    def body(x_ref, nw_ref, wi_ref, wo_ref, o_ref, xn_ref, acc_ref):
        j = pl.program_id(1)

        @pl.when(j == 0)
        def _():
            xf = x_ref[...]
            xn_ref[...] = (xf * _rstd(xf) * nw_ref[...]).astype(BF16)
            acc_ref[...] = jnp.zeros_like(acc_ref)

        xn = xn_ref[...]
        g = _dot(xn, wi_ref[0])
        u = _dot(xn, wi_ref[1])
        h = (g * jax.nn.sigmoid(g) * u).astype(BF16)
        acc_ref[...] += _dot(h, wo_ref[...])

        @pl.when(j == nj - 1)
        def _():
            o_ref[...] = x_ref[...] + 0.5 * acc_ref[...]

    return pl.pallas_call(
        body, grid=(t // tm, nj),
        in_specs=[pl.BlockSpec((tm, d), lambda i, j: (i, 0)),
                  pl.BlockSpec((1, d), lambda i, j: (0, 0)),
                  pl.BlockSpec((2, None, d, fb), lambda i, j: (0, j, 0, 0)),
                  pl.BlockSpec((None, fb, d), lambda i, j: (j, 0, 0))],
        out_specs=pl.BlockSpec((tm, d), lambda i, j: (i, 0)),
        out_shape=jax.ShapeDtypeStruct((t, d), F32),
        scratch_shapes=[pltpu.VMEM((tm, d), BF16), pltpu.VMEM((tm, d), F32)],
        compiler_params=_cp("parallel", "arbitrary"), name="ffn_fwd")(x, nw, w_in, w_out)


def ffn_bwd(x, dy, nw, w_in, w_out, tm=512):
    t, d = x.shape
    nj, fb = w_out.shape[0], w_out.shape[1]

    def body(x_ref, dy_ref, nw_ref, wi_ref, wo_ref,
             dx_ref, dnw_ref, dgu_ref, h_ref, xn_ref, dyh_ref, acc_ref):
        i, j = pl.program_id(0), pl.program_id(1)

        @pl.when(j == 0)
        def _():
            xf = x_ref[...]
            xn_ref[...] = (xf * _rstd(xf) * nw_ref[...]).astype(BF16)
            dyh_ref[...] = (0.5 * dy_ref[...]).astype(BF16)
            acc_ref[...] = jnp.zeros_like(acc_ref)

        @pl.when((i == 0) & (j == 0))
        def _():
            dnw_ref[...] = jnp.zeros_like(dnw_ref)

        xn = xn_ref[...]
        g = _dot(xn, wi_ref[0])
        u = _dot(xn, wi_ref[1])
        sg = jax.nn.sigmoid(g)
        silu = g * sg
        dh = _dot_nt(dyh_ref[...], wo_ref[...])
        dg = (dh * u * (sg * (1.0 + g * (1.0 - sg)))).astype(BF16)
        du = (dh * silu).astype(BF16)
        dgu_ref[0] = dg
        dgu_ref[1] = du
        h_ref[...] = (silu * u).astype(BF16)
        acc_ref[...] += _dot_nt(dg, wi_ref[0]) + _dot_nt(du, wi_ref[1])

        @pl.when(j == nj - 1)
        def _():
            xf = x_ref[...]
            r = _rstd(xf)
            dxn = acc_ref[...]
            dnw_ref[...] += jnp.sum(dxn * xf * r, axis=0, keepdims=True)
            dx_ref[...] = _rms_bwd(xf, r, dxn * nw_ref[...]) + dy_ref[...]

    return pl.pallas_call(
        body, grid=(t // tm, nj),
        in_specs=[pl.BlockSpec((tm, d), lambda i, j: (i, 0)),
                  pl.BlockSpec((tm, d), lambda i, j: (i, 0)),
                  pl.BlockSpec((1, d), lambda i, j: (0, 0)),
                  pl.BlockSpec((2, None, d, fb), lambda i, j: (0, j, 0, 0)),
                  pl.BlockSpec((None, fb, d), lambda i, j: (j, 0, 0))],
        out_specs=[pl.BlockSpec((tm, d), lambda i, j: (i, 0)),
                   pl.BlockSpec((1, d), lambda i, j: (0, 0)),
                   pl.BlockSpec((2, None, tm, fb), lambda i, j: (0, j, i, 0)),
                   pl.BlockSpec((None, tm, fb), lambda i, j: (j, i, 0)),
                   pl.BlockSpec((tm, d), lambda i, j: (i, 0)),
                   pl.BlockSpec((tm, d), lambda i, j: (i, 0))],
        out_shape=[jax.ShapeDtypeStruct((t, d), F32),
                   jax.ShapeDtypeStruct((1, d), F32),
                   jax.ShapeDtypeStruct((2, nj, t, fb), BF16),
                   jax.ShapeDtypeStruct((nj, t, fb), BF16),
                   jax.ShapeDtypeStruct((t, d), BF16),
                   jax.ShapeDtypeStruct((t, d), BF16)],
        scratch_shapes=[pltpu.VMEM((tm, d), F32)],
        compiler_params=_cp("arbitrary", "arbitrary"), name="ffn_bwd")(x, dy, nw, w_in, w_out)


def _wgrad_call(a, b, a_spec, b_spec, out_shape, out_spec, grid, name):
    last = len(grid) - 1

    def body(a_ref, b_ref, o_ref):
        @pl.when(pl.program_id(last) == 0)
        def _():
            o_ref[...] = jnp.zeros_like(o_ref)

        o_ref[...] += _dot_tn(a_ref[...], b_ref[...])

    sem = ("parallel",) * last + ("arbitrary",)
    return pl.pallas_call(body, grid=grid, in_specs=[a_spec, b_spec], out_specs=out_spec,
                          out_shape=jax.ShapeDtypeStruct(out_shape, F32),
                          compiler_params=_cp(*sem), name=name)(a, b)


def wgrad_ffn_in(xn, dgu, tm=512):
    t, d = xn.shape
    _, nj, _, fb = dgu.shape
    return _wgrad_call(xn, dgu,
                       pl.BlockSpec((tm, d), lambda p, j, k: (k, 0)),
                       pl.BlockSpec((None, None, tm, fb), lambda p, j, k: (p, j, k, 0)),
                       (2, nj, d, fb), pl.BlockSpec((None, None, d, fb), lambda p, j, k: (p, j, 0, 0)),
                       (2, nj, t // tm), "wgrad_ffn_in")


def wgrad_ffn_out(h, dyh, tm=512):
    nj, t, fb = h.shape
    d = dyh.shape[1]
    return _wgrad_call(h, dyh,
                       pl.BlockSpec((None, tm, fb), lambda j, k: (j, k, 0)),
                       pl.BlockSpec((tm, d), lambda j, k: (k, 0)),
                       (nj, fb, d), pl.BlockSpec((None, fb, d), lambda j, k: (j, 0, 0)),
                       (nj, t // tm), "wgrad_ffn_out")


def wgrad_2d(a, b, tk, name, tm=512):
    t, k = a.shape
    n = b.shape[1]
    return _wgrad_call(a, b,
                       pl.BlockSpec((tm, tk), lambda c, s: (s, c)),
                       pl.BlockSpec((tm, n), lambda c, s: (s, 0)),
                       (k, n), pl.BlockSpec((tk, n), lambda c, s: (c, 0)),
                       (k // tk, t // tm), name)


N_BIG = 7 * 512
N_PROJ = N_BIG + 128
COL_SMALL = N_BIG // 128


def inproj_fwd(x, nw, w, tm=256):
    t, d = x.shape
    n = w.shape[1]

    def body(x_ref, nw_ref, w_ref, p_ref, hn_ref):
        xf = x_ref[...]
        hn = (xf * _rstd(xf) * nw_ref[...]).astype(BF16)
        hn_ref[...] = hn
        p_ref[...] = _dot(hn, w_ref[...])

    return pl.pallas_call(
        body, grid=(t // tm,),
        in_specs=[pl.BlockSpec((tm, d), lambda i: (i, 0)), pl.BlockSpec((1, d), lambda i: (0, 0)),
                  pl.BlockSpec((d, n), lambda i: (0, 0))],
        out_specs=[pl.BlockSpec((tm, n), lambda i: (i, 0)), pl.BlockSpec((tm, d), lambda i: (i, 0))],
        out_shape=[jax.ShapeDtypeStruct((t, n), F32), jax.ShapeDtypeStruct((t, d), BF16)],
        compiler_params=_cp("parallel"), name="inproj_fwd")(x, nw, w)


def inproj_bwd(x, dres, nw, w, dparts, tm=256):
    t, d = x.shape
    n = w.shape[1]
    widths = [p.shape[1] for p in dparts]
    assert sum(widths) == n

    def body(x_ref, dres_ref, nw_ref, w_ref, *rest):
        part_refs, (dx_ref, dnw_ref, dp_ref) = rest[:len(widths)], rest[len(widths):]

        @pl.when(pl.program_id(0) == 0)
        def _():
            dnw_ref[...] = jnp.zeros_like(dnw_ref)

        dp = jnp.concatenate([r[...].astype(BF16) for r in part_refs], axis=1)
        dp_ref[...] = dp
        dhn = _dot_nt(dp, w_ref[...])
        xf = x_ref[...]
        r = _rstd(xf)
        dnw_ref[...] += jnp.sum(dhn * xf * r, axis=0, keepdims=True)
        dx_ref[...] = _rms_bwd(xf, r, dhn * nw_ref[...]) + dres_ref[...]

    return pl.pallas_call(
        body, grid=(t // tm,),
        in_specs=[pl.BlockSpec((tm, d), lambda i: (i, 0)), pl.BlockSpec((tm, d), lambda i: (i, 0)),
                  pl.BlockSpec((1, d), lambda i: (0, 0)), pl.BlockSpec((d, n), lambda i: (0, 0))]
                 + [pl.BlockSpec((tm, wd), lambda i: (i, 0)) for wd in widths],
        out_specs=[pl.BlockSpec((tm, d), lambda i: (i, 0)), pl.BlockSpec((1, d), lambda i: (0, 0)),
                   pl.BlockSpec((tm, n), lambda i: (i, 0))],
        out_shape=[jax.ShapeDtypeStruct((t, d), F32), jax.ShapeDtypeStruct((1, d), F32),
                   jax.ShapeDtypeStruct((t, n), BF16)],
        compiler_params=_cp("arbitrary"), name="inproj_bwd")(x, dres, nw, w, *dparts)


def outproj_fwd(x, yf, yg, w, tm=512):
    t, d = x.shape
    hw = yf.shape[1]

    def body(x_ref, yf_ref, yg_ref, w_ref, o_ref, y_ref):
        y = jnp.concatenate([yf_ref[...], yg_ref[...]], axis=1).astype(BF16)
        y_ref[...] = y
        o_ref[...] = x_ref[...] + _dot(y, w_ref[...])

    return pl.pallas_call(
        body, grid=(t // tm,),
        in_specs=[pl.BlockSpec((tm, d), lambda i: (i, 0)), pl.BlockSpec((tm, hw), lambda i: (i, 0)),
                  pl.BlockSpec((tm, hw), lambda i: (i, 0)), pl.BlockSpec((2 * hw, d), lambda i: (0, 0))],
        out_specs=[pl.BlockSpec((tm, d), lambda i: (i, 0)), pl.BlockSpec((tm, 2 * hw), lambda i: (i, 0))],
        out_shape=[jax.ShapeDtypeStruct((t, d), F32), jax.ShapeDtypeStruct((t, 2 * hw), BF16)],
        compiler_params=_cp("parallel"), name="outproj_fwd")(x, yf, yg, w)


def outproj_bwd(dy, w, tm=512):
    t, d = dy.shape
    hw = w.shape[0] // 2

    def body(dy_ref, w_ref, df_ref, dg_ref, dyb_ref):
        dyb = dy_ref[...].astype(BF16)
        dyb_ref[...] = dyb
        dyy = _dot_nt(dyb, w_ref[...])
        df_ref[...] = dyy[:, :hw]
        dg_ref[...] = dyy[:, hw:]

    return pl.pallas_call(
        body, grid=(t // tm,),
        in_specs=[pl.BlockSpec((tm, d), lambda i: (i, 0)), pl.BlockSpec((2 * hw, d), lambda i: (0, 0))],
        out_specs=[pl.BlockSpec((tm, hw), lambda i: (i, 0)), pl.BlockSpec((tm, hw), lambda i: (i, 0)),
                   pl.BlockSpec((tm, d), lambda i: (i, 0))],
        out_shape=[jax.ShapeDtypeStruct((t, hw), F32), jax.ShapeDtypeStruct((t, hw), F32),
                   jax.ShapeDtypeStruct((t, d), BF16)],
        compiler_params=_cp("parallel"), name="outproj_bwd")(dy, w)


def _lane(shape):
    return lax.broadcasted_iota(jnp.int32, shape, 1)


def _row(shape):
    return lax.broadcasted_iota(jnp.int32, shape, 0)


def _gate_terms(val, gp_ref):
    z = val + gp_ref[0:1, :]
    sp = jnp.log(1.0 + jnp.exp(-jnp.abs(z)))
    return z, sp


def gates_fwd(proj, gp, seq, ts=512):
    t = proj.shape[0]
    nb, ns = t // seq, seq // ts

    def body(v_ref, gp_ref, o_ref, carry_ref):
        @pl.when(pl.program_id(1) == 0)
        def _():
            carry_ref[...] = jnp.zeros_like(carry_ref)

        z, sp = _gate_terms(v_ref[...], gp_ref)
        logsig = jnp.minimum(z, 0.0) - sp
        tri = (_row((ts, ts)) >= _lane((ts, ts))).astype(F32)
        cum = jnp.dot(tri, logsig, precision=HIGHEST, preferred_element_type=F32) + carry_ref[0:1, :]
        carry_ref[0:1, :] = cum[ts - 1:ts, :]
        g = -jnp.exp(gp_ref[1:2, :]) * (jnp.maximum(z, 0.0) + sp)
        beta = jax.nn.sigmoid(z)
        lane = _lane((ts, 128))
        o_ref[...] = jnp.where(lane < 8, cum, jnp.where(lane < 12, g, jnp.where(lane < 16, beta, 0.0)))

    return pl.pallas_call(
        body, grid=(nb, ns),
        in_specs=[pl.BlockSpec((ts, 128), lambda b, s: (b * ns + s, COL_SMALL)),
                  pl.BlockSpec((8, 128), lambda b, s: (0, 0))],
        out_specs=pl.BlockSpec((ts, 128), lambda b, s: (b * ns + s, 0)),
        out_shape=jax.ShapeDtypeStruct((t, 128), F32),
        scratch_shapes=[pltpu.VMEM((8, 128), F32)],
        compiler_params=_cp("parallel", "arbitrary"), name="gates_fwd")(proj, gp)


def gates_bwd(proj, gp, dga, dgb, seq, ts=512):
    t = proj.shape[0]
    nb, ns = t // seq, seq // ts

    def body(v_ref, gp_ref, da_ref, db_ref, ds_ref, dgp_ref, carry_ref):
        @pl.when(pl.program_id(1) == 0)
        def _():
            carry_ref[...] = jnp.zeros_like(carry_ref)

        @pl.when((pl.program_id(0) == 0) & (pl.program_id(1) == 0))
        def _():
            dgp_ref[...] = jnp.zeros_like(dgp_ref)

        lane = _lane((ts, 128))
        dgate = jnp.where(lane < 8, da_ref[...], jnp.where(lane < 16, db_ref[...], 0.0))
        z, sp = _gate_terms(v_ref[...], gp_ref)
        triu = (_row((ts, ts)) <= _lane((ts, ts))).astype(F32)
        dlog = jnp.dot(triu, dgate, precision=HIGHEST, preferred_element_type=F32) + carry_ref[0:1, :]
        carry_ref[0:1, :] = dlog[0:1, :]
        sig = jax.nn.sigmoid(z)
        nea = -jnp.exp(gp_ref[1:2, :])
        g = nea * (jnp.maximum(z, 0.0) + sp)
        dz = jnp.where(lane < 8, dlog * (1.0 - sig),
                       jnp.where(lane < 12, dgate * nea * sig, dgate * sig * (1.0 - sig)))
        dz = jnp.where(lane < 16, dz, 0.0)
        ds_ref[...] = dz
        dgp_ref[0:1, :] += jnp.where(lane[0:1] < 12, jnp.sum(dz, axis=0, keepdims=True), 0.0)
        dgp_ref[1:2, :] += jnp.where((lane[0:1] >= 8) & (lane[0:1] < 12), jnp.sum(dgate * g, axis=0, keepdims=True), 0.0)

    rev = lambda b, s: (b * ns + (ns - 1 - s), 0)
    return pl.pallas_call(
        body, grid=(nb, ns),
        in_specs=[pl.BlockSpec((ts, 128), lambda b, s: (b * ns + (ns - 1 - s), COL_SMALL)),
                  pl.BlockSpec((8, 128), lambda b, s: (0, 0)),
                  pl.BlockSpec((ts, 128), rev), pl.BlockSpec((ts, 128), rev)],
        out_specs=[pl.BlockSpec((ts, 128), rev), pl.BlockSpec((8, 128), lambda b, s: (0, 0))],
        out_shape=[jax.ShapeDtypeStruct((t, 128), F32), jax.ShapeDtypeStruct((8, 128), F32)],
        scratch_shapes=[pltpu.VMEM((8, 128), F32)],
        compiler_params=_cp("arbitrary", "arbitrary"), name="gates_bwd")(proj, gp, dga, dgb)


NEG = -1e30
ATTN_TQ_FWD = 1024
ATTN_TQ_BWD = 512


def _pick_lane(tile, idx):
    return jnp.sum(jnp.where(_lane(tile.shape) == idx, tile, 0.0), axis=1, keepdims=True)


def _col_to_row(col, n):
    return jnp.sum(jnp.where(_row((n, n)) == _lane((n, n)), col, 0.0), axis=0, keepdims=True)


def _row_to_col(row, n):
    return jnp.sum(jnp.where(_row((n, n)) == _lane((n, n)), row, 0.0), axis=1, keepdims=True)


def _rows(i, n):
    return pl.ds(pl.multiple_of(i * n, n), n)


def _once(shape, index_map):
    return pl.BlockSpec(shape, index_map, pipeline_mode=pl.Buffered(1))


def attn_fwd(proj, gates, qw, kw, seq, tq=256):
    t = proj.shape[0]
    nb, nq, dh = t // seq, seq // tq, FOX_DH
    scale = dh ** -0.5

    def body(q_ref, k_ref, v_ref, g_ref, qw_ref, kw_ref, y_ref, lse_ref, qs, ks, vs, ccol, crow):
        p = pl.program_id(1)
        heads = range(2)

        def prep(i, _):
            r = _rows(i, tq)
            for hh in heads:
                lanes = slice(hh * dh, (hh + 1) * dh)
                qf, kf = q_ref[r, lanes], k_ref[r, lanes]
                qs[hh, r, :] = (qf * _rstd(qf) * qw_ref[...] * scale).astype(BF16)
                ks[hh, r, :] = (kf * _rstd(kf) * kw_ref[...]).astype(BF16)
                vs[hh, r, :] = v_ref[r, lanes].astype(BF16)
                cc = _pick_lane(g_ref[r, :], 2 * p + hh)
                ccol[hh, r, :] = cc
                crow[hh * nq + i] = _col_to_row(cc, tq)
            return 0

        lax.fori_loop(0, nq, prep, 0)

        def q_tile(i, _):
            r = _rows(i, tq)
            qt = [qs[hh, r, :] for hh in heads]
            cc = [ccol[hh, r, :] for hh in heads]

            def kv_step(j, carry, masked):
                kr = _rows(j, tq)
                out = []
                for hh in heads:
                    m, l, acc = carry[hh]
                    s = _dot_nt(qt[hh], ks[hh, kr, :]) + (cc[hh] - crow[hh * nq + j])
                    if masked:
                        s = jnp.where(_row((tq, tq)) >= _lane((tq, tq)), s, NEG)
                    m_new = jnp.maximum(m, jnp.max(s, axis=1, keepdims=True))
                    pe = jnp.exp(s - m_new)
                    a = jnp.exp(m - m_new)
                    out.append((m_new, a * l + jnp.sum(pe, axis=1, keepdims=True),
                                a * acc + _dot(pe.astype(BF16), vs[hh, kr, :])))
                return tuple(out)

            one = (jnp.full((tq, 1), NEG, F32), jnp.zeros((tq, 1), F32), jnp.zeros((tq, dh), F32))
            carry = lax.fori_loop(0, i, lambda j, c: kv_step(j, c, False), (one, one))
            carry = kv_step(i, carry, True)
            for hh in heads:
                m, l, acc = carry[hh]
                lanes = slice(hh * dh, (hh + 1) * dh)
                y_ref[r, lanes] = acc / l
                lse_ref[r, lanes] = jnp.broadcast_to(m + jnp.log(l), (tq, dh))
            return 0

        lax.fori_loop(0, nq, q_tile, 0)

    blk = lambda off: _once((seq, 128), lambda b, p: (b, off + p))
    return pl.pallas_call(
        body, grid=(nb, 4),
        in_specs=[blk(0), blk(4), blk(8), _once((seq, 128), lambda b, p: (b, 0)),
                  pl.BlockSpec((1, dh), lambda b, p: (0, 0)), pl.BlockSpec((1, dh), lambda b, p: (0, 0))],
        out_specs=[pl.BlockSpec((seq, 128), lambda b, p: (b, p)), pl.BlockSpec((seq, 128), lambda b, p: (b, p))],
        out_shape=[jax.ShapeDtypeStruct((t, 512), F32), jax.ShapeDtypeStruct((t, 512), F32)],
        scratch_shapes=[pltpu.VMEM((2, seq, dh), BF16), pltpu.VMEM((2, seq, dh), BF16), pltpu.VMEM((2, seq, dh), BF16),
                        pltpu.VMEM((2, seq, 1), F32), pltpu.VMEM((2 * nq, 1, tq), F32)],
        compiler_params=_cp("parallel", "arbitrary"), name="attn_fwd")(proj, proj, proj, gates, qw, kw)


def attn_bwd(proj, gates, qw, kw, y, lse, dy, seq, tq=256):
    t = proj.shape[0]
    nb, nq, dh = t // seq, seq // tq, FOX_DH
    scale = dh ** -0.5

    def body(q_ref, k_ref, v_ref, g_ref, qw_ref, kw_ref, y_ref, lse_ref, dy_ref,
             dq_ref, dk_ref, dv_ref, dg_ref, dqw_ref, dkw_ref,
             qs, ks, vs, dos, cols, crow, dqa, dka):
        b, p = pl.program_id(0), pl.program_id(1)

        @pl.when((b == 0) & (p == 0))
        def _():
            dqw_ref[...] = jnp.zeros_like(dqw_ref)
            dkw_ref[...] = jnp.zeros_like(dkw_ref)

        @pl.when(p == 0)
        def _():
            dg_ref[...] = jnp.zeros_like(dg_ref)

        heads = range(2)
        hl = lambda hh: slice(hh * dh, (hh + 1) * dh)

        def prep(i, _):
            r = _rows(i, tq)
            for hh in heads:
                lanes = hl(hh)
                qf, kf = q_ref[r, lanes], k_ref[r, lanes]
                qs[hh, r, :] = (qf * _rstd(qf) * qw_ref[...] * scale).astype(BF16)
                ks[hh, r, :] = (kf * _rstd(kf) * kw_ref[...]).astype(BF16)
                vs[hh, r, :] = v_ref[r, lanes].astype(BF16)
                dyf = dy_ref[r, lanes]
                dos[hh, r, :] = dyf.astype(BF16)
                cc = _pick_lane(g_ref[r, :], 2 * p + hh)
                crow[hh * nq + i] = _col_to_row(cc, tq)
                delta = jnp.sum(dyf * y_ref[r, lanes], axis=1, keepdims=True)
                lane = _lane((tq, 128))
                cols[hh, r, :] = jnp.where(lane == 0, cc, jnp.where(lane == 1, lse_ref[r, hh * dh:hh * dh + 1],
                                                                     jnp.where(lane == 2, delta, 0.0)))
                dqa[hh, r, :] = jnp.zeros((tq, dh), F32)
            return 0

        lax.fori_loop(0, nq, prep, 0)

        def kv_tile(j, _):
            kr = _rows(j, tq)
            kt = [ks[hh, kr, :] for hh in heads]
            vt = [vs[hh, kr, :] for hh in heads]
            cr = [crow[hh * nq + j] for hh in heads]

            def q_step(i, carry, masked):
                r = _rows(i, tq)
                out = []
                for hh in heads:
                    dk, dv, dcr = carry[hh]
                    qt, dot, cl = qs[hh, r, :], dos[hh, r, :], cols[hh, r, :]
                    s = _dot_nt(qt, kt[hh]) + (cl[:, 0:1] - cr[hh])
                    if masked:
                        s = jnp.where(_row((tq, tq)) >= _lane((tq, tq)), s, NEG)
                    pe = jnp.exp(s - cl[:, 1:2])
                    ds = pe * (_dot_nt(dot, vt[hh]) - cl[:, 2:3])
                    dsb = ds.astype(BF16)
                    dqa[hh, r, :] += _dot(dsb, kt[hh])
                    cols[hh, r, :] = cl + jnp.where(_lane((tq, 128)) == 3, jnp.sum(ds, axis=1, keepdims=True), 0.0)
                    out.append((dk + _dot_tn(dsb, qt), dv + _dot_tn(pe.astype(BF16), dot),
                                dcr - jnp.sum(ds, axis=0, keepdims=True)))
                return tuple(out)

            one = (jnp.zeros((tq, dh), F32), jnp.zeros((tq, dh), F32), jnp.zeros((1, tq), F32))
            carry = q_step(j, (one, one), True)
            carry = lax.fori_loop(j + 1, nq, lambda i, c: q_step(i, c, False), carry)
            for hh in heads:
                dk, dv, dcr = carry[hh]
                dka[hh, kr, :] = dk
                dv_ref[kr, hl(hh)] = dv
                dg_ref[kr, :] = jnp.where(_lane((tq, 128)) == 2 * p + hh, _row_to_col(dcr, tq), dg_ref[kr, :])
            return 0

        lax.fori_loop(0, nq, kv_tile, 0)

        def post(i, _):
            r = _rows(i, tq)
            for hh in heads:
                lanes = hl(hh)
                qf, kf = q_ref[r, lanes], k_ref[r, lanes]
                rq, rk = _rstd(qf), _rstd(kf)
                dqn, dkn = dqa[hh, r, :] * scale, dka[hh, r, :]
                dqw_ref[...] += jnp.sum(dqn * qf * rq, axis=0, keepdims=True)
                dkw_ref[...] += jnp.sum(dkn * kf * rk, axis=0, keepdims=True)
                dq_ref[r, lanes] = _rms_bwd(qf, rq, dqn * qw_ref[...])
                dk_ref[r, lanes] = _rms_bwd(kf, rk, dkn * kw_ref[...])
                dg_ref[r, :] += jnp.where(_lane((tq, 128)) == 2 * p + hh, cols[hh, r, 3:4], 0.0)
            return 0

        lax.fori_loop(0, nq, post, 0)

    blk = lambda off: _once((seq, 128), lambda b, p: (b, off + p))
    own = lambda: _once((seq, 128), lambda b, p: (b, p))
    vec = lambda: pl.BlockSpec((1, dh), lambda b, p: (0, 0))
    return pl.pallas_call(
        body, grid=(nb, 4),
        in_specs=[blk(0), blk(4), blk(8), _once((seq, 128), lambda b, p: (b, 0)), vec(), vec(), own(), own(), own()],
        out_specs=[own(), own(), own(), _once((seq, 128), lambda b, p: (b, 0)), vec(), vec()],
        out_shape=[jax.ShapeDtypeStruct((t, 512), F32)] * 3
                  + [jax.ShapeDtypeStruct((t, 128), F32), jax.ShapeDtypeStruct((1, dh), F32), jax.ShapeDtypeStruct((1, dh), F32)],
        scratch_shapes=[pltpu.VMEM((2, seq, dh), BF16)] * 4
                       + [pltpu.VMEM((2, seq, 128), F32), pltpu.VMEM((2 * nq, 1, tq), F32),
                          pltpu.VMEM((2, seq, dh), F32), pltpu.VMEM((2, seq, dh), F32)],
        compiler_params=_cp("arbitrary", "arbitrary"), name="attn_bwd")(proj, proj, proj, gates, qw, kw, y, lse, dy)


def _silu_grad(c, sg):
    return sg * (1.0 + c * (1.0 - sg))


def _conv(x, w, n):
    row = _row(x.shape)
    c = x * w[CONV_W - 1:CONV_W, :]
    for k in range(CONV_W - 1):
        sh = CONV_W - 1 - k
        c = c + w[k:k + 1, :] * jnp.where(row >= sh, pltpu.roll(x, sh, 0), 0.0)
    return c


def gdn_pre_fwd(proj, cw, seq):
    t = proj.shape[0]
    nb = t // seq
    scale = GDN_DH ** -0.5

    def body(xq_ref, xk_ref, xv_ref, wq_ref, wk_ref, wv_ref, q_ref, k_ref, v_ref):
        def act(x_ref, w_ref):
            c = _conv(x_ref[...], w_ref[...], seq)
            return c * jax.nn.sigmoid(c)

        aq, ak = act(xq_ref, wq_ref), act(xk_ref, wk_ref)
        q_ref[...] = aq * lax.rsqrt(jnp.sum(aq * aq, axis=1, keepdims=True) + EPS) * scale
        k_ref[...] = ak * lax.rsqrt(jnp.sum(ak * ak, axis=1, keepdims=True) + EPS)
        v_ref[...] = act(xv_ref, wv_ref)

    xb = lambda off: pl.BlockSpec((seq, 128), lambda b, h: (b, off + h))
    wb = lambda off: pl.BlockSpec((CONV_W, 128), lambda b, h: (0, off + h))
    ob = lambda: pl.BlockSpec((seq, 128), lambda b, h: (b, h))
    return pl.pallas_call(
        body, grid=(nb, GDN_HEADS),
        in_specs=[xb(12), xb(16), xb(20), wb(0), wb(4), wb(8)],
        out_specs=[ob(), ob(), ob()],
        out_shape=[jax.ShapeDtypeStruct((t, 512), F32)] * 3,
        compiler_params=_cp("parallel", "parallel"), name="gdn_pre_fwd")(proj, proj, proj, cw, cw, cw)


def gdn_pre_bwd(proj, cw, dq, dk, dv, seq):
    t = proj.shape[0]
    nb = t // seq
    scale = GDN_DH ** -0.5

    def body(xq_ref, xk_ref, xv_ref, wq_ref, wk_ref, wv_ref, dq_ref, dk_ref, dv_ref,
             dxq_ref, dxk_ref, dxv_ref, dwq_ref, dwk_ref, dwv_ref):
        first = pl.program_id(1) == 0
        row = _row((seq, 128))

        def one(x_ref, w_ref, dy_ref, dx_ref, dw_ref, norm, sc):
            x, w = x_ref[...], w_ref[...]
            c = _conv(x, w, seq)
            sg = jax.nn.sigmoid(c)
            dy = dy_ref[...]
            if norm:
                a = c * sg
                rs = lax.rsqrt(jnp.sum(a * a, axis=1, keepdims=True) + EPS)
                dy = dy * sc
                da = rs * dy - a * (rs * rs * rs) * jnp.sum(dy * a, axis=1, keepdims=True)
            else:
                da = dy
            dc = da * _silu_grad(c, sg)
            dx = dc * w[CONV_W - 1:CONV_W, :]
            dws = [None] * CONV_W
            dws[CONV_W - 1] = jnp.sum(dc * x, axis=0, keepdims=True)
            for k in range(CONV_W - 1):
                sh = CONV_W - 1 - k
                dx = dx + w[k:k + 1, :] * jnp.where(row < seq - sh, pltpu.roll(dc, seq - sh, 0), 0.0)
                dws[k] = jnp.sum(dc * jnp.where(row >= sh, pltpu.roll(x, sh, 0), 0.0), axis=0, keepdims=True)
            dx_ref[...] = dx
            dwn = jnp.concatenate(dws, axis=0)

            @pl.when(first)
            def _():
                dw_ref[...] = dwn

            @pl.when(jnp.logical_not(first))
            def _():
                dw_ref[...] += dwn

        one(xq_ref, wq_ref, dq_ref, dxq_ref, dwq_ref, True, scale)
        one(xk_ref, wk_ref, dk_ref, dxk_ref, dwk_ref, True, 1.0)
        one(xv_ref, wv_ref, dv_ref, dxv_ref, dwv_ref, False, 1.0)

    xb = lambda off: pl.BlockSpec((seq, 128), lambda h, b: (b, off + h))
    wb = lambda off: pl.BlockSpec((CONV_W, 128), lambda h, b: (0, off + h))
    ob = lambda: pl.BlockSpec((seq, 128), lambda h, b: (b, h))
    return pl.pallas_call(
        body, grid=(GDN_HEADS, nb),
        in_specs=[xb(12), xb(16), xb(20), wb(0), wb(4), wb(8), ob(), ob(), ob()],
        out_specs=[ob(), ob(), ob()] + [pl.BlockSpec((CONV_W, 128), lambda h, b: (0, h))] * 3,
        out_shape=[jax.ShapeDtypeStruct((t, 512), F32)] * 3 + [jax.ShapeDtypeStruct((CONV_W, 512), F32)] * 3,
        compiler_params=_cp("parallel", "arbitrary"), name="gdn_pre_bwd")(proj, proj, proj, cw, cw, cw, dq, dk, dv)


def _b16(x):
    return x.astype(BF16)


@jax.custom_vjp
def _mm(a, b):
    return _dot(_b16(a), _b16(b))


_mm.defvjp(lambda a, b: (_mm(a, b), (a, b)),
           lambda res, g: (_dot_nt(_b16(g), _b16(res[1])), _dot_tn(_b16(res[0]), _b16(g))))


@jax.custom_vjp
def _mm_nt(a, b):
    return _dot_nt(_b16(a), _b16(b))


_mm_nt.defvjp(lambda a, b: (_mm_nt(a, b), (a, b)),
              lambda res, g: (_dot(_b16(g), _b16(res[1])), _dot_tn(_b16(g), _b16(res[0]))))


@jax.custom_vjp
def _mm_tn(a, b):
    return _dot_tn(_b16(a), _b16(b))


_mm_tn.defvjp(lambda a, b: (_mm_tn(a, b), (a, b)),
              lambda res, g: (_dot_nt(_b16(res[1]), _b16(g)), _dot(_b16(res[0]), _b16(g))))


def _dot32(a, b, dims=(((1,), (0,)), ((), ()))):
    def split(x):
        hi = x.astype(BF16)
        return hi, (x - hi.astype(F32)).astype(BF16)

    (ah, al), (bh, bl) = split(a), split(b)
    d = lambda x, y: lax.dot_general(x, y, dims, preferred_element_type=F32)
    return d(ah, bh) + (d(ah, bl) + d(al, bh))


def _inv_fwd_many(mats):
    n = mats[0].shape[0]
    eye = (_row((n, n)) == _lane((n, n))).astype(F32)
    invs, pws = [eye - a for a in mats], list(mats)
    for _ in range(n.bit_length() - 2):
        pws = [_dot32(pw, pw) for pw in pws]
        invs = [inv + _dot32(inv, pw) for inv, pw in zip(invs, pws)]
    return invs


@jax.custom_vjp
def _inv_saved(a, inv):
    return inv


def _inv_saved_bwd(inv, g):
    tg = _dot32(inv, g, (((0,), (0,)), ((), ())))
    return -_dot32(tg, inv, (((1,), (1,)), ((), ()))), jnp.zeros_like(inv)


_inv_saved.defvjp(lambda a, inv: (inv, inv), _inv_saved_bwd)


def _gdn_decay(gcol):
    c = CHUNK
    ri, ci = _row((c, c)), _lane((c, c))
    incl, eye = ri >= ci, ri == ci
    grow = jnp.sum(jnp.where(eye, gcol, 0.0), axis=0, keepdims=True)
    gc = jnp.sum(jnp.where(incl, grow, 0.0), axis=1, keepdims=True)
    gcr = jnp.sum(jnp.where(eye, gc, 0.0), axis=0, keepdims=True)
    gl = jnp.sum(jnp.where(_row((c, 1)) == c - 1, gc, 0.0), axis=0, keepdims=True)
    return gc, gl, jnp.exp(jnp.where(incl, gc - gcr, NEG))


def _gdn_a(k, bcol, decay):
    c = CHUNK
    return jnp.where(_row((c, c)) > _lane((c, c)), _mm_nt(k * bcol, k) * decay, 0.0)


def _gdn_chunk(q, k, v, gcol, bcol, state, gg, nw, inv_saved):
    c = CHUNK
    incl = _row((c, c)) >= _lane((c, c))
    gc, gl, decay = _gdn_decay(gcol)
    kb, vb = k * bcol, v * bcol
    inv = _inv_saved(_gdn_a(k, bcol, decay), inv_saved)
    eg = jnp.exp(gc)
    u = _mm(inv, vb)
    w = _mm(inv, kb * eg)
    pm = jnp.where(incl, _mm_nt(q, k) * decay, 0.0)
    kd = k * jnp.exp(gl - gc)
    qd = q * eg
    v_new = u - _mm(w, state)
    o = _mm(qd, state) + _mm(pm, v_new)
    state_new = state * jnp.exp(gl) + _mm_tn(kd, v_new)
    y = o * _rstd(o) * nw * (gg * jax.nn.sigmoid(gg))
    return y, state_new


_gdn_chunks = jax.vmap(_gdn_chunk, in_axes=(0, 0, 0, 0, 0, 0, 0, None, 0))


def _gdn_chain_inputs(chains, p, r, c, q_ref, k_ref, v_ref, g_ref, gg_ref, inv_ref):
    cols = {nm: [] for nm in ("q", "k", "v", "g", "b", "gg", "inv")}
    for b, hh in chains:
        h = 2 * p + hh
        ln = slice(hh * 128, (hh + 1) * 128)
        gt = g_ref[b, r, :]
        cols["q"].append(q_ref[b, r, ln])
        cols["k"].append(k_ref[b, r, ln])
        cols["v"].append(v_ref[b, r, ln])
        cols["g"].append(_pick_lane(gt, 8 + h))
        cols["b"].append(_pick_lane(gt, 12 + h))
        cols["gg"].append(gg_ref[b, r, ln])
        cols["inv"].append(inv_ref[b, hh, c])
    return [jnp.stack(cols[nm]) for nm in ("q", "k", "v", "g", "b", "gg", "inv")]


GDN_CB = 8


def gdn_inv(k, gates, seq):
    t = k.shape[0]
    nb, nc = t // seq, seq // CHUNK
    rb = GDN_CB * CHUNK
    nsb = seq // rb

    def body(k_ref, g_ref, o_ref):
        h = pl.program_id(1)
        mats = []
        for c in range(GDN_CB):
            r = slice(c * CHUNK, (c + 1) * CHUNK)
            gt = g_ref[r, :]
            _, _, decay = _gdn_decay(_pick_lane(gt, 8 + h))
            mats.append(_gdn_a(k_ref[r, :], _pick_lane(gt, 12 + h), decay))
        for c, inv in enumerate(_inv_fwd_many(mats)):
            o_ref[c] = inv

    return pl.pallas_call(
        body, grid=(nb, GDN_HEADS, nsb),
        in_specs=[pl.BlockSpec((rb, 128), lambda b, h, s: (b * nsb + s, h)),
                  pl.BlockSpec((rb, 128), lambda b, h, s: (b * nsb + s, 0))],
        out_specs=pl.BlockSpec((None, None, GDN_CB, CHUNK, CHUNK), lambda b, h, s: (b, h, s, 0, 0)),
        out_shape=jax.ShapeDtypeStruct((nb, GDN_HEADS, nc, CHUNK, CHUNK), F32),
        compiler_params=_cp("parallel", "parallel", "parallel"), name="gdn_inv")(k, gates)


def _gdn_specs(nb, nsb, rev):
    blk = (lambda s: nsb - 1 - s) if rev else (lambda s: s)
    rb = GDN_CB * CHUNK
    pair = lambda off=0: pl.BlockSpec((nb, rb, 256), lambda s, p: (0, blk(s), off + p))
    gate = lambda: pl.BlockSpec((nb, rb, 128), lambda s, p: (0, blk(s), 0))
    mats = lambda n: pl.BlockSpec((nb, 2, GDN_CB, n, n), lambda s, p: (0, p, blk(s), 0, 0))
    return pair, gate, mats


def gdn_fwd(q, k, v, gates, proj, nw, inv, seq):
    t = q.shape[0]
    nb, nc = t // seq, seq // CHUNK
    nsb = nc // GDN_CB
    chains = [(b, hh) for b in range(nb) for hh in range(2)]
    nch = len(chains)
    pair, gate, mats = _gdn_specs(nb, nsb, False)

    def body(q_ref, k_ref, v_ref, g_ref, gg_ref, inv_ref, nw_ref, y_ref, st_ref, carry):
        s, p = pl.program_id(0), pl.program_id(1)

        @pl.when(s == 0)
        def _():
            for ci in range(nch):
                carry[p * nch + ci] = jnp.zeros((GDN_DH, GDN_DH), F32)

        def step(c, states):
            r = _rows(c, CHUNK)
            for ci, (b, hh) in enumerate(chains):
                st_ref[b, hh, c] = states[ci]
            ins = _gdn_chain_inputs(chains, p, r, c, q_ref, k_ref, v_ref, g_ref, gg_ref, inv_ref)
            y, states = _gdn_chunks(*ins[:5], states, ins[5], nw_ref[...], ins[6])
            for ci, (b, hh) in enumerate(chains):
                y_ref[b, r, hh * 128:(hh + 1) * 128] = y[ci]
            return states

        states = lax.fori_loop(0, GDN_CB, step, jnp.stack([carry[p * nch + ci] for ci in range(nch)]))
        for ci in range(nch):
            carry[p * nch + ci] = states[ci]

    v3 = lambda a: a.reshape(nb, seq, a.shape[1])
    y, st = pl.pallas_call(
        body, grid=(nsb, 2),
        in_specs=[pair(), pair(), pair(), gate(), pair(12), mats(CHUNK), pl.BlockSpec((1, 128), lambda s, p: (0, 0))],
        out_specs=[pair(), mats(GDN_DH)],
        out_shape=[jax.ShapeDtypeStruct((nb, seq, 512), F32),
                   jax.ShapeDtypeStruct((nb, GDN_HEADS, nc, GDN_DH, GDN_DH), F32)],
        scratch_shapes=[pltpu.VMEM((2 * nch, GDN_DH, GDN_DH), F32)],
        compiler_params=_cp("arbitrary", "arbitrary"), name="gdn_fwd")(v3(q), v3(k), v3(v), v3(gates), v3(proj), inv, nw)
    return y.reshape(t, 512), st


def gdn_bwd(q, k, v, gates, proj, nw, inv, states, dy, seq):
    t = q.shape[0]
    nb, nc = t // seq, seq // CHUNK
    nsb = nc // GDN_CB
    chains = [(b, hh) for b in range(nb) for hh in range(2)]
    nch = len(chains)
    pair, gate, mats = _gdn_specs(nb, nsb, True)

    def body(q_ref, k_ref, v_ref, g_ref, gg_ref, inv_ref, st_ref, dy_ref, nw_ref,
             dq_ref, dk_ref, dv_ref, dgg_ref, dg_ref, dnw_ref, carry):
        s, p = pl.program_id(0), pl.program_id(1)

        @pl.when((s == 0) & (p == 0))
        def _():
            dnw_ref[...] = jnp.zeros_like(dnw_ref)

        @pl.when(p == 0)
        def _():
            dg_ref[...] = jnp.zeros_like(dg_ref)

        @pl.when(s == 0)
        def _():
            for ci in range(nch):
                carry[p * nch + ci] = jnp.zeros((GDN_DH, GDN_DH), F32)

        def step(idx, dstates):
            c = GDN_CB - 1 - idx
            r = _rows(c, CHUNK)
            ins = _gdn_chain_inputs(chains, p, r, c, q_ref, k_ref, v_ref, g_ref, gg_ref, inv_ref)
            st = jnp.stack([st_ref[b, hh, c] for b, hh in chains])
            dy = jnp.stack([dy_ref[b, r, hh * 128:(hh + 1) * 128] for b, hh in chains])
            _, vjp = jax.vjp(_gdn_chunks, *ins[:5], st, ins[5], nw_ref[...], ins[6])
            dq, dk, dv, dgc, dbc, dstates, dgg, dnw, _ = vjp((dy, dstates))
            dnw_ref[...] += dnw
            lane = _lane((CHUNK, 128))
            for ci, (b, hh) in enumerate(chains):
                h = 2 * p + hh
                ln = slice(hh * 128, (hh + 1) * 128)
                dq_ref[b, r, ln] = dq[ci]
                dk_ref[b, r, ln] = dk[ci]
                dv_ref[b, r, ln] = dv[ci]
                dgg_ref[b, r, ln] = dgg[ci]
                dg_ref[b, r, :] = jnp.where(lane == 8 + h, dgc[ci], jnp.where(lane == 12 + h, dbc[ci], dg_ref[b, r, :]))
            return dstates

        dstates = lax.fori_loop(0, GDN_CB, step, jnp.stack([carry[p * nch + ci] for ci in range(nch)]))
        for ci in range(nch):
            carry[p * nch + ci] = dstates[ci]

    v3 = lambda a: a.reshape(nb, seq, a.shape[1])
    res = pl.pallas_call(
        body, grid=(nsb, 2),
        in_specs=[pair(), pair(), pair(), gate(), pair(12), mats(CHUNK), mats(GDN_DH), pair(),
                  pl.BlockSpec((1, 128), lambda s, p: (0, 0))],
        out_specs=[pair(), pair(), pair(), pair(), gate(), pl.BlockSpec((1, 128), lambda s, p: (0, 0))],
        out_shape=[jax.ShapeDtypeStruct((nb, seq, 512), F32)] * 4
                  + [jax.ShapeDtypeStruct((nb, seq, 128), F32), jax.ShapeDtypeStruct((1, 128), F32)],
        scratch_shapes=[pltpu.VMEM((2 * nch, GDN_DH, GDN_DH), F32)],
        compiler_params=_cp("arbitrary", "arbitrary"),
        name="gdn_bwd")(v3(q), v3(k), v3(v), v3(gates), v3(proj), inv, states, v3(dy), nw)
    return [a.reshape(t, a.shape[2]) for a in res[:5]] + [res[5]]


def loss_head(y, target, tm=512):
    t, d = y.shape

    def body(y_ref, t_ref, s_ref, dy_ref):
        @pl.when(pl.program_id(0) == 0)
        def _():
            s_ref[...] = jnp.zeros_like(s_ref)

        err = y_ref[...] - t_ref[...]
        s_ref[...] += jnp.sum(err * err, axis=0, keepdims=True)
        dy_ref[...] = err * (1.0 / d)

    return pl.pallas_call(
        body, grid=(t // tm,),
        in_specs=[pl.BlockSpec((tm, d), lambda i: (i, 0)), pl.BlockSpec((tm, d), lambda i: (i, 0))],
        out_specs=[pl.BlockSpec((1, d), lambda i: (0, 0)), pl.BlockSpec((tm, d), lambda i: (i, 0))],
        out_shape=[jax.ShapeDtypeStruct((1, d), F32), jax.ShapeDtypeStruct((t, d), F32)],
        compiler_params=_cp("arbitrary"), name="loss_head")(y, target)


def _place():
    return lax.axis_index("x"), lax.axis_index("y"), lax.axis_index("c")


def _peer(k):
    x, y, c = _place()
    px = 1 - x if (k >> 2) & 1 else x
    py = 1 - y if (k >> 1) & 1 else y
    pc = 1 - c if k & 1 else c
    return (px, py, pc), 4 * px + 2 * py + pc


_ANY = pl.BlockSpec(memory_space=pl.ANY)
_SEM = pl.BlockSpec(memory_space=pltpu.SEMAPHORE)
_EFFECT = pltpu.SideEffectType.DATAFLOW_SIDE_EFFECTING


def _me():
    x, y, c = _place()
    return 4 * x + 2 * y + c


def _remote_copy(ins, lands, scatter, send_sems, recv_sems, a, k, arriving):
    pid, pidx = _peer(k)
    return pltpu.make_async_remote_copy(src_ref=ins[a].at[pidx] if scatter[a] else ins[a],
                                        dst_ref=lands[a].at[pidx if arriving else _me()],
                                        send_sem=send_sems.at[a * N_DEV + k], recv_sem=recv_sems.at[a * N_DEV + k],
                                        device_id=pid, device_id_type=MESH)


def exchange_place(arrays, scatter, name, after):
    n = len(arrays)

    def body(*refs):
        ins, lands, sems = refs[:n], refs[n + 1:2 * n + 1], refs[2 * n + 1]
        me = _me()
        local = [pltpu.make_async_copy(ins[a].at[me] if scatter[a] else ins[a], lands[a].at[me], sems.at[a])
                 for a in range(n)]
        for cp in local:
            cp.start()
        for cp in local:
            cp.wait()

    out_shape = [jax.ShapeDtypeStruct(a.shape if s else (N_DEV,) + a.shape, a.dtype) for a, s in zip(arrays, scatter)]
    return pl.pallas_call(body, in_specs=[_ANY] * (n + 1), out_specs=[_ANY] * n, out_shape=out_shape,
                          scratch_shapes=[pltpu.SemaphoreType.DMA((n,))], name=name)(*arrays, after)


def exchange_start(arrays, lands, scatter, name):
    n = len(arrays)

    def body(*refs):
        ins, lds = refs[:n], refs[n:2 * n]
        send_sems, recv_sems = refs[2 * n], refs[2 * n + 1]
        token = refs[-1]
        for k in range(1, N_DEV):
            for a in range(n):
                _remote_copy(ins, lds, scatter, send_sems, recv_sems, a, k, False).start()
        token[...] = jnp.zeros_like(token)

    hbm = lambda a: pltpu.HBM(a.shape, a.dtype)
    res = pl.pallas_call(
        body, name=name,
        in_specs=[_ANY] * (2 * n),
        out_specs=[_SEM, _SEM] + [_ANY] * (2 * n) + [pl.BlockSpec(memory_space=pltpu.VMEM)],
        out_shape=[pltpu.SemaphoreType.DMA((n * N_DEV,)), pltpu.SemaphoreType.DMA((n * N_DEV,))]
                  + [hbm(a) for a in arrays] + [hbm(a) for a in lands] + [jax.ShapeDtypeStruct((8, 128), F32)],
        input_output_aliases={i: 2 + i for i in range(2 * n)},
        compiler_params=pltpu.CompilerParams(has_side_effects=_EFFECT),
    )(*[pltpu.with_memory_space_constraint(a, pltpu.HBM) for a in list(arrays) + list(lands)])
    return res[0], res[1], res[2:2 + n], res[2 + n:2 + 2 * n], res[-1]


def exchange_wait(send_sems, recv_sems, arrays, lands, scatter, after, name):
    n = len(arrays)

    def body(*refs):
        ins, lds = refs[:n], refs[n:2 * n]
        ssem, rsem = refs[2 * n], refs[2 * n + 1]
        for k in range(1, N_DEV):
            for a in range(n):
                _remote_copy(ins, lds, scatter, ssem, rsem, a, k, True).wait_recv()
        for k in range(1, N_DEV):
            for a in range(n):
                _remote_copy(ins, lds, scatter, ssem, rsem, a, k, False).wait_send()

    hbm = lambda a: pltpu.HBM(a.shape, a.dtype)
    res = pl.pallas_call(
        body, name=name,
        in_specs=[_ANY] * (2 * n) + [_SEM, _SEM, _ANY],
        out_specs=[_ANY] * (2 * n),
        out_shape=[hbm(a) for a in arrays] + [hbm(a) for a in lands],
        input_output_aliases={i: i for i in range(2 * n)},
        compiler_params=pltpu.CompilerParams(has_side_effects=_EFFECT),
    )(*arrays, *lands, send_sems, recv_sems, after)
    return list(res[n:])


def exchange_begin(arrays, scatter, name, after):
    lands = exchange_place(arrays, scatter, name + "_place", after)
    send_sems, recv_sems, arrays_thru, lands_thru, token = exchange_start(arrays, lands, scatter, name + "_start")
    return (send_sems, recv_sems, arrays_thru, lands_thru, scatter, name), token


def exchange_end(state, after):
    send_sems, recv_sems, arrays_thru, lands_thru, scatter, name = state
    return exchange_wait(send_sems, recv_sems, arrays_thru, lands_thru, scatter, after, name + "_wait")


def adamw_reduce(slots, w, m, v, name, after=None):
    r, c = w.shape
    tr = r
    while tr * c * 4 > (1 << 20) and tr % 16 == 0:
        tr //= 2
    bc1 = 1.0 - ADAM_B1 ** ADAM_STEP
    bc2 = 1.0 - ADAM_B2 ** ADAM_STEP

    def body(s_ref, w_ref, m_ref, v_ref, *rest):
        g_ref, d_ref, nm_ref, nv_ref = rest[-4:]
        g = s_ref[0]
        for j in range(1, N_DEV):
            g = g + s_ref[j]
        nm = ADAM_B1 * m_ref[...] + (1.0 - ADAM_B1) * g
        nv = ADAM_B2 * v_ref[...] + (1.0 - ADAM_B2) * (g * g)
        g_ref[...] = g
        nm_ref[...] = nm
        nv_ref[...] = nv
        d_ref[...] = -ADAM_LR * ((nm / bc1) / (jnp.sqrt(nv / bc2) + ADAM_EPS) + ADAM_WD * w_ref[...])

    blk = lambda: pl.BlockSpec((tr, c), lambda i: (i, 0))
    extra = [] if after is None else [after]
    return pl.pallas_call(
        body, grid=(r // tr,),
        in_specs=[pl.BlockSpec((N_DEV, tr, c), lambda i: (0, i, 0)), blk(), blk(), blk()] + [_ANY] * len(extra),
        out_specs=[blk(), blk(), blk(), blk()],
        out_shape=[jax.ShapeDtypeStruct((r, c), F32)] * 4,
        compiler_params=_cp("parallel"), name=name)(slots, w, m, v, *extra)


BIG = ("ffn1_w_in", "ffn1_w_out", "w_in", "gdn_conv", "w_out", "ffn2_w_in", "ffn2_w_out")
SMALL = ("ffn1_norm", "mix_norm", "fox_q_norm", "fox_k_norm", "fox_f_bias", "gdn_a_log", "gdn_dt_bias",
         "gdn_out_norm", "ffn2_norm")
WEIGHTS = ("ffn1_norm", "ffn1_w_in", "ffn1_w_out", "mix_norm", "w_in", "fox_q_norm", "fox_k_norm", "fox_f_bias",
           "gdn_conv", "gdn_a_log", "gdn_dt_bias", "gdn_out_norm", "w_out", "ffn2_norm", "ffn2_w_in", "ffn2_w_out")
IN_COLS = (("fq", 512), ("fk", 512), ("fv", 512), ("ff", 8), ("gq", 512), ("gk", 512), ("gv", 512),
           ("ga", 4), ("gb", 4), ("gg", 512))
MY_BIG = ("fq", "fk", "fv", "gq", "gk", "gv", "gg")
MY_SMALL = ("ff", "ga", "gb")
SMALL_ROWS = 8 * 128


def _in_cols_to_mine(w):
    off, parts = 0, {}
    for nm, wd in IN_COLS:
        parts[nm] = w[:, off:off + wd]
        off += wd
    small = jnp.concatenate([parts[nm] for nm in MY_SMALL], axis=1)
    small = jnp.pad(small, ((0, 0), (0, 128 - small.shape[1])))
    return jnp.concatenate([parts[nm] for nm in MY_BIG] + [small], axis=1)


def _in_cols_from_mine(g):
    parts = {nm: g[:, i * 512:(i + 1) * 512] for i, nm in enumerate(MY_BIG)}
    off = N_BIG
    for nm in MY_SMALL:
        wd = dict(IN_COLS)[nm]
        parts[nm] = g[:, off:off + wd]
        off += wd
    return jnp.concatenate([parts[nm] for nm, _ in IN_COLS], axis=1)


def _pack_small(vals):
    rows = []
    nl = vals[SMALL[0]].shape[0]
    for l in range(nl):
        for nm in SMALL:
            v = vals[nm][l].reshape(-1)
            pad = (-v.shape[0]) % SMALL_ROWS
            rows.append(jnp.pad(v, (0, pad)).reshape(-1, 128))
    return jnp.concatenate(rows, axis=0)


def _unpack_small(packed, like):
    out = {nm: [] for nm in SMALL}
    row = 0
    nl = like[SMALL[0]].shape[0]
    for l in range(nl):
        for nm in SMALL:
            n = like[nm].shape[1]
            nr = -(-n // SMALL_ROWS) * 8
            out[nm].append(packed[row:row + nr].reshape(-1)[:n])
            row += nr
    return {nm: jnp.stack(v) for nm, v in out.items()}


def kernel(x, ffn1_norm, ffn1_w_in, ffn1_w_out, mix_norm, w_in, fox_q_norm, fox_k_norm, fox_f_bias, gdn_conv, gdn_a_log, gdn_dt_bias, gdn_out_norm, w_out, ffn2_norm, ffn2_w_in, ffn2_w_out, loss_target, m_ffn1_norm, m_ffn1_w_in, m_ffn1_w_out, m_mix_norm, m_w_in, m_fox_q_norm, m_fox_k_norm, m_fox_f_bias, m_gdn_conv, m_gdn_a_log, m_gdn_dt_bias, m_gdn_out_norm, m_w_out, m_ffn2_norm, m_ffn2_w_in, m_ffn2_w_out, v_ffn1_norm, v_ffn1_w_in, v_ffn1_w_out, v_mix_norm, v_w_in, v_fox_q_norm, v_fox_k_norm, v_fox_f_bias, v_gdn_conv, v_gdn_a_log, v_gdn_dt_bias, v_gdn_out_norm, v_w_out, v_ffn2_norm, v_ffn2_w_in, v_ffn2_w_out):
    wts = dict(ffn1_norm=ffn1_norm, ffn1_w_in=ffn1_w_in, ffn1_w_out=ffn1_w_out, mix_norm=mix_norm, w_in=w_in,
               fox_q_norm=fox_q_norm, fox_k_norm=fox_k_norm, fox_f_bias=fox_f_bias, gdn_conv=gdn_conv,
               gdn_a_log=gdn_a_log, gdn_dt_bias=gdn_dt_bias, gdn_out_norm=gdn_out_norm, w_out=w_out,
               ffn2_norm=ffn2_norm, ffn2_w_in=ffn2_w_in, ffn2_w_out=ffn2_w_out)
    mom = dict(ffn1_norm=m_ffn1_norm, ffn1_w_in=m_ffn1_w_in, ffn1_w_out=m_ffn1_w_out, mix_norm=m_mix_norm, w_in=m_w_in,
               fox_q_norm=m_fox_q_norm, fox_k_norm=m_fox_k_norm, fox_f_bias=m_fox_f_bias, gdn_conv=m_gdn_conv,
               gdn_a_log=m_gdn_a_log, gdn_dt_bias=m_gdn_dt_bias, gdn_out_norm=m_gdn_out_norm, w_out=m_w_out,
               ffn2_norm=m_ffn2_norm, ffn2_w_in=m_ffn2_w_in, ffn2_w_out=m_ffn2_w_out)
    var = dict(ffn1_norm=v_ffn1_norm, ffn1_w_in=v_ffn1_w_in, ffn1_w_out=v_ffn1_w_out, mix_norm=v_mix_norm, w_in=v_w_in,
               fox_q_norm=v_fox_q_norm, fox_k_norm=v_fox_k_norm, fox_f_bias=v_fox_f_bias, gdn_conv=v_gdn_conv,
               gdn_a_log=v_gdn_a_log, gdn_dt_bias=v_gdn_dt_bias, gdn_out_norm=v_gdn_out_norm, w_out=v_w_out,
               ffn2_norm=v_ffn2_norm, ffn2_w_in=v_ffn2_w_in, ffn2_w_out=v_ffn2_w_out)
    nb, seq, d = x.shape
    t = nb * seq
    depth = ffn1_norm.shape[0]

    def shards_of(l):
        return [wts[nm][l] if nm == "gdn_conv" else wts[nm][l].astype(BF16) for nm in BIG]

    def behind(nw, token):
        return nw + token[0:1, 0:1]

    def layer_weights(l, g):
        g = dict(zip(BIG, g))
        fb = g["ffn1_w_in"].shape[2]
        return dict(
            w1i=g["ffn1_w_in"].reshape(2, 4, d, fb), w1o=g["ffn1_w_out"].reshape(4, fb, d),
            w2i=g["ffn2_w_in"].reshape(2, 4, d, fb), w2o=g["ffn2_w_out"].reshape(4, fb, d),
            wi=_in_cols_to_mine(g["w_in"].transpose(1, 0, 2).reshape(d, -1)),
            cw=g["gdn_conv"].transpose(1, 0, 2).reshape(CONV_W, -1),
            wo=g["w_out"].reshape(d, d),
            n1=ffn1_norm[l][None], nmix=mix_norm[l][None], n2=ffn2_norm[l][None],
            qw=fox_q_norm[l][None], kw=fox_k_norm[l][None], onw=gdn_out_norm[l][None],
            gp=jnp.concatenate([
                jnp.concatenate([fox_f_bias[l], gdn_dt_bias[l], jnp.zeros((116,), F32)])[None],
                jnp.concatenate([jnp.zeros((8,), F32), gdn_a_log[l], jnp.zeros((116,), F32)])[None],
                jnp.zeros((6, 128), F32)], axis=0))

    h = x.reshape(t, d)
    gather_flags = [False] * len(BIG)
    state, token = exchange_begin(shards_of(0), gather_flags, "gather_weights_0", ffn1_norm)
    landed = exchange_end(state, token)
    saved = []
    for l in range(depth):
        p = layer_weights(l, landed)
        if l + 1 < depth:
            state, token = exchange_begin(shards_of(l + 1), gather_flags, f"gather_weights_{l + 1}", landed[0])
            p["n1"] = behind(p["n1"], token)
        x0 = h
        x1 = ffn_fwd(x0, p["n1"], p["w1i"], p["w1o"])
        proj, hn = inproj_fwd(x1, p["nmix"], p["wi"])
        gates = gates_fwd(proj, p["gp"], seq)
        yf, lse = attn_fwd(proj, gates, p["qw"], p["kw"], seq, tq=min(seq, ATTN_TQ_FWD))
        qh, kh, vh = gdn_pre_fwd(proj, p["cw"], seq)
        inv = gdn_inv(kh, gates, seq)
        yg, st = gdn_fwd(qh, kh, vh, gates, proj, p["onw"], inv, seq)
        x2, ycat = outproj_fwd(x1, yf, yg, p["wo"])
        h = ffn_fwd(x2, p["n2"], p["w2i"], p["w2o"])
        saved.append(dict(p=p, x0=x0, x1=x1, x2=x2, proj=proj, hn=hn, gates=gates, yf=yf, lse=lse,
                          qh=qh, kh=kh, vh=vh, st=st, inv=inv, ycat=ycat))
        if l + 1 < depth:
            landed = exchange_end(state, h)

    sq, dh = loss_head(h, loss_target.reshape(t, d))
    loss = lax.psum(0.5 * jnp.sum(sq) / d, ("x", "y", "c"))

    gbig = [None] * depth
    got = [None] * depth
    pending = None
    gsmall = {nm: [None] * depth for nm in SMALL}
    for l in reversed(range(depth)):
        s = saved[l]
        p = s["p"]
        n2 = p["n2"] if pending is None else behind(p["n2"], token)
        dx2, dn2, dgu, hh, xn, dyh = ffn_bwd(s["x2"], dh, n2, p["w2i"], p["w2o"])
        g_w2i = wgrad_ffn_in(xn, dgu)
        g_w2o = wgrad_ffn_out(hh, dyh)
        dyf, dyg, dyb = outproj_bwd(dx2, p["wo"])
        g_wo = wgrad_2d(s["ycat"], dyb, 512, "wgrad_w_out")
        dq, dk, dv, dga, dqw, dkw = attn_bwd(s["proj"], s["gates"], p["qw"], p["kw"], s["yf"], s["lse"], dyf, seq,
                                             tq=min(seq, ATTN_TQ_BWD))
        dqh, dkh, dvh, dgg, dgb, donw = gdn_bwd(s["qh"], s["kh"], s["vh"], s["gates"], s["proj"], p["onw"],
                                                 s["inv"], s["st"], dyg, seq)
        dxq, dxk, dxv, dwq, dwk, dwv = gdn_pre_bwd(s["proj"], p["cw"], dqh, dkh, dvh, seq)
        dsm, dgp = gates_bwd(s["proj"], p["gp"], dga, dgb, seq)
        dx1, dnmix, dproj = inproj_bwd(s["x1"], dx2, p["nmix"], p["wi"], [dq, dk, dv, dxq, dxk, dxv, dgg, dsm])
        g_wi = wgrad_2d(s["hn"], dproj, 512, "wgrad_w_in")
        dh, dn1, dgu, hh, xn, dyh = ffn_bwd(s["x0"], dx1, p["n1"], p["w1i"], p["w1o"])
        g_w1i = wgrad_ffn_in(xn, dgu)
        g_w1o = wgrad_ffn_out(hh, dyh)
        fb = g_w1i.shape[3]
        g_cw = jnp.concatenate([dwq, dwk, dwv], axis=1)
        gbig[l] = dict(
            ffn1_w_in=g_w1i.reshape(N_DEV, d, fb), ffn1_w_out=g_w1o.reshape(N_DEV, -1, d),
            w_in=_in_cols_from_mine(g_wi).reshape(d, N_DEV, -1).transpose(1, 0, 2),
            gdn_conv=g_cw.reshape(CONV_W, N_DEV, -1).transpose(1, 0, 2),
            w_out=g_wo.reshape(N_DEV, -1, d),
            ffn2_w_in=g_w2i.reshape(N_DEV, d, fb), ffn2_w_out=g_w2o.reshape(N_DEV, -1, d))
        for nm, val in (("ffn1_norm", dn1[0]), ("mix_norm", dnmix[0]), ("fox_q_norm", dqw[0]), ("fox_k_norm", dkw[0]),
                        ("fox_f_bias", dgp[0, 0:8]), ("gdn_a_log", dgp[1, 8:12]), ("gdn_dt_bias", dgp[0, 8:12]),
                        ("gdn_out_norm", donw[0]), ("ffn2_norm", dn2[0])):
            gsmall[nm][l] = val
        send, flags = [gbig[l][nm] for nm in BIG], [True] * len(BIG)
        if l == 0:
            send.append(_pack_small({nm: jnp.stack(v) for nm, v in gsmall.items()}))
            flags.append(False)
        prev = dh
        if pending is not None:
            got[pending[1]] = exchange_end(pending[0], dh)
            prev = got[pending[1]][0]
        state, token = exchange_begin(send, flags, f"exchange_grads_{l}", prev)
        pending = (state, l)
    grad_x = dh.reshape(nb, seq, d)

    def update_layer(l, slots, after):
        out = {}
        for i, nm in enumerate(BIG):
            r, c = wts[nm].shape[1:]
            out[nm] = adamw_reduce(slots[i].reshape(N_DEV, r, c), wts[nm][l], mom[nm][l], var[nm][l], f"adamw_{nm}_{l}",
                                   after)
            if after is not None:
                after = out[nm][0]
        return out

    per_layer = [None] * depth
    last = token
    for l in range(1, depth):
        per_layer[l] = update_layer(l, got[l], token)
        last = per_layer[l][BIG[-1]][0]
    got[0] = exchange_end(pending[0], last)
    per_layer[0] = update_layer(0, got[0], None)
    res = {nm: [jnp.stack([per_layer[l][nm][j] for l in range(depth)]) for j in range(4)] for nm in BIG}
    small_like = {nm: wts[nm] for nm in SMALL}
    sm = adamw_reduce(got[0][-1], _pack_small(small_like), _pack_small({nm: mom[nm] for nm in SMALL}),
                      _pack_small({nm: var[nm] for nm in SMALL}), "adamw_small")
    sm = [_unpack_small(a, small_like) for a in sm]
    for nm in SMALL:
        res[nm] = [sm[j][nm] for j in range(4)]
    return (loss, grad_x, *[res[nm][0] for nm in WEIGHTS], *[res[nm][1] for nm in WEIGHTS],
            *[res[nm][2] for nm in WEIGHTS], *[res[nm][3] for nm in WEIGHTS])
```

```python
import functools

import jax
import jax.numpy as jnp
from jax import lax
from jax.experimental import pallas as pl
from jax.experimental.pallas import tpu as pltpu

F32 = jnp.float32
BF16 = jnp.bfloat16
EPS = 1e-6
N_DEV = 8
MESH = pl.DeviceIdType.MESH
HIGHEST = lax.Precision.HIGHEST
VMEM_LIMIT = 56 * 1024 * 1024

FOX_HEADS, FOX_DH = 8, 64
GDN_HEADS, GDN_DH = 4, 128
CHUNK = 64
CONV_W = 4

ADAM_LR, ADAM_B1, ADAM_B2, ADAM_EPS, ADAM_WD, ADAM_STEP = 0.001, 0.9, 0.999, 1e-08, 0.01, 10


def _cp(*sem):
    return pltpu.CompilerParams(dimension_semantics=sem, vmem_limit_bytes=VMEM_LIMIT)


def _dot(a, b):
    return jnp.dot(a, b, preferred_element_type=F32)


def _dot_nt(a, b):
    return lax.dot_general(a, b, (((1,), (1,)), ((), ())), preferred_element_type=F32)


def _dot_tn(a, b):
    return lax.dot_general(a, b, (((0,), (0,)), ((), ())), preferred_element_type=F32)


def _rstd(xf):
    return lax.rsqrt(jnp.mean(xf * xf, axis=-1, keepdims=True) + EPS)


def _rms_bwd(xf, r, dyn):
    return r * dyn - xf * (r * r * r) * jnp.mean(dyn * xf, axis=-1, keepdims=True)


def ffn_fwd(x, nw, w_in, w_out, tm=512):
    t, d = x.shape
    nj, fb = w_out.shape[0], w_out.shape[1]

    def body(x_ref, nw_ref, wi_ref, wo_ref, o_ref, xn_ref, acc_ref):
        j = pl.program_id(1)

        @pl.when(j == 0)
        def _():
            xf = x_ref[...]
            xn_ref[...] = (xf * _rstd(xf) * nw_ref[...]).astype(BF16)
            acc_ref[...] = jnp.zeros_like(acc_ref)

        xn = xn_ref[...]
        g = _dot(xn, wi_ref[0])
        u = _dot(xn, wi_ref[1])
        h = (g * jax.nn.sigmoid(g) * u).astype(BF16)
        acc_ref[...] += _dot(h, wo_ref[...])

        @pl.when(j == nj - 1)
        def _():
            o_ref[...] = x_ref[...] + 0.5 * acc_ref[...]

    return pl.pallas_call(
        body, grid=(t // tm, nj),
        in_specs=[pl.BlockSpec((tm, d), lambda i, j: (i, 0)),
                  pl.BlockSpec((1, d), lambda i, j: (0, 0)),
                  pl.BlockSpec((2, None, d, fb), lambda i, j: (0, j, 0, 0)),
                  pl.BlockSpec((None, fb, d), lambda i, j: (j, 0, 0))],
        out_specs=pl.BlockSpec((tm, d), lambda i, j: (i, 0)),
        out_shape=jax.ShapeDtypeStruct((t, d), F32),
        scratch_shapes=[pltpu.VMEM((tm, d), BF16), pltpu.VMEM((tm, d), F32)],
        compiler_params=_cp("parallel", "arbitrary"), name="ffn_fwd")(x, nw, w_in, w_out)


def ffn_bwd(x, dy, nw, w_in, w_out, tm=512):
    t, d = x.shape
    nj, fb = w_out.shape[0], w_out.shape[1]

    def body(x_ref, dy_ref, nw_ref, wi_ref, wo_ref,
             dx_ref, dnw_ref, dgu_ref, h_ref, xn_ref, dyh_ref, acc_ref):
        i, j = pl.program_id(0), pl.program_id(1)

        @pl.when(j == 0)
        def _():
            xf = x_ref[...]
            xn_ref[...] = (xf * _rstd(xf) * nw_ref[...]).astype(BF16)
            dyh_ref[...] = (0.5 * dy_ref[...]).astype(BF16)
            acc_ref[...] = jnp.zeros_like(acc_ref)

        @pl.when((i == 0) & (j == 0))
        def _():
            dnw_ref[...] = jnp.zeros_like(dnw_ref)

        xn = xn_ref[...]
        g = _dot(xn, wi_ref[0])
        u = _dot(xn, wi_ref[1])
        sg = jax.nn.sigmoid(g)
        silu = g * sg
        dh = _dot_nt(dyh_ref[...], wo_ref[...])
        dg = (dh * u * (sg * (1.0 + g * (1.0 - sg)))).astype(BF16)
        du = (dh * silu).astype(BF16)
        dgu_ref[0] = dg
        dgu_ref[1] = du
        h_ref[...] = (silu * u).astype(BF16)
        acc_ref[...] += _dot_nt(dg, wi_ref[0]) + _dot_nt(du, wi_ref[1])

        @pl.when(j == nj - 1)
        def _():
            xf = x_ref[...]
            r = _rstd(xf)
            dxn = acc_ref[...]
            dnw_ref[...] += jnp.sum(dxn * xf * r, axis=0, keepdims=True)
            dx_ref[...] = _rms_bwd(xf, r, dxn * nw_ref[...]) + dy_ref[...]

    return pl.pallas_call(
        body, grid=(t // tm, nj),
        in_specs=[pl.BlockSpec((tm, d), lambda i, j: (i, 0)),
                  pl.BlockSpec((tm, d), lambda i, j: (i, 0)),
                  pl.BlockSpec((1, d), lambda i, j: (0, 0)),
                  pl.BlockSpec((2, None, d, fb), lambda i, j: (0, j, 0, 0)),
                  pl.BlockSpec((None, fb, d), lambda i, j: (j, 0, 0))],
        out_specs=[pl.BlockSpec((tm, d), lambda i, j: (i, 0)),
                   pl.BlockSpec((1, d), lambda i, j: (0, 0)),
                   pl.BlockSpec((2, None, tm, fb), lambda i, j: (0, j, i, 0)),
                   pl.BlockSpec((None, tm, fb), lambda i, j: (j, i, 0)),
                   pl.BlockSpec((tm, d), lambda i, j: (i, 0)),
                   pl.BlockSpec((tm, d), lambda i, j: (i, 0))],
        out_shape=[jax.ShapeDtypeStruct((t, d), F32),
                   jax.ShapeDtypeStruct((1, d), F32),
                   jax.ShapeDtypeStruct((2, nj, t, fb), BF16),
                   jax.ShapeDtypeStruct((nj, t, fb), BF16),
                   jax.ShapeDtypeStruct((t, d), BF16),
                   jax.ShapeDtypeStruct((t, d), BF16)],
        scratch_shapes=[pltpu.VMEM((tm, d), F32)],
        compiler_params=_cp("arbitrary", "arbitrary"), name="ffn_bwd")(x, dy, nw, w_in, w_out)


def _wgrad_call(a, b, a_spec, b_spec, out_shape, out_spec, grid, name):
    last = len(grid) - 1

    def body(a_ref, b_ref, o_ref):
        @pl.when(pl.program_id(last) == 0)
        def _():
            o_ref[...] = jnp.zeros_like(o_ref)

        o_ref[...] += _dot_tn(a_ref[...], b_ref[...])

    sem = ("parallel",) * last + ("arbitrary",)
    return pl.pallas_call(body, grid=grid, in_specs=[a_spec, b_spec], out_specs=out_spec,
                          out_shape=jax.ShapeDtypeStruct(out_shape, F32),
                          compiler_params=_cp(*sem), name=name)(a, b)


def wgrad_ffn_in(xn, dgu, tm=512):
    t, d = xn.shape
    _, nj, _, fb = dgu.shape
    return _wgrad_call(xn, dgu,
                       pl.BlockSpec((tm, d), lambda p, j, k: (k, 0)),
                       pl.BlockSpec((None, None, tm, fb), lambda p, j, k: (p, j, k, 0)),
                       (2, nj, d, fb), pl.BlockSpec((None, None, d, fb), lambda p, j, k: (p, j, 0, 0)),
                       (2, nj, t // tm), "wgrad_ffn_in")


def wgrad_ffn_out(h, dyh, tm=512):
    nj, t, fb = h.shape
    d = dyh.shape[1]
    return _wgrad_call(h, dyh,
                       pl.BlockSpec((None, tm, fb), lambda j, k: (j, k, 0)),
                       pl.BlockSpec((tm, d), lambda j, k: (k, 0)),
                       (nj, fb, d), pl.BlockSpec((None, fb, d), lambda j, k: (j, 0, 0)),
                       (nj, t // tm), "wgrad_ffn_out")


def wgrad_2d(a, b, tk, name, tm=512):
    t, k = a.shape
    n = b.shape[1]
    return _wgrad_call(a, b,
                       pl.BlockSpec((tm, tk), lambda c, s: (s, c)),
                       pl.BlockSpec((tm, n), lambda c, s: (s, 0)),
                       (k, n), pl.BlockSpec((tk, n), lambda c, s: (c, 0)),
                       (k // tk, t // tm), name)


N_BIG = 7 * 512
N_PROJ = N_BIG + 128
COL_SMALL = N_BIG // 128


def inproj_fwd(x, nw, w, tm=256):
    t, d = x.shape
    n = w.shape[1]

    def body(x_ref, nw_ref, w_ref, p_ref, hn_ref):
        xf = x_ref[...]
        hn = (xf * _rstd(xf) * nw_ref[...]).astype(BF16)
        hn_ref[...] = hn
        p_ref[...] = _dot(hn, w_ref[...])

    return pl.pallas_call(
        body, grid=(t // tm,),
        in_specs=[pl.BlockSpec((tm, d), lambda i: (i, 0)), pl.BlockSpec((1, d), lambda i: (0, 0)),
                  pl.BlockSpec((d, n), lambda i: (0, 0))],
        out_specs=[pl.BlockSpec((tm, n), lambda i: (i, 0)), pl.BlockSpec((tm, d), lambda i: (i, 0))],
        out_shape=[jax.ShapeDtypeStruct((t, n), F32), jax.ShapeDtypeStruct((t, d), BF16)],
        compiler_params=_cp("parallel"), name="inproj_fwd")(x, nw, w)


def inproj_bwd(x, dres, nw, w, dparts, tm=256):
    t, d = x.shape
    n = w.shape[1]
    widths = [p.shape[1] for p in dparts]
    assert sum(widths) == n

    def body(x_ref, dres_ref, nw_ref, w_ref, *rest):
        part_refs, (dx_ref, dnw_ref, dp_ref) = rest[:len(widths)], rest[len(widths):]

        @pl.when(pl.program_id(0) == 0)
        def _():
            dnw_ref[...] = jnp.zeros_like(dnw_ref)

        dp = jnp.concatenate([r[...].astype(BF16) for r in part_refs], axis=1)
        dp_ref[...] = dp
        dhn = _dot_nt(dp, w_ref[...])
        xf = x_ref[...]
        r = _rstd(xf)
        dnw_ref[...] += jnp.sum(dhn * xf * r, axis=0, keepdims=True)
        dx_ref[...] = _rms_bwd(xf, r, dhn * nw_ref[...]) + dres_ref[...]

    return pl.pallas_call(
        body, grid=(t // tm,),
        in_specs=[pl.BlockSpec((tm, d), lambda i: (i, 0)), pl.BlockSpec((tm, d), lambda i: (i, 0)),
                  pl.BlockSpec((1, d), lambda i: (0, 0)), pl.BlockSpec((d, n), lambda i: (0, 0))]
                 + [pl.BlockSpec((tm, wd), lambda i: (i, 0)) for wd in widths],
        out_specs=[pl.BlockSpec((tm, d), lambda i: (i, 0)), pl.BlockSpec((1, d), lambda i: (0, 0)),
                   pl.BlockSpec((tm, n), lambda i: (i, 0))],
        out_shape=[jax.ShapeDtypeStruct((t, d), F32), jax.ShapeDtypeStruct((1, d), F32),
                   jax.ShapeDtypeStruct((t, n), BF16)],
        compiler_params=_cp("arbitrary"), name="inproj_bwd")(x, dres, nw, w, *dparts)


def outproj_fwd(x, yf, yg, w, tm=512):
    t, d = x.shape
    hw = yf.shape[1]

    def body(x_ref, yf_ref, yg_ref, w_ref, o_ref, y_ref):
        y = jnp.concatenate([yf_ref[...], yg_ref[...]], axis=1).astype(BF16)
        y_ref[...] = y
        o_ref[...] = x_ref[...] + _dot(y, w_ref[...])

    return pl.pallas_call(
        body, grid=(t // tm,),
        in_specs=[pl.BlockSpec((tm, d), lambda i: (i, 0)), pl.BlockSpec((tm, hw), lambda i: (i, 0)),
                  pl.BlockSpec((tm, hw), lambda i: (i, 0)), pl.BlockSpec((2 * hw, d), lambda i: (0, 0))],
        out_specs=[pl.BlockSpec((tm, d), lambda i: (i, 0)), pl.BlockSpec((tm, 2 * hw), lambda i: (i, 0))],
        out_shape=[jax.ShapeDtypeStruct((t, d), F32), jax.ShapeDtypeStruct((t, 2 * hw), BF16)],
        compiler_params=_cp("parallel"), name="outproj_fwd")(x, yf, yg, w)


def outproj_bwd(dy, w, tm=512):
    t, d = dy.shape
    hw = w.shape[0] // 2

    def body(dy_ref, w_ref, df_ref, dg_ref, dyb_ref):
        dyb = dy_ref[...].astype(BF16)
        dyb_ref[...] = dyb
        dyy = _dot_nt(dyb, w_ref[...])
        df_ref[...] = dyy[:, :hw]
        dg_ref[...] = dyy[:, hw:]

    return pl.pallas_call(
        body, grid=(t // tm,),
        in_specs=[pl.BlockSpec((tm, d), lambda i: (i, 0)), pl.BlockSpec((2 * hw, d), lambda i: (0, 0))],
        out_specs=[pl.BlockSpec((tm, hw), lambda i: (i, 0)), pl.BlockSpec((tm, hw), lambda i: (i, 0)),
                   pl.BlockSpec((tm, d), lambda i: (i, 0))],
        out_shape=[jax.ShapeDtypeStruct((t, hw), F32), jax.ShapeDtypeStruct((t, hw), F32),
                   jax.ShapeDtypeStruct((t, d), BF16)],
        compiler_params=_cp("parallel"), name="outproj_bwd")(dy, w)


def _lane(shape):
    return lax.broadcasted_iota(jnp.int32, shape, 1)


def _row(shape):
    return lax.broadcasted_iota(jnp.int32, shape, 0)


def _gate_terms(val, gp_ref):
    z = val + gp_ref[0:1, :]
    sp = jnp.log(1.0 + jnp.exp(-jnp.abs(z)))
    return z, sp


def gates_fwd(proj, gp, seq, ts=512):
    t = proj.shape[0]
    nb, ns = t // seq, seq // ts

    def body(v_ref, gp_ref, o_ref, carry_ref):
        @pl.when(pl.program_id(1) == 0)
        def _():
            carry_ref[...] = jnp.zeros_like(carry_ref)

        z, sp = _gate_terms(v_ref[...], gp_ref)
        logsig = jnp.minimum(z, 0.0) - sp
        tri = (_row((ts, ts)) >= _lane((ts, ts))).astype(F32)
        cum = jnp.dot(tri, logsig, precision=HIGHEST, preferred_element_type=F32) + carry_ref[0:1, :]
        carry_ref[0:1, :] = cum[ts - 1:ts, :]
        g = -jnp.exp(gp_ref[1:2, :]) * (jnp.maximum(z, 0.0) + sp)
        beta = jax.nn.sigmoid(z)
        lane = _lane((ts, 128))
        o_ref[...] = jnp.where(lane < 8, cum, jnp.where(lane < 12, g, jnp.where(lane < 16, beta, 0.0)))

    return pl.pallas_call(
        body, grid=(nb, ns),
        in_specs=[pl.BlockSpec((ts, 128), lambda b, s: (b * ns + s, COL_SMALL)),
                  pl.BlockSpec((8, 128), lambda b, s: (0, 0))],
        out_specs=pl.BlockSpec((ts, 128), lambda b, s: (b * ns + s, 0)),
        out_shape=jax.ShapeDtypeStruct((t, 128), F32),
        scratch_shapes=[pltpu.VMEM((8, 128), F32)],
        compiler_params=_cp("parallel", "arbitrary"), name="gates_fwd")(proj, gp)


def gates_bwd(proj, gp, dga, dgb, seq, ts=512):
    t = proj.shape[0]
    nb, ns = t // seq, seq // ts

    def body(v_ref, gp_ref, da_ref, db_ref, ds_ref, dgp_ref, carry_ref):
        @pl.when(pl.program_id(1) == 0)
        def _():
            carry_ref[...] = jnp.zeros_like(carry_ref)

        @pl.when((pl.program_id(0) == 0) & (pl.program_id(1) == 0))
        def _():
            dgp_ref[...] = jnp.zeros_like(dgp_ref)

        lane = _lane((ts, 128))
        dgate = jnp.where(lane < 8, da_ref[...], jnp.where(lane < 16, db_ref[...], 0.0))
        z, sp = _gate_terms(v_ref[...], gp_ref)
        triu = (_row((ts, ts)) <= _lane((ts, ts))).astype(F32)
        dlog = jnp.dot(triu, dgate, precision=HIGHEST, preferred_element_type=F32) + carry_ref[0:1, :]
        carry_ref[0:1, :] = dlog[0:1, :]
        sig = jax.nn.sigmoid(z)
        nea = -jnp.exp(gp_ref[1:2, :])
        g = nea * (jnp.maximum(z, 0.0) + sp)
        dz = jnp.where(lane < 8, dlog * (1.0 - sig),
                       jnp.where(lane < 12, dgate * nea * sig, dgate * sig * (1.0 - sig)))
        dz = jnp.where(lane < 16, dz, 0.0)
        ds_ref[...] = dz
        dgp_ref[0:1, :] += jnp.where(lane[0:1] < 12, jnp.sum(dz, axis=0, keepdims=True), 0.0)
        dgp_ref[1:2, :] += jnp.where((lane[0:1] >= 8) & (lane[0:1] < 12), jnp.sum(dgate * g, axis=0, keepdims=True), 0.0)

    rev = lambda b, s: (b * ns + (ns - 1 - s), 0)
    return pl.pallas_call(
        body, grid=(nb, ns),
        in_specs=[pl.BlockSpec((ts, 128), lambda b, s: (b * ns + (ns - 1 - s), COL_SMALL)),
                  pl.BlockSpec((8, 128), lambda b, s: (0, 0)),
                  pl.BlockSpec((ts, 128), rev), pl.BlockSpec((ts, 128), rev)],
        out_specs=[pl.BlockSpec((ts, 128), rev), pl.BlockSpec((8, 128), lambda b, s: (0, 0))],
        out_shape=[jax.ShapeDtypeStruct((t, 128), F32), jax.ShapeDtypeStruct((8, 128), F32)],
        scratch_shapes=[pltpu.VMEM((8, 128), F32)],
        compiler_params=_cp("arbitrary", "arbitrary"), name="gates_bwd")(proj, gp, dga, dgb)


NEG = -1e30
ATTN_TQ_FWD = 1024
ATTN_TQ_BWD = 512


def _pick_lane(tile, idx):
    return jnp.sum(jnp.where(_lane(tile.shape) == idx, tile, 0.0), axis=1, keepdims=True)


def _col_to_row(col, n):
    return jnp.sum(jnp.where(_row((n, n)) == _lane((n, n)), col, 0.0), axis=0, keepdims=True)


def _row_to_col(row, n):
    return jnp.sum(jnp.where(_row((n, n)) == _lane((n, n)), row, 0.0), axis=1, keepdims=True)


def _rows(i, n):
    return pl.ds(pl.multiple_of(i * n, n), n)


def _once(shape, index_map):
    return pl.BlockSpec(shape, index_map, pipeline_mode=pl.Buffered(1))


def attn_fwd(proj, gates, qw, kw, seq, tq=256):
    t = proj.shape[0]
    nb, nq, dh = t // seq, seq // tq, FOX_DH
    scale = dh ** -0.5

    def body(q_ref, k_ref, v_ref, g_ref, qw_ref, kw_ref, y_ref, lse_ref, qs, ks, vs, ccol, crow):
        p = pl.program_id(1)
        heads = range(2)

        def prep(i, _):
            r = _rows(i, tq)
            for hh in heads:
                lanes = slice(hh * dh, (hh + 1) * dh)
                qf, kf = q_ref[r, lanes], k_ref[r, lanes]
                qs[hh, r, :] = (qf * _rstd(qf) * qw_ref[...] * scale).astype(BF16)
                ks[hh, r, :] = (kf * _rstd(kf) * kw_ref[...]).astype(BF16)
                vs[hh, r, :] = v_ref[r, lanes].astype(BF16)
                cc = _pick_lane(g_ref[r, :], 2 * p + hh)
                ccol[hh, r, :] = cc
                crow[hh * nq + i] = _col_to_row(cc, tq)
            return 0

        lax.fori_loop(0, nq, prep, 0)

        def q_tile(i, _):
            r = _rows(i, tq)
            qt = [qs[hh, r, :] for hh in heads]
            cc = [ccol[hh, r, :] for hh in heads]

            def kv_step(j, carry, masked):
                kr = _rows(j, tq)
                out = []
                for hh in heads:
                    m, l, acc = carry[hh]
                    s = _dot_nt(qt[hh], ks[hh, kr, :]) + (cc[hh] - crow[hh * nq + j])
                    if masked:
                        s = jnp.where(_row((tq, tq)) >= _lane((tq, tq)), s, NEG)
                    m_new = jnp.maximum(m, jnp.max(s, axis=1, keepdims=True))
                    pe = jnp.exp(s - m_new)
                    a = jnp.exp(m - m_new)
                    out.append((m_new, a * l + jnp.sum(pe, axis=1, keepdims=True),
                                a * acc + _dot(pe.astype(BF16), vs[hh, kr, :])))
                return tuple(out)

            one = (jnp.full((tq, 1), NEG, F32), jnp.zeros((tq, 1), F32), jnp.zeros((tq, dh), F32))
            carry = lax.fori_loop(0, i, lambda j, c: kv_step(j, c, False), (one, one))
            carry = kv_step(i, carry, True)
            for hh in heads:
                m, l, acc = carry[hh]
                lanes = slice(hh * dh, (hh + 1) * dh)
                y_ref[r, lanes] = acc / l
                lse_ref[r, lanes] = jnp.broadcast_to(m + jnp.log(l), (tq, dh))
            return 0

        lax.fori_loop(0, nq, q_tile, 0)

    blk = lambda off: _once((seq, 128), lambda b, p: (b, off + p))
    return pl.pallas_call(
        body, grid=(nb, 4),
        in_specs=[blk(0), blk(4), blk(8), _once((seq, 128), lambda b, p: (b, 0)),
                  pl.BlockSpec((1, dh), lambda b, p: (0, 0)), pl.BlockSpec((1, dh), lambda b, p: (0, 0))],
        out_specs=[pl.BlockSpec((seq, 128), lambda b, p: (b, p)), pl.BlockSpec((seq, 128), lambda b, p: (b, p))],
        out_shape=[jax.ShapeDtypeStruct((t, 512), F32), jax.ShapeDtypeStruct((t, 512), F32)],
        scratch_shapes=[pltpu.VMEM((2, seq, dh), BF16), pltpu.VMEM((2, seq, dh), BF16), pltpu.VMEM((2, seq, dh), BF16),
                        pltpu.VMEM((2, seq, 1), F32), pltpu.VMEM((2 * nq, 1, tq), F32)],
        compiler_params=_cp("parallel", "arbitrary"), name="attn_fwd")(proj, proj, proj, gates, qw, kw)


def attn_bwd(proj, gates, qw, kw, y, lse, dy, seq, tq=256):
    t = proj.shape[0]
    nb, nq, dh = t // seq, seq // tq, FOX_DH
    scale = dh ** -0.5

    def body(q_ref, k_ref, v_ref, g_ref, qw_ref, kw_ref, y_ref, lse_ref, dy_ref,
             dq_ref, dk_ref, dv_ref, dg_ref, dqw_ref, dkw_ref,
             qs, ks, vs, dos, cols, crow, dqa, dka):
        b, p = pl.program_id(0), pl.program_id(1)

        @pl.when((b == 0) & (p == 0))
        def _():
            dqw_ref[...] = jnp.zeros_like(dqw_ref)
            dkw_ref[...] = jnp.zeros_like(dkw_ref)

        @pl.when(p == 0)
        def _():
            dg_ref[...] = jnp.zeros_like(dg_ref)

        heads = range(2)
        hl = lambda hh: slice(hh * dh, (hh + 1) * dh)

        def prep(i, _):
            r = _rows(i, tq)
            for hh in heads:
                lanes = hl(hh)
                qf, kf = q_ref[r, lanes], k_ref[r, lanes]
                qs[hh, r, :] = (qf * _rstd(qf) * qw_ref[...] * scale).astype(BF16)
                ks[hh, r, :] = (kf * _rstd(kf) * kw_ref[...]).astype(BF16)
                vs[hh, r, :] = v_ref[r, lanes].astype(BF16)
                dyf = dy_ref[r, lanes]
                dos[hh, r, :] = dyf.astype(BF16)
                cc = _pick_lane(g_ref[r, :], 2 * p + hh)
                crow[hh * nq + i] = _col_to_row(cc, tq)
                delta = jnp.sum(dyf * y_ref[r, lanes], axis=1, keepdims=True)
                lane = _lane((tq, 128))
                cols[hh, r, :] = jnp.where(lane == 0, cc, jnp.where(lane == 1, lse_ref[r, hh * dh:hh * dh + 1],
                                                                     jnp.where(lane == 2, delta, 0.0)))
                dqa[hh, r, :] = jnp.zeros((tq, dh), F32)
            return 0

        lax.fori_loop(0, nq, prep, 0)

        def kv_tile(j, _):
            kr = _rows(j, tq)
            kt = [ks[hh, kr, :] for hh in heads]
            vt = [vs[hh, kr, :] for hh in heads]
            cr = [crow[hh * nq + j] for hh in heads]

            def q_step(i, carry, masked):
                r = _rows(i, tq)
                out = []
                for hh in heads:
                    dk, dv, dcr = carry[hh]
                    qt, dot, cl = qs[hh, r, :], dos[hh, r, :], cols[hh, r, :]
                    s = _dot_nt(qt, kt[hh]) + (cl[:, 0:1] - cr[hh])
                    if masked:
                        s = jnp.where(_row((tq, tq)) >= _lane((tq, tq)), s, NEG)
                    pe = jnp.exp(s - cl[:, 1:2])
                    ds = pe * (_dot_nt(dot, vt[hh]) - cl[:, 2:3])
                    dsb = ds.astype(BF16)
                    dqa[hh, r, :] += _dot(dsb, kt[hh])
                    cols[hh, r, :] = cl + jnp.where(_lane((tq, 128)) == 3, jnp.sum(ds, axis=1, keepdims=True), 0.0)
                    out.append((dk + _dot_tn(dsb, qt), dv + _dot_tn(pe.astype(BF16), dot),
                                dcr - jnp.sum(ds, axis=0, keepdims=True)))
                return tuple(out)

            one = (jnp.zeros((tq, dh), F32), jnp.zeros((tq, dh), F32), jnp.zeros((1, tq), F32))
            carry = q_step(j, (one, one), True)
            carry = lax.fori_loop(j + 1, nq, lambda i, c: q_step(i, c, False), carry)
            for hh in heads:
                dk, dv, dcr = carry[hh]
                dka[hh, kr, :] = dk
                dv_ref[kr, hl(hh)] = dv
                dg_ref[kr, :] = jnp.where(_lane((tq, 128)) == 2 * p + hh, _row_to_col(dcr, tq), dg_ref[kr, :])
            return 0

        lax.fori_loop(0, nq, kv_tile, 0)

        def post(i, _):
            r = _rows(i, tq)
            for hh in heads:
                lanes = hl(hh)
                qf, kf = q_ref[r, lanes], k_ref[r, lanes]
                rq, rk = _rstd(qf), _rstd(kf)
                dqn, dkn = dqa[hh, r, :] * scale, dka[hh, r, :]
                dqw_ref[...] += jnp.sum(dqn * qf * rq, axis=0, keepdims=True)
                dkw_ref[...] += jnp.sum(dkn * kf * rk, axis=0, keepdims=True)
                dq_ref[r, lanes] = _rms_bwd(qf, rq, dqn * qw_ref[...])
                dk_ref[r, lanes] = _rms_bwd(kf, rk, dkn * kw_ref[...])
                dg_ref[r, :] += jnp.where(_lane((tq, 128)) == 2 * p + hh, cols[hh, r, 3:4], 0.0)
            return 0

        lax.fori_loop(0, nq, post, 0)

    blk = lambda off: _once((seq, 128), lambda b, p: (b, off + p))
    own = lambda: _once((seq, 128), lambda b, p: (b, p))
    vec = lambda: pl.BlockSpec((1, dh), lambda b, p: (0, 0))
    return pl.pallas_call(
        body, grid=(nb, 4),
        in_specs=[blk(0), blk(4), blk(8), _once((seq, 128), lambda b, p: (b, 0)), vec(), vec(), own(), own(), own()],
        out_specs=[own(), own(), own(), _once((seq, 128), lambda b, p: (b, 0)), vec(), vec()],
        out_shape=[jax.ShapeDtypeStruct((t, 512), F32)] * 3
                  + [jax.ShapeDtypeStruct((t, 128), F32), jax.ShapeDtypeStruct((1, dh), F32), jax.ShapeDtypeStruct((1, dh), F32)],
        scratch_shapes=[pltpu.VMEM((2, seq, dh), BF16)] * 4
                       + [pltpu.VMEM((2, seq, 128), F32), pltpu.VMEM((2 * nq, 1, tq), F32),
                          pltpu.VMEM((2, seq, dh), F32), pltpu.VMEM((2, seq, dh), F32)],
        compiler_params=_cp("arbitrary", "arbitrary"), name="attn_bwd")(proj, proj, proj, gates, qw, kw, y, lse, dy)


def _silu_grad(c, sg):
    return sg * (1.0 + c * (1.0 - sg))


def _conv(x, w, n):
    row = _row(x.shape)
    c = x * w[CONV_W - 1:CONV_W, :]
    for k in range(CONV_W - 1):
        sh = CONV_W - 1 - k
        c = c + w[k:k + 1, :] * jnp.where(row >= sh, pltpu.roll(x, sh, 0), 0.0)
    return c


def gdn_pre_fwd(proj, cw, seq):
    t = proj.shape[0]
    nb = t // seq
    scale = GDN_DH ** -0.5

    def body(xq_ref, xk_ref, xv_ref, wq_ref, wk_ref, wv_ref, q_ref, k_ref, v_ref):
        def act(x_ref, w_ref):
            c = _conv(x_ref[...], w_ref[...], seq)
            return c * jax.nn.sigmoid(c)

        aq, ak = act(xq_ref, wq_ref), act(xk_ref, wk_ref)
        q_ref[...] = aq * lax.rsqrt(jnp.sum(aq * aq, axis=1, keepdims=True) + EPS) * scale
        k_ref[...] = ak * lax.rsqrt(jnp.sum(ak * ak, axis=1, keepdims=True) + EPS)
        v_ref[...] = act(xv_ref, wv_ref)

    xb = lambda off: pl.BlockSpec((seq, 128), lambda b, h: (b, off + h))
    wb = lambda off: pl.BlockSpec((CONV_W, 128), lambda b, h: (0, off + h))
    ob = lambda: pl.BlockSpec((seq, 128), lambda b, h: (b, h))
    return pl.pallas_call(
        body, grid=(nb, GDN_HEADS),
        in_specs=[xb(12), xb(16), xb(20), wb(0), wb(4), wb(8)],
        out_specs=[ob(), ob(), ob()],
        out_shape=[jax.ShapeDtypeStruct((t, 512), F32)] * 3,
        compiler_params=_cp("parallel", "parallel"), name="gdn_pre_fwd")(proj, proj, proj, cw, cw, cw)


def gdn_pre_bwd(proj, cw, dq, dk, dv, seq):
    t = proj.shape[0]
    nb = t // seq
    scale = GDN_DH ** -0.5

    def body(xq_ref, xk_ref, xv_ref, wq_ref, wk_ref, wv_ref, dq_ref, dk_ref, dv_ref,
             dxq_ref, dxk_ref, dxv_ref, dwq_ref, dwk_ref, dwv_ref):
        first = pl.program_id(1) == 0
        row = _row((seq, 128))

        def one(x_ref, w_ref, dy_ref, dx_ref, dw_ref, norm, sc):
            x, w = x_ref[...], w_ref[...]
            c = _conv(x, w, seq)
            sg = jax.nn.sigmoid(c)
            dy = dy_ref[...]
            if norm:
                a = c * sg
                rs = lax.rsqrt(jnp.sum(a * a, axis=1, keepdims=True) + EPS)
                dy = dy * sc
                da = rs * dy - a * (rs * rs * rs) * jnp.sum(dy * a, axis=1, keepdims=True)
            else:
                da = dy
            dc = da * _silu_grad(c, sg)
            dx = dc * w[CONV_W - 1:CONV_W, :]
            dws = [None] * CONV_W
            dws[CONV_W - 1] = jnp.sum(dc * x, axis=0, keepdims=True)
            for k in range(CONV_W - 1):
                sh = CONV_W - 1 - k
                dx = dx + w[k:k + 1, :] * jnp.where(row < seq - sh, pltpu.roll(dc, seq - sh, 0), 0.0)
                dws[k] = jnp.sum(dc * jnp.where(row >= sh, pltpu.roll(x, sh, 0), 0.0), axis=0, keepdims=True)
            dx_ref[...] = dx
            dwn = jnp.concatenate(dws, axis=0)

            @pl.when(first)
            def _():
                dw_ref[...] = dwn

            @pl.when(jnp.logical_not(first))
            def _():
                dw_ref[...] += dwn

        one(xq_ref, wq_ref, dq_ref, dxq_ref, dwq_ref, True, scale)
        one(xk_ref, wk_ref, dk_ref, dxk_ref, dwk_ref, True, 1.0)
        one(xv_ref, wv_ref, dv_ref, dxv_ref, dwv_ref, False, 1.0)

    xb = lambda off: pl.BlockSpec((seq, 128), lambda h, b: (b, off + h))
    wb = lambda off: pl.BlockSpec((CONV_W, 128), lambda h, b: (0, off + h))
    ob = lambda: pl.BlockSpec((seq, 128), lambda h, b: (b, h))
    return pl.pallas_call(
        body, grid=(GDN_HEADS, nb),
        in_specs=[xb(12), xb(16), xb(20), wb(0), wb(4), wb(8), ob(), ob(), ob()],
        out_specs=[ob(), ob(), ob()] + [pl.BlockSpec((CONV_W, 128), lambda h, b: (0, h))] * 3,
        out_shape=[jax.ShapeDtypeStruct((t, 512), F32)] * 3 + [jax.ShapeDtypeStruct((CONV_W, 512), F32)] * 3,
        compiler_params=_cp("parallel", "arbitrary"), name="gdn_pre_bwd")(proj, proj, proj, cw, cw, cw, dq, dk, dv)


def _b16(x):
    return x.astype(BF16)


@jax.custom_vjp
def _mm(a, b):
    return _dot(_b16(a), _b16(b))


_mm.defvjp(lambda a, b: (_mm(a, b), (a, b)),
           lambda res, g: (_dot_nt(_b16(g), _b16(res[1])), _dot_tn(_b16(res[0]), _b16(g))))


@jax.custom_vjp
def _mm_nt(a, b):
    return _dot_nt(_b16(a), _b16(b))


_mm_nt.defvjp(lambda a, b: (_mm_nt(a, b), (a, b)),
              lambda res, g: (_dot(_b16(g), _b16(res[1])), _dot_tn(_b16(g), _b16(res[0]))))


@jax.custom_vjp
def _mm_tn(a, b):
    return _dot_tn(_b16(a), _b16(b))


_mm_tn.defvjp(lambda a, b: (_mm_tn(a, b), (a, b)),
              lambda res, g: (_dot_nt(_b16(res[1]), _b16(g)), _dot(_b16(res[0]), _b16(g))))


def _dot32(a, b, dims=(((1,), (0,)), ((), ()))):
    def split(x):
        hi = x.astype(BF16)
        return hi, (x - hi.astype(F32)).astype(BF16)

    (ah, al), (bh, bl) = split(a), split(b)
    d = lambda x, y: lax.dot_general(x, y, dims, preferred_element_type=F32)
    return d(ah, bh) + (d(ah, bl) + d(al, bh))


def _inv_fwd_many(mats):
    n = mats[0].shape[0]
    eye = (_row((n, n)) == _lane((n, n))).astype(F32)
    invs, pws = [eye - a for a in mats], list(mats)
    for _ in range(n.bit_length() - 2):
        pws = [_dot32(pw, pw) for pw in pws]
        invs = [inv + _dot32(inv, pw) for inv, pw in zip(invs, pws)]
    return invs


@jax.custom_vjp
def _inv_saved(a, inv):
    return inv


def _inv_saved_bwd(inv, g):
    tg = _dot32(inv, g, (((0,), (0,)), ((), ())))
    return -_dot32(tg, inv, (((1,), (1,)), ((), ()))), jnp.zeros_like(inv)


_inv_saved.defvjp(lambda a, inv: (inv, inv), _inv_saved_bwd)


def _gdn_decay(gcol):
    c = CHUNK
    ri, ci = _row((c, c)), _lane((c, c))
    incl, eye = ri >= ci, ri == ci
    grow = jnp.sum(jnp.where(eye, gcol, 0.0), axis=0, keepdims=True)
    gc = jnp.sum(jnp.where(incl, grow, 0.0), axis=1, keepdims=True)
    gcr = jnp.sum(jnp.where(eye, gc, 0.0), axis=0, keepdims=True)
    gl = jnp.sum(jnp.where(_row((c, 1)) == c - 1, gc, 0.0), axis=0, keepdims=True)
    return gc, gl, jnp.exp(jnp.where(incl, gc - gcr, NEG))


def _gdn_a(k, bcol, decay):
    c = CHUNK
    return jnp.where(_row((c, c)) > _lane((c, c)), _mm_nt(k * bcol, k) * decay, 0.0)


def _gdn_chunk(q, k, v, gcol, bcol, state, gg, nw, inv_saved):
    c = CHUNK
    incl = _row((c, c)) >= _lane((c, c))
    gc, gl, decay = _gdn_decay(gcol)
    kb, vb = k * bcol, v * bcol
    inv = _inv_saved(_gdn_a(k, bcol, decay), inv_saved)
    eg = jnp.exp(gc)
    u = _mm(inv, vb)
    w = _mm(inv, kb * eg)
    pm = jnp.where(incl, _mm_nt(q, k) * decay, 0.0)
    kd = k * jnp.exp(gl - gc)
    qd = q * eg
    v_new = u - _mm(w, state)
    o = _mm(qd, state) + _mm(pm, v_new)
    state_new = state * jnp.exp(gl) + _mm_tn(kd, v_new)
    y = o * _rstd(o) * nw * (gg * jax.nn.sigmoid(gg))
    return y, state_new


_gdn_chunks = jax.vmap(_gdn_chunk, in_axes=(0, 0, 0, 0, 0, 0, 0, None, 0))


def _gdn_chain_inputs(chains, p, r, c, q_ref, k_ref, v_ref, g_ref, gg_ref, inv_ref):
    cols = {nm: [] for nm in ("q", "k", "v", "g", "b", "gg", "inv")}
    for b, hh in chains:
        h = 2 * p + hh
        ln = slice(hh * 128, (hh + 1) * 128)
        gt = g_ref[b, r, :]
        cols["q"].append(q_ref[b, r, ln])
        cols["k"].append(k_ref[b, r, ln])
        cols["v"].append(v_ref[b, r, ln])
        cols["g"].append(_pick_lane(gt, 8 + h))
        cols["b"].append(_pick_lane(gt, 12 + h))
        cols["gg"].append(gg_ref[b, r, ln])
        cols["inv"].append(inv_ref[b, hh, c])
    return [jnp.stack(cols[nm]) for nm in ("q", "k", "v", "g", "b", "gg", "inv")]


GDN_CB = 8


def gdn_inv(k, gates, seq):
    t = k.shape[0]
    nb, nc = t // seq, seq // CHUNK
    rb = GDN_CB * CHUNK
    nsb = seq // rb

    def body(k_ref, g_ref, o_ref):
        h = pl.program_id(1)
        mats = []
        for c in range(GDN_CB):
            r = slice(c * CHUNK, (c + 1) * CHUNK)
            gt = g_ref[r, :]
            _, _, decay = _gdn_decay(_pick_lane(gt, 8 + h))
            mats.append(_gdn_a(k_ref[r, :], _pick_lane(gt, 12 + h), decay))
        for c, inv in enumerate(_inv_fwd_many(mats)):
            o_ref[c] = inv

    return pl.pallas_call(
        body, grid=(nb, GDN_HEADS, nsb),
        in_specs=[pl.BlockSpec((rb, 128), lambda b, h, s: (b * nsb + s, h)),
                  pl.BlockSpec((rb, 128), lambda b, h, s: (b * nsb + s, 0))],
        out_specs=pl.BlockSpec((None, None, GDN_CB, CHUNK, CHUNK), lambda b, h, s: (b, h, s, 0, 0)),
        out_shape=jax.ShapeDtypeStruct((nb, GDN_HEADS, nc, CHUNK, CHUNK), F32),
        compiler_params=_cp("parallel", "parallel", "parallel"), name="gdn_inv")(k, gates)


def _gdn_specs(nb, nsb, rev):
    blk = (lambda s: nsb - 1 - s) if rev else (lambda s: s)
    rb = GDN_CB * CHUNK
    pair = lambda off=0: pl.BlockSpec((nb, rb, 256), lambda s, p: (0, blk(s), off + p))
    gate = lambda: pl.BlockSpec((nb, rb, 128), lambda s, p: (0, blk(s), 0))
    mats = lambda n: pl.BlockSpec((nb, 2, GDN_CB, n, n), lambda s, p: (0, p, blk(s), 0, 0))
    return pair, gate, mats


def gdn_fwd(q, k, v, gates, proj, nw, inv, seq):
    t = q.shape[0]
    nb, nc = t // seq, seq // CHUNK
    nsb = nc // GDN_CB
    chains = [(b, hh) for b in range(nb) for hh in range(2)]
    nch = len(chains)
    pair, gate, mats = _gdn_specs(nb, nsb, False)

    def body(q_ref, k_ref, v_ref, g_ref, gg_ref, inv_ref, nw_ref, y_ref, st_ref, carry):
        s, p = pl.program_id(0), pl.program_id(1)

        @pl.when(s == 0)
        def _():
            for ci in range(nch):
                carry[p * nch + ci] = jnp.zeros((GDN_DH, GDN_DH), F32)

        def step(c, states):
            r = _rows(c, CHUNK)
            for ci, (b, hh) in enumerate(chains):
                st_ref[b, hh, c] = states[ci]
            ins = _gdn_chain_inputs(chains, p, r, c, q_ref, k_ref, v_ref, g_ref, gg_ref, inv_ref)
            y, states = _gdn_chunks(*ins[:5], states, ins[5], nw_ref[...], ins[6])
            for ci, (b, hh) in enumerate(chains):
                y_ref[b, r, hh * 128:(hh + 1) * 128] = y[ci]
            return states

        states = lax.fori_loop(0, GDN_CB, step, jnp.stack([carry[p * nch + ci] for ci in range(nch)]))
        for ci in range(nch):
            carry[p * nch + ci] = states[ci]

    v3 = lambda a: a.reshape(nb, seq, a.shape[1])
    y, st = pl.pallas_call(
        body, grid=(nsb, 2),
        in_specs=[pair(), pair(), pair(), gate(), pair(12), mats(CHUNK), pl.BlockSpec((1, 128), lambda s, p: (0, 0))],
        out_specs=[pair(), mats(GDN_DH)],
        out_shape=[jax.ShapeDtypeStruct((nb, seq, 512), F32),
                   jax.ShapeDtypeStruct((nb, GDN_HEADS, nc, GDN_DH, GDN_DH), F32)],
        scratch_shapes=[pltpu.VMEM((2 * nch, GDN_DH, GDN_DH), F32)],
        compiler_params=_cp("arbitrary", "arbitrary"), name="gdn_fwd")(v3(q), v3(k), v3(v), v3(gates), v3(proj), inv, nw)
    return y.reshape(t, 512), st


def gdn_bwd(q, k, v, gates, proj, nw, inv, states, dy, seq):
    t = q.shape[0]
    nb, nc = t // seq, seq // CHUNK
    nsb = nc // GDN_CB
    chains = [(b, hh) for b in range(nb) for hh in range(2)]
    nch = len(chains)
    pair, gate, mats = _gdn_specs(nb, nsb, True)

    def body(q_ref, k_ref, v_ref, g_ref, gg_ref, inv_ref, st_ref, dy_ref, nw_ref,
             dq_ref, dk_ref, dv_ref, dgg_ref, dg_ref, dnw_ref, carry):
        s, p = pl.program_id(0), pl.program_id(1)

        @pl.when((s == 0) & (p == 0))
        def _():
            dnw_ref[...] = jnp.zeros_like(dnw_ref)

        @pl.when(p == 0)
        def _():
            dg_ref[...] = jnp.zeros_like(dg_ref)

        @pl.when(s == 0)
        def _():
            for ci in range(nch):
                carry[p * nch + ci] = jnp.zeros((GDN_DH, GDN_DH), F32)

        def step(idx, dstates):
            c = GDN_CB - 1 - idx
            r = _rows(c, CHUNK)
            ins = _gdn_chain_inputs(chains, p, r, c, q_ref, k_ref, v_ref, g_ref, gg_ref, inv_ref)
            st = jnp.stack([st_ref[b, hh, c] for b, hh in chains])
            dy = jnp.stack([dy_ref[b, r, hh * 128:(hh + 1) * 128] for b, hh in chains])
            _, vjp = jax.vjp(_gdn_chunks, *ins[:5], st, ins[5], nw_ref[...], ins[6])
            dq, dk, dv, dgc, dbc, dstates, dgg, dnw, _ = vjp((dy, dstates))
            dnw_ref[...] += dnw
            lane = _lane((CHUNK, 128))
            for ci, (b, hh) in enumerate(chains):
                h = 2 * p + hh
                ln = slice(hh * 128, (hh + 1) * 128)
                dq_ref[b, r, ln] = dq[ci]
                dk_ref[b, r, ln] = dk[ci]
                dv_ref[b, r, ln] = dv[ci]
                dgg_ref[b, r, ln] = dgg[ci]
                dg_ref[b, r, :] = jnp.where(lane == 8 + h, dgc[ci], jnp.where(lane == 12 + h, dbc[ci], dg_ref[b, r, :]))
            return dstates

        dstates = lax.fori_loop(0, GDN_CB, step, jnp.stack([carry[p * nch + ci] for ci in range(nch)]))
        for ci in range(nch):
            carry[p * nch + ci] = dstates[ci]

    v3 = lambda a: a.reshape(nb, seq, a.shape[1])
    res = pl.pallas_call(
        body, grid=(nsb, 2),
        in_specs=[pair(), pair(), pair(), gate(), pair(12), mats(CHUNK), mats(GDN_DH), pair(),
                  pl.BlockSpec((1, 128), lambda s, p: (0, 0))],
        out_specs=[pair(), pair(), pair(), pair(), gate(), pl.BlockSpec((1, 128), lambda s, p: (0, 0))],
        out_shape=[jax.ShapeDtypeStruct((nb, seq, 512), F32)] * 4
                  + [jax.ShapeDtypeStruct((nb, seq, 128), F32), jax.ShapeDtypeStruct((1, 128), F32)],
        scratch_shapes=[pltpu.VMEM((2 * nch, GDN_DH, GDN_DH), F32)],
        compiler_params=_cp("arbitrary", "arbitrary"),
        name="gdn_bwd")(v3(q), v3(k), v3(v), v3(gates), v3(proj), inv, states, v3(dy), nw)
    return [a.reshape(t, a.shape[2]) for a in res[:5]] + [res[5]]


def loss_head(y, target, tm=512):
    t, d = y.shape

    def body(y_ref, t_ref, s_ref, dy_ref):
        @pl.when(pl.program_id(0) == 0)
        def _():
            s_ref[...] = jnp.zeros_like(s_ref)

        err = y_ref[...] - t_ref[...]
        s_ref[...] += jnp.sum(err * err, axis=0, keepdims=True)
        dy_ref[...] = err * (1.0 / d)

    return pl.pallas_call(
        body, grid=(t // tm,),
        in_specs=[pl.BlockSpec((tm, d), lambda i: (i, 0)), pl.BlockSpec((tm, d), lambda i: (i, 0))],
        out_specs=[pl.BlockSpec((1, d), lambda i: (0, 0)), pl.BlockSpec((tm, d), lambda i: (i, 0))],
        out_shape=[jax.ShapeDtypeStruct((1, d), F32), jax.ShapeDtypeStruct((t, d), F32)],
        compiler_params=_cp("arbitrary"), name="loss_head")(y, target)


def _place():
    return lax.axis_index("x"), lax.axis_index("y"), lax.axis_index("c")


def _peer(k):
    x, y, c = _place()
    px = 1 - x if (k >> 2) & 1 else x
    py = 1 - y if (k >> 1) & 1 else y
    pc = 1 - c if k & 1 else c
    return (px, py, pc), 4 * px + 2 * py + pc


_ANY = pl.BlockSpec(memory_space=pl.ANY)
_SEM = pl.BlockSpec(memory_space=pltpu.SEMAPHORE)
_EFFECT = pltpu.SideEffectType.DATAFLOW_SIDE_EFFECTING


def _me():
    x, y, c = _place()
    return 4 * x + 2 * y + c


def _remote_copy(ins, lands, scatter, send_sems, recv_sems, a, k, arriving):
    pid, pidx = _peer(k)
    return pltpu.make_async_remote_copy(src_ref=ins[a].at[pidx] if scatter[a] else ins[a],
                                        dst_ref=lands[a].at[pidx if arriving else _me()],
                                        send_sem=send_sems.at[a * N_DEV + k], recv_sem=recv_sems.at[a * N_DEV + k],
                                        device_id=pid, device_id_type=MESH)


def _local_copy(ins, lands, scatter, loc_sems, a):
    me = _me()
    return pltpu.make_async_copy(ins[a].at[me] if scatter[a] else ins[a], lands[a].at[me], loc_sems.at[a])


def exchange_start(arrays, scatter, name, after):
    n = len(arrays)
    lands = [lax.empty(a.shape if s else (N_DEV,) + a.shape, a.dtype) for a, s in zip(arrays, scatter)]

    def body(*refs):
        ins, lds = refs[:n], refs[n:2 * n]
        send_sems, recv_sems, loc_sems = refs[2 * n + 1:2 * n + 4]
        token = refs[-1]
        for k in range(1, N_DEV):
            for a in range(n):
                _remote_copy(ins, lds, scatter, send_sems, recv_sems, a, k, False).start()
        for a in range(n):
            _local_copy(ins, lds, scatter, loc_sems, a).start()
        token[...] = jnp.zeros_like(token)

    hbm = lambda a: pltpu.HBM(a.shape, a.dtype)
    res = pl.pallas_call(
        body, name=name,
        in_specs=[_ANY] * (2 * n + 1),
        out_specs=[_SEM, _SEM, _SEM] + [_ANY] * (2 * n) + [pl.BlockSpec(memory_space=pltpu.VMEM)],
        out_shape=[pltpu.SemaphoreType.DMA((n * N_DEV,)), pltpu.SemaphoreType.DMA((n * N_DEV,)),
                   pltpu.SemaphoreType.DMA((n,))]
                  + [hbm(a) for a in arrays] + [hbm(a) for a in lands] + [jax.ShapeDtypeStruct((8, 128), F32)],
        input_output_aliases={i: 3 + i for i in range(2 * n)},
        compiler_params=pltpu.CompilerParams(has_side_effects=_EFFECT),
    )(*[pltpu.with_memory_space_constraint(a, pltpu.HBM) for a in list(arrays) + lands], after)
    return res[0:3], res[3:3 + n], res[3 + n:3 + 2 * n], res[-1]


def exchange_wait(sems, arrays, lands, scatter, after, name):
    n = len(arrays)

    def body(*refs):
        ins, lds = refs[:n], refs[n:2 * n]
        ssem, rsem, lsem = refs[2 * n:2 * n + 3]
        for a in range(n):
            _local_copy(ins, lds, scatter, lsem, a).wait()
        for k in range(1, N_DEV):
            for a in range(n):
                _remote_copy(ins, lds, scatter, ssem, rsem, a, k, True).wait_recv()
        for k in range(1, N_DEV):
            for a in range(n):
                _remote_copy(ins, lds, scatter, ssem, rsem, a, k, False).wait_send()

    hbm = lambda a: pltpu.HBM(a.shape, a.dtype)
    res = pl.pallas_call(
        body, name=name,
        in_specs=[_ANY] * (2 * n) + [_SEM, _SEM, _SEM, _ANY],
        out_specs=[_ANY] * (2 * n),
        out_shape=[hbm(a) for a in arrays] + [hbm(a) for a in lands],
        input_output_aliases={i: i for i in range(2 * n)},
        compiler_params=pltpu.CompilerParams(has_side_effects=_EFFECT),
    )(*arrays, *lands, *sems, after)
    return list(res[n:])


def exchange_begin(arrays, scatter, name, after):
    sems, arrays_thru, lands_thru, token = exchange_start(arrays, scatter, name + "_start", after)
    return (sems, arrays_thru, lands_thru, scatter, name), token


def exchange_end(state, after):
    sems, arrays_thru, lands_thru, scatter, name = state
    return exchange_wait(sems, arrays_thru, lands_thru, scatter, after, name + "_wait")


def adamw_reduce(slots, w, m, v, name, after=None):
    r, c = w.shape
    tr = r
    while tr * c * 4 > (1 << 20) and tr % 16 == 0:
        tr //= 2
    bc1 = 1.0 - ADAM_B1 ** ADAM_STEP
    bc2 = 1.0 - ADAM_B2 ** ADAM_STEP

    def body(s_ref, w_ref, m_ref, v_ref, *rest):
        g_ref, d_ref, nm_ref, nv_ref = rest[-4:]
        g = s_ref[0]
        for j in range(1, N_DEV):
            g = g + s_ref[j]
        nm = ADAM_B1 * m_ref[...] + (1.0 - ADAM_B1) * g
        nv = ADAM_B2 * v_ref[...] + (1.0 - ADAM_B2) * (g * g)
        g_ref[...] = g
        nm_ref[...] = nm
        nv_ref[...] = nv
        d_ref[...] = -ADAM_LR * ((nm / bc1) / (jnp.sqrt(nv / bc2) + ADAM_EPS) + ADAM_WD * w_ref[...])

    blk = lambda: pl.BlockSpec((tr, c), lambda i: (i, 0))
    extra = [] if after is None else [after]
    return pl.pallas_call(
        body, grid=(r // tr,),
        in_specs=[pl.BlockSpec((N_DEV, tr, c), lambda i: (0, i, 0)), blk(), blk(), blk()] + [_ANY] * len(extra),
        out_specs=[blk(), blk(), blk(), blk()],
        out_shape=[jax.ShapeDtypeStruct((r, c), F32)] * 4,
        compiler_params=_cp("parallel"), name=name)(slots, w, m, v, *extra)


BIG = ("ffn1_w_in", "ffn1_w_out", "w_in", "gdn_conv", "w_out", "ffn2_w_in", "ffn2_w_out")
SMALL = ("ffn1_norm", "mix_norm", "fox_q_norm", "fox_k_norm", "fox_f_bias", "gdn_a_log", "gdn_dt_bias",
         "gdn_out_norm", "ffn2_norm")
WEIGHTS = ("ffn1_norm", "ffn1_w_in", "ffn1_w_out", "mix_norm", "w_in", "fox_q_norm", "fox_k_norm", "fox_f_bias",
           "gdn_conv", "gdn_a_log", "gdn_dt_bias", "gdn_out_norm", "w_out", "ffn2_norm", "ffn2_w_in", "ffn2_w_out")
IN_COLS = (("fq", 512), ("fk", 512), ("fv", 512), ("ff", 8), ("gq", 512), ("gk", 512), ("gv", 512),
           ("ga", 4), ("gb", 4), ("gg", 512))
MY_BIG = ("fq", "fk", "fv", "gq", "gk", "gv", "gg")
MY_SMALL = ("ff", "ga", "gb")
SMALL_ROWS = 8 * 128


def _in_cols_to_mine(w):
    off, parts = 0, {}
    for nm, wd in IN_COLS:
        parts[nm] = w[:, off:off + wd]
        off += wd
    small = jnp.concatenate([parts[nm] for nm in MY_SMALL], axis=1)
    small = jnp.pad(small, ((0, 0), (0, 128 - small.shape[1])))
    return jnp.concatenate([parts[nm] for nm in MY_BIG] + [small], axis=1)


def _in_cols_from_mine(g):
    parts = {nm: g[:, i * 512:(i + 1) * 512] for i, nm in enumerate(MY_BIG)}
    off = N_BIG
    for nm in MY_SMALL:
        wd = dict(IN_COLS)[nm]
        parts[nm] = g[:, off:off + wd]
        off += wd
    return jnp.concatenate([parts[nm] for nm, _ in IN_COLS], axis=1)


def _pack_small(vals):
    rows = []
    nl = vals[SMALL[0]].shape[0]
    for l in range(nl):
        for nm in SMALL:
            v = vals[nm][l].reshape(-1)
            pad = (-v.shape[0]) % SMALL_ROWS
            rows.append(jnp.pad(v, (0, pad)).reshape(-1, 128))
    return jnp.concatenate(rows, axis=0)


def _unpack_small(packed, like):
    out = {nm: [] for nm in SMALL}
    row = 0
    nl = like[SMALL[0]].shape[0]
    for l in range(nl):
        for nm in SMALL:
            n = like[nm].shape[1]
            nr = -(-n // SMALL_ROWS) * 8
            out[nm].append(packed[row:row + nr].reshape(-1)[:n])
            row += nr
    return {nm: jnp.stack(v) for nm, v in out.items()}


def kernel(x, ffn1_norm, ffn1_w_in, ffn1_w_out, mix_norm, w_in, fox_q_norm, fox_k_norm, fox_f_bias, gdn_conv, gdn_a_log, gdn_dt_bias, gdn_out_norm, w_out, ffn2_norm, ffn2_w_in, ffn2_w_out, loss_target, m_ffn1_norm, m_ffn1_w_in, m_ffn1_w_out, m_mix_norm, m_w_in, m_fox_q_norm, m_fox_k_norm, m_fox_f_bias, m_gdn_conv, m_gdn_a_log, m_gdn_dt_bias, m_gdn_out_norm, m_w_out, m_ffn2_norm, m_ffn2_w_in, m_ffn2_w_out, v_ffn1_norm, v_ffn1_w_in, v_ffn1_w_out, v_mix_norm, v_w_in, v_fox_q_norm, v_fox_k_norm, v_fox_f_bias, v_gdn_conv, v_gdn_a_log, v_gdn_dt_bias, v_gdn_out_norm, v_w_out, v_ffn2_norm, v_ffn2_w_in, v_ffn2_w_out):
    wts = dict(ffn1_norm=ffn1_norm, ffn1_w_in=ffn1_w_in, ffn1_w_out=ffn1_w_out, mix_norm=mix_norm, w_in=w_in,
               fox_q_norm=fox_q_norm, fox_k_norm=fox_k_norm, fox_f_bias=fox_f_bias, gdn_conv=gdn_conv,
               gdn_a_log=gdn_a_log, gdn_dt_bias=gdn_dt_bias, gdn_out_norm=gdn_out_norm, w_out=w_out,
               ffn2_norm=ffn2_norm, ffn2_w_in=ffn2_w_in, ffn2_w_out=ffn2_w_out)
    mom = dict(ffn1_norm=m_ffn1_norm, ffn1_w_in=m_ffn1_w_in, ffn1_w_out=m_ffn1_w_out, mix_norm=m_mix_norm, w_in=m_w_in,
               fox_q_norm=m_fox_q_norm, fox_k_norm=m_fox_k_norm, fox_f_bias=m_fox_f_bias, gdn_conv=m_gdn_conv,
               gdn_a_log=m_gdn_a_log, gdn_dt_bias=m_gdn_dt_bias, gdn_out_norm=m_gdn_out_norm, w_out=m_w_out,
               ffn2_norm=m_ffn2_norm, ffn2_w_in=m_ffn2_w_in, ffn2_w_out=m_ffn2_w_out)
    var = dict(ffn1_norm=v_ffn1_norm, ffn1_w_in=v_ffn1_w_in, ffn1_w_out=v_ffn1_w_out, mix_norm=v_mix_norm, w_in=v_w_in,
               fox_q_norm=v_fox_q_norm, fox_k_norm=v_fox_k_norm, fox_f_bias=v_fox_f_bias, gdn_conv=v_gdn_conv,
               gdn_a_log=v_gdn_a_log, gdn_dt_bias=v_gdn_dt_bias, gdn_out_norm=v_gdn_out_norm, w_out=v_w_out,
               ffn2_norm=v_ffn2_norm, ffn2_w_in=v_ffn2_w_in, ffn2_w_out=v_ffn2_w_out)
    nb, seq, d = x.shape
    t = nb * seq
    depth = ffn1_norm.shape[0]

    def shards_of(l):
        return [wts[nm][l] if nm == "gdn_conv" else wts[nm][l].astype(BF16) for nm in BIG]

    def behind(nw, token):
        return nw + token[0:1, 0:1]

    def layer_weights(l, g):
        g = dict(zip(BIG, g))
        fb = g["ffn1_w_in"].shape[2]
        return dict(
            w1i=g["ffn1_w_in"].reshape(2, 4, d, fb), w1o=g["ffn1_w_out"].reshape(4, fb, d),
            w2i=g["ffn2_w_in"].reshape(2, 4, d, fb), w2o=g["ffn2_w_out"].reshape(4, fb, d),
            wi=_in_cols_to_mine(g["w_in"].transpose(1, 0, 2).reshape(d, -1)),
            cw=g["gdn_conv"].transpose(1, 0, 2).reshape(CONV_W, -1),
            wo=g["w_out"].reshape(d, d),
            n1=ffn1_norm[l][None], nmix=mix_norm[l][None], n2=ffn2_norm[l][None],
            qw=fox_q_norm[l][None], kw=fox_k_norm[l][None], onw=gdn_out_norm[l][None],
            gp=jnp.concatenate([
                jnp.concatenate([fox_f_bias[l], gdn_dt_bias[l], jnp.zeros((116,), F32)])[None],
                jnp.concatenate([jnp.zeros((8,), F32), gdn_a_log[l], jnp.zeros((116,), F32)])[None],
                jnp.zeros((6, 128), F32)], axis=0))

    h = x.reshape(t, d)
    gather_flags = [False] * len(BIG)
    state, token = exchange_begin(shards_of(0), gather_flags, "gather_weights_0", ffn1_norm)
    landed = exchange_end(state, token)
    saved = []
    for l in range(depth):
        p = layer_weights(l, landed)
        if l + 1 < depth:
            state, token = exchange_begin(shards_of(l + 1), gather_flags, f"gather_weights_{l + 1}", landed[0])
            p["n1"] = behind(p["n1"], token)
        x0 = h
        x1 = ffn_fwd(x0, p["n1"], p["w1i"], p["w1o"])
        proj, hn = inproj_fwd(x1, p["nmix"], p["wi"])
        gates = gates_fwd(proj, p["gp"], seq)
        yf, lse = attn_fwd(proj, gates, p["qw"], p["kw"], seq, tq=min(seq, ATTN_TQ_FWD))
        qh, kh, vh = gdn_pre_fwd(proj, p["cw"], seq)
        inv = gdn_inv(kh, gates, seq)
        yg, st = gdn_fwd(qh, kh, vh, gates, proj, p["onw"], inv, seq)
        x2, ycat = outproj_fwd(x1, yf, yg, p["wo"])
        h = ffn_fwd(x2, p["n2"], p["w2i"], p["w2o"])
        saved.append(dict(p=p, x0=x0, x1=x1, x2=x2, proj=proj, hn=hn, gates=gates, yf=yf, lse=lse,
                          qh=qh, kh=kh, vh=vh, st=st, inv=inv, ycat=ycat))
        if l + 1 < depth:
            landed = exchange_end(state, h)

    sq, dh = loss_head(h, loss_target.reshape(t, d))
    loss = lax.psum(0.5 * jnp.sum(sq) / d, ("x", "y", "c"))

    gbig = [None] * depth
    got = [None] * depth
    pending = None
    gsmall = {nm: [None] * depth for nm in SMALL}
    for l in reversed(range(depth)):
        s = saved[l]
        p = s["p"]
        n2 = p["n2"] if pending is None else behind(p["n2"], token)
        dx2, dn2, dgu, hh, xn, dyh = ffn_bwd(s["x2"], dh, n2, p["w2i"], p["w2o"])
        g_w2i = wgrad_ffn_in(xn, dgu)
        g_w2o = wgrad_ffn_out(hh, dyh)
        dyf, dyg, dyb = outproj_bwd(dx2, p["wo"])
        g_wo = wgrad_2d(s["ycat"], dyb, 512, "wgrad_w_out")
        dq, dk, dv, dga, dqw, dkw = attn_bwd(s["proj"], s["gates"], p["qw"], p["kw"], s["yf"], s["lse"], dyf, seq,
                                             tq=min(seq, ATTN_TQ_BWD))
        dqh, dkh, dvh, dgg, dgb, donw = gdn_bwd(s["qh"], s["kh"], s["vh"], s["gates"], s["proj"], p["onw"],
                                                 s["inv"], s["st"], dyg, seq)
        dxq, dxk, dxv, dwq, dwk, dwv = gdn_pre_bwd(s["proj"], p["cw"], dqh, dkh, dvh, seq)
        dsm, dgp = gates_bwd(s["proj"], p["gp"], dga, dgb, seq)
        dx1, dnmix, dproj = inproj_bwd(s["x1"], dx2, p["nmix"], p["wi"], [dq, dk, dv, dxq, dxk, dxv, dgg, dsm])
        g_wi = wgrad_2d(s["hn"], dproj, 512, "wgrad_w_in")
        dh, dn1, dgu, hh, xn, dyh = ffn_bwd(s["x0"], dx1, p["n1"], p["w1i"], p["w1o"])
        g_w1i = wgrad_ffn_in(xn, dgu)
        g_w1o = wgrad_ffn_out(hh, dyh)
        fb = g_w1i.shape[3]
        g_cw = jnp.concatenate([dwq, dwk, dwv], axis=1)
        gbig[l] = dict(
            ffn1_w_in=g_w1i.reshape(N_DEV, d, fb), ffn1_w_out=g_w1o.reshape(N_DEV, -1, d),
            w_in=_in_cols_from_mine(g_wi).reshape(d, N_DEV, -1).transpose(1, 0, 2),
            gdn_conv=g_cw.reshape(CONV_W, N_DEV, -1).transpose(1, 0, 2),
            w_out=g_wo.reshape(N_DEV, -1, d),
            ffn2_w_in=g_w2i.reshape(N_DEV, d, fb), ffn2_w_out=g_w2o.reshape(N_DEV, -1, d))
        for nm, val in (("ffn1_norm", dn1[0]), ("mix_norm", dnmix[0]), ("fox_q_norm", dqw[0]), ("fox_k_norm", dkw[0]),
                        ("fox_f_bias", dgp[0, 0:8]), ("gdn_a_log", dgp[1, 8:12]), ("gdn_dt_bias", dgp[0, 8:12]),
                        ("gdn_out_norm", donw[0]), ("ffn2_norm", dn2[0])):
            gsmall[nm][l] = val
        send, flags = [gbig[l][nm] for nm in BIG], [True] * len(BIG)
        if l == 0:
            send.append(_pack_small({nm: jnp.stack(v) for nm, v in gsmall.items()}))
            flags.append(False)
        prev = dh
        if pending is not None:
            got[pending[1]] = exchange_end(pending[0], dh)
            prev = got[pending[1]][0]
        state, token = exchange_begin(send, flags, f"exchange_grads_{l}", prev)
        pending = (state, l)
    grad_x = dh.reshape(nb, seq, d)

    def update_layer(l, slots, after):
        out = {}
        for i, nm in enumerate(BIG):
            r, c = wts[nm].shape[1:]
            out[nm] = adamw_reduce(slots[i].reshape(N_DEV, r, c), wts[nm][l], mom[nm][l], var[nm][l], f"adamw_{nm}_{l}",
                                   after)
            if after is not None:
                after = out[nm][0]
        return out

    per_layer = [None] * depth
    last = token
    for l in range(1, depth):
        per_layer[l] = update_layer(l, got[l], token)
        last = per_layer[l][BIG[-1]][0]
    got[0] = exchange_end(pending[0], last)
    per_layer[0] = update_layer(0, got[0], None)
    res = {nm: [jnp.stack([per_layer[l][nm][j] for l in range(depth)]) for j in range(4)] for nm in BIG}
    small_like = {nm: wts[nm] for nm in SMALL}
    sm = adamw_reduce(got[0][-1], _pack_small(small_like), _pack_small({nm: mom[nm] for nm in SMALL}),
                      _pack_small({nm: var[nm] for nm in SMALL}), "adamw_small")
    sm = [_unpack_small(a, small_like) for a in sm]
    for nm in SMALL:
        res[nm] = [sm[j][nm] for j in range(4)]
    return (loss, grad_x, *[res[nm][0] for nm in WEIGHTS], *[res[nm][1] for nm in WEIGHTS],
            *[res[nm][2] for nm in WEIGHTS], *[res[nm][3] for nm in WEIGHTS])
```

```python
import functools

import jax
import jax.numpy as jnp
from jax import lax
from jax.experimental import pallas as pl
from jax.experimental.pallas import tpu as pltpu

F32 = jnp.float32
BF16 = jnp.bfloat16
EPS = 1e-6
N_DEV = 8
MESH = pl.DeviceIdType.MESH
HIGHEST = lax.Precision.HIGHEST
VMEM_LIMIT = 56 * 1024 * 1024

FOX_HEADS, FOX_DH = 8, 64
GDN_HEADS, GDN_DH = 4, 128
CHUNK = 64
CONV_W = 4

ADAM_LR, ADAM_B1, ADAM_B2, ADAM_EPS, ADAM_WD, ADAM_STEP = 0.001, 0.9, 0.999, 1e-08, 0.01, 10


def _cp(*sem):
    return pltpu.CompilerParams(dimension_semantics=sem, vmem_limit_bytes=VMEM_LIMIT)


def _dot(a, b):
    return jnp.dot(a, b, preferred_element_type=F32)


def _dot_nt(a, b):
    return lax.dot_general(a, b, (((1,), (1,)), ((), ())), preferred_element_type=F32)


def _dot_tn(a, b):
    return lax.dot_general(a, b, (((0,), (0,)), ((), ())), preferred_element_type=F32)


def _rstd(xf):
    return lax.rsqrt(jnp.mean(xf * xf, axis=-1, keepdims=True) + EPS)


def _rms_bwd(xf, r, dyn):
    return r * dyn - xf * (r * r * r) * jnp.mean(dyn * xf, axis=-1, keepdims=True)


def ffn_fwd(x, nw, w_in, w_out, tm=512):
    t, d = x.shape
    nj, fb = w_out.shape[0], w_out.shape[1]

    def body(x_ref, nw_ref, wi_ref, wo_ref, o_ref, xn_ref, acc_ref):
        j = pl.program_id(1)

        @pl.when(j == 0)
        def _():
            xf = x_ref[...]
            xn_ref[...] = (xf * _rstd(xf) * nw_ref[...]).astype(BF16)
            acc_ref[...] = jnp.zeros_like(acc_ref)

        xn = xn_ref[...]
        g = _dot(xn, wi_ref[0])
        u = _dot(xn, wi_ref[1])
        h = (g * jax.nn.sigmoid(g) * u).astype(BF16)
        acc_ref[...] += _dot(h, wo_ref[...])

        @pl.when(j == nj - 1)
        def _():
            o_ref[...] = x_ref[...] + 0.5 * acc_ref[...]

    return pl.pallas_call(
        body, grid=(t // tm, nj),
        in_specs=[pl.BlockSpec((tm, d), lambda i, j: (i, 0)),
                  pl.BlockSpec((1, d), lambda i, j: (0, 0)),
                  pl.BlockSpec((2, None, d, fb), lambda i, j: (0, j, 0, 0)),
                  pl.BlockSpec((None, fb, d), lambda i, j: (j, 0, 0))],
        out_specs=pl.BlockSpec((tm, d), lambda i, j: (i, 0)),
        out_shape=jax.ShapeDtypeStruct((t, d), F32),
        scratch_shapes=[pltpu.VMEM((tm, d), BF16), pltpu.VMEM((tm, d), F32)],
        compiler_params=_cp("parallel", "arbitrary"), name="ffn_fwd")(x, nw, w_in, w_out)


def ffn_bwd(x, dy, nw, w_in, w_out, tm=512):
    t, d = x.shape
    nj, fb = w_out.shape[0], w_out.shape[1]

    def body(x_ref, dy_ref, nw_ref, wi_ref, wo_ref,
             dx_ref, dnw_ref, dgu_ref, h_ref, xn_ref, dyh_ref, acc_ref):
        i, j = pl.program_id(0), pl.program_id(1)

        @pl.when(j == 0)
        def _():
            xf = x_ref[...]
            xn_ref[...] = (xf * _rstd(xf) * nw_ref[...]).astype(BF16)
            dyh_ref[...] = (0.5 * dy_ref[...]).astype(BF16)
            acc_ref[...] = jnp.zeros_like(acc_ref)

        @pl.when((i == 0) & (j == 0))
        def _():
            dnw_ref[...] = jnp.zeros_like(dnw_ref)

        xn = xn_ref[...]
        g = _dot(xn, wi_ref[0])
        u = _dot(xn, wi_ref[1])
        sg = jax.nn.sigmoid(g)
        silu = g * sg
        dh = _dot_nt(dyh_ref[...], wo_ref[...])
        dg = (dh * u * (sg * (1.0 + g * (1.0 - sg)))).astype(BF16)
        du = (dh * silu).astype(BF16)
        dgu_ref[0] = dg
        dgu_ref[1] = du
        h_ref[...] = (silu * u).astype(BF16)
        acc_ref[...] += _dot_nt(dg, wi_ref[0]) + _dot_nt(du, wi_ref[1])

        @pl.when(j == nj - 1)
        def _():
            xf = x_ref[...]
            r = _rstd(xf)
            dxn = acc_ref[...]
            dnw_ref[...] += jnp.sum(dxn * xf * r, axis=0, keepdims=True)
            dx_ref[...] = _rms_bwd(xf, r, dxn * nw_ref[...]) + dy_ref[...]

    return pl.pallas_call(
        body, grid=(t // tm, nj),
        in_specs=[pl.BlockSpec((tm, d), lambda i, j: (i, 0)),
                  pl.BlockSpec((tm, d), lambda i, j: (i, 0)),
                  pl.BlockSpec((1, d), lambda i, j: (0, 0)),
                  pl.BlockSpec((2, None, d, fb), lambda i, j: (0, j, 0, 0)),
                  pl.BlockSpec((None, fb, d), lambda i, j: (j, 0, 0))],
        out_specs=[pl.BlockSpec((tm, d), lambda i, j: (i, 0)),
                   pl.BlockSpec((1, d), lambda i, j: (0, 0)),
                   pl.BlockSpec((2, None, tm, fb), lambda i, j: (0, j, i, 0)),
                   pl.BlockSpec((None, tm, fb), lambda i, j: (j, i, 0)),
                   pl.BlockSpec((tm, d), lambda i, j: (i, 0)),
                   pl.BlockSpec((tm, d), lambda i, j: (i, 0))],
        out_shape=[jax.ShapeDtypeStruct((t, d), F32),
                   jax.ShapeDtypeStruct((1, d), F32),
                   jax.ShapeDtypeStruct((2, nj, t, fb), BF16),
                   jax.ShapeDtypeStruct((nj, t, fb), BF16),
                   jax.ShapeDtypeStruct((t, d), BF16),
                   jax.ShapeDtypeStruct((t, d), BF16)],
        scratch_shapes=[pltpu.VMEM((tm, d), F32)],
        compiler_params=_cp("arbitrary", "arbitrary"), name="ffn_bwd")(x, dy, nw, w_in, w_out)


def _wgrad_call(a, b, a_spec, b_spec, out_shape, out_spec, grid, name, out_dtype=BF16):
    last = len(grid) - 1
    acc_shape = tuple(s for s in out_spec.block_shape if s is not None)

    def body(a_ref, b_ref, o_ref, acc_ref):
        @pl.when(pl.program_id(last) == 0)
        def _():
            acc_ref[...] = jnp.zeros_like(acc_ref)

        acc_ref[...] += _dot_tn(a_ref[...], b_ref[...])

        @pl.when(pl.program_id(last) == grid[last] - 1)
        def _():
            o_ref[...] = acc_ref[...].astype(o_ref.dtype)

    sem = ("parallel",) * last + ("arbitrary",)
    return pl.pallas_call(body, grid=grid, in_specs=[a_spec, b_spec], out_specs=out_spec,
                          out_shape=jax.ShapeDtypeStruct(out_shape, out_dtype),
                          scratch_shapes=[pltpu.VMEM(acc_shape, F32)],
                          compiler_params=_cp(*sem), name=name)(a, b)


def wgrad_ffn_in(xn, dgu, tm=512):
    t, d = xn.shape
    _, nj, _, fb = dgu.shape
    return _wgrad_call(xn, dgu,
                       pl.BlockSpec((tm, d), lambda p, j, k: (k, 0)),
                       pl.BlockSpec((None, None, tm, fb), lambda p, j, k: (p, j, k, 0)),
                       (2, nj, d, fb), pl.BlockSpec((None, None, d, fb), lambda p, j, k: (p, j, 0, 0)),
                       (2, nj, t // tm), "wgrad_ffn_in")


def wgrad_ffn_out(h, dyh, tm=512):
    nj, t, fb = h.shape
    d = dyh.shape[1]
    return _wgrad_call(h, dyh,
                       pl.BlockSpec((None, tm, fb), lambda j, k: (j, k, 0)),
                       pl.BlockSpec((tm, d), lambda j, k: (k, 0)),
                       (nj, fb, d), pl.BlockSpec((None, fb, d), lambda j, k: (j, 0, 0)),
                       (nj, t // tm), "wgrad_ffn_out")


def wgrad_2d(a, b, tk, name, out_dtype=BF16, tm=512):
    t, k = a.shape
    n = b.shape[1]
    return _wgrad_call(a, b,
                       pl.BlockSpec((tm, tk), lambda c, s: (s, c)),
                       pl.BlockSpec((tm, n), lambda c, s: (s, 0)),
                       (k, n), pl.BlockSpec((tk, n), lambda c, s: (c, 0)),
                       (k // tk, t // tm), name, out_dtype)


N_BIG = 7 * 512
N_PROJ = N_BIG + 128
COL_SMALL = N_BIG // 128


def inproj_fwd(x, nw, w, tm=256):
    t, d = x.shape
    n = w.shape[1]

    def body(x_ref, nw_ref, w_ref, p_ref, hn_ref):
        xf = x_ref[...]
        hn = (xf * _rstd(xf) * nw_ref[...]).astype(BF16)
        hn_ref[...] = hn
        p_ref[...] = _dot(hn, w_ref[...])

    return pl.pallas_call(
        body, grid=(t // tm,),
        in_specs=[pl.BlockSpec((tm, d), lambda i: (i, 0)), pl.BlockSpec((1, d), lambda i: (0, 0)),
                  pl.BlockSpec((d, n), lambda i: (0, 0))],
        out_specs=[pl.BlockSpec((tm, n), lambda i: (i, 0)), pl.BlockSpec((tm, d), lambda i: (i, 0))],
        out_shape=[jax.ShapeDtypeStruct((t, n), F32), jax.ShapeDtypeStruct((t, d), BF16)],
        compiler_params=_cp("parallel"), name="inproj_fwd")(x, nw, w)


def inproj_bwd(x, dres, nw, w, dparts, tm=256):
    t, d = x.shape
    n = w.shape[1]
    widths = [p.shape[1] for p in dparts]
    assert sum(widths) == n

    def body(x_ref, dres_ref, nw_ref, w_ref, *rest):
        part_refs, (dx_ref, dnw_ref, dp_ref) = rest[:len(widths)], rest[len(widths):]

        @pl.when(pl.program_id(0) == 0)
        def _():
            dnw_ref[...] = jnp.zeros_like(dnw_ref)

        dp = jnp.concatenate([r[...].astype(BF16) for r in part_refs], axis=1)
        dp_ref[...] = dp
        dhn = _dot_nt(dp, w_ref[...])
        xf = x_ref[...]
        r = _rstd(xf)
        dnw_ref[...] += jnp.sum(dhn * xf * r, axis=0, keepdims=True)
        dx_ref[...] = _rms_bwd(xf, r, dhn * nw_ref[...]) + dres_ref[...]

    return pl.pallas_call(
        body, grid=(t // tm,),
        in_specs=[pl.BlockSpec((tm, d), lambda i: (i, 0)), pl.BlockSpec((tm, d), lambda i: (i, 0)),
                  pl.BlockSpec((1, d), lambda i: (0, 0)), pl.BlockSpec((d, n), lambda i: (0, 0))]
                 + [pl.BlockSpec((tm, wd), lambda i: (i, 0)) for wd in widths],
        out_specs=[pl.BlockSpec((tm, d), lambda i: (i, 0)), pl.BlockSpec((1, d), lambda i: (0, 0)),
                   pl.BlockSpec((tm, n), lambda i: (i, 0))],
        out_shape=[jax.ShapeDtypeStruct((t, d), F32), jax.ShapeDtypeStruct((1, d), F32),
                   jax.ShapeDtypeStruct((t, n), BF16)],
        compiler_params=_cp("arbitrary"), name="inproj_bwd")(x, dres, nw, w, *dparts)


def outproj_fwd(x, yf, yg, w, tm=512):
    t, d = x.shape
    hw = yf.shape[1]

    def body(x_ref, yf_ref, yg_ref, w_ref, o_ref, y_ref):
        y = jnp.concatenate([yf_ref[...], yg_ref[...]], axis=1).astype(BF16)
        y_ref[...] = y
        o_ref[...] = x_ref[...] + _dot(y, w_ref[...])

    return pl.pallas_call(
        body, grid=(t // tm,),
        in_specs=[pl.BlockSpec((tm, d), lambda i: (i, 0)), pl.BlockSpec((tm, hw), lambda i: (i, 0)),
                  pl.BlockSpec((tm, hw), lambda i: (i, 0)), pl.BlockSpec((2 * hw, d), lambda i: (0, 0))],
        out_specs=[pl.BlockSpec((tm, d), lambda i: (i, 0)), pl.BlockSpec((tm, 2 * hw), lambda i: (i, 0))],
        out_shape=[jax.ShapeDtypeStruct((t, d), F32), jax.ShapeDtypeStruct((t, 2 * hw), BF16)],
        compiler_params=_cp("parallel"), name="outproj_fwd")(x, yf, yg, w)


def outproj_bwd(dy, w, after=None, tm=512):
    t, d = dy.shape
    hw = w.shape[0] // 2
    extra = [] if after is None else [after]

    def body(dy_ref, w_ref, *rest):
        df_ref, dg_ref, dyb_ref = rest[-3:]
        dyb = dy_ref[...].astype(BF16)
        dyb_ref[...] = dyb
        dyy = _dot_nt(dyb, w_ref[...])
        df_ref[...] = dyy[:, :hw]
        dg_ref[...] = dyy[:, hw:]

    return pl.pallas_call(
        body, grid=(t // tm,),
        in_specs=[pl.BlockSpec((tm, d), lambda i: (i, 0)), pl.BlockSpec((2 * hw, d), lambda i: (0, 0))]
                 + [pl.BlockSpec(memory_space=pl.ANY)] * len(extra),
        out_specs=[pl.BlockSpec((tm, hw), lambda i: (i, 0)), pl.BlockSpec((tm, hw), lambda i: (i, 0)),
                   pl.BlockSpec((tm, d), lambda i: (i, 0))],
        out_shape=[jax.ShapeDtypeStruct((t, hw), F32), jax.ShapeDtypeStruct((t, hw), F32),
                   jax.ShapeDtypeStruct((t, d), BF16)],
        compiler_params=_cp("parallel"), name="outproj_bwd")(dy, w, *extra)


def _lane(shape):
    return lax.broadcasted_iota(jnp.int32, shape, 1)


def _row(shape):
    return lax.broadcasted_iota(jnp.int32, shape, 0)


def _gate_terms(val, gp_ref):
    z = val + gp_ref[0:1, :]
    sp = jnp.log(1.0 + jnp.exp(-jnp.abs(z)))
    return z, sp


def gates_fwd(proj, gp, seq, ts=512):
    t = proj.shape[0]
    nb, ns = t // seq, seq // ts

    def body(v_ref, gp_ref, o_ref, carry_ref):
        @pl.when(pl.program_id(1) == 0)
        def _():
            carry_ref[...] = jnp.zeros_like(carry_ref)

        z, sp = _gate_terms(v_ref[...], gp_ref)
        logsig = jnp.minimum(z, 0.0) - sp
        tri = (_row((ts, ts)) >= _lane((ts, ts))).astype(F32)
        cum = jnp.dot(tri, logsig, precision=HIGHEST, preferred_element_type=F32) + carry_ref[0:1, :]
        carry_ref[0:1, :] = cum[ts - 1:ts, :]
        g = -jnp.exp(gp_ref[1:2, :]) * (jnp.maximum(z, 0.0) + sp)
        beta = jax.nn.sigmoid(z)
        lane = _lane((ts, 128))
        o_ref[...] = jnp.where(lane < 8, cum, jnp.where(lane < 12, g, jnp.where(lane < 16, beta, 0.0)))

    return pl.pallas_call(
        body, grid=(nb, ns),
        in_specs=[pl.BlockSpec((ts, 128), lambda b, s: (b * ns + s, COL_SMALL)),
                  pl.BlockSpec((8, 128), lambda b, s: (0, 0))],
        out_specs=pl.BlockSpec((ts, 128), lambda b, s: (b * ns + s, 0)),
        out_shape=jax.ShapeDtypeStruct((t, 128), F32),
        scratch_shapes=[pltpu.VMEM((8, 128), F32)],
        compiler_params=_cp("parallel", "arbitrary"), name="gates_fwd")(proj, gp)


def gates_bwd(proj, gp, dga, dgb, seq, ts=512):
    t = proj.shape[0]
    nb, ns = t // seq, seq // ts

    def body(v_ref, gp_ref, da_ref, db_ref, ds_ref, dgp_ref, carry_ref):
        @pl.when(pl.program_id(1) == 0)
        def _():
            carry_ref[...] = jnp.zeros_like(carry_ref)

        @pl.when((pl.program_id(0) == 0) & (pl.program_id(1) == 0))
        def _():
            dgp_ref[...] = jnp.zeros_like(dgp_ref)

        lane = _lane((ts, 128))
        dgate = jnp.where(lane < 8, da_ref[...], jnp.where(lane < 16, db_ref[...], 0.0))
        z, sp = _gate_terms(v_ref[...], gp_ref)
        triu = (_row((ts, ts)) <= _lane((ts, ts))).astype(F32)
        dlog = jnp.dot(triu, dgate, precision=HIGHEST, preferred_element_type=F32) + carry_ref[0:1, :]
        carry_ref[0:1, :] = dlog[0:1, :]
        sig = jax.nn.sigmoid(z)
        nea = -jnp.exp(gp_ref[1:2, :])
        g = nea * (jnp.maximum(z, 0.0) + sp)
        dz = jnp.where(lane < 8, dlog * (1.0 - sig),
                       jnp.where(lane < 12, dgate * nea * sig, dgate * sig * (1.0 - sig)))
        dz = jnp.where(lane < 16, dz, 0.0)
        ds_ref[...] = dz
        dgp_ref[0:1, :] += jnp.where(lane[0:1] < 12, jnp.sum(dz, axis=0, keepdims=True), 0.0)
        dgp_ref[1:2, :] += jnp.where((lane[0:1] >= 8) & (lane[0:1] < 12), jnp.sum(dgate * g, axis=0, keepdims=True), 0.0)

    rev = lambda b, s: (b * ns + (ns - 1 - s), 0)
    return pl.pallas_call(
        body, grid=(nb, ns),
        in_specs=[pl.BlockSpec((ts, 128), lambda b, s: (b * ns + (ns - 1 - s), COL_SMALL)),
                  pl.BlockSpec((8, 128), lambda b, s: (0, 0)),
                  pl.BlockSpec((ts, 128), rev), pl.BlockSpec((ts, 128), rev)],
        out_specs=[pl.BlockSpec((ts, 128), rev), pl.BlockSpec((8, 128), lambda b, s: (0, 0))],
        out_shape=[jax.ShapeDtypeStruct((t, 128), F32), jax.ShapeDtypeStruct((8, 128), F32)],
        scratch_shapes=[pltpu.VMEM((8, 128), F32)],
        compiler_params=_cp("arbitrary", "arbitrary"), name="gates_bwd")(proj, gp, dga, dgb)


NEG = -1e30
ATTN_TQ_FWD = 1024
ATTN_TQ_BWD = 512


def _pick_lane(tile, idx):
    return jnp.sum(jnp.where(_lane(tile.shape) == idx, tile, 0.0), axis=1, keepdims=True)


def _col_to_row(col, n):
    return jnp.sum(jnp.where(_row((n, n)) == _lane((n, n)), col, 0.0), axis=0, keepdims=True)


def _row_to_col(row, n):
    return jnp.sum(jnp.where(_row((n, n)) == _lane((n, n)), row, 0.0), axis=1, keepdims=True)


def _rows(i, n):
    return pl.ds(pl.multiple_of(i * n, n), n)


def _once(shape, index_map):
    return pl.BlockSpec(shape, index_map, pipeline_mode=pl.Buffered(1))


def attn_fwd(proj, gates, qw, kw, seq, tq=256):
    t = proj.shape[0]
    nb, nq, dh = t // seq, seq // tq, FOX_DH
    scale = dh ** -0.5

    def body(q_ref, k_ref, v_ref, g_ref, qw_ref, kw_ref, y_ref, lse_ref, qs, ks, vs, ccol, crow):
        p = pl.program_id(1)
        heads = range(2)

        def prep(i, _):
            r = _rows(i, tq)
            for hh in heads:
                lanes = slice(hh * dh, (hh + 1) * dh)
                qf, kf = q_ref[r, lanes], k_ref[r, lanes]
                qs[hh, r, :] = (qf * _rstd(qf) * qw_ref[...] * scale).astype(BF16)
                ks[hh, r, :] = (kf * _rstd(kf) * kw_ref[...]).astype(BF16)
                vs[hh, r, :] = v_ref[r, lanes].astype(BF16)
                cc = _pick_lane(g_ref[r, :], 2 * p + hh)
                ccol[hh, r, :] = cc
                crow[hh * nq + i] = _col_to_row(cc, tq)
            return 0

        lax.fori_loop(0, nq, prep, 0)

        def q_tile(i, _):
            r = _rows(i, tq)
            qt = [qs[hh, r, :] for hh in heads]
            cc = [ccol[hh, r, :] for hh in heads]

            def kv_step(j, carry, masked):
                kr = _rows(j, tq)
                out = []
                for hh in heads:
                    m, l, acc = carry[hh]
                    s = _dot_nt(qt[hh], ks[hh, kr, :]) + (cc[hh] - crow[hh * nq + j])
                    if masked:
                        s = jnp.where(_row((tq, tq)) >= _lane((tq, tq)), s, NEG)
                    m_new = jnp.maximum(m, jnp.max(s, axis=1, keepdims=True))
                    pe = jnp.exp(s - m_new)
                    a = jnp.exp(m - m_new)
                    out.append((m_new, a * l + jnp.sum(pe, axis=1, keepdims=True),
                                a * acc + _dot(pe.astype(BF16), vs[hh, kr, :])))
                return tuple(out)

            one = (jnp.full((tq, 1), NEG, F32), jnp.zeros((tq, 1), F32), jnp.zeros((tq, dh), F32))
            carry = lax.fori_loop(0, i, lambda j, c: kv_step(j, c, False), (one, one))
            carry = kv_step(i, carry, True)
            for hh in heads:
                m, l, acc = carry[hh]
                lanes = slice(hh * dh, (hh + 1) * dh)
                y_ref[r, lanes] = acc / l
                lse_ref[r, lanes] = jnp.broadcast_to(m + jnp.log(l), (tq, dh))
            return 0

        lax.fori_loop(0, nq, q_tile, 0)

    blk = lambda off: _once((seq, 128), lambda b, p: (b, off + p))
    return pl.pallas_call(
        body, grid=(nb, 4),
        in_specs=[blk(0), blk(4), blk(8), _once((seq, 128), lambda b, p: (b, 0)),
                  pl.BlockSpec((1, dh), lambda b, p: (0, 0)), pl.BlockSpec((1, dh), lambda b, p: (0, 0))],
        out_specs=[pl.BlockSpec((seq, 128), lambda b, p: (b, p)), pl.BlockSpec((seq, 128), lambda b, p: (b, p))],
        out_shape=[jax.ShapeDtypeStruct((t, 512), F32), jax.ShapeDtypeStruct((t, 512), F32)],
        scratch_shapes=[pltpu.VMEM((2, seq, dh), BF16), pltpu.VMEM((2, seq, dh), BF16), pltpu.VMEM((2, seq, dh), BF16),
                        pltpu.VMEM((2, seq, 1), F32), pltpu.VMEM((2 * nq, 1, tq), F32)],
        compiler_params=_cp("parallel", "arbitrary"), name="attn_fwd")(proj, proj, proj, gates, qw, kw)


def attn_bwd(proj, gates, qw, kw, y, lse, dy, seq, tq=256):
    t = proj.shape[0]
    nb, nq, dh = t // seq, seq // tq, FOX_DH
    scale = dh ** -0.5

    def body(q_ref, k_ref, v_ref, g_ref, qw_ref, kw_ref, y_ref, lse_ref, dy_ref,
             dq_ref, dk_ref, dv_ref, dg_ref, dqw_ref, dkw_ref,
             qs, ks, vs, dos, cols, crow, dqa, dka):
        b, p = pl.program_id(0), pl.program_id(1)

        @pl.when((b == 0) & (p == 0))
        def _():
            dqw_ref[...] = jnp.zeros_like(dqw_ref)
            dkw_ref[...] = jnp.zeros_like(dkw_ref)

        @pl.when(p == 0)
        def _():
            dg_ref[...] = jnp.zeros_like(dg_ref)

        heads = range(2)
        hl = lambda hh: slice(hh * dh, (hh + 1) * dh)

        def prep(i, _):
            r = _rows(i, tq)
            for hh in heads:
                lanes = hl(hh)
                qf, kf = q_ref[r, lanes], k_ref[r, lanes]
                qs[hh, r, :] = (qf * _rstd(qf) * qw_ref[...] * scale).astype(BF16)
                ks[hh, r, :] = (kf * _rstd(kf) * kw_ref[...]).astype(BF16)
                vs[hh, r, :] = v_ref[r, lanes].astype(BF16)
                dyf = dy_ref[r, lanes]
                dos[hh, r, :] = dyf.astype(BF16)
                cc = _pick_lane(g_ref[r, :], 2 * p + hh)
                crow[hh * nq + i] = _col_to_row(cc, tq)
                delta = jnp.sum(dyf * y_ref[r, lanes], axis=1, keepdims=True)
                lane = _lane((tq, 128))
                cols[hh, r, :] = jnp.where(lane == 0, cc, jnp.where(lane == 1, lse_ref[r, hh * dh:hh * dh + 1],
                                                                     jnp.where(lane == 2, delta, 0.0)))
                dqa[hh, r, :] = jnp.zeros((tq, dh), F32)
            return 0

        lax.fori_loop(0, nq, prep, 0)

        def kv_tile(j, _):
            kr = _rows(j, tq)
            kt = [ks[hh, kr, :] for hh in heads]
            vt = [vs[hh, kr, :] for hh in heads]
            cr = [crow[hh * nq + j] for hh in heads]

            def q_step(i, carry, masked):
                r = _rows(i, tq)
                out = []
                for hh in heads:
                    dk, dv, dcr = carry[hh]
                    qt, dot, cl = qs[hh, r, :], dos[hh, r, :], cols[hh, r, :]
                    s = _dot_nt(qt, kt[hh]) + (cl[:, 0:1] - cr[hh])
                    if masked:
                        s = jnp.where(_row((tq, tq)) >= _lane((tq, tq)), s, NEG)
                    pe = jnp.exp(s - cl[:, 1:2])
                    ds = pe * (_dot_nt(dot, vt[hh]) - cl[:, 2:3])
                    dsb = ds.astype(BF16)
                    dqa[hh, r, :] += _dot(dsb, kt[hh])
                    cols[hh, r, :] = cl + jnp.where(_lane((tq, 128)) == 3, jnp.sum(ds, axis=1, keepdims=True), 0.0)
                    out.append((dk + _dot_tn(dsb, qt), dv + _dot_tn(pe.astype(BF16), dot),
                                dcr - jnp.sum(ds, axis=0, keepdims=True)))
                return tuple(out)

            one = (jnp.zeros((tq, dh), F32), jnp.zeros((tq, dh), F32), jnp.zeros((1, tq), F32))
            carry = q_step(j, (one, one), True)
            carry = lax.fori_loop(j + 1, nq, lambda i, c: q_step(i, c, False), carry)
            for hh in heads:
                dk, dv, dcr = carry[hh]
                dka[hh, kr, :] = dk
                dv_ref[kr, hl(hh)] = dv
                dg_ref[kr, :] = jnp.where(_lane((tq, 128)) == 2 * p + hh, _row_to_col(dcr, tq), dg_ref[kr, :])
            return 0

        lax.fori_loop(0, nq, kv_tile, 0)

        def post(i, _):
            r = _rows(i, tq)
            for hh in heads:
                lanes = hl(hh)
                qf, kf = q_ref[r, lanes], k_ref[r, lanes]
                rq, rk = _rstd(qf), _rstd(kf)
                dqn, dkn = dqa[hh, r, :] * scale, dka[hh, r, :]
                dqw_ref[...] += jnp.sum(dqn * qf * rq, axis=0, keepdims=True)
                dkw_ref[...] += jnp.sum(dkn * kf * rk, axis=0, keepdims=True)
                dq_ref[r, lanes] = _rms_bwd(qf, rq, dqn * qw_ref[...])
                dk_ref[r, lanes] = _rms_bwd(kf, rk, dkn * kw_ref[...])
                dg_ref[r, :] += jnp.where(_lane((tq, 128)) == 2 * p + hh, cols[hh, r, 3:4], 0.0)
            return 0

        lax.fori_loop(0, nq, post, 0)

    blk = lambda off: _once((seq, 128), lambda b, p: (b, off + p))
    own = lambda: _once((seq, 128), lambda b, p: (b, p))
    vec = lambda: pl.BlockSpec((1, dh), lambda b, p: (0, 0))
    return pl.pallas_call(
        body, grid=(nb, 4),
        in_specs=[blk(0), blk(4), blk(8), _once((seq, 128), lambda b, p: (b, 0)), vec(), vec(), own(), own(), own()],
        out_specs=[own(), own(), own(), _once((seq, 128), lambda b, p: (b, 0)), vec(), vec()],
        out_shape=[jax.ShapeDtypeStruct((t, 512), F32)] * 3
                  + [jax.ShapeDtypeStruct((t, 128), F32), jax.ShapeDtypeStruct((1, dh), F32), jax.ShapeDtypeStruct((1, dh), F32)],
        scratch_shapes=[pltpu.VMEM((2, seq, dh), BF16)] * 4
                       + [pltpu.VMEM((2, seq, 128), F32), pltpu.VMEM((2 * nq, 1, tq), F32),
                          pltpu.VMEM((2, seq, dh), F32), pltpu.VMEM((2, seq, dh), F32)],
        compiler_params=_cp("arbitrary", "arbitrary"), name="attn_bwd")(proj, proj, proj, gates, qw, kw, y, lse, dy)


def _silu_grad(c, sg):
    return sg * (1.0 + c * (1.0 - sg))


def _conv(x, w, n):
    row = _row(x.shape)
    c = x * w[CONV_W - 1:CONV_W, :]
    for k in range(CONV_W - 1):
        sh = CONV_W - 1 - k
        c = c + w[k:k + 1, :] * jnp.where(row >= sh, pltpu.roll(x, sh, 0), 0.0)
    return c


def gdn_pre_fwd(proj, cw, seq):
    t = proj.shape[0]
    nb = t // seq
    scale = GDN_DH ** -0.5

    def body(xq_ref, xk_ref, xv_ref, wq_ref, wk_ref, wv_ref, q_ref, k_ref, v_ref):
        def act(x_ref, w_ref):
            c = _conv(x_ref[...], w_ref[...], seq)
            return c * jax.nn.sigmoid(c)

        aq, ak = act(xq_ref, wq_ref), act(xk_ref, wk_ref)
        q_ref[...] = aq * lax.rsqrt(jnp.sum(aq * aq, axis=1, keepdims=True) + EPS) * scale
        k_ref[...] = ak * lax.rsqrt(jnp.sum(ak * ak, axis=1, keepdims=True) + EPS)
        v_ref[...] = act(xv_ref, wv_ref)

    xb = lambda off: pl.BlockSpec((seq, 128), lambda b, h: (b, off + h))
    wb = lambda off: pl.BlockSpec((CONV_W, 128), lambda b, h: (0, off + h))
    ob = lambda: pl.BlockSpec((seq, 128), lambda b, h: (b, h))
    return pl.pallas_call(
        body, grid=(nb, GDN_HEADS),
        in_specs=[xb(12), xb(16), xb(20), wb(0), wb(4), wb(8)],
        out_specs=[ob(), ob(), ob()],
        out_shape=[jax.ShapeDtypeStruct((t, 512), F32)] * 3,
        compiler_params=_cp("parallel", "parallel"), name="gdn_pre_fwd")(proj, proj, proj, cw, cw, cw)


def gdn_pre_bwd(proj, cw, dq, dk, dv, seq):
    t = proj.shape[0]
    nb = t // seq
    scale = GDN_DH ** -0.5

    def body(xq_ref, xk_ref, xv_ref, wq_ref, wk_ref, wv_ref, dq_ref, dk_ref, dv_ref,
             dxq_ref, dxk_ref, dxv_ref, dwq_ref, dwk_ref, dwv_ref):
        first = pl.program_id(1) == 0
        row = _row((seq, 128))

        def one(x_ref, w_ref, dy_ref, dx_ref, dw_ref, norm, sc):
            x, w = x_ref[...], w_ref[...]
            c = _conv(x, w, seq)
            sg = jax.nn.sigmoid(c)
            dy = dy_ref[...]
            if norm:
                a = c * sg
                rs = lax.rsqrt(jnp.sum(a * a, axis=1, keepdims=True) + EPS)
                dy = dy * sc
                da = rs * dy - a * (rs * rs * rs) * jnp.sum(dy * a, axis=1, keepdims=True)
            else:
                da = dy
            dc = da * _silu_grad(c, sg)
            dx = dc * w[CONV_W - 1:CONV_W, :]
            dws = [None] * CONV_W
            dws[CONV_W - 1] = jnp.sum(dc * x, axis=0, keepdims=True)
            for k in range(CONV_W - 1):
                sh = CONV_W - 1 - k
                dx = dx + w[k:k + 1, :] * jnp.where(row < seq - sh, pltpu.roll(dc, seq - sh, 0), 0.0)
                dws[k] = jnp.sum(dc * jnp.where(row >= sh, pltpu.roll(x, sh, 0), 0.0), axis=0, keepdims=True)
            dx_ref[...] = dx
            dwn = jnp.concatenate(dws, axis=0)

            @pl.when(first)
            def _():
                dw_ref[...] = dwn

            @pl.when(jnp.logical_not(first))
            def _():
                dw_ref[...] += dwn

        one(xq_ref, wq_ref, dq_ref, dxq_ref, dwq_ref, True, scale)
        one(xk_ref, wk_ref, dk_ref, dxk_ref, dwk_ref, True, 1.0)
        one(xv_ref, wv_ref, dv_ref, dxv_ref, dwv_ref, False, 1.0)

    xb = lambda off: pl.BlockSpec((seq, 128), lambda h, b: (b, off + h))
    wb = lambda off: pl.BlockSpec((CONV_W, 128), lambda h, b: (0, off + h))
    ob = lambda: pl.BlockSpec((seq, 128), lambda h, b: (b, h))
    return pl.pallas_call(
        body, grid=(GDN_HEADS, nb),
        in_specs=[xb(12), xb(16), xb(20), wb(0), wb(4), wb(8), ob(), ob(), ob()],
        out_specs=[ob(), ob(), ob()] + [pl.BlockSpec((CONV_W, 128), lambda h, b: (0, h))] * 3,
        out_shape=[jax.ShapeDtypeStruct((t, 512), F32)] * 3 + [jax.ShapeDtypeStruct((CONV_W, 512), F32)] * 3,
        compiler_params=_cp("parallel", "arbitrary"), name="gdn_pre_bwd")(proj, proj, proj, cw, cw, cw, dq, dk, dv)


def _b16(x):
    return x.astype(BF16)


@jax.custom_vjp
def _mm(a, b):
    return _dot(_b16(a), _b16(b))


_mm.defvjp(lambda a, b: (_mm(a, b), (a, b)),
           lambda res, g: (_dot_nt(_b16(g), _b16(res[1])), _dot_tn(_b16(res[0]), _b16(g))))


@jax.custom_vjp
def _mm_nt(a, b):
    return _dot_nt(_b16(a), _b16(b))


_mm_nt.defvjp(lambda a, b: (_mm_nt(a, b), (a, b)),
              lambda res, g: (_dot(_b16(g), _b16(res[1])), _dot_tn(_b16(g), _b16(res[0]))))


@jax.custom_vjp
def _mm_tn(a, b):
    return _dot_tn(_b16(a), _b16(b))


_mm_tn.defvjp(lambda a, b: (_mm_tn(a, b), (a, b)),
              lambda res, g: (_dot_nt(_b16(res[1]), _b16(g)), _dot(_b16(res[0]), _b16(g))))


def _dot32(a, b, dims=(((1,), (0,)), ((), ()))):
    def split(x):
        hi = x.astype(BF16)
        return hi, (x - hi.astype(F32)).astype(BF16)

    (ah, al), (bh, bl) = split(a), split(b)
    d = lambda x, y: lax.dot_general(x, y, dims, preferred_element_type=F32)
    return d(ah, bh) + (d(ah, bl) + d(al, bh))


def _inv_fwd_many(mats):
    n = mats[0].shape[0]
    eye = (_row((n, n)) == _lane((n, n))).astype(F32)
    invs, pws = [eye - a for a in mats], list(mats)
    for _ in range(n.bit_length() - 2):
        pws = [_dot32(pw, pw) for pw in pws]
        invs = [inv + _dot32(inv, pw) for inv, pw in zip(invs, pws)]
    return invs


@jax.custom_vjp
def _inv_saved(a, inv):
    return inv


def _inv_saved_bwd(inv, g):
    tg = _dot32(inv, g, (((0,), (0,)), ((), ())))
    return -_dot32(tg, inv, (((1,), (1,)), ((), ()))), jnp.zeros_like(inv)


_inv_saved.defvjp(lambda a, inv: (inv, inv), _inv_saved_bwd)


def _gdn_decay(gcol):
    c = CHUNK
    ri, ci = _row((c, c)), _lane((c, c))
    incl, eye = ri >= ci, ri == ci
    grow = jnp.sum(jnp.where(eye, gcol, 0.0), axis=0, keepdims=True)
    gc = jnp.sum(jnp.where(incl, grow, 0.0), axis=1, keepdims=True)
    gcr = jnp.sum(jnp.where(eye, gc, 0.0), axis=0, keepdims=True)
    gl = jnp.sum(jnp.where(_row((c, 1)) == c - 1, gc, 0.0), axis=0, keepdims=True)
    return gc, gl, jnp.exp(jnp.where(incl, gc - gcr, NEG))


def _gdn_a(k, bcol, decay):
    c = CHUNK
    return jnp.where(_row((c, c)) > _lane((c, c)), _mm_nt(k * bcol, k) * decay, 0.0)


def _gdn_chunk(q, k, v, gcol, bcol, state, gg, nw, inv_saved):
    c = CHUNK
    incl = _row((c, c)) >= _lane((c, c))
    gc, gl, decay = _gdn_decay(gcol)
    kb, vb = k * bcol, v * bcol
    inv = _inv_saved(_gdn_a(k, bcol, decay), inv_saved)
    eg = jnp.exp(gc)
    u = _mm(inv, vb)
    w = _mm(inv, kb * eg)
    pm = jnp.where(incl, _mm_nt(q, k) * decay, 0.0)
    kd = k * jnp.exp(gl - gc)
    qd = q * eg
    v_new = u - _mm(w, state)
    o = _mm(qd, state) + _mm(pm, v_new)
    state_new = state * jnp.exp(gl) + _mm_tn(kd, v_new)
    y = o * _rstd(o) * nw * (gg * jax.nn.sigmoid(gg))
    return y, state_new


_gdn_chunks = jax.vmap(_gdn_chunk, in_axes=(0, 0, 0, 0, 0, 0, 0, None, 0))


def _gdn_chain_inputs(chains, p, r, c, q_ref, k_ref, v_ref, g_ref, gg_ref, inv_ref):
    cols = {nm: [] for nm in ("q", "k", "v", "g", "b", "gg", "inv")}
    for b, hh in chains:
        h = 2 * p + hh
        ln = slice(hh * 128, (hh + 1) * 128)
        gt = g_ref[b, r, :]
        cols["q"].append(q_ref[b, r, ln])
        cols["k"].append(k_ref[b, r, ln])
        cols["v"].append(v_ref[b, r, ln])
        cols["g"].append(_pick_lane(gt, 8 + h))
        cols["b"].append(_pick_lane(gt, 12 + h))
        cols["gg"].append(gg_ref[b, r, ln])
        cols["inv"].append(inv_ref[b, hh, c])
    return [jnp.stack(cols[nm]) for nm in ("q", "k", "v", "g", "b", "gg", "inv")]


GDN_CB = 8


def gdn_inv(k, gates, seq):
    t = k.shape[0]
    nb, nc = t // seq, seq // CHUNK
    rb = GDN_CB * CHUNK
    nsb = seq // rb

    def body(k_ref, g_ref, o_ref):
        h = pl.program_id(1)
        mats = []
        for c in range(GDN_CB):
            r = slice(c * CHUNK, (c + 1) * CHUNK)
            gt = g_ref[r, :]
            _, _, decay = _gdn_decay(_pick_lane(gt, 8 + h))
            mats.append(_gdn_a(k_ref[r, :], _pick_lane(gt, 12 + h), decay))
        for c, inv in enumerate(_inv_fwd_many(mats)):
            o_ref[c] = inv

    return pl.pallas_call(
        body, grid=(nb, GDN_HEADS, nsb),
        in_specs=[pl.BlockSpec((rb, 128), lambda b, h, s: (b * nsb + s, h)),
                  pl.BlockSpec((rb, 128), lambda b, h, s: (b * nsb + s, 0))],
        out_specs=pl.BlockSpec((None, None, GDN_CB, CHUNK, CHUNK), lambda b, h, s: (b, h, s, 0, 0)),
        out_shape=jax.ShapeDtypeStruct((nb, GDN_HEADS, nc, CHUNK, CHUNK), F32),
        compiler_params=_cp("parallel", "parallel", "parallel"), name="gdn_inv")(k, gates)


def _gdn_specs(nb, nsb, rev):
    blk = (lambda s: nsb - 1 - s) if rev else (lambda s: s)
    rb = GDN_CB * CHUNK
    pair = lambda off=0: pl.BlockSpec((nb, rb, 256), lambda s, p: (0, blk(s), off + p))
    gate = lambda: pl.BlockSpec((nb, rb, 128), lambda s, p: (0, blk(s), 0))
    mats = lambda n: pl.BlockSpec((nb, 2, GDN_CB, n, n), lambda s, p: (0, p, blk(s), 0, 0))
    return pair, gate, mats


def gdn_fwd(q, k, v, gates, proj, nw, inv, seq):
    t = q.shape[0]
    nb, nc = t // seq, seq // CHUNK
    nsb = nc // GDN_CB
    chains = [(b, hh) for b in range(nb) for hh in range(2)]
    nch = len(chains)
    pair, gate, mats = _gdn_specs(nb, nsb, False)

    def body(q_ref, k_ref, v_ref, g_ref, gg_ref, inv_ref, nw_ref, y_ref, st_ref, carry):
        s, p = pl.program_id(0), pl.program_id(1)

        @pl.when(s == 0)
        def _():
            for ci in range(nch):
                carry[p * nch + ci] = jnp.zeros((GDN_DH, GDN_DH), F32)

        def step(c, states):
            r = _rows(c, CHUNK)
            for ci, (b, hh) in enumerate(chains):
                st_ref[b, hh, c] = states[ci]
            ins = _gdn_chain_inputs(chains, p, r, c, q_ref, k_ref, v_ref, g_ref, gg_ref, inv_ref)
            y, states = _gdn_chunks(*ins[:5], states, ins[5], nw_ref[...], ins[6])
            for ci, (b, hh) in enumerate(chains):
                y_ref[b, r, hh * 128:(hh + 1) * 128] = y[ci]
            return states

        states = lax.fori_loop(0, GDN_CB, step, jnp.stack([carry[p * nch + ci] for ci in range(nch)]))
        for ci in range(nch):
            carry[p * nch + ci] = states[ci]

    v3 = lambda a: a.reshape(nb, seq, a.shape[1])
    y, st = pl.pallas_call(
        body, grid=(nsb, 2),
        in_specs=[pair(), pair(), pair(), gate(), pair(12), mats(CHUNK), pl.BlockSpec((1, 128), lambda s, p: (0, 0))],
        out_specs=[pair(), mats(GDN_DH)],
        out_shape=[jax.ShapeDtypeStruct((nb, seq, 512), F32),
                   jax.ShapeDtypeStruct((nb, GDN_HEADS, nc, GDN_DH, GDN_DH), F32)],
        scratch_shapes=[pltpu.VMEM((2 * nch, GDN_DH, GDN_DH), F32)],
        compiler_params=_cp("arbitrary", "arbitrary"), name="gdn_fwd")(v3(q), v3(k), v3(v), v3(gates), v3(proj), inv, nw)
    return y.reshape(t, 512), st


def gdn_bwd(q, k, v, gates, proj, nw, inv, states, dy, seq):
    t = q.shape[0]
    nb, nc = t // seq, seq // CHUNK
    nsb = nc // GDN_CB
    chains = [(b, hh) for b in range(nb) for hh in range(2)]
    nch = len(chains)
    pair, gate, mats = _gdn_specs(nb, nsb, True)

    def body(q_ref, k_ref, v_ref, g_ref, gg_ref, inv_ref, st_ref, dy_ref, nw_ref,
             dq_ref, dk_ref, dv_ref, dgg_ref, dg_ref, dnw_ref, carry):
        s, p = pl.program_id(0), pl.program_id(1)

        @pl.when((s == 0) & (p == 0))
        def _():
            dnw_ref[...] = jnp.zeros_like(dnw_ref)

        @pl.when(p == 0)
        def _():
            dg_ref[...] = jnp.zeros_like(dg_ref)

        @pl.when(s == 0)
        def _():
            for ci in range(nch):
                carry[p * nch + ci] = jnp.zeros((GDN_DH, GDN_DH), F32)

        def step(idx, dstates):
            c = GDN_CB - 1 - idx
            r = _rows(c, CHUNK)
            ins = _gdn_chain_inputs(chains, p, r, c, q_ref, k_ref, v_ref, g_ref, gg_ref, inv_ref)
            st = jnp.stack([st_ref[b, hh, c] for b, hh in chains])
            dy = jnp.stack([dy_ref[b, r, hh * 128:(hh + 1) * 128] for b, hh in chains])
            _, vjp = jax.vjp(_gdn_chunks, *ins[:5], st, ins[5], nw_ref[...], ins[6])
            dq, dk, dv, dgc, dbc, dstates, dgg, dnw, _ = vjp((dy, dstates))
            dnw_ref[...] += dnw
            lane = _lane((CHUNK, 128))
            for ci, (b, hh) in enumerate(chains):
                h = 2 * p + hh
                ln = slice(hh * 128, (hh + 1) * 128)
                dq_ref[b, r, ln] = dq[ci]
                dk_ref[b, r, ln] = dk[ci]
                dv_ref[b, r, ln] = dv[ci]
                dgg_ref[b, r, ln] = dgg[ci]
                dg_ref[b, r, :] = jnp.where(lane == 8 + h, dgc[ci], jnp.where(lane == 12 + h, dbc[ci], dg_ref[b, r, :]))
            return dstates

        dstates = lax.fori_loop(0, GDN_CB, step, jnp.stack([carry[p * nch + ci] for ci in range(nch)]))
        for ci in range(nch):
            carry[p * nch + ci] = dstates[ci]

    v3 = lambda a: a.reshape(nb, seq, a.shape[1])
    res = pl.pallas_call(
        body, grid=(nsb, 2),
        in_specs=[pair(), pair(), pair(), gate(), pair(12), mats(CHUNK), mats(GDN_DH), pair(),
                  pl.BlockSpec((1, 128), lambda s, p: (0, 0))],
        out_specs=[pair(), pair(), pair(), pair(), gate(), pl.BlockSpec((1, 128), lambda s, p: (0, 0))],
        out_shape=[jax.ShapeDtypeStruct((nb, seq, 512), F32)] * 4
                  + [jax.ShapeDtypeStruct((nb, seq, 128), F32), jax.ShapeDtypeStruct((1, 128), F32)],
        scratch_shapes=[pltpu.VMEM((2 * nch, GDN_DH, GDN_DH), F32)],
        compiler_params=_cp("arbitrary", "arbitrary"),
        name="gdn_bwd")(v3(q), v3(k), v3(v), v3(gates), v3(proj), inv, states, v3(dy), nw)
    return [a.reshape(t, a.shape[2]) for a in res[:5]] + [res[5]]


def loss_head(y, target, tm=512):
    t, d = y.shape

    def body(y_ref, t_ref, s_ref, dy_ref):
        @pl.when(pl.program_id(0) == 0)
        def _():
            s_ref[...] = jnp.zeros_like(s_ref)

        err = y_ref[...] - t_ref[...]
        s_ref[...] += jnp.sum(err * err, axis=0, keepdims=True)
        dy_ref[...] = err * (1.0 / d)

    return pl.pallas_call(
        body, grid=(t // tm,),
        in_specs=[pl.BlockSpec((tm, d), lambda i: (i, 0)), pl.BlockSpec((tm, d), lambda i: (i, 0))],
        out_specs=[pl.BlockSpec((1, d), lambda i: (0, 0)), pl.BlockSpec((tm, d), lambda i: (i, 0))],
        out_shape=[jax.ShapeDtypeStruct((1, d), F32), jax.ShapeDtypeStruct((t, d), F32)],
        compiler_params=_cp("arbitrary"), name="loss_head")(y, target)


def _place():
    return lax.axis_index("x"), lax.axis_index("y"), lax.axis_index("c")


def _peer(k):
    x, y, c = _place()
    px = 1 - x if (k >> 2) & 1 else x
    py = 1 - y if (k >> 1) & 1 else y
    pc = 1 - c if k & 1 else c
    return (px, py, pc), 4 * px + 2 * py + pc


_ANY = pl.BlockSpec(memory_space=pl.ANY)
_SEM = pl.BlockSpec(memory_space=pltpu.SEMAPHORE)
_EFFECT = pltpu.SideEffectType.DATAFLOW_SIDE_EFFECTING


def _me():
    x, y, c = _place()
    return 4 * x + 2 * y + c


def _remote_copy(ins, lands, scatter, send_sems, recv_sems, a, k, arriving):
    pid, pidx = _peer(k)
    return pltpu.make_async_remote_copy(src_ref=ins[a].at[pidx] if scatter[a] else ins[a],
                                        dst_ref=lands[a].at[pidx if arriving else _me()],
                                        send_sem=send_sems.at[a * N_DEV + k], recv_sem=recv_sems.at[a * N_DEV + k],
                                        device_id=pid, device_id_type=MESH)


def _local_copy(ins, lands, scatter, loc_sems, a):
    me = _me()
    return pltpu.make_async_copy(ins[a].at[me] if scatter[a] else ins[a], lands[a].at[me], loc_sems.at[a])


def exchange_start(arrays, scatter, name, after):
    n = len(arrays)
    lands = [lax.empty(a.shape if s else (N_DEV,) + a.shape, a.dtype) for a, s in zip(arrays, scatter)]

    def body(*refs):
        ins, lds = refs[:n], refs[n:2 * n]
        send_sems, recv_sems, loc_sems = refs[2 * n + 1:2 * n + 4]
        token = refs[-1]
        for k in range(1, N_DEV):
            for a in range(n):
                _remote_copy(ins, lds, scatter, send_sems, recv_sems, a, k, False).start()
        for a in range(n):
            _local_copy(ins, lds, scatter, loc_sems, a).start()
        token[...] = jnp.zeros_like(token)

    hbm = lambda a: pltpu.HBM(a.shape, a.dtype)
    res = pl.pallas_call(
        body, name=name,
        in_specs=[_ANY] * (2 * n + 1),
        out_specs=[_SEM, _SEM, _SEM] + [_ANY] * (2 * n) + [pl.BlockSpec(memory_space=pltpu.VMEM)],
        out_shape=[pltpu.SemaphoreType.DMA((n * N_DEV,)), pltpu.SemaphoreType.DMA((n * N_DEV,)),
                   pltpu.SemaphoreType.DMA((n,))]
                  + [hbm(a) for a in arrays] + [hbm(a) for a in lands] + [jax.ShapeDtypeStruct((8, 128), F32)],
        input_output_aliases={i: 3 + i for i in range(2 * n)},
        compiler_params=pltpu.CompilerParams(has_side_effects=_EFFECT),
    )(*[pltpu.with_memory_space_constraint(a, pltpu.HBM) for a in list(arrays) + lands], after)
    return res[0:3], res[3:3 + n], res[3 + n:3 + 2 * n], res[-1]


def exchange_wait(sems, arrays, lands, scatter, after, name):
    n = len(arrays)

    def body(*refs):
        ins, lds = refs[:n], refs[n:2 * n]
        ssem, rsem, lsem = refs[2 * n:2 * n + 3]
        for a in range(n):
            _local_copy(ins, lds, scatter, lsem, a).wait()
        for k in range(1, N_DEV):
            for a in range(n):
                _remote_copy(ins, lds, scatter, ssem, rsem, a, k, True).wait_recv()
        for k in range(1, N_DEV):
            for a in range(n):
                _remote_copy(ins, lds, scatter, ssem, rsem, a, k, False).wait_send()

    hbm = lambda a: pltpu.HBM(a.shape, a.dtype)
    res = pl.pallas_call(
        body, name=name,
        in_specs=[_ANY] * (2 * n) + [_SEM, _SEM, _SEM, _ANY],
        out_specs=[_ANY] * (2 * n),
        out_shape=[hbm(a) for a in arrays] + [hbm(a) for a in lands],
        input_output_aliases={i: i for i in range(2 * n)},
        compiler_params=pltpu.CompilerParams(has_side_effects=_EFFECT),
    )(*arrays, *lands, *sems, after)
    return list(res[n:])


def exchange_begin(arrays, scatter, name, after):
    sems, arrays_thru, lands_thru, token = exchange_start(arrays, scatter, name + "_start", after)
    return (sems, arrays_thru, lands_thru, scatter, name), token


def exchange_end(state, after):
    sems, arrays_thru, lands_thru, scatter, name = state
    return exchange_wait(sems, arrays_thru, lands_thru, scatter, after, name + "_wait")


def adamw_reduce(slots, w, m, v, name, after=None):
    r, c = w.shape
    tr = r
    while tr * c * 4 > (1 << 20) and tr % 16 == 0:
        tr //= 2
    bc1 = 1.0 - ADAM_B1 ** ADAM_STEP
    bc2 = 1.0 - ADAM_B2 ** ADAM_STEP

    def body(s_ref, w_ref, m_ref, v_ref, *rest):
        g_ref, d_ref, nm_ref, nv_ref = rest[-4:]
        g = s_ref[0].astype(F32)
        for j in range(1, N_DEV):
            g = g + s_ref[j].astype(F32)
        nm = ADAM_B1 * m_ref[...] + (1.0 - ADAM_B1) * g
        nv = ADAM_B2 * v_ref[...] + (1.0 - ADAM_B2) * (g * g)
        g_ref[...] = g
        nm_ref[...] = nm
        nv_ref[...] = nv
        d_ref[...] = -ADAM_LR * ((nm / bc1) / (jnp.sqrt(nv / bc2) + ADAM_EPS) + ADAM_WD * w_ref[...])

    blk = lambda: pl.BlockSpec((tr, c), lambda i: (i, 0))
    extra = [] if after is None else [after]
    return pl.pallas_call(
        body, grid=(r // tr,),
        in_specs=[pl.BlockSpec((N_DEV, tr, c), lambda i: (0, i, 0)), blk(), blk(), blk()] + [_ANY] * len(extra),
        out_specs=[blk(), blk(), blk(), blk()],
        out_shape=[jax.ShapeDtypeStruct((r, c), F32)] * 4,
        compiler_params=_cp("parallel"), name=name)(slots, w, m, v, *extra)


BIG = ("ffn1_w_in", "ffn1_w_out", "w_in", "gdn_conv", "w_out", "ffn2_w_in", "ffn2_w_out")
GROUPS = (BIG[0:2], BIG[2:5], BIG[5:7])
SMALL = ("ffn1_norm", "mix_norm", "fox_q_norm", "fox_k_norm", "fox_f_bias", "gdn_a_log", "gdn_dt_bias",
         "gdn_out_norm", "ffn2_norm")
WEIGHTS = ("ffn1_norm", "ffn1_w_in", "ffn1_w_out", "mix_norm", "w_in", "fox_q_norm", "fox_k_norm", "fox_f_bias",
           "gdn_conv", "gdn_a_log", "gdn_dt_bias", "gdn_out_norm", "w_out", "ffn2_norm", "ffn2_w_in", "ffn2_w_out")
IN_COLS = (("fq", 512), ("fk", 512), ("fv", 512), ("ff", 8), ("gq", 512), ("gk", 512), ("gv", 512),
           ("ga", 4), ("gb", 4), ("gg", 512))
MY_BIG = ("fq", "fk", "fv", "gq", "gk", "gv", "gg")
MY_SMALL = ("ff", "ga", "gb")
SMALL_ROWS = 8 * 128


def _in_cols_to_mine(w):
    off, parts = 0, {}
    for nm, wd in IN_COLS:
        parts[nm] = w[:, off:off + wd]
        off += wd
    small = jnp.concatenate([parts[nm] for nm in MY_SMALL], axis=1)
    small = jnp.pad(small, ((0, 0), (0, 128 - small.shape[1])))
    return jnp.concatenate([parts[nm] for nm in MY_BIG] + [small], axis=1)


def _in_cols_from_mine(g):
    parts = {nm: g[:, i * 512:(i + 1) * 512] for i, nm in enumerate(MY_BIG)}
    off = N_BIG
    for nm in MY_SMALL:
        wd = dict(IN_COLS)[nm]
        parts[nm] = g[:, off:off + wd]
        off += wd
    return jnp.concatenate([parts[nm] for nm, _ in IN_COLS], axis=1)


def _pack_small(vals):
    rows = []
    nl = vals[SMALL[0]].shape[0]
    for l in range(nl):
        for nm in SMALL:
            v = vals[nm][l].reshape(-1)
            pad = (-v.shape[0]) % SMALL_ROWS
            rows.append(jnp.pad(v, (0, pad)).reshape(-1, 128))
    return jnp.concatenate(rows, axis=0)


def _unpack_small(packed, like):
    out = {nm: [] for nm in SMALL}
    row = 0
    nl = like[SMALL[0]].shape[0]
    for l in range(nl):
        for nm in SMALL:
            n = like[nm].shape[1]
            nr = -(-n // SMALL_ROWS) * 8
            out[nm].append(packed[row:row + nr].reshape(-1)[:n])
            row += nr
    return {nm: jnp.stack(v) for nm, v in out.items()}


def kernel(x, ffn1_norm, ffn1_w_in, ffn1_w_out, mix_norm, w_in, fox_q_norm, fox_k_norm, fox_f_bias, gdn_conv, gdn_a_log, gdn_dt_bias, gdn_out_norm, w_out, ffn2_norm, ffn2_w_in, ffn2_w_out, loss_target, m_ffn1_norm, m_ffn1_w_in, m_ffn1_w_out, m_mix_norm, m_w_in, m_fox_q_norm, m_fox_k_norm, m_fox_f_bias, m_gdn_conv, m_gdn_a_log, m_gdn_dt_bias, m_gdn_out_norm, m_w_out, m_ffn2_norm, m_ffn2_w_in, m_ffn2_w_out, v_ffn1_norm, v_ffn1_w_in, v_ffn1_w_out, v_mix_norm, v_w_in, v_fox_q_norm, v_fox_k_norm, v_fox_f_bias, v_gdn_conv, v_gdn_a_log, v_gdn_dt_bias, v_gdn_out_norm, v_w_out, v_ffn2_norm, v_ffn2_w_in, v_ffn2_w_out):
    wts = dict(ffn1_norm=ffn1_norm, ffn1_w_in=ffn1_w_in, ffn1_w_out=ffn1_w_out, mix_norm=mix_norm, w_in=w_in,
               fox_q_norm=fox_q_norm, fox_k_norm=fox_k_norm, fox_f_bias=fox_f_bias, gdn_conv=gdn_conv,
               gdn_a_log=gdn_a_log, gdn_dt_bias=gdn_dt_bias, gdn_out_norm=gdn_out_norm, w_out=w_out,
               ffn2_norm=ffn2_norm, ffn2_w_in=ffn2_w_in, ffn2_w_out=ffn2_w_out)
    mom = dict(ffn1_norm=m_ffn1_norm, ffn1_w_in=m_ffn1_w_in, ffn1_w_out=m_ffn1_w_out, mix_norm=m_mix_norm, w_in=m_w_in,
               fox_q_norm=m_fox_q_norm, fox_k_norm=m_fox_k_norm, fox_f_bias=m_fox_f_bias, gdn_conv=m_gdn_conv,
               gdn_a_log=m_gdn_a_log, gdn_dt_bias=m_gdn_dt_bias, gdn_out_norm=m_gdn_out_norm, w_out=m_w_out,
               ffn2_norm=m_ffn2_norm, ffn2_w_in=m_ffn2_w_in, ffn2_w_out=m_ffn2_w_out)
    var = dict(ffn1_norm=v_ffn1_norm, ffn1_w_in=v_ffn1_w_in, ffn1_w_out=v_ffn1_w_out, mix_norm=v_mix_norm, w_in=v_w_in,
               fox_q_norm=v_fox_q_norm, fox_k_norm=v_fox_k_norm, fox_f_bias=v_fox_f_bias, gdn_conv=v_gdn_conv,
               gdn_a_log=v_gdn_a_log, gdn_dt_bias=v_gdn_dt_bias, gdn_out_norm=v_gdn_out_norm, w_out=v_w_out,
               ffn2_norm=v_ffn2_norm, ffn2_w_in=v_ffn2_w_in, ffn2_w_out=v_ffn2_w_out)
    nb, seq, d = x.shape
    t = nb * seq
    depth = ffn1_norm.shape[0]

    stages = [(l, gi) for l in range(depth) for gi in range(len(GROUPS))]

    def shards_of(l, gi):
        return [wts[nm][l] if nm == "gdn_conv" else wts[nm][l].astype(BF16) for nm in GROUPS[gi]]

    def behind(nw, token):
        return nw if token is None else nw + token[0:1, 0:1]

    def small_params(l):
        return dict(
            n1=ffn1_norm[l][None], nmix=mix_norm[l][None], n2=ffn2_norm[l][None],
            qw=fox_q_norm[l][None], kw=fox_k_norm[l][None], onw=gdn_out_norm[l][None],
            gp=jnp.concatenate([
                jnp.concatenate([fox_f_bias[l], gdn_dt_bias[l], jnp.zeros((116,), F32)])[None],
                jnp.concatenate([jnp.zeros((8,), F32), gdn_a_log[l], jnp.zeros((116,), F32)])[None],
                jnp.zeros((6, 128), F32)], axis=0))

    h = x.reshape(t, d)
    state, token = exchange_begin(shards_of(0, 0), [False] * len(GROUPS[0]), "gather_0", ffn1_norm)
    landed = exchange_end(state, token)
    saved = [dict(p=small_params(l)) for l in range(depth)]
    for k, (l, gi) in enumerate(stages):
        s, w, token = saved[l], landed, None
        p = s["p"]
        if k + 1 < len(stages):
            nl, ng = stages[k + 1]
            state, token = exchange_begin(shards_of(nl, ng), [False] * len(GROUPS[ng]), f"gather_{k + 1}", landed[0])
        if gi == 0:
            fb = w[0].shape[2]
            p["w1i"], p["w1o"] = w[0].reshape(2, 4, d, fb), w[1].reshape(4, fb, d)
            s["x0"] = h
            h = s["x1"] = ffn_fwd(h, behind(p["n1"], token), p["w1i"], p["w1o"])
        elif gi == 1:
            p["wi"] = _in_cols_to_mine(w[0].transpose(1, 0, 2).reshape(d, -1))
            p["cw"] = w[1].transpose(1, 0, 2).reshape(CONV_W, -1)
            p["wo"] = w[2].reshape(d, d)
            proj, hn = inproj_fwd(h, behind(p["nmix"], token), p["wi"])
            gates = gates_fwd(proj, p["gp"], seq)
            yf, lse = attn_fwd(proj, gates, p["qw"], p["kw"], seq, tq=min(seq, ATTN_TQ_FWD))
            qh, kh, vh = gdn_pre_fwd(proj, p["cw"], seq)
            inv = gdn_inv(kh, gates, seq)
            yg, st = gdn_fwd(qh, kh, vh, gates, proj, p["onw"], inv, seq)
            h, ycat = outproj_fwd(h, yf, yg, p["wo"])
            s.update(x2=h, proj=proj, hn=hn, gates=gates, yf=yf, lse=lse, qh=qh, kh=kh, vh=vh, st=st, inv=inv, ycat=ycat)
        else:
            fb = w[0].shape[2]
            p["w2i"], p["w2o"] = w[0].reshape(2, 4, d, fb), w[1].reshape(4, fb, d)
            h = ffn_fwd(h, behind(p["n2"], token), p["w2i"], p["w2o"])
        if k + 1 < len(stages):
            landed = exchange_end(state, h)

    sq, dh = loss_head(h, loss_target.reshape(t, d))
    loss = lax.psum(0.5 * jnp.sum(sq) / d, ("x", "y", "c"))

    got = [None] * len(stages)
    pending, token = None, None
    gsmall = {nm: [None] * depth for nm in SMALL}
    for k in reversed(range(len(stages))):
        l, gi = stages[k]
        s = saved[l]
        p = s["p"]
        if gi != 1:
            nw, xin, wi_, wo_, nm_n = ((p["n1"], s["x0"], p["w1i"], p["w1o"], "ffn1_norm") if gi == 0 else
                                       (p["n2"], s["x2"], p["w2i"], p["w2o"], "ffn2_norm"))
            dh, dn, dgu, hh, xn, dyh = ffn_bwd(xin, dh, behind(nw, token), wi_, wo_)
            g_in, g_out = wgrad_ffn_in(xn, dgu), wgrad_ffn_out(hh, dyh)
            send = [g_in.reshape(N_DEV, d, g_in.shape[3]), g_out.reshape(N_DEV, -1, d)]
            gsmall[nm_n][l] = dn[0]
        else:
            dyf, dyg, dyb = outproj_bwd(dh, p["wo"], token)
            g_wo = wgrad_2d(s["ycat"], dyb, 512, "wgrad_w_out")
            dq, dk, dv, dga, dqw, dkw = attn_bwd(s["proj"], s["gates"], p["qw"], p["kw"], s["yf"], s["lse"], dyf, seq,
                                                 tq=min(seq, ATTN_TQ_BWD))
            dqh, dkh, dvh, dgg, dgb, donw = gdn_bwd(s["qh"], s["kh"], s["vh"], s["gates"], s["proj"], p["onw"],
                                                     s["inv"], s["st"], dyg, seq)
            dxq, dxk, dxv, dwq, dwk, dwv = gdn_pre_bwd(s["proj"], p["cw"], dqh, dkh, dvh, seq)
            dsm, dgp = gates_bwd(s["proj"], p["gp"], dga, dgb, seq)
            dh, dnmix, dproj = inproj_bwd(s["x1"], dh, p["nmix"], p["wi"], [dq, dk, dv, dxq, dxk, dxv, dgg, dsm])
            g_wi = wgrad_2d(s["hn"], dproj, 512, "wgrad_w_in", F32)
            g_cw = jnp.concatenate([dwq, dwk, dwv], axis=1)
            send = [_in_cols_from_mine(g_wi).reshape(d, N_DEV, -1).transpose(1, 0, 2),
                    g_cw.reshape(CONV_W, N_DEV, -1).transpose(1, 0, 2), g_wo.reshape(N_DEV, -1, d)]
            for nm, val in (("mix_norm", dnmix[0]), ("fox_q_norm", dqw[0]), ("fox_k_norm", dkw[0]),
                            ("fox_f_bias", dgp[0, 0:8]), ("gdn_a_log", dgp[1, 8:12]), ("gdn_dt_bias", dgp[0, 8:12]),
                            ("gdn_out_norm", donw[0])):
                gsmall[nm][l] = val
        flags = [True] * len(send)
        if k == 0:
            send.append(_pack_small({nm: jnp.stack(v) for nm, v in gsmall.items()}))
            flags.append(False)
        prev = dh
        if pending is not None:
            got[pending[1]] = exchange_end(pending[0], dh)
            prev = got[pending[1]][0]
        state, token = exchange_begin(send, flags, f"exchange_grads_{k}", prev)
        pending = (state, k)
    grad_x = dh.reshape(nb, seq, d)

    def update_stage(k, slots, after):
        l, gi = stages[k]
        out = {}
        for i, nm in enumerate(GROUPS[gi]):
            r, c = wts[nm].shape[1:]
            out[nm] = adamw_reduce(slots[i].reshape(N_DEV, r, c), wts[nm][l], mom[nm][l], var[nm][l], f"adamw_{nm}_{l}",
                                   after)
            if after is not None:
                after = out[nm][0]
        return out, after

    per_layer = [dict() for _ in range(depth)]
    last = token
    for k in range(1, len(stages)):
        out, last = update_stage(k, got[k], last)
        per_layer[stages[k][0]].update(out)
    got[0] = exchange_end(pending[0], last)
    per_layer[0].update(update_stage(0, got[0], None)[0])
    res = {nm: [jnp.stack([per_layer[l][nm][j] for l in range(depth)]) for j in range(4)] for nm in BIG}
    small_like = {nm: wts[nm] for nm in SMALL}
    sm = adamw_reduce(got[0][-1], _pack_small(small_like), _pack_small({nm: mom[nm] for nm in SMALL}),
                      _pack_small({nm: var[nm] for nm in SMALL}), "adamw_small")
    sm = [_unpack_small(a, small_like) for a in sm]
    for nm in SMALL:
        res[nm] = [sm[j][nm] for j in range(4)]
    return (loss, grad_x, *[res[nm][0] for nm in WEIGHTS], *[res[nm][1] for nm in WEIGHTS],
            *[res[nm][2] for nm in WEIGHTS], *[res[nm][3] for nm in WEIGHTS])
```

```python
import functools

import jax
import jax.numpy as jnp
from jax import lax
from jax.experimental import pallas as pl
from jax.experimental.pallas import tpu as pltpu

F32 = jnp.float32
BF16 = jnp.bfloat16
EPS = 1e-6
N_DEV = 8
MESH = pl.DeviceIdType.MESH
HIGHEST = lax.Precision.HIGHEST
VMEM_LIMIT = 56 * 1024 * 1024

FOX_HEADS, FOX_DH = 8, 64
GDN_HEADS, GDN_DH = 4, 128
CHUNK = 64
CONV_W = 4

ADAM_LR, ADAM_B1, ADAM_B2, ADAM_EPS, ADAM_WD, ADAM_STEP = 0.001, 0.9, 0.999, 1e-08, 0.01, 10


def _cp(*sem):
    return pltpu.CompilerParams(dimension_semantics=sem, vmem_limit_bytes=VMEM_LIMIT)


def _dot(a, b):
    return jnp.dot(a, b, preferred_element_type=F32)


def _dot_nt(a, b):
    return lax.dot_general(a, b, (((1,), (1,)), ((), ())), preferred_element_type=F32)


def _dot_tn(a, b):
    return lax.dot_general(a, b, (((0,), (0,)), ((), ())), preferred_element_type=F32)


def _rstd(xf):
    return lax.rsqrt(jnp.mean(xf * xf, axis=-1, keepdims=True) + EPS)


def _rms_bwd(xf, r, dyn):
    return r * dyn - xf * (r * r * r) * jnp.mean(dyn * xf, axis=-1, keepdims=True)


def ffn_fwd(x, nw, w_in, w_out, tm=512):
    t, d = x.shape
    nj, fb = w_out.shape[0], w_out.shape[1]

    def body(x_ref, nw_ref, wi_ref, wo_ref, o_ref, xn_ref, gu_ref, h_ref, acc_ref):
        j = pl.program_id(1)

        @pl.when(j == 0)
        def _():
            xf = x_ref[...]
            xn_ref[...] = (xf * _rstd(xf) * nw_ref[...]).astype(BF16)
            acc_ref[...] = jnp.zeros_like(acc_ref)

        xn = xn_ref[...]
        g = _dot(xn, wi_ref[0])
        u = _dot(xn, wi_ref[1])
        h = (g * jax.nn.sigmoid(g) * u).astype(BF16)
        gu_ref[0] = g.astype(BF16)
        gu_ref[1] = u.astype(BF16)
        h_ref[...] = h
        acc_ref[...] += _dot(h, wo_ref[...])

        @pl.when(j == nj - 1)
        def _():
            o_ref[...] = x_ref[...] + 0.5 * acc_ref[...]

    return pl.pallas_call(
        body, grid=(t // tm, nj),
        in_specs=[pl.BlockSpec((tm, d), lambda i, j: (i, 0)),
                  pl.BlockSpec((1, d), lambda i, j: (0, 0)),
                  pl.BlockSpec((2, None, d, fb), lambda i, j: (0, j, 0, 0)),
                  pl.BlockSpec((None, fb, d), lambda i, j: (j, 0, 0))],
        out_specs=[pl.BlockSpec((tm, d), lambda i, j: (i, 0)),
                   pl.BlockSpec((tm, d), lambda i, j: (i, 0)),
                   pl.BlockSpec((2, None, tm, fb), lambda i, j: (0, j, i, 0)),
                   pl.BlockSpec((None, tm, fb), lambda i, j: (j, i, 0))],
        out_shape=[jax.ShapeDtypeStruct((t, d), F32), jax.ShapeDtypeStruct((t, d), BF16),
                   jax.ShapeDtypeStruct((2, nj, t, fb), BF16), jax.ShapeDtypeStruct((nj, t, fb), BF16)],
        scratch_shapes=[pltpu.VMEM((tm, d), F32)],
        compiler_params=_cp("parallel", "arbitrary"), name="ffn_fwd")(x, nw, w_in, w_out)


def ffn_bwd(x, dy, nw, gu, w_in, w_out, tm=512, rc=256):
    t, d = x.shape
    nj, fb = w_out.shape[0], w_out.shape[1]
    rc = min(rc, tm)

    def body(x_ref, dy_ref, nw_ref, gu_ref, wi_ref, wo_ref,
             dx_ref, dnw_ref, dgu_ref, dyh_ref, acc_ref):
        i, j = pl.program_id(0), pl.program_id(1)

        @pl.when(j == 0)
        def _():
            dyh_ref[...] = (0.5 * dy_ref[...]).astype(BF16)
            acc_ref[...] = jnp.zeros_like(acc_ref)

        @pl.when((i == 0) & (j == 0))
        def _():
            dnw_ref[...] = jnp.zeros_like(dnw_ref)

        for c in range(tm // rc):
            r = slice(c * rc, (c + 1) * rc)
            g = gu_ref[0, r, :].astype(F32)
            u = gu_ref[1, r, :].astype(F32)
            sg = jax.nn.sigmoid(g)
            dh = _dot_nt(dyh_ref[r, :], wo_ref[...])
            dg = (dh * u * (sg * (1.0 + g * (1.0 - sg)))).astype(BF16)
            du = (dh * (g * sg)).astype(BF16)
            dgu_ref[0, r, :] = dg
            dgu_ref[1, r, :] = du
            acc_ref[r, :] += _dot_nt(dg, wi_ref[0]) + _dot_nt(du, wi_ref[1])

        @pl.when(j == nj - 1)
        def _():
            xf = x_ref[...]
            r = _rstd(xf)
            dxn = acc_ref[...]
            dnw_ref[...] += jnp.sum(dxn * xf * r, axis=0, keepdims=True)
            dx_ref[...] = _rms_bwd(xf, r, dxn * nw_ref[...]) + dy_ref[...]

    return pl.pallas_call(
        body, grid=(t // tm, nj),
        in_specs=[pl.BlockSpec((tm, d), lambda i, j: (i, 0)),
                  pl.BlockSpec((tm, d), lambda i, j: (i, 0)),
                  pl.BlockSpec((1, d), lambda i, j: (0, 0)),
                  pl.BlockSpec((2, None, tm, fb), lambda i, j: (0, j, i, 0)),
                  pl.BlockSpec((2, None, d, fb), lambda i, j: (0, j, 0, 0)),
                  pl.BlockSpec((None, fb, d), lambda i, j: (j, 0, 0))],
        out_specs=[pl.BlockSpec((tm, d), lambda i, j: (i, 0)),
                   pl.BlockSpec((1, d), lambda i, j: (0, 0)),
                   pl.BlockSpec((2, None, tm, fb), lambda i, j: (0, j, i, 0)),
                   pl.BlockSpec((tm, d), lambda i, j: (i, 0))],
        out_shape=[jax.ShapeDtypeStruct((t, d), F32),
                   jax.ShapeDtypeStruct((1, d), F32),
                   jax.ShapeDtypeStruct((2, nj, t, fb), BF16),
                   jax.ShapeDtypeStruct((t, d), BF16)],
        scratch_shapes=[pltpu.VMEM((tm, d), F32)],
        compiler_params=_cp("arbitrary", "arbitrary"), name="ffn_bwd")(x, dy, nw, gu, w_in, w_out)


def _wgrad_call(a, b, a_spec, b_spec, out_shape, out_spec, grid, name, out_dtype=BF16):
    last = len(grid) - 1
    acc_shape = tuple(s for s in out_spec.block_shape if s is not None)

    def body(a_ref, b_ref, o_ref, acc_ref):
        @pl.when(pl.program_id(last) == 0)
        def _():
            acc_ref[...] = jnp.zeros_like(acc_ref)

        acc_ref[...] += _dot_tn(a_ref[...], b_ref[...])

        @pl.when(pl.program_id(last) == grid[last] - 1)
        def _():
            o_ref[...] = acc_ref[...].astype(o_ref.dtype)

    sem = ("parallel",) * last + ("arbitrary",)
    return pl.pallas_call(body, grid=grid, in_specs=[a_spec, b_spec], out_specs=out_spec,
                          out_shape=jax.ShapeDtypeStruct(out_shape, out_dtype),
                          scratch_shapes=[pltpu.VMEM(acc_shape, F32)],
                          compiler_params=_cp(*sem), name=name)(a, b)


WGRAD_TM = 1024


def wgrad_ffn_in(xn, dgu, tm=WGRAD_TM):
    t, d = xn.shape
    _, nj, _, fb = dgu.shape
    tm = min(tm, t)
    return _wgrad_call(xn, dgu,
                       pl.BlockSpec((tm, d), lambda p, j, k: (k, 0)),
                       pl.BlockSpec((None, None, tm, fb), lambda p, j, k: (p, j, k, 0)),
                       (2, nj, d, fb), pl.BlockSpec((None, None, d, fb), lambda p, j, k: (p, j, 0, 0)),
                       (2, nj, t // tm), "wgrad_ffn_in")


def wgrad_ffn_out(h, dyh, tm=WGRAD_TM):
    nj, t, fb = h.shape
    d = dyh.shape[1]
    tm = min(tm, t)
    return _wgrad_call(h, dyh,
                       pl.BlockSpec((None, tm, fb), lambda j, k: (j, k, 0)),
                       pl.BlockSpec((tm, d), lambda j, k: (k, 0)),
                       (nj, fb, d), pl.BlockSpec((None, fb, d), lambda j, k: (j, 0, 0)),
                       (nj, t // tm), "wgrad_ffn_out")


def wgrad_2d(a, b, tk, name, out_dtype=BF16, tm=512):
    t, k = a.shape
    n = b.shape[1]
    return _wgrad_call(a, b,
                       pl.BlockSpec((tm, tk), lambda c, s: (s, c)),
                       pl.BlockSpec((tm, n), lambda c, s: (s, 0)),
                       (k, n), pl.BlockSpec((tk, n), lambda c, s: (c, 0)),
                       (k // tk, t // tm), name, out_dtype)


N_BIG = 7 * 512
N_PROJ = N_BIG + 128
COL_SMALL = N_BIG // 128


def inproj_fwd(x, nw, w, tm=256):
    t, d = x.shape
    n = w.shape[1]

    def body(x_ref, nw_ref, w_ref, p_ref, hn_ref):
        xf = x_ref[...]
        hn = (xf * _rstd(xf) * nw_ref[...]).astype(BF16)
        hn_ref[...] = hn
        p_ref[...] = _dot(hn, w_ref[...])

    return pl.pallas_call(
        body, grid=(t // tm,),
        in_specs=[pl.BlockSpec((tm, d), lambda i: (i, 0)), pl.BlockSpec((1, d), lambda i: (0, 0)),
                  pl.BlockSpec((d, n), lambda i: (0, 0))],
        out_specs=[pl.BlockSpec((tm, n), lambda i: (i, 0)), pl.BlockSpec((tm, d), lambda i: (i, 0))],
        out_shape=[jax.ShapeDtypeStruct((t, n), F32), jax.ShapeDtypeStruct((t, d), BF16)],
        compiler_params=_cp("parallel"), name="inproj_fwd")(x, nw, w)


def inproj_bwd(x, dres, nw, w, dparts, tm=256):
    t, d = x.shape
    n = w.shape[1]
    widths = [p.shape[1] for p in dparts]
    assert sum(widths) == n

    def body(x_ref, dres_ref, nw_ref, w_ref, *rest):
        part_refs, (dx_ref, dnw_ref, dp_ref) = rest[:len(widths)], rest[len(widths):]

        @pl.when(pl.program_id(0) == 0)
        def _():
            dnw_ref[...] = jnp.zeros_like(dnw_ref)

        dp = jnp.concatenate([r[...].astype(BF16) for r in part_refs], axis=1)
        dp_ref[...] = dp
        dhn = _dot_nt(dp, w_ref[...])
        xf = x_ref[...]
        r = _rstd(xf)
        dnw_ref[...] += jnp.sum(dhn * xf * r, axis=0, keepdims=True)
        dx_ref[...] = _rms_bwd(xf, r, dhn * nw_ref[...]) + dres_ref[...]

    return pl.pallas_call(
        body, grid=(t // tm,),
        in_specs=[pl.BlockSpec((tm, d), lambda i: (i, 0)), pl.BlockSpec((tm, d), lambda i: (i, 0)),
                  pl.BlockSpec((1, d), lambda i: (0, 0)), pl.BlockSpec((d, n), lambda i: (0, 0))]
                 + [pl.BlockSpec((tm, wd), lambda i: (i, 0)) for wd in widths],
        out_specs=[pl.BlockSpec((tm, d), lambda i: (i, 0)), pl.BlockSpec((1, d), lambda i: (0, 0)),
                   pl.BlockSpec((tm, n), lambda i: (i, 0))],
        out_shape=[jax.ShapeDtypeStruct((t, d), F32), jax.ShapeDtypeStruct((1, d), F32),
                   jax.ShapeDtypeStruct((t, n), BF16)],
        compiler_params=_cp("arbitrary"), name="inproj_bwd")(x, dres, nw, w, *dparts)


def outproj_fwd(x, yf, yg, w, tm=512):
    t, d = x.shape
    hw = yf.shape[1]

    def body(x_ref, yf_ref, yg_ref, w_ref, o_ref, y_ref):
        y = jnp.concatenate([yf_ref[...], yg_ref[...]], axis=1).astype(BF16)
        y_ref[...] = y
        o_ref[...] = x_ref[...] + _dot(y, w_ref[...])

    return pl.pallas_call(
        body, grid=(t // tm,),
        in_specs=[pl.BlockSpec((tm, d), lambda i: (i, 0)), pl.BlockSpec((tm, hw), lambda i: (i, 0)),
                  pl.BlockSpec((tm, hw), lambda i: (i, 0)), pl.BlockSpec((2 * hw, d), lambda i: (0, 0))],
        out_specs=[pl.BlockSpec((tm, d), lambda i: (i, 0)), pl.BlockSpec((tm, 2 * hw), lambda i: (i, 0))],
        out_shape=[jax.ShapeDtypeStruct((t, d), F32), jax.ShapeDtypeStruct((t, 2 * hw), BF16)],
        compiler_params=_cp("parallel"), name="outproj_fwd")(x, yf, yg, w)


def outproj_bwd(dy, w, after=None, tm=512):
    t, d = dy.shape
    hw = w.shape[0] // 2
    extra = [] if after is None else [after]

    def body(dy_ref, w_ref, *rest):
        df_ref, dg_ref, dyb_ref = rest[-3:]
        dyb = dy_ref[...].astype(BF16)
        dyb_ref[...] = dyb
        dyy = _dot_nt(dyb, w_ref[...])
        df_ref[...] = dyy[:, :hw]
        dg_ref[...] = dyy[:, hw:]

    return pl.pallas_call(
        body, grid=(t // tm,),
        in_specs=[pl.BlockSpec((tm, d), lambda i: (i, 0)), pl.BlockSpec((2 * hw, d), lambda i: (0, 0))]
                 + [pl.BlockSpec(memory_space=pl.ANY)] * len(extra),
        out_specs=[pl.BlockSpec((tm, hw), lambda i: (i, 0)), pl.BlockSpec((tm, hw), lambda i: (i, 0)),
                   pl.BlockSpec((tm, d), lambda i: (i, 0))],
        out_shape=[jax.ShapeDtypeStruct((t, hw), F32), jax.ShapeDtypeStruct((t, hw), F32),
                   jax.ShapeDtypeStruct((t, d), BF16)],
        compiler_params=_cp("parallel"), name="outproj_bwd")(dy, w, *extra)


def _lane(shape):
    return lax.broadcasted_iota(jnp.int32, shape, 1)


def _row(shape):
    return lax.broadcasted_iota(jnp.int32, shape, 0)


def _gate_terms(val, gp_ref):
    z = val + gp_ref[0:1, :]
    sp = jnp.log(1.0 + jnp.exp(-jnp.abs(z)))
    return z, sp


def gates_fwd(proj, gp, seq, ts=512):
    t = proj.shape[0]
    nb, ns = t // seq, seq // ts

    def body(v_ref, gp_ref, o_ref, carry_ref):
        @pl.when(pl.program_id(1) == 0)
        def _():
            carry_ref[...] = jnp.zeros_like(carry_ref)

        z, sp = _gate_terms(v_ref[...], gp_ref)
        logsig = jnp.minimum(z, 0.0) - sp
        tri = (_row((ts, ts)) >= _lane((ts, ts))).astype(F32)
        cum = jnp.dot(tri, logsig, precision=HIGHEST, preferred_element_type=F32) + carry_ref[0:1, :]
        carry_ref[0:1, :] = cum[ts - 1:ts, :]
        g = -jnp.exp(gp_ref[1:2, :]) * (jnp.maximum(z, 0.0) + sp)
        beta = jax.nn.sigmoid(z)
        lane = _lane((ts, 128))
        o_ref[...] = jnp.where(lane < 8, cum, jnp.where(lane < 12, g, jnp.where(lane < 16, beta, 0.0)))

    return pl.pallas_call(
        body, grid=(nb, ns),
        in_specs=[pl.BlockSpec((ts, 128), lambda b, s: (b * ns + s, COL_SMALL)),
                  pl.BlockSpec((8, 128), lambda b, s: (0, 0))],
        out_specs=pl.BlockSpec((ts, 128), lambda b, s: (b * ns + s, 0)),
        out_shape=jax.ShapeDtypeStruct((t, 128), F32),
        scratch_shapes=[pltpu.VMEM((8, 128), F32)],
        compiler_params=_cp("parallel", "arbitrary"), name="gates_fwd")(proj, gp)


def gates_bwd(proj, gp, dga, dgb, seq, ts=512):
    t = proj.shape[0]
    nb, ns = t // seq, seq // ts

    def body(v_ref, gp_ref, da_ref, db_ref, ds_ref, dgp_ref, carry_ref):
        @pl.when(pl.program_id(1) == 0)
        def _():
            carry_ref[...] = jnp.zeros_like(carry_ref)

        @pl.when((pl.program_id(0) == 0) & (pl.program_id(1) == 0))
        def _():
            dgp_ref[...] = jnp.zeros_like(dgp_ref)

        lane = _lane((ts, 128))
        dgate = jnp.where(lane < 8, da_ref[...], jnp.where(lane < 16, db_ref[...], 0.0))
        z, sp = _gate_terms(v_ref[...], gp_ref)
        triu = (_row((ts, ts)) <= _lane((ts, ts))).astype(F32)
        dlog = jnp.dot(triu, dgate, precision=HIGHEST, preferred_element_type=F32) + carry_ref[0:1, :]
        carry_ref[0:1, :] = dlog[0:1, :]
        sig = jax.nn.sigmoid(z)
        nea = -jnp.exp(gp_ref[1:2, :])
        g = nea * (jnp.maximum(z, 0.0) + sp)
        dz = jnp.where(lane < 8, dlog * (1.0 - sig),
                       jnp.where(lane < 12, dgate * nea * sig, dgate * sig * (1.0 - sig)))
        dz = jnp.where(lane < 16, dz, 0.0)
        ds_ref[...] = dz
        dgp_ref[0:1, :] += jnp.where(lane[0:1] < 12, jnp.sum(dz, axis=0, keepdims=True), 0.0)
        dgp_ref[1:2, :] += jnp.where((lane[0:1] >= 8) & (lane[0:1] < 12), jnp.sum(dgate * g, axis=0, keepdims=True), 0.0)

    rev = lambda b, s: (b * ns + (ns - 1 - s), 0)
    return pl.pallas_call(
        body, grid=(nb, ns),
        in_specs=[pl.BlockSpec((ts, 128), lambda b, s: (b * ns + (ns - 1 - s), COL_SMALL)),
                  pl.BlockSpec((8, 128), lambda b, s: (0, 0)),
                  pl.BlockSpec((ts, 128), rev), pl.BlockSpec((ts, 128), rev)],
        out_specs=[pl.BlockSpec((ts, 128), rev), pl.BlockSpec((8, 128), lambda b, s: (0, 0))],
        out_shape=[jax.ShapeDtypeStruct((t, 128), F32), jax.ShapeDtypeStruct((8, 128), F32)],
        scratch_shapes=[pltpu.VMEM((8, 128), F32)],
        compiler_params=_cp("arbitrary", "arbitrary"), name="gates_bwd")(proj, gp, dga, dgb)


NEG = -1e30
ATTN_TQ_FWD = 1024
ATTN_TQ_BWD = 512


def _pick_lane(tile, idx):
    return jnp.sum(jnp.where(_lane(tile.shape) == idx, tile, 0.0), axis=1, keepdims=True)


def _col_to_row(col, n):
    return jnp.sum(jnp.where(_row((n, n)) == _lane((n, n)), col, 0.0), axis=0, keepdims=True)


def _row_to_col(row, n):
    return jnp.sum(jnp.where(_row((n, n)) == _lane((n, n)), row, 0.0), axis=1, keepdims=True)


def _rows(i, n):
    return pl.ds(pl.multiple_of(i * n, n), n)


def _once(shape, index_map):
    return pl.BlockSpec(shape, index_map, pipeline_mode=pl.Buffered(1))


def attn_fwd(proj, gates, qw, kw, seq, tq=256):
    t = proj.shape[0]
    nb, nq, dh = t // seq, seq // tq, FOX_DH
    scale = dh ** -0.5

    def body(q_ref, k_ref, v_ref, g_ref, qw_ref, kw_ref, y_ref, lse_ref, qs, ks, vs, ccol, crow):
        p = pl.program_id(1)
        heads = range(2)

        def prep(i, _):
            r = _rows(i, tq)
            for hh in heads:
                lanes = slice(hh * dh, (hh + 1) * dh)
                qf, kf = q_ref[r, lanes], k_ref[r, lanes]
                qs[hh, r, :] = (qf * _rstd(qf) * qw_ref[...] * scale).astype(BF16)
                ks[hh, r, :] = (kf * _rstd(kf) * kw_ref[...]).astype(BF16)
                vs[hh, r, :] = v_ref[r, lanes].astype(BF16)
                cc = _pick_lane(g_ref[r, :], 2 * p + hh)
                ccol[hh, r, :] = cc
                crow[hh * nq + i] = _col_to_row(cc, tq)
            return 0

        lax.fori_loop(0, nq, prep, 0)

        def q_tile(i, _):
            r = _rows(i, tq)
            qt = [qs[hh, r, :] for hh in heads]
            cc = [ccol[hh, r, :] for hh in heads]

            def kv_step(j, carry, masked):
                kr = _rows(j, tq)
                out = []
                for hh in heads:
                    m, l, acc = carry[hh]
                    s = _dot_nt(qt[hh], ks[hh, kr, :]) + (cc[hh] - crow[hh * nq + j])
                    if masked:
                        s = jnp.where(_row((tq, tq)) >= _lane((tq, tq)), s, NEG)
                    m_new = jnp.maximum(m, jnp.max(s, axis=1, keepdims=True))
                    pe = jnp.exp(s - m_new)
                    a = jnp.exp(m - m_new)
                    out.append((m_new, a * l + jnp.sum(pe, axis=1, keepdims=True),
                                a * acc + _dot(pe.astype(BF16), vs[hh, kr, :])))
                return tuple(out)

            one = (jnp.full((tq, 1), NEG, F32), jnp.zeros((tq, 1), F32), jnp.zeros((tq, dh), F32))
            carry = lax.fori_loop(0, i, lambda j, c: kv_step(j, c, False), (one, one))
            carry = kv_step(i, carry, True)
            for hh in heads:
                m, l, acc = carry[hh]
                lanes = slice(hh * dh, (hh + 1) * dh)
                y_ref[r, lanes] = acc / l
                lse_ref[r, lanes] = jnp.broadcast_to(m + jnp.log(l), (tq, dh))
            return 0

        lax.fori_loop(0, nq, q_tile, 0)

    blk = lambda off: _once((seq, 128), lambda b, p: (b, off + p))
    return pl.pallas_call(
        body, grid=(nb, 4),
        in_specs=[blk(0), blk(4), blk(8), _once((seq, 128), lambda b, p: (b, 0)),
                  pl.BlockSpec((1, dh), lambda b, p: (0, 0)), pl.BlockSpec((1, dh), lambda b, p: (0, 0))],
        out_specs=[pl.BlockSpec((seq, 128), lambda b, p: (b, p)), pl.BlockSpec((seq, 128), lambda b, p: (b, p))],
        out_shape=[jax.ShapeDtypeStruct((t, 512), F32), jax.ShapeDtypeStruct((t, 512), F32)],
        scratch_shapes=[pltpu.VMEM((2, seq, dh), BF16), pltpu.VMEM((2, seq, dh), BF16), pltpu.VMEM((2, seq, dh), BF16),
                        pltpu.VMEM((2, seq, 1), F32), pltpu.VMEM((2 * nq, 1, tq), F32)],
        compiler_params=_cp("parallel", "arbitrary"), name="attn_fwd")(proj, proj, proj, gates, qw, kw)


def attn_bwd(proj, gates, qw, kw, y, lse, dy, seq, tq=256):
    t = proj.shape[0]
    nb, nq, dh = t // seq, seq // tq, FOX_DH
    scale = dh ** -0.5

    def body(q_ref, k_ref, v_ref, g_ref, qw_ref, kw_ref, y_ref, lse_ref, dy_ref,
             dq_ref, dk_ref, dv_ref, dg_ref, dqw_ref, dkw_ref,
             qs, ks, vs, dos, cols, crow, dqa, dka):
        b, p = pl.program_id(0), pl.program_id(1)

        @pl.when((b == 0) & (p == 0))
        def _():
            dqw_ref[...] = jnp.zeros_like(dqw_ref)
            dkw_ref[...] = jnp.zeros_like(dkw_ref)

        @pl.when(p == 0)
        def _():
            dg_ref[...] = jnp.zeros_like(dg_ref)

        heads = range(2)
        hl = lambda hh: slice(hh * dh, (hh + 1) * dh)

        def prep(i, _):
            r = _rows(i, tq)
            for hh in heads:
                lanes = hl(hh)
                qf, kf = q_ref[r, lanes], k_ref[r, lanes]
                qs[hh, r, :] = (qf * _rstd(qf) * qw_ref[...] * scale).astype(BF16)
                ks[hh, r, :] = (kf * _rstd(kf) * kw_ref[...]).astype(BF16)
                vs[hh, r, :] = v_ref[r, lanes].astype(BF16)
                dyf = dy_ref[r, lanes]
                dos[hh, r, :] = dyf.astype(BF16)
                cc = _pick_lane(g_ref[r, :], 2 * p + hh)
                crow[hh * nq + i] = _col_to_row(cc, tq)
                delta = jnp.sum(dyf * y_ref[r, lanes], axis=1, keepdims=True)
                lane = _lane((tq, 128))
                cols[hh, r, :] = jnp.where(lane == 0, cc, jnp.where(lane == 1, lse_ref[r, hh * dh:hh * dh + 1],
                                                                     jnp.where(lane == 2, delta, 0.0)))
                dqa[hh, r, :] = jnp.zeros((tq, dh), F32)
            return 0

        lax.fori_loop(0, nq, prep, 0)

        def kv_tile(j, _):
            kr = _rows(j, tq)
            kt = [ks[hh, kr, :] for hh in heads]
            vt = [vs[hh, kr, :] for hh in heads]
            cr = [crow[hh * nq + j] for hh in heads]

            def q_step(i, carry, masked):
                r = _rows(i, tq)
                out = []
                for hh in heads:
                    dk, dv, dcr = carry[hh]
                    qt, dot, cl = qs[hh, r, :], dos[hh, r, :], cols[hh, r, :]
                    s = _dot_nt(qt, kt[hh]) + (cl[:, 0:1] - cr[hh])
                    if masked:
                        s = jnp.where(_row((tq, tq)) >= _lane((tq, tq)), s, NEG)
                    pe = jnp.exp(s - cl[:, 1:2])
                    ds = pe * (_dot_nt(dot, vt[hh]) - cl[:, 2:3])
                    dsb = ds.astype(BF16)
                    dqa[hh, r, :] += _dot(dsb, kt[hh])
                    cols[hh, r, :] = cl + jnp.where(_lane((tq, 128)) == 3, jnp.sum(ds, axis=1, keepdims=True), 0.0)
                    out.append((dk + _dot_tn(dsb, qt), dv + _dot_tn(pe.astype(BF16), dot),
                                dcr - jnp.sum(ds, axis=0, keepdims=True)))
                return tuple(out)

            one = (jnp.zeros((tq, dh), F32), jnp.zeros((tq, dh), F32), jnp.zeros((1, tq), F32))
            carry = q_step(j, (one, one), True)
            carry = lax.fori_loop(j + 1, nq, lambda i, c: q_step(i, c, False), carry)
            for hh in heads:
                dk, dv, dcr = carry[hh]
                dka[hh, kr, :] = dk
                dv_ref[kr, hl(hh)] = dv
                dg_ref[kr, :] = jnp.where(_lane((tq, 128)) == 2 * p + hh, _row_to_col(dcr, tq), dg_ref[kr, :])
            return 0

        lax.fori_loop(0, nq, kv_tile, 0)

        def post(i, _):
            r = _rows(i, tq)
            for hh in heads:
                lanes = hl(hh)
                qf, kf = q_ref[r, lanes], k_ref[r, lanes]
                rq, rk = _rstd(qf), _rstd(kf)
                dqn, dkn = dqa[hh, r, :] * scale, dka[hh, r, :]
                dqw_ref[...] += jnp.sum(dqn * qf * rq, axis=0, keepdims=True)
                dkw_ref[...] += jnp.sum(dkn * kf * rk, axis=0, keepdims=True)
                dq_ref[r, lanes] = _rms_bwd(qf, rq, dqn * qw_ref[...])
                dk_ref[r, lanes] = _rms_bwd(kf, rk, dkn * kw_ref[...])
                dg_ref[r, :] += jnp.where(_lane((tq, 128)) == 2 * p + hh, cols[hh, r, 3:4], 0.0)
            return 0

        lax.fori_loop(0, nq, post, 0)

    blk = lambda off: _once((seq, 128), lambda b, p: (b, off + p))
    own = lambda: _once((seq, 128), lambda b, p: (b, p))
    vec = lambda: pl.BlockSpec((1, dh), lambda b, p: (0, 0))
    return pl.pallas_call(
        body, grid=(nb, 4),
        in_specs=[blk(0), blk(4), blk(8), _once((seq, 128), lambda b, p: (b, 0)), vec(), vec(), own(), own(), own()],
        out_specs=[own(), own(), own(), _once((seq, 128), lambda b, p: (b, 0)), vec(), vec()],
        out_shape=[jax.ShapeDtypeStruct((t, 512), F32)] * 3
                  + [jax.ShapeDtypeStruct((t, 128), F32), jax.ShapeDtypeStruct((1, dh), F32), jax.ShapeDtypeStruct((1, dh), F32)],
        scratch_shapes=[pltpu.VMEM((2, seq, dh), BF16)] * 4
                       + [pltpu.VMEM((2, seq, 128), F32), pltpu.VMEM((2 * nq, 1, tq), F32),
                          pltpu.VMEM((2, seq, dh), F32), pltpu.VMEM((2, seq, dh), F32)],
        compiler_params=_cp("arbitrary", "arbitrary"), name="attn_bwd")(proj, proj, proj, gates, qw, kw, y, lse, dy)


def _silu_grad(c, sg):
    return sg * (1.0 + c * (1.0 - sg))


def _conv(x, w, n):
    row = _row(x.shape)
    c = x * w[CONV_W - 1:CONV_W, :]
    for k in range(CONV_W - 1):
        sh = CONV_W - 1 - k
        c = c + w[k:k + 1, :] * jnp.where(row >= sh, pltpu.roll(x, sh, 0), 0.0)
    return c


def gdn_pre_fwd(proj, cw, seq):
    t = proj.shape[0]
    nb = t // seq
    scale = GDN_DH ** -0.5

    def body(xq_ref, xk_ref, xv_ref, wq_ref, wk_ref, wv_ref, q_ref, k_ref, v_ref):
        def act(x_ref, w_ref):
            c = _conv(x_ref[...], w_ref[...], seq)
            return c * jax.nn.sigmoid(c)

        aq, ak = act(xq_ref, wq_ref), act(xk_ref, wk_ref)
        q_ref[...] = aq * lax.rsqrt(jnp.sum(aq * aq, axis=1, keepdims=True) + EPS) * scale
        k_ref[...] = ak * lax.rsqrt(jnp.sum(ak * ak, axis=1, keepdims=True) + EPS)
        v_ref[...] = act(xv_ref, wv_ref)

    xb = lambda off: pl.BlockSpec((seq, 128), lambda b, h: (b, off + h))
    wb = lambda off: pl.BlockSpec((CONV_W, 128), lambda b, h: (0, off + h))
    ob = lambda: pl.BlockSpec((seq, 128), lambda b, h: (b, h))
    return pl.pallas_call(
        body, grid=(nb, GDN_HEADS),
        in_specs=[xb(12), xb(16), xb(20), wb(0), wb(4), wb(8)],
        out_specs=[ob(), ob(), ob()],
        out_shape=[jax.ShapeDtypeStruct((t, 512), F32)] * 3,
        compiler_params=_cp("parallel", "parallel"), name="gdn_pre_fwd")(proj, proj, proj, cw, cw, cw)


def gdn_pre_bwd(proj, cw, dq, dk, dv, seq):
    t = proj.shape[0]
    nb = t // seq
    scale = GDN_DH ** -0.5

    def body(xq_ref, xk_ref, xv_ref, wq_ref, wk_ref, wv_ref, dq_ref, dk_ref, dv_ref,
             dxq_ref, dxk_ref, dxv_ref, dwq_ref, dwk_ref, dwv_ref):
        first = pl.program_id(1) == 0
        row = _row((seq, 128))

        def one(x_ref, w_ref, dy_ref, dx_ref, dw_ref, norm, sc):
            x, w = x_ref[...], w_ref[...]
            c = _conv(x, w, seq)
            sg = jax.nn.sigmoid(c)
            dy = dy_ref[...]
            if norm:
                a = c * sg
                rs = lax.rsqrt(jnp.sum(a * a, axis=1, keepdims=True) + EPS)
                dy = dy * sc
                da = rs * dy - a * (rs * rs * rs) * jnp.sum(dy * a, axis=1, keepdims=True)
            else:
                da = dy
            dc = da * _silu_grad(c, sg)
            dx = dc * w[CONV_W - 1:CONV_W, :]
            dws = [None] * CONV_W
            dws[CONV_W - 1] = jnp.sum(dc * x, axis=0, keepdims=True)
            for k in range(CONV_W - 1):
                sh = CONV_W - 1 - k
                dx = dx + w[k:k + 1, :] * jnp.where(row < seq - sh, pltpu.roll(dc, seq - sh, 0), 0.0)
                dws[k] = jnp.sum(dc * jnp.where(row >= sh, pltpu.roll(x, sh, 0), 0.0), axis=0, keepdims=True)
            dx_ref[...] = dx
            dwn = jnp.concatenate(dws, axis=0)

            @pl.when(first)
            def _():
                dw_ref[...] = dwn

            @pl.when(jnp.logical_not(first))
            def _():
                dw_ref[...] += dwn

        one(xq_ref, wq_ref, dq_ref, dxq_ref, dwq_ref, True, scale)
        one(xk_ref, wk_ref, dk_ref, dxk_ref, dwk_ref, True, 1.0)
        one(xv_ref, wv_ref, dv_ref, dxv_ref, dwv_ref, False, 1.0)

    xb = lambda off: pl.BlockSpec((seq, 128), lambda h, b: (b, off + h))
    wb = lambda off: pl.BlockSpec((CONV_W, 128), lambda h, b: (0, off + h))
    ob = lambda: pl.BlockSpec((seq, 128), lambda h, b: (b, h))
    return pl.pallas_call(
        body, grid=(GDN_HEADS, nb),
        in_specs=[xb(12), xb(16), xb(20), wb(0), wb(4), wb(8), ob(), ob(), ob()],
        out_specs=[ob(), ob(), ob()] + [pl.BlockSpec((CONV_W, 128), lambda h, b: (0, h))] * 3,
        out_shape=[jax.ShapeDtypeStruct((t, 512), F32)] * 3 + [jax.ShapeDtypeStruct((CONV_W, 512), F32)] * 3,
        compiler_params=_cp("parallel", "arbitrary"), name="gdn_pre_bwd")(proj, proj, proj, cw, cw, cw, dq, dk, dv)


def _b16(x):
    return x.astype(BF16)


@jax.custom_vjp
def _mm(a, b):
    return _dot(_b16(a), _b16(b))


_mm.defvjp(lambda a, b: (_mm(a, b), (a, b)),
           lambda res, g: (_dot_nt(_b16(g), _b16(res[1])), _dot_tn(_b16(res[0]), _b16(g))))


@jax.custom_vjp
def _mm_nt(a, b):
    return _dot_nt(_b16(a), _b16(b))


_mm_nt.defvjp(lambda a, b: (_mm_nt(a, b), (a, b)),
              lambda res, g: (_dot(_b16(g), _b16(res[1])), _dot_tn(_b16(g), _b16(res[0]))))


@jax.custom_vjp
def _mm_tn(a, b):
    return _dot_tn(_b16(a), _b16(b))


_mm_tn.defvjp(lambda a, b: (_mm_tn(a, b), (a, b)),
              lambda res, g: (_dot_nt(_b16(res[1]), _b16(g)), _dot(_b16(res[0]), _b16(g))))


def _dot32(a, b, dims=(((1,), (0,)), ((), ()))):
    def split(x):
        hi = x.astype(BF16)
        return hi, (x - hi.astype(F32)).astype(BF16)

    (ah, al), (bh, bl) = split(a), split(b)
    d = lambda x, y: lax.dot_general(x, y, dims, preferred_element_type=F32)
    return d(ah, bh) + (d(ah, bl) + d(al, bh))


def _inv_fwd_many(mats):
    n = mats[0].shape[0]
    eye = (_row((n, n)) == _lane((n, n))).astype(F32)
    invs, pws = [eye - a for a in mats], list(mats)
    for _ in range(n.bit_length() - 2):
        pws = [_dot32(pw, pw) for pw in pws]
        invs = [inv + _dot32(inv, pw) for inv, pw in zip(invs, pws)]
    return invs


@jax.custom_vjp
def _inv_saved(a, inv):
    return inv


def _inv_saved_bwd(inv, g):
    tg = _dot32(inv, g, (((0,), (0,)), ((), ())))
    return -_dot32(tg, inv, (((1,), (1,)), ((), ()))), jnp.zeros_like(inv)


_inv_saved.defvjp(lambda a, inv: (inv, inv), _inv_saved_bwd)


def _gdn_decay(gcol):
    c = CHUNK
    ri, ci = _row((c, c)), _lane((c, c))
    incl, eye = ri >= ci, ri == ci
    grow = jnp.sum(jnp.where(eye, gcol, 0.0), axis=0, keepdims=True)
    gc = jnp.sum(jnp.where(incl, grow, 0.0), axis=1, keepdims=True)
    gcr = jnp.sum(jnp.where(eye, gc, 0.0), axis=0, keepdims=True)
    gl = jnp.sum(jnp.where(_row((c, 1)) == c - 1, gc, 0.0), axis=0, keepdims=True)
    return gc, gl, jnp.exp(jnp.where(incl, gc - gcr, NEG))


def _gdn_a(k, bcol, decay):
    c = CHUNK
    return jnp.where(_row((c, c)) > _lane((c, c)), _mm_nt(k * bcol, k) * decay, 0.0)


def _gdn_chunk(q, k, v, gcol, bcol, state, gg, nw, inv_saved):
    c = CHUNK
    incl = _row((c, c)) >= _lane((c, c))
    gc, gl, decay = _gdn_decay(gcol)
    kb, vb = k * bcol, v * bcol
    inv = _inv_saved(_gdn_a(k, bcol, decay), inv_saved)
    eg = jnp.exp(gc)
    u = _mm(inv, vb)
    w = _mm(inv, kb * eg)
    pm = jnp.where(incl, _mm_nt(q, k) * decay, 0.0)
    kd = k * jnp.exp(gl - gc)
    qd = q * eg
    v_new = u - _mm(w, state)
    o = _mm(qd, state) + _mm(pm, v_new)
    state_new = state * jnp.exp(gl) + _mm_tn(kd, v_new)
    y = o * _rstd(o) * nw * (gg * jax.nn.sigmoid(gg))
    return y, state_new


_gdn_chunks = jax.vmap(_gdn_chunk, in_axes=(0, 0, 0, 0, 0, 0, 0, None, 0))


def _gdn_chain_inputs(chains, p, r, c, q_ref, k_ref, v_ref, g_ref, gg_ref, inv_ref):
    cols = {nm: [] for nm in ("q", "k", "v", "g", "b", "gg", "inv")}
    for b, hh in chains:
        h = 2 * p + hh
        ln = slice(hh * 128, (hh + 1) * 128)
        gt = g_ref[b, r, :]
        cols["q"].append(q_ref[b, r, ln])
        cols["k"].append(k_ref[b, r, ln])
        cols["v"].append(v_ref[b, r, ln])
        cols["g"].append(_pick_lane(gt, 8 + h))
        cols["b"].append(_pick_lane(gt, 12 + h))
        cols["gg"].append(gg_ref[b, r, ln])
        cols["inv"].append(inv_ref[b, hh, c])
    return [jnp.stack(cols[nm]) for nm in ("q", "k", "v", "g", "b", "gg", "inv")]


GDN_CB = 8


def gdn_inv(k, gates, seq):
    t = k.shape[0]
    nb, nc = t // seq, seq // CHUNK
    rb = GDN_CB * CHUNK
    nsb = seq // rb

    def body(k_ref, g_ref, o_ref):
        h = pl.program_id(1)
        mats = []
        for c in range(GDN_CB):
            r = slice(c * CHUNK, (c + 1) * CHUNK)
            gt = g_ref[r, :]
            _, _, decay = _gdn_decay(_pick_lane(gt, 8 + h))
            mats.append(_gdn_a(k_ref[r, :], _pick_lane(gt, 12 + h), decay))
        for c, inv in enumerate(_inv_fwd_many(mats)):
            o_ref[c] = inv

    return pl.pallas_call(
        body, grid=(nb, GDN_HEADS, nsb),
        in_specs=[pl.BlockSpec((rb, 128), lambda b, h, s: (b * nsb + s, h)),
                  pl.BlockSpec((rb, 128), lambda b, h, s: (b * nsb + s, 0))],
        out_specs=pl.BlockSpec((None, None, GDN_CB, CHUNK, CHUNK), lambda b, h, s: (b, h, s, 0, 0)),
        out_shape=jax.ShapeDtypeStruct((nb, GDN_HEADS, nc, CHUNK, CHUNK), F32),
        compiler_params=_cp("parallel", "parallel", "parallel"), name="gdn_inv")(k, gates)


def _gdn_specs(nb, nsb, rev):
    blk = (lambda s: nsb - 1 - s) if rev else (lambda s: s)
    rb = GDN_CB * CHUNK
    pair = lambda off=0: pl.BlockSpec((nb, rb, 256), lambda s, p: (0, blk(s), off + p))
    gate = lambda: pl.BlockSpec((nb, rb, 128), lambda s, p: (0, blk(s), 0))
    mats = lambda n: pl.BlockSpec((nb, 2, GDN_CB, n, n), lambda s, p: (0, p, blk(s), 0, 0))
    return pair, gate, mats


def gdn_fwd(q, k, v, gates, proj, nw, inv, seq):
    t = q.shape[0]
    nb, nc = t // seq, seq // CHUNK
    nsb = nc // GDN_CB
    chains = [(b, hh) for b in range(nb) for hh in range(2)]
    nch = len(chains)
    pair, gate, mats = _gdn_specs(nb, nsb, False)

    def body(q_ref, k_ref, v_ref, g_ref, gg_ref, inv_ref, nw_ref, y_ref, st_ref, carry):
        s, p = pl.program_id(0), pl.program_id(1)

        @pl.when(s == 0)
        def _():
            for ci in range(nch):
                carry[p * nch + ci] = jnp.zeros((GDN_DH, GDN_DH), F32)

        def step(c, states):
            r = _rows(c, CHUNK)
            for ci, (b, hh) in enumerate(chains):
                st_ref[b, hh, c] = states[ci]
            ins = _gdn_chain_inputs(chains, p, r, c, q_ref, k_ref, v_ref, g_ref, gg_ref, inv_ref)
            y, states = _gdn_chunks(*ins[:5], states, ins[5], nw_ref[...], ins[6])
            for ci, (b, hh) in enumerate(chains):
                y_ref[b, r, hh * 128:(hh + 1) * 128] = y[ci]
            return states

        states = lax.fori_loop(0, GDN_CB, step, jnp.stack([carry[p * nch + ci] for ci in range(nch)]))
        for ci in range(nch):
            carry[p * nch + ci] = states[ci]

    v3 = lambda a: a.reshape(nb, seq, a.shape[1])
    y, st = pl.pallas_call(
        body, grid=(nsb, 2),
        in_specs=[pair(), pair(), pair(), gate(), pair(12), mats(CHUNK), pl.BlockSpec((1, 128), lambda s, p: (0, 0))],
        out_specs=[pair(), mats(GDN_DH)],
        out_shape=[jax.ShapeDtypeStruct((nb, seq, 512), F32),
                   jax.ShapeDtypeStruct((nb, GDN_HEADS, nc, GDN_DH, GDN_DH), F32)],
        scratch_shapes=[pltpu.VMEM((2 * nch, GDN_DH, GDN_DH), F32)],
        compiler_params=_cp("arbitrary", "arbitrary"), name="gdn_fwd")(v3(q), v3(k), v3(v), v3(gates), v3(proj), inv, nw)
    return y.reshape(t, 512), st


def gdn_bwd(q, k, v, gates, proj, nw, inv, states, dy, seq):
    t = q.shape[0]
    nb, nc = t // seq, seq // CHUNK
    nsb = nc // GDN_CB
    chains = [(b, hh) for b in range(nb) for hh in range(2)]
    nch = len(chains)
    pair, gate, mats = _gdn_specs(nb, nsb, True)

    def body(q_ref, k_ref, v_ref, g_ref, gg_ref, inv_ref, st_ref, dy_ref, nw_ref,
             dq_ref, dk_ref, dv_ref, dgg_ref, dg_ref, dnw_ref, carry):
        s, p = pl.program_id(0), pl.program_id(1)

        @pl.when((s == 0) & (p == 0))
        def _():
            dnw_ref[...] = jnp.zeros_like(dnw_ref)

        @pl.when(p == 0)
        def _():
            dg_ref[...] = jnp.zeros_like(dg_ref)

        @pl.when(s == 0)
        def _():
            for ci in range(nch):
                carry[p * nch + ci] = jnp.zeros((GDN_DH, GDN_DH), F32)

        def step(idx, dstates):
            c = GDN_CB - 1 - idx
            r = _rows(c, CHUNK)
            ins = _gdn_chain_inputs(chains, p, r, c, q_ref, k_ref, v_ref, g_ref, gg_ref, inv_ref)
            st = jnp.stack([st_ref[b, hh, c] for b, hh in chains])
            dy = jnp.stack([dy_ref[b, r, hh * 128:(hh + 1) * 128] for b, hh in chains])
            _, vjp = jax.vjp(_gdn_chunks, *ins[:5], st, ins[5], nw_ref[...], ins[6])
            dq, dk, dv, dgc, dbc, dstates, dgg, dnw, _ = vjp((dy, dstates))
            dnw_ref[...] += dnw
            lane = _lane((CHUNK, 128))
            for ci, (b, hh) in enumerate(chains):
                h = 2 * p + hh
                ln = slice(hh * 128, (hh + 1) * 128)
                dq_ref[b, r, ln] = dq[ci]
                dk_ref[b, r, ln] = dk[ci]
                dv_ref[b, r, ln] = dv[ci]
                dgg_ref[b, r, ln] = dgg[ci]
                dg_ref[b, r, :] = jnp.where(lane == 8 + h, dgc[ci], jnp.where(lane == 12 + h, dbc[ci], dg_ref[b, r, :]))
            return dstates

        dstates = lax.fori_loop(0, GDN_CB, step, jnp.stack([carry[p * nch + ci] for ci in range(nch)]))
        for ci in range(nch):
            carry[p * nch + ci] = dstates[ci]

    v3 = lambda a: a.reshape(nb, seq, a.shape[1])
    res = pl.pallas_call(
        body, grid=(nsb, 2),
        in_specs=[pair(), pair(), pair(), gate(), pair(12), mats(CHUNK), mats(GDN_DH), pair(),
                  pl.BlockSpec((1, 128), lambda s, p: (0, 0))],
        out_specs=[pair(), pair(), pair(), pair(), gate(), pl.BlockSpec((1, 128), lambda s, p: (0, 0))],
        out_shape=[jax.ShapeDtypeStruct((nb, seq, 512), F32)] * 4
                  + [jax.ShapeDtypeStruct((nb, seq, 128), F32), jax.ShapeDtypeStruct((1, 128), F32)],
        scratch_shapes=[pltpu.VMEM((2 * nch, GDN_DH, GDN_DH), F32)],
        compiler_params=_cp("arbitrary", "arbitrary"),
        name="gdn_bwd")(v3(q), v3(k), v3(v), v3(gates), v3(proj), inv, states, v3(dy), nw)
    return [a.reshape(t, a.shape[2]) for a in res[:5]] + [res[5]]


def loss_head(y, target, tm=512):
    t, d = y.shape

    def body(y_ref, t_ref, s_ref, dy_ref):
        @pl.when(pl.program_id(0) == 0)
        def _():
            s_ref[...] = jnp.zeros_like(s_ref)

        err = y_ref[...] - t_ref[...]
        s_ref[...] += jnp.sum(err * err, axis=0, keepdims=True)
        dy_ref[...] = err * (1.0 / d)

    return pl.pallas_call(
        body, grid=(t // tm,),
        in_specs=[pl.BlockSpec((tm, d), lambda i: (i, 0)), pl.BlockSpec((tm, d), lambda i: (i, 0))],
        out_specs=[pl.BlockSpec((1, d), lambda i: (0, 0)), pl.BlockSpec((tm, d), lambda i: (i, 0))],
        out_shape=[jax.ShapeDtypeStruct((1, d), F32), jax.ShapeDtypeStruct((t, d), F32)],
        compiler_params=_cp("arbitrary"), name="loss_head")(y, target)


def _place():
    return lax.axis_index("x"), lax.axis_index("y"), lax.axis_index("c")


def _peer(k):
    x, y, c = _place()
    px = 1 - x if (k >> 2) & 1 else x
    py = 1 - y if (k >> 1) & 1 else y
    pc = 1 - c if k & 1 else c
    return (px, py, pc), 4 * px + 2 * py + pc


_ANY = pl.BlockSpec(memory_space=pl.ANY)
_SEM = pl.BlockSpec(memory_space=pltpu.SEMAPHORE)
_EFFECT = pltpu.SideEffectType.DATAFLOW_SIDE_EFFECTING


def _me():
    x, y, c = _place()
    return 4 * x + 2 * y + c


def _remote_copy(ins, lands, scatter, send_sems, recv_sems, a, k, arriving):
    pid, pidx = _peer(k)
    return pltpu.make_async_remote_copy(src_ref=ins[a].at[pidx] if scatter[a] else ins[a],
                                        dst_ref=lands[a].at[pidx if arriving else _me()],
                                        send_sem=send_sems.at[a * N_DEV + k], recv_sem=recv_sems.at[a * N_DEV + k],
                                        device_id=pid, device_id_type=MESH)


def _local_copy(ins, lands, scatter, loc_sems, a):
    me = _me()
    return pltpu.make_async_copy(ins[a].at[me] if scatter[a] else ins[a], lands[a].at[me], loc_sems.at[a])


def exchange_start(arrays, scatter, name, after):
    n = len(arrays)
    lands = [lax.empty(a.shape if s else (N_DEV,) + a.shape, a.dtype) for a, s in zip(arrays, scatter)]

    def body(*refs):
        ins, lds = refs[:n], refs[n:2 * n]
        send_sems, recv_sems, loc_sems = refs[2 * n + 1:2 * n + 4]
        token = refs[-1]
        for k in range(1, N_DEV):
            for a in range(n):
                _remote_copy(ins, lds, scatter, send_sems, recv_sems, a, k, False).start()
        for a in range(n):
            _local_copy(ins, lds, scatter, loc_sems, a).start()
        token[...] = jnp.zeros_like(token)

    hbm = lambda a: pltpu.HBM(a.shape, a.dtype)
    res = pl.pallas_call(
        body, name=name,
        in_specs=[_ANY] * (2 * n + 1),
        out_specs=[_SEM, _SEM, _SEM] + [_ANY] * (2 * n) + [pl.BlockSpec(memory_space=pltpu.VMEM)],
        out_shape=[pltpu.SemaphoreType.DMA((n * N_DEV,)), pltpu.SemaphoreType.DMA((n * N_DEV,)),
                   pltpu.SemaphoreType.DMA((n,))]
                  + [hbm(a) for a in arrays] + [hbm(a) for a in lands] + [jax.ShapeDtypeStruct((8, 128), F32)],
        input_output_aliases={i: 3 + i for i in range(2 * n)},
        compiler_params=pltpu.CompilerParams(has_side_effects=_EFFECT),
    )(*[pltpu.with_memory_space_constraint(a, pltpu.HBM) for a in list(arrays) + lands], after)
    return res[0:3], res[3:3 + n], res[3 + n:3 + 2 * n], res[-1]


def exchange_wait(sems, arrays, lands, scatter, after, name):
    n = len(arrays)

    def body(*refs):
        ins, lds = refs[:n], refs[n:2 * n]
        ssem, rsem, lsem = refs[2 * n:2 * n + 3]
        for a in range(n):
            _local_copy(ins, lds, scatter, lsem, a).wait()
        for k in range(1, N_DEV):
            for a in range(n):
                _remote_copy(ins, lds, scatter, ssem, rsem, a, k, True).wait_recv()
        for k in range(1, N_DEV):
            for a in range(n):
                _remote_copy(ins, lds, scatter, ssem, rsem, a, k, False).wait_send()

    hbm = lambda a: pltpu.HBM(a.shape, a.dtype)
    res = pl.pallas_call(
        body, name=name,
        in_specs=[_ANY] * (2 * n) + [_SEM, _SEM, _SEM, _ANY],
        out_specs=[_ANY] * (2 * n),
        out_shape=[hbm(a) for a in arrays] + [hbm(a) for a in lands],
        input_output_aliases={i: i for i in range(2 * n)},
        compiler_params=pltpu.CompilerParams(has_side_effects=_EFFECT),
    )(*arrays, *lands, *sems, after)
    return list(res[n:])


def exchange_begin(arrays, scatter, name, after):
    sems, arrays_thru, lands_thru, token = exchange_start(arrays, scatter, name + "_start", after)
    return (sems, arrays_thru, lands_thru, scatter, name), token


def exchange_end(state, after):
    sems, arrays_thru, lands_thru, scatter, name = state
    return exchange_wait(sems, arrays_thru, lands_thru, scatter, after, name + "_wait")


def adamw_reduce(slots, w, m, v, name, after=None):
    r, c = w.shape
    tr = r
    while tr * c * 4 > (1 << 20) and tr % 16 == 0:
        tr //= 2
    bc1 = 1.0 - ADAM_B1 ** ADAM_STEP
    bc2 = 1.0 - ADAM_B2 ** ADAM_STEP

    def body(s_ref, w_ref, m_ref, v_ref, *rest):
        g_ref, d_ref, nm_ref, nv_ref = rest[-4:]
        g = s_ref[0].astype(F32)
        for j in range(1, N_DEV):
            g = g + s_ref[j].astype(F32)
        nm = ADAM_B1 * m_ref[...] + (1.0 - ADAM_B1) * g
        nv = ADAM_B2 * v_ref[...] + (1.0 - ADAM_B2) * (g * g)
        g_ref[...] = g
        nm_ref[...] = nm
        nv_ref[...] = nv
        d_ref[...] = -ADAM_LR * ((nm / bc1) / (jnp.sqrt(nv / bc2) + ADAM_EPS) + ADAM_WD * w_ref[...])

    blk = lambda: pl.BlockSpec((tr, c), lambda i: (i, 0))
    extra = [] if after is None else [after]
    return pl.pallas_call(
        body, grid=(r // tr,),
        in_specs=[pl.BlockSpec((N_DEV, tr, c), lambda i: (0, i, 0)), blk(), blk(), blk()] + [_ANY] * len(extra),
        out_specs=[blk(), blk(), blk(), blk()],
        out_shape=[jax.ShapeDtypeStruct((r, c), F32)] * 4,
        compiler_params=_cp("parallel"), name=name)(slots, w, m, v, *extra)


BIG = ("ffn1_w_in", "ffn1_w_out", "w_in", "gdn_conv", "w_out", "ffn2_w_in", "ffn2_w_out")
GROUPS = (BIG[0:2], BIG[2:5], BIG[5:7])
SMALL = ("ffn1_norm", "mix_norm", "fox_q_norm", "fox_k_norm", "fox_f_bias", "gdn_a_log", "gdn_dt_bias",
         "gdn_out_norm", "ffn2_norm")
WEIGHTS = ("ffn1_norm", "ffn1_w_in", "ffn1_w_out", "mix_norm", "w_in", "fox_q_norm", "fox_k_norm", "fox_f_bias",
           "gdn_conv", "gdn_a_log", "gdn_dt_bias", "gdn_out_norm", "w_out", "ffn2_norm", "ffn2_w_in", "ffn2_w_out")
IN_COLS = (("fq", 512), ("fk", 512), ("fv", 512), ("ff", 8), ("gq", 512), ("gk", 512), ("gv", 512),
           ("ga", 4), ("gb", 4), ("gg", 512))
MY_BIG = ("fq", "fk", "fv", "gq", "gk", "gv", "gg")
MY_SMALL = ("ff", "ga", "gb")
SMALL_ROWS = 8 * 128


def _in_cols_to_mine(w):
    off, parts = 0, {}
    for nm, wd in IN_COLS:
        parts[nm] = w[:, off:off + wd]
        off += wd
    small = jnp.concatenate([parts[nm] for nm in MY_SMALL], axis=1)
    small = jnp.pad(small, ((0, 0), (0, 128 - small.shape[1])))
    return jnp.concatenate([parts[nm] for nm in MY_BIG] + [small], axis=1)


def _in_cols_from_mine(g):
    parts = {nm: g[:, i * 512:(i + 1) * 512] for i, nm in enumerate(MY_BIG)}
    off = N_BIG
    for nm in MY_SMALL:
        wd = dict(IN_COLS)[nm]
        parts[nm] = g[:, off:off + wd]
        off += wd
    return jnp.concatenate([parts[nm] for nm, _ in IN_COLS], axis=1)


def _pack_small(vals):
    rows = []
    nl = vals[SMALL[0]].shape[0]
    for l in range(nl):
        for nm in SMALL:
            v = vals[nm][l].reshape(-1)
            pad = (-v.shape[0]) % SMALL_ROWS
            rows.append(jnp.pad(v, (0, pad)).reshape(-1, 128))
    return jnp.concatenate(rows, axis=0)


def _unpack_small(packed, like):
    out = {nm: [] for nm in SMALL}
    row = 0
    nl = like[SMALL[0]].shape[0]
    for l in range(nl):
        for nm in SMALL:
            n = like[nm].shape[1]
            nr = -(-n // SMALL_ROWS) * 8
            out[nm].append(packed[row:row + nr].reshape(-1)[:n])
            row += nr
    return {nm: jnp.stack(v) for nm, v in out.items()}


def kernel(x, ffn1_norm, ffn1_w_in, ffn1_w_out, mix_norm, w_in, fox_q_norm, fox_k_norm, fox_f_bias, gdn_conv, gdn_a_log, gdn_dt_bias, gdn_out_norm, w_out, ffn2_norm, ffn2_w_in, ffn2_w_out, loss_target, m_ffn1_norm, m_ffn1_w_in, m_ffn1_w_out, m_mix_norm, m_w_in, m_fox_q_norm, m_fox_k_norm, m_fox_f_bias, m_gdn_conv, m_gdn_a_log, m_gdn_dt_bias, m_gdn_out_norm, m_w_out, m_ffn2_norm, m_ffn2_w_in, m_ffn2_w_out, v_ffn1_norm, v_ffn1_w_in, v_ffn1_w_out, v_mix_norm, v_w_in, v_fox_q_norm, v_fox_k_norm, v_fox_f_bias, v_gdn_conv, v_gdn_a_log, v_gdn_dt_bias, v_gdn_out_norm, v_w_out, v_ffn2_norm, v_ffn2_w_in, v_ffn2_w_out):
    wts = dict(ffn1_norm=ffn1_norm, ffn1_w_in=ffn1_w_in, ffn1_w_out=ffn1_w_out, mix_norm=mix_norm, w_in=w_in,
               fox_q_norm=fox_q_norm, fox_k_norm=fox_k_norm, fox_f_bias=fox_f_bias, gdn_conv=gdn_conv,
               gdn_a_log=gdn_a_log, gdn_dt_bias=gdn_dt_bias, gdn_out_norm=gdn_out_norm, w_out=w_out,
               ffn2_norm=ffn2_norm, ffn2_w_in=ffn2_w_in, ffn2_w_out=ffn2_w_out)
    mom = dict(ffn1_norm=m_ffn1_norm, ffn1_w_in=m_ffn1_w_in, ffn1_w_out=m_ffn1_w_out, mix_norm=m_mix_norm, w_in=m_w_in,
               fox_q_norm=m_fox_q_norm, fox_k_norm=m_fox_k_norm, fox_f_bias=m_fox_f_bias, gdn_conv=m_gdn_conv,
               gdn_a_log=m_gdn_a_log, gdn_dt_bias=m_gdn_dt_bias, gdn_out_norm=m_gdn_out_norm, w_out=m_w_out,
               ffn2_norm=m_ffn2_norm, ffn2_w_in=m_ffn2_w_in, ffn2_w_out=m_ffn2_w_out)
    var = dict(ffn1_norm=v_ffn1_norm, ffn1_w_in=v_ffn1_w_in, ffn1_w_out=v_ffn1_w_out, mix_norm=v_mix_norm, w_in=v_w_in,
               fox_q_norm=v_fox_q_norm, fox_k_norm=v_fox_k_norm, fox_f_bias=v_fox_f_bias, gdn_conv=v_gdn_conv,
               gdn_a_log=v_gdn_a_log, gdn_dt_bias=v_gdn_dt_bias, gdn_out_norm=v_gdn_out_norm, w_out=v_w_out,
               ffn2_norm=v_ffn2_norm, ffn2_w_in=v_ffn2_w_in, ffn2_w_out=v_ffn2_w_out)
    nb, seq, d = x.shape
    t = nb * seq
    depth = ffn1_norm.shape[0]

    stages = [(l, gi) for l in range(depth) for gi in range(len(GROUPS))]

    def shards_of(l, gi):
        return [wts[nm][l] if nm == "gdn_conv" else wts[nm][l].astype(BF16) for nm in GROUPS[gi]]

    def behind(nw, token):
        return nw if token is None else nw + token[0:1, 0:1]

    def small_params(l):
        return dict(
            n1=ffn1_norm[l][None], nmix=mix_norm[l][None], n2=ffn2_norm[l][None],
            qw=fox_q_norm[l][None], kw=fox_k_norm[l][None], onw=gdn_out_norm[l][None],
            gp=jnp.concatenate([
                jnp.concatenate([fox_f_bias[l], gdn_dt_bias[l], jnp.zeros((116,), F32)])[None],
                jnp.concatenate([jnp.zeros((8,), F32), gdn_a_log[l], jnp.zeros((116,), F32)])[None],
                jnp.zeros((6, 128), F32)], axis=0))

    h = x.reshape(t, d)
    state, token = exchange_begin(shards_of(0, 0), [False] * len(GROUPS[0]), "gather_0", ffn1_norm)
    landed = exchange_end(state, token)
    saved = [dict(p=small_params(l)) for l in range(depth)]
    for k, (l, gi) in enumerate(stages):
        s, w, token = saved[l], landed, None
        p = s["p"]
        if k + 1 < len(stages):
            nl, ng = stages[k + 1]
            state, token = exchange_begin(shards_of(nl, ng), [False] * len(GROUPS[ng]), f"gather_{k + 1}", landed[0])
        if gi == 0:
            fb = w[0].shape[2]
            p["w1i"], p["w1o"] = w[0].reshape(2, 4, d, fb), w[1].reshape(4, fb, d)
            s["x0"] = h
            h, *s["ffn1"] = ffn_fwd(h, behind(p["n1"], token), p["w1i"], p["w1o"])
            s["x1"] = h
        elif gi == 1:
            p["wi"] = _in_cols_to_mine(w[0].transpose(1, 0, 2).reshape(d, -1))
            p["cw"] = w[1].transpose(1, 0, 2).reshape(CONV_W, -1)
            p["wo"] = w[2].reshape(d, d)
            proj, hn = inproj_fwd(h, behind(p["nmix"], token), p["wi"])
            gates = gates_fwd(proj, p["gp"], seq)
            yf, lse = attn_fwd(proj, gates, p["qw"], p["kw"], seq, tq=min(seq, ATTN_TQ_FWD))
            qh, kh, vh = gdn_pre_fwd(proj, p["cw"], seq)
            inv = gdn_inv(kh, gates, seq)
            yg, st = gdn_fwd(qh, kh, vh, gates, proj, p["onw"], inv, seq)
            h, ycat = outproj_fwd(h, yf, yg, p["wo"])
            s.update(x2=h, proj=proj, hn=hn, gates=gates, yf=yf, lse=lse, qh=qh, kh=kh, vh=vh, st=st, inv=inv, ycat=ycat)
        else:
            fb = w[0].shape[2]
            p["w2i"], p["w2o"] = w[0].reshape(2, 4, d, fb), w[1].reshape(4, fb, d)
            h, *s["ffn2"] = ffn_fwd(h, behind(p["n2"], token), p["w2i"], p["w2o"])
        if k + 1 < len(stages):
            landed = exchange_end(state, h)

    sq, dh = loss_head(h, loss_target.reshape(t, d))
    loss = lax.psum(0.5 * jnp.sum(sq) / d, ("x", "y", "c"))

    got = [None] * len(stages)
    pending, token = None, None
    gsmall = {nm: [None] * depth for nm in SMALL}
    for k in reversed(range(len(stages))):
        l, gi = stages[k]
        s = saved[l]
        p = s["p"]
        if gi != 1:
            nw, xin, wi_, wo_, nm_n, (xn, gu, hh) = (
                (p["n1"], s["x0"], p["w1i"], p["w1o"], "ffn1_norm", s["ffn1"]) if gi == 0 else
                (p["n2"], s["x2"], p["w2i"], p["w2o"], "ffn2_norm", s["ffn2"]))
            dh, dn, dgu, dyh = ffn_bwd(xin, dh, behind(nw, token), gu, wi_, wo_)
            g_in, g_out = wgrad_ffn_in(xn, dgu), wgrad_ffn_out(hh, dyh)
            send = [g_in.reshape(N_DEV, d, g_in.shape[3]), g_out.reshape(N_DEV, -1, d)]
            gsmall[nm_n][l] = dn[0]
        else:
            dyf, dyg, dyb = outproj_bwd(dh, p["wo"], token)
            g_wo = wgrad_2d(s["ycat"], dyb, 512, "wgrad_w_out")
            dq, dk, dv, dga, dqw, dkw = attn_bwd(s["proj"], s["gates"], p["qw"], p["kw"], s["yf"], s["lse"], dyf, seq,
                                                 tq=min(seq, ATTN_TQ_BWD))
            dqh, dkh, dvh, dgg, dgb, donw = gdn_bwd(s["qh"], s["kh"], s["vh"], s["gates"], s["proj"], p["onw"],
                                                     s["inv"], s["st"], dyg, seq)
            dxq, dxk, dxv, dwq, dwk, dwv = gdn_pre_bwd(s["proj"], p["cw"], dqh, dkh, dvh, seq)
            dsm, dgp = gates_bwd(s["proj"], p["gp"], dga, dgb, seq)
            dh, dnmix, dproj = inproj_bwd(s["x1"], dh, p["nmix"], p["wi"], [dq, dk, dv, dxq, dxk, dxv, dgg, dsm])
            g_wi = wgrad_2d(s["hn"], dproj, 512, "wgrad_w_in", F32)
            g_cw = jnp.concatenate([dwq, dwk, dwv], axis=1)
            send = [_in_cols_from_mine(g_wi).reshape(d, N_DEV, -1).transpose(1, 0, 2),
                    g_cw.reshape(CONV_W, N_DEV, -1).transpose(1, 0, 2), g_wo.reshape(N_DEV, -1, d)]
            for nm, val in (("mix_norm", dnmix[0]), ("fox_q_norm", dqw[0]), ("fox_k_norm", dkw[0]),
                            ("fox_f_bias", dgp[0, 0:8]), ("gdn_a_log", dgp[1, 8:12]), ("gdn_dt_bias", dgp[0, 8:12]),
                            ("gdn_out_norm", donw[0])):
                gsmall[nm][l] = val
        flags = [True] * len(send)
        if k == 0:
            send.append(_pack_small({nm: jnp.stack(v) for nm, v in gsmall.items()}))
            flags.append(False)
        prev = dh
        if pending is not None:
            got[pending[1]] = exchange_end(pending[0], dh)
            prev = got[pending[1]][0]
        state, token = exchange_begin(send, flags, f"exchange_grads_{k}", prev)
        pending = (state, k)
    grad_x = dh.reshape(nb, seq, d)

    def update_stage(k, slots, after):
        l, gi = stages[k]
        out = {}
        for i, nm in enumerate(GROUPS[gi]):
            r, c = wts[nm].shape[1:]
            out[nm] = adamw_reduce(slots[i].reshape(N_DEV, r, c), wts[nm][l], mom[nm][l], var[nm][l], f"adamw_{nm}_{l}",
                                   after)
            if after is not None:
                after = out[nm][0]
        return out, after

    per_layer = [dict() for _ in range(depth)]
    last = token
    for k in range(1, len(stages)):
        out, last = update_stage(k, got[k], last)
        per_layer[stages[k][0]].update(out)
    got[0] = exchange_end(pending[0], last)
    per_layer[0].update(update_stage(0, got[0], None)[0])
    res = {nm: [jnp.stack([per_layer[l][nm][j] for l in range(depth)]) for j in range(4)] for nm in BIG}
    small_like = {nm: wts[nm] for nm in SMALL}
    sm = adamw_reduce(got[0][-1], _pack_small(small_like), _pack_small({nm: mom[nm] for nm in SMALL}),
                      _pack_small({nm: var[nm] for nm in SMALL}), "adamw_small")
    sm = [_unpack_small(a, small_like) for a in sm]
    for nm in SMALL:
        res[nm] = [sm[j][nm] for j in range(4)]
    return (loss, grad_x, *[res[nm][0] for nm in WEIGHTS], *[res[nm][1] for nm in WEIGHTS],
            *[res[nm][2] for nm in WEIGHTS], *[res[nm][3] for nm in WEIGHTS])
```

```python
import functools

import jax
import jax.numpy as jnp
from jax import lax
from jax.experimental import pallas as pl
from jax.experimental.pallas import tpu as pltpu

F32 = jnp.float32
BF16 = jnp.bfloat16
EPS = 1e-6
N_DEV = 8
MESH = pl.DeviceIdType.MESH
HIGHEST = lax.Precision.HIGHEST
VMEM_LIMIT = 56 * 1024 * 1024

FOX_HEADS, FOX_DH = 8, 64
GDN_HEADS, GDN_DH = 4, 128
CHUNK = 64
CONV_W = 4

ADAM_LR, ADAM_B1, ADAM_B2, ADAM_EPS, ADAM_WD, ADAM_STEP = 0.001, 0.9, 0.999, 1e-08, 0.01, 10


def _cp(*sem):
    return pltpu.CompilerParams(dimension_semantics=sem, vmem_limit_bytes=VMEM_LIMIT)


def _dot(a, b):
    return jnp.dot(a, b, preferred_element_type=F32)


def _dot_nt(a, b):
    return lax.dot_general(a, b, (((1,), (1,)), ((), ())), preferred_element_type=F32)


def _dot_tn(a, b):
    return lax.dot_general(a, b, (((0,), (0,)), ((), ())), preferred_element_type=F32)


def _rstd(xf):
    return lax.rsqrt(jnp.mean(xf * xf, axis=-1, keepdims=True) + EPS)


def _rms_bwd(xf, r, dyn):
    return r * dyn - xf * (r * r * r) * jnp.mean(dyn * xf, axis=-1, keepdims=True)


def ffn_fwd(x, nw, w_in, w_out, tm=512):
    t, d = x.shape
    nj, fb = w_out.shape[0], w_out.shape[1]

    def body(x_ref, nw_ref, wi_ref, wo_ref, o_ref, xn_ref, gu_ref, h_ref, acc_ref):
        j = pl.program_id(1)

        @pl.when(j == 0)
        def _():
            xf = x_ref[...]
            xn_ref[...] = (xf * _rstd(xf) * nw_ref[...]).astype(BF16)
            acc_ref[...] = jnp.zeros_like(acc_ref)

        xn = xn_ref[...]
        g = _dot(xn, wi_ref[0])
        u = _dot(xn, wi_ref[1])
        h = (g * jax.nn.sigmoid(g) * u).astype(BF16)
        gu_ref[0] = g.astype(BF16)
        gu_ref[1] = u.astype(BF16)
        h_ref[...] = h
        acc_ref[...] += _dot(h, wo_ref[...])

        @pl.when(j == nj - 1)
        def _():
            o_ref[...] = x_ref[...] + 0.5 * acc_ref[...]

    return pl.pallas_call(
        body, grid=(t // tm, nj),
        in_specs=[pl.BlockSpec((tm, d), lambda i, j: (i, 0)),
                  pl.BlockSpec((1, d), lambda i, j: (0, 0)),
                  pl.BlockSpec((2, None, d, fb), lambda i, j: (0, j, 0, 0)),
                  pl.BlockSpec((None, fb, d), lambda i, j: (j, 0, 0))],
        out_specs=[pl.BlockSpec((tm, d), lambda i, j: (i, 0)),
                   pl.BlockSpec((tm, d), lambda i, j: (i, 0)),
                   pl.BlockSpec((2, None, tm, fb), lambda i, j: (0, j, i, 0)),
                   pl.BlockSpec((None, tm, fb), lambda i, j: (j, i, 0))],
        out_shape=[jax.ShapeDtypeStruct((t, d), F32), jax.ShapeDtypeStruct((t, d), BF16),
                   jax.ShapeDtypeStruct((2, nj, t, fb), BF16), jax.ShapeDtypeStruct((nj, t, fb), BF16)],
        scratch_shapes=[pltpu.VMEM((tm, d), F32)],
        compiler_params=_cp("parallel", "arbitrary"), name="ffn_fwd")(x, nw, w_in, w_out)


def ffn_bwd(x, dy, nw, gu, w_in, w_out, tm=512, rc=256):
    t, d = x.shape
    nj, fb = w_out.shape[0], w_out.shape[1]
    rc = min(rc, tm)

    def body(x_ref, dy_ref, nw_ref, gu_ref, wi_ref, wo_ref,
             dx_ref, dnw_ref, dgu_ref, dyh_ref, acc_ref):
        i, j = pl.program_id(0), pl.program_id(1)

        @pl.when(j == 0)
        def _():
            dyh_ref[...] = (0.5 * dy_ref[...]).astype(BF16)
            acc_ref[...] = jnp.zeros_like(acc_ref)

        @pl.when((i == 0) & (j == 0))
        def _():
            dnw_ref[...] = jnp.zeros_like(dnw_ref)

        for c in range(tm // rc):
            r = slice(c * rc, (c + 1) * rc)
            g = gu_ref[0, r, :].astype(F32)
            u = gu_ref[1, r, :].astype(F32)
            sg = jax.nn.sigmoid(g)
            dh = _dot_nt(dyh_ref[r, :], wo_ref[...])
            dg = (dh * u * (sg * (1.0 + g * (1.0 - sg)))).astype(BF16)
            du = (dh * (g * sg)).astype(BF16)
            dgu_ref[0, r, :] = dg
            dgu_ref[1, r, :] = du
            acc_ref[r, :] += _dot_nt(dg, wi_ref[0]) + _dot_nt(du, wi_ref[1])

        @pl.when(j == nj - 1)
        def _():
            xf = x_ref[...]
            r = _rstd(xf)
            dxn = acc_ref[...]
            dnw_ref[...] += jnp.sum(dxn * xf * r, axis=0, keepdims=True)
            dx_ref[...] = _rms_bwd(xf, r, dxn * nw_ref[...]) + dy_ref[...]

    return pl.pallas_call(
        body, grid=(t // tm, nj),
        in_specs=[pl.BlockSpec((tm, d), lambda i, j: (i, 0)),
                  pl.BlockSpec((tm, d), lambda i, j: (i, 0)),
                  pl.BlockSpec((1, d), lambda i, j: (0, 0)),
                  pl.BlockSpec((2, None, tm, fb), lambda i, j: (0, j, i, 0)),
                  pl.BlockSpec((2, None, d, fb), lambda i, j: (0, j, 0, 0)),
                  pl.BlockSpec((None, fb, d), lambda i, j: (j, 0, 0))],
        out_specs=[pl.BlockSpec((tm, d), lambda i, j: (i, 0)),
                   pl.BlockSpec((1, d), lambda i, j: (0, 0)),
                   pl.BlockSpec((2, None, tm, fb), lambda i, j: (0, j, i, 0)),
                   pl.BlockSpec((tm, d), lambda i, j: (i, 0))],
        out_shape=[jax.ShapeDtypeStruct((t, d), F32),
                   jax.ShapeDtypeStruct((1, d), F32),
                   jax.ShapeDtypeStruct((2, nj, t, fb), BF16),
                   jax.ShapeDtypeStruct((t, d), BF16)],
        scratch_shapes=[pltpu.VMEM((tm, d), F32)],
        compiler_params=_cp("arbitrary", "arbitrary"), name="ffn_bwd")(x, dy, nw, gu, w_in, w_out)


def _wgrad_call(a, b, a_spec, b_spec, out_shape, out_spec, grid, name, out_dtype=BF16):
    last = len(grid) - 1
    acc_shape = tuple(s for s in out_spec.block_shape if s is not None)

    def body(a_ref, b_ref, o_ref, acc_ref):
        @pl.when(pl.program_id(last) == 0)
        def _():
            acc_ref[...] = jnp.zeros_like(acc_ref)

        acc_ref[...] += _dot_tn(a_ref[...], b_ref[...])

        @pl.when(pl.program_id(last) == grid[last] - 1)
        def _():
            o_ref[...] = acc_ref[...].astype(o_ref.dtype)

    sem = ("parallel",) * last + ("arbitrary",)
    return pl.pallas_call(body, grid=grid, in_specs=[a_spec, b_spec], out_specs=out_spec,
                          out_shape=jax.ShapeDtypeStruct(out_shape, out_dtype),
                          scratch_shapes=[pltpu.VMEM(acc_shape, F32)],
                          compiler_params=_cp(*sem), name=name)(a, b)


WGRAD_TM = 1024


def wgrad_ffn_in(xn, dgu, tm=WGRAD_TM):
    t, d = xn.shape
    _, nj, _, fb = dgu.shape
    tm = min(tm, t)
    return _wgrad_call(xn, dgu,
                       pl.BlockSpec((tm, d), lambda p, j, k: (k, 0)),
                       pl.BlockSpec((None, None, tm, fb), lambda p, j, k: (p, j, k, 0)),
                       (2, nj, d, fb), pl.BlockSpec((None, None, d, fb), lambda p, j, k: (p, j, 0, 0)),
                       (2, nj, t // tm), "wgrad_ffn_in")


def wgrad_ffn_out(h, dyh, tm=WGRAD_TM):
    nj, t, fb = h.shape
    d = dyh.shape[1]
    tm = min(tm, t)
    return _wgrad_call(h, dyh,
                       pl.BlockSpec((None, tm, fb), lambda j, k: (j, k, 0)),
                       pl.BlockSpec((tm, d), lambda j, k: (k, 0)),
                       (nj, fb, d), pl.BlockSpec((None, fb, d), lambda j, k: (j, 0, 0)),
                       (nj, t // tm), "wgrad_ffn_out")


def wgrad_2d(a, b, tk, name, out_dtype=BF16, tm=512):
    t, k = a.shape
    n = b.shape[1]
    return _wgrad_call(a, b,
                       pl.BlockSpec((tm, tk), lambda c, s: (s, c)),
                       pl.BlockSpec((tm, n), lambda c, s: (s, 0)),
                       (k, n), pl.BlockSpec((tk, n), lambda c, s: (c, 0)),
                       (k // tk, t // tm), name, out_dtype)


N_BIG = 7 * 512
N_PROJ = N_BIG + 128
COL_SMALL = N_BIG // 128


def inproj_fwd(x, nw, w, tm=256):
    t, d = x.shape
    n = w.shape[1]

    def body(x_ref, nw_ref, w_ref, p_ref, hn_ref):
        xf = x_ref[...]
        hn = (xf * _rstd(xf) * nw_ref[...]).astype(BF16)
        hn_ref[...] = hn
        p_ref[...] = _dot(hn, w_ref[...])

    return pl.pallas_call(
        body, grid=(t // tm,),
        in_specs=[pl.BlockSpec((tm, d), lambda i: (i, 0)), pl.BlockSpec((1, d), lambda i: (0, 0)),
                  pl.BlockSpec((d, n), lambda i: (0, 0))],
        out_specs=[pl.BlockSpec((tm, n), lambda i: (i, 0)), pl.BlockSpec((tm, d), lambda i: (i, 0))],
        out_shape=[jax.ShapeDtypeStruct((t, n), F32), jax.ShapeDtypeStruct((t, d), BF16)],
        compiler_params=_cp("parallel"), name="inproj_fwd")(x, nw, w)


def inproj_bwd(x, dres, nw, w, dparts, tm=256):
    t, d = x.shape
    n = w.shape[1]
    widths = [p.shape[1] for p in dparts]
    assert sum(widths) == n

    def body(x_ref, dres_ref, nw_ref, w_ref, *rest):
        part_refs, (dx_ref, dnw_ref, dp_ref) = rest[:len(widths)], rest[len(widths):]

        @pl.when(pl.program_id(0) == 0)
        def _():
            dnw_ref[...] = jnp.zeros_like(dnw_ref)

        dp = jnp.concatenate([r[...].astype(BF16) for r in part_refs], axis=1)
        dp_ref[...] = dp
        dhn = _dot_nt(dp, w_ref[...])
        xf = x_ref[...]
        r = _rstd(xf)
        dnw_ref[...] += jnp.sum(dhn * xf * r, axis=0, keepdims=True)
        dx_ref[...] = _rms_bwd(xf, r, dhn * nw_ref[...]) + dres_ref[...]

    return pl.pallas_call(
        body, grid=(t // tm,),
        in_specs=[pl.BlockSpec((tm, d), lambda i: (i, 0)), pl.BlockSpec((tm, d), lambda i: (i, 0)),
                  pl.BlockSpec((1, d), lambda i: (0, 0)), pl.BlockSpec((d, n), lambda i: (0, 0))]
                 + [pl.BlockSpec((tm, wd), lambda i: (i, 0)) for wd in widths],
        out_specs=[pl.BlockSpec((tm, d), lambda i: (i, 0)), pl.BlockSpec((1, d), lambda i: (0, 0)),
                   pl.BlockSpec((tm, n), lambda i: (i, 0))],
        out_shape=[jax.ShapeDtypeStruct((t, d), F32), jax.ShapeDtypeStruct((1, d), F32),
                   jax.ShapeDtypeStruct((t, n), BF16)],
        compiler_params=_cp("arbitrary"), name="inproj_bwd")(x, dres, nw, w, *dparts)


def outproj_fwd(x, yf, yg, w, tm=512):
    t, d = x.shape
    hw = yf.shape[1]

    def body(x_ref, yf_ref, yg_ref, w_ref, o_ref, y_ref):
        y = jnp.concatenate([yf_ref[...], yg_ref[...]], axis=1).astype(BF16)
        y_ref[...] = y
        o_ref[...] = x_ref[...] + _dot(y, w_ref[...])

    return pl.pallas_call(
        body, grid=(t // tm,),
        in_specs=[pl.BlockSpec((tm, d), lambda i: (i, 0)), pl.BlockSpec((tm, hw), lambda i: (i, 0)),
                  pl.BlockSpec((tm, hw), lambda i: (i, 0)), pl.BlockSpec((2 * hw, d), lambda i: (0, 0))],
        out_specs=[pl.BlockSpec((tm, d), lambda i: (i, 0)), pl.BlockSpec((tm, 2 * hw), lambda i: (i, 0))],
        out_shape=[jax.ShapeDtypeStruct((t, d), F32), jax.ShapeDtypeStruct((t, 2 * hw), BF16)],
        compiler_params=_cp("parallel"), name="outproj_fwd")(x, yf, yg, w)


def outproj_bwd(dy, w, after=None, tm=512):
    t, d = dy.shape
    hw = w.shape[0] // 2
    extra = [] if after is None else [after]

    def body(dy_ref, w_ref, *rest):
        df_ref, dg_ref, dyb_ref = rest[-3:]
        dyb = dy_ref[...].astype(BF16)
        dyb_ref[...] = dyb
        dyy = _dot_nt(dyb, w_ref[...])
        df_ref[...] = dyy[:, :hw]
        dg_ref[...] = dyy[:, hw:]

    return pl.pallas_call(
        body, grid=(t // tm,),
        in_specs=[pl.BlockSpec((tm, d), lambda i: (i, 0)), pl.BlockSpec((2 * hw, d), lambda i: (0, 0))]
                 + [pl.BlockSpec(memory_space=pl.ANY)] * len(extra),
        out_specs=[pl.BlockSpec((tm, hw), lambda i: (i, 0)), pl.BlockSpec((tm, hw), lambda i: (i, 0)),
                   pl.BlockSpec((tm, d), lambda i: (i, 0))],
        out_shape=[jax.ShapeDtypeStruct((t, hw), F32), jax.ShapeDtypeStruct((t, hw), F32),
                   jax.ShapeDtypeStruct((t, d), BF16)],
        compiler_params=_cp("parallel"), name="outproj_bwd")(dy, w, *extra)


def _lane(shape):
    return lax.broadcasted_iota(jnp.int32, shape, 1)


def _row(shape):
    return lax.broadcasted_iota(jnp.int32, shape, 0)


def _gate_terms(val, gp_ref):
    z = val + gp_ref[0:1, :]
    sp = jnp.log(1.0 + jnp.exp(-jnp.abs(z)))
    return z, sp


def gates_fwd(proj, gp, seq, ts=512):
    t = proj.shape[0]
    nb, ns = t // seq, seq // ts

    def body(v_ref, gp_ref, o_ref, carry_ref):
        @pl.when(pl.program_id(1) == 0)
        def _():
            carry_ref[...] = jnp.zeros_like(carry_ref)

        z, sp = _gate_terms(v_ref[...], gp_ref)
        logsig = jnp.minimum(z, 0.0) - sp
        tri = (_row((ts, ts)) >= _lane((ts, ts))).astype(F32)
        cum = jnp.dot(tri, logsig, precision=HIGHEST, preferred_element_type=F32) + carry_ref[0:1, :]
        carry_ref[0:1, :] = cum[ts - 1:ts, :]
        g = -jnp.exp(gp_ref[1:2, :]) * (jnp.maximum(z, 0.0) + sp)
        beta = jax.nn.sigmoid(z)
        lane = _lane((ts, 128))
        o_ref[...] = jnp.where(lane < 8, cum, jnp.where(lane < 12, g, jnp.where(lane < 16, beta, 0.0)))

    return pl.pallas_call(
        body, grid=(nb, ns),
        in_specs=[pl.BlockSpec((ts, 128), lambda b, s: (b * ns + s, COL_SMALL)),
                  pl.BlockSpec((8, 128), lambda b, s: (0, 0))],
        out_specs=pl.BlockSpec((ts, 128), lambda b, s: (b * ns + s, 0)),
        out_shape=jax.ShapeDtypeStruct((t, 128), F32),
        scratch_shapes=[pltpu.VMEM((8, 128), F32)],
        compiler_params=_cp("parallel", "arbitrary"), name="gates_fwd")(proj, gp)


def gates_bwd(proj, gp, dga, dgb, seq, ts=512):
    t = proj.shape[0]
    nb, ns = t // seq, seq // ts

    def body(v_ref, gp_ref, da_ref, db_ref, ds_ref, dgp_ref, carry_ref):
        @pl.when(pl.program_id(1) == 0)
        def _():
            carry_ref[...] = jnp.zeros_like(carry_ref)

        @pl.when((pl.program_id(0) == 0) & (pl.program_id(1) == 0))
        def _():
            dgp_ref[...] = jnp.zeros_like(dgp_ref)

        lane = _lane((ts, 128))
        dgate = jnp.where(lane < 8, da_ref[...], jnp.where(lane < 16, db_ref[...], 0.0))
        z, sp = _gate_terms(v_ref[...], gp_ref)
        triu = (_row((ts, ts)) <= _lane((ts, ts))).astype(F32)
        dlog = jnp.dot(triu, dgate, precision=HIGHEST, preferred_element_type=F32) + carry_ref[0:1, :]
        carry_ref[0:1, :] = dlog[0:1, :]
        sig = jax.nn.sigmoid(z)
        nea = -jnp.exp(gp_ref[1:2, :])
        g = nea * (jnp.maximum(z, 0.0) + sp)
        dz = jnp.where(lane < 8, dlog * (1.0 - sig),
                       jnp.where(lane < 12, dgate * nea * sig, dgate * sig * (1.0 - sig)))
        dz = jnp.where(lane < 16, dz, 0.0)
        ds_ref[...] = dz
        dgp_ref[0:1, :] += jnp.where(lane[0:1] < 12, jnp.sum(dz, axis=0, keepdims=True), 0.0)
        dgp_ref[1:2, :] += jnp.where((lane[0:1] >= 8) & (lane[0:1] < 12), jnp.sum(dgate * g, axis=0, keepdims=True), 0.0)

    rev = lambda b, s: (b * ns + (ns - 1 - s), 0)
    return pl.pallas_call(
        body, grid=(nb, ns),
        in_specs=[pl.BlockSpec((ts, 128), lambda b, s: (b * ns + (ns - 1 - s), COL_SMALL)),
                  pl.BlockSpec((8, 128), lambda b, s: (0, 0)),
                  pl.BlockSpec((ts, 128), rev), pl.BlockSpec((ts, 128), rev)],
        out_specs=[pl.BlockSpec((ts, 128), rev), pl.BlockSpec((8, 128), lambda b, s: (0, 0))],
        out_shape=[jax.ShapeDtypeStruct((t, 128), F32), jax.ShapeDtypeStruct((8, 128), F32)],
        scratch_shapes=[pltpu.VMEM((8, 128), F32)],
        compiler_params=_cp("arbitrary", "arbitrary"), name="gates_bwd")(proj, gp, dga, dgb)


NEG = -1e30
ATTN_TQ_FWD = 1024
ATTN_TQ_BWD = 512


def _pick_lane(tile, idx):
    return jnp.sum(jnp.where(_lane(tile.shape) == idx, tile, 0.0), axis=1, keepdims=True)


def _col_to_row(col, n):
    return jnp.sum(jnp.where(_row((n, n)) == _lane((n, n)), col, 0.0), axis=0, keepdims=True)


def _row_to_col(row, n):
    return jnp.sum(jnp.where(_row((n, n)) == _lane((n, n)), row, 0.0), axis=1, keepdims=True)


def _rows(i, n):
    return pl.ds(pl.multiple_of(i * n, n), n)


def _once(shape, index_map):
    return pl.BlockSpec(shape, index_map, pipeline_mode=pl.Buffered(1))


def attn_fwd(proj, gates, qw, kw, seq, tq=256):
    t = proj.shape[0]
    nb, nq, dh = t // seq, seq // tq, FOX_DH
    scale = dh ** -0.5

    def body(q_ref, k_ref, v_ref, g_ref, qw_ref, kw_ref, y_ref, lse_ref, qs, ks, vs, ccol, crow):
        p = pl.program_id(1)
        heads = range(2)

        def prep(i, _):
            r = _rows(i, tq)
            for hh in heads:
                lanes = slice(hh * dh, (hh + 1) * dh)
                qf, kf = q_ref[r, lanes], k_ref[r, lanes]
                qs[hh, r, :] = (qf * _rstd(qf) * qw_ref[...] * scale).astype(BF16)
                ks[hh, r, :] = (kf * _rstd(kf) * kw_ref[...]).astype(BF16)
                vs[hh, r, :] = v_ref[r, lanes].astype(BF16)
                cc = _pick_lane(g_ref[r, :], 2 * p + hh)
                ccol[hh, r, :] = cc
                crow[hh * nq + i] = _col_to_row(cc, tq)
            return 0

        lax.fori_loop(0, nq, prep, 0)

        def q_tile(i, _):
            r = _rows(i, tq)
            qt = [qs[hh, r, :] for hh in heads]
            cc = [ccol[hh, r, :] for hh in heads]

            def kv_step(j, carry, masked):
                kr = _rows(j, tq)
                out = []
                for hh in heads:
                    m, l, acc = carry[hh]
                    s = _dot_nt(qt[hh], ks[hh, kr, :]) + (cc[hh] - crow[hh * nq + j])
                    if masked:
                        s = jnp.where(_row((tq, tq)) >= _lane((tq, tq)), s, NEG)
                    m_new = jnp.maximum(m, jnp.max(s, axis=1, keepdims=True))
                    pe = jnp.exp(s - m_new)
                    a = jnp.exp(m - m_new)
                    out.append((m_new, a * l + jnp.sum(pe, axis=1, keepdims=True),
                                a * acc + _dot(pe.astype(BF16), vs[hh, kr, :])))
                return tuple(out)

            one = (jnp.full((tq, 1), NEG, F32), jnp.zeros((tq, 1), F32), jnp.zeros((tq, dh), F32))
            carry = lax.fori_loop(0, i, lambda j, c: kv_step(j, c, False), (one, one))
            carry = kv_step(i, carry, True)
            for hh in heads:
                m, l, acc = carry[hh]
                lanes = slice(hh * dh, (hh + 1) * dh)
                y_ref[r, lanes] = acc / l
                lse_ref[r, lanes] = jnp.broadcast_to(m + jnp.log(l), (tq, dh))
            return 0

        lax.fori_loop(0, nq, q_tile, 0)

    blk = lambda off: _once((seq, 128), lambda b, p: (b, off + p))
    return pl.pallas_call(
        body, grid=(nb, 4),
        in_specs=[blk(0), blk(4), blk(8), _once((seq, 128), lambda b, p: (b, 0)),
                  pl.BlockSpec((1, dh), lambda b, p: (0, 0)), pl.BlockSpec((1, dh), lambda b, p: (0, 0))],
        out_specs=[pl.BlockSpec((seq, 128), lambda b, p: (b, p)), pl.BlockSpec((seq, 128), lambda b, p: (b, p))],
        out_shape=[jax.ShapeDtypeStruct((t, 512), F32), jax.ShapeDtypeStruct((t, 512), F32)],
        scratch_shapes=[pltpu.VMEM((2, seq, dh), BF16), pltpu.VMEM((2, seq, dh), BF16), pltpu.VMEM((2, seq, dh), BF16),
                        pltpu.VMEM((2, seq, 1), F32), pltpu.VMEM((2 * nq, 1, tq), F32)],
        compiler_params=_cp("parallel", "arbitrary"), name="attn_fwd")(proj, proj, proj, gates, qw, kw)


def attn_bwd(proj, gates, qw, kw, y, lse, dy, seq, tq=256):
    t = proj.shape[0]
    nb, nq, dh = t // seq, seq // tq, FOX_DH
    scale = dh ** -0.5

    def body(q_ref, k_ref, v_ref, g_ref, qw_ref, kw_ref, y_ref, lse_ref, dy_ref,
             dq_ref, dk_ref, dv_ref, dg_ref, dqw_ref, dkw_ref,
             qs, ks, vs, dos, cols, crow, dqa, dka):
        b, p = pl.program_id(0), pl.program_id(1)

        @pl.when((b == 0) & (p == 0))
        def _():
            dqw_ref[...] = jnp.zeros_like(dqw_ref)
            dkw_ref[...] = jnp.zeros_like(dkw_ref)

        @pl.when(p == 0)
        def _():
            dg_ref[...] = jnp.zeros_like(dg_ref)

        heads = range(2)
        hl = lambda hh: slice(hh * dh, (hh + 1) * dh)

        def prep(i, _):
            r = _rows(i, tq)
            for hh in heads:
                lanes = hl(hh)
                qf, kf = q_ref[r, lanes], k_ref[r, lanes]
                qs[hh, r, :] = (qf * _rstd(qf) * qw_ref[...] * scale).astype(BF16)
                ks[hh, r, :] = (kf * _rstd(kf) * kw_ref[...]).astype(BF16)
                vs[hh, r, :] = v_ref[r, lanes].astype(BF16)
                dyf = dy_ref[r, lanes]
                dos[hh, r, :] = dyf.astype(BF16)
                cc = _pick_lane(g_ref[r, :], 2 * p + hh)
                crow[hh * nq + i] = _col_to_row(cc, tq)
                delta = jnp.sum(dyf * y_ref[r, lanes], axis=1, keepdims=True)
                lane = _lane((tq, 128))
                cols[hh, r, :] = jnp.where(lane == 0, cc, jnp.where(lane == 1, lse_ref[r, hh * dh:hh * dh + 1],
                                                                     jnp.where(lane == 2, delta, 0.0)))
                dqa[hh, r, :] = jnp.zeros((tq, dh), F32)
            return 0

        lax.fori_loop(0, nq, prep, 0)

        def kv_tile(j, _):
            kr = _rows(j, tq)
            kt = [ks[hh, kr, :] for hh in heads]
            vt = [vs[hh, kr, :] for hh in heads]
            cr = [crow[hh * nq + j] for hh in heads]

            def q_step(i, carry, masked):
                r = _rows(i, tq)
                out = []
                for hh in heads:
                    dk, dv, dcr = carry[hh]
                    qt, dot, cl = qs[hh, r, :], dos[hh, r, :], cols[hh, r, :]
                    s = _dot_nt(qt, kt[hh]) + (cl[:, 0:1] - cr[hh])
                    if masked:
                        s = jnp.where(_row((tq, tq)) >= _lane((tq, tq)), s, NEG)
                    pe = jnp.exp(s - cl[:, 1:2])
                    ds = pe * (_dot_nt(dot, vt[hh]) - cl[:, 2:3])
                    dsb = ds.astype(BF16)
                    dqa[hh, r, :] += _dot(dsb, kt[hh])
                    cols[hh, r, :] = cl + jnp.where(_lane((tq, 128)) == 3, jnp.sum(ds, axis=1, keepdims=True), 0.0)
                    out.append((dk + _dot_tn(dsb, qt), dv + _dot_tn(pe.astype(BF16), dot),
                                dcr - jnp.sum(ds, axis=0, keepdims=True)))
                return tuple(out)

            one = (jnp.zeros((tq, dh), F32), jnp.zeros((tq, dh), F32), jnp.zeros((1, tq), F32))
            carry = q_step(j, (one, one), True)
            carry = lax.fori_loop(j + 1, nq, lambda i, c: q_step(i, c, False), carry)
            for hh in heads:
                dk, dv, dcr = carry[hh]
                dka[hh, kr, :] = dk
                dv_ref[kr, hl(hh)] = dv
                dg_ref[kr, :] = jnp.where(_lane((tq, 128)) == 2 * p + hh, _row_to_col(dcr, tq), dg_ref[kr, :])
            return 0

        lax.fori_loop(0, nq, kv_tile, 0)

        def post(i, _):
            r = _rows(i, tq)
            for hh in heads:
                lanes = hl(hh)
                qf, kf = q_ref[r, lanes], k_ref[r, lanes]
                rq, rk = _rstd(qf), _rstd(kf)
                dqn, dkn = dqa[hh, r, :] * scale, dka[hh, r, :]
                dqw_ref[...] += jnp.sum(dqn * qf * rq, axis=0, keepdims=True)
                dkw_ref[...] += jnp.sum(dkn * kf * rk, axis=0, keepdims=True)
                dq_ref[r, lanes] = _rms_bwd(qf, rq, dqn * qw_ref[...])
                dk_ref[r, lanes] = _rms_bwd(kf, rk, dkn * kw_ref[...])
                dg_ref[r, :] += jnp.where(_lane((tq, 128)) == 2 * p + hh, cols[hh, r, 3:4], 0.0)
            return 0

        lax.fori_loop(0, nq, post, 0)

    blk = lambda off: _once((seq, 128), lambda b, p: (b, off + p))
    own = lambda: _once((seq, 128), lambda b, p: (b, p))
    vec = lambda: pl.BlockSpec((1, dh), lambda b, p: (0, 0))
    return pl.pallas_call(
        body, grid=(nb, 4),
        in_specs=[blk(0), blk(4), blk(8), _once((seq, 128), lambda b, p: (b, 0)), vec(), vec(), own(), own(), own()],
        out_specs=[own(), own(), own(), _once((seq, 128), lambda b, p: (b, 0)), vec(), vec()],
        out_shape=[jax.ShapeDtypeStruct((t, 512), F32)] * 3
                  + [jax.ShapeDtypeStruct((t, 128), F32), jax.ShapeDtypeStruct((1, dh), F32), jax.ShapeDtypeStruct((1, dh), F32)],
        scratch_shapes=[pltpu.VMEM((2, seq, dh), BF16)] * 4
                       + [pltpu.VMEM((2, seq, 128), F32), pltpu.VMEM((2 * nq, 1, tq), F32),
                          pltpu.VMEM((2, seq, dh), F32), pltpu.VMEM((2, seq, dh), F32)],
        compiler_params=_cp("arbitrary", "arbitrary"), name="attn_bwd")(proj, proj, proj, gates, qw, kw, y, lse, dy)


def _silu_grad(c, sg):
    return sg * (1.0 + c * (1.0 - sg))


def _conv(x, w, n):
    row = _row(x.shape)
    c = x * w[CONV_W - 1:CONV_W, :]
    for k in range(CONV_W - 1):
        sh = CONV_W - 1 - k
        c = c + w[k:k + 1, :] * jnp.where(row >= sh, pltpu.roll(x, sh, 0), 0.0)
    return c


def gdn_pre_fwd(proj, cw, seq):
    t = proj.shape[0]
    nb = t // seq
    scale = GDN_DH ** -0.5

    def body(xq_ref, xk_ref, xv_ref, wq_ref, wk_ref, wv_ref, q_ref, k_ref, v_ref):
        def act(x_ref, w_ref):
            c = _conv(x_ref[...], w_ref[...], seq)
            return c * jax.nn.sigmoid(c)

        aq, ak = act(xq_ref, wq_ref), act(xk_ref, wk_ref)
        q_ref[...] = aq * lax.rsqrt(jnp.sum(aq * aq, axis=1, keepdims=True) + EPS) * scale
        k_ref[...] = ak * lax.rsqrt(jnp.sum(ak * ak, axis=1, keepdims=True) + EPS)
        v_ref[...] = act(xv_ref, wv_ref)

    xb = lambda off: pl.BlockSpec((seq, 128), lambda b, h: (b, off + h))
    wb = lambda off: pl.BlockSpec((CONV_W, 128), lambda b, h: (0, off + h))
    ob = lambda: pl.BlockSpec((seq, 128), lambda b, h: (b, h))
    return pl.pallas_call(
        body, grid=(nb, GDN_HEADS),
        in_specs=[xb(12), xb(16), xb(20), wb(0), wb(4), wb(8)],
        out_specs=[ob(), ob(), ob()],
        out_shape=[jax.ShapeDtypeStruct((t, 512), F32)] * 3,
        compiler_params=_cp("parallel", "parallel"), name="gdn_pre_fwd")(proj, proj, proj, cw, cw, cw)


def gdn_pre_bwd(proj, cw, dq, dk, dv, seq):
    t = proj.shape[0]
    nb = t // seq
    scale = GDN_DH ** -0.5

    def body(xq_ref, xk_ref, xv_ref, wq_ref, wk_ref, wv_ref, dq_ref, dk_ref, dv_ref,
             dxq_ref, dxk_ref, dxv_ref, dwq_ref, dwk_ref, dwv_ref):
        first = pl.program_id(1) == 0
        row = _row((seq, 128))

        def one(x_ref, w_ref, dy_ref, dx_ref, dw_ref, norm, sc):
            x, w = x_ref[...], w_ref[...]
            c = _conv(x, w, seq)
            sg = jax.nn.sigmoid(c)
            dy = dy_ref[...]
            if norm:
                a = c * sg
                rs = lax.rsqrt(jnp.sum(a * a, axis=1, keepdims=True) + EPS)
                dy = dy * sc
                da = rs * dy - a * (rs * rs * rs) * jnp.sum(dy * a, axis=1, keepdims=True)
            else:
                da = dy
            dc = da * _silu_grad(c, sg)
            dx = dc * w[CONV_W - 1:CONV_W, :]
            dws = [None] * CONV_W
            dws[CONV_W - 1] = jnp.sum(dc * x, axis=0, keepdims=True)
            for k in range(CONV_W - 1):
                sh = CONV_W - 1 - k
                dx = dx + w[k:k + 1, :] * jnp.where(row < seq - sh, pltpu.roll(dc, seq - sh, 0), 0.0)
                dws[k] = jnp.sum(dc * jnp.where(row >= sh, pltpu.roll(x, sh, 0), 0.0), axis=0, keepdims=True)
            dx_ref[...] = dx
            dwn = jnp.concatenate(dws, axis=0)

            @pl.when(first)
            def _():
                dw_ref[...] = dwn

            @pl.when(jnp.logical_not(first))
            def _():
                dw_ref[...] += dwn

        one(xq_ref, wq_ref, dq_ref, dxq_ref, dwq_ref, True, scale)
        one(xk_ref, wk_ref, dk_ref, dxk_ref, dwk_ref, True, 1.0)
        one(xv_ref, wv_ref, dv_ref, dxv_ref, dwv_ref, False, 1.0)

    xb = lambda off: pl.BlockSpec((seq, 128), lambda h, b: (b, off + h))
    wb = lambda off: pl.BlockSpec((CONV_W, 128), lambda h, b: (0, off + h))
    ob = lambda: pl.BlockSpec((seq, 128), lambda h, b: (b, h))
    return pl.pallas_call(
        body, grid=(GDN_HEADS, nb),
        in_specs=[xb(12), xb(16), xb(20), wb(0), wb(4), wb(8), ob(), ob(), ob()],
        out_specs=[ob(), ob(), ob()] + [pl.BlockSpec((CONV_W, 128), lambda h, b: (0, h))] * 3,
        out_shape=[jax.ShapeDtypeStruct((t, 512), F32)] * 3 + [jax.ShapeDtypeStruct((CONV_W, 512), F32)] * 3,
        compiler_params=_cp("parallel", "arbitrary"), name="gdn_pre_bwd")(proj, proj, proj, cw, cw, cw, dq, dk, dv)


def _b16(x):
    return x.astype(BF16)


@jax.custom_vjp
def _mm(a, b):
    return _dot(_b16(a), _b16(b))


_mm.defvjp(lambda a, b: (_mm(a, b), (a, b)),
           lambda res, g: (_dot_nt(_b16(g), _b16(res[1])), _dot_tn(_b16(res[0]), _b16(g))))


@jax.custom_vjp
def _mm_nt(a, b):
    return _dot_nt(_b16(a), _b16(b))


_mm_nt.defvjp(lambda a, b: (_mm_nt(a, b), (a, b)),
              lambda res, g: (_dot(_b16(g), _b16(res[1])), _dot_tn(_b16(g), _b16(res[0]))))


@jax.custom_vjp
def _mm_tn(a, b):
    return _dot_tn(_b16(a), _b16(b))


_mm_tn.defvjp(lambda a, b: (_mm_tn(a, b), (a, b)),
              lambda res, g: (_dot_nt(_b16(res[1]), _b16(g)), _dot(_b16(res[0]), _b16(g))))


def _dot32(a, b, dims=(((1,), (0,)), ((), ()))):
    def split(x):
        hi = x.astype(BF16)
        return hi, (x - hi.astype(F32)).astype(BF16)

    (ah, al), (bh, bl) = split(a), split(b)
    d = lambda x, y: lax.dot_general(x, y, dims, preferred_element_type=F32)
    return d(ah, bh) + (d(ah, bl) + d(al, bh))


def _inv_fwd_many(mats):
    n = mats[0].shape[0]
    eye = (_row((n, n)) == _lane((n, n))).astype(F32)
    invs, pws = [eye - a for a in mats], list(mats)
    for _ in range(n.bit_length() - 2):
        pws = [_dot32(pw, pw) for pw in pws]
        invs = [inv + _dot32(inv, pw) for inv, pw in zip(invs, pws)]
    return invs


@jax.custom_vjp
def _inv_saved(a, inv):
    return inv


def _inv_saved_bwd(inv, g):
    tg = _dot32(inv, g, (((0,), (0,)), ((), ())))
    return -_dot32(tg, inv, (((1,), (1,)), ((), ()))), jnp.zeros_like(inv)


_inv_saved.defvjp(lambda a, inv: (inv, inv), _inv_saved_bwd)


def _gdn_decay(gcol):
    c = CHUNK
    ri, ci = _row((c, c)), _lane((c, c))
    incl, eye = ri >= ci, ri == ci
    grow = jnp.sum(jnp.where(eye, gcol, 0.0), axis=0, keepdims=True)
    gc = jnp.sum(jnp.where(incl, grow, 0.0), axis=1, keepdims=True)
    gcr = jnp.sum(jnp.where(eye, gc, 0.0), axis=0, keepdims=True)
    gl = jnp.sum(jnp.where(_row((c, 1)) == c - 1, gc, 0.0), axis=0, keepdims=True)
    return gc, gl, jnp.exp(jnp.where(incl, gc - gcr, NEG))


def _gdn_a(k, bcol, decay):
    c = CHUNK
    return jnp.where(_row((c, c)) > _lane((c, c)), _mm_nt(k * bcol, k) * decay, 0.0)


def _gdn_chunk(q, k, v, gcol, bcol, state, gg, nw, inv_saved):
    c = CHUNK
    incl = _row((c, c)) >= _lane((c, c))
    gc, gl, decay = _gdn_decay(gcol)
    kb, vb = k * bcol, v * bcol
    inv = _inv_saved(_gdn_a(k, bcol, decay), inv_saved)
    eg = jnp.exp(gc)
    u = _mm(inv, vb)
    w = _mm(inv, kb * eg)
    pm = jnp.where(incl, _mm_nt(q, k) * decay, 0.0)
    kd = k * jnp.exp(gl - gc)
    qd = q * eg
    v_new = u - _mm(w, state)
    o = _mm(qd, state) + _mm(pm, v_new)
    state_new = state * jnp.exp(gl) + _mm_tn(kd, v_new)
    y = o * _rstd(o) * nw * (gg * jax.nn.sigmoid(gg))
    return y, state_new


_gdn_chunks = jax.vmap(_gdn_chunk, in_axes=(0, 0, 0, 0, 0, 0, 0, None, 0))


def _gdn_chain_inputs(chains, p, r, c, q_ref, k_ref, v_ref, g_ref, gg_ref, inv_ref):
    cols = {nm: [] for nm in ("q", "k", "v", "g", "b", "gg", "inv")}
    for b, hh in chains:
        h = GDN_HPS * p + hh
        ln = slice(hh * 128, (hh + 1) * 128)
        gt = g_ref[b, r, :]
        cols["q"].append(q_ref[b, r, ln])
        cols["k"].append(k_ref[b, r, ln])
        cols["v"].append(v_ref[b, r, ln])
        cols["g"].append(_pick_lane(gt, 8 + h))
        cols["b"].append(_pick_lane(gt, 12 + h))
        cols["gg"].append(gg_ref[b, r, ln])
        cols["inv"].append(inv_ref[b, hh, c])
    return [jnp.stack(cols[nm]) for nm in ("q", "k", "v", "g", "b", "gg", "inv")]


GDN_CB = 8
GDN_HPS = 4


def gdn_inv(k, gates, seq):
    t = k.shape[0]
    nb, nc = t // seq, seq // CHUNK
    rb = GDN_CB * CHUNK
    nsb = seq // rb

    def body(k_ref, g_ref, o_ref):
        h = pl.program_id(1)
        mats = []
        for c in range(GDN_CB):
            r = slice(c * CHUNK, (c + 1) * CHUNK)
            gt = g_ref[r, :]
            _, _, decay = _gdn_decay(_pick_lane(gt, 8 + h))
            mats.append(_gdn_a(k_ref[r, :], _pick_lane(gt, 12 + h), decay))
        for c, inv in enumerate(_inv_fwd_many(mats)):
            o_ref[c] = inv

    return pl.pallas_call(
        body, grid=(nb, GDN_HEADS, nsb),
        in_specs=[pl.BlockSpec((rb, 128), lambda b, h, s: (b * nsb + s, h)),
                  pl.BlockSpec((rb, 128), lambda b, h, s: (b * nsb + s, 0))],
        out_specs=pl.BlockSpec((None, None, GDN_CB, CHUNK, CHUNK), lambda b, h, s: (b, h, s, 0, 0)),
        out_shape=jax.ShapeDtypeStruct((nb, GDN_HEADS, nc, CHUNK, CHUNK), F32),
        compiler_params=_cp("parallel", "parallel", "parallel"), name="gdn_inv")(k, gates)


def _gdn_specs(nb, nsb, cb, rev):
    blk = (lambda s: nsb - 1 - s) if rev else (lambda s: s)
    rb = cb * CHUNK
    pair = lambda off=0: pl.BlockSpec((nb, rb, 128 * GDN_HPS), lambda s, p: (0, blk(s), off + p))
    gate = lambda: pl.BlockSpec((nb, rb, 128), lambda s, p: (0, blk(s), 0))
    mats = lambda n: pl.BlockSpec((nb, GDN_HPS, cb, n, n), lambda s, p: (0, p, blk(s), 0, 0))
    return pair, gate, mats


def gdn_fwd(q, k, v, gates, proj, nw, inv, seq):
    t = q.shape[0]
    nb, nc = t // seq, seq // CHUNK
    cb = GDN_CB
    nsb = nc // cb
    chains = [(b, hh) for b in range(nb) for hh in range(GDN_HPS)]
    nch = len(chains)
    pair, gate, mats = _gdn_specs(nb, nsb, cb, False)

    def body(q_ref, k_ref, v_ref, g_ref, gg_ref, inv_ref, nw_ref, y_ref, st_ref, carry):
        s, p = pl.program_id(0), pl.program_id(1)

        @pl.when(s == 0)
        def _():
            for ci in range(nch):
                carry[p * nch + ci] = jnp.zeros((GDN_DH, GDN_DH), F32)

        def step(c, states):
            r = _rows(c, CHUNK)
            for ci, (b, hh) in enumerate(chains):
                st_ref[b, hh, c] = states[ci]
            ins = _gdn_chain_inputs(chains, p, r, c, q_ref, k_ref, v_ref, g_ref, gg_ref, inv_ref)
            y, states = _gdn_chunks(*ins[:5], states, ins[5], nw_ref[...], ins[6])
            for ci, (b, hh) in enumerate(chains):
                y_ref[b, r, hh * 128:(hh + 1) * 128] = y[ci]
            return states

        states = lax.fori_loop(0, cb, step, jnp.stack([carry[p * nch + ci] for ci in range(nch)]))
        for ci in range(nch):
            carry[p * nch + ci] = states[ci]

    v3 = lambda a: a.reshape(nb, seq, a.shape[1])
    y, st = pl.pallas_call(
        body, grid=(nsb, GDN_HEADS // GDN_HPS),
        in_specs=[pair(), pair(), pair(), gate(), pair(24 // GDN_HPS), mats(CHUNK), pl.BlockSpec((1, 128), lambda s, p: (0, 0))],
        out_specs=[pair(), mats(GDN_DH)],
        out_shape=[jax.ShapeDtypeStruct((nb, seq, 512), F32),
                   jax.ShapeDtypeStruct((nb, GDN_HEADS, nc, GDN_DH, GDN_DH), F32)],
        scratch_shapes=[pltpu.VMEM((GDN_HEADS // GDN_HPS * nch, GDN_DH, GDN_DH), F32)],
        compiler_params=_cp("arbitrary", "arbitrary"), name="gdn_fwd")(v3(q), v3(k), v3(v), v3(gates), v3(proj), inv, nw)
    return y.reshape(t, 512), st


def gdn_bwd(q, k, v, gates, proj, nw, inv, states, dy, seq):
    t = q.shape[0]
    nb, nc = t // seq, seq // CHUNK
    cb = GDN_CB // 2
    nsb = nc // cb
    chains = [(b, hh) for b in range(nb) for hh in range(GDN_HPS)]
    nch = len(chains)
    pair, gate, mats = _gdn_specs(nb, nsb, cb, True)

    def body(q_ref, k_ref, v_ref, g_ref, gg_ref, inv_ref, st_ref, dy_ref, nw_ref,
             dq_ref, dk_ref, dv_ref, dgg_ref, dg_ref, dnw_ref, carry):
        s, p = pl.program_id(0), pl.program_id(1)

        @pl.when((s == 0) & (p == 0))
        def _():
            dnw_ref[...] = jnp.zeros_like(dnw_ref)

        @pl.when(p == 0)
        def _():
            dg_ref[...] = jnp.zeros_like(dg_ref)

        @pl.when(s == 0)
        def _():
            for ci in range(nch):
                carry[p * nch + ci] = jnp.zeros((GDN_DH, GDN_DH), F32)

        def step(idx, dstates):
            c = cb - 1 - idx
            r = _rows(c, CHUNK)
            ins = _gdn_chain_inputs(chains, p, r, c, q_ref, k_ref, v_ref, g_ref, gg_ref, inv_ref)
            st = jnp.stack([st_ref[b, hh, c] for b, hh in chains])
            dy = jnp.stack([dy_ref[b, r, hh * 128:(hh + 1) * 128] for b, hh in chains])
            _, vjp = jax.vjp(_gdn_chunks, *ins[:5], st, ins[5], nw_ref[...], ins[6])
            dq, dk, dv, dgc, dbc, dstates, dgg, dnw, _ = vjp((dy, dstates))
            dnw_ref[...] += dnw
            lane = _lane((CHUNK, 128))
            for ci, (b, hh) in enumerate(chains):
                h = GDN_HPS * p + hh
                ln = slice(hh * 128, (hh + 1) * 128)
                dq_ref[b, r, ln] = dq[ci]
                dk_ref[b, r, ln] = dk[ci]
                dv_ref[b, r, ln] = dv[ci]
                dgg_ref[b, r, ln] = dgg[ci]
                dg_ref[b, r, :] = jnp.where(lane == 8 + h, dgc[ci], jnp.where(lane == 12 + h, dbc[ci], dg_ref[b, r, :]))
            return dstates

        dstates = lax.fori_loop(0, cb, step, jnp.stack([carry[p * nch + ci] for ci in range(nch)]))
        for ci in range(nch):
            carry[p * nch + ci] = dstates[ci]

    v3 = lambda a: a.reshape(nb, seq, a.shape[1])
    res = pl.pallas_call(
        body, grid=(nsb, GDN_HEADS // GDN_HPS),
        in_specs=[pair(), pair(), pair(), gate(), pair(24 // GDN_HPS), mats(CHUNK), mats(GDN_DH), pair(),
                  pl.BlockSpec((1, 128), lambda s, p: (0, 0))],
        out_specs=[pair(), pair(), pair(), pair(), gate(), pl.BlockSpec((1, 128), lambda s, p: (0, 0))],
        out_shape=[jax.ShapeDtypeStruct((nb, seq, 512), F32)] * 4
                  + [jax.ShapeDtypeStruct((nb, seq, 128), F32), jax.ShapeDtypeStruct((1, 128), F32)],
        scratch_shapes=[pltpu.VMEM((GDN_HEADS // GDN_HPS * nch, GDN_DH, GDN_DH), F32)],
        compiler_params=_cp("arbitrary", "arbitrary"),
        name="gdn_bwd")(v3(q), v3(k), v3(v), v3(gates), v3(proj), inv, states, v3(dy), nw)
    return [a.reshape(t, a.shape[2]) for a in res[:5]] + [res[5]]


def loss_head(y, target, tm=512):
    t, d = y.shape

    def body(y_ref, t_ref, s_ref, dy_ref):
        @pl.when(pl.program_id(0) == 0)
        def _():
            s_ref[...] = jnp.zeros_like(s_ref)

        err = y_ref[...] - t_ref[...]
        s_ref[...] += jnp.sum(err * err, axis=0, keepdims=True)
        dy_ref[...] = err * (1.0 / d)

    return pl.pallas_call(
        body, grid=(t // tm,),
        in_specs=[pl.BlockSpec((tm, d), lambda i: (i, 0)), pl.BlockSpec((tm, d), lambda i: (i, 0))],
        out_specs=[pl.BlockSpec((1, d), lambda i: (0, 0)), pl.BlockSpec((tm, d), lambda i: (i, 0))],
        out_shape=[jax.ShapeDtypeStruct((1, d), F32), jax.ShapeDtypeStruct((t, d), F32)],
        compiler_params=_cp("arbitrary"), name="loss_head")(y, target)


def _place():
    return lax.axis_index("x"), lax.axis_index("y"), lax.axis_index("c")


def _peer(k):
    x, y, c = _place()
    px = 1 - x if (k >> 2) & 1 else x
    py = 1 - y if (k >> 1) & 1 else y
    pc = 1 - c if k & 1 else c
    return (px, py, pc), 4 * px + 2 * py + pc


_ANY = pl.BlockSpec(memory_space=pl.ANY)
_SEM = pl.BlockSpec(memory_space=pltpu.SEMAPHORE)
_EFFECT = pltpu.SideEffectType.DATAFLOW_SIDE_EFFECTING


def _me():
    x, y, c = _place()
    return 4 * x + 2 * y + c


def _remote_copy(ins, lands, scatter, send_sems, recv_sems, a, k, arriving):
    pid, pidx = _peer(k)
    return pltpu.make_async_remote_copy(src_ref=ins[a].at[pidx] if scatter[a] else ins[a],
                                        dst_ref=lands[a].at[pidx if arriving else _me()],
                                        send_sem=send_sems.at[a * N_DEV + k], recv_sem=recv_sems.at[a * N_DEV + k],
                                        device_id=pid, device_id_type=MESH)


def _local_copy(ins, lands, scatter, loc_sems, a):
    me = _me()
    return pltpu.make_async_copy(ins[a].at[me] if scatter[a] else ins[a], lands[a].at[me], loc_sems.at[a])


def exchange_start(arrays, scatter, name, after):
    n = len(arrays)
    lands = [lax.empty(a.shape if s else (N_DEV,) + a.shape, a.dtype) for a, s in zip(arrays, scatter)]

    def body(*refs):
        ins, lds = refs[:n], refs[n:2 * n]
        send_sems, recv_sems, loc_sems = refs[2 * n + 1:2 * n + 4]
        token = refs[-1]
        for k in range(1, N_DEV):
            for a in range(n):
                _remote_copy(ins, lds, scatter, send_sems, recv_sems, a, k, False).start()
        for a in range(n):
            _local_copy(ins, lds, scatter, loc_sems, a).start()
        token[...] = jnp.zeros_like(token)

    hbm = lambda a: pltpu.HBM(a.shape, a.dtype)
    res = pl.pallas_call(
        body, name=name,
        in_specs=[_ANY] * (2 * n + 1),
        out_specs=[_SEM, _SEM, _SEM] + [_ANY] * (2 * n) + [pl.BlockSpec(memory_space=pltpu.VMEM)],
        out_shape=[pltpu.SemaphoreType.DMA((n * N_DEV,)), pltpu.SemaphoreType.DMA((n * N_DEV,)),
                   pltpu.SemaphoreType.DMA((n,))]
                  + [hbm(a) for a in arrays] + [hbm(a) for a in lands] + [jax.ShapeDtypeStruct((8, 128), F32)],
        input_output_aliases={i: 3 + i for i in range(2 * n)},
        compiler_params=pltpu.CompilerParams(has_side_effects=_EFFECT),
    )(*[pltpu.with_memory_space_constraint(a, pltpu.HBM) for a in list(arrays) + lands], after)
    return res[0:3], res[3:3 + n], res[3 + n:3 + 2 * n], res[-1]


def exchange_wait(sems, arrays, lands, scatter, after, name):
    n = len(arrays)

    def body(*refs):
        ins, lds = refs[:n], refs[n:2 * n]
        ssem, rsem, lsem = refs[2 * n:2 * n + 3]
        for a in range(n):
            _local_copy(ins, lds, scatter, lsem, a).wait()
        for k in range(1, N_DEV):
            for a in range(n):
                _remote_copy(ins, lds, scatter, ssem, rsem, a, k, True).wait_recv()
        for k in range(1, N_DEV):
            for a in range(n):
                _remote_copy(ins, lds, scatter, ssem, rsem, a, k, False).wait_send()

    hbm = lambda a: pltpu.HBM(a.shape, a.dtype)
    res = pl.pallas_call(
        body, name=name,
        in_specs=[_ANY] * (2 * n) + [_SEM, _SEM, _SEM, _ANY],
        out_specs=[_ANY] * (2 * n),
        out_shape=[hbm(a) for a in arrays] + [hbm(a) for a in lands],
        input_output_aliases={i: i for i in range(2 * n)},
        compiler_params=pltpu.CompilerParams(has_side_effects=_EFFECT),
    )(*arrays, *lands, *sems, after)
    return list(res[n:])


def exchange_begin(arrays, scatter, name, after):
    sems, arrays_thru, lands_thru, token = exchange_start(arrays, scatter, name + "_start", after)
    return (sems, arrays_thru, lands_thru, scatter, name), token


def exchange_end(state, after):
    sems, arrays_thru, lands_thru, scatter, name = state
    return exchange_wait(sems, arrays_thru, lands_thru, scatter, after, name + "_wait")


def adamw_reduce(slots, w, m, v, l, name, after=None, prev=None):
    nl, r, c = w.shape
    tr = r
    while tr * c * 4 > (1 << 20) and tr % 16 == 0:
        tr //= 2
    bc1 = 1.0 - ADAM_B1 ** ADAM_STEP
    bc2 = 1.0 - ADAM_B2 ** ADAM_STEP

    def body(s_ref, w_ref, m_ref, v_ref, *rest):
        g_ref, d_ref, nm_ref, nv_ref = rest[-4:]
        g = s_ref[0].astype(F32)
        for j in range(1, N_DEV):
            g = g + s_ref[j].astype(F32)
        nm = ADAM_B1 * m_ref[...] + (1.0 - ADAM_B1) * g
        nv = ADAM_B2 * v_ref[...] + (1.0 - ADAM_B2) * (g * g)
        g_ref[...] = g
        nm_ref[...] = nm
        nv_ref[...] = nv
        d_ref[...] = -ADAM_LR * ((nm / bc1) / (jnp.sqrt(nv / bc2) + ADAM_EPS) + ADAM_WD * w_ref[...])

    blk = lambda: pl.BlockSpec((None, tr, c), lambda i: (l, i, 0))
    extra = ([] if after is None else [after]) + ([] if prev is None else list(prev))
    first_prev = 4 + (after is not None)
    return pl.pallas_call(
        body, grid=(r // tr,),
        in_specs=[pl.BlockSpec((N_DEV, tr, c), lambda i: (0, i, 0)), blk(), blk(), blk()] + [_ANY] * len(extra),
        out_specs=[blk(), blk(), blk(), blk()],
        out_shape=[jax.ShapeDtypeStruct((nl, r, c), F32)] * 4,
        input_output_aliases={} if prev is None else {first_prev + j: j for j in range(4)},
        compiler_params=_cp("parallel"), name=name)(slots, w, m, v, *extra)


BIG = ("ffn1_w_in", "ffn1_w_out", "w_in", "gdn_conv", "w_out", "ffn2_w_in", "ffn2_w_out")
GROUPS = (BIG[0:2], BIG[2:5], BIG[5:7])
SMALL = ("ffn1_norm", "mix_norm", "fox_q_norm", "fox_k_norm", "fox_f_bias", "gdn_a_log", "gdn_dt_bias",
         "gdn_out_norm", "ffn2_norm")
WEIGHTS = ("ffn1_norm", "ffn1_w_in", "ffn1_w_out", "mix_norm", "w_in", "fox_q_norm", "fox_k_norm", "fox_f_bias",
           "gdn_conv", "gdn_a_log", "gdn_dt_bias", "gdn_out_norm", "w_out", "ffn2_norm", "ffn2_w_in", "ffn2_w_out")
IN_COLS = (("fq", 512), ("fk", 512), ("fv", 512), ("ff", 8), ("gq", 512), ("gk", 512), ("gv", 512),
           ("ga", 4), ("gb", 4), ("gg", 512))
MY_BIG = ("fq", "fk", "fv", "gq", "gk", "gv", "gg")
MY_SMALL = ("ff", "ga", "gb")
SMALL_ROWS = 8 * 128


def _in_cols_to_mine(w):
    off, parts = 0, {}
    for nm, wd in IN_COLS:
        parts[nm] = w[:, off:off + wd]
        off += wd
    small = jnp.concatenate([parts[nm] for nm in MY_SMALL], axis=1)
    small = jnp.pad(small, ((0, 0), (0, 128 - small.shape[1])))
    return jnp.concatenate([parts[nm] for nm in MY_BIG] + [small], axis=1)


def _in_cols_from_mine(g):
    parts = {nm: g[:, i * 512:(i + 1) * 512] for i, nm in enumerate(MY_BIG)}
    off = N_BIG
    for nm in MY_SMALL:
        wd = dict(IN_COLS)[nm]
        parts[nm] = g[:, off:off + wd]
        off += wd
    return jnp.concatenate([parts[nm] for nm, _ in IN_COLS], axis=1)


def _pack_small(vals):
    rows = []
    nl = vals[SMALL[0]].shape[0]
    for l in range(nl):
        for nm in SMALL:
            v = vals[nm][l].reshape(-1)
            pad = (-v.shape[0]) % SMALL_ROWS
            rows.append(jnp.pad(v, (0, pad)).reshape(-1, 128))
    return jnp.concatenate(rows, axis=0)


def _unpack_small(packed, like):
    out = {nm: [] for nm in SMALL}
    row = 0
    nl = like[SMALL[0]].shape[0]
    for l in range(nl):
        for nm in SMALL:
            n = like[nm].shape[1]
            nr = -(-n // SMALL_ROWS) * 8
            out[nm].append(packed[row:row + nr].reshape(-1)[:n])
            row += nr
    return {nm: jnp.stack(v) for nm, v in out.items()}


def kernel(x, ffn1_norm, ffn1_w_in, ffn1_w_out, mix_norm, w_in, fox_q_norm, fox_k_norm, fox_f_bias, gdn_conv, gdn_a_log, gdn_dt_bias, gdn_out_norm, w_out, ffn2_norm, ffn2_w_in, ffn2_w_out, loss_target, m_ffn1_norm, m_ffn1_w_in, m_ffn1_w_out, m_mix_norm, m_w_in, m_fox_q_norm, m_fox_k_norm, m_fox_f_bias, m_gdn_conv, m_gdn_a_log, m_gdn_dt_bias, m_gdn_out_norm, m_w_out, m_ffn2_norm, m_ffn2_w_in, m_ffn2_w_out, v_ffn1_norm, v_ffn1_w_in, v_ffn1_w_out, v_mix_norm, v_w_in, v_fox_q_norm, v_fox_k_norm, v_fox_f_bias, v_gdn_conv, v_gdn_a_log, v_gdn_dt_bias, v_gdn_out_norm, v_w_out, v_ffn2_norm, v_ffn2_w_in, v_ffn2_w_out):
    wts = dict(ffn1_norm=ffn1_norm, ffn1_w_in=ffn1_w_in, ffn1_w_out=ffn1_w_out, mix_norm=mix_norm, w_in=w_in,
               fox_q_norm=fox_q_norm, fox_k_norm=fox_k_norm, fox_f_bias=fox_f_bias, gdn_conv=gdn_conv,
               gdn_a_log=gdn_a_log, gdn_dt_bias=gdn_dt_bias, gdn_out_norm=gdn_out_norm, w_out=w_out,
               ffn2_norm=ffn2_norm, ffn2_w_in=ffn2_w_in, ffn2_w_out=ffn2_w_out)
    mom = dict(ffn1_norm=m_ffn1_norm, ffn1_w_in=m_ffn1_w_in, ffn1_w_out=m_ffn1_w_out, mix_norm=m_mix_norm, w_in=m_w_in,
               fox_q_norm=m_fox_q_norm, fox_k_norm=m_fox_k_norm, fox_f_bias=m_fox_f_bias, gdn_conv=m_gdn_conv,
               gdn_a_log=m_gdn_a_log, gdn_dt_bias=m_gdn_dt_bias, gdn_out_norm=m_gdn_out_norm, w_out=m_w_out,
               ffn2_norm=m_ffn2_norm, ffn2_w_in=m_ffn2_w_in, ffn2_w_out=m_ffn2_w_out)
    var = dict(ffn1_norm=v_ffn1_norm, ffn1_w_in=v_ffn1_w_in, ffn1_w_out=v_ffn1_w_out, mix_norm=v_mix_norm, w_in=v_w_in,
               fox_q_norm=v_fox_q_norm, fox_k_norm=v_fox_k_norm, fox_f_bias=v_fox_f_bias, gdn_conv=v_gdn_conv,
               gdn_a_log=v_gdn_a_log, gdn_dt_bias=v_gdn_dt_bias, gdn_out_norm=v_gdn_out_norm, w_out=v_w_out,
               ffn2_norm=v_ffn2_norm, ffn2_w_in=v_ffn2_w_in, ffn2_w_out=v_ffn2_w_out)
    nb, seq, d = x.shape
    t = nb * seq
    depth = ffn1_norm.shape[0]

    stages = [(l, gi) for l in range(depth) for gi in range(len(GROUPS))]

    def shards_of(l, gi):
        return [wts[nm][l] if nm == "gdn_conv" else wts[nm][l].astype(BF16) for nm in GROUPS[gi]]

    def behind(nw, token):
        return nw if token is None else nw + token[0:1, 0:1]

    def small_params(l):
        return dict(
            n1=ffn1_norm[l][None], nmix=mix_norm[l][None], n2=ffn2_norm[l][None],
            qw=fox_q_norm[l][None], kw=fox_k_norm[l][None], onw=gdn_out_norm[l][None],
            gp=jnp.concatenate([
                jnp.concatenate([fox_f_bias[l], gdn_dt_bias[l], jnp.zeros((116,), F32)])[None],
                jnp.concatenate([jnp.zeros((8,), F32), gdn_a_log[l], jnp.zeros((116,), F32)])[None],
                jnp.zeros((6, 128), F32)], axis=0))

    h = x.reshape(t, d)
    state, token = exchange_begin(shards_of(0, 0), [False] * len(GROUPS[0]), "gather_0", ffn1_norm)
    landed = exchange_end(state, token)
    saved = [dict(p=small_params(l)) for l in range(depth)]
    for k, (l, gi) in enumerate(stages):
        s, w, token = saved[l], landed, None
        p = s["p"]
        if k + 1 < len(stages):
            nl, ng = stages[k + 1]
            state, token = exchange_begin(shards_of(nl, ng), [False] * len(GROUPS[ng]), f"gather_{k + 1}", landed[0])
        if gi == 0:
            fb = w[0].shape[2]
            p["w1i"], p["w1o"] = w[0].reshape(2, 4, d, fb), w[1].reshape(4, fb, d)
            s["x0"] = h
            h, *s["ffn1"] = ffn_fwd(h, behind(p["n1"], token), p["w1i"], p["w1o"])
            s["x1"] = h
        elif gi == 1:
            p["wi"] = _in_cols_to_mine(w[0].transpose(1, 0, 2).reshape(d, -1))
            p["cw"] = w[1].transpose(1, 0, 2).reshape(CONV_W, -1)
            p["wo"] = w[2].reshape(d, d)
            proj, hn = inproj_fwd(h, behind(p["nmix"], token), p["wi"])
            gates = gates_fwd(proj, p["gp"], seq)
            yf, lse = attn_fwd(proj, gates, p["qw"], p["kw"], seq, tq=min(seq, ATTN_TQ_FWD))
            qh, kh, vh = gdn_pre_fwd(proj, p["cw"], seq)
            inv = gdn_inv(kh, gates, seq)
            yg, st = gdn_fwd(qh, kh, vh, gates, proj, p["onw"], inv, seq)
            h, ycat = outproj_fwd(h, yf, yg, p["wo"])
            s.update(x2=h, proj=proj, hn=hn, gates=gates, yf=yf, lse=lse, qh=qh, kh=kh, vh=vh, st=st, inv=inv, ycat=ycat)
        else:
            fb = w[0].shape[2]
            p["w2i"], p["w2o"] = w[0].reshape(2, 4, d, fb), w[1].reshape(4, fb, d)
            h, *s["ffn2"] = ffn_fwd(h, behind(p["n2"], token), p["w2i"], p["w2o"])
        if k + 1 < len(stages):
            landed = exchange_end(state, h)

    sq, dh = loss_head(h, loss_target.reshape(t, d))
    loss = lax.psum(0.5 * jnp.sum(sq) / d, ("x", "y", "c"))

    got = [None] * len(stages)
    pending, token = None, None
    gsmall = {nm: [None] * depth for nm in SMALL}
    for k in reversed(range(len(stages))):
        l, gi = stages[k]
        s = saved[l]
        p = s["p"]
        if gi != 1:
            nw, xin, wi_, wo_, nm_n, (xn, gu, hh) = (
                (p["n1"], s["x0"], p["w1i"], p["w1o"], "ffn1_norm", s["ffn1"]) if gi == 0 else
                (p["n2"], s["x2"], p["w2i"], p["w2o"], "ffn2_norm", s["ffn2"]))
            dh, dn, dgu, dyh = ffn_bwd(xin, dh, behind(nw, token), gu, wi_, wo_)
            g_in, g_out = wgrad_ffn_in(xn, dgu), wgrad_ffn_out(hh, dyh)
            send = [g_in.reshape(N_DEV, d, g_in.shape[3]), g_out.reshape(N_DEV, -1, d)]
            gsmall[nm_n][l] = dn[0]
        else:
            dyf, dyg, dyb = outproj_bwd(dh, p["wo"], token)
            g_wo = wgrad_2d(s["ycat"], dyb, 512, "wgrad_w_out")
            dq, dk, dv, dga, dqw, dkw = attn_bwd(s["proj"], s["gates"], p["qw"], p["kw"], s["yf"], s["lse"], dyf, seq,
                                                 tq=min(seq, ATTN_TQ_BWD))
            dqh, dkh, dvh, dgg, dgb, donw = gdn_bwd(s["qh"], s["kh"], s["vh"], s["gates"], s["proj"], p["onw"],
                                                     s["inv"], s["st"], dyg, seq)
            dxq, dxk, dxv, dwq, dwk, dwv = gdn_pre_bwd(s["proj"], p["cw"], dqh, dkh, dvh, seq)
            dsm, dgp = gates_bwd(s["proj"], p["gp"], dga, dgb, seq)
            dh, dnmix, dproj = inproj_bwd(s["x1"], dh, p["nmix"], p["wi"], [dq, dk, dv, dxq, dxk, dxv, dgg, dsm])
            g_wi = wgrad_2d(s["hn"], dproj, 512, "wgrad_w_in", F32)
            g_cw = jnp.concatenate([dwq, dwk, dwv], axis=1)
            send = [_in_cols_from_mine(g_wi).reshape(d, N_DEV, -1).transpose(1, 0, 2),
                    g_cw.reshape(CONV_W, N_DEV, -1).transpose(1, 0, 2), g_wo.reshape(N_DEV, -1, d)]
            for nm, val in (("mix_norm", dnmix[0]), ("fox_q_norm", dqw[0]), ("fox_k_norm", dkw[0]),
                            ("fox_f_bias", dgp[0, 0:8]), ("gdn_a_log", dgp[1, 8:12]), ("gdn_dt_bias", dgp[0, 8:12]),
                            ("gdn_out_norm", donw[0])):
                gsmall[nm][l] = val
        flags = [True] * len(send)
        if k == 0:
            send.append(_pack_small({nm: jnp.stack(v) for nm, v in gsmall.items()}))
            flags.append(False)
        prev = dh
        if pending is not None:
            got[pending[1]] = exchange_end(pending[0], dh)
            prev = got[pending[1]][0]
        state, token = exchange_begin(send, flags, f"exchange_grads_{k}", prev)
        pending = (state, k)
    grad_x = dh.reshape(nb, seq, d)

    res = {}

    def update_stage(k, slots, after):
        l, gi = stages[k]
        for i, nm in enumerate(GROUPS[gi]):
            r, c = wts[nm].shape[1:]
            res[nm] = adamw_reduce(slots[i].reshape(N_DEV, r, c), wts[nm], mom[nm], var[nm], l, f"adamw_{nm}_{l}",
                                   after, res.get(nm))
            if after is not None:
                after = res[nm][0]
        return after

    last = token
    for k in range(1, len(stages)):
        last = update_stage(k, got[k], last)
    got[0] = exchange_end(pending[0], last)
    update_stage(0, got[0], None)
    small_like = {nm: wts[nm] for nm in SMALL}
    sm = adamw_reduce(got[0][-1], _pack_small(small_like)[None], _pack_small({nm: mom[nm] for nm in SMALL})[None],
                      _pack_small({nm: var[nm] for nm in SMALL})[None], 0, "adamw_small")
    sm = [_unpack_small(a[0], small_like) for a in sm]
    for nm in SMALL:
        res[nm] = [sm[j][nm] for j in range(4)]
    return (loss, grad_x, *[res[nm][0] for nm in WEIGHTS], *[res[nm][1] for nm in WEIGHTS],
            *[res[nm][2] for nm in WEIGHTS], *[res[nm][3] for nm in WEIGHTS])
```

```python
import functools

import jax
import jax.numpy as jnp
from jax import lax
from jax.experimental import pallas as pl
from jax.experimental.pallas import tpu as pltpu

F32 = jnp.float32
BF16 = jnp.bfloat16
EPS = 1e-6
N_DEV = 8
MESH = pl.DeviceIdType.MESH
HIGHEST = lax.Precision.HIGHEST
VMEM_LIMIT = 56 * 1024 * 1024

FOX_HEADS, FOX_DH = 8, 64
GDN_HEADS, GDN_DH = 4, 128
CHUNK = 64
CONV_W = 4

ADAM_LR, ADAM_B1, ADAM_B2, ADAM_EPS, ADAM_WD, ADAM_STEP = 0.001, 0.9, 0.999, 1e-08, 0.01, 10


def _cp(*sem):
    return pltpu.CompilerParams(dimension_semantics=sem, vmem_limit_bytes=VMEM_LIMIT)


def _dot(a, b):
    return jnp.dot(a, b, preferred_element_type=F32)


def _dot_nt(a, b):
    return lax.dot_general(a, b, (((1,), (1,)), ((), ())), preferred_element_type=F32)


def _dot_tn(a, b):
    return lax.dot_general(a, b, (((0,), (0,)), ((), ())), preferred_element_type=F32)


def _rstd(xf):
    return lax.rsqrt(jnp.mean(xf * xf, axis=-1, keepdims=True) + EPS)


def _rms_bwd(xf, r, dyn):
    return r * dyn - xf * (r * r * r) * jnp.mean(dyn * xf, axis=-1, keepdims=True)


def ffn_fwd(x, nw, w_in, w_out, tm=512):
    t, d = x.shape
    nj, fb = w_out.shape[0], w_out.shape[1]

    def body(x_ref, nw_ref, wi_ref, wo_ref, o_ref, xn_ref, gu_ref, h_ref, acc_ref):
        j = pl.program_id(1)

        @pl.when(j == 0)
        def _():
            xf = x_ref[...]
            xn_ref[...] = (xf * _rstd(xf) * nw_ref[...]).astype(BF16)
            acc_ref[...] = jnp.zeros_like(acc_ref)

        xn = xn_ref[...]
        g = _dot(xn, wi_ref[0])
        u = _dot(xn, wi_ref[1])
        h = (g * jax.nn.sigmoid(g) * u).astype(BF16)
        gu_ref[0] = g.astype(BF16)
        gu_ref[1] = u.astype(BF16)
        h_ref[...] = h
        acc_ref[...] += _dot(h, wo_ref[...])

        @pl.when(j == nj - 1)
        def _():
            o_ref[...] = x_ref[...] + 0.5 * acc_ref[...]

    return pl.pallas_call(
        body, grid=(t // tm, nj),
        in_specs=[pl.BlockSpec((tm, d), lambda i, j: (i, 0)),
                  pl.BlockSpec((1, d), lambda i, j: (0, 0)),
                  pl.BlockSpec((2, None, d, fb), lambda i, j: (0, j, 0, 0)),
                  pl.BlockSpec((None, fb, d), lambda i, j: (j, 0, 0))],
        out_specs=[pl.BlockSpec((tm, d), lambda i, j: (i, 0)),
                   pl.BlockSpec((tm, d), lambda i, j: (i, 0)),
                   pl.BlockSpec((2, None, tm, fb), lambda i, j: (0, j, i, 0)),
                   pl.BlockSpec((None, tm, fb), lambda i, j: (j, i, 0))],
        out_shape=[jax.ShapeDtypeStruct((t, d), F32), jax.ShapeDtypeStruct((t, d), BF16),
                   jax.ShapeDtypeStruct((2, nj, t, fb), BF16), jax.ShapeDtypeStruct((nj, t, fb), BF16)],
        scratch_shapes=[pltpu.VMEM((tm, d), F32)],
        compiler_params=_cp("parallel", "arbitrary"), name="ffn_fwd")(x, nw, w_in, w_out)


def ffn_bwd(x, dy, nw, gu, w_in, w_out, tm=512, rc=256):
    t, d = x.shape
    nj, fb = w_out.shape[0], w_out.shape[1]
    rc = min(rc, tm)

    def body(x_ref, dy_ref, nw_ref, gu_ref, wi_ref, wo_ref,
             dx_ref, dnw_ref, dgu_ref, dyh_ref, acc_ref):
        i, j = pl.program_id(0), pl.program_id(1)

        @pl.when(j == 0)
        def _():
            dyh_ref[...] = (0.5 * dy_ref[...]).astype(BF16)
            acc_ref[...] = jnp.zeros_like(acc_ref)

        @pl.when((i == 0) & (j == 0))
        def _():
            dnw_ref[...] = jnp.zeros_like(dnw_ref)

        for c in range(tm // rc):
            r = slice(c * rc, (c + 1) * rc)
            g = gu_ref[0, r, :].astype(F32)
            u = gu_ref[1, r, :].astype(F32)
            sg = jax.nn.sigmoid(g)
            dh = _dot_nt(dyh_ref[r, :], wo_ref[...])
            dg = (dh * u * (sg * (1.0 + g * (1.0 - sg)))).astype(BF16)
            du = (dh * (g * sg)).astype(BF16)
            dgu_ref[0, r, :] = dg
            dgu_ref[1, r, :] = du
            acc_ref[r, :] += _dot_nt(dg, wi_ref[0]) + _dot_nt(du, wi_ref[1])

        @pl.when(j == nj - 1)
        def _():
            xf = x_ref[...]
            r = _rstd(xf)
            dxn = acc_ref[...]
            dnw_ref[...] += jnp.sum(dxn * xf * r, axis=0, keepdims=True)
            dx_ref[...] = _rms_bwd(xf, r, dxn * nw_ref[...]) + dy_ref[...]

    return pl.pallas_call(
        body, grid=(t // tm, nj),
        in_specs=[pl.BlockSpec((tm, d), lambda i, j: (i, 0)),
                  pl.BlockSpec((tm, d), lambda i, j: (i, 0)),
                  pl.BlockSpec((1, d), lambda i, j: (0, 0)),
                  pl.BlockSpec((2, None, tm, fb), lambda i, j: (0, j, i, 0)),
                  pl.BlockSpec((2, None, d, fb), lambda i, j: (0, j, 0, 0)),
                  pl.BlockSpec((None, fb, d), lambda i, j: (j, 0, 0))],
        out_specs=[pl.BlockSpec((tm, d), lambda i, j: (i, 0)),
                   pl.BlockSpec((1, d), lambda i, j: (0, 0)),
                   pl.BlockSpec((2, None, tm, fb), lambda i, j: (0, j, i, 0)),
                   pl.BlockSpec((tm, d), lambda i, j: (i, 0))],
        out_shape=[jax.ShapeDtypeStruct((t, d), F32),
                   jax.ShapeDtypeStruct((1, d), F32),
                   jax.ShapeDtypeStruct((2, nj, t, fb), BF16),
                   jax.ShapeDtypeStruct((t, d), BF16)],
        scratch_shapes=[pltpu.VMEM((tm, d), F32)],
        compiler_params=_cp("arbitrary", "arbitrary"), name="ffn_bwd")(x, dy, nw, gu, w_in, w_out)


def _wgrad_call(a, b, a_spec, b_spec, out_shape, out_spec, grid, name, out_dtype=BF16):
    last = len(grid) - 1
    acc_shape = tuple(s for s in out_spec.block_shape if s is not None)

    def body(a_ref, b_ref, o_ref, acc_ref):
        @pl.when(pl.program_id(last) == 0)
        def _():
            acc_ref[...] = jnp.zeros_like(acc_ref)

        acc_ref[...] += _dot_tn(a_ref[...], b_ref[...])

        @pl.when(pl.program_id(last) == grid[last] - 1)
        def _():
            o_ref[...] = acc_ref[...].astype(o_ref.dtype)

    sem = ("parallel",) * last + ("arbitrary",)
    return pl.pallas_call(body, grid=grid, in_specs=[a_spec, b_spec], out_specs=out_spec,
                          out_shape=jax.ShapeDtypeStruct(out_shape, out_dtype),
                          scratch_shapes=[pltpu.VMEM(acc_shape, F32)],
                          compiler_params=_cp(*sem), name=name)(a, b)


WGRAD_TM = 1024


def wgrad_ffn_in(xn, dgu, tm=WGRAD_TM):
    t, d = xn.shape
    _, nj, _, fb = dgu.shape
    tm = min(tm, t)
    return _wgrad_call(xn, dgu,
                       pl.BlockSpec((tm, d), lambda p, j, k: (k, 0)),
                       pl.BlockSpec((None, None, tm, fb), lambda p, j, k: (p, j, k, 0)),
                       (2, nj, d, fb), pl.BlockSpec((None, None, d, fb), lambda p, j, k: (p, j, 0, 0)),
                       (2, nj, t // tm), "wgrad_ffn_in")


def wgrad_ffn_out(h, dyh, tm=WGRAD_TM):
    nj, t, fb = h.shape
    d = dyh.shape[1]
    tm = min(tm, t)
    return _wgrad_call(h, dyh,
                       pl.BlockSpec((None, tm, fb), lambda j, k: (j, k, 0)),
                       pl.BlockSpec((tm, d), lambda j, k: (k, 0)),
                       (nj, fb, d), pl.BlockSpec((None, fb, d), lambda j, k: (j, 0, 0)),
                       (nj, t // tm), "wgrad_ffn_out")


def wgrad_2d(a, b, tk, name, out_dtype=BF16, tm=512):
    t, k = a.shape
    n = b.shape[1]
    return _wgrad_call(a, b,
                       pl.BlockSpec((tm, tk), lambda c, s: (s, c)),
                       pl.BlockSpec((tm, n), lambda c, s: (s, 0)),
                       (k, n), pl.BlockSpec((tk, n), lambda c, s: (c, 0)),
                       (k // tk, t // tm), name, out_dtype)


N_BIG = 7 * 512
N_PROJ = N_BIG + 128
COL_SMALL = N_BIG // 128


def inproj_fwd(x, nw, w, tm=256):
    t, d = x.shape
    n = w.shape[1]

    def body(x_ref, nw_ref, w_ref, p_ref, hn_ref):
        xf = x_ref[...]
        hn = (xf * _rstd(xf) * nw_ref[...]).astype(BF16)
        hn_ref[...] = hn
        p_ref[...] = _dot(hn, w_ref[...])

    return pl.pallas_call(
        body, grid=(t // tm,),
        in_specs=[pl.BlockSpec((tm, d), lambda i: (i, 0)), pl.BlockSpec((1, d), lambda i: (0, 0)),
                  pl.BlockSpec((d, n), lambda i: (0, 0))],
        out_specs=[pl.BlockSpec((tm, n), lambda i: (i, 0)), pl.BlockSpec((tm, d), lambda i: (i, 0))],
        out_shape=[jax.ShapeDtypeStruct((t, n), F32), jax.ShapeDtypeStruct((t, d), BF16)],
        compiler_params=_cp("parallel"), name="inproj_fwd")(x, nw, w)


def inproj_bwd(x, dres, nw, w, dparts, tm=256):
    t, d = x.shape
    n = w.shape[1]
    widths = [p.shape[1] for p in dparts]
    assert sum(widths) == n

    def body(x_ref, dres_ref, nw_ref, w_ref, *rest):
        part_refs, (dx_ref, dnw_ref, dp_ref) = rest[:len(widths)], rest[len(widths):]

        @pl.when(pl.program_id(0) == 0)
        def _():
            dnw_ref[...] = jnp.zeros_like(dnw_ref)

        dp = jnp.concatenate([r[...].astype(BF16) for r in part_refs], axis=1)
        dp_ref[...] = dp
        dhn = _dot_nt(dp, w_ref[...])
        xf = x_ref[...]
        r = _rstd(xf)
        dnw_ref[...] += jnp.sum(dhn * xf * r, axis=0, keepdims=True)
        dx_ref[...] = _rms_bwd(xf, r, dhn * nw_ref[...]) + dres_ref[...]

    return pl.pallas_call(
        body, grid=(t // tm,),
        in_specs=[pl.BlockSpec((tm, d), lambda i: (i, 0)), pl.BlockSpec((tm, d), lambda i: (i, 0)),
                  pl.BlockSpec((1, d), lambda i: (0, 0)), pl.BlockSpec((d, n), lambda i: (0, 0))]
                 + [pl.BlockSpec((tm, wd), lambda i: (i, 0)) for wd in widths],
        out_specs=[pl.BlockSpec((tm, d), lambda i: (i, 0)), pl.BlockSpec((1, d), lambda i: (0, 0)),
                   pl.BlockSpec((tm, n), lambda i: (i, 0))],
        out_shape=[jax.ShapeDtypeStruct((t, d), F32), jax.ShapeDtypeStruct((1, d), F32),
                   jax.ShapeDtypeStruct((t, n), BF16)],
        compiler_params=_cp("arbitrary"), name="inproj_bwd")(x, dres, nw, w, *dparts)


def outproj_fwd(x, yf, yg, w, tm=512):
    t, d = x.shape
    hw = yf.shape[1]

    def body(x_ref, yf_ref, yg_ref, w_ref, o_ref, y_ref):
        y = jnp.concatenate([yf_ref[...], yg_ref[...]], axis=1).astype(BF16)
        y_ref[...] = y
        o_ref[...] = x_ref[...] + _dot(y, w_ref[...])

    return pl.pallas_call(
        body, grid=(t // tm,),
        in_specs=[pl.BlockSpec((tm, d), lambda i: (i, 0)), pl.BlockSpec((tm, hw), lambda i: (i, 0)),
                  pl.BlockSpec((tm, hw), lambda i: (i, 0)), pl.BlockSpec((2 * hw, d), lambda i: (0, 0))],
        out_specs=[pl.BlockSpec((tm, d), lambda i: (i, 0)), pl.BlockSpec((tm, 2 * hw), lambda i: (i, 0))],
        out_shape=[jax.ShapeDtypeStruct((t, d), F32), jax.ShapeDtypeStruct((t, 2 * hw), BF16)],
        compiler_params=_cp("parallel"), name="outproj_fwd")(x, yf, yg, w)


def outproj_bwd(dy, w, after=None, tm=512):
    t, d = dy.shape
    hw = w.shape[0] // 2
    extra = [] if after is None else [after]

    def body(dy_ref, w_ref, *rest):
        df_ref, dg_ref, dyb_ref = rest[-3:]
        dyb = dy_ref[...].astype(BF16)
        dyb_ref[...] = dyb
        dyy = _dot_nt(dyb, w_ref[...])
        df_ref[...] = dyy[:, :hw]
        dg_ref[...] = dyy[:, hw:]

    return pl.pallas_call(
        body, grid=(t // tm,),
        in_specs=[pl.BlockSpec((tm, d), lambda i: (i, 0)), pl.BlockSpec((2 * hw, d), lambda i: (0, 0))]
                 + [pl.BlockSpec(memory_space=pl.ANY)] * len(extra),
        out_specs=[pl.BlockSpec((tm, hw), lambda i: (i, 0)), pl.BlockSpec((tm, hw), lambda i: (i, 0)),
                   pl.BlockSpec((tm, d), lambda i: (i, 0))],
        out_shape=[jax.ShapeDtypeStruct((t, hw), F32), jax.ShapeDtypeStruct((t, hw), F32),
                   jax.ShapeDtypeStruct((t, d), BF16)],
        compiler_params=_cp("parallel"), name="outproj_bwd")(dy, w, *extra)


def _lane(shape):
    return lax.broadcasted_iota(jnp.int32, shape, 1)


def _row(shape):
    return lax.broadcasted_iota(jnp.int32, shape, 0)


def _gate_terms(val, gp_ref):
    z = val + gp_ref[0:1, :]
    sp = jnp.log(1.0 + jnp.exp(-jnp.abs(z)))
    return z, sp


def gates_fwd(proj, gp, seq, ts=512):
    t = proj.shape[0]
    nb, ns = t // seq, seq // ts

    def body(v_ref, gp_ref, o_ref, carry_ref):
        @pl.when(pl.program_id(1) == 0)
        def _():
            carry_ref[...] = jnp.zeros_like(carry_ref)

        z, sp = _gate_terms(v_ref[...], gp_ref)
        logsig = jnp.minimum(z, 0.0) - sp
        tri = (_row((ts, ts)) >= _lane((ts, ts))).astype(F32)
        cum = jnp.dot(tri, logsig, precision=HIGHEST, preferred_element_type=F32) + carry_ref[0:1, :]
        carry_ref[0:1, :] = cum[ts - 1:ts, :]
        g = -jnp.exp(gp_ref[1:2, :]) * (jnp.maximum(z, 0.0) + sp)
        beta = jax.nn.sigmoid(z)
        lane = _lane((ts, 128))
        o_ref[...] = jnp.where(lane < 8, cum, jnp.where(lane < 12, g, jnp.where(lane < 16, beta, 0.0)))

    return pl.pallas_call(
        body, grid=(nb, ns),
        in_specs=[pl.BlockSpec((ts, 128), lambda b, s: (b * ns + s, COL_SMALL)),
                  pl.BlockSpec((8, 128), lambda b, s: (0, 0))],
        out_specs=pl.BlockSpec((ts, 128), lambda b, s: (b * ns + s, 0)),
        out_shape=jax.ShapeDtypeStruct((t, 128), F32),
        scratch_shapes=[pltpu.VMEM((8, 128), F32)],
        compiler_params=_cp("parallel", "arbitrary"), name="gates_fwd")(proj, gp)


def gates_bwd(proj, gp, dga, dgb, seq, ts=512):
    t = proj.shape[0]
    nb, ns = t // seq, seq // ts

    def body(v_ref, gp_ref, da_ref, db_ref, ds_ref, dgp_ref, carry_ref):
        @pl.when(pl.program_id(1) == 0)
        def _():
            carry_ref[...] = jnp.zeros_like(carry_ref)

        @pl.when((pl.program_id(0) == 0) & (pl.program_id(1) == 0))
        def _():
            dgp_ref[...] = jnp.zeros_like(dgp_ref)

        lane = _lane((ts, 128))
        dgate = jnp.where(lane < 8, da_ref[...], jnp.where(lane < 16, db_ref[...], 0.0))
        z, sp = _gate_terms(v_ref[...], gp_ref)
        triu = (_row((ts, ts)) <= _lane((ts, ts))).astype(F32)
        dlog = jnp.dot(triu, dgate, precision=HIGHEST, preferred_element_type=F32) + carry_ref[0:1, :]
        carry_ref[0:1, :] = dlog[0:1, :]
        sig = jax.nn.sigmoid(z)
        nea = -jnp.exp(gp_ref[1:2, :])
        g = nea * (jnp.maximum(z, 0.0) + sp)
        dz = jnp.where(lane < 8, dlog * (1.0 - sig),
                       jnp.where(lane < 12, dgate * nea * sig, dgate * sig * (1.0 - sig)))
        dz = jnp.where(lane < 16, dz, 0.0)
        ds_ref[...] = dz
        dgp_ref[0:1, :] += jnp.where(lane[0:1] < 12, jnp.sum(dz, axis=0, keepdims=True), 0.0)
        dgp_ref[1:2, :] += jnp.where((lane[0:1] >= 8) & (lane[0:1] < 12), jnp.sum(dgate * g, axis=0, keepdims=True), 0.0)

    rev = lambda b, s: (b * ns + (ns - 1 - s), 0)
    return pl.pallas_call(
        body, grid=(nb, ns),
        in_specs=[pl.BlockSpec((ts, 128), lambda b, s: (b * ns + (ns - 1 - s), COL_SMALL)),
                  pl.BlockSpec((8, 128), lambda b, s: (0, 0)),
                  pl.BlockSpec((ts, 128), rev), pl.BlockSpec((ts, 128), rev)],
        out_specs=[pl.BlockSpec((ts, 128), rev), pl.BlockSpec((8, 128), lambda b, s: (0, 0))],
        out_shape=[jax.ShapeDtypeStruct((t, 128), F32), jax.ShapeDtypeStruct((8, 128), F32)],
        scratch_shapes=[pltpu.VMEM((8, 128), F32)],
        compiler_params=_cp("arbitrary", "arbitrary"), name="gates_bwd")(proj, gp, dga, dgb)


NEG = -1e30
ATTN_TQ_FWD = 1024
ATTN_TQ_BWD = 512


def _pick_lane(tile, idx):
    return jnp.sum(jnp.where(_lane(tile.shape) == idx, tile, 0.0), axis=1, keepdims=True)


def _col_to_row(col, n):
    return jnp.sum(jnp.where(_row((n, n)) == _lane((n, n)), col, 0.0), axis=0, keepdims=True)


def _row_to_col(row, n):
    return jnp.sum(jnp.where(_row((n, n)) == _lane((n, n)), row, 0.0), axis=1, keepdims=True)


def _rows(i, n):
    return pl.ds(pl.multiple_of(i * n, n), n)


LOG2E = 1.4426950408889634
LN2 = 0.6931471805599453


def _split3(x):
    hi = x.astype(BF16).astype(F32)
    mid = (x - hi).astype(BF16).astype(F32)
    return [hi, mid, (x - hi - mid).astype(BF16).astype(F32)]


def _aug_cols(cols, n, width):
    lane = _lane((n, width))
    out = jnp.zeros((n, width), F32)
    for i, c in enumerate(cols):
        out = jnp.where(lane == i, c, out)
    return out


def _once(shape, index_map):
    return pl.BlockSpec(shape, index_map, pipeline_mode=pl.Buffered(1))


def attn_fwd(proj, gates, qw, kw, seq, tq=256):
    t = proj.shape[0]
    nb, nq, dh = t // seq, seq // tq, FOX_DH
    scale = dh ** -0.5

    def body(q_ref, k_ref, v_ref, g_ref, qw_ref, kw_ref, y_ref, lse_ref, qs, ks, vs):
        p = pl.program_id(1)
        heads = range(2)

        def prep(i, _):
            r = _rows(i, tq)
            for hh in heads:
                lanes = slice(hh * dh, (hh + 1) * dh)
                qf, kf = q_ref[r, lanes], k_ref[r, lanes]
                cc = _pick_lane(g_ref[r, :], 2 * p + hh) * LOG2E
                qs[hh, r, 0:dh] = (qf * _rstd(qf) * qw_ref[...] * (scale * LOG2E)).astype(BF16)
                qs[hh, r, dh:2 * dh] = _aug_cols(_split3(cc) + [1.0, 1.0, 1.0], tq, dh).astype(BF16)
                ks[hh, r, 0:dh] = (kf * _rstd(kf) * kw_ref[...]).astype(BF16)
                ks[hh, r, dh:2 * dh] = _aug_cols([1.0, 1.0, 1.0] + _split3(-cc), tq, dh).astype(BF16)
                vs[hh, r, :] = v_ref[r, lanes].astype(BF16)
            return 0

        lax.fori_loop(0, nq, prep, 0)

        def q_tile(i, _):
            r = _rows(i, tq)
            qt = [qs[hh, r, :] for hh in heads]

            def kv_step(j, carry, masked):
                kr = _rows(j, tq)
                out = []
                for hh in heads:
                    m, l, acc = carry[hh]
                    s = _dot_nt(qt[hh], ks[hh, kr, :])
                    if masked:
                        s = jnp.where(_row((tq, tq)) >= _lane((tq, tq)), s, NEG)
                    m_new = jnp.maximum(m, jnp.max(s, axis=1, keepdims=True))
                    pe = jnp.exp2(s - m_new)
                    a = jnp.exp2(m - m_new)
                    out.append((m_new, a * l + jnp.sum(pe, axis=1, keepdims=True),
                                a * acc + _dot(pe.astype(BF16), vs[hh, kr, :])))
                return tuple(out)

            one = (jnp.full((tq, 1), NEG, F32), jnp.zeros((tq, 1), F32), jnp.zeros((tq, dh), F32))
            carry = lax.fori_loop(0, i, lambda j, c: kv_step(j, c, False), (one, one))
            carry = kv_step(i, carry, True)
            for hh in heads:
                m, l, acc = carry[hh]
                lanes = slice(hh * dh, (hh + 1) * dh)
                y_ref[r, lanes] = acc / l
                lse_ref[r, lanes] = jnp.broadcast_to(m + jnp.log2(l), (tq, dh))
            return 0

        lax.fori_loop(0, nq, q_tile, 0)

    blk = lambda off: _once((seq, 128), lambda b, p: (b, off + p))
    return pl.pallas_call(
        body, grid=(nb, 4),
        in_specs=[blk(0), blk(4), blk(8), _once((seq, 128), lambda b, p: (b, 0)),
                  pl.BlockSpec((1, dh), lambda b, p: (0, 0)), pl.BlockSpec((1, dh), lambda b, p: (0, 0))],
        out_specs=[pl.BlockSpec((seq, 128), lambda b, p: (b, p)), pl.BlockSpec((seq, 128), lambda b, p: (b, p))],
        out_shape=[jax.ShapeDtypeStruct((t, 512), F32), jax.ShapeDtypeStruct((t, 512), F32)],
        scratch_shapes=[pltpu.VMEM((2, seq, 2 * dh), BF16), pltpu.VMEM((2, seq, 2 * dh), BF16),
                        pltpu.VMEM((2, seq, dh), BF16)],
        compiler_params=_cp("parallel", "arbitrary"), name="attn_fwd")(proj, proj, proj, gates, qw, kw)


def attn_bwd(proj, gates, qw, kw, y, lse, dy, seq, tq=256):
    t = proj.shape[0]
    nb, nq, dh = t // seq, seq // tq, FOX_DH
    scale = dh ** -0.5

    def body(q_ref, k_ref, v_ref, g_ref, qw_ref, kw_ref, y_ref, lse_ref, dy_ref,
             dq_ref, dk_ref, dv_ref, dg_ref, dqw_ref, dkw_ref,
             qs, ks, vs, dos, dsrow, dqa, dka):
        b, p = pl.program_id(0), pl.program_id(1)

        @pl.when((b == 0) & (p == 0))
        def _():
            dqw_ref[...] = jnp.zeros_like(dqw_ref)
            dkw_ref[...] = jnp.zeros_like(dkw_ref)

        @pl.when(p == 0)
        def _():
            dg_ref[...] = jnp.zeros_like(dg_ref)

        heads = range(2)
        hl = lambda hh: slice(hh * dh, (hh + 1) * dh)

        def prep(i, _):
            r = _rows(i, tq)
            for hh in heads:
                lanes = hl(hh)
                qf, kf = q_ref[r, lanes], k_ref[r, lanes]
                cc = _pick_lane(g_ref[r, :], 2 * p + hh) * LOG2E
                lse2 = lse_ref[r, hh * dh:hh * dh + 1]
                dyf = dy_ref[r, lanes]
                delta = jnp.sum(dyf * y_ref[r, lanes], axis=1, keepdims=True)
                qs[hh, r, 0:dh] = (qf * _rstd(qf) * qw_ref[...] * (scale * LOG2E)).astype(BF16)
                qs[hh, r, dh:2 * dh] = _aug_cols(_split3(cc) + [1.0, 1.0, 1.0] + _split3(-lse2), tq, dh).astype(BF16)
                ks[hh, r, 0:dh] = (kf * _rstd(kf) * kw_ref[...]).astype(BF16)
                ks[hh, r, dh:2 * dh] = _aug_cols([1.0, 1.0, 1.0] + _split3(-cc) + [1.0, 1.0, 1.0], tq, dh).astype(BF16)
                vs[hh, r, 0:dh] = v_ref[r, lanes].astype(BF16)
                vs[hh, r, dh:2 * dh] = _aug_cols([1.0, 1.0, 1.0], tq, dh).astype(BF16)
                dos[hh, r, 0:dh] = dyf.astype(BF16)
                dos[hh, r, dh:2 * dh] = _aug_cols(_split3(-delta), tq, dh).astype(BF16)
                dsrow[hh, r, :] = jnp.zeros((tq, 1), F32)
                dqa[hh, r, :] = jnp.zeros((tq, dh), F32)
            return 0

        lax.fori_loop(0, nq, prep, 0)

        def kv_tile(j, _):
            kr = _rows(j, tq)
            kt = [ks[hh, kr, :] for hh in heads]
            vt = [vs[hh, kr, :] for hh in heads]

            def q_step(i, carry, masked):
                r = _rows(i, tq)
                out = []
                for hh in heads:
                    dk, dv, dcr = carry[hh]
                    qt, dot = qs[hh, r, :], dos[hh, r, :]
                    s = _dot_nt(qt, kt[hh])
                    if masked:
                        s = jnp.where(_row((tq, tq)) >= _lane((tq, tq)), s, NEG)
                    pe = jnp.exp2(s)
                    ds = pe * _dot_nt(dot, vt[hh])
                    dsb = ds.astype(BF16)
                    dqa[hh, r, :] += _dot(dsb, kt[hh][:, 0:dh])
                    dsrow[hh, r, :] += jnp.sum(ds, axis=1, keepdims=True)
                    out.append((dk + _dot_tn(dsb, qt[:, 0:dh]), dv + _dot_tn(pe.astype(BF16), dot[:, 0:dh]),
                                dcr - jnp.sum(ds, axis=0, keepdims=True)))
                return tuple(out)

            one = (jnp.zeros((tq, dh), F32), jnp.zeros((tq, dh), F32), jnp.zeros((1, tq), F32))
            carry = q_step(j, (one, one), True)
            carry = lax.fori_loop(j + 1, nq, lambda i, c: q_step(i, c, False), carry)
            for hh in heads:
                dk, dv, dcr = carry[hh]
                dka[hh, kr, :] = dk
                dv_ref[kr, hl(hh)] = dv
                dg_ref[kr, :] = jnp.where(_lane((tq, 128)) == 2 * p + hh, _row_to_col(dcr, tq), dg_ref[kr, :])
            return 0

        lax.fori_loop(0, nq, kv_tile, 0)

        def post(i, _):
            r = _rows(i, tq)
            for hh in heads:
                lanes = hl(hh)
                qf, kf = q_ref[r, lanes], k_ref[r, lanes]
                rq, rk = _rstd(qf), _rstd(kf)
                dqn, dkn = dqa[hh, r, :] * scale, dka[hh, r, :] * LN2
                dqw_ref[...] += jnp.sum(dqn * qf * rq, axis=0, keepdims=True)
                dkw_ref[...] += jnp.sum(dkn * kf * rk, axis=0, keepdims=True)
                dq_ref[r, lanes] = _rms_bwd(qf, rq, dqn * qw_ref[...])
                dk_ref[r, lanes] = _rms_bwd(kf, rk, dkn * kw_ref[...])
                dg_ref[r, :] += jnp.where(_lane((tq, 128)) == 2 * p + hh, dsrow[hh, r, :], 0.0)
            return 0

        lax.fori_loop(0, nq, post, 0)

    blk = lambda off: _once((seq, 128), lambda b, p: (b, off + p))
    own = lambda: _once((seq, 128), lambda b, p: (b, p))
    vec = lambda: pl.BlockSpec((1, dh), lambda b, p: (0, 0))
    return pl.pallas_call(
        body, grid=(nb, 4),
        in_specs=[blk(0), blk(4), blk(8), _once((seq, 128), lambda b, p: (b, 0)), vec(), vec(), own(), own(), own()],
        out_specs=[own(), own(), own(), _once((seq, 128), lambda b, p: (b, 0)), vec(), vec()],
        out_shape=[jax.ShapeDtypeStruct((t, 512), F32)] * 3
                  + [jax.ShapeDtypeStruct((t, 128), F32), jax.ShapeDtypeStruct((1, dh), F32), jax.ShapeDtypeStruct((1, dh), F32)],
        scratch_shapes=[pltpu.VMEM((2, seq, 2 * dh), BF16)] * 4
                       + [pltpu.VMEM((2, seq, 1), F32), pltpu.VMEM((2, seq, dh), F32), pltpu.VMEM((2, seq, dh), F32)],
        compiler_params=_cp("arbitrary", "arbitrary"), name="attn_bwd")(proj, proj, proj, gates, qw, kw, y, lse, dy)


def _silu_grad(c, sg):
    return sg * (1.0 + c * (1.0 - sg))


def _conv(x, w, n):
    row = _row(x.shape)
    c = x * w[CONV_W - 1:CONV_W, :]
    for k in range(CONV_W - 1):
        sh = CONV_W - 1 - k
        c = c + w[k:k + 1, :] * jnp.where(row >= sh, pltpu.roll(x, sh, 0), 0.0)
    return c


def gdn_pre_fwd(proj, cw, seq):
    t = proj.shape[0]
    nb = t // seq
    scale = GDN_DH ** -0.5

    def body(xq_ref, xk_ref, xv_ref, wq_ref, wk_ref, wv_ref, q_ref, k_ref, v_ref):
        def act(x_ref, w_ref):
            c = _conv(x_ref[...], w_ref[...], seq)
            return c * jax.nn.sigmoid(c)

        aq, ak = act(xq_ref, wq_ref), act(xk_ref, wk_ref)
        q_ref[...] = aq * lax.rsqrt(jnp.sum(aq * aq, axis=1, keepdims=True) + EPS) * scale
        k_ref[...] = ak * lax.rsqrt(jnp.sum(ak * ak, axis=1, keepdims=True) + EPS)
        v_ref[...] = act(xv_ref, wv_ref)

    xb = lambda off: pl.BlockSpec((seq, 128), lambda b, h: (b, off + h))
    wb = lambda off: pl.BlockSpec((CONV_W, 128), lambda b, h: (0, off + h))
    ob = lambda: pl.BlockSpec((seq, 128), lambda b, h: (b, h))
    return pl.pallas_call(
        body, grid=(nb, GDN_HEADS),
        in_specs=[xb(12), xb(16), xb(20), wb(0), wb(4), wb(8)],
        out_specs=[ob(), ob(), ob()],
        out_shape=[jax.ShapeDtypeStruct((t, 512), F32)] * 3,
        compiler_params=_cp("parallel", "parallel"), name="gdn_pre_fwd")(proj, proj, proj, cw, cw, cw)


def gdn_pre_bwd(proj, cw, dq, dk, dv, seq):
    t = proj.shape[0]
    nb = t // seq
    scale = GDN_DH ** -0.5

    def body(xq_ref, xk_ref, xv_ref, wq_ref, wk_ref, wv_ref, dq_ref, dk_ref, dv_ref,
             dxq_ref, dxk_ref, dxv_ref, dwq_ref, dwk_ref, dwv_ref):
        first = pl.program_id(1) == 0
        row = _row((seq, 128))

        def one(x_ref, w_ref, dy_ref, dx_ref, dw_ref, norm, sc):
            x, w = x_ref[...], w_ref[...]
            c = _conv(x, w, seq)
            sg = jax.nn.sigmoid(c)
            dy = dy_ref[...]
            if norm:
                a = c * sg
                rs = lax.rsqrt(jnp.sum(a * a, axis=1, keepdims=True) + EPS)
                dy = dy * sc
                da = rs * dy - a * (rs * rs * rs) * jnp.sum(dy * a, axis=1, keepdims=True)
            else:
                da = dy
            dc = da * _silu_grad(c, sg)
            dx = dc * w[CONV_W - 1:CONV_W, :]
            dws = [None] * CONV_W
            dws[CONV_W - 1] = jnp.sum(dc * x, axis=0, keepdims=True)
            for k in range(CONV_W - 1):
                sh = CONV_W - 1 - k
                dx = dx + w[k:k + 1, :] * jnp.where(row < seq - sh, pltpu.roll(dc, seq - sh, 0), 0.0)
                dws[k] = jnp.sum(dc * jnp.where(row >= sh, pltpu.roll(x, sh, 0), 0.0), axis=0, keepdims=True)
            dx_ref[...] = dx
            dwn = jnp.concatenate(dws, axis=0)

            @pl.when(first)
            def _():
                dw_ref[...] = dwn

            @pl.when(jnp.logical_not(first))
            def _():
                dw_ref[...] += dwn

        one(xq_ref, wq_ref, dq_ref, dxq_ref, dwq_ref, True, scale)
        one(xk_ref, wk_ref, dk_ref, dxk_ref, dwk_ref, True, 1.0)
        one(xv_ref, wv_ref, dv_ref, dxv_ref, dwv_ref, False, 1.0)

    xb = lambda off: pl.BlockSpec((seq, 128), lambda h, b: (b, off + h))
    wb = lambda off: pl.BlockSpec((CONV_W, 128), lambda h, b: (0, off + h))
    ob = lambda: pl.BlockSpec((seq, 128), lambda h, b: (b, h))
    return pl.pallas_call(
        body, grid=(GDN_HEADS, nb),
        in_specs=[xb(12), xb(16), xb(20), wb(0), wb(4), wb(8), ob(), ob(), ob()],
        out_specs=[ob(), ob(), ob()] + [pl.BlockSpec((CONV_W, 128), lambda h, b: (0, h))] * 3,
        out_shape=[jax.ShapeDtypeStruct((t, 512), F32)] * 3 + [jax.ShapeDtypeStruct((CONV_W, 512), F32)] * 3,
        compiler_params=_cp("parallel", "arbitrary"), name="gdn_pre_bwd")(proj, proj, proj, cw, cw, cw, dq, dk, dv)


def _b16(x):
    return x.astype(BF16)


@jax.custom_vjp
def _mm(a, b):
    return _dot(_b16(a), _b16(b))


_mm.defvjp(lambda a, b: (_mm(a, b), (a, b)),
           lambda res, g: (_dot_nt(_b16(g), _b16(res[1])), _dot_tn(_b16(res[0]), _b16(g))))


@jax.custom_vjp
def _mm_nt(a, b):
    return _dot_nt(_b16(a), _b16(b))


_mm_nt.defvjp(lambda a, b: (_mm_nt(a, b), (a, b)),
              lambda res, g: (_dot(_b16(g), _b16(res[1])), _dot_tn(_b16(g), _b16(res[0]))))


@jax.custom_vjp
def _mm_tn(a, b):
    return _dot_tn(_b16(a), _b16(b))


_mm_tn.defvjp(lambda a, b: (_mm_tn(a, b), (a, b)),
              lambda res, g: (_dot_nt(_b16(res[1]), _b16(g)), _dot(_b16(res[0]), _b16(g))))


def _dot32(a, b, dims=(((1,), (0,)), ((), ()))):
    def split(x):
        hi = x.astype(BF16)
        return hi, (x - hi.astype(F32)).astype(BF16)

    (ah, al), (bh, bl) = split(a), split(b)
    d = lambda x, y: lax.dot_general(x, y, dims, preferred_element_type=F32)
    return d(ah, bh) + (d(ah, bl) + d(al, bh))


def _inv_fwd_many(mats):
    n = mats[0].shape[0]
    eye = (_row((n, n)) == _lane((n, n))).astype(F32)
    invs, pws = [eye - a for a in mats], list(mats)
    for _ in range(n.bit_length() - 2):
        pws = [_dot32(pw, pw) for pw in pws]
        invs = [inv + _dot32(inv, pw) for inv, pw in zip(invs, pws)]
    return invs


@jax.custom_vjp
def _inv_saved(a, inv):
    return inv


def _inv_saved_bwd(inv, g):
    tg = _dot32(inv, g, (((0,), (0,)), ((), ())))
    return -_dot32(tg, inv, (((1,), (1,)), ((), ()))), jnp.zeros_like(inv)


_inv_saved.defvjp(lambda a, inv: (inv, inv), _inv_saved_bwd)


def _gdn_decay(gcol):
    c = CHUNK
    ri, ci = _row((c, c)), _lane((c, c))
    incl, eye = ri >= ci, ri == ci
    grow = jnp.sum(jnp.where(eye, gcol, 0.0), axis=0, keepdims=True)
    gc = jnp.sum(jnp.where(incl, grow, 0.0), axis=1, keepdims=True)
    gcr = jnp.sum(jnp.where(eye, gc, 0.0), axis=0, keepdims=True)
    gl = jnp.sum(jnp.where(_row((c, 1)) == c - 1, gc, 0.0), axis=0, keepdims=True)
    return gc, gl, jnp.exp(jnp.where(incl, gc - gcr, NEG))


def _gdn_a(k, bcol, decay):
    c = CHUNK
    return jnp.where(_row((c, c)) > _lane((c, c)), _mm_nt(k * bcol, k) * decay, 0.0)


def _gdn_chunk(q, k, v, gcol, bcol, state, gg, nw, inv_saved):
    c = CHUNK
    incl = _row((c, c)) >= _lane((c, c))
    gc, gl, decay = _gdn_decay(gcol)
    kb, vb = k * bcol, v * bcol
    inv = _inv_saved(_gdn_a(k, bcol, decay), inv_saved)
    eg = jnp.exp(gc)
    u = _mm(inv, vb)
    w = _mm(inv, kb * eg)
    pm = jnp.where(incl, _mm_nt(q, k) * decay, 0.0)
    kd = k * jnp.exp(gl - gc)
    qd = q * eg
    v_new = u - _mm(w, state)
    o = _mm(qd, state) + _mm(pm, v_new)
    state_new = state * jnp.exp(gl) + _mm_tn(kd, v_new)
    y = o * _rstd(o) * nw * (gg * jax.nn.sigmoid(gg))
    return y, state_new


_gdn_chunks = jax.vmap(_gdn_chunk, in_axes=(0, 0, 0, 0, 0, 0, 0, None, 0))


def _gdn_chain_inputs(chains, p, r, c, q_ref, k_ref, v_ref, g_ref, gg_ref, inv_ref):
    cols = {nm: [] for nm in ("q", "k", "v", "g", "b", "gg", "inv")}
    for b, hh in chains:
        h = GDN_HPS * p + hh
        ln = slice(hh * 128, (hh + 1) * 128)
        gt = g_ref[b, r, :]
        cols["q"].append(q_ref[b, r, ln])
        cols["k"].append(k_ref[b, r, ln])
        cols["v"].append(v_ref[b, r, ln])
        cols["g"].append(_pick_lane(gt, 8 + h))
        cols["b"].append(_pick_lane(gt, 12 + h))
        cols["gg"].append(gg_ref[b, r, ln])
        cols["inv"].append(inv_ref[b, hh, c])
    return [jnp.stack(cols[nm]) for nm in ("q", "k", "v", "g", "b", "gg", "inv")]


GDN_CB = 8
GDN_HPS = 4


def gdn_inv(k, gates, seq):
    t = k.shape[0]
    nb, nc = t // seq, seq // CHUNK
    rb = GDN_CB * CHUNK
    nsb = seq // rb

    def body(k_ref, g_ref, o_ref):
        h = pl.program_id(1)
        mats = []
        for c in range(GDN_CB):
            r = slice(c * CHUNK, (c + 1) * CHUNK)
            gt = g_ref[r, :]
            _, _, decay = _gdn_decay(_pick_lane(gt, 8 + h))
            mats.append(_gdn_a(k_ref[r, :], _pick_lane(gt, 12 + h), decay))
        for c, inv in enumerate(_inv_fwd_many(mats)):
            o_ref[c] = inv

    return pl.pallas_call(
        body, grid=(nb, GDN_HEADS, nsb),
        in_specs=[pl.BlockSpec((rb, 128), lambda b, h, s: (b * nsb + s, h)),
                  pl.BlockSpec((rb, 128), lambda b, h, s: (b * nsb + s, 0))],
        out_specs=pl.BlockSpec((None, None, GDN_CB, CHUNK, CHUNK), lambda b, h, s: (b, h, s, 0, 0)),
        out_shape=jax.ShapeDtypeStruct((nb, GDN_HEADS, nc, CHUNK, CHUNK), F32),
        compiler_params=_cp("parallel", "parallel", "parallel"), name="gdn_inv")(k, gates)


def _gdn_specs(nb, nsb, cb, rev):
    blk = (lambda s: nsb - 1 - s) if rev else (lambda s: s)
    rb = cb * CHUNK
    pair = lambda off=0: pl.BlockSpec((nb, rb, 128 * GDN_HPS), lambda s, p: (0, blk(s), off + p))
    gate = lambda: pl.BlockSpec((nb, rb, 128), lambda s, p: (0, blk(s), 0))
    mats = lambda n: pl.BlockSpec((nb, GDN_HPS, cb, n, n), lambda s, p: (0, p, blk(s), 0, 0))
    return pair, gate, mats


def gdn_fwd(q, k, v, gates, proj, nw, inv, seq):
    t = q.shape[0]
    nb, nc = t // seq, seq // CHUNK
    cb = GDN_CB
    nsb = nc // cb
    chains = [(b, hh) for b in range(nb) for hh in range(GDN_HPS)]
    nch = len(chains)
    pair, gate, mats = _gdn_specs(nb, nsb, cb, False)

    def body(q_ref, k_ref, v_ref, g_ref, gg_ref, inv_ref, nw_ref, y_ref, st_ref, carry):
        s, p = pl.program_id(0), pl.program_id(1)

        @pl.when(s == 0)
        def _():
            for ci in range(nch):
                carry[p * nch + ci] = jnp.zeros((GDN_DH, GDN_DH), F32)

        def step(c, states):
            r = _rows(c, CHUNK)
            for ci, (b, hh) in enumerate(chains):
                st_ref[b, hh, c] = states[ci]
            ins = _gdn_chain_inputs(chains, p, r, c, q_ref, k_ref, v_ref, g_ref, gg_ref, inv_ref)
            y, states = _gdn_chunks(*ins[:5], states, ins[5], nw_ref[...], ins[6])
            for ci, (b, hh) in enumerate(chains):
                y_ref[b, r, hh * 128:(hh + 1) * 128] = y[ci]
            return states

        states = lax.fori_loop(0, cb, step, jnp.stack([carry[p * nch + ci] for ci in range(nch)]))
        for ci in range(nch):
            carry[p * nch + ci] = states[ci]

    v3 = lambda a: a.reshape(nb, seq, a.shape[1])
    y, st = pl.pallas_call(
        body, grid=(nsb, GDN_HEADS // GDN_HPS),
        in_specs=[pair(), pair(), pair(), gate(), pair(24 // GDN_HPS), mats(CHUNK), pl.BlockSpec((1, 128), lambda s, p: (0, 0))],
        out_specs=[pair(), mats(GDN_DH)],
        out_shape=[jax.ShapeDtypeStruct((nb, seq, 512), F32),
                   jax.ShapeDtypeStruct((nb, GDN_HEADS, nc, GDN_DH, GDN_DH), F32)],
        scratch_shapes=[pltpu.VMEM((GDN_HEADS // GDN_HPS * nch, GDN_DH, GDN_DH), F32)],
        compiler_params=_cp("arbitrary", "arbitrary"), name="gdn_fwd")(v3(q), v3(k), v3(v), v3(gates), v3(proj), inv, nw)
    return y.reshape(t, 512), st


def gdn_bwd(q, k, v, gates, proj, nw, inv, states, dy, seq):
    t = q.shape[0]
    nb, nc = t // seq, seq // CHUNK
    cb = GDN_CB // 2
    nsb = nc // cb
    chains = [(b, hh) for b in range(nb) for hh in range(GDN_HPS)]
    nch = len(chains)
    pair, gate, mats = _gdn_specs(nb, nsb, cb, True)

    def body(q_ref, k_ref, v_ref, g_ref, gg_ref, inv_ref, st_ref, dy_ref, nw_ref,
             dq_ref, dk_ref, dv_ref, dgg_ref, dg_ref, dnw_ref, carry):
        s, p = pl.program_id(0), pl.program_id(1)

        @pl.when((s == 0) & (p == 0))
        def _():
            dnw_ref[...] = jnp.zeros_like(dnw_ref)

        @pl.when(p == 0)
        def _():
            dg_ref[...] = jnp.zeros_like(dg_ref)

        @pl.when(s == 0)
        def _():
            for ci in range(nch):
                carry[p * nch + ci] = jnp.zeros((GDN_DH, GDN_DH), F32)

        def step(idx, dstates):
            c = cb - 1 - idx
            r = _rows(c, CHUNK)
            ins = _gdn_chain_inputs(chains, p, r, c, q_ref, k_ref, v_ref, g_ref, gg_ref, inv_ref)
            st = jnp.stack([st_ref[b, hh, c] for b, hh in chains])
            dy = jnp.stack([dy_ref[b, r, hh * 128:(hh + 1) * 128] for b, hh in chains])
            _, vjp = jax.vjp(_gdn_chunks, *ins[:5], st, ins[5], nw_ref[...], ins[6])
            dq, dk, dv, dgc, dbc, dstates, dgg, dnw, _ = vjp((dy, dstates))
            dnw_ref[...] += dnw
            lane = _lane((CHUNK, 128))
            for ci, (b, hh) in enumerate(chains):
                h = GDN_HPS * p + hh
                ln = slice(hh * 128, (hh + 1) * 128)
                dq_ref[b, r, ln] = dq[ci]
                dk_ref[b, r, ln] = dk[ci]
                dv_ref[b, r, ln] = dv[ci]
                dgg_ref[b, r, ln] = dgg[ci]
                dg_ref[b, r, :] = jnp.where(lane == 8 + h, dgc[ci], jnp.where(lane == 12 + h, dbc[ci], dg_ref[b, r, :]))
            return dstates

        dstates = lax.fori_loop(0, cb, step, jnp.stack([carry[p * nch + ci] for ci in range(nch)]))
        for ci in range(nch):
            carry[p * nch + ci] = dstates[ci]

    v3 = lambda a: a.reshape(nb, seq, a.shape[1])
    res = pl.pallas_call(
        body, grid=(nsb, GDN_HEADS // GDN_HPS),
        in_specs=[pair(), pair(), pair(), gate(), pair(24 // GDN_HPS), mats(CHUNK), mats(GDN_DH), pair(),
                  pl.BlockSpec((1, 128), lambda s, p: (0, 0))],
        out_specs=[pair(), pair(), pair(), pair(), gate(), pl.BlockSpec((1, 128), lambda s, p: (0, 0))],
        out_shape=[jax.ShapeDtypeStruct((nb, seq, 512), F32)] * 4
                  + [jax.ShapeDtypeStruct((nb, seq, 128), F32), jax.ShapeDtypeStruct((1, 128), F32)],
        scratch_shapes=[pltpu.VMEM((GDN_HEADS // GDN_HPS * nch, GDN_DH, GDN_DH), F32)],
        compiler_params=_cp("arbitrary", "arbitrary"),
        name="gdn_bwd")(v3(q), v3(k), v3(v), v3(gates), v3(proj), inv, states, v3(dy), nw)
    return [a.reshape(t, a.shape[2]) for a in res[:5]] + [res[5]]


def loss_head(y, target, tm=512):
    t, d = y.shape

    def body(y_ref, t_ref, s_ref, dy_ref):
        @pl.when(pl.program_id(0) == 0)
        def _():
            s_ref[...] = jnp.zeros_like(s_ref)

        err = y_ref[...] - t_ref[...]
        s_ref[...] += jnp.sum(err * err, axis=0, keepdims=True)
        dy_ref[...] = err * (1.0 / d)

    return pl.pallas_call(
        body, grid=(t // tm,),
        in_specs=[pl.BlockSpec((tm, d), lambda i: (i, 0)), pl.BlockSpec((tm, d), lambda i: (i, 0))],
        out_specs=[pl.BlockSpec((1, d), lambda i: (0, 0)), pl.BlockSpec((tm, d), lambda i: (i, 0))],
        out_shape=[jax.ShapeDtypeStruct((1, d), F32), jax.ShapeDtypeStruct((t, d), F32)],
        compiler_params=_cp("arbitrary"), name="loss_head")(y, target)


def _place():
    return lax.axis_index("x"), lax.axis_index("y"), lax.axis_index("c")


def _peer(k):
    x, y, c = _place()
    px = 1 - x if (k >> 2) & 1 else x
    py = 1 - y if (k >> 1) & 1 else y
    pc = 1 - c if k & 1 else c
    return (px, py, pc), 4 * px + 2 * py + pc


_ANY = pl.BlockSpec(memory_space=pl.ANY)
_SEM = pl.BlockSpec(memory_space=pltpu.SEMAPHORE)
_EFFECT = pltpu.SideEffectType.DATAFLOW_SIDE_EFFECTING


def _me():
    x, y, c = _place()
    return 4 * x + 2 * y + c


def _remote_copy(ins, lands, scatter, send_sems, recv_sems, a, k, arriving):
    pid, pidx = _peer(k)
    return pltpu.make_async_remote_copy(src_ref=ins[a].at[pidx] if scatter[a] else ins[a],
                                        dst_ref=lands[a].at[pidx if arriving else _me()],
                                        send_sem=send_sems.at[a * N_DEV + k], recv_sem=recv_sems.at[a * N_DEV + k],
                                        device_id=pid, device_id_type=MESH)


def _local_copy(ins, lands, scatter, loc_sems, a):
    me = _me()
    return pltpu.make_async_copy(ins[a].at[me] if scatter[a] else ins[a], lands[a].at[me], loc_sems.at[a])


def exchange_start(arrays, scatter, name, after):
    n = len(arrays)
    lands = [lax.empty(a.shape if s else (N_DEV,) + a.shape, a.dtype) for a, s in zip(arrays, scatter)]

    def body(*refs):
        ins, lds = refs[:n], refs[n:2 * n]
        send_sems, recv_sems, loc_sems = refs[2 * n + 1:2 * n + 4]
        token = refs[-1]
        for k in range(1, N_DEV):
            for a in range(n):
                _remote_copy(ins, lds, scatter, send_sems, recv_sems, a, k, False).start()
        for a in range(n):
            _local_copy(ins, lds, scatter, loc_sems, a).start()
        token[...] = jnp.zeros_like(token)

    hbm = lambda a: pltpu.HBM(a.shape, a.dtype)
    res = pl.pallas_call(
        body, name=name,
        in_specs=[_ANY] * (2 * n + 1),
        out_specs=[_SEM, _SEM, _SEM] + [_ANY] * (2 * n) + [pl.BlockSpec(memory_space=pltpu.VMEM)],
        out_shape=[pltpu.SemaphoreType.DMA((n * N_DEV,)), pltpu.SemaphoreType.DMA((n * N_DEV,)),
                   pltpu.SemaphoreType.DMA((n,))]
                  + [hbm(a) for a in arrays] + [hbm(a) for a in lands] + [jax.ShapeDtypeStruct((8, 128), F32)],
        input_output_aliases={i: 3 + i for i in range(2 * n)},
        compiler_params=pltpu.CompilerParams(has_side_effects=_EFFECT),
    )(*[pltpu.with_memory_space_constraint(a, pltpu.HBM) for a in list(arrays) + lands], after)
    return res[0:3], res[3:3 + n], res[3 + n:3 + 2 * n], res[-1]


def exchange_wait(sems, arrays, lands, scatter, after, name):
    n = len(arrays)

    def body(*refs):
        ins, lds = refs[:n], refs[n:2 * n]
        ssem, rsem, lsem = refs[2 * n:2 * n + 3]
        for a in range(n):
            _local_copy(ins, lds, scatter, lsem, a).wait()
        for k in range(1, N_DEV):
            for a in range(n):
                _remote_copy(ins, lds, scatter, ssem, rsem, a, k, True).wait_recv()
        for k in range(1, N_DEV):
            for a in range(n):
                _remote_copy(ins, lds, scatter, ssem, rsem, a, k, False).wait_send()

    hbm = lambda a: pltpu.HBM(a.shape, a.dtype)
    res = pl.pallas_call(
        body, name=name,
        in_specs=[_ANY] * (2 * n) + [_SEM, _SEM, _SEM, _ANY],
        out_specs=[_ANY] * (2 * n),
        out_shape=[hbm(a) for a in arrays] + [hbm(a) for a in lands],
        input_output_aliases={i: i for i in range(2 * n)},
        compiler_params=pltpu.CompilerParams(has_side_effects=_EFFECT),
    )(*arrays, *lands, *sems, after)
    return list(res[n:])


def exchange_begin(arrays, scatter, name, after):
    sems, arrays_thru, lands_thru, token = exchange_start(arrays, scatter, name + "_start", after)
    return (sems, arrays_thru, lands_thru, scatter, name), token


def exchange_end(state, after):
    sems, arrays_thru, lands_thru, scatter, name = state
    return exchange_wait(sems, arrays_thru, lands_thru, scatter, after, name + "_wait")


def adamw_reduce(slots, w, m, v, l, name, after=None, prev=None):
    nl, r, c = w.shape
    tr = r
    while tr * c * 4 > (1 << 20) and tr % 16 == 0:
        tr //= 2
    bc1 = 1.0 - ADAM_B1 ** ADAM_STEP
    bc2 = 1.0 - ADAM_B2 ** ADAM_STEP

    def body(s_ref, w_ref, m_ref, v_ref, *rest):
        g_ref, d_ref, nm_ref, nv_ref = rest[-4:]
        g = s_ref[0].astype(F32)
        for j in range(1, N_DEV):
            g = g + s_ref[j].astype(F32)
        nm = ADAM_B1 * m_ref[...] + (1.0 - ADAM_B1) * g
        nv = ADAM_B2 * v_ref[...] + (1.0 - ADAM_B2) * (g * g)
        g_ref[...] = g
        nm_ref[...] = nm
        nv_ref[...] = nv
        d_ref[...] = -ADAM_LR * ((nm / bc1) / (jnp.sqrt(nv / bc2) + ADAM_EPS) + ADAM_WD * w_ref[...])

    blk = lambda: pl.BlockSpec((None, tr, c), lambda i: (l, i, 0))
    extra = ([] if after is None else [after]) + ([] if prev is None else list(prev))
    first_prev = 4 + (after is not None)
    return pl.pallas_call(
        body, grid=(r // tr,),
        in_specs=[pl.BlockSpec((N_DEV, tr, c), lambda i: (0, i, 0)), blk(), blk(), blk()] + [_ANY] * len(extra),
        out_specs=[blk(), blk(), blk(), blk()],
        out_shape=[jax.ShapeDtypeStruct((nl, r, c), F32)] * 4,
        input_output_aliases={} if prev is None else {first_prev + j: j for j in range(4)},
        compiler_params=_cp("parallel"), name=name)(slots, w, m, v, *extra)


BIG = ("ffn1_w_in", "ffn1_w_out", "w_in", "gdn_conv", "w_out", "ffn2_w_in", "ffn2_w_out")
GROUPS = (BIG[0:2], BIG[2:5], BIG[5:7])
SMALL = ("ffn1_norm", "mix_norm", "fox_q_norm", "fox_k_norm", "fox_f_bias", "gdn_a_log", "gdn_dt_bias",
         "gdn_out_norm", "ffn2_norm")
WEIGHTS = ("ffn1_norm", "ffn1_w_in", "ffn1_w_out", "mix_norm", "w_in", "fox_q_norm", "fox_k_norm", "fox_f_bias",
           "gdn_conv", "gdn_a_log", "gdn_dt_bias", "gdn_out_norm", "w_out", "ffn2_norm", "ffn2_w_in", "ffn2_w_out")
IN_COLS = (("fq", 512), ("fk", 512), ("fv", 512), ("ff", 8), ("gq", 512), ("gk", 512), ("gv", 512),
           ("ga", 4), ("gb", 4), ("gg", 512))
MY_BIG = ("fq", "fk", "fv", "gq", "gk", "gv", "gg")
MY_SMALL = ("ff", "ga", "gb")
SMALL_ROWS = 8 * 128


def _in_cols_to_mine(w):
    off, parts = 0, {}
    for nm, wd in IN_COLS:
        parts[nm] = w[:, off:off + wd]
        off += wd
    small = jnp.concatenate([parts[nm] for nm in MY_SMALL], axis=1)
    small = jnp.pad(small, ((0, 0), (0, 128 - small.shape[1])))
    return jnp.concatenate([parts[nm] for nm in MY_BIG] + [small], axis=1)


def _in_cols_from_mine(g):
    parts = {nm: g[:, i * 512:(i + 1) * 512] for i, nm in enumerate(MY_BIG)}
    off = N_BIG
    for nm in MY_SMALL:
        wd = dict(IN_COLS)[nm]
        parts[nm] = g[:, off:off + wd]
        off += wd
    return jnp.concatenate([parts[nm] for nm, _ in IN_COLS], axis=1)


def _pack_small(vals):
    rows = []
    nl = vals[SMALL[0]].shape[0]
    for l in range(nl):
        for nm in SMALL:
            v = vals[nm][l].reshape(-1)
            pad = (-v.shape[0]) % SMALL_ROWS
            rows.append(jnp.pad(v, (0, pad)).reshape(-1, 128))
    return jnp.concatenate(rows, axis=0)


def _unpack_small(packed, like):
    out = {nm: [] for nm in SMALL}
    row = 0
    nl = like[SMALL[0]].shape[0]
    for l in range(nl):
        for nm in SMALL:
            n = like[nm].shape[1]
            nr = -(-n // SMALL_ROWS) * 8
            out[nm].append(packed[row:row + nr].reshape(-1)[:n])
            row += nr
    return {nm: jnp.stack(v) for nm, v in out.items()}


def kernel(x, ffn1_norm, ffn1_w_in, ffn1_w_out, mix_norm, w_in, fox_q_norm, fox_k_norm, fox_f_bias, gdn_conv, gdn_a_log, gdn_dt_bias, gdn_out_norm, w_out, ffn2_norm, ffn2_w_in, ffn2_w_out, loss_target, m_ffn1_norm, m_ffn1_w_in, m_ffn1_w_out, m_mix_norm, m_w_in, m_fox_q_norm, m_fox_k_norm, m_fox_f_bias, m_gdn_conv, m_gdn_a_log, m_gdn_dt_bias, m_gdn_out_norm, m_w_out, m_ffn2_norm, m_ffn2_w_in, m_ffn2_w_out, v_ffn1_norm, v_ffn1_w_in, v_ffn1_w_out, v_mix_norm, v_w_in, v_fox_q_norm, v_fox_k_norm, v_fox_f_bias, v_gdn_conv, v_gdn_a_log, v_gdn_dt_bias, v_gdn_out_norm, v_w_out, v_ffn2_norm, v_ffn2_w_in, v_ffn2_w_out):
    wts = dict(ffn1_norm=ffn1_norm, ffn1_w_in=ffn1_w_in, ffn1_w_out=ffn1_w_out, mix_norm=mix_norm, w_in=w_in,
               fox_q_norm=fox_q_norm, fox_k_norm=fox_k_norm, fox_f_bias=fox_f_bias, gdn_conv=gdn_conv,
               gdn_a_log=gdn_a_log, gdn_dt_bias=gdn_dt_bias, gdn_out_norm=gdn_out_norm, w_out=w_out,
               ffn2_norm=ffn2_norm, ffn2_w_in=ffn2_w_in, ffn2_w_out=ffn2_w_out)
    mom = dict(ffn1_norm=m_ffn1_norm, ffn1_w_in=m_ffn1_w_in, ffn1_w_out=m_ffn1_w_out, mix_norm=m_mix_norm, w_in=m_w_in,
               fox_q_norm=m_fox_q_norm, fox_k_norm=m_fox_k_norm, fox_f_bias=m_fox_f_bias, gdn_conv=m_gdn_conv,
               gdn_a_log=m_gdn_a_log, gdn_dt_bias=m_gdn_dt_bias, gdn_out_norm=m_gdn_out_norm, w_out=m_w_out,
               ffn2_norm=m_ffn2_norm, ffn2_w_in=m_ffn2_w_in, ffn2_w_out=m_ffn2_w_out)
    var = dict(ffn1_norm=v_ffn1_norm, ffn1_w_in=v_ffn1_w_in, ffn1_w_out=v_ffn1_w_out, mix_norm=v_mix_norm, w_in=v_w_in,
               fox_q_norm=v_fox_q_norm, fox_k_norm=v_fox_k_norm, fox_f_bias=v_fox_f_bias, gdn_conv=v_gdn_conv,
               gdn_a_log=v_gdn_a_log, gdn_dt_bias=v_gdn_dt_bias, gdn_out_norm=v_gdn_out_norm, w_out=v_w_out,
               ffn2_norm=v_ffn2_norm, ffn2_w_in=v_ffn2_w_in, ffn2_w_out=v_ffn2_w_out)
    nb, seq, d = x.shape
    t = nb * seq
    depth = ffn1_norm.shape[0]

    stages = [(l, gi) for l in range(depth) for gi in range(len(GROUPS))]

    def shards_of(l, gi):
        return [wts[nm][l] if nm == "gdn_conv" else wts[nm][l].astype(BF16) for nm in GROUPS[gi]]

    def behind(nw, token):
        return nw if token is None else nw + token[0:1, 0:1]

    def small_params(l):
        return dict(
            n1=ffn1_norm[l][None], nmix=mix_norm[l][None], n2=ffn2_norm[l][None],
            qw=fox_q_norm[l][None], kw=fox_k_norm[l][None], onw=gdn_out_norm[l][None],
            gp=jnp.concatenate([
                jnp.concatenate([fox_f_bias[l], gdn_dt_bias[l], jnp.zeros((116,), F32)])[None],
                jnp.concatenate([jnp.zeros((8,), F32), gdn_a_log[l], jnp.zeros((116,), F32)])[None],
                jnp.zeros((6, 128), F32)], axis=0))

    h = x.reshape(t, d)
    state, token = exchange_begin(shards_of(0, 0), [False] * len(GROUPS[0]), "gather_0", ffn1_norm)
    landed = exchange_end(state, token)
    saved = [dict(p=small_params(l)) for l in range(depth)]
    for k, (l, gi) in enumerate(stages):
        s, w, token = saved[l], landed, None
        p = s["p"]
        if k + 1 < len(stages):
            nl, ng = stages[k + 1]
            state, token = exchange_begin(shards_of(nl, ng), [False] * len(GROUPS[ng]), f"gather_{k + 1}", landed[0])
        if gi == 0:
            fb = w[0].shape[2]
            p["w1i"], p["w1o"] = w[0].reshape(2, 4, d, fb), w[1].reshape(4, fb, d)
            s["x0"] = h
            h, *s["ffn1"] = ffn_fwd(h, behind(p["n1"], token), p["w1i"], p["w1o"])
            s["x1"] = h
        elif gi == 1:
            p["wi"] = _in_cols_to_mine(w[0].transpose(1, 0, 2).reshape(d, -1))
            p["cw"] = w[1].transpose(1, 0, 2).reshape(CONV_W, -1)
            p["wo"] = w[2].reshape(d, d)
            proj, hn = inproj_fwd(h, behind(p["nmix"], token), p["wi"])
            gates = gates_fwd(proj, p["gp"], seq)
            yf, lse = attn_fwd(proj, gates, p["qw"], p["kw"], seq, tq=min(seq, ATTN_TQ_FWD))
            qh, kh, vh = gdn_pre_fwd(proj, p["cw"], seq)
            inv = gdn_inv(kh, gates, seq)
            yg, st = gdn_fwd(qh, kh, vh, gates, proj, p["onw"], inv, seq)
            h, ycat = outproj_fwd(h, yf, yg, p["wo"])
            s.update(x2=h, proj=proj, hn=hn, gates=gates, yf=yf, lse=lse, qh=qh, kh=kh, vh=vh, st=st, inv=inv, ycat=ycat)
        else:
            fb = w[0].shape[2]
            p["w2i"], p["w2o"] = w[0].reshape(2, 4, d, fb), w[1].reshape(4, fb, d)
            h, *s["ffn2"] = ffn_fwd(h, behind(p["n2"], token), p["w2i"], p["w2o"])
        if k + 1 < len(stages):
            landed = exchange_end(state, h)

    sq, dh = loss_head(h, loss_target.reshape(t, d))
    loss = lax.psum(0.5 * jnp.sum(sq) / d, ("x", "y", "c"))

    got = [None] * len(stages)
    pending, token = None, None
    gsmall = {nm: [None] * depth for nm in SMALL}
    for k in reversed(range(len(stages))):
        l, gi = stages[k]
        s = saved[l]
        p = s["p"]
        if gi != 1:
            nw, xin, wi_, wo_, nm_n, (xn, gu, hh) = (
                (p["n1"], s["x0"], p["w1i"], p["w1o"], "ffn1_norm", s["ffn1"]) if gi == 0 else
                (p["n2"], s["x2"], p["w2i"], p["w2o"], "ffn2_norm", s["ffn2"]))
            dh, dn, dgu, dyh = ffn_bwd(xin, dh, behind(nw, token), gu, wi_, wo_)
            g_in, g_out = wgrad_ffn_in(xn, dgu), wgrad_ffn_out(hh, dyh)
            send = [g_in.reshape(N_DEV, d, g_in.shape[3]), g_out.reshape(N_DEV, -1, d)]
            gsmall[nm_n][l] = dn[0]
        else:
            dyf, dyg, dyb = outproj_bwd(dh, p["wo"], token)
            g_wo = wgrad_2d(s["ycat"], dyb, 512, "wgrad_w_out")
            dq, dk, dv, dga, dqw, dkw = attn_bwd(s["proj"], s["gates"], p["qw"], p["kw"], s["yf"], s["lse"], dyf, seq,
                                                 tq=min(seq, ATTN_TQ_BWD))
            dqh, dkh, dvh, dgg, dgb, donw = gdn_bwd(s["qh"], s["kh"], s["vh"], s["gates"], s["proj"], p["onw"],
                                                     s["inv"], s["st"], dyg, seq)
            dxq, dxk, dxv, dwq, dwk, dwv = gdn_pre_bwd(s["proj"], p["cw"], dqh, dkh, dvh, seq)
            dsm, dgp = gates_bwd(s["proj"], p["gp"], dga, dgb, seq)
            dh, dnmix, dproj = inproj_bwd(s["x1"], dh, p["nmix"], p["wi"], [dq, dk, dv, dxq, dxk, dxv, dgg, dsm])
            g_wi = wgrad_2d(s["hn"], dproj, 512, "wgrad_w_in", F32)
            g_cw = jnp.concatenate([dwq, dwk, dwv], axis=1)
            send = [_in_cols_from_mine(g_wi).reshape(d, N_DEV, -1).transpose(1, 0, 2),
                    g_cw.reshape(CONV_W, N_DEV, -1).transpose(1, 0, 2), g_wo.reshape(N_DEV, -1, d)]
            for nm, val in (("mix_norm", dnmix[0]), ("fox_q_norm", dqw[0]), ("fox_k_norm", dkw[0]),
                            ("fox_f_bias", dgp[0, 0:8]), ("gdn_a_log", dgp[1, 8:12]), ("gdn_dt_bias", dgp[0, 8:12]),
                            ("gdn_out_norm", donw[0])):
                gsmall[nm][l] = val
        flags = [True] * len(send)
        if k == 0:
            send.append(_pack_small({nm: jnp.stack(v) for nm, v in gsmall.items()}))
            flags.append(False)
        prev = dh
        if pending is not None:
            got[pending[1]] = exchange_end(pending[0], dh)
            prev = got[pending[1]][0]
        state, token = exchange_begin(send, flags, f"exchange_grads_{k}", prev)
        pending = (state, k)
    grad_x = dh.reshape(nb, seq, d)

    res = {}

    def update_stage(k, slots, after):
        l, gi = stages[k]
        for i, nm in enumerate(GROUPS[gi]):
            r, c = wts[nm].shape[1:]
            res[nm] = adamw_reduce(slots[i].reshape(N_DEV, r, c), wts[nm], mom[nm], var[nm], l, f"adamw_{nm}_{l}",
                                   after, res.get(nm))
            if after is not None:
                after = res[nm][0]
        return after

    last = token
    for k in range(1, len(stages)):
        last = update_stage(k, got[k], last)
    got[0] = exchange_end(pending[0], last)
    update_stage(0, got[0], None)
    small_like = {nm: wts[nm] for nm in SMALL}
    sm = adamw_reduce(got[0][-1], _pack_small(small_like)[None], _pack_small({nm: mom[nm] for nm in SMALL})[None],
                      _pack_small({nm: var[nm] for nm in SMALL})[None], 0, "adamw_small")
    sm = [_unpack_small(a[0], small_like) for a in sm]
    for nm in SMALL:
        res[nm] = [sm[j][nm] for j in range(4)]
    return (loss, grad_x, *[res[nm][0] for nm in WEIGHTS], *[res[nm][1] for nm in WEIGHTS],
            *[res[nm][2] for nm in WEIGHTS], *[res[nm][3] for nm in WEIGHTS])
```

```python
import functools

import jax
import jax.numpy as jnp
from jax import lax
from jax.experimental import pallas as pl
from jax.experimental.pallas import tpu as pltpu

F32 = jnp.float32
BF16 = jnp.bfloat16
EPS = 1e-6
N_DEV = 8
MESH = pl.DeviceIdType.MESH
HIGHEST = lax.Precision.HIGHEST
VMEM_LIMIT = 56 * 1024 * 1024

FOX_HEADS, FOX_DH = 8, 64
GDN_HEADS, GDN_DH = 4, 128
CHUNK = 64
CONV_W = 4

ADAM_LR, ADAM_B1, ADAM_B2, ADAM_EPS, ADAM_WD, ADAM_STEP = 0.001, 0.9, 0.999, 1e-08, 0.01, 10


def _cp(*sem):
    return pltpu.CompilerParams(dimension_semantics=sem, vmem_limit_bytes=VMEM_LIMIT)


def _dot(a, b):
    return jnp.dot(a, b, preferred_element_type=F32)


def _dot_nt(a, b):
    return lax.dot_general(a, b, (((1,), (1,)), ((), ())), preferred_element_type=F32)


def _dot_tn(a, b):
    return lax.dot_general(a, b, (((0,), (0,)), ((), ())), preferred_element_type=F32)


def _rstd(xf):
    return lax.rsqrt(jnp.mean(xf * xf, axis=-1, keepdims=True) + EPS)


def _rms_bwd(xf, r, dyn):
    return r * dyn - xf * (r * r * r) * jnp.mean(dyn * xf, axis=-1, keepdims=True)


def ffn_fwd(x, nw, w_in, w_out, tm=512, rc=512):
    t, d = x.shape
    nj, fb = w_out.shape[0], w_out.shape[1]
    rc = min(rc, tm)

    def body(x_ref, nw_ref, wi_ref, wo_ref, o_ref, xn_ref, gu_ref, h_ref, acc_ref):
        j = pl.program_id(1)

        @pl.when(j == 0)
        def _():
            xf = x_ref[...]
            xn_ref[...] = (xf * _rstd(xf) * nw_ref[...]).astype(BF16)
            acc_ref[...] = jnp.zeros_like(acc_ref)

        rows = [slice(c * rc, (c + 1) * rc) for c in range(tm // rc)]
        gs = [_dot(xn_ref[r, :], wi_ref[0]) for r in rows]
        us = [_dot(xn_ref[r, :], wi_ref[1]) for r in rows]
        hs = []
        for g, u, r in zip(gs, us, rows):
            sg = jax.nn.sigmoid(g)
            silu = g * sg
            h = (silu * u).astype(BF16)
            gu_ref[0, r, :] = (u * (sg * (1.0 + g * (1.0 - sg)))).astype(BF16)
            gu_ref[1, r, :] = silu.astype(BF16)
            h_ref[r, :] = h
            hs.append(h)
        for h, r in zip(hs, rows):
            acc_ref[r, :] += _dot(h, wo_ref[...])

        @pl.when(j == nj - 1)
        def _():
            o_ref[...] = x_ref[...] + 0.5 * acc_ref[...]

    return pl.pallas_call(
        body, grid=(t // tm, nj),
        in_specs=[pl.BlockSpec((tm, d), lambda i, j: (i, 0)),
                  pl.BlockSpec((1, d), lambda i, j: (0, 0)),
                  pl.BlockSpec((2, None, d, fb), lambda i, j: (0, j, 0, 0)),
                  pl.BlockSpec((None, fb, d), lambda i, j: (j, 0, 0))],
        out_specs=[pl.BlockSpec((tm, d), lambda i, j: (i, 0)),
                   pl.BlockSpec((tm, d), lambda i, j: (i, 0)),
                   pl.BlockSpec((2, None, tm, fb), lambda i, j: (0, j, i, 0)),
                   pl.BlockSpec((None, tm, fb), lambda i, j: (j, i, 0))],
        out_shape=[jax.ShapeDtypeStruct((t, d), F32), jax.ShapeDtypeStruct((t, d), BF16),
                   jax.ShapeDtypeStruct((2, nj, t, fb), BF16), jax.ShapeDtypeStruct((nj, t, fb), BF16)],
        scratch_shapes=[pltpu.VMEM((tm, d), F32)],
        compiler_params=_cp("parallel", "arbitrary"), name="ffn_fwd")(x, nw, w_in, w_out)


def ffn_bwd(x, dy, nw, gu, w_in, w_out, tm=512, rc=256):
    t, d = x.shape
    nj, fb = w_out.shape[0], w_out.shape[1]
    rc = min(rc, tm)

    def body(x_ref, dy_ref, nw_ref, gu_ref, wi_ref, wo_ref,
             dx_ref, dnw_ref, dgu_ref, dyh_ref, acc_ref):
        i, j = pl.program_id(0), pl.program_id(1)

        @pl.when(j == 0)
        def _():
            dyh_ref[...] = (0.5 * dy_ref[...]).astype(BF16)
            acc_ref[...] = jnp.zeros_like(acc_ref)

        @pl.when((i == 0) & (j == 0))
        def _():
            dnw_ref[...] = jnp.zeros_like(dnw_ref)

        rows = [slice(c * rc, (c + 1) * rc) for c in range(tm // rc)]
        dhs = [_dot_nt(dyh_ref[r, :], wo_ref[...]) for r in rows]
        dgs = [(dh * gu_ref[0, r, :].astype(F32)).astype(BF16) for dh, r in zip(dhs, rows)]
        dus = [(dh * gu_ref[1, r, :].astype(F32)).astype(BF16) for dh, r in zip(dhs, rows)]
        for dg, du, r in zip(dgs, dus, rows):
            dgu_ref[0, r, :] = dg
            dgu_ref[1, r, :] = du
        for dg, du, r in zip(dgs, dus, rows):
            acc_ref[r, :] += _dot_nt(dg, wi_ref[0]) + _dot_nt(du, wi_ref[1])

        @pl.when(j == nj - 1)
        def _():
            xf = x_ref[...]
            r = _rstd(xf)
            dxn = acc_ref[...]
            dnw_ref[...] += jnp.sum(dxn * xf * r, axis=0, keepdims=True)
            dx_ref[...] = _rms_bwd(xf, r, dxn * nw_ref[...]) + dy_ref[...]

    return pl.pallas_call(
        body, grid=(t // tm, nj),
        in_specs=[pl.BlockSpec((tm, d), lambda i, j: (i, 0)),
                  pl.BlockSpec((tm, d), lambda i, j: (i, 0)),
                  pl.BlockSpec((1, d), lambda i, j: (0, 0)),
                  pl.BlockSpec((2, None, tm, fb), lambda i, j: (0, j, i, 0)),
                  pl.BlockSpec((2, None, d, fb), lambda i, j: (0, j, 0, 0)),
                  pl.BlockSpec((None, fb, d), lambda i, j: (j, 0, 0))],
        out_specs=[pl.BlockSpec((tm, d), lambda i, j: (i, 0)),
                   pl.BlockSpec((1, d), lambda i, j: (0, 0)),
                   pl.BlockSpec((2, None, tm, fb), lambda i, j: (0, j, i, 0)),
                   pl.BlockSpec((tm, d), lambda i, j: (i, 0))],
        out_shape=[jax.ShapeDtypeStruct((t, d), F32),
                   jax.ShapeDtypeStruct((1, d), F32),
                   jax.ShapeDtypeStruct((2, nj, t, fb), BF16),
                   jax.ShapeDtypeStruct((t, d), BF16)],
        scratch_shapes=[pltpu.VMEM((tm, d), F32)],
        compiler_params=_cp("arbitrary", "arbitrary"), name="ffn_bwd")(x, dy, nw, gu, w_in, w_out)


def _wgrad_call(a, b, a_spec, b_spec, out_shape, out_spec, grid, name, out_dtype=BF16):
    last = len(grid) - 1
    acc_shape = tuple(s for s in out_spec.block_shape if s is not None)

    def body(a_ref, b_ref, o_ref, acc_ref):
        @pl.when(pl.program_id(last) == 0)
        def _():
            acc_ref[...] = jnp.zeros_like(acc_ref)

        acc_ref[...] += _dot_tn(a_ref[...], b_ref[...])

        @pl.when(pl.program_id(last) == grid[last] - 1)
        def _():
            o_ref[...] = acc_ref[...].astype(o_ref.dtype)

    sem = ("parallel",) * last + ("arbitrary",)
    return pl.pallas_call(body, grid=grid, in_specs=[a_spec, b_spec], out_specs=out_spec,
                          out_shape=jax.ShapeDtypeStruct(out_shape, out_dtype),
                          scratch_shapes=[pltpu.VMEM(acc_shape, F32)],
                          compiler_params=_cp(*sem), name=name)(a, b)


WGRAD_TM = 1024


def wgrad_ffn_in(xn, dgu, tm=WGRAD_TM):
    t, d = xn.shape
    _, nj, _, fb = dgu.shape
    tm = min(tm, t)
    return _wgrad_call(xn, dgu,
                       pl.BlockSpec((tm, d), lambda p, j, k: (k, 0)),
                       pl.BlockSpec((None, None, tm, fb), lambda p, j, k: (p, j, k, 0)),
                       (2, nj, d, fb), pl.BlockSpec((None, None, d, fb), lambda p, j, k: (p, j, 0, 0)),
                       (2, nj, t // tm), "wgrad_ffn_in")


def wgrad_ffn_out(h, dyh, tm=WGRAD_TM):
    nj, t, fb = h.shape
    d = dyh.shape[1]
    tm = min(tm, t)
    return _wgrad_call(h, dyh,
                       pl.BlockSpec((None, tm, fb), lambda j, k: (j, k, 0)),
                       pl.BlockSpec((tm, d), lambda j, k: (k, 0)),
                       (nj, fb, d), pl.BlockSpec((None, fb, d), lambda j, k: (j, 0, 0)),
                       (nj, t // tm), "wgrad_ffn_out")


def wgrad_2d(a, b, tk, name, out_dtype=BF16, tm=512):
    t, k = a.shape
    n = b.shape[1]
    return _wgrad_call(a, b,
                       pl.BlockSpec((tm, tk), lambda c, s: (s, c)),
                       pl.BlockSpec((tm, n), lambda c, s: (s, 0)),
                       (k, n), pl.BlockSpec((tk, n), lambda c, s: (c, 0)),
                       (k // tk, t // tm), name, out_dtype)


N_BIG = 7 * 512
N_PROJ = N_BIG + 128
COL_SMALL = N_BIG // 128


def inproj_fwd(x, nw, w, tm=256):
    t, d = x.shape
    n = w.shape[1]

    def body(x_ref, nw_ref, w_ref, p_ref, hn_ref):
        xf = x_ref[...]
        hn = (xf * _rstd(xf) * nw_ref[...]).astype(BF16)
        hn_ref[...] = hn
        p_ref[...] = _dot(hn, w_ref[...])

    return pl.pallas_call(
        body, grid=(t // tm,),
        in_specs=[pl.BlockSpec((tm, d), lambda i: (i, 0)), pl.BlockSpec((1, d), lambda i: (0, 0)),
                  pl.BlockSpec((d, n), lambda i: (0, 0))],
        out_specs=[pl.BlockSpec((tm, n), lambda i: (i, 0)), pl.BlockSpec((tm, d), lambda i: (i, 0))],
        out_shape=[jax.ShapeDtypeStruct((t, n), F32), jax.ShapeDtypeStruct((t, d), BF16)],
        compiler_params=_cp("parallel"), name="inproj_fwd")(x, nw, w)


def inproj_bwd(x, dres, nw, w, dparts, tm=256):
    t, d = x.shape
    n = w.shape[1]
    widths = [p.shape[1] for p in dparts]
    assert sum(widths) == n

    def body(x_ref, dres_ref, nw_ref, w_ref, *rest):
        part_refs, (dx_ref, dnw_ref, dp_ref) = rest[:len(widths)], rest[len(widths):]

        @pl.when(pl.program_id(0) == 0)
        def _():
            dnw_ref[...] = jnp.zeros_like(dnw_ref)

        dp = jnp.concatenate([r[...].astype(BF16) for r in part_refs], axis=1)
        dp_ref[...] = dp
        dhn = _dot_nt(dp, w_ref[...])
        xf = x_ref[...]
        r = _rstd(xf)
        dnw_ref[...] += jnp.sum(dhn * xf * r, axis=0, keepdims=True)
        dx_ref[...] = _rms_bwd(xf, r, dhn * nw_ref[...]) + dres_ref[...]

    return pl.pallas_call(
        body, grid=(t // tm,),
        in_specs=[pl.BlockSpec((tm, d), lambda i: (i, 0)), pl.BlockSpec((tm, d), lambda i: (i, 0)),
                  pl.BlockSpec((1, d), lambda i: (0, 0)), pl.BlockSpec((d, n), lambda i: (0, 0))]
                 + [pl.BlockSpec((tm, wd), lambda i: (i, 0)) for wd in widths],
        out_specs=[pl.BlockSpec((tm, d), lambda i: (i, 0)), pl.BlockSpec((1, d), lambda i: (0, 0)),
                   pl.BlockSpec((tm, n), lambda i: (i, 0))],
        out_shape=[jax.ShapeDtypeStruct((t, d), F32), jax.ShapeDtypeStruct((1, d), F32),
                   jax.ShapeDtypeStruct((t, n), BF16)],
        compiler_params=_cp("arbitrary"), name="inproj_bwd")(x, dres, nw, w, *dparts)


def outproj_fwd(x, yf, yg, w, tm=512):
    t, d = x.shape
    hw = yf.shape[1]

    def body(x_ref, yf_ref, yg_ref, w_ref, o_ref, y_ref):
        y = jnp.concatenate([yf_ref[...], yg_ref[...]], axis=1).astype(BF16)
        y_ref[...] = y
        o_ref[...] = x_ref[...] + _dot(y, w_ref[...])

    return pl.pallas_call(
        body, grid=(t // tm,),
        in_specs=[pl.BlockSpec((tm, d), lambda i: (i, 0)), pl.BlockSpec((tm, hw), lambda i: (i, 0)),
                  pl.BlockSpec((tm, hw), lambda i: (i, 0)), pl.BlockSpec((2 * hw, d), lambda i: (0, 0))],
        out_specs=[pl.BlockSpec((tm, d), lambda i: (i, 0)), pl.BlockSpec((tm, 2 * hw), lambda i: (i, 0))],
        out_shape=[jax.ShapeDtypeStruct((t, d), F32), jax.ShapeDtypeStruct((t, 2 * hw), BF16)],
        compiler_params=_cp("parallel"), name="outproj_fwd")(x, yf, yg, w)


def outproj_bwd(dy, w, after=None, tm=512):
    t, d = dy.shape
    hw = w.shape[0] // 2
    extra = [] if after is None else [after]

    def body(dy_ref, w_ref, *rest):
        df_ref, dg_ref, dyb_ref = rest[-3:]
        dyb = dy_ref[...].astype(BF16)
        dyb_ref[...] = dyb
        dyy = _dot_nt(dyb, w_ref[...])
        df_ref[...] = dyy[:, :hw]
        dg_ref[...] = dyy[:, hw:]

    return pl.pallas_call(
        body, grid=(t // tm,),
        in_specs=[pl.BlockSpec((tm, d), lambda i: (i, 0)), pl.BlockSpec((2 * hw, d), lambda i: (0, 0))]
                 + [pl.BlockSpec(memory_space=pl.ANY)] * len(extra),
        out_specs=[pl.BlockSpec((tm, hw), lambda i: (i, 0)), pl.BlockSpec((tm, hw), lambda i: (i, 0)),
                   pl.BlockSpec((tm, d), lambda i: (i, 0))],
        out_shape=[jax.ShapeDtypeStruct((t, hw), F32), jax.ShapeDtypeStruct((t, hw), F32),
                   jax.ShapeDtypeStruct((t, d), BF16)],
        compiler_params=_cp("parallel"), name="outproj_bwd")(dy, w, *extra)


def _lane(shape):
    return lax.broadcasted_iota(jnp.int32, shape, 1)


def _row(shape):
    return lax.broadcasted_iota(jnp.int32, shape, 0)


def _gate_terms(val, gp_ref):
    z = val + gp_ref[0:1, :]
    sp = jnp.log(1.0 + jnp.exp(-jnp.abs(z)))
    return z, sp


def gates_fwd(proj, gp, seq, ts=512):
    t = proj.shape[0]
    nb, ns = t // seq, seq // ts

    def body(v_ref, gp_ref, o_ref, carry_ref):
        @pl.when(pl.program_id(1) == 0)
        def _():
            carry_ref[...] = jnp.zeros_like(carry_ref)

        z, sp = _gate_terms(v_ref[...], gp_ref)
        logsig = jnp.minimum(z, 0.0) - sp
        tri = (_row((ts, ts)) >= _lane((ts, ts))).astype(F32)
        cum = jnp.dot(tri, logsig, precision=HIGHEST, preferred_element_type=F32) + carry_ref[0:1, :]
        carry_ref[0:1, :] = cum[ts - 1:ts, :]
        g = -jnp.exp(gp_ref[1:2, :]) * (jnp.maximum(z, 0.0) + sp)
        beta = jax.nn.sigmoid(z)
        lane = _lane((ts, 128))
        o_ref[...] = jnp.where(lane < 8, cum, jnp.where(lane < 12, g, jnp.where(lane < 16, beta, 0.0)))

    return pl.pallas_call(
        body, grid=(nb, ns),
        in_specs=[pl.BlockSpec((ts, 128), lambda b, s: (b * ns + s, COL_SMALL)),
                  pl.BlockSpec((8, 128), lambda b, s: (0, 0))],
        out_specs=pl.BlockSpec((ts, 128), lambda b, s: (b * ns + s, 0)),
        out_shape=jax.ShapeDtypeStruct((t, 128), F32),
        scratch_shapes=[pltpu.VMEM((8, 128), F32)],
        compiler_params=_cp("parallel", "arbitrary"), name="gates_fwd")(proj, gp)


def gates_bwd(proj, gp, dga, dgb, seq, ts=512):
    t = proj.shape[0]
    nb, ns = t // seq, seq // ts

    def body(v_ref, gp_ref, da_ref, db_ref, ds_ref, dgp_ref, carry_ref):
        @pl.when(pl.program_id(1) == 0)
        def _():
            carry_ref[...] = jnp.zeros_like(carry_ref)

        @pl.when((pl.program_id(0) == 0) & (pl.program_id(1) == 0))
        def _():
            dgp_ref[...] = jnp.zeros_like(dgp_ref)

        lane = _lane((ts, 128))
        dgate = jnp.where(lane < 8, da_ref[...], jnp.where(lane < 16, db_ref[...], 0.0))
        z, sp = _gate_terms(v_ref[...], gp_ref)
        triu = (_row((ts, ts)) <= _lane((ts, ts))).astype(F32)
        dlog = jnp.dot(triu, dgate, precision=HIGHEST, preferred_element_type=F32) + carry_ref[0:1, :]
        carry_ref[0:1, :] = dlog[0:1, :]
        sig = jax.nn.sigmoid(z)
        nea = -jnp.exp(gp_ref[1:2, :])
        g = nea * (jnp.maximum(z, 0.0) + sp)
        dz = jnp.where(lane < 8, dlog * (1.0 - sig),
                       jnp.where(lane < 12, dgate * nea * sig, dgate * sig * (1.0 - sig)))
        dz = jnp.where(lane < 16, dz, 0.0)
        ds_ref[...] = dz
        dgp_ref[0:1, :] += jnp.where(lane[0:1] < 12, jnp.sum(dz, axis=0, keepdims=True), 0.0)
        dgp_ref[1:2, :] += jnp.where((lane[0:1] >= 8) & (lane[0:1] < 12), jnp.sum(dgate * g, axis=0, keepdims=True), 0.0)

    rev = lambda b, s: (b * ns + (ns - 1 - s), 0)
    return pl.pallas_call(
        body, grid=(nb, ns),
        in_specs=[pl.BlockSpec((ts, 128), lambda b, s: (b * ns + (ns - 1 - s), COL_SMALL)),
                  pl.BlockSpec((8, 128), lambda b, s: (0, 0)),
                  pl.BlockSpec((ts, 128), rev), pl.BlockSpec((ts, 128), rev)],
        out_specs=[pl.BlockSpec((ts, 128), rev), pl.BlockSpec((8, 128), lambda b, s: (0, 0))],
        out_shape=[jax.ShapeDtypeStruct((t, 128), F32), jax.ShapeDtypeStruct((8, 128), F32)],
        scratch_shapes=[pltpu.VMEM((8, 128), F32)],
        compiler_params=_cp("arbitrary", "arbitrary"), name="gates_bwd")(proj, gp, dga, dgb)


NEG = -1e30
ATTN_TQ_FWD = 1024
ATTN_TQ_BWD = 512


def _pick_lane(tile, idx):
    return jnp.sum(jnp.where(_lane(tile.shape) == idx, tile, 0.0), axis=1, keepdims=True)


def _col_to_row(col, n):
    return jnp.sum(jnp.where(_row((n, n)) == _lane((n, n)), col, 0.0), axis=0, keepdims=True)


def _row_to_col(row, n):
    return jnp.sum(jnp.where(_row((n, n)) == _lane((n, n)), row, 0.0), axis=1, keepdims=True)


def _rows(i, n):
    return pl.ds(pl.multiple_of(i * n, n), n)


LOG2E = 1.4426950408889634
LN2 = 0.6931471805599453


def _rowsum(z, width, passes=2):
    ones = jnp.ones((z.shape[1], width), BF16)
    total, rest = None, z
    for _ in range(passes):
        part = rest.astype(BF16)
        rest = rest - part.astype(F32)
        total = _dot(part, ones) if total is None else total + _dot(part, ones)
    return total


def _rstd_mxu(xf):
    return lax.rsqrt(_rowsum(xf * xf, xf.shape[1]) * (1.0 / xf.shape[1]) + EPS)


def _rms_bwd_mxu(xf, r, dyn):
    return r * dyn - xf * (r * r * r) * (_rowsum(dyn * xf, xf.shape[1]) * (1.0 / xf.shape[1]))


def _pick_lane_mxu(tile, idx, width):
    onehot = (_row((tile.shape[1], width)) == idx).astype(BF16)
    total, rest = None, tile
    for _ in range(3):
        part = rest.astype(BF16)
        rest = rest - part.astype(F32)
        total = _dot(part, onehot) if total is None else total + _dot(part, onehot)
    return total


def _split3(x):
    hi = x.astype(BF16).astype(F32)
    mid = (x - hi).astype(BF16).astype(F32)
    return [hi, mid, (x - hi - mid).astype(BF16).astype(F32)]


def _aug_cols(cols, n, width):
    lane = _lane((n, width))
    out = jnp.zeros((n, width), F32)
    for i, c in enumerate(cols):
        out = jnp.where(lane == i, c, out)
    return out


def _once(shape, index_map):
    return pl.BlockSpec(shape, index_map, pipeline_mode=pl.Buffered(1))


def attn_fwd(proj, gates, qw, kw, seq, tq=256):
    t = proj.shape[0]
    nb, nq, dh = t // seq, seq // tq, FOX_DH
    scale = dh ** -0.5

    def body(q_ref, k_ref, v_ref, g_ref, qw_ref, kw_ref, y_ref, lse_ref, qs, ks, vs):
        p = pl.program_id(1)
        heads = range(2)

        def prep(i, _):
            r = _rows(i, tq)
            for hh in heads:
                lanes = slice(hh * dh, (hh + 1) * dh)
                qf, kf = q_ref[r, lanes], k_ref[r, lanes]
                cc = _pick_lane_mxu(g_ref[r, :], 2 * p + hh, dh) * LOG2E
                qs[hh, r, 0:dh] = (qf * _rstd_mxu(qf) * qw_ref[...] * (scale * LOG2E)).astype(BF16)
                qs[hh, r, dh:2 * dh] = _aug_cols(_split3(cc) + [1.0, 1.0, 1.0], tq, dh).astype(BF16)
                ks[hh, r, 0:dh] = (kf * _rstd_mxu(kf) * kw_ref[...]).astype(BF16)
                ks[hh, r, dh:2 * dh] = _aug_cols([1.0, 1.0, 1.0] + _split3(-cc), tq, dh).astype(BF16)
                vs[hh, r, :] = v_ref[r, lanes].astype(BF16)
            return 0

        lax.fori_loop(0, nq, prep, 0)

        def q_tile(i, _):
            r = _rows(i, tq)
            qt = [qs[hh, r, :] for hh in heads]

            def kv_step(j, carry, masked):
                kr = _rows(j, tq)
                out = []
                for hh in heads:
                    m, l, acc = carry[hh]
                    s = _dot_nt(qt[hh], ks[hh, kr, :])
                    if masked:
                        s = jnp.where(_row((tq, tq)) >= _lane((tq, tq)), s, NEG)
                    m_new = jnp.maximum(m, jnp.max(s, axis=1, keepdims=True))
                    pe = jnp.exp2(s - m_new)
                    a = jnp.exp2(m - m_new)
                    out.append((m_new, a * l + jnp.sum(pe, axis=1, keepdims=True),
                                a * acc + _dot(pe.astype(BF16), vs[hh, kr, :])))
                return tuple(out)

            one = (jnp.full((tq, 1), NEG, F32), jnp.zeros((tq, 1), F32), jnp.zeros((tq, dh), F32))
            carry = lax.fori_loop(0, i, lambda j, c: kv_step(j, c, False), (one, one))
            carry = kv_step(i, carry, True)
            for hh in heads:
                m, l, acc = carry[hh]
                lanes = slice(hh * dh, (hh + 1) * dh)
                y_ref[r, lanes] = acc / l
                lse_ref[r, lanes] = jnp.broadcast_to(m + jnp.log2(l), (tq, dh))
            return 0

        lax.fori_loop(0, nq, q_tile, 0)

    blk = lambda off: _once((seq, 128), lambda b, p: (b, off + p))
    return pl.pallas_call(
        body, grid=(nb, 4),
        in_specs=[blk(0), blk(4), blk(8), _once((seq, 128), lambda b, p: (b, 0)),
                  pl.BlockSpec((1, dh), lambda b, p: (0, 0)), pl.BlockSpec((1, dh), lambda b, p: (0, 0))],
        out_specs=[pl.BlockSpec((seq, 128), lambda b, p: (b, p)), pl.BlockSpec((seq, 128), lambda b, p: (b, p))],
        out_shape=[jax.ShapeDtypeStruct((t, 512), F32), jax.ShapeDtypeStruct((t, 512), F32)],
        scratch_shapes=[pltpu.VMEM((2, seq, 2 * dh), BF16), pltpu.VMEM((2, seq, 2 * dh), BF16),
                        pltpu.VMEM((2, seq, dh), BF16)],
        compiler_params=_cp("parallel", "arbitrary"), name="attn_fwd")(proj, proj, proj, gates, qw, kw)


def attn_bwd(proj, gates, qw, kw, y, lse, dy, seq, tq=256):
    t = proj.shape[0]
    nb, nq, dh = t // seq, seq // tq, FOX_DH
    scale = dh ** -0.5

    def body(q_ref, k_ref, v_ref, g_ref, qw_ref, kw_ref, y_ref, lse_ref, dy_ref,
             dq_ref, dk_ref, dv_ref, dg_ref, dqw_ref, dkw_ref,
             qs, ks, vs, dos, dsrow, dqa, dka):
        b, p = pl.program_id(0), pl.program_id(1)

        @pl.when((b == 0) & (p == 0))
        def _():
            dqw_ref[...] = jnp.zeros_like(dqw_ref)
            dkw_ref[...] = jnp.zeros_like(dkw_ref)

        @pl.when(p == 0)
        def _():
            dg_ref[...] = jnp.zeros_like(dg_ref)

        heads = range(2)
        hl = lambda hh: slice(hh * dh, (hh + 1) * dh)

        def prep(i, _):
            r = _rows(i, tq)
            for hh in heads:
                lanes = hl(hh)
                qf, kf = q_ref[r, lanes], k_ref[r, lanes]
                cc = _pick_lane_mxu(g_ref[r, :], 2 * p + hh, dh) * LOG2E
                lse2 = lse_ref[r, lanes]
                dyf = dy_ref[r, lanes]
                delta = _rowsum(dyf * y_ref[r, lanes], dh)
                qs[hh, r, 0:dh] = (qf * _rstd_mxu(qf) * qw_ref[...] * (scale * LOG2E)).astype(BF16)
                qs[hh, r, dh:2 * dh] = _aug_cols(_split3(cc) + [1.0, 1.0, 1.0] + _split3(-lse2), tq, dh).astype(BF16)
                ks[hh, r, 0:dh] = (kf * _rstd_mxu(kf) * kw_ref[...]).astype(BF16)
                ks[hh, r, dh:2 * dh] = _aug_cols([1.0, 1.0, 1.0] + _split3(-cc) + [1.0, 1.0, 1.0], tq, dh).astype(BF16)
                vs[hh, r, 0:dh] = v_ref[r, lanes].astype(BF16)
                vs[hh, r, dh:2 * dh] = _aug_cols([1.0, 1.0, 1.0], tq, dh).astype(BF16)
                dos[hh, r, 0:dh] = dyf.astype(BF16)
                dos[hh, r, dh:2 * dh] = _aug_cols(_split3(-delta), tq, dh).astype(BF16)
                dsrow[hh, r, :] = jnp.zeros((tq, 1), F32)
                dqa[hh, r, :] = jnp.zeros((tq, dh), F32)
            return 0

        lax.fori_loop(0, nq, prep, 0)

        def kv_tile(j, _):
            kr = _rows(j, tq)
            kt = [ks[hh, kr, :] for hh in heads]
            vt = [vs[hh, kr, :] for hh in heads]

            def q_step(i, carry, masked):
                r = _rows(i, tq)
                out = []
                for hh in heads:
                    dk, dv, dcr = carry[hh]
                    qt, dot = qs[hh, r, :], dos[hh, r, :]
                    s = _dot_nt(qt, kt[hh])
                    if masked:
                        s = jnp.where(_row((tq, tq)) >= _lane((tq, tq)), s, NEG)
                    pe = jnp.exp2(s)
                    ds = pe * _dot_nt(dot, vt[hh])
                    dsb = ds.astype(BF16)
                    dqa[hh, r, :] += _dot(dsb, kt[hh][:, 0:dh])
                    dsrow[hh, r, :] += jnp.sum(ds, axis=1, keepdims=True)
                    out.append((dk + _dot_tn(dsb, qt[:, 0:dh]), dv + _dot_tn(pe.astype(BF16), dot[:, 0:dh]),
                                dcr - jnp.sum(ds, axis=0, keepdims=True)))
                return tuple(out)

            one = (jnp.zeros((tq, dh), F32), jnp.zeros((tq, dh), F32), jnp.zeros((1, tq), F32))
            carry = q_step(j, (one, one), True)
            carry = lax.fori_loop(j + 1, nq, lambda i, c: q_step(i, c, False), carry)
            for hh in heads:
                dk, dv, dcr = carry[hh]
                dka[hh, kr, :] = dk
                dv_ref[kr, hl(hh)] = dv
                dg_ref[kr, :] = jnp.where(_lane((tq, 128)) == 2 * p + hh, _row_to_col(dcr, tq), dg_ref[kr, :])
            return 0

        lax.fori_loop(0, nq, kv_tile, 0)

        def post(i, _):
            r = _rows(i, tq)
            for hh in heads:
                lanes = hl(hh)
                qf, kf = q_ref[r, lanes], k_ref[r, lanes]
                rq, rk = _rstd_mxu(qf), _rstd_mxu(kf)
                dqn, dkn = dqa[hh, r, :] * scale, dka[hh, r, :] * LN2
                dqw_ref[...] += jnp.sum(dqn * qf * rq, axis=0, keepdims=True)
                dkw_ref[...] += jnp.sum(dkn * kf * rk, axis=0, keepdims=True)
                dq_ref[r, lanes] = _rms_bwd_mxu(qf, rq, dqn * qw_ref[...])
                dk_ref[r, lanes] = _rms_bwd_mxu(kf, rk, dkn * kw_ref[...])
                dg_ref[r, :] += jnp.where(_lane((tq, 128)) == 2 * p + hh, dsrow[hh, r, :], 0.0)
            return 0

        lax.fori_loop(0, nq, post, 0)

    blk = lambda off: _once((seq, 128), lambda b, p: (b, off + p))
    own = lambda: _once((seq, 128), lambda b, p: (b, p))
    vec = lambda: pl.BlockSpec((1, dh), lambda b, p: (0, 0))
    return pl.pallas_call(
        body, grid=(nb, 4),
        in_specs=[blk(0), blk(4), blk(8), _once((seq, 128), lambda b, p: (b, 0)), vec(), vec(), own(), own(), own()],
        out_specs=[own(), own(), own(), _once((seq, 128), lambda b, p: (b, 0)), vec(), vec()],
        out_shape=[jax.ShapeDtypeStruct((t, 512), F32)] * 3
                  + [jax.ShapeDtypeStruct((t, 128), F32), jax.ShapeDtypeStruct((1, dh), F32), jax.ShapeDtypeStruct((1, dh), F32)],
        scratch_shapes=[pltpu.VMEM((2, seq, 2 * dh), BF16)] * 4
                       + [pltpu.VMEM((2, seq, 1), F32), pltpu.VMEM((2, seq, dh), F32), pltpu.VMEM((2, seq, dh), F32)],
        compiler_params=_cp("arbitrary", "arbitrary"), name="attn_bwd")(proj, proj, proj, gates, qw, kw, y, lse, dy)


def _silu_grad(c, sg):
    return sg * (1.0 + c * (1.0 - sg))


def _conv(x, w, n):
    row = _row(x.shape)
    c = x * w[CONV_W - 1:CONV_W, :]
    for k in range(CONV_W - 1):
        sh = CONV_W - 1 - k
        c = c + w[k:k + 1, :] * jnp.where(row >= sh, pltpu.roll(x, sh, 0), 0.0)
    return c


def gdn_pre_fwd(proj, cw, seq):
    t = proj.shape[0]
    nb = t // seq
    scale = GDN_DH ** -0.5

    def body(xq_ref, xk_ref, xv_ref, wq_ref, wk_ref, wv_ref, q_ref, k_ref, v_ref):
        def act(x_ref, w_ref):
            c = _conv(x_ref[...], w_ref[...], seq)
            return c * jax.nn.sigmoid(c)

        aq, ak = act(xq_ref, wq_ref), act(xk_ref, wk_ref)
        q_ref[...] = aq * lax.rsqrt(jnp.sum(aq * aq, axis=1, keepdims=True) + EPS) * scale
        k_ref[...] = ak * lax.rsqrt(jnp.sum(ak * ak, axis=1, keepdims=True) + EPS)
        v_ref[...] = act(xv_ref, wv_ref)

    xb = lambda off: pl.BlockSpec((seq, 128), lambda b, h: (b, off + h))
    wb = lambda off: pl.BlockSpec((CONV_W, 128), lambda b, h: (0, off + h))
    ob = lambda: pl.BlockSpec((seq, 128), lambda b, h: (b, h))
    return pl.pallas_call(
        body, grid=(nb, GDN_HEADS),
        in_specs=[xb(12), xb(16), xb(20), wb(0), wb(4), wb(8)],
        out_specs=[ob(), ob(), ob()],
        out_shape=[jax.ShapeDtypeStruct((t, 512), F32)] * 3,
        compiler_params=_cp("parallel", "parallel"), name="gdn_pre_fwd")(proj, proj, proj, cw, cw, cw)


def gdn_pre_bwd(proj, cw, dq, dk, dv, seq):
    t = proj.shape[0]
    nb = t // seq
    scale = GDN_DH ** -0.5

    def body(xq_ref, xk_ref, xv_ref, wq_ref, wk_ref, wv_ref, dq_ref, dk_ref, dv_ref,
             dxq_ref, dxk_ref, dxv_ref, dwq_ref, dwk_ref, dwv_ref):
        first = pl.program_id(1) == 0
        row = _row((seq, 128))

        def one(x_ref, w_ref, dy_ref, dx_ref, dw_ref, norm, sc):
            x, w = x_ref[...], w_ref[...]
            c = _conv(x, w, seq)
            sg = jax.nn.sigmoid(c)
            dy = dy_ref[...]
            if norm:
                a = c * sg
                rs = lax.rsqrt(jnp.sum(a * a, axis=1, keepdims=True) + EPS)
                dy = dy * sc
                da = rs * dy - a * (rs * rs * rs) * jnp.sum(dy * a, axis=1, keepdims=True)
            else:
                da = dy
            dc = da * _silu_grad(c, sg)
            dx = dc * w[CONV_W - 1:CONV_W, :]
            dws = [None] * CONV_W
            dws[CONV_W - 1] = jnp.sum(dc * x, axis=0, keepdims=True)
            for k in range(CONV_W - 1):
                sh = CONV_W - 1 - k
                dx = dx + w[k:k + 1, :] * jnp.where(row < seq - sh, pltpu.roll(dc, seq - sh, 0), 0.0)
                dws[k] = jnp.sum(dc * jnp.where(row >= sh, pltpu.roll(x, sh, 0), 0.0), axis=0, keepdims=True)
            dx_ref[...] = dx
            dwn = jnp.concatenate(dws, axis=0)

            @pl.when(first)
            def _():
                dw_ref[...] = dwn

            @pl.when(jnp.logical_not(first))
            def _():
                dw_ref[...] += dwn

        one(xq_ref, wq_ref, dq_ref, dxq_ref, dwq_ref, True, scale)
        one(xk_ref, wk_ref, dk_ref, dxk_ref, dwk_ref, True, 1.0)
        one(xv_ref, wv_ref, dv_ref, dxv_ref, dwv_ref, False, 1.0)

    xb = lambda off: pl.BlockSpec((seq, 128), lambda h, b: (b, off + h))
    wb = lambda off: pl.BlockSpec((CONV_W, 128), lambda h, b: (0, off + h))
    ob = lambda: pl.BlockSpec((seq, 128), lambda h, b: (b, h))
    return pl.pallas_call(
        body, grid=(GDN_HEADS, nb),
        in_specs=[xb(12), xb(16), xb(20), wb(0), wb(4), wb(8), ob(), ob(), ob()],
        out_specs=[ob(), ob(), ob()] + [pl.BlockSpec((CONV_W, 128), lambda h, b: (0, h))] * 3,
        out_shape=[jax.ShapeDtypeStruct((t, 512), F32)] * 3 + [jax.ShapeDtypeStruct((CONV_W, 512), F32)] * 3,
        compiler_params=_cp("parallel", "arbitrary"), name="gdn_pre_bwd")(proj, proj, proj, cw, cw, cw, dq, dk, dv)


def _b16(x):
    return x.astype(BF16)


@jax.custom_vjp
def _mm(a, b):
    return _dot(_b16(a), _b16(b))


_mm.defvjp(lambda a, b: (_mm(a, b), (a, b)),
           lambda res, g: (_dot_nt(_b16(g), _b16(res[1])), _dot_tn(_b16(res[0]), _b16(g))))


@jax.custom_vjp
def _mm_nt(a, b):
    return _dot_nt(_b16(a), _b16(b))


_mm_nt.defvjp(lambda a, b: (_mm_nt(a, b), (a, b)),
              lambda res, g: (_dot(_b16(g), _b16(res[1])), _dot_tn(_b16(g), _b16(res[0]))))


@jax.custom_vjp
def _mm_tn(a, b):
    return _dot_tn(_b16(a), _b16(b))


_mm_tn.defvjp(lambda a, b: (_mm_tn(a, b), (a, b)),
              lambda res, g: (_dot_nt(_b16(res[1]), _b16(g)), _dot(_b16(res[0]), _b16(g))))


def _dot32(a, b, dims=(((1,), (0,)), ((), ()))):
    def split(x):
        hi = x.astype(BF16)
        return hi, (x - hi.astype(F32)).astype(BF16)

    (ah, al), (bh, bl) = split(a), split(b)
    d = lambda x, y: lax.dot_general(x, y, dims, preferred_element_type=F32)
    return d(ah, bh) + (d(ah, bl) + d(al, bh))


def _inv_fwd_many(mats):
    n = mats[0].shape[0]
    eye = (_row((n, n)) == _lane((n, n))).astype(F32)
    invs, pws = [eye - a for a in mats], list(mats)
    for _ in range(n.bit_length() - 2):
        pws = [_dot32(pw, pw) for pw in pws]
        invs = [inv + _dot32(inv, pw) for inv, pw in zip(invs, pws)]
    return invs


@jax.custom_vjp
def _inv_saved(a, inv):
    return inv


def _inv_saved_bwd(inv, g):
    tg = _dot32(inv, g, (((0,), (0,)), ((), ())))
    return -_dot32(tg, inv, (((1,), (1,)), ((), ()))), jnp.zeros_like(inv)


_inv_saved.defvjp(lambda a, inv: (inv, inv), _inv_saved_bwd)


def _gdn_decay(gcol):
    c = CHUNK
    ri, ci = _row((c, c)), _lane((c, c))
    incl, eye = ri >= ci, ri == ci
    grow = jnp.sum(jnp.where(eye, gcol, 0.0), axis=0, keepdims=True)
    gc = jnp.sum(jnp.where(incl, grow, 0.0), axis=1, keepdims=True)
    gcr = jnp.sum(jnp.where(eye, gc, 0.0), axis=0, keepdims=True)
    gl = jnp.sum(jnp.where(_row((c, 1)) == c - 1, gc, 0.0), axis=0, keepdims=True)
    return gc, gl, jnp.exp(jnp.where(incl, gc - gcr, NEG))


def _gdn_a(k, bcol, decay):
    c = CHUNK
    return jnp.where(_row((c, c)) > _lane((c, c)), _mm_nt(k * bcol, k) * decay, 0.0)


def _gdn_chunk(q, k, v, gcol, bcol, state, gg, nw, inv_saved):
    c = CHUNK
    incl = _row((c, c)) >= _lane((c, c))
    gc, gl, decay = _gdn_decay(gcol)
    kb, vb = k * bcol, v * bcol
    inv = _inv_saved(_gdn_a(k, bcol, decay), inv_saved)
    eg = jnp.exp(gc)
    u = _mm(inv, vb)
    w = _mm(inv, kb * eg)
    pm = jnp.where(incl, _mm_nt(q, k) * decay, 0.0)
    kd = k * jnp.exp(gl - gc)
    qd = q * eg
    v_new = u - _mm(w, state)
    o = _mm(qd, state) + _mm(pm, v_new)
    state_new = state * jnp.exp(gl) + _mm_tn(kd, v_new)
    y = o * _rstd(o) * nw * (gg * jax.nn.sigmoid(gg))
    return y, state_new


_gdn_chunks = jax.vmap(_gdn_chunk, in_axes=(0, 0, 0, 0, 0, 0, 0, None, 0))


def _gdn_chain_inputs(chains, p, r, c, q_ref, k_ref, v_ref, g_ref, gg_ref, inv_ref):
    cols = {nm: [] for nm in ("q", "k", "v", "g", "b", "gg", "inv")}
    for b, hh in chains:
        h = GDN_HPS * p + hh
        ln = slice(hh * 128, (hh + 1) * 128)
        gt = g_ref[b, r, :]
        cols["q"].append(q_ref[b, r, ln])
        cols["k"].append(k_ref[b, r, ln])
        cols["v"].append(v_ref[b, r, ln])
        cols["g"].append(_pick_lane(gt, 8 + h))
        cols["b"].append(_pick_lane(gt, 12 + h))
        cols["gg"].append(gg_ref[b, r, ln])
        cols["inv"].append(inv_ref[b, hh, c])
    return [jnp.stack(cols[nm]) for nm in ("q", "k", "v", "g", "b", "gg", "inv")]


GDN_CB = 8
GDN_HPS = 4


def gdn_inv(k, gates, seq):
    t = k.shape[0]
    nb, nc = t // seq, seq // CHUNK
    rb = GDN_CB * CHUNK
    nsb = seq // rb

    def body(k_ref, g_ref, o_ref):
        h = pl.program_id(1)
        mats = []
        for c in range(GDN_CB):
            r = slice(c * CHUNK, (c + 1) * CHUNK)
            gt = g_ref[r, :]
            _, _, decay = _gdn_decay(_pick_lane(gt, 8 + h))
            mats.append(_gdn_a(k_ref[r, :], _pick_lane(gt, 12 + h), decay))
        for c, inv in enumerate(_inv_fwd_many(mats)):
            o_ref[c] = inv

    return pl.pallas_call(
        body, grid=(nb, GDN_HEADS, nsb),
        in_specs=[pl.BlockSpec((rb, 128), lambda b, h, s: (b * nsb + s, h)),
                  pl.BlockSpec((rb, 128), lambda b, h, s: (b * nsb + s, 0))],
        out_specs=pl.BlockSpec((None, None, GDN_CB, CHUNK, CHUNK), lambda b, h, s: (b, h, s, 0, 0)),
        out_shape=jax.ShapeDtypeStruct((nb, GDN_HEADS, nc, CHUNK, CHUNK), F32),
        compiler_params=_cp("parallel", "parallel", "parallel"), name="gdn_inv")(k, gates)


def _gdn_specs(nb, nsb, cb, rev):
    blk = (lambda s: nsb - 1 - s) if rev else (lambda s: s)
    rb = cb * CHUNK
    pair = lambda off=0: pl.BlockSpec((nb, rb, 128 * GDN_HPS), lambda s, p: (0, blk(s), off + p))
    gate = lambda: pl.BlockSpec((nb, rb, 128), lambda s, p: (0, blk(s), 0))
    mats = lambda n: pl.BlockSpec((nb, GDN_HPS, cb, n, n), lambda s, p: (0, p, blk(s), 0, 0))
    return pair, gate, mats


def gdn_fwd(q, k, v, gates, proj, nw, inv, seq):
    t = q.shape[0]
    nb, nc = t // seq, seq // CHUNK
    cb = GDN_CB
    nsb = nc // cb
    chains = [(b, hh) for b in range(nb) for hh in range(GDN_HPS)]
    nch = len(chains)
    pair, gate, mats = _gdn_specs(nb, nsb, cb, False)

    def body(q_ref, k_ref, v_ref, g_ref, gg_ref, inv_ref, nw_ref, y_ref, st_ref, carry):
        s, p = pl.program_id(0), pl.program_id(1)

        @pl.when(s == 0)
        def _():
            for ci in range(nch):
                carry[p * nch + ci] = jnp.zeros((GDN_DH, GDN_DH), F32)

        def step(c, states):
            r = _rows(c, CHUNK)
            for ci, (b, hh) in enumerate(chains):
                st_ref[b, hh, c] = states[ci]
            ins = _gdn_chain_inputs(chains, p, r, c, q_ref, k_ref, v_ref, g_ref, gg_ref, inv_ref)
            y, states = _gdn_chunks(*ins[:5], states, ins[5], nw_ref[...], ins[6])
            for ci, (b, hh) in enumerate(chains):
                y_ref[b, r, hh * 128:(hh + 1) * 128] = y[ci]
            return states

        states = lax.fori_loop(0, cb, step, jnp.stack([carry[p * nch + ci] for ci in range(nch)]))
        for ci in range(nch):
            carry[p * nch + ci] = states[ci]

    v3 = lambda a: a.reshape(nb, seq, a.shape[1])
    y, st = pl.pallas_call(
        body, grid=(nsb, GDN_HEADS // GDN_HPS),
        in_specs=[pair(), pair(), pair(), gate(), pair(24 // GDN_HPS), mats(CHUNK), pl.BlockSpec((1, 128), lambda s, p: (0, 0))],
        out_specs=[pair(), mats(GDN_DH)],
        out_shape=[jax.ShapeDtypeStruct((nb, seq, 512), F32),
                   jax.ShapeDtypeStruct((nb, GDN_HEADS, nc, GDN_DH, GDN_DH), F32)],
        scratch_shapes=[pltpu.VMEM((GDN_HEADS // GDN_HPS * nch, GDN_DH, GDN_DH), F32)],
        compiler_params=_cp("arbitrary", "arbitrary"), name="gdn_fwd")(v3(q), v3(k), v3(v), v3(gates), v3(proj), inv, nw)
    return y.reshape(t, 512), st


def gdn_bwd(q, k, v, gates, proj, nw, inv, states, dy, seq):
    t = q.shape[0]
    nb, nc = t // seq, seq // CHUNK
    cb = GDN_CB // 2
    nsb = nc // cb
    chains = [(b, hh) for b in range(nb) for hh in range(GDN_HPS)]
    nch = len(chains)
    pair, gate, mats = _gdn_specs(nb, nsb, cb, True)

    def body(q_ref, k_ref, v_ref, g_ref, gg_ref, inv_ref, st_ref, dy_ref, nw_ref,
             dq_ref, dk_ref, dv_ref, dgg_ref, dg_ref, dnw_ref, carry):
        s, p = pl.program_id(0), pl.program_id(1)

        @pl.when((s == 0) & (p == 0))
        def _():
            dnw_ref[...] = jnp.zeros_like(dnw_ref)

        @pl.when(p == 0)
        def _():
            dg_ref[...] = jnp.zeros_like(dg_ref)

        @pl.when(s == 0)
        def _():
            for ci in range(nch):
                carry[p * nch + ci] = jnp.zeros((GDN_DH, GDN_DH), F32)

        def step(idx, dstates):
            c = cb - 1 - idx
            r = _rows(c, CHUNK)
            ins = _gdn_chain_inputs(chains, p, r, c, q_ref, k_ref, v_ref, g_ref, gg_ref, inv_ref)
            st = jnp.stack([st_ref[b, hh, c] for b, hh in chains])
            dy = jnp.stack([dy_ref[b, r, hh * 128:(hh + 1) * 128] for b, hh in chains])
            _, vjp = jax.vjp(_gdn_chunks, *ins[:5], st, ins[5], nw_ref[...], ins[6])
            dq, dk, dv, dgc, dbc, dstates, dgg, dnw, _ = vjp((dy, dstates))
            dnw_ref[...] += dnw
            lane = _lane((CHUNK, 128))
            for ci, (b, hh) in enumerate(chains):
                h = GDN_HPS * p + hh
                ln = slice(hh * 128, (hh + 1) * 128)
                dq_ref[b, r, ln] = dq[ci]
                dk_ref[b, r, ln] = dk[ci]
                dv_ref[b, r, ln] = dv[ci]
                dgg_ref[b, r, ln] = dgg[ci]
                dg_ref[b, r, :] = jnp.where(lane == 8 + h, dgc[ci], jnp.where(lane == 12 + h, dbc[ci], dg_ref[b, r, :]))
            return dstates

        dstates = lax.fori_loop(0, cb, step, jnp.stack([carry[p * nch + ci] for ci in range(nch)]))
        for ci in range(nch):
            carry[p * nch + ci] = dstates[ci]

    v3 = lambda a: a.reshape(nb, seq, a.shape[1])
    res = pl.pallas_call(
        body, grid=(nsb, GDN_HEADS // GDN_HPS),
        in_specs=[pair(), pair(), pair(), gate(), pair(24 // GDN_HPS), mats(CHUNK), mats(GDN_DH), pair(),
                  pl.BlockSpec((1, 128), lambda s, p: (0, 0))],
        out_specs=[pair(), pair(), pair(), pair(), gate(), pl.BlockSpec((1, 128), lambda s, p: (0, 0))],
        out_shape=[jax.ShapeDtypeStruct((nb, seq, 512), F32)] * 4
                  + [jax.ShapeDtypeStruct((nb, seq, 128), F32), jax.ShapeDtypeStruct((1, 128), F32)],
        scratch_shapes=[pltpu.VMEM((GDN_HEADS // GDN_HPS * nch, GDN_DH, GDN_DH), F32)],
        compiler_params=_cp("arbitrary", "arbitrary"),
        name="gdn_bwd")(v3(q), v3(k), v3(v), v3(gates), v3(proj), inv, states, v3(dy), nw)
    return [a.reshape(t, a.shape[2]) for a in res[:5]] + [res[5]]


def loss_head(y, target, tm=512):
    t, d = y.shape

    def body(y_ref, t_ref, s_ref, dy_ref):
        @pl.when(pl.program_id(0) == 0)
        def _():
            s_ref[...] = jnp.zeros_like(s_ref)

        err = y_ref[...] - t_ref[...]
        s_ref[...] += jnp.sum(err * err, axis=0, keepdims=True)
        dy_ref[...] = err * (1.0 / d)

    return pl.pallas_call(
        body, grid=(t // tm,),
        in_specs=[pl.BlockSpec((tm, d), lambda i: (i, 0)), pl.BlockSpec((tm, d), lambda i: (i, 0))],
        out_specs=[pl.BlockSpec((1, d), lambda i: (0, 0)), pl.BlockSpec((tm, d), lambda i: (i, 0))],
        out_shape=[jax.ShapeDtypeStruct((1, d), F32), jax.ShapeDtypeStruct((t, d), F32)],
        compiler_params=_cp("arbitrary"), name="loss_head")(y, target)


def _place():
    return lax.axis_index("x"), lax.axis_index("y"), lax.axis_index("c")


def _peer(k):
    x, y, c = _place()
    px = 1 - x if (k >> 2) & 1 else x
    py = 1 - y if (k >> 1) & 1 else y
    pc = 1 - c if k & 1 else c
    return (px, py, pc), 4 * px + 2 * py + pc


_ANY = pl.BlockSpec(memory_space=pl.ANY)
_SEM = pl.BlockSpec(memory_space=pltpu.SEMAPHORE)
_EFFECT = pltpu.SideEffectType.DATAFLOW_SIDE_EFFECTING


def _me():
    x, y, c = _place()
    return 4 * x + 2 * y + c


def _remote_copy(ins, lands, scatter, send_sems, recv_sems, a, k, arriving):
    pid, pidx = _peer(k)
    return pltpu.make_async_remote_copy(src_ref=ins[a].at[pidx] if scatter[a] else ins[a],
                                        dst_ref=lands[a].at[pidx if arriving else _me()],
                                        send_sem=send_sems.at[a * N_DEV + k], recv_sem=recv_sems.at[a * N_DEV + k],
                                        device_id=pid, device_id_type=MESH)


def _local_copy(ins, lands, scatter, loc_sems, a):
    me = _me()
    return pltpu.make_async_copy(ins[a].at[me] if scatter[a] else ins[a], lands[a].at[me], loc_sems.at[a])


def exchange_start(arrays, scatter, name, after):
    n = len(arrays)
    lands = [lax.empty(a.shape if s else (N_DEV,) + a.shape, a.dtype) for a, s in zip(arrays, scatter)]

    def body(*refs):
        ins, lds = refs[:n], refs[n:2 * n]
        send_sems, recv_sems, loc_sems = refs[2 * n + 1:2 * n + 4]
        token = refs[-1]
        for k in range(1, N_DEV):
            for a in range(n):
                _remote_copy(ins, lds, scatter, send_sems, recv_sems, a, k, False).start()
        for a in range(n):
            _local_copy(ins, lds, scatter, loc_sems, a).start()
        token[...] = jnp.zeros_like(token)

    hbm = lambda a: pltpu.HBM(a.shape, a.dtype)
    res = pl.pallas_call(
        body, name=name,
        in_specs=[_ANY] * (2 * n + 1),
        out_specs=[_SEM, _SEM, _SEM] + [_ANY] * (2 * n) + [pl.BlockSpec(memory_space=pltpu.VMEM)],
        out_shape=[pltpu.SemaphoreType.DMA((n * N_DEV,)), pltpu.SemaphoreType.DMA((n * N_DEV,)),
                   pltpu.SemaphoreType.DMA((n,))]
                  + [hbm(a) for a in arrays] + [hbm(a) for a in lands] + [jax.ShapeDtypeStruct((8, 128), F32)],
        input_output_aliases={i: 3 + i for i in range(2 * n)},
        compiler_params=pltpu.CompilerParams(has_side_effects=_EFFECT),
    )(*[pltpu.with_memory_space_constraint(a, pltpu.HBM) for a in list(arrays) + lands], after)
    return res[0:3], res[3:3 + n], res[3 + n:3 + 2 * n], res[-1]


def exchange_wait(sems, arrays, lands, scatter, after, name):
    n = len(arrays)

    def body(*refs):
        ins, lds = refs[:n], refs[n:2 * n]
        ssem, rsem, lsem = refs[2 * n:2 * n + 3]
        for a in range(n):
            _local_copy(ins, lds, scatter, lsem, a).wait()
        for k in range(1, N_DEV):
            for a in range(n):
                _remote_copy(ins, lds, scatter, ssem, rsem, a, k, True).wait_recv()
        for k in range(1, N_DEV):
            for a in range(n):
                _remote_copy(ins, lds, scatter, ssem, rsem, a, k, False).wait_send()

    hbm = lambda a: pltpu.HBM(a.shape, a.dtype)
    res = pl.pallas_call(
        body, name=name,
        in_specs=[_ANY] * (2 * n) + [_SEM, _SEM, _SEM, _ANY],
        out_specs=[_ANY] * (2 * n),
        out_shape=[hbm(a) for a in arrays] + [hbm(a) for a in lands],
        input_output_aliases={i: i for i in range(2 * n)},
        compiler_params=pltpu.CompilerParams(has_side_effects=_EFFECT),
    )(*arrays, *lands, *sems, after)
    return list(res[n:])


def exchange_begin(arrays, scatter, name, after):
    sems, arrays_thru, lands_thru, token = exchange_start(arrays, scatter, name + "_start", after)
    return (sems, arrays_thru, lands_thru, scatter, name), token


def exchange_end(state, after):
    sems, arrays_thru, lands_thru, scatter, name = state
    return exchange_wait(sems, arrays_thru, lands_thru, scatter, after, name + "_wait")


def adamw_reduce(slots, w, m, v, l, name, after=None, prev=None):
    nl, r, c = w.shape
    tr = r
    while tr * c * 4 > (1 << 20) and tr % 16 == 0:
        tr //= 2
    bc1 = 1.0 - ADAM_B1 ** ADAM_STEP
    bc2 = 1.0 - ADAM_B2 ** ADAM_STEP

    def body(s_ref, w_ref, m_ref, v_ref, *rest):
        g_ref, d_ref, nm_ref, nv_ref = rest[-4:]
        g = s_ref[0].astype(F32)
        for j in range(1, N_DEV):
            g = g + s_ref[j].astype(F32)
        nm = ADAM_B1 * m_ref[...] + (1.0 - ADAM_B1) * g
        nv = ADAM_B2 * v_ref[...] + (1.0 - ADAM_B2) * (g * g)
        g_ref[...] = g
        nm_ref[...] = nm
        nv_ref[...] = nv
        d_ref[...] = -ADAM_LR * ((nm / bc1) / (jnp.sqrt(nv / bc2) + ADAM_EPS) + ADAM_WD * w_ref[...])

    blk = lambda: pl.BlockSpec((None, tr, c), lambda i: (l, i, 0))
    extra = ([] if after is None else [after]) + ([] if prev is None else list(prev))
    first_prev = 4 + (after is not None)
    return pl.pallas_call(
        body, grid=(r // tr,),
        in_specs=[pl.BlockSpec((N_DEV, tr, c), lambda i: (0, i, 0)), blk(), blk(), blk()] + [_ANY] * len(extra),
        out_specs=[blk(), blk(), blk(), blk()],
        out_shape=[jax.ShapeDtypeStruct((nl, r, c), F32)] * 4,
        input_output_aliases={} if prev is None else {first_prev + j: j for j in range(4)},
        compiler_params=_cp("parallel"), name=name)(slots, w, m, v, *extra)


BIG = ("ffn1_w_in", "ffn1_w_out", "w_in", "gdn_conv", "w_out", "ffn2_w_in", "ffn2_w_out")
GROUPS = (BIG[0:2], BIG[2:5], BIG[5:7])
SMALL = ("ffn1_norm", "mix_norm", "fox_q_norm", "fox_k_norm", "fox_f_bias", "gdn_a_log", "gdn_dt_bias",
         "gdn_out_norm", "ffn2_norm")
WEIGHTS = ("ffn1_norm", "ffn1_w_in", "ffn1_w_out", "mix_norm", "w_in", "fox_q_norm", "fox_k_norm", "fox_f_bias",
           "gdn_conv", "gdn_a_log", "gdn_dt_bias", "gdn_out_norm", "w_out", "ffn2_norm", "ffn2_w_in", "ffn2_w_out")
IN_COLS = (("fq", 512), ("fk", 512), ("fv", 512), ("ff", 8), ("gq", 512), ("gk", 512), ("gv", 512),
           ("ga", 4), ("gb", 4), ("gg", 512))
MY_BIG = ("fq", "fk", "fv", "gq", "gk", "gv", "gg")
MY_SMALL = ("ff", "ga", "gb")
SMALL_ROWS = 8 * 128


def _in_cols_to_mine(w):
    off, parts = 0, {}
    for nm, wd in IN_COLS:
        parts[nm] = w[:, off:off + wd]
        off += wd
    small = jnp.concatenate([parts[nm] for nm in MY_SMALL], axis=1)
    small = jnp.pad(small, ((0, 0), (0, 128 - small.shape[1])))
    return jnp.concatenate([parts[nm] for nm in MY_BIG] + [small], axis=1)


def _in_cols_from_mine(g):
    parts = {nm: g[:, i * 512:(i + 1) * 512] for i, nm in enumerate(MY_BIG)}
    off = N_BIG
    for nm in MY_SMALL:
        wd = dict(IN_COLS)[nm]
        parts[nm] = g[:, off:off + wd]
        off += wd
    return jnp.concatenate([parts[nm] for nm, _ in IN_COLS], axis=1)


def _pack_small(vals):
    rows = []
    nl = vals[SMALL[0]].shape[0]
    for l in range(nl):
        for nm in SMALL:
            v = vals[nm][l].reshape(-1)
            pad = (-v.shape[0]) % SMALL_ROWS
            rows.append(jnp.pad(v, (0, pad)).reshape(-1, 128))
    return jnp.concatenate(rows, axis=0)


def _unpack_small(packed, like):
    out = {nm: [] for nm in SMALL}
    row = 0
    nl = like[SMALL[0]].shape[0]
    for l in range(nl):
        for nm in SMALL:
            n = like[nm].shape[1]
            nr = -(-n // SMALL_ROWS) * 8
            out[nm].append(packed[row:row + nr].reshape(-1)[:n])
            row += nr
    return {nm: jnp.stack(v) for nm, v in out.items()}


def kernel(x, ffn1_norm, ffn1_w_in, ffn1_w_out, mix_norm, w_in, fox_q_norm, fox_k_norm, fox_f_bias, gdn_conv, gdn_a_log, gdn_dt_bias, gdn_out_norm, w_out, ffn2_norm, ffn2_w_in, ffn2_w_out, loss_target, m_ffn1_norm, m_ffn1_w_in, m_ffn1_w_out, m_mix_norm, m_w_in, m_fox_q_norm, m_fox_k_norm, m_fox_f_bias, m_gdn_conv, m_gdn_a_log, m_gdn_dt_bias, m_gdn_out_norm, m_w_out, m_ffn2_norm, m_ffn2_w_in, m_ffn2_w_out, v_ffn1_norm, v_ffn1_w_in, v_ffn1_w_out, v_mix_norm, v_w_in, v_fox_q_norm, v_fox_k_norm, v_fox_f_bias, v_gdn_conv, v_gdn_a_log, v_gdn_dt_bias, v_gdn_out_norm, v_w_out, v_ffn2_norm, v_ffn2_w_in, v_ffn2_w_out):
    wts = dict(ffn1_norm=ffn1_norm, ffn1_w_in=ffn1_w_in, ffn1_w_out=ffn1_w_out, mix_norm=mix_norm, w_in=w_in,
               fox_q_norm=fox_q_norm, fox_k_norm=fox_k_norm, fox_f_bias=fox_f_bias, gdn_conv=gdn_conv,
               gdn_a_log=gdn_a_log, gdn_dt_bias=gdn_dt_bias, gdn_out_norm=gdn_out_norm, w_out=w_out,
               ffn2_norm=ffn2_norm, ffn2_w_in=ffn2_w_in, ffn2_w_out=ffn2_w_out)
    mom = dict(ffn1_norm=m_ffn1_norm, ffn1_w_in=m_ffn1_w_in, ffn1_w_out=m_ffn1_w_out, mix_norm=m_mix_norm, w_in=m_w_in,
               fox_q_norm=m_fox_q_norm, fox_k_norm=m_fox_k_norm, fox_f_bias=m_fox_f_bias, gdn_conv=m_gdn_conv,
               gdn_a_log=m_gdn_a_log, gdn_dt_bias=m_gdn_dt_bias, gdn_out_norm=m_gdn_out_norm, w_out=m_w_out,
               ffn2_norm=m_ffn2_norm, ffn2_w_in=m_ffn2_w_in, ffn2_w_out=m_ffn2_w_out)
    var = dict(ffn1_norm=v_ffn1_norm, ffn1_w_in=v_ffn1_w_in, ffn1_w_out=v_ffn1_w_out, mix_norm=v_mix_norm, w_in=v_w_in,
               fox_q_norm=v_fox_q_norm, fox_k_norm=v_fox_k_norm, fox_f_bias=v_fox_f_bias, gdn_conv=v_gdn_conv,
               gdn_a_log=v_gdn_a_log, gdn_dt_bias=v_gdn_dt_bias, gdn_out_norm=v_gdn_out_norm, w_out=v_w_out,
               ffn2_norm=v_ffn2_norm, ffn2_w_in=v_ffn2_w_in, ffn2_w_out=v_ffn2_w_out)
    nb, seq, d = x.shape
    t = nb * seq
    depth = ffn1_norm.shape[0]

    stages = [(l, gi) for l in range(depth) for gi in range(len(GROUPS))]

    def shards_of(l, gi):
        return [wts[nm][l] if nm == "gdn_conv" else wts[nm][l].astype(BF16) for nm in GROUPS[gi]]

    def behind(nw, token):
        return nw if token is None else nw + token[0:1, 0:1]

    def small_params(l):
        return dict(
            n1=ffn1_norm[l][None], nmix=mix_norm[l][None], n2=ffn2_norm[l][None],
            qw=fox_q_norm[l][None], kw=fox_k_norm[l][None], onw=gdn_out_norm[l][None],
            gp=jnp.concatenate([
                jnp.concatenate([fox_f_bias[l], gdn_dt_bias[l], jnp.zeros((116,), F32)])[None],
                jnp.concatenate([jnp.zeros((8,), F32), gdn_a_log[l], jnp.zeros((116,), F32)])[None],
                jnp.zeros((6, 128), F32)], axis=0))

    h = x.reshape(t, d)
    state, token = exchange_begin(shards_of(0, 0), [False] * len(GROUPS[0]), "gather_0", ffn1_norm)
    landed = exchange_end(state, token)
    saved = [dict(p=small_params(l)) for l in range(depth)]
    for k, (l, gi) in enumerate(stages):
        s, w, token = saved[l], landed, None
        p = s["p"]
        if k + 1 < len(stages):
            nl, ng = stages[k + 1]
            state, token = exchange_begin(shards_of(nl, ng), [False] * len(GROUPS[ng]), f"gather_{k + 1}", landed[0])
        if gi == 0:
            fb = w[0].shape[2]
            p["w1i"], p["w1o"] = w[0].reshape(2, 4, d, fb), w[1].reshape(4, fb, d)
            s["x0"] = h
            h, *s["ffn1"] = ffn_fwd(h, behind(p["n1"], token), p["w1i"], p["w1o"])
            s["x1"] = h
        elif gi == 1:
            p["wi"] = _in_cols_to_mine(w[0].transpose(1, 0, 2).reshape(d, -1))
            p["cw"] = w[1].transpose(1, 0, 2).reshape(CONV_W, -1)
            p["wo"] = w[2].reshape(d, d)
            proj, hn = inproj_fwd(h, behind(p["nmix"], token), p["wi"])
            gates = gates_fwd(proj, p["gp"], seq)
            yf, lse = attn_fwd(proj, gates, p["qw"], p["kw"], seq, tq=min(seq, ATTN_TQ_FWD))
            qh, kh, vh = gdn_pre_fwd(proj, p["cw"], seq)
            inv = gdn_inv(kh, gates, seq)
            yg, st = gdn_fwd(qh, kh, vh, gates, proj, p["onw"], inv, seq)
            h, ycat = outproj_fwd(h, yf, yg, p["wo"])
            s.update(x2=h, proj=proj, hn=hn, gates=gates, yf=yf, lse=lse, qh=qh, kh=kh, vh=vh, st=st, inv=inv, ycat=ycat)
        else:
            fb = w[0].shape[2]
            p["w2i"], p["w2o"] = w[0].reshape(2, 4, d, fb), w[1].reshape(4, fb, d)
            h, *s["ffn2"] = ffn_fwd(h, behind(p["n2"], token), p["w2i"], p["w2o"])
        if k + 1 < len(stages):
            landed = exchange_end(state, h)

    sq, dh = loss_head(h, loss_target.reshape(t, d))
    loss = lax.psum(0.5 * jnp.sum(sq) / d, ("x", "y", "c"))

    got = [None] * len(stages)
    pending, token = None, None
    gsmall = {nm: [None] * depth for nm in SMALL}
    for k in reversed(range(len(stages))):
        l, gi = stages[k]
        s = saved[l]
        p = s["p"]
        if gi != 1:
            nw, xin, wi_, wo_, nm_n, (xn, gu, hh) = (
                (p["n1"], s["x0"], p["w1i"], p["w1o"], "ffn1_norm", s["ffn1"]) if gi == 0 else
                (p["n2"], s["x2"], p["w2i"], p["w2o"], "ffn2_norm", s["ffn2"]))
            dh, dn, dgu, dyh = ffn_bwd(xin, dh, behind(nw, token), gu, wi_, wo_)
            g_in, g_out = wgrad_ffn_in(xn, dgu), wgrad_ffn_out(hh, dyh)
            send = [g_in.reshape(N_DEV, d, g_in.shape[3]), g_out.reshape(N_DEV, -1, d)]
            gsmall[nm_n][l] = dn[0]
        else:
            dyf, dyg, dyb = outproj_bwd(dh, p["wo"], token)
            g_wo = wgrad_2d(s["ycat"], dyb, 512, "wgrad_w_out")
            dq, dk, dv, dga, dqw, dkw = attn_bwd(s["proj"], s["gates"], p["qw"], p["kw"], s["yf"], s["lse"], dyf, seq,
                                                 tq=min(seq, ATTN_TQ_BWD))
            dqh, dkh, dvh, dgg, dgb, donw = gdn_bwd(s["qh"], s["kh"], s["vh"], s["gates"], s["proj"], p["onw"],
                                                     s["inv"], s["st"], dyg, seq)
            dxq, dxk, dxv, dwq, dwk, dwv = gdn_pre_bwd(s["proj"], p["cw"], dqh, dkh, dvh, seq)
            dsm, dgp = gates_bwd(s["proj"], p["gp"], dga, dgb, seq)
            dh, dnmix, dproj = inproj_bwd(s["x1"], dh, p["nmix"], p["wi"], [dq, dk, dv, dxq, dxk, dxv, dgg, dsm])
            g_wi = wgrad_2d(s["hn"], dproj, 512, "wgrad_w_in", F32)
            g_cw = jnp.concatenate([dwq, dwk, dwv], axis=1)
            send = [_in_cols_from_mine(g_wi).reshape(d, N_DEV, -1).transpose(1, 0, 2),
                    g_cw.reshape(CONV_W, N_DEV, -1).transpose(1, 0, 2), g_wo.reshape(N_DEV, -1, d)]
            for nm, val in (("mix_norm", dnmix[0]), ("fox_q_norm", dqw[0]), ("fox_k_norm", dkw[0]),
                            ("fox_f_bias", dgp[0, 0:8]), ("gdn_a_log", dgp[1, 8:12]), ("gdn_dt_bias", dgp[0, 8:12]),
                            ("gdn_out_norm", donw[0])):
                gsmall[nm][l] = val
        flags = [True] * len(send)
        if k == 0:
            send.append(_pack_small({nm: jnp.stack(v) for nm, v in gsmall.items()}))
            flags.append(False)
        prev = dh
        if pending is not None:
            got[pending[1]] = exchange_end(pending[0], dh)
            prev = got[pending[1]][0]
        state, token = exchange_begin(send, flags, f"exchange_grads_{k}", prev)
        pending = (state, k)
    grad_x = dh.reshape(nb, seq, d)

    res = {}

    def update_stage(k, slots, after):
        l, gi = stages[k]
        for i, nm in enumerate(GROUPS[gi]):
            r, c = wts[nm].shape[1:]
            res[nm] = adamw_reduce(slots[i].reshape(N_DEV, r, c), wts[nm], mom[nm], var[nm], l, f"adamw_{nm}_{l}",
                                   after, res.get(nm))
            if after is not None:
                after = res[nm][0]
        return after

    last = token
    for k in range(1, len(stages)):
        last = update_stage(k, got[k], last)
    got[0] = exchange_end(pending[0], last)
    update_stage(0, got[0], None)
    small_like = {nm: wts[nm] for nm in SMALL}
    sm = adamw_reduce(got[0][-1], _pack_small(small_like)[None], _pack_small({nm: mom[nm] for nm in SMALL})[None],
                      _pack_small({nm: var[nm] for nm in SMALL})[None], 0, "adamw_small")
    sm = [_unpack_small(a[0], small_like) for a in sm]
    for nm in SMALL:
        res[nm] = [sm[j][nm] for j in range(4)]
    return (loss, grad_x, *[res[nm][0] for nm in WEIGHTS], *[res[nm][1] for nm in WEIGHTS],
            *[res[nm][2] for nm in WEIGHTS], *[res[nm][3] for nm in WEIGHTS])
```

```python
import functools

import jax
import jax.numpy as jnp
from jax import lax
from jax.experimental import pallas as pl
from jax.experimental.pallas import tpu as pltpu

F32 = jnp.float32
BF16 = jnp.bfloat16
EPS = 1e-6
N_DEV = 8
MESH = pl.DeviceIdType.MESH
HIGHEST = lax.Precision.HIGHEST
VMEM_LIMIT = 56 * 1024 * 1024

FOX_HEADS, FOX_DH = 8, 64
GDN_HEADS, GDN_DH = 4, 128
CHUNK = 64
CONV_W = 4

ADAM_LR, ADAM_B1, ADAM_B2, ADAM_EPS, ADAM_WD, ADAM_STEP = 0.001, 0.9, 0.999, 1e-08, 0.01, 10


def _cp(*sem):
    return pltpu.CompilerParams(dimension_semantics=sem, vmem_limit_bytes=VMEM_LIMIT)


def _dot(a, b):
    return jnp.dot(a, b, preferred_element_type=F32)


def _dot_nt(a, b):
    return lax.dot_general(a, b, (((1,), (1,)), ((), ())), preferred_element_type=F32)


def _dot_tn(a, b):
    return lax.dot_general(a, b, (((0,), (0,)), ((), ())), preferred_element_type=F32)


def _rstd(xf):
    return lax.rsqrt(jnp.mean(xf * xf, axis=-1, keepdims=True) + EPS)


def _rms_bwd(xf, r, dyn):
    return r * dyn - xf * (r * r * r) * jnp.mean(dyn * xf, axis=-1, keepdims=True)


def ffn_fwd(x, nw, w_in, w_out, tm=1024, rc=1024):
    t, d = x.shape
    nj, fb = w_out.shape[0], w_out.shape[1]
    tm = min(tm, t)
    rc = min(rc, tm)

    def body(x_ref, nw_ref, wi_ref, wo_ref, o_ref, xn_ref, gu_ref, h_ref, acc_ref):
        j = pl.program_id(1)

        @pl.when(j == 0)
        def _():
            xf = x_ref[...]
            xn_ref[...] = (xf * _rstd(xf) * nw_ref[...]).astype(BF16)
            acc_ref[...] = jnp.zeros_like(acc_ref)

        rows = [slice(c * rc, (c + 1) * rc) for c in range(tm // rc)]
        gs = [_dot(xn_ref[r, :], wi_ref[0]) for r in rows]
        us = [_dot(xn_ref[r, :], wi_ref[1]) for r in rows]
        hs = []
        for g, u, r in zip(gs, us, rows):
            sg = jax.nn.sigmoid(g)
            silu = g * sg
            h = (silu * u).astype(BF16)
            gu_ref[0, r, :] = (u * (sg * (1.0 + g * (1.0 - sg)))).astype(BF16)
            gu_ref[1, r, :] = silu.astype(BF16)
            h_ref[r, :] = h
            hs.append(h)
        for h, r in zip(hs, rows):
            acc_ref[r, :] += _dot(h, wo_ref[...])

        @pl.when(j == nj - 1)
        def _():
            o_ref[...] = x_ref[...] + 0.5 * acc_ref[...]

    return pl.pallas_call(
        body, grid=(t // tm, nj),
        in_specs=[pl.BlockSpec((tm, d), lambda i, j: (i, 0)),
                  pl.BlockSpec((1, d), lambda i, j: (0, 0)),
                  pl.BlockSpec((2, None, d, fb), lambda i, j: (0, j, 0, 0)),
                  pl.BlockSpec((None, fb, d), lambda i, j: (j, 0, 0))],
        out_specs=[pl.BlockSpec((tm, d), lambda i, j: (i, 0)),
                   pl.BlockSpec((tm, d), lambda i, j: (i, 0)),
                   pl.BlockSpec((2, None, tm, fb), lambda i, j: (0, j, i, 0)),
                   pl.BlockSpec((None, tm, fb), lambda i, j: (j, i, 0))],
        out_shape=[jax.ShapeDtypeStruct((t, d), F32), jax.ShapeDtypeStruct((t, d), BF16),
                   jax.ShapeDtypeStruct((2, nj, t, fb), BF16), jax.ShapeDtypeStruct((nj, t, fb), BF16)],
        scratch_shapes=[pltpu.VMEM((tm, d), F32)],
        compiler_params=_cp("parallel", "arbitrary"), name="ffn_fwd")(x, nw, w_in, w_out)


def ffn_bwd(x, dy, nw, gu, w_in, w_out, tm=512, rc=256):
    t, d = x.shape
    nj, fb = w_out.shape[0], w_out.shape[1]
    rc = min(rc, tm)

    def body(x_ref, dy_ref, nw_ref, gu_ref, wi_ref, wo_ref,
             dx_ref, dnw_ref, dgu_ref, dyh_ref, acc_ref):
        i, j = pl.program_id(0), pl.program_id(1)

        @pl.when(j == 0)
        def _():
            dyh_ref[...] = (0.5 * dy_ref[...]).astype(BF16)
            acc_ref[...] = jnp.zeros_like(acc_ref)

        @pl.when((i == 0) & (j == 0))
        def _():
            dnw_ref[...] = jnp.zeros_like(dnw_ref)

        rows = [slice(c * rc, (c + 1) * rc) for c in range(tm // rc)]
        dhs = [_dot_nt(dyh_ref[r, :], wo_ref[...]) for r in rows]
        dgs = [(dh * gu_ref[0, r, :].astype(F32)).astype(BF16) for dh, r in zip(dhs, rows)]
        dus = [(dh * gu_ref[1, r, :].astype(F32)).astype(BF16) for dh, r in zip(dhs, rows)]
        for dg, du, r in zip(dgs, dus, rows):
            dgu_ref[0, r, :] = dg
            dgu_ref[1, r, :] = du
        for dg, du, r in zip(dgs, dus, rows):
            acc_ref[r, :] += _dot_nt(dg, wi_ref[0]) + _dot_nt(du, wi_ref[1])

        @pl.when(j == nj - 1)
        def _():
            xf = x_ref[...]
            r = _rstd(xf)
            dxn = acc_ref[...]
            dnw_ref[...] += jnp.sum(dxn * xf * r, axis=0, keepdims=True)
            dx_ref[...] = _rms_bwd(xf, r, dxn * nw_ref[...]) + dy_ref[...]

    return pl.pallas_call(
        body, grid=(t // tm, nj),
        in_specs=[pl.BlockSpec((tm, d), lambda i, j: (i, 0)),
                  pl.BlockSpec((tm, d), lambda i, j: (i, 0)),
                  pl.BlockSpec((1, d), lambda i, j: (0, 0)),
                  pl.BlockSpec((2, None, tm, fb), lambda i, j: (0, j, i, 0)),
                  pl.BlockSpec((2, None, d, fb), lambda i, j: (0, j, 0, 0)),
                  pl.BlockSpec((None, fb, d), lambda i, j: (j, 0, 0))],
        out_specs=[pl.BlockSpec((tm, d), lambda i, j: (i, 0)),
                   pl.BlockSpec((1, d), lambda i, j: (0, 0)),
                   pl.BlockSpec((2, None, tm, fb), lambda i, j: (0, j, i, 0)),
                   pl.BlockSpec((tm, d), lambda i, j: (i, 0))],
        out_shape=[jax.ShapeDtypeStruct((t, d), F32),
                   jax.ShapeDtypeStruct((1, d), F32),
                   jax.ShapeDtypeStruct((2, nj, t, fb), BF16),
                   jax.ShapeDtypeStruct((t, d), BF16)],
        scratch_shapes=[pltpu.VMEM((tm, d), F32)],
        compiler_params=_cp("arbitrary", "arbitrary"), name="ffn_bwd")(x, dy, nw, gu, w_in, w_out)


def _wgrad_call(a, b, a_spec, b_spec, out_shape, out_spec, grid, name, out_dtype=BF16):
    last = len(grid) - 1
    acc_shape = tuple(s for s in out_spec.block_shape if s is not None)

    def body(a_ref, b_ref, o_ref, acc_ref):
        @pl.when(pl.program_id(last) == 0)
        def _():
            acc_ref[...] = jnp.zeros_like(acc_ref)

        acc_ref[...] += _dot_tn(a_ref[...], b_ref[...])

        @pl.when(pl.program_id(last) == grid[last] - 1)
        def _():
            o_ref[...] = acc_ref[...].astype(o_ref.dtype)

    sem = ("parallel",) * last + ("arbitrary",)
    return pl.pallas_call(body, grid=grid, in_specs=[a_spec, b_spec], out_specs=out_spec,
                          out_shape=jax.ShapeDtypeStruct(out_shape, out_dtype),
                          scratch_shapes=[pltpu.VMEM(acc_shape, F32)],
                          compiler_params=_cp(*sem), name=name)(a, b)


WGRAD_TM = 1024


def wgrad_ffn_in(xn, dgu, tm=WGRAD_TM):
    t, d = xn.shape
    _, nj, _, fb = dgu.shape
    tm = min(tm, t)
    return _wgrad_call(xn, dgu,
                       pl.BlockSpec((tm, d), lambda p, j, k: (k, 0)),
                       pl.BlockSpec((None, None, tm, fb), lambda p, j, k: (p, j, k, 0)),
                       (2, nj, d, fb), pl.BlockSpec((None, None, d, fb), lambda p, j, k: (p, j, 0, 0)),
                       (2, nj, t // tm), "wgrad_ffn_in")


def wgrad_ffn_out(h, dyh, tm=WGRAD_TM):
    nj, t, fb = h.shape
    d = dyh.shape[1]
    tm = min(tm, t)
    return _wgrad_call(h, dyh,
                       pl.BlockSpec((None, tm, fb), lambda j, k: (j, k, 0)),
                       pl.BlockSpec((tm, d), lambda j, k: (k, 0)),
                       (nj, fb, d), pl.BlockSpec((None, fb, d), lambda j, k: (j, 0, 0)),
                       (nj, t // tm), "wgrad_ffn_out")


def wgrad_2d(a, b, tk, name, out_dtype=BF16, tm=512):
    t, k = a.shape
    n = b.shape[1]
    return _wgrad_call(a, b,
                       pl.BlockSpec((tm, tk), lambda c, s: (s, c)),
                       pl.BlockSpec((tm, n), lambda c, s: (s, 0)),
                       (k, n), pl.BlockSpec((tk, n), lambda c, s: (c, 0)),
                       (k // tk, t // tm), name, out_dtype)


N_BIG = 7 * 512
N_PROJ = N_BIG + 128
COL_SMALL = N_BIG // 128


def inproj_fwd(x, nw, w, tm=256):
    t, d = x.shape
    n = w.shape[1]

    def body(x_ref, nw_ref, w_ref, p_ref, hn_ref):
        xf = x_ref[...]
        hn = (xf * _rstd(xf) * nw_ref[...]).astype(BF16)
        hn_ref[...] = hn
        p_ref[...] = _dot(hn, w_ref[...])

    return pl.pallas_call(
        body, grid=(t // tm,),
        in_specs=[pl.BlockSpec((tm, d), lambda i: (i, 0)), pl.BlockSpec((1, d), lambda i: (0, 0)),
                  pl.BlockSpec((d, n), lambda i: (0, 0))],
        out_specs=[pl.BlockSpec((tm, n), lambda i: (i, 0)), pl.BlockSpec((tm, d), lambda i: (i, 0))],
        out_shape=[jax.ShapeDtypeStruct((t, n), F32), jax.ShapeDtypeStruct((t, d), BF16)],
        compiler_params=_cp("parallel"), name="inproj_fwd")(x, nw, w)


def inproj_bwd(x, dres, nw, w, dparts, tm=256):
    t, d = x.shape
    n = w.shape[1]
    widths = [p.shape[1] for p in dparts]
    assert sum(widths) == n

    def body(x_ref, dres_ref, nw_ref, w_ref, *rest):
        part_refs, (dx_ref, dnw_ref, dp_ref) = rest[:len(widths)], rest[len(widths):]

        @pl.when(pl.program_id(0) == 0)
        def _():
            dnw_ref[...] = jnp.zeros_like(dnw_ref)

        dp = jnp.concatenate([r[...].astype(BF16) for r in part_refs], axis=1)
        dp_ref[...] = dp
        dhn = _dot_nt(dp, w_ref[...])
        xf = x_ref[...]
        r = _rstd(xf)
        dnw_ref[...] += jnp.sum(dhn * xf * r, axis=0, keepdims=True)
        dx_ref[...] = _rms_bwd(xf, r, dhn * nw_ref[...]) + dres_ref[...]

    return pl.pallas_call(
        body, grid=(t // tm,),
        in_specs=[pl.BlockSpec((tm, d), lambda i: (i, 0)), pl.BlockSpec((tm, d), lambda i: (i, 0)),
                  pl.BlockSpec((1, d), lambda i: (0, 0)), pl.BlockSpec((d, n), lambda i: (0, 0))]
                 + [pl.BlockSpec((tm, wd), lambda i: (i, 0)) for wd in widths],
        out_specs=[pl.BlockSpec((tm, d), lambda i: (i, 0)), pl.BlockSpec((1, d), lambda i: (0, 0)),
                   pl.BlockSpec((tm, n), lambda i: (i, 0))],
        out_shape=[jax.ShapeDtypeStruct((t, d), F32), jax.ShapeDtypeStruct((1, d), F32),
                   jax.ShapeDtypeStruct((t, n), BF16)],
        compiler_params=_cp("arbitrary"), name="inproj_bwd")(x, dres, nw, w, *dparts)


def outproj_fwd(x, yf, yg, w, tm=512):
    t, d = x.shape
    hw = yf.shape[1]

    def body(x_ref, yf_ref, yg_ref, w_ref, o_ref, y_ref):
        y = jnp.concatenate([yf_ref[...], yg_ref[...]], axis=1).astype(BF16)
        y_ref[...] = y
        o_ref[...] = x_ref[...] + _dot(y, w_ref[...])

    return pl.pallas_call(
        body, grid=(t // tm,),
        in_specs=[pl.BlockSpec((tm, d), lambda i: (i, 0)), pl.BlockSpec((tm, hw), lambda i: (i, 0)),
                  pl.BlockSpec((tm, hw), lambda i: (i, 0)), pl.BlockSpec((2 * hw, d), lambda i: (0, 0))],
        out_specs=[pl.BlockSpec((tm, d), lambda i: (i, 0)), pl.BlockSpec((tm, 2 * hw), lambda i: (i, 0))],
        out_shape=[jax.ShapeDtypeStruct((t, d), F32), jax.ShapeDtypeStruct((t, 2 * hw), BF16)],
        compiler_params=_cp("parallel"), name="outproj_fwd")(x, yf, yg, w)


def outproj_bwd(dy, w, after=None, tm=512):
    t, d = dy.shape
    hw = w.shape[0] // 2
    extra = [] if after is None else [after]

    def body(dy_ref, w_ref, *rest):
        df_ref, dg_ref, dyb_ref = rest[-3:]
        dyb = dy_ref[...].astype(BF16)
        dyb_ref[...] = dyb
        dyy = _dot_nt(dyb, w_ref[...])
        df_ref[...] = dyy[:, :hw]
        dg_ref[...] = dyy[:, hw:]

    return pl.pallas_call(
        body, grid=(t // tm,),
        in_specs=[pl.BlockSpec((tm, d), lambda i: (i, 0)), pl.BlockSpec((2 * hw, d), lambda i: (0, 0))]
                 + [pl.BlockSpec(memory_space=pl.ANY)] * len(extra),
        out_specs=[pl.BlockSpec((tm, hw), lambda i: (i, 0)), pl.BlockSpec((tm, hw), lambda i: (i, 0)),
                   pl.BlockSpec((tm, d), lambda i: (i, 0))],
        out_shape=[jax.ShapeDtypeStruct((t, hw), F32), jax.ShapeDtypeStruct((t, hw), F32),
                   jax.ShapeDtypeStruct((t, d), BF16)],
        compiler_params=_cp("parallel"), name="outproj_bwd")(dy, w, *extra)


def _lane(shape):
    return lax.broadcasted_iota(jnp.int32, shape, 1)


def _row(shape):
    return lax.broadcasted_iota(jnp.int32, shape, 0)


def _gate_terms(val, gp_ref):
    z = val + gp_ref[0:1, :]
    sp = jnp.log(1.0 + jnp.exp(-jnp.abs(z)))
    return z, sp


def gates_fwd(proj, gp, seq, ts=512):
    t = proj.shape[0]
    nb, ns = t // seq, seq // ts

    def body(v_ref, gp_ref, o_ref, carry_ref):
        @pl.when(pl.program_id(1) == 0)
        def _():
            carry_ref[...] = jnp.zeros_like(carry_ref)

        z, sp = _gate_terms(v_ref[...], gp_ref)
        logsig = jnp.minimum(z, 0.0) - sp
        tri = (_row((ts, ts)) >= _lane((ts, ts))).astype(F32)
        cum = jnp.dot(tri, logsig, precision=HIGHEST, preferred_element_type=F32) + carry_ref[0:1, :]
        carry_ref[0:1, :] = cum[ts - 1:ts, :]
        g = -jnp.exp(gp_ref[1:2, :]) * (jnp.maximum(z, 0.0) + sp)
        beta = jax.nn.sigmoid(z)
        lane = _lane((ts, 128))
        o_ref[...] = jnp.where(lane < 8, cum, jnp.where(lane < 12, g, jnp.where(lane < 16, beta, 0.0)))

    return pl.pallas_call(
        body, grid=(nb, ns),
        in_specs=[pl.BlockSpec((ts, 128), lambda b, s: (b * ns + s, COL_SMALL)),
                  pl.BlockSpec((8, 128), lambda b, s: (0, 0))],
        out_specs=pl.BlockSpec((ts, 128), lambda b, s: (b * ns + s, 0)),
        out_shape=jax.ShapeDtypeStruct((t, 128), F32),
        scratch_shapes=[pltpu.VMEM((8, 128), F32)],
        compiler_params=_cp("parallel", "arbitrary"), name="gates_fwd")(proj, gp)


def gates_bwd(proj, gp, dga, dgb, seq, ts=512):
    t = proj.shape[0]
    nb, ns = t // seq, seq // ts

    def body(v_ref, gp_ref, da_ref, db_ref, ds_ref, dgp_ref, carry_ref):
        @pl.when(pl.program_id(1) == 0)
        def _():
            carry_ref[...] = jnp.zeros_like(carry_ref)

        @pl.when((pl.program_id(0) == 0) & (pl.program_id(1) == 0))
        def _():
            dgp_ref[...] = jnp.zeros_like(dgp_ref)

        lane = _lane((ts, 128))
        dgate = jnp.where(lane < 8, da_ref[...], jnp.where(lane < 16, db_ref[...], 0.0))
        z, sp = _gate_terms(v_ref[...], gp_ref)
        triu = (_row((ts, ts)) <= _lane((ts, ts))).astype(F32)
        dlog = jnp.dot(triu, dgate, precision=HIGHEST, preferred_element_type=F32) + carry_ref[0:1, :]
        carry_ref[0:1, :] = dlog[0:1, :]
        sig = jax.nn.sigmoid(z)
        nea = -jnp.exp(gp_ref[1:2, :])
        g = nea * (jnp.maximum(z, 0.0) + sp)
        dz = jnp.where(lane < 8, dlog * (1.0 - sig),
                       jnp.where(lane < 12, dgate * nea * sig, dgate * sig * (1.0 - sig)))
        dz = jnp.where(lane < 16, dz, 0.0)
        ds_ref[...] = dz
        dgp_ref[0:1, :] += jnp.where(lane[0:1] < 12, jnp.sum(dz, axis=0, keepdims=True), 0.0)
        dgp_ref[1:2, :] += jnp.where((lane[0:1] >= 8) & (lane[0:1] < 12), jnp.sum(dgate * g, axis=0, keepdims=True), 0.0)

    rev = lambda b, s: (b * ns + (ns - 1 - s), 0)
    return pl.pallas_call(
        body, grid=(nb, ns),
        in_specs=[pl.BlockSpec((ts, 128), lambda b, s: (b * ns + (ns - 1 - s), COL_SMALL)),
                  pl.BlockSpec((8, 128), lambda b, s: (0, 0)),
                  pl.BlockSpec((ts, 128), rev), pl.BlockSpec((ts, 128), rev)],
        out_specs=[pl.BlockSpec((ts, 128), rev), pl.BlockSpec((8, 128), lambda b, s: (0, 0))],
        out_shape=[jax.ShapeDtypeStruct((t, 128), F32), jax.ShapeDtypeStruct((8, 128), F32)],
        scratch_shapes=[pltpu.VMEM((8, 128), F32)],
        compiler_params=_cp("arbitrary", "arbitrary"), name="gates_bwd")(proj, gp, dga, dgb)


NEG = -1e30
ATTN_TQ_FWD = 1024
ATTN_TQ_BWD = 512


def _pick_lane(tile, idx):
    return jnp.sum(jnp.where(_lane(tile.shape) == idx, tile, 0.0), axis=1, keepdims=True)


def _col_to_row(col, n):
    return jnp.sum(jnp.where(_row((n, n)) == _lane((n, n)), col, 0.0), axis=0, keepdims=True)


def _row_to_col(row, n):
    return jnp.sum(jnp.where(_row((n, n)) == _lane((n, n)), row, 0.0), axis=1, keepdims=True)


def _rows(i, n):
    return pl.ds(pl.multiple_of(i * n, n), n)


LOG2E = 1.4426950408889634
LN2 = 0.6931471805599453


def _rowsum(z, width, passes=2):
    ones = jnp.ones((z.shape[1], width), BF16)
    total, rest = None, z
    for _ in range(passes):
        part = rest.astype(BF16)
        rest = rest - part.astype(F32)
        total = _dot(part, ones) if total is None else total + _dot(part, ones)
    return total


def _split_dot(x, mat, passes):
    total, rest = None, x
    for _ in range(passes):
        part = rest.astype(BF16)
        rest = rest - part.astype(F32)
        total = _dot(part, mat) if total is None else total + _dot(part, mat)
    return total


def _pair_mats(p, dh):
    r, l = _row((128, 128)), _lane((128, 128))
    same = (r < dh) == (l < dh)
    upper = (l >= dh).astype(jnp.int32)
    as_bf16 = lambda m: m.astype(BF16)
    return dict(own=as_bf16(same), other=as_bf16(jnp.logical_not(same)), pick_other=as_bf16(r == 2 * p + 1 - upper),
                swap=as_bf16(((r == 0) & (l >= dh)) | ((r == dh) & (l < dh))))


def _pair_rstd(x2, sel, dh):
    return lax.rsqrt(_split_dot(x2 * x2, sel["own"], 2) * (1.0 / dh) + EPS)


def _pair_aug(cols, n, dh):
    lane = _lane((n, 128))
    li = jnp.where(lane >= dh, lane - dh, lane)
    out = jnp.zeros((n, 128), F32)
    for i, c in enumerate(cols):
        out = jnp.where(li == i, c, out)
    return out


def _rstd_mxu(xf):
    return lax.rsqrt(_rowsum(xf * xf, xf.shape[1]) * (1.0 / xf.shape[1]) + EPS)


def _rms_bwd_mxu(xf, r, dyn):
    return r * dyn - xf * (r * r * r) * (_rowsum(dyn * xf, xf.shape[1]) * (1.0 / xf.shape[1]))


def _pick_lane_mxu(tile, idx, width):
    onehot = (_row((tile.shape[1], width)) == idx).astype(BF16)
    total, rest = None, tile
    for _ in range(3):
        part = rest.astype(BF16)
        rest = rest - part.astype(F32)
        total = _dot(part, onehot) if total is None else total + _dot(part, onehot)
    return total


def _split3(x):
    hi = x.astype(BF16).astype(F32)
    mid = (x - hi).astype(BF16).astype(F32)
    return [hi, mid, (x - hi - mid).astype(BF16).astype(F32)]


def _aug_cols(cols, n, width):
    lane = _lane((n, width))
    out = jnp.zeros((n, width), F32)
    for i, c in enumerate(cols):
        out = jnp.where(lane == i, c, out)
    return out


def _once(shape, index_map):
    return pl.BlockSpec(shape, index_map, pipeline_mode=pl.Buffered(1))


def attn_fwd(proj, gates, qw, kw, seq, tq=256):
    t = proj.shape[0]
    nb, nq, dh = t // seq, seq // tq, FOX_DH
    scale = dh ** -0.5

    def body(q_ref, k_ref, v_ref, g_ref, qw_ref, kw_ref, y_ref, lse_ref, qs, ks, vs):
        p = pl.program_id(1)
        heads = range(2)
        low = _lane((tq, 128)) < dh
        sel = _pair_mats(p, dh)

        def prep(i, _):
            r = _rows(i, tq)
            q2, k2 = q_ref[r, :], k_ref[r, :]
            cc = _split_dot(g_ref[r, :], sel["pick_other"], 3) * LOG2E
            qn = q2 * _pair_rstd(q2, sel, dh) * qw_ref[...] * (scale * LOG2E)
            kn = k2 * _pair_rstd(k2, sel, dh) * kw_ref[...]
            qx = _pair_aug(_split3(cc) + [1.0, 1.0, 1.0], tq, dh)
            kx = _pair_aug([1.0, 1.0, 1.0] + _split3(-cc), tq, dh)
            for hh in heads:
                own = low if hh == 0 else jnp.logical_not(low)
                qs[hh, r, :] = jnp.where(own, qn, qx).astype(BF16)
                ks[hh, r, :] = jnp.where(own, kn, kx).astype(BF16)
            vs[r, :] = v_ref[r, :].astype(BF16)
            return 0

        lax.fori_loop(0, nq, prep, 0)

        def q_tile(i, _):
            r = _rows(i, tq)
            qt = [qs[hh, r, :] for hh in heads]

            def kv_step(j, carry, masked):
                kr = _rows(j, tq)
                vt = vs[kr, :]
                out = []
                for hh in heads:
                    m, l, acc = carry[hh]
                    s = _dot_nt(qt[hh], ks[hh, kr, :])
                    if masked:
                        s = jnp.where(_row((tq, tq)) >= _lane((tq, tq)), s, NEG)
                    m_new = jnp.maximum(m, jnp.max(s, axis=1, keepdims=True))
                    pe = jnp.exp2(s - m_new)
                    a = jnp.exp2(m - m_new)
                    out.append((m_new, a * l + jnp.sum(pe, axis=1, keepdims=True), a * acc + _dot(pe.astype(BF16), vt)))
                return tuple(out)

            one = (jnp.full((tq, 1), NEG, F32), jnp.zeros((tq, 1), F32), jnp.zeros((tq, 128), F32))
            carry = lax.fori_loop(0, i, lambda j, c: kv_step(j, c, False), (one, one))
            (m0, l0, acc0), (m1, l1, acc1) = kv_step(i, carry, True)
            y_ref[r, :] = jnp.where(low, acc0 / l0, acc1 / l1)
            lse_ref[r, :] = jnp.where(low, m0 + jnp.log2(l0), m1 + jnp.log2(l1))
            return 0

        lax.fori_loop(0, nq, q_tile, 0)

    blk = lambda off: _once((seq, 128), lambda b, p: (b, off + p))
    return pl.pallas_call(
        body, grid=(nb, 4),
        in_specs=[blk(0), blk(4), blk(8), _once((seq, 128), lambda b, p: (b, 0)),
                  pl.BlockSpec((1, 128), lambda b, p: (0, 0)), pl.BlockSpec((1, 128), lambda b, p: (0, 0))],
        out_specs=[pl.BlockSpec((seq, 128), lambda b, p: (b, p)), pl.BlockSpec((seq, 128), lambda b, p: (b, p))],
        out_shape=[jax.ShapeDtypeStruct((t, 512), F32), jax.ShapeDtypeStruct((t, 512), F32)],
        scratch_shapes=[pltpu.VMEM((2, seq, 128), BF16), pltpu.VMEM((2, seq, 128), BF16), pltpu.VMEM((seq, 128), BF16)],
        compiler_params=_cp("parallel", "arbitrary"),
        name="attn_fwd")(proj, proj, proj, gates, jnp.tile(qw, (1, 2)), jnp.tile(kw, (1, 2)))


def attn_bwd(proj, gates, qw, kw, y, lse, dy, seq, tq=256):
    t = proj.shape[0]
    nb, nq, dh = t // seq, seq // tq, FOX_DH
    scale = dh ** -0.5

    def body(q_ref, k_ref, v_ref, g_ref, qw_ref, kw_ref, y_ref, lse_ref, dy_ref,
             dq_ref, dk_ref, dv_ref, dg_ref, dqw_ref, dkw_ref,
             qs, ks, vs, dos, dsrow, dqa, dka):
        b, p = pl.program_id(0), pl.program_id(1)

        @pl.when((b == 0) & (p == 0))
        def _():
            dqw_ref[...] = jnp.zeros_like(dqw_ref)
            dkw_ref[...] = jnp.zeros_like(dkw_ref)

        @pl.when(p == 0)
        def _():
            dg_ref[...] = jnp.zeros_like(dg_ref)

        heads = range(2)
        low = _lane((tq, 128)) < dh
        sel = _pair_mats(p, dh)

        def prep(i, _):
            r = _rows(i, tq)
            q2, k2, dy2 = q_ref[r, :], k_ref[r, :], dy_ref[r, :]
            cc = _split_dot(g_ref[r, :], sel["pick_other"], 3) * LOG2E
            lse_x = _split_dot(lse_ref[r, :], sel["swap"], 3)
            delta_x = _split_dot(dy2 * y_ref[r, :], sel["other"], 2)
            qn = q2 * _pair_rstd(q2, sel, dh) * qw_ref[...] * (scale * LOG2E)
            kn = k2 * _pair_rstd(k2, sel, dh) * kw_ref[...]
            qx = _pair_aug(_split3(cc) + [1.0, 1.0, 1.0] + _split3(-lse_x), tq, dh)
            kx = _pair_aug([1.0, 1.0, 1.0] + _split3(-cc) + [1.0, 1.0, 1.0], tq, dh)
            vx = _pair_aug([1.0, 1.0, 1.0], tq, dh)
            dx = _pair_aug(_split3(-delta_x), tq, dh)
            for hh in heads:
                own = low if hh == 0 else jnp.logical_not(low)
                qs[hh, r, :] = jnp.where(own, qn, qx).astype(BF16)
                ks[hh, r, :] = jnp.where(own, kn, kx).astype(BF16)
                vs[hh, r, :] = jnp.where(own, v_ref[r, :], vx).astype(BF16)
                dos[hh, r, :] = jnp.where(own, dy2, dx).astype(BF16)
                dsrow[hh, r, :] = jnp.zeros((tq, 1), F32)
                dqa[hh, r, :] = jnp.zeros((tq, 128), F32)
            return 0

        lax.fori_loop(0, nq, prep, 0)

        def kv_tile(j, _):
            kr = _rows(j, tq)
            kt = [ks[hh, kr, :] for hh in heads]
            vt = [vs[hh, kr, :] for hh in heads]

            def q_step(i, carry, masked):
                r = _rows(i, tq)
                out = []
                for hh in heads:
                    dk, dv, dcr = carry[hh]
                    qt, dot = qs[hh, r, :], dos[hh, r, :]
                    s = _dot_nt(qt, kt[hh])
                    if masked:
                        s = jnp.where(_row((tq, tq)) >= _lane((tq, tq)), s, NEG)
                    pe = jnp.exp2(s)
                    ds = pe * _dot_nt(dot, vt[hh])
                    dsb = ds.astype(BF16)
                    dqa[hh, r, :] += _dot(dsb, kt[hh])
                    dsrow[hh, r, :] += jnp.sum(ds, axis=1, keepdims=True)
                    out.append((dk + _dot_tn(dsb, qt), dv + _dot_tn(pe.astype(BF16), dot),
                                dcr - jnp.sum(ds, axis=0, keepdims=True)))
                return tuple(out)

            one = (jnp.zeros((tq, 128), F32), jnp.zeros((tq, 128), F32), jnp.zeros((1, tq), F32))
            carry = q_step(j, (one, one), True)
            (dk0, dv0, dcr0), (dk1, dv1, dcr1) = lax.fori_loop(j + 1, nq, lambda i, c: q_step(i, c, False), carry)
            dka[kr, :] = jnp.where(low, dk0, dk1)
            dv_ref[kr, :] = jnp.where(low, dv0, dv1)
            lane = _lane((tq, 128))
            dg_ref[kr, :] = jnp.where(lane == 2 * p, _row_to_col(dcr0, tq),
                                      jnp.where(lane == 2 * p + 1, _row_to_col(dcr1, tq), dg_ref[kr, :]))
            return 0

        lax.fori_loop(0, nq, kv_tile, 0)

        def post(i, _):
            r = _rows(i, tq)
            q2, k2 = q_ref[r, :], k_ref[r, :]
            rq, rk = _pair_rstd(q2, sel, dh), _pair_rstd(k2, sel, dh)
            dqn = jnp.where(low, dqa[0, r, :], dqa[1, r, :]) * scale
            dkn = dka[r, :] * LN2
            dqw_ref[...] += jnp.sum(dqn * q2 * rq, axis=0, keepdims=True)
            dkw_ref[...] += jnp.sum(dkn * k2 * rk, axis=0, keepdims=True)
            for x2, rr, dyn, o_ref in ((q2, rq, dqn * qw_ref[...], dq_ref), (k2, rk, dkn * kw_ref[...], dk_ref)):
                mean = _split_dot(dyn * x2, sel["own"], 2) * (1.0 / dh)
                o_ref[r, :] = rr * dyn - x2 * (rr * rr * rr) * mean
            lane = _lane((tq, 128))
            dg_ref[r, :] += jnp.where(lane == 2 * p, dsrow[0, r, :], jnp.where(lane == 2 * p + 1, dsrow[1, r, :], 0.0))
            return 0

        lax.fori_loop(0, nq, post, 0)

    blk = lambda off: _once((seq, 128), lambda b, p: (b, off + p))
    own = lambda: _once((seq, 128), lambda b, p: (b, p))
    vec = lambda: pl.BlockSpec((1, 128), lambda b, p: (0, 0))
    res = pl.pallas_call(
        body, grid=(nb, 4),
        in_specs=[blk(0), blk(4), blk(8), _once((seq, 128), lambda b, p: (b, 0)), vec(), vec(), own(), own(), own()],
        out_specs=[own(), own(), own(), _once((seq, 128), lambda b, p: (b, 0)), vec(), vec()],
        out_shape=[jax.ShapeDtypeStruct((t, 512), F32)] * 3
                  + [jax.ShapeDtypeStruct((t, 128), F32), jax.ShapeDtypeStruct((1, 128), F32), jax.ShapeDtypeStruct((1, 128), F32)],
        scratch_shapes=[pltpu.VMEM((2, seq, 128), BF16)] * 4
                       + [pltpu.VMEM((2, seq, 1), F32), pltpu.VMEM((2, seq, 128), F32), pltpu.VMEM((seq, 128), F32)],
        compiler_params=_cp("arbitrary", "arbitrary"),
        name="attn_bwd")(proj, proj, proj, gates, jnp.tile(qw, (1, 2)), jnp.tile(kw, (1, 2)), y, lse, dy)
    return list(res[:4]) + [res[4][:, :dh] + res[4][:, dh:], res[5][:, :dh] + res[5][:, dh:]]


def _silu_grad(c, sg):
    return sg * (1.0 + c * (1.0 - sg))


def _conv(x, w, n):
    row = _row(x.shape)
    c = x * w[CONV_W - 1:CONV_W, :]
    for k in range(CONV_W - 1):
        sh = CONV_W - 1 - k
        c = c + w[k:k + 1, :] * jnp.where(row >= sh, pltpu.roll(x, sh, 0), 0.0)
    return c


def gdn_pre_fwd(proj, cw, seq):
    t = proj.shape[0]
    nb = t // seq
    scale = GDN_DH ** -0.5

    def body(xq_ref, xk_ref, xv_ref, wq_ref, wk_ref, wv_ref, q_ref, k_ref, v_ref):
        def act(x_ref, w_ref):
            c = _conv(x_ref[...], w_ref[...], seq)
            return c * jax.nn.sigmoid(c)

        aq, ak = act(xq_ref, wq_ref), act(xk_ref, wk_ref)
        q_ref[...] = aq * lax.rsqrt(jnp.sum(aq * aq, axis=1, keepdims=True) + EPS) * scale
        k_ref[...] = ak * lax.rsqrt(jnp.sum(ak * ak, axis=1, keepdims=True) + EPS)
        v_ref[...] = act(xv_ref, wv_ref)

    xb = lambda off: pl.BlockSpec((seq, 128), lambda b, h: (b, off + h))
    wb = lambda off: pl.BlockSpec((CONV_W, 128), lambda b, h: (0, off + h))
    ob = lambda: pl.BlockSpec((seq, 128), lambda b, h: (b, h))
    return pl.pallas_call(
        body, grid=(nb, GDN_HEADS),
        in_specs=[xb(12), xb(16), xb(20), wb(0), wb(4), wb(8)],
        out_specs=[ob(), ob(), ob()],
        out_shape=[jax.ShapeDtypeStruct((t, 512), F32)] * 3,
        compiler_params=_cp("parallel", "parallel"), name="gdn_pre_fwd")(proj, proj, proj, cw, cw, cw)


def gdn_pre_bwd(proj, cw, dq, dk, dv, seq):
    t = proj.shape[0]
    nb = t // seq
    scale = GDN_DH ** -0.5

    def body(xq_ref, xk_ref, xv_ref, wq_ref, wk_ref, wv_ref, dq_ref, dk_ref, dv_ref,
             dxq_ref, dxk_ref, dxv_ref, dwq_ref, dwk_ref, dwv_ref):
        first = pl.program_id(1) == 0
        row = _row((seq, 128))

        def one(x_ref, w_ref, dy_ref, dx_ref, dw_ref, norm, sc):
            x, w = x_ref[...], w_ref[...]
            c = _conv(x, w, seq)
            sg = jax.nn.sigmoid(c)
            dy = dy_ref[...]
            if norm:
                a = c * sg
                rs = lax.rsqrt(jnp.sum(a * a, axis=1, keepdims=True) + EPS)
                dy = dy * sc
                da = rs * dy - a * (rs * rs * rs) * jnp.sum(dy * a, axis=1, keepdims=True)
            else:
                da = dy
            dc = da * _silu_grad(c, sg)
            dx = dc * w[CONV_W - 1:CONV_W, :]
            dws = [None] * CONV_W
            dws[CONV_W - 1] = jnp.sum(dc * x, axis=0, keepdims=True)
            for k in range(CONV_W - 1):
                sh = CONV_W - 1 - k
                dx = dx + w[k:k + 1, :] * jnp.where(row < seq - sh, pltpu.roll(dc, seq - sh, 0), 0.0)
                dws[k] = jnp.sum(dc * jnp.where(row >= sh, pltpu.roll(x, sh, 0), 0.0), axis=0, keepdims=True)
            dx_ref[...] = dx
            dwn = jnp.concatenate(dws, axis=0)

            @pl.when(first)
            def _():
                dw_ref[...] = dwn

            @pl.when(jnp.logical_not(first))
            def _():
                dw_ref[...] += dwn

        one(xq_ref, wq_ref, dq_ref, dxq_ref, dwq_ref, True, scale)
        one(xk_ref, wk_ref, dk_ref, dxk_ref, dwk_ref, True, 1.0)
        one(xv_ref, wv_ref, dv_ref, dxv_ref, dwv_ref, False, 1.0)

    xb = lambda off: pl.BlockSpec((seq, 128), lambda h, b: (b, off + h))
    wb = lambda off: pl.BlockSpec((CONV_W, 128), lambda h, b: (0, off + h))
    ob = lambda: pl.BlockSpec((seq, 128), lambda h, b: (b, h))
    return pl.pallas_call(
        body, grid=(GDN_HEADS, nb),
        in_specs=[xb(12), xb(16), xb(20), wb(0), wb(4), wb(8), ob(), ob(), ob()],
        out_specs=[ob(), ob(), ob()] + [pl.BlockSpec((CONV_W, 128), lambda h, b: (0, h))] * 3,
        out_shape=[jax.ShapeDtypeStruct((t, 512), F32)] * 3 + [jax.ShapeDtypeStruct((CONV_W, 512), F32)] * 3,
        compiler_params=_cp("parallel", "arbitrary"), name="gdn_pre_bwd")(proj, proj, proj, cw, cw, cw, dq, dk, dv)


def _b16(x):
    return x.astype(BF16)


@jax.custom_vjp
def _mm(a, b):
    return _dot(_b16(a), _b16(b))


_mm.defvjp(lambda a, b: (_mm(a, b), (a, b)),
           lambda res, g: (_dot_nt(_b16(g), _b16(res[1])), _dot_tn(_b16(res[0]), _b16(g))))


@jax.custom_vjp
def _mm_nt(a, b):
    return _dot_nt(_b16(a), _b16(b))


_mm_nt.defvjp(lambda a, b: (_mm_nt(a, b), (a, b)),
              lambda res, g: (_dot(_b16(g), _b16(res[1])), _dot_tn(_b16(g), _b16(res[0]))))


@jax.custom_vjp
def _mm_tn(a, b):
    return _dot_tn(_b16(a), _b16(b))


_mm_tn.defvjp(lambda a, b: (_mm_tn(a, b), (a, b)),
              lambda res, g: (_dot_nt(_b16(res[1]), _b16(g)), _dot(_b16(res[0]), _b16(g))))


def _dot32(a, b, dims=(((1,), (0,)), ((), ()))):
    def split(x):
        hi = x.astype(BF16)
        return hi, (x - hi.astype(F32)).astype(BF16)

    (ah, al), (bh, bl) = split(a), split(b)
    d = lambda x, y: lax.dot_general(x, y, dims, preferred_element_type=F32)
    return d(ah, bh) + (d(ah, bl) + d(al, bh))


def _inv_fwd_many(mats):
    n = mats[0].shape[0]
    eye = (_row((n, n)) == _lane((n, n))).astype(F32)
    invs, pws = [eye - a for a in mats], list(mats)
    for _ in range(n.bit_length() - 2):
        pws = [_dot32(pw, pw) for pw in pws]
        invs = [inv + _dot32(inv, pw) for inv, pw in zip(invs, pws)]
    return invs


@jax.custom_vjp
def _inv_saved(a, inv):
    return inv


def _inv_saved_bwd(inv, g):
    tg = _dot32(inv, g, (((0,), (0,)), ((), ())))
    return -_dot32(tg, inv, (((1,), (1,)), ((), ()))), jnp.zeros_like(inv)


_inv_saved.defvjp(lambda a, inv: (inv, inv), _inv_saved_bwd)


def _gdn_decay(gcol):
    c = CHUNK
    ri, ci = _row((c, c)), _lane((c, c))
    incl, eye = ri >= ci, ri == ci
    grow = jnp.sum(jnp.where(eye, gcol, 0.0), axis=0, keepdims=True)
    gc = jnp.sum(jnp.where(incl, grow, 0.0), axis=1, keepdims=True)
    gcr = jnp.sum(jnp.where(eye, gc, 0.0), axis=0, keepdims=True)
    gl = jnp.sum(jnp.where(_row((c, 1)) == c - 1, gc, 0.0), axis=0, keepdims=True)
    return gc, gl, jnp.exp(jnp.where(incl, gc - gcr, NEG))


def _gdn_a(k, bcol, decay):
    c = CHUNK
    return jnp.where(_row((c, c)) > _lane((c, c)), _mm_nt(k * bcol, k) * decay, 0.0)


def _gdn_chunk(q, k, v, gcol, bcol, state, gg, nw, inv_saved):
    c = CHUNK
    incl = _row((c, c)) >= _lane((c, c))
    gc, gl, decay = _gdn_decay(gcol)
    kb, vb = k * bcol, v * bcol
    inv = _inv_saved(_gdn_a(k, bcol, decay), inv_saved)
    eg = jnp.exp(gc)
    u = _mm(inv, vb)
    w = _mm(inv, kb * eg)
    pm = jnp.where(incl, _mm_nt(q, k) * decay, 0.0)
    kd = k * jnp.exp(gl - gc)
    qd = q * eg
    v_new = u - _mm(w, state)
    o = _mm(qd, state) + _mm(pm, v_new)
    state_new = state * jnp.exp(gl) + _mm_tn(kd, v_new)
    y = o * _rstd(o) * nw * (gg * jax.nn.sigmoid(gg))
    return y, state_new


_gdn_chunks = jax.vmap(_gdn_chunk, in_axes=(0, 0, 0, 0, 0, 0, 0, None, 0))


def _gdn_chain_inputs(chains, p, r, c, q_ref, k_ref, v_ref, g_ref, gg_ref, inv_ref):
    cols = {nm: [] for nm in ("q", "k", "v", "g", "b", "gg", "inv")}
    for b, hh in chains:
        h = GDN_HPS * p + hh
        ln = slice(hh * 128, (hh + 1) * 128)
        gt = g_ref[b, r, :]
        cols["q"].append(q_ref[b, r, ln])
        cols["k"].append(k_ref[b, r, ln])
        cols["v"].append(v_ref[b, r, ln])
        cols["g"].append(_pick_lane(gt, 8 + h))
        cols["b"].append(_pick_lane(gt, 12 + h))
        cols["gg"].append(gg_ref[b, r, ln])
        cols["inv"].append(inv_ref[b, hh, c])
    return [jnp.stack(cols[nm]) for nm in ("q", "k", "v", "g", "b", "gg", "inv")]


GDN_CB = 8
GDN_HPS = 4


def gdn_inv(k, gates, seq):
    t = k.shape[0]
    nb, nc = t // seq, seq // CHUNK
    rb = GDN_CB * CHUNK
    nsb = seq // rb

    def body(k_ref, g_ref, o_ref):
        h = pl.program_id(1)
        mats = []
        for c in range(GDN_CB):
            r = slice(c * CHUNK, (c + 1) * CHUNK)
            gt = g_ref[r, :]
            _, _, decay = _gdn_decay(_pick_lane(gt, 8 + h))
            mats.append(_gdn_a(k_ref[r, :], _pick_lane(gt, 12 + h), decay))
        for c, inv in enumerate(_inv_fwd_many(mats)):
            o_ref[c] = inv

    return pl.pallas_call(
        body, grid=(nb, GDN_HEADS, nsb),
        in_specs=[pl.BlockSpec((rb, 128), lambda b, h, s: (b * nsb + s, h)),
                  pl.BlockSpec((rb, 128), lambda b, h, s: (b * nsb + s, 0))],
        out_specs=pl.BlockSpec((None, None, GDN_CB, CHUNK, CHUNK), lambda b, h, s: (b, h, s, 0, 0)),
        out_shape=jax.ShapeDtypeStruct((nb, GDN_HEADS, nc, CHUNK, CHUNK), F32),
        compiler_params=_cp("parallel", "parallel", "parallel"), name="gdn_inv")(k, gates)


def _gdn_specs(nb, nsb, cb, rev):
    blk = (lambda s: nsb - 1 - s) if rev else (lambda s: s)
    rb = cb * CHUNK
    pair = lambda off=0: pl.BlockSpec((nb, rb, 128 * GDN_HPS), lambda s, p: (0, blk(s), off + p))
    gate = lambda: pl.BlockSpec((nb, rb, 128), lambda s, p: (0, blk(s), 0))
    mats = lambda n: pl.BlockSpec((nb, GDN_HPS, cb, n, n), lambda s, p: (0, p, blk(s), 0, 0))
    return pair, gate, mats


def gdn_fwd(q, k, v, gates, proj, nw, inv, seq):
    t = q.shape[0]
    nb, nc = t // seq, seq // CHUNK
    cb = GDN_CB
    nsb = nc // cb
    chains = [(b, hh) for b in range(nb) for hh in range(GDN_HPS)]
    nch = len(chains)
    pair, gate, mats = _gdn_specs(nb, nsb, cb, False)

    def body(q_ref, k_ref, v_ref, g_ref, gg_ref, inv_ref, nw_ref, y_ref, st_ref, carry):
        s, p = pl.program_id(0), pl.program_id(1)

        @pl.when(s == 0)
        def _():
            for ci in range(nch):
                carry[p * nch + ci] = jnp.zeros((GDN_DH, GDN_DH), F32)

        def step(c, states):
            r = _rows(c, CHUNK)
            for ci, (b, hh) in enumerate(chains):
                st_ref[b, hh, c] = states[ci]
            ins = _gdn_chain_inputs(chains, p, r, c, q_ref, k_ref, v_ref, g_ref, gg_ref, inv_ref)
            y, states = _gdn_chunks(*ins[:5], states, ins[5], nw_ref[...], ins[6])
            for ci, (b, hh) in enumerate(chains):
                y_ref[b, r, hh * 128:(hh + 1) * 128] = y[ci]
            return states

        states = lax.fori_loop(0, cb, step, jnp.stack([carry[p * nch + ci] for ci in range(nch)]))
        for ci in range(nch):
            carry[p * nch + ci] = states[ci]

    v3 = lambda a: a.reshape(nb, seq, a.shape[1])
    y, st = pl.pallas_call(
        body, grid=(nsb, GDN_HEADS // GDN_HPS),
        in_specs=[pair(), pair(), pair(), gate(), pair(24 // GDN_HPS), mats(CHUNK), pl.BlockSpec((1, 128), lambda s, p: (0, 0))],
        out_specs=[pair(), mats(GDN_DH)],
        out_shape=[jax.ShapeDtypeStruct((nb, seq, 512), F32),
                   jax.ShapeDtypeStruct((nb, GDN_HEADS, nc, GDN_DH, GDN_DH), F32)],
        scratch_shapes=[pltpu.VMEM((GDN_HEADS // GDN_HPS * nch, GDN_DH, GDN_DH), F32)],
        compiler_params=_cp("arbitrary", "arbitrary"), name="gdn_fwd")(v3(q), v3(k), v3(v), v3(gates), v3(proj), inv, nw)
    return y.reshape(t, 512), st


def gdn_bwd(q, k, v, gates, proj, nw, inv, states, dy, seq):
    t = q.shape[0]
    nb, nc = t // seq, seq // CHUNK
    cb = GDN_CB // 2
    nsb = nc // cb
    chains = [(b, hh) for b in range(nb) for hh in range(GDN_HPS)]
    nch = len(chains)
    pair, gate, mats = _gdn_specs(nb, nsb, cb, True)

    def body(q_ref, k_ref, v_ref, g_ref, gg_ref, inv_ref, st_ref, dy_ref, nw_ref,
             dq_ref, dk_ref, dv_ref, dgg_ref, dg_ref, dnw_ref, carry):
        s, p = pl.program_id(0), pl.program_id(1)

        @pl.when((s == 0) & (p == 0))
        def _():
            dnw_ref[...] = jnp.zeros_like(dnw_ref)

        @pl.when(p == 0)
        def _():
            dg_ref[...] = jnp.zeros_like(dg_ref)

        @pl.when(s == 0)
        def _():
            for ci in range(nch):
                carry[p * nch + ci] = jnp.zeros((GDN_DH, GDN_DH), F32)

        def step(idx, dstates):
            c = cb - 1 - idx
            r = _rows(c, CHUNK)
            ins = _gdn_chain_inputs(chains, p, r, c, q_ref, k_ref, v_ref, g_ref, gg_ref, inv_ref)
            st = jnp.stack([st_ref[b, hh, c] for b, hh in chains])
            dy = jnp.stack([dy_ref[b, r, hh * 128:(hh + 1) * 128] for b, hh in chains])
            _, vjp = jax.vjp(_gdn_chunks, *ins[:5], st, ins[5], nw_ref[...], ins[6])
            dq, dk, dv, dgc, dbc, dstates, dgg, dnw, _ = vjp((dy, dstates))
            dnw_ref[...] += dnw
            lane = _lane((CHUNK, 128))
            for ci, (b, hh) in enumerate(chains):
                h = GDN_HPS * p + hh
                ln = slice(hh * 128, (hh + 1) * 128)
                dq_ref[b, r, ln] = dq[ci]
                dk_ref[b, r, ln] = dk[ci]
                dv_ref[b, r, ln] = dv[ci]
                dgg_ref[b, r, ln] = dgg[ci]
                dg_ref[b, r, :] = jnp.where(lane == 8 + h, dgc[ci], jnp.where(lane == 12 + h, dbc[ci], dg_ref[b, r, :]))
            return dstates

        dstates = lax.fori_loop(0, cb, step, jnp.stack([carry[p * nch + ci] for ci in range(nch)]))
        for ci in range(nch):
            carry[p * nch + ci] = dstates[ci]

    v3 = lambda a: a.reshape(nb, seq, a.shape[1])
    res = pl.pallas_call(
        body, grid=(nsb, GDN_HEADS // GDN_HPS),
        in_specs=[pair(), pair(), pair(), gate(), pair(24 // GDN_HPS), mats(CHUNK), mats(GDN_DH), pair(),
                  pl.BlockSpec((1, 128), lambda s, p: (0, 0))],
        out_specs=[pair(), pair(), pair(), pair(), gate(), pl.BlockSpec((1, 128), lambda s, p: (0, 0))],
        out_shape=[jax.ShapeDtypeStruct((nb, seq, 512), F32)] * 4
                  + [jax.ShapeDtypeStruct((nb, seq, 128), F32), jax.ShapeDtypeStruct((1, 128), F32)],
        scratch_shapes=[pltpu.VMEM((GDN_HEADS // GDN_HPS * nch, GDN_DH, GDN_DH), F32)],
        compiler_params=_cp("arbitrary", "arbitrary"),
        name="gdn_bwd")(v3(q), v3(k), v3(v), v3(gates), v3(proj), inv, states, v3(dy), nw)
    return [a.reshape(t, a.shape[2]) for a in res[:5]] + [res[5]]


def loss_head(y, target, tm=512):
    t, d = y.shape

    def body(y_ref, t_ref, s_ref, dy_ref):
        @pl.when(pl.program_id(0) == 0)
        def _():
            s_ref[...] = jnp.zeros_like(s_ref)

        err = y_ref[...] - t_ref[...]
        s_ref[...] += jnp.sum(err * err, axis=0, keepdims=True)
        dy_ref[...] = err * (1.0 / d)

    return pl.pallas_call(
        body, grid=(t // tm,),
        in_specs=[pl.BlockSpec((tm, d), lambda i: (i, 0)), pl.BlockSpec((tm, d), lambda i: (i, 0))],
        out_specs=[pl.BlockSpec((1, d), lambda i: (0, 0)), pl.BlockSpec((tm, d), lambda i: (i, 0))],
        out_shape=[jax.ShapeDtypeStruct((1, d), F32), jax.ShapeDtypeStruct((t, d), F32)],
        compiler_params=_cp("arbitrary"), name="loss_head")(y, target)


def _place():
    return lax.axis_index("x"), lax.axis_index("y"), lax.axis_index("c")


def _peer(k):
    x, y, c = _place()
    px = 1 - x if (k >> 2) & 1 else x
    py = 1 - y if (k >> 1) & 1 else y
    pc = 1 - c if k & 1 else c
    return (px, py, pc), 4 * px + 2 * py + pc


_ANY = pl.BlockSpec(memory_space=pl.ANY)
_SEM = pl.BlockSpec(memory_space=pltpu.SEMAPHORE)
_EFFECT = pltpu.SideEffectType.DATAFLOW_SIDE_EFFECTING


def _me():
    x, y, c = _place()
    return 4 * x + 2 * y + c


def _remote_copy(ins, lands, scatter, send_sems, recv_sems, a, k, arriving):
    pid, pidx = _peer(k)
    return pltpu.make_async_remote_copy(src_ref=ins[a].at[pidx] if scatter[a] else ins[a],
                                        dst_ref=lands[a].at[pidx if arriving else _me()],
                                        send_sem=send_sems.at[a * N_DEV + k], recv_sem=recv_sems.at[a * N_DEV + k],
                                        device_id=pid, device_id_type=MESH)


def _local_copy(ins, lands, scatter, loc_sems, a):
    me = _me()
    return pltpu.make_async_copy(ins[a].at[me] if scatter[a] else ins[a], lands[a].at[me], loc_sems.at[a])


def exchange_start(arrays, scatter, name, after):
    n = len(arrays)
    lands = [lax.empty(a.shape if s else (N_DEV,) + a.shape, a.dtype) for a, s in zip(arrays, scatter)]

    def body(*refs):
        ins, lds = refs[:n], refs[n:2 * n]
        send_sems, recv_sems, loc_sems = refs[2 * n + 1:2 * n + 4]
        token = refs[-1]
        for k in range(1, N_DEV):
            for a in range(n):
                _remote_copy(ins, lds, scatter, send_sems, recv_sems, a, k, False).start()
        for a in range(n):
            _local_copy(ins, lds, scatter, loc_sems, a).start()
        token[...] = jnp.zeros_like(token)

    hbm = lambda a: pltpu.HBM(a.shape, a.dtype)
    res = pl.pallas_call(
        body, name=name,
        in_specs=[_ANY] * (2 * n + 1),
        out_specs=[_SEM, _SEM, _SEM] + [_ANY] * (2 * n) + [pl.BlockSpec(memory_space=pltpu.VMEM)],
        out_shape=[pltpu.SemaphoreType.DMA((n * N_DEV,)), pltpu.SemaphoreType.DMA((n * N_DEV,)),
                   pltpu.SemaphoreType.DMA((n,))]
                  + [hbm(a) for a in arrays] + [hbm(a) for a in lands] + [jax.ShapeDtypeStruct((8, 128), F32)],
        input_output_aliases={i: 3 + i for i in range(2 * n)},
        compiler_params=pltpu.CompilerParams(has_side_effects=_EFFECT),
    )(*[pltpu.with_memory_space_constraint(a, pltpu.HBM) for a in list(arrays) + lands], after)
    return res[0:3], res[3:3 + n], res[3 + n:3 + 2 * n], res[-1]


def exchange_wait(sems, arrays, lands, scatter, after, name):
    n = len(arrays)

    def body(*refs):
        ins, lds = refs[:n], refs[n:2 * n]
        ssem, rsem, lsem = refs[2 * n:2 * n + 3]
        for a in range(n):
            _local_copy(ins, lds, scatter, lsem, a).wait()
        for k in range(1, N_DEV):
            for a in range(n):
                _remote_copy(ins, lds, scatter, ssem, rsem, a, k, True).wait_recv()
        for k in range(1, N_DEV):
            for a in range(n):
                _remote_copy(ins, lds, scatter, ssem, rsem, a, k, False).wait_send()

    hbm = lambda a: pltpu.HBM(a.shape, a.dtype)
    res = pl.pallas_call(
        body, name=name,
        in_specs=[_ANY] * (2 * n) + [_SEM, _SEM, _SEM, _ANY],
        out_specs=[_ANY] * (2 * n),
        out_shape=[hbm(a) for a in arrays] + [hbm(a) for a in lands],
        input_output_aliases={i: i for i in range(2 * n)},
        compiler_params=pltpu.CompilerParams(has_side_effects=_EFFECT),
    )(*arrays, *lands, *sems, after)
    return list(res[n:])


def exchange_begin(arrays, scatter, name, after):
    sems, arrays_thru, lands_thru, token = exchange_start(arrays, scatter, name + "_start", after)
    return (sems, arrays_thru, lands_thru, scatter, name), token


def exchange_end(state, after):
    sems, arrays_thru, lands_thru, scatter, name = state
    return exchange_wait(sems, arrays_thru, lands_thru, scatter, after, name + "_wait")


def adamw_reduce(slots, w, m, v, l, name, after=None, prev=None):
    nl, r, c = w.shape
    tr = r
    while tr * c * 4 > (1 << 20) and tr % 16 == 0:
        tr //= 2
    bc1 = 1.0 - ADAM_B1 ** ADAM_STEP
    bc2 = 1.0 - ADAM_B2 ** ADAM_STEP

    def body(s_ref, w_ref, m_ref, v_ref, *rest):
        g_ref, d_ref, nm_ref, nv_ref = rest[-4:]
        g = s_ref[0].astype(F32)
        for j in range(1, N_DEV):
            g = g + s_ref[j].astype(F32)
        nm = ADAM_B1 * m_ref[...] + (1.0 - ADAM_B1) * g
        nv = ADAM_B2 * v_ref[...] + (1.0 - ADAM_B2) * (g * g)
        g_ref[...] = g
        nm_ref[...] = nm
        nv_ref[...] = nv
        d_ref[...] = -ADAM_LR * ((nm / bc1) / (jnp.sqrt(nv / bc2) + ADAM_EPS) + ADAM_WD * w_ref[...])

    blk = lambda: pl.BlockSpec((None, tr, c), lambda i: (l, i, 0))
    extra = ([] if after is None else [after]) + ([] if prev is None else list(prev))
    first_prev = 4 + (after is not None)
    return pl.pallas_call(
        body, grid=(r // tr,),
        in_specs=[pl.BlockSpec((N_DEV, tr, c), lambda i: (0, i, 0)), blk(), blk(), blk()] + [_ANY] * len(extra),
        out_specs=[blk(), blk(), blk(), blk()],
        out_shape=[jax.ShapeDtypeStruct((nl, r, c), F32)] * 4,
        input_output_aliases={} if prev is None else {first_prev + j: j for j in range(4)},
        compiler_params=_cp("parallel"), name=name)(slots, w, m, v, *extra)


BIG = ("ffn1_w_in", "ffn1_w_out", "w_in", "gdn_conv", "w_out", "ffn2_w_in", "ffn2_w_out")
GROUPS = (BIG[0:2], BIG[2:5], BIG[5:7])
SMALL = ("ffn1_norm", "mix_norm", "fox_q_norm", "fox_k_norm", "fox_f_bias", "gdn_a_log", "gdn_dt_bias",
         "gdn_out_norm", "ffn2_norm")
WEIGHTS = ("ffn1_norm", "ffn1_w_in", "ffn1_w_out", "mix_norm", "w_in", "fox_q_norm", "fox_k_norm", "fox_f_bias",
           "gdn_conv", "gdn_a_log", "gdn_dt_bias", "gdn_out_norm", "w_out", "ffn2_norm", "ffn2_w_in", "ffn2_w_out")
IN_COLS = (("fq", 512), ("fk", 512), ("fv", 512), ("ff", 8), ("gq", 512), ("gk", 512), ("gv", 512),
           ("ga", 4), ("gb", 4), ("gg", 512))
MY_BIG = ("fq", "fk", "fv", "gq", "gk", "gv", "gg")
MY_SMALL = ("ff", "ga", "gb")
SMALL_ROWS = 8 * 128


def _in_cols_to_mine(w):
    off, parts = 0, {}
    for nm, wd in IN_COLS:
        parts[nm] = w[:, off:off + wd]
        off += wd
    small = jnp.concatenate([parts[nm] for nm in MY_SMALL], axis=1)
    small = jnp.pad(small, ((0, 0), (0, 128 - small.shape[1])))
    return jnp.concatenate([parts[nm] for nm in MY_BIG] + [small], axis=1)


def _in_cols_from_mine(g):
    parts = {nm: g[:, i * 512:(i + 1) * 512] for i, nm in enumerate(MY_BIG)}
    off = N_BIG
    for nm in MY_SMALL:
        wd = dict(IN_COLS)[nm]
        parts[nm] = g[:, off:off + wd]
        off += wd
    return jnp.concatenate([parts[nm] for nm, _ in IN_COLS], axis=1)


def _pack_small(vals):
    rows = []
    nl = vals[SMALL[0]].shape[0]
    for l in range(nl):
        for nm in SMALL:
            v = vals[nm][l].reshape(-1)
            pad = (-v.shape[0]) % SMALL_ROWS
            rows.append(jnp.pad(v, (0, pad)).reshape(-1, 128))
    return jnp.concatenate(rows, axis=0)


def _unpack_small(packed, like):
    out = {nm: [] for nm in SMALL}
    row = 0
    nl = like[SMALL[0]].shape[0]
    for l in range(nl):
        for nm in SMALL:
            n = like[nm].shape[1]
            nr = -(-n // SMALL_ROWS) * 8
            out[nm].append(packed[row:row + nr].reshape(-1)[:n])
            row += nr
    return {nm: jnp.stack(v) for nm, v in out.items()}


def kernel(x, ffn1_norm, ffn1_w_in, ffn1_w_out, mix_norm, w_in, fox_q_norm, fox_k_norm, fox_f_bias, gdn_conv, gdn_a_log, gdn_dt_bias, gdn_out_norm, w_out, ffn2_norm, ffn2_w_in, ffn2_w_out, loss_target, m_ffn1_norm, m_ffn1_w_in, m_ffn1_w_out, m_mix_norm, m_w_in, m_fox_q_norm, m_fox_k_norm, m_fox_f_bias, m_gdn_conv, m_gdn_a_log, m_gdn_dt_bias, m_gdn_out_norm, m_w_out, m_ffn2_norm, m_ffn2_w_in, m_ffn2_w_out, v_ffn1_norm, v_ffn1_w_in, v_ffn1_w_out, v_mix_norm, v_w_in, v_fox_q_norm, v_fox_k_norm, v_fox_f_bias, v_gdn_conv, v_gdn_a_log, v_gdn_dt_bias, v_gdn_out_norm, v_w_out, v_ffn2_norm, v_ffn2_w_in, v_ffn2_w_out):
    wts = dict(ffn1_norm=ffn1_norm, ffn1_w_in=ffn1_w_in, ffn1_w_out=ffn1_w_out, mix_norm=mix_norm, w_in=w_in,
               fox_q_norm=fox_q_norm, fox_k_norm=fox_k_norm, fox_f_bias=fox_f_bias, gdn_conv=gdn_conv,
               gdn_a_log=gdn_a_log, gdn_dt_bias=gdn_dt_bias, gdn_out_norm=gdn_out_norm, w_out=w_out,
               ffn2_norm=ffn2_norm, ffn2_w_in=ffn2_w_in, ffn2_w_out=ffn2_w_out)
    mom = dict(ffn1_norm=m_ffn1_norm, ffn1_w_in=m_ffn1_w_in, ffn1_w_out=m_ffn1_w_out, mix_norm=m_mix_norm, w_in=m_w_in,
               fox_q_norm=m_fox_q_norm, fox_k_norm=m_fox_k_norm, fox_f_bias=m_fox_f_bias, gdn_conv=m_gdn_conv,
               gdn_a_log=m_gdn_a_log, gdn_dt_bias=m_gdn_dt_bias, gdn_out_norm=m_gdn_out_norm, w_out=m_w_out,
               ffn2_norm=m_ffn2_norm, ffn2_w_in=m_ffn2_w_in, ffn2_w_out=m_ffn2_w_out)
    var = dict(ffn1_norm=v_ffn1_norm, ffn1_w_in=v_ffn1_w_in, ffn1_w_out=v_ffn1_w_out, mix_norm=v_mix_norm, w_in=v_w_in,
               fox_q_norm=v_fox_q_norm, fox_k_norm=v_fox_k_norm, fox_f_bias=v_fox_f_bias, gdn_conv=v_gdn_conv,
               gdn_a_log=v_gdn_a_log, gdn_dt_bias=v_gdn_dt_bias, gdn_out_norm=v_gdn_out_norm, w_out=v_w_out,
               ffn2_norm=v_ffn2_norm, ffn2_w_in=v_ffn2_w_in, ffn2_w_out=v_ffn2_w_out)
    nb, seq, d = x.shape
    t = nb * seq
    depth = ffn1_norm.shape[0]

    stages = [(l, gi) for l in range(depth) for gi in range(len(GROUPS))]

    def shards_of(l, gi):
        return [wts[nm][l] if nm == "gdn_conv" else wts[nm][l].astype(BF16) for nm in GROUPS[gi]]

    def behind(nw, token):
        return nw if token is None else nw + token[0:1, 0:1]

    def small_params(l):
        return dict(
            n1=ffn1_norm[l][None], nmix=mix_norm[l][None], n2=ffn2_norm[l][None],
            qw=fox_q_norm[l][None], kw=fox_k_norm[l][None], onw=gdn_out_norm[l][None],
            gp=jnp.concatenate([
                jnp.concatenate([fox_f_bias[l], gdn_dt_bias[l], jnp.zeros((116,), F32)])[None],
                jnp.concatenate([jnp.zeros((8,), F32), gdn_a_log[l], jnp.zeros((116,), F32)])[None],
                jnp.zeros((6, 128), F32)], axis=0))

    h = x.reshape(t, d)
    state, token = exchange_begin(shards_of(0, 0), [False] * len(GROUPS[0]), "gather_0", ffn1_norm)
    landed = exchange_end(state, token)
    saved = [dict(p=small_params(l)) for l in range(depth)]
    for k, (l, gi) in enumerate(stages):
        s, w, token = saved[l], landed, None
        p = s["p"]
        if k + 1 < len(stages):
            nl, ng = stages[k + 1]
            state, token = exchange_begin(shards_of(nl, ng), [False] * len(GROUPS[ng]), f"gather_{k + 1}", landed[0])
        if gi == 0:
            fb = w[0].shape[2]
            p["w1i"], p["w1o"] = w[0].reshape(2, 4, d, fb), w[1].reshape(4, fb, d)
            s["x0"] = h
            h, *s["ffn1"] = ffn_fwd(h, behind(p["n1"], token), p["w1i"], p["w1o"])
            s["x1"] = h
        elif gi == 1:
            p["wi"] = _in_cols_to_mine(w[0].transpose(1, 0, 2).reshape(d, -1))
            p["cw"] = w[1].transpose(1, 0, 2).reshape(CONV_W, -1)
            p["wo"] = w[2].reshape(d, d)
            proj, hn = inproj_fwd(h, behind(p["nmix"], token), p["wi"])
            gates = gates_fwd(proj, p["gp"], seq)
            yf, lse = attn_fwd(proj, gates, p["qw"], p["kw"], seq, tq=min(seq, ATTN_TQ_FWD))
            qh, kh, vh = gdn_pre_fwd(proj, p["cw"], seq)
            inv = gdn_inv(kh, gates, seq)
            yg, st = gdn_fwd(qh, kh, vh, gates, proj, p["onw"], inv, seq)
            h, ycat = outproj_fwd(h, yf, yg, p["wo"])
            s.update(x2=h, proj=proj, hn=hn, gates=gates, yf=yf, lse=lse, qh=qh, kh=kh, vh=vh, st=st, inv=inv, ycat=ycat)
        else:
            fb = w[0].shape[2]
            p["w2i"], p["w2o"] = w[0].reshape(2, 4, d, fb), w[1].reshape(4, fb, d)
            h, *s["ffn2"] = ffn_fwd(h, behind(p["n2"], token), p["w2i"], p["w2o"])
        if k + 1 < len(stages):
            landed = exchange_end(state, h)

    sq, dh = loss_head(h, loss_target.reshape(t, d))
    loss = lax.psum(0.5 * jnp.sum(sq) / d, ("x", "y", "c"))

    got = [None] * len(stages)
    pending, token = None, None
    gsmall = {nm: [None] * depth for nm in SMALL}
    for k in reversed(range(len(stages))):
        l, gi = stages[k]
        s = saved[l]
        p = s["p"]
        if gi != 1:
            nw, xin, wi_, wo_, nm_n, (xn, gu, hh) = (
                (p["n1"], s["x0"], p["w1i"], p["w1o"], "ffn1_norm", s["ffn1"]) if gi == 0 else
                (p["n2"], s["x2"], p["w2i"], p["w2o"], "ffn2_norm", s["ffn2"]))
            dh, dn, dgu, dyh = ffn_bwd(xin, dh, behind(nw, token), gu, wi_, wo_)
            g_in, g_out = wgrad_ffn_in(xn, dgu), wgrad_ffn_out(hh, dyh)
            send = [g_in.reshape(N_DEV, d, g_in.shape[3]), g_out.reshape(N_DEV, -1, d)]
            gsmall[nm_n][l] = dn[0]
        else:
            dyf, dyg, dyb = outproj_bwd(dh, p["wo"], token)
            g_wo = wgrad_2d(s["ycat"], dyb, 512, "wgrad_w_out")
            dq, dk, dv, dga, dqw, dkw = attn_bwd(s["proj"], s["gates"], p["qw"], p["kw"], s["yf"], s["lse"], dyf, seq,
                                                 tq=min(seq, ATTN_TQ_BWD))
            dqh, dkh, dvh, dgg, dgb, donw = gdn_bwd(s["qh"], s["kh"], s["vh"], s["gates"], s["proj"], p["onw"],
                                                     s["inv"], s["st"], dyg, seq)
            dxq, dxk, dxv, dwq, dwk, dwv = gdn_pre_bwd(s["proj"], p["cw"], dqh, dkh, dvh, seq)
            dsm, dgp = gates_bwd(s["proj"], p["gp"], dga, dgb, seq)
            dh, dnmix, dproj = inproj_bwd(s["x1"], dh, p["nmix"], p["wi"], [dq, dk, dv, dxq, dxk, dxv, dgg, dsm])
            g_wi = wgrad_2d(s["hn"], dproj, 512, "wgrad_w_in", F32)
            g_cw = jnp.concatenate([dwq, dwk, dwv], axis=1)
            send = [_in_cols_from_mine(g_wi).reshape(d, N_DEV, -1).transpose(1, 0, 2),
                    g_cw.reshape(CONV_W, N_DEV, -1).transpose(1, 0, 2), g_wo.reshape(N_DEV, -1, d)]
            for nm, val in (("mix_norm", dnmix[0]), ("fox_q_norm", dqw[0]), ("fox_k_norm", dkw[0]),
                            ("fox_f_bias", dgp[0, 0:8]), ("gdn_a_log", dgp[1, 8:12]), ("gdn_dt_bias", dgp[0, 8:12]),
                            ("gdn_out_norm", donw[0])):
                gsmall[nm][l] = val
        flags = [True] * len(send)
        if k == 0:
            send.append(_pack_small({nm: jnp.stack(v) for nm, v in gsmall.items()}))
            flags.append(False)
        prev = dh
        if pending is not None:
            got[pending[1]] = exchange_end(pending[0], dh)
            prev = got[pending[1]][0]
        state, token = exchange_begin(send, flags, f"exchange_grads_{k}", prev)
        pending = (state, k)
    grad_x = dh.reshape(nb, seq, d)

    res = {}

    def update_stage(k, slots, after):
        l, gi = stages[k]
        for i, nm in enumerate(GROUPS[gi]):
            r, c = wts[nm].shape[1:]
            res[nm] = adamw_reduce(slots[i].reshape(N_DEV, r, c), wts[nm], mom[nm], var[nm], l, f"adamw_{nm}_{l}",
                                   after, res.get(nm))
            if after is not None:
                after = res[nm][0]
        return after

    last = token
    for k in range(1, len(stages)):
        last = update_stage(k, got[k], last)
    got[0] = exchange_end(pending[0], last)
    update_stage(0, got[0], None)
    small_like = {nm: wts[nm] for nm in SMALL}
    sm = adamw_reduce(got[0][-1], _pack_small(small_like)[None], _pack_small({nm: mom[nm] for nm in SMALL})[None],
                      _pack_small({nm: var[nm] for nm in SMALL})[None], 0, "adamw_small")
    sm = [_unpack_small(a[0], small_like) for a in sm]
    for nm in SMALL:
        res[nm] = [sm[j][nm] for j in range(4)]
    return (loss, grad_x, *[res[nm][0] for nm in WEIGHTS], *[res[nm][1] for nm in WEIGHTS],
            *[res[nm][2] for nm in WEIGHTS], *[res[nm][3] for nm in WEIGHTS])
```

```python
import functools

import jax
import jax.numpy as jnp
from jax import lax
from jax.experimental import pallas as pl
from jax.experimental.pallas import tpu as pltpu

F32 = jnp.float32
BF16 = jnp.bfloat16
EPS = 1e-6
N_DEV = 8
MESH = pl.DeviceIdType.MESH
HIGHEST = lax.Precision.HIGHEST
VMEM_LIMIT = 56 * 1024 * 1024

FOX_HEADS, FOX_DH = 8, 64
GDN_HEADS, GDN_DH = 4, 128
CHUNK = 64
CONV_W = 4

ADAM_LR, ADAM_B1, ADAM_B2, ADAM_EPS, ADAM_WD, ADAM_STEP = 0.001, 0.9, 0.999, 1e-08, 0.01, 10


def _cp(*sem):
    return pltpu.CompilerParams(dimension_semantics=sem, vmem_limit_bytes=VMEM_LIMIT)


def _dot(a, b):
    return jnp.dot(a, b, preferred_element_type=F32)


def _dot_nt(a, b):
    return lax.dot_general(a, b, (((1,), (1,)), ((), ())), preferred_element_type=F32)


def _dot_tn(a, b):
    return lax.dot_general(a, b, (((0,), (0,)), ((), ())), preferred_element_type=F32)


def _rstd(xf):
    return lax.rsqrt(jnp.mean(xf * xf, axis=-1, keepdims=True) + EPS)


def _rms_bwd(xf, r, dyn):
    return r * dyn - xf * (r * r * r) * jnp.mean(dyn * xf, axis=-1, keepdims=True)


def ffn_fwd(x, nw, w_in, w_out, tm=1024, rc=1024):
    t, d = x.shape
    nj, fb = w_out.shape[0], w_out.shape[1]
    tm = min(tm, t)
    rc = min(rc, tm)

    def body(x_ref, nw_ref, wi_ref, wo_ref, o_ref, xn_ref, gu_ref, h_ref, acc_ref):
        j = pl.program_id(1)

        @pl.when(j == 0)
        def _():
            xf = x_ref[...]
            xn_ref[...] = (xf * _rstd(xf) * nw_ref[...]).astype(BF16)
            acc_ref[...] = jnp.zeros_like(acc_ref)

        rows = [slice(c * rc, (c + 1) * rc) for c in range(tm // rc)]
        gs = [_dot(xn_ref[r, :], wi_ref[0]) for r in rows]
        us = [_dot(xn_ref[r, :], wi_ref[1]) for r in rows]
        hs = []
        for g, u, r in zip(gs, us, rows):
            sg = jax.nn.sigmoid(g)
            silu = g * sg
            h = (silu * u).astype(BF16)
            gu_ref[0, r, :] = (u * (sg * (1.0 + g * (1.0 - sg)))).astype(BF16)
            gu_ref[1, r, :] = silu.astype(BF16)
            h_ref[r, :] = h
            hs.append(h)
        for h, r in zip(hs, rows):
            acc_ref[r, :] += _dot(h, wo_ref[...])

        @pl.when(j == nj - 1)
        def _():
            o_ref[...] = x_ref[...] + 0.5 * acc_ref[...]

    return pl.pallas_call(
        body, grid=(t // tm, nj),
        in_specs=[pl.BlockSpec((tm, d), lambda i, j: (i, 0)),
                  pl.BlockSpec((1, d), lambda i, j: (0, 0)),
                  pl.BlockSpec((2, None, d, fb), lambda i, j: (0, j, 0, 0)),
                  pl.BlockSpec((None, fb, d), lambda i, j: (j, 0, 0))],
        out_specs=[pl.BlockSpec((tm, d), lambda i, j: (i, 0)),
                   pl.BlockSpec((tm, d), lambda i, j: (i, 0)),
                   pl.BlockSpec((2, None, tm, fb), lambda i, j: (0, j, i, 0)),
                   pl.BlockSpec((None, tm, fb), lambda i, j: (j, i, 0))],
        out_shape=[jax.ShapeDtypeStruct((t, d), F32), jax.ShapeDtypeStruct((t, d), BF16),
                   jax.ShapeDtypeStruct((2, nj, t, fb), BF16), jax.ShapeDtypeStruct((nj, t, fb), BF16)],
        scratch_shapes=[pltpu.VMEM((tm, d), F32)],
        compiler_params=_cp("parallel", "arbitrary"), name="ffn_fwd")(x, nw, w_in, w_out)


def ffn_bwd(x, dy, nw, gu, w_in, w_out, tm=1024, rc=256):
    t, d = x.shape
    nj, fb = w_out.shape[0], w_out.shape[1]
    tm = min(tm, t)
    rc = min(rc, tm)

    def body(x_ref, dy_ref, nw_ref, gu_ref, wi_ref, wo_ref,
             dx_ref, dnw_ref, dgu_ref, dyh_ref, acc_ref):
        i, j = pl.program_id(0), pl.program_id(1)

        @pl.when(j == 0)
        def _():
            dyh_ref[...] = (0.5 * dy_ref[...]).astype(BF16)
            acc_ref[...] = jnp.zeros_like(acc_ref)

        @pl.when((i == 0) & (j == 0))
        def _():
            dnw_ref[...] = jnp.zeros_like(dnw_ref)

        rows = [slice(c * rc, (c + 1) * rc) for c in range(tm // rc)]
        dhs = [_dot_nt(dyh_ref[r, :], wo_ref[...]) for r in rows]
        dgs = [(dh * gu_ref[0, r, :].astype(F32)).astype(BF16) for dh, r in zip(dhs, rows)]
        dus = [(dh * gu_ref[1, r, :].astype(F32)).astype(BF16) for dh, r in zip(dhs, rows)]
        for dg, du, r in zip(dgs, dus, rows):
            dgu_ref[0, r, :] = dg
            dgu_ref[1, r, :] = du
        for dg, du, r in zip(dgs, dus, rows):
            acc_ref[r, :] += _dot_nt(dg, wi_ref[0]) + _dot_nt(du, wi_ref[1])

        @pl.when(j == nj - 1)
        def _():
            xf = x_ref[...]
            r = _rstd(xf)
            dxn = acc_ref[...]
            dnw_ref[...] += jnp.sum(dxn * xf * r, axis=0, keepdims=True)
            dx_ref[...] = _rms_bwd(xf, r, dxn * nw_ref[...]) + dy_ref[...]

    return pl.pallas_call(
        body, grid=(t // tm, nj),
        in_specs=[_once((tm, d), lambda i, j: (i, 0)),
                  _once((tm, d), lambda i, j: (i, 0)),
                  pl.BlockSpec((1, d), lambda i, j: (0, 0)),
                  pl.BlockSpec((2, None, tm, fb), lambda i, j: (0, j, i, 0)),
                  pl.BlockSpec((2, None, d, fb), lambda i, j: (0, j, 0, 0)),
                  pl.BlockSpec((None, fb, d), lambda i, j: (j, 0, 0))],
        out_specs=[pl.BlockSpec((tm, d), lambda i, j: (i, 0)),
                   pl.BlockSpec((1, d), lambda i, j: (0, 0)),
                   pl.BlockSpec((2, None, tm, fb), lambda i, j: (0, j, i, 0)),
                   pl.BlockSpec((tm, d), lambda i, j: (i, 0))],
        out_shape=[jax.ShapeDtypeStruct((t, d), F32),
                   jax.ShapeDtypeStruct((1, d), F32),
                   jax.ShapeDtypeStruct((2, nj, t, fb), BF16),
                   jax.ShapeDtypeStruct((t, d), BF16)],
        scratch_shapes=[pltpu.VMEM((tm, d), F32)],
        compiler_params=_cp("arbitrary", "arbitrary"), name="ffn_bwd")(x, dy, nw, gu, w_in, w_out)


def _wgrad_call(a, b, a_spec, b_spec, out_shape, out_spec, grid, name, out_dtype=BF16):
    last = len(grid) - 1
    acc_shape = tuple(s for s in out_spec.block_shape if s is not None)

    def body(a_ref, b_ref, o_ref, acc_ref):
        @pl.when(pl.program_id(last) == 0)
        def _():
            acc_ref[...] = jnp.zeros_like(acc_ref)

        if len(acc_shape) == 3:
            a = a_ref[...]
            for s in range(acc_shape[0]):
                acc_ref[s] += _dot_tn(a, b_ref[s])
        else:
            acc_ref[...] += _dot_tn(a_ref[...], b_ref[...])

        @pl.when(pl.program_id(last) == grid[last] - 1)
        def _():
            o_ref[...] = acc_ref[...].astype(o_ref.dtype)

    sem = ("parallel",) * last + ("arbitrary",)
    return pl.pallas_call(body, grid=grid, in_specs=[a_spec, b_spec], out_specs=out_spec,
                          out_shape=jax.ShapeDtypeStruct(out_shape, out_dtype),
                          scratch_shapes=[pltpu.VMEM(acc_shape, F32)],
                          compiler_params=_cp(*sem), name=name)(a, b)


WGRAD_TM = 1024


def wgrad_ffn_in(xn, dgu, tm=WGRAD_TM):
    t, d = xn.shape
    _, nj, _, fb = dgu.shape
    tm = min(tm, t)
    return _wgrad_call(xn, dgu,
                       pl.BlockSpec((tm, d), lambda j, k: (k, 0)),
                       pl.BlockSpec((2, None, tm, fb), lambda j, k: (0, j, k, 0)),
                       (2, nj, d, fb), pl.BlockSpec((2, None, d, fb), lambda j, k: (0, j, 0, 0)),
                       (nj, t // tm), "wgrad_ffn_in")


def wgrad_ffn_out(h, dyh, tm=WGRAD_TM):
    nj, t, fb = h.shape
    d = dyh.shape[1]
    tm = min(tm, t)
    return _wgrad_call(h, dyh,
                       pl.BlockSpec((None, tm, fb), lambda j, k: (j, k, 0)),
                       pl.BlockSpec((tm, d), lambda j, k: (k, 0)),
                       (nj, fb, d), pl.BlockSpec((None, fb, d), lambda j, k: (j, 0, 0)),
                       (nj, t // tm), "wgrad_ffn_out")


def wgrad_2d(a, b, tk, name, out_dtype=BF16, tm=512):
    t, k = a.shape
    n = b.shape[1]
    return _wgrad_call(a, b,
                       pl.BlockSpec((tm, tk), lambda c, s: (s, c)),
                       pl.BlockSpec((tm, n), lambda c, s: (s, 0)),
                       (k, n), pl.BlockSpec((tk, n), lambda c, s: (c, 0)),
                       (k // tk, t // tm), name, out_dtype)


N_BIG = 7 * 512
N_PROJ = N_BIG + 128
COL_SMALL = N_BIG // 128


def inproj_fwd(x, nw, w, tm=256):
    t, d = x.shape
    n = w.shape[1]

    def body(x_ref, nw_ref, w_ref, p_ref, hn_ref):
        xf = x_ref[...]
        hn = (xf * _rstd(xf) * nw_ref[...]).astype(BF16)
        hn_ref[...] = hn
        p_ref[...] = _dot(hn, w_ref[...])

    return pl.pallas_call(
        body, grid=(t // tm,),
        in_specs=[pl.BlockSpec((tm, d), lambda i: (i, 0)), pl.BlockSpec((1, d), lambda i: (0, 0)),
                  pl.BlockSpec((d, n), lambda i: (0, 0))],
        out_specs=[pl.BlockSpec((tm, n), lambda i: (i, 0)), pl.BlockSpec((tm, d), lambda i: (i, 0))],
        out_shape=[jax.ShapeDtypeStruct((t, n), F32), jax.ShapeDtypeStruct((t, d), BF16)],
        compiler_params=_cp("parallel"), name="inproj_fwd")(x, nw, w)


def inproj_bwd(x, dres, nw, w, dparts, tm=256):
    t, d = x.shape
    n = w.shape[1]
    widths = [p.shape[1] for p in dparts]
    assert sum(widths) == n

    def body(x_ref, dres_ref, nw_ref, w_ref, *rest):
        part_refs, (dx_ref, dnw_ref, dp_ref) = rest[:len(widths)], rest[len(widths):]

        @pl.when(pl.program_id(0) == 0)
        def _():
            dnw_ref[...] = jnp.zeros_like(dnw_ref)

        dp = jnp.concatenate([r[...].astype(BF16) for r in part_refs], axis=1)
        dp_ref[...] = dp
        dhn = _dot_nt(dp, w_ref[...])
        xf = x_ref[...]
        r = _rstd(xf)
        dnw_ref[...] += jnp.sum(dhn * xf * r, axis=0, keepdims=True)
        dx_ref[...] = _rms_bwd(xf, r, dhn * nw_ref[...]) + dres_ref[...]

    return pl.pallas_call(
        body, grid=(t // tm,),
        in_specs=[pl.BlockSpec((tm, d), lambda i: (i, 0)), pl.BlockSpec((tm, d), lambda i: (i, 0)),
                  pl.BlockSpec((1, d), lambda i: (0, 0)), pl.BlockSpec((d, n), lambda i: (0, 0))]
                 + [pl.BlockSpec((tm, wd), lambda i: (i, 0)) for wd in widths],
        out_specs=[pl.BlockSpec((tm, d), lambda i: (i, 0)), pl.BlockSpec((1, d), lambda i: (0, 0)),
                   pl.BlockSpec((tm, n), lambda i: (i, 0))],
        out_shape=[jax.ShapeDtypeStruct((t, d), F32), jax.ShapeDtypeStruct((1, d), F32),
                   jax.ShapeDtypeStruct((t, n), BF16)],
        compiler_params=_cp("arbitrary"), name="inproj_bwd")(x, dres, nw, w, *dparts)


def outproj_fwd(x, yf, yg, w, tm=512):
    t, d = x.shape
    hw = yf.shape[1]

    def body(x_ref, yf_ref, yg_ref, w_ref, o_ref, y_ref):
        y = jnp.concatenate([yf_ref[...], yg_ref[...]], axis=1).astype(BF16)
        y_ref[...] = y
        o_ref[...] = x_ref[...] + _dot(y, w_ref[...])

    return pl.pallas_call(
        body, grid=(t // tm,),
        in_specs=[pl.BlockSpec((tm, d), lambda i: (i, 0)), pl.BlockSpec((tm, hw), lambda i: (i, 0)),
                  pl.BlockSpec((tm, hw), lambda i: (i, 0)), pl.BlockSpec((2 * hw, d), lambda i: (0, 0))],
        out_specs=[pl.BlockSpec((tm, d), lambda i: (i, 0)), pl.BlockSpec((tm, 2 * hw), lambda i: (i, 0))],
        out_shape=[jax.ShapeDtypeStruct((t, d), F32), jax.ShapeDtypeStruct((t, 2 * hw), BF16)],
        compiler_params=_cp("parallel"), name="outproj_fwd")(x, yf, yg, w)


def outproj_bwd(dy, w, after=None, tm=512):
    t, d = dy.shape
    hw = w.shape[0] // 2
    extra = [] if after is None else [after]

    def body(dy_ref, w_ref, *rest):
        df_ref, dg_ref, dyb_ref = rest[-3:]
        dyb = dy_ref[...].astype(BF16)
        dyb_ref[...] = dyb
        dyy = _dot_nt(dyb, w_ref[...])
        df_ref[...] = dyy[:, :hw]
        dg_ref[...] = dyy[:, hw:]

    return pl.pallas_call(
        body, grid=(t // tm,),
        in_specs=[pl.BlockSpec((tm, d), lambda i: (i, 0)), pl.BlockSpec((2 * hw, d), lambda i: (0, 0))]
                 + [pl.BlockSpec(memory_space=pl.ANY)] * len(extra),
        out_specs=[pl.BlockSpec((tm, hw), lambda i: (i, 0)), pl.BlockSpec((tm, hw), lambda i: (i, 0)),
                   pl.BlockSpec((tm, d), lambda i: (i, 0))],
        out_shape=[jax.ShapeDtypeStruct((t, hw), F32), jax.ShapeDtypeStruct((t, hw), F32),
                   jax.ShapeDtypeStruct((t, d), BF16)],
        compiler_params=_cp("parallel"), name="outproj_bwd")(dy, w, *extra)


def _lane(shape):
    return lax.broadcasted_iota(jnp.int32, shape, 1)


def _row(shape):
    return lax.broadcasted_iota(jnp.int32, shape, 0)


def _gate_terms(val, gp_ref):
    z = val + gp_ref[0:1, :]
    sp = jnp.log(1.0 + jnp.exp(-jnp.abs(z)))
    return z, sp


def gates_fwd(proj, gp, seq, ts=512):
    t = proj.shape[0]
    nb, ns = t // seq, seq // ts

    def body(v_ref, gp_ref, o_ref, carry_ref):
        @pl.when(pl.program_id(1) == 0)
        def _():
            carry_ref[...] = jnp.zeros_like(carry_ref)

        z, sp = _gate_terms(v_ref[...], gp_ref)
        logsig = jnp.minimum(z, 0.0) - sp
        tri = (_row((ts, ts)) >= _lane((ts, ts))).astype(F32)
        cum = jnp.dot(tri, logsig, precision=HIGHEST, preferred_element_type=F32) + carry_ref[0:1, :]
        carry_ref[0:1, :] = cum[ts - 1:ts, :]
        g = -jnp.exp(gp_ref[1:2, :]) * (jnp.maximum(z, 0.0) + sp)
        beta = jax.nn.sigmoid(z)
        lane = _lane((ts, 128))
        o_ref[...] = jnp.where(lane < 8, cum, jnp.where(lane < 12, g, jnp.where(lane < 16, beta, 0.0)))

    return pl.pallas_call(
        body, grid=(nb, ns),
        in_specs=[pl.BlockSpec((ts, 128), lambda b, s: (b * ns + s, COL_SMALL)),
                  pl.BlockSpec((8, 128), lambda b, s: (0, 0))],
        out_specs=pl.BlockSpec((ts, 128), lambda b, s: (b * ns + s, 0)),
        out_shape=jax.ShapeDtypeStruct((t, 128), F32),
        scratch_shapes=[pltpu.VMEM((8, 128), F32)],
        compiler_params=_cp("parallel", "arbitrary"), name="gates_fwd")(proj, gp)


def gates_bwd(proj, gp, dga, dgb, seq, ts=512):
    t = proj.shape[0]
    nb, ns = t // seq, seq // ts

    def body(v_ref, gp_ref, da_ref, db_ref, ds_ref, dgp_ref, carry_ref):
        @pl.when(pl.program_id(1) == 0)
        def _():
            carry_ref[...] = jnp.zeros_like(carry_ref)

        @pl.when((pl.program_id(0) == 0) & (pl.program_id(1) == 0))
        def _():
            dgp_ref[...] = jnp.zeros_like(dgp_ref)

        lane = _lane((ts, 128))
        dgate = jnp.where(lane < 8, da_ref[...], jnp.where(lane < 16, db_ref[...], 0.0))
        z, sp = _gate_terms(v_ref[...], gp_ref)
        triu = (_row((ts, ts)) <= _lane((ts, ts))).astype(F32)
        dlog = jnp.dot(triu, dgate, precision=HIGHEST, preferred_element_type=F32) + carry_ref[0:1, :]
        carry_ref[0:1, :] = dlog[0:1, :]
        sig = jax.nn.sigmoid(z)
        nea = -jnp.exp(gp_ref[1:2, :])
        g = nea * (jnp.maximum(z, 0.0) + sp)
        dz = jnp.where(lane < 8, dlog * (1.0 - sig),
                       jnp.where(lane < 12, dgate * nea * sig, dgate * sig * (1.0 - sig)))
        dz = jnp.where(lane < 16, dz, 0.0)
        ds_ref[...] = dz.astype(BF16)
        dgp_ref[0:1, :] += jnp.where(lane[0:1] < 12, jnp.sum(dz, axis=0, keepdims=True), 0.0)
        dgp_ref[1:2, :] += jnp.where((lane[0:1] >= 8) & (lane[0:1] < 12), jnp.sum(dgate * g, axis=0, keepdims=True), 0.0)

    rev = lambda b, s: (b * ns + (ns - 1 - s), 0)
    return pl.pallas_call(
        body, grid=(nb, ns),
        in_specs=[pl.BlockSpec((ts, 128), lambda b, s: (b * ns + (ns - 1 - s), COL_SMALL)),
                  pl.BlockSpec((8, 128), lambda b, s: (0, 0)),
                  pl.BlockSpec((ts, 128), rev), pl.BlockSpec((ts, 128), rev)],
        out_specs=[pl.BlockSpec((ts, 128), rev), pl.BlockSpec((8, 128), lambda b, s: (0, 0))],
        out_shape=[jax.ShapeDtypeStruct((t, 128), BF16), jax.ShapeDtypeStruct((8, 128), F32)],
        scratch_shapes=[pltpu.VMEM((8, 128), F32)],
        compiler_params=_cp("arbitrary", "arbitrary"), name="gates_bwd")(proj, gp, dga, dgb)


NEG = -1e30
ATTN_TQ_FWD = 1024
ATTN_TQ_BWD = 512


def _pick_lane(tile, idx):
    return jnp.sum(jnp.where(_lane(tile.shape) == idx, tile, 0.0), axis=1, keepdims=True)


def _col_to_row(col, n):
    return jnp.sum(jnp.where(_row((n, n)) == _lane((n, n)), col, 0.0), axis=0, keepdims=True)


def _row_to_col(row, n):
    return jnp.sum(jnp.where(_row((n, n)) == _lane((n, n)), row, 0.0), axis=1, keepdims=True)


def _rows(i, n):
    return pl.ds(pl.multiple_of(i * n, n), n)


LOG2E = 1.4426950408889634
LN2 = 0.6931471805599453


def _rowsum(z, width, passes=2):
    ones = jnp.ones((z.shape[1], width), BF16)
    total, rest = None, z
    for _ in range(passes):
        part = rest.astype(BF16)
        rest = rest - part.astype(F32)
        total = _dot(part, ones) if total is None else total + _dot(part, ones)
    return total


def _split_dot(x, mat, passes):
    total, rest = None, x
    for _ in range(passes):
        part = rest.astype(BF16)
        rest = rest - part.astype(F32)
        total = _dot(part, mat) if total is None else total + _dot(part, mat)
    return total


def _pair_mats(p, dh):
    r, l = _row((128, 128)), _lane((128, 128))
    same = (r < dh) == (l < dh)
    upper = (l >= dh).astype(jnp.int32)
    as_bf16 = lambda m: m.astype(BF16)
    return dict(own=as_bf16(same), other=as_bf16(jnp.logical_not(same)), pick_other=as_bf16(r == 2 * p + 1 - upper),
                swap=as_bf16(((r == 0) & (l >= dh)) | ((r == dh) & (l < dh))))


def _pair_rstd(x2, sel, dh):
    return lax.rsqrt(_split_dot(x2 * x2, sel["own"], 2) * (1.0 / dh) + EPS)


def _pair_aug(cols, n, dh):
    lane = _lane((n, 128))
    li = jnp.where(lane >= dh, lane - dh, lane)
    out = jnp.zeros((n, 128), F32)
    for i, c in enumerate(cols):
        out = jnp.where(li == i, c, out)
    return out


def _rstd_mxu(xf):
    return lax.rsqrt(_rowsum(xf * xf, xf.shape[1]) * (1.0 / xf.shape[1]) + EPS)


def _rms_bwd_mxu(xf, r, dyn):
    return r * dyn - xf * (r * r * r) * (_rowsum(dyn * xf, xf.shape[1]) * (1.0 / xf.shape[1]))


def _pick_lane_mxu(tile, idx, width):
    onehot = (_row((tile.shape[1], width)) == idx).astype(BF16)
    total, rest = None, tile
    for _ in range(3):
        part = rest.astype(BF16)
        rest = rest - part.astype(F32)
        total = _dot(part, onehot) if total is None else total + _dot(part, onehot)
    return total


def _split3(x):
    hi = x.astype(BF16).astype(F32)
    mid = (x - hi).astype(BF16).astype(F32)
    return [hi, mid, (x - hi - mid).astype(BF16).astype(F32)]


def _aug_cols(cols, n, width):
    lane = _lane((n, width))
    out = jnp.zeros((n, width), F32)
    for i, c in enumerate(cols):
        out = jnp.where(lane == i, c, out)
    return out


def _once(shape, index_map):
    return pl.BlockSpec(shape, index_map, pipeline_mode=pl.Buffered(1))


def attn_fwd(proj, gates, qw, kw, seq, tq=256):
    t = proj.shape[0]
    nb, nq, dh = t // seq, seq // tq, FOX_DH
    scale = dh ** -0.5

    def body(q_ref, k_ref, v_ref, g_ref, qw_ref, kw_ref, y_ref, lse_ref, qs, ks, vs):
        p = pl.program_id(1)
        heads = range(2)
        low = _lane((tq, 128)) < dh
        sel = _pair_mats(p, dh)

        def prep(i, _):
            r = _rows(i, tq)
            q2, k2 = q_ref[r, :], k_ref[r, :]
            cc = _split_dot(g_ref[r, :], sel["pick_other"], 3) * LOG2E
            qn = q2 * _pair_rstd(q2, sel, dh) * qw_ref[...] * (scale * LOG2E)
            kn = k2 * _pair_rstd(k2, sel, dh) * kw_ref[...]
            qx = _pair_aug(_split3(cc) + [1.0, 1.0, 1.0], tq, dh)
            kx = _pair_aug([1.0, 1.0, 1.0] + _split3(-cc), tq, dh)
            for hh in heads:
                own = low if hh == 0 else jnp.logical_not(low)
                qs[hh, r, :] = jnp.where(own, qn, qx).astype(BF16)
                ks[hh, r, :] = jnp.where(own, kn, kx).astype(BF16)
            vs[r, :] = v_ref[r, :].astype(BF16)
            return 0

        lax.fori_loop(0, nq, prep, 0)

        def q_tile(i, _):
            r = _rows(i, tq)
            qt = [qs[hh, r, :] for hh in heads]

            def kv_step(j, carry, masked):
                kr = _rows(j, tq)
                vt = vs[kr, :]
                out = []
                for hh in heads:
                    m, l, acc = carry[hh]
                    s = _dot_nt(qt[hh], ks[hh, kr, :])
                    if masked:
                        s = jnp.where(_row((tq, tq)) >= _lane((tq, tq)), s, NEG)
                    m_new = jnp.maximum(m, jnp.max(s, axis=1, keepdims=True))
                    pe = jnp.exp2(s - m_new)
                    a = jnp.exp2(m - m_new)
                    out.append((m_new, a * l + jnp.sum(pe, axis=1, keepdims=True), a * acc + _dot(pe.astype(BF16), vt)))
                return tuple(out)

            one = (jnp.full((tq, 1), NEG, F32), jnp.zeros((tq, 1), F32), jnp.zeros((tq, 128), F32))
            carry = lax.fori_loop(0, i, lambda j, c: kv_step(j, c, False), (one, one))
            (m0, l0, acc0), (m1, l1, acc1) = kv_step(i, carry, True)
            y_ref[r, :] = jnp.where(low, acc0 / l0, acc1 / l1)
            lse_ref[r, :] = jnp.where(low, m0 + jnp.log2(l0), m1 + jnp.log2(l1))
            return 0

        lax.fori_loop(0, nq, q_tile, 0)

    blk = lambda off: _once((seq, 128), lambda b, p: (b, off + p))
    return pl.pallas_call(
        body, grid=(nb, 4),
        in_specs=[blk(0), blk(4), blk(8), _once((seq, 128), lambda b, p: (b, 0)),
                  pl.BlockSpec((1, 128), lambda b, p: (0, 0)), pl.BlockSpec((1, 128), lambda b, p: (0, 0))],
        out_specs=[pl.BlockSpec((seq, 128), lambda b, p: (b, p)), pl.BlockSpec((seq, 128), lambda b, p: (b, p))],
        out_shape=[jax.ShapeDtypeStruct((t, 512), F32), jax.ShapeDtypeStruct((t, 512), F32)],
        scratch_shapes=[pltpu.VMEM((2, seq, 128), BF16), pltpu.VMEM((2, seq, 128), BF16), pltpu.VMEM((seq, 128), BF16)],
        compiler_params=_cp("parallel", "arbitrary"),
        name="attn_fwd")(proj, proj, proj, gates, jnp.tile(qw, (1, 2)), jnp.tile(kw, (1, 2)))


def attn_bwd(proj, gates, qw, kw, y, lse, dy, seq, tq=256):
    t = proj.shape[0]
    nb, nq, dh = t // seq, seq // tq, FOX_DH
    scale = dh ** -0.5

    def body(q_ref, k_ref, v_ref, g_ref, qw_ref, kw_ref, y_ref, lse_ref, dy_ref,
             dq_ref, dk_ref, dv_ref, dg_ref, dqw_ref, dkw_ref,
             qs, ks, vs, dos, dsrow, dqa, dka):
        b, p = pl.program_id(0), pl.program_id(1)

        @pl.when((b == 0) & (p == 0))
        def _():
            dqw_ref[...] = jnp.zeros_like(dqw_ref)
            dkw_ref[...] = jnp.zeros_like(dkw_ref)

        @pl.when(p == 0)
        def _():
            dg_ref[...] = jnp.zeros_like(dg_ref)

        heads = range(2)
        low = _lane((tq, 128)) < dh
        sel = _pair_mats(p, dh)

        def prep(i, _):
            r = _rows(i, tq)
            q2, k2, dy2 = q_ref[r, :], k_ref[r, :], dy_ref[r, :]
            cc = _split_dot(g_ref[r, :], sel["pick_other"], 3) * LOG2E
            lse_x = _split_dot(lse_ref[r, :], sel["swap"], 3)
            delta_x = _split_dot(dy2 * y_ref[r, :], sel["other"], 2)
            qn = q2 * _pair_rstd(q2, sel, dh) * qw_ref[...] * (scale * LOG2E)
            kn = k2 * _pair_rstd(k2, sel, dh) * kw_ref[...]
            qx = _pair_aug(_split3(cc) + [1.0, 1.0, 1.0] + _split3(-lse_x), tq, dh)
            kx = _pair_aug([1.0, 1.0, 1.0] + _split3(-cc) + [1.0, 1.0, 1.0], tq, dh)
            vx = _pair_aug([1.0, 1.0, 1.0], tq, dh)
            dx = _pair_aug(_split3(-delta_x), tq, dh)
            for hh in heads:
                own = low if hh == 0 else jnp.logical_not(low)
                qs[hh, r, :] = jnp.where(own, qn, qx).astype(BF16)
                ks[hh, r, :] = jnp.where(own, kn, kx).astype(BF16)
                vs[hh, r, :] = jnp.where(own, v_ref[r, :], vx).astype(BF16)
                dos[hh, r, :] = jnp.where(own, dy2, dx).astype(BF16)
                dsrow[hh, r, :] = jnp.zeros((tq, 1), F32)
                dqa[hh, r, :] = jnp.zeros((tq, 128), F32)
            return 0

        lax.fori_loop(0, nq, prep, 0)

        def kv_tile(j, _):
            kr = _rows(j, tq)
            kt = [ks[hh, kr, :] for hh in heads]
            vt = [vs[hh, kr, :] for hh in heads]

            def q_step(i, carry, masked):
                r = _rows(i, tq)
                out = []
                for hh in heads:
                    dk, dv, dcr = carry[hh]
                    qt, dot = qs[hh, r, :], dos[hh, r, :]
                    s = _dot_nt(qt, kt[hh])
                    if masked:
                        s = jnp.where(_row((tq, tq)) >= _lane((tq, tq)), s, NEG)
                    pe = jnp.exp2(s)
                    ds = pe * _dot_nt(dot, vt[hh])
                    dsb = ds.astype(BF16)
                    dqa[hh, r, :] += _dot(dsb, kt[hh])
                    dsrow[hh, r, :] += jnp.sum(ds, axis=1, keepdims=True)
                    out.append((dk + _dot_tn(dsb, qt), dv + _dot_tn(pe.astype(BF16), dot),
                                dcr - jnp.sum(ds, axis=0, keepdims=True)))
                return tuple(out)

            one = (jnp.zeros((tq, 128), F32), jnp.zeros((tq, 128), F32), jnp.zeros((1, tq), F32))
            carry = q_step(j, (one, one), True)
            (dk0, dv0, dcr0), (dk1, dv1, dcr1) = lax.fori_loop(j + 1, nq, lambda i, c: q_step(i, c, False), carry)
            dka[kr, :] = jnp.where(low, dk0, dk1)
            dv_ref[kr, :] = jnp.where(low, dv0, dv1).astype(BF16)
            lane = _lane((tq, 128))
            dg_ref[kr, :] = jnp.where(lane == 2 * p, _row_to_col(dcr0, tq),
                                      jnp.where(lane == 2 * p + 1, _row_to_col(dcr1, tq), dg_ref[kr, :]))
            return 0

        lax.fori_loop(0, nq, kv_tile, 0)

        def post(i, _):
            r = _rows(i, tq)
            q2, k2 = q_ref[r, :], k_ref[r, :]
            rq, rk = _pair_rstd(q2, sel, dh), _pair_rstd(k2, sel, dh)
            dqn = jnp.where(low, dqa[0, r, :], dqa[1, r, :]) * scale
            dkn = dka[r, :] * LN2
            dqw_ref[...] += jnp.sum(dqn * q2 * rq, axis=0, keepdims=True)
            dkw_ref[...] += jnp.sum(dkn * k2 * rk, axis=0, keepdims=True)
            for x2, rr, dyn, o_ref in ((q2, rq, dqn * qw_ref[...], dq_ref), (k2, rk, dkn * kw_ref[...], dk_ref)):
                mean = _split_dot(dyn * x2, sel["own"], 2) * (1.0 / dh)
                o_ref[r, :] = (rr * dyn - x2 * (rr * rr * rr) * mean).astype(BF16)
            lane = _lane((tq, 128))
            dg_ref[r, :] += jnp.where(lane == 2 * p, dsrow[0, r, :], jnp.where(lane == 2 * p + 1, dsrow[1, r, :], 0.0))
            return 0

        lax.fori_loop(0, nq, post, 0)

    blk = lambda off: _once((seq, 128), lambda b, p: (b, off + p))
    own = lambda: _once((seq, 128), lambda b, p: (b, p))
    vec = lambda: pl.BlockSpec((1, 128), lambda b, p: (0, 0))
    res = pl.pallas_call(
        body, grid=(nb, 4),
        in_specs=[blk(0), blk(4), blk(8), _once((seq, 128), lambda b, p: (b, 0)), vec(), vec(), own(), own(), own()],
        out_specs=[own(), own(), own(), _once((seq, 128), lambda b, p: (b, 0)), vec(), vec()],
        out_shape=[jax.ShapeDtypeStruct((t, 512), BF16)] * 3
                  + [jax.ShapeDtypeStruct((t, 128), F32), jax.ShapeDtypeStruct((1, 128), F32), jax.ShapeDtypeStruct((1, 128), F32)],
        scratch_shapes=[pltpu.VMEM((2, seq, 128), BF16)] * 4
                       + [pltpu.VMEM((2, seq, 1), F32), pltpu.VMEM((2, seq, 128), F32), pltpu.VMEM((seq, 128), F32)],
        compiler_params=_cp("arbitrary", "arbitrary"),
        name="attn_bwd")(proj, proj, proj, gates, jnp.tile(qw, (1, 2)), jnp.tile(kw, (1, 2)), y, lse, dy)
    return list(res[:4]) + [res[4][:, :dh] + res[4][:, dh:], res[5][:, :dh] + res[5][:, dh:]]


def _silu_grad(c, sg):
    return sg * (1.0 + c * (1.0 - sg))


def _conv(x, w, n):
    row = _row(x.shape)
    c = x * w[CONV_W - 1:CONV_W, :]
    for k in range(CONV_W - 1):
        sh = CONV_W - 1 - k
        c = c + w[k:k + 1, :] * jnp.where(row >= sh, pltpu.roll(x, sh, 0), 0.0)
    return c


def gdn_pre_fwd(proj, cw, seq):
    t = proj.shape[0]
    nb = t // seq
    scale = GDN_DH ** -0.5

    def body(xq_ref, xk_ref, xv_ref, wq_ref, wk_ref, wv_ref, q_ref, k_ref, v_ref):
        def act(x_ref, w_ref):
            c = _conv(x_ref[...], w_ref[...], seq)
            return c * jax.nn.sigmoid(c)

        aq, ak = act(xq_ref, wq_ref), act(xk_ref, wk_ref)
        q_ref[...] = aq * lax.rsqrt(jnp.sum(aq * aq, axis=1, keepdims=True) + EPS) * scale
        k_ref[...] = ak * lax.rsqrt(jnp.sum(ak * ak, axis=1, keepdims=True) + EPS)
        v_ref[...] = act(xv_ref, wv_ref)

    xb = lambda off: pl.BlockSpec((seq, 128), lambda b, h: (b, off + h))
    wb = lambda off: pl.BlockSpec((CONV_W, 128), lambda b, h: (0, off + h))
    ob = lambda: pl.BlockSpec((seq, 128), lambda b, h: (b, h))
    return pl.pallas_call(
        body, grid=(nb, GDN_HEADS),
        in_specs=[xb(12), xb(16), xb(20), wb(0), wb(4), wb(8)],
        out_specs=[ob(), ob(), ob()],
        out_shape=[jax.ShapeDtypeStruct((t, 512), F32)] * 3,
        compiler_params=_cp("parallel", "parallel"), name="gdn_pre_fwd")(proj, proj, proj, cw, cw, cw)


def gdn_pre_bwd(proj, cw, dq, dk, dv, seq):
    t = proj.shape[0]
    nb = t // seq
    scale = GDN_DH ** -0.5

    def body(xq_ref, xk_ref, xv_ref, wq_ref, wk_ref, wv_ref, dq_ref, dk_ref, dv_ref,
             dxq_ref, dxk_ref, dxv_ref, dwq_ref, dwk_ref, dwv_ref):
        first = pl.program_id(1) == 0
        row = _row((seq, 128))

        def one(x_ref, w_ref, dy_ref, dx_ref, dw_ref, norm, sc):
            x, w = x_ref[...], w_ref[...]
            c = _conv(x, w, seq)
            sg = jax.nn.sigmoid(c)
            dy = dy_ref[...]
            if norm:
                a = c * sg
                rs = lax.rsqrt(jnp.sum(a * a, axis=1, keepdims=True) + EPS)
                dy = dy * sc
                da = rs * dy - a * (rs * rs * rs) * jnp.sum(dy * a, axis=1, keepdims=True)
            else:
                da = dy
            dc = da * _silu_grad(c, sg)
            dx = dc * w[CONV_W - 1:CONV_W, :]
            dws = [None] * CONV_W
            dws[CONV_W - 1] = jnp.sum(dc * x, axis=0, keepdims=True)
            for k in range(CONV_W - 1):
                sh = CONV_W - 1 - k
                dx = dx + w[k:k + 1, :] * jnp.where(row < seq - sh, pltpu.roll(dc, seq - sh, 0), 0.0)
                dws[k] = jnp.sum(dc * jnp.where(row >= sh, pltpu.roll(x, sh, 0), 0.0), axis=0, keepdims=True)
            dx_ref[...] = dx.astype(BF16)
            dwn = jnp.concatenate(dws, axis=0)

            @pl.when(first)
            def _():
                dw_ref[...] = dwn

            @pl.when(jnp.logical_not(first))
            def _():
                dw_ref[...] += dwn

        one(xq_ref, wq_ref, dq_ref, dxq_ref, dwq_ref, True, scale)
        one(xk_ref, wk_ref, dk_ref, dxk_ref, dwk_ref, True, 1.0)
        one(xv_ref, wv_ref, dv_ref, dxv_ref, dwv_ref, False, 1.0)

    xb = lambda off: pl.BlockSpec((seq, 128), lambda h, b: (b, off + h))
    wb = lambda off: pl.BlockSpec((CONV_W, 128), lambda h, b: (0, off + h))
    ob = lambda: pl.BlockSpec((seq, 128), lambda h, b: (b, h))
    return pl.pallas_call(
        body, grid=(GDN_HEADS, nb),
        in_specs=[xb(12), xb(16), xb(20), wb(0), wb(4), wb(8), ob(), ob(), ob()],
        out_specs=[ob(), ob(), ob()] + [pl.BlockSpec((CONV_W, 128), lambda h, b: (0, h))] * 3,
        out_shape=[jax.ShapeDtypeStruct((t, 512), BF16)] * 3 + [jax.ShapeDtypeStruct((CONV_W, 512), F32)] * 3,
        compiler_params=_cp("parallel", "arbitrary"), name="gdn_pre_bwd")(proj, proj, proj, cw, cw, cw, dq, dk, dv)


def _b16(x):
    return x.astype(BF16)


@jax.custom_vjp
def _mm(a, b):
    return _dot(_b16(a), _b16(b))


_mm.defvjp(lambda a, b: (_mm(a, b), (a, b)),
           lambda res, g: (_dot_nt(_b16(g), _b16(res[1])), _dot_tn(_b16(res[0]), _b16(g))))


@jax.custom_vjp
def _mm_nt(a, b):
    return _dot_nt(_b16(a), _b16(b))


_mm_nt.defvjp(lambda a, b: (_mm_nt(a, b), (a, b)),
              lambda res, g: (_dot(_b16(g), _b16(res[1])), _dot_tn(_b16(g), _b16(res[0]))))


@jax.custom_vjp
def _mm_tn(a, b):
    return _dot_tn(_b16(a), _b16(b))


_mm_tn.defvjp(lambda a, b: (_mm_tn(a, b), (a, b)),
              lambda res, g: (_dot_nt(_b16(res[1]), _b16(g)), _dot(_b16(res[0]), _b16(g))))


def _dot32(a, b, dims=(((1,), (0,)), ((), ()))):
    def split(x):
        hi = x.astype(BF16)
        return hi, (x - hi.astype(F32)).astype(BF16)

    (ah, al), (bh, bl) = split(a), split(b)
    d = lambda x, y: lax.dot_general(x, y, dims, preferred_element_type=F32)
    return d(ah, bh) + (d(ah, bl) + d(al, bh))


def _inv_fwd_many(mats):
    n = mats[0].shape[0]
    eye = (_row((n, n)) == _lane((n, n))).astype(F32)
    invs, pws = [eye - a for a in mats], list(mats)
    for _ in range(n.bit_length() - 2):
        pws = [_dot32(pw, pw) for pw in pws]
        invs = [inv + _dot32(inv, pw) for inv, pw in zip(invs, pws)]
    return invs


@jax.custom_vjp
def _inv_saved(a, inv):
    return inv


def _inv_saved_bwd(inv, g):
    tg = _dot32(inv, g, (((0,), (0,)), ((), ())))
    return -_dot32(tg, inv, (((1,), (1,)), ((), ()))), jnp.zeros_like(inv)


_inv_saved.defvjp(lambda a, inv: (inv, inv), _inv_saved_bwd)


def _gdn_decay(gcol):
    c = CHUNK
    ri, ci = _row((c, c)), _lane((c, c))
    incl, eye = ri >= ci, ri == ci
    grow = jnp.sum(jnp.where(eye, gcol, 0.0), axis=0, keepdims=True)
    gc = jnp.sum(jnp.where(incl, grow, 0.0), axis=1, keepdims=True)
    gcr = jnp.sum(jnp.where(eye, gc, 0.0), axis=0, keepdims=True)
    gl = jnp.sum(jnp.where(_row((c, 1)) == c - 1, gc, 0.0), axis=0, keepdims=True)
    return gc, gl, jnp.exp(jnp.where(incl, gc - gcr, NEG))


def _gdn_a(k, bcol, decay):
    c = CHUNK
    return jnp.where(_row((c, c)) > _lane((c, c)), _mm_nt(k * bcol, k) * decay, 0.0)


def _gdn_chunk(q, k, v, gcol, bcol, state, gg, nw, inv_saved):
    c = CHUNK
    incl = _row((c, c)) >= _lane((c, c))
    gc, gl, decay = _gdn_decay(gcol)
    kb, vb = k * bcol, v * bcol
    inv = _inv_saved(_gdn_a(k, bcol, decay), inv_saved)
    eg = jnp.exp(gc)
    u = _mm(inv, vb)
    w = _mm(inv, kb * eg)
    pm = jnp.where(incl, _mm_nt(q, k) * decay, 0.0)
    kd = k * jnp.exp(gl - gc)
    qd = q * eg
    v_new = u - _mm(w, state)
    o = _mm(qd, state) + _mm(pm, v_new)
    state_new = state * jnp.exp(gl) + _mm_tn(kd, v_new)
    y = o * _rstd(o) * nw * (gg * jax.nn.sigmoid(gg))
    return y, state_new


_gdn_chunks = jax.vmap(_gdn_chunk, in_axes=(0, 0, 0, 0, 0, 0, 0, None, 0))


def _gdn_chain_inputs(chains, p, r, c, q_ref, k_ref, v_ref, g_ref, gg_ref, inv_ref):
    cols = {nm: [] for nm in ("q", "k", "v", "g", "b", "gg", "inv")}
    for b, hh in chains:
        h = GDN_HPS * p + hh
        ln = slice(hh * 128, (hh + 1) * 128)
        gt = g_ref[b, r, :]
        cols["q"].append(q_ref[b, r, ln])
        cols["k"].append(k_ref[b, r, ln])
        cols["v"].append(v_ref[b, r, ln])
        cols["g"].append(_pick_lane(gt, 8 + h))
        cols["b"].append(_pick_lane(gt, 12 + h))
        cols["gg"].append(gg_ref[b, r, ln])
        cols["inv"].append(inv_ref[b, hh, c])
    return [jnp.stack(cols[nm]) for nm in ("q", "k", "v", "g", "b", "gg", "inv")]


GDN_CB = 8
GDN_HPS = 4


def gdn_inv(k, gates, seq):
    t = k.shape[0]
    nb, nc = t // seq, seq // CHUNK
    rb = GDN_CB * CHUNK
    nsb = seq // rb

    def body(k_ref, g_ref, o_ref):
        h = pl.program_id(1)
        mats = []
        for c in range(GDN_CB):
            r = slice(c * CHUNK, (c + 1) * CHUNK)
            gt = g_ref[r, :]
            _, _, decay = _gdn_decay(_pick_lane(gt, 8 + h))
            mats.append(_gdn_a(k_ref[r, :], _pick_lane(gt, 12 + h), decay))
        for c, inv in enumerate(_inv_fwd_many(mats)):
            o_ref[c] = inv

    return pl.pallas_call(
        body, grid=(nb, GDN_HEADS, nsb),
        in_specs=[pl.BlockSpec((rb, 128), lambda b, h, s: (b * nsb + s, h)),
                  pl.BlockSpec((rb, 128), lambda b, h, s: (b * nsb + s, 0))],
        out_specs=pl.BlockSpec((None, None, GDN_CB, CHUNK, CHUNK), lambda b, h, s: (b, h, s, 0, 0)),
        out_shape=jax.ShapeDtypeStruct((nb, GDN_HEADS, nc, CHUNK, CHUNK), F32),
        compiler_params=_cp("parallel", "parallel", "parallel"), name="gdn_inv")(k, gates)


def _gdn_specs(nb, nsb, cb, rev):
    blk = (lambda s: nsb - 1 - s) if rev else (lambda s: s)
    rb = cb * CHUNK
    pair = lambda off=0: pl.BlockSpec((nb, rb, 128 * GDN_HPS), lambda s, p: (0, blk(s), off + p))
    gate = lambda: pl.BlockSpec((nb, rb, 128), lambda s, p: (0, blk(s), 0))
    mats = lambda n: pl.BlockSpec((nb, GDN_HPS, cb, n, n), lambda s, p: (0, p, blk(s), 0, 0))
    return pair, gate, mats


def gdn_fwd(q, k, v, gates, proj, nw, inv, seq):
    t = q.shape[0]
    nb, nc = t // seq, seq // CHUNK
    cb = GDN_CB
    nsb = nc // cb
    chains = [(b, hh) for b in range(nb) for hh in range(GDN_HPS)]
    nch = len(chains)
    pair, gate, mats = _gdn_specs(nb, nsb, cb, False)

    def body(q_ref, k_ref, v_ref, g_ref, gg_ref, inv_ref, nw_ref, y_ref, st_ref, carry):
        s, p = pl.program_id(0), pl.program_id(1)

        @pl.when(s == 0)
        def _():
            for ci in range(nch):
                carry[p * nch + ci] = jnp.zeros((GDN_DH, GDN_DH), F32)

        def step(c, states):
            r = _rows(c, CHUNK)
            for ci, (b, hh) in enumerate(chains):
                st_ref[b, hh, c] = states[ci]
            ins = _gdn_chain_inputs(chains, p, r, c, q_ref, k_ref, v_ref, g_ref, gg_ref, inv_ref)
            y, states = _gdn_chunks(*ins[:5], states, ins[5], nw_ref[...], ins[6])
            for ci, (b, hh) in enumerate(chains):
                y_ref[b, r, hh * 128:(hh + 1) * 128] = y[ci]
            return states

        states = lax.fori_loop(0, cb, step, jnp.stack([carry[p * nch + ci] for ci in range(nch)]))
        for ci in range(nch):
            carry[p * nch + ci] = states[ci]

    v3 = lambda a: a.reshape(nb, seq, a.shape[1])
    y, st = pl.pallas_call(
        body, grid=(nsb, GDN_HEADS // GDN_HPS),
        in_specs=[pair(), pair(), pair(), gate(), pair(24 // GDN_HPS), mats(CHUNK), pl.BlockSpec((1, 128), lambda s, p: (0, 0))],
        out_specs=[pair(), mats(GDN_DH)],
        out_shape=[jax.ShapeDtypeStruct((nb, seq, 512), F32),
                   jax.ShapeDtypeStruct((nb, GDN_HEADS, nc, GDN_DH, GDN_DH), F32)],
        scratch_shapes=[pltpu.VMEM((GDN_HEADS // GDN_HPS * nch, GDN_DH, GDN_DH), F32)],
        compiler_params=_cp("arbitrary", "arbitrary"), name="gdn_fwd")(v3(q), v3(k), v3(v), v3(gates), v3(proj), inv, nw)
    return y.reshape(t, 512), st


def gdn_bwd(q, k, v, gates, proj, nw, inv, states, dy, seq):
    t = q.shape[0]
    nb, nc = t // seq, seq // CHUNK
    cb = GDN_CB // 2
    nsb = nc // cb
    chains = [(b, hh) for b in range(nb) for hh in range(GDN_HPS)]
    nch = len(chains)
    pair, gate, mats = _gdn_specs(nb, nsb, cb, True)

    def body(q_ref, k_ref, v_ref, g_ref, gg_ref, inv_ref, st_ref, dy_ref, nw_ref,
             dq_ref, dk_ref, dv_ref, dgg_ref, dg_ref, dnw_ref, carry):
        s, p = pl.program_id(0), pl.program_id(1)

        @pl.when((s == 0) & (p == 0))
        def _():
            dnw_ref[...] = jnp.zeros_like(dnw_ref)

        @pl.when(p == 0)
        def _():
            dg_ref[...] = jnp.zeros_like(dg_ref)

        @pl.when(s == 0)
        def _():
            for ci in range(nch):
                carry[p * nch + ci] = jnp.zeros((GDN_DH, GDN_DH), F32)

        def step(idx, dstates):
            c = cb - 1 - idx
            r = _rows(c, CHUNK)
            ins = _gdn_chain_inputs(chains, p, r, c, q_ref, k_ref, v_ref, g_ref, gg_ref, inv_ref)
            st = jnp.stack([st_ref[b, hh, c] for b, hh in chains])
            dy = jnp.stack([dy_ref[b, r, hh * 128:(hh + 1) * 128] for b, hh in chains])
            _, vjp = jax.vjp(_gdn_chunks, *ins[:5], st, ins[5], nw_ref[...], ins[6])
            dq, dk, dv, dgc, dbc, dstates, dgg, dnw, _ = vjp((dy, dstates))
            dnw_ref[...] += dnw
            lane = _lane((CHUNK, 128))
            for ci, (b, hh) in enumerate(chains):
                h = GDN_HPS * p + hh
                ln = slice(hh * 128, (hh + 1) * 128)
                dq_ref[b, r, ln] = dq[ci]
                dk_ref[b, r, ln] = dk[ci]
                dv_ref[b, r, ln] = dv[ci]
                dgg_ref[b, r, ln] = dgg[ci].astype(BF16)
                dg_ref[b, r, :] = jnp.where(lane == 8 + h, dgc[ci], jnp.where(lane == 12 + h, dbc[ci], dg_ref[b, r, :]))
            return dstates

        dstates = lax.fori_loop(0, cb, step, jnp.stack([carry[p * nch + ci] for ci in range(nch)]))
        for ci in range(nch):
            carry[p * nch + ci] = dstates[ci]

    v3 = lambda a: a.reshape(nb, seq, a.shape[1])
    res = pl.pallas_call(
        body, grid=(nsb, GDN_HEADS // GDN_HPS),
        in_specs=[pair(), pair(), pair(), gate(), pair(24 // GDN_HPS), mats(CHUNK), mats(GDN_DH), pair(),
                  pl.BlockSpec((1, 128), lambda s, p: (0, 0))],
        out_specs=[pair(), pair(), pair(), pair(), gate(), pl.BlockSpec((1, 128), lambda s, p: (0, 0))],
        out_shape=[jax.ShapeDtypeStruct((nb, seq, 512), F32)] * 3 + [jax.ShapeDtypeStruct((nb, seq, 512), BF16)]
                  + [jax.ShapeDtypeStruct((nb, seq, 128), F32), jax.ShapeDtypeStruct((1, 128), F32)],
        scratch_shapes=[pltpu.VMEM((GDN_HEADS // GDN_HPS * nch, GDN_DH, GDN_DH), F32)],
        compiler_params=_cp("arbitrary", "arbitrary"),
        name="gdn_bwd")(v3(q), v3(k), v3(v), v3(gates), v3(proj), inv, states, v3(dy), nw)
    return [a.reshape(t, a.shape[2]) for a in res[:5]] + [res[5]]


def loss_head(y, target, tm=512):
    t, d = y.shape

    def body(y_ref, t_ref, s_ref, dy_ref):
        @pl.when(pl.program_id(0) == 0)
        def _():
            s_ref[...] = jnp.zeros_like(s_ref)

        err = y_ref[...] - t_ref[...]
        s_ref[...] += jnp.sum(err * err, axis=0, keepdims=True)
        dy_ref[...] = err * (1.0 / d)

    return pl.pallas_call(
        body, grid=(t // tm,),
        in_specs=[pl.BlockSpec((tm, d), lambda i: (i, 0)), pl.BlockSpec((tm, d), lambda i: (i, 0))],
        out_specs=[pl.BlockSpec((1, d), lambda i: (0, 0)), pl.BlockSpec((tm, d), lambda i: (i, 0))],
        out_shape=[jax.ShapeDtypeStruct((1, d), F32), jax.ShapeDtypeStruct((t, d), F32)],
        compiler_params=_cp("arbitrary"), name="loss_head")(y, target)


def _place():
    return lax.axis_index("x"), lax.axis_index("y"), lax.axis_index("c")


def _peer(k):
    x, y, c = _place()
    px = 1 - x if (k >> 2) & 1 else x
    py = 1 - y if (k >> 1) & 1 else y
    pc = 1 - c if k & 1 else c
    return (px, py, pc), 4 * px + 2 * py + pc


_ANY = pl.BlockSpec(memory_space=pl.ANY)
_SEM = pl.BlockSpec(memory_space=pltpu.SEMAPHORE)
_EFFECT = pltpu.SideEffectType.DATAFLOW_SIDE_EFFECTING


def _me():
    x, y, c = _place()
    return 4 * x + 2 * y + c


def _remote_copy(ins, lands, scatter, send_sems, recv_sems, a, k, arriving):
    pid, pidx = _peer(k)
    return pltpu.make_async_remote_copy(src_ref=ins[a].at[pidx] if scatter[a] else ins[a],
                                        dst_ref=lands[a].at[pidx if arriving else _me()],
                                        send_sem=send_sems.at[a * N_DEV + k], recv_sem=recv_sems.at[a * N_DEV + k],
                                        device_id=pid, device_id_type=MESH)


def _local_copy(ins, lands, scatter, loc_sems, a):
    me = _me()
    return pltpu.make_async_copy(ins[a].at[me] if scatter[a] else ins[a], lands[a].at[me], loc_sems.at[a])


def exchange_start(arrays, scatter, name, after):
    n = len(arrays)
    lands = [lax.empty(a.shape if s else (N_DEV,) + a.shape, a.dtype) for a, s in zip(arrays, scatter)]

    def body(*refs):
        ins, lds = refs[:n], refs[n:2 * n]
        send_sems, recv_sems, loc_sems = refs[2 * n + 1:2 * n + 4]
        token = refs[-1]
        for k in range(1, N_DEV):
            for a in range(n):
                _remote_copy(ins, lds, scatter, send_sems, recv_sems, a, k, False).start()
        for a in range(n):
            _local_copy(ins, lds, scatter, loc_sems, a).start()
        token[...] = jnp.zeros_like(token)

    hbm = lambda a: pltpu.HBM(a.shape, a.dtype)
    res = pl.pallas_call(
        body, name=name,
        in_specs=[_ANY] * (2 * n + 1),
        out_specs=[_SEM, _SEM, _SEM] + [_ANY] * (2 * n) + [pl.BlockSpec(memory_space=pltpu.VMEM)],
        out_shape=[pltpu.SemaphoreType.DMA((n * N_DEV,)), pltpu.SemaphoreType.DMA((n * N_DEV,)),
                   pltpu.SemaphoreType.DMA((n,))]
                  + [hbm(a) for a in arrays] + [hbm(a) for a in lands] + [jax.ShapeDtypeStruct((8, 128), F32)],
        input_output_aliases={i: 3 + i for i in range(2 * n)},
        compiler_params=pltpu.CompilerParams(has_side_effects=_EFFECT),
    )(*[pltpu.with_memory_space_constraint(a, pltpu.HBM) for a in list(arrays) + lands], after)
    return res[0:3], res[3:3 + n], res[3 + n:3 + 2 * n], res[-1]


def exchange_wait(sems, arrays, lands, scatter, after, name):
    n = len(arrays)

    def body(*refs):
        ins, lds = refs[:n], refs[n:2 * n]
        ssem, rsem, lsem = refs[2 * n:2 * n + 3]
        for a in range(n):
            _local_copy(ins, lds, scatter, lsem, a).wait()
        for k in range(1, N_DEV):
            for a in range(n):
                _remote_copy(ins, lds, scatter, ssem, rsem, a, k, True).wait_recv()
        for k in range(1, N_DEV):
            for a in range(n):
                _remote_copy(ins, lds, scatter, ssem, rsem, a, k, False).wait_send()

    hbm = lambda a: pltpu.HBM(a.shape, a.dtype)
    res = pl.pallas_call(
        body, name=name,
        in_specs=[_ANY] * (2 * n) + [_SEM, _SEM, _SEM, _ANY],
        out_specs=[_ANY] * (2 * n),
        out_shape=[hbm(a) for a in arrays] + [hbm(a) for a in lands],
        input_output_aliases={i: i for i in range(2 * n)},
        compiler_params=pltpu.CompilerParams(has_side_effects=_EFFECT),
    )(*arrays, *lands, *sems, after)
    return list(res[n:])


def exchange_begin(arrays, scatter, name, after):
    sems, arrays_thru, lands_thru, token = exchange_start(arrays, scatter, name + "_start", after)
    return (sems, arrays_thru, lands_thru, scatter, name), token


def exchange_end(state, after):
    sems, arrays_thru, lands_thru, scatter, name = state
    return exchange_wait(sems, arrays_thru, lands_thru, scatter, after, name + "_wait")


def adamw_reduce(slots, w, m, v, l, name, after=None, prev=None):
    nl, r, c = w.shape
    tr = r
    while tr * c * 4 > (1 << 20) and tr % 16 == 0:
        tr //= 2
    bc1 = 1.0 - ADAM_B1 ** ADAM_STEP
    bc2 = 1.0 - ADAM_B2 ** ADAM_STEP

    def body(s_ref, w_ref, m_ref, v_ref, *rest):
        g_ref, d_ref, nm_ref, nv_ref = rest[-4:]
        g = s_ref[0].astype(F32)
        for j in range(1, N_DEV):
            g = g + s_ref[j].astype(F32)
        nm = ADAM_B1 * m_ref[...] + (1.0 - ADAM_B1) * g
        nv = ADAM_B2 * v_ref[...] + (1.0 - ADAM_B2) * (g * g)
        g_ref[...] = g
        nm_ref[...] = nm
        nv_ref[...] = nv
        d_ref[...] = -ADAM_LR * ((nm / bc1) / (jnp.sqrt(nv / bc2) + ADAM_EPS) + ADAM_WD * w_ref[...])

    blk = lambda: pl.BlockSpec((None, tr, c), lambda i: (l, i, 0))
    extra = ([] if after is None else [after]) + ([] if prev is None else list(prev))
    first_prev = 4 + (after is not None)
    return pl.pallas_call(
        body, grid=(r // tr,),
        in_specs=[pl.BlockSpec((N_DEV, tr, c), lambda i: (0, i, 0)), blk(), blk(), blk()] + [_ANY] * len(extra),
        out_specs=[blk(), blk(), blk(), blk()],
        out_shape=[jax.ShapeDtypeStruct((nl, r, c), F32)] * 4,
        input_output_aliases={} if prev is None else {first_prev + j: j for j in range(4)},
        compiler_params=_cp("parallel"), name=name)(slots, w, m, v, *extra)


BIG = ("ffn1_w_in", "ffn1_w_out", "w_in", "gdn_conv", "w_out", "ffn2_w_in", "ffn2_w_out")
GROUPS = (BIG[0:2], BIG[2:5], BIG[5:7])
SMALL = ("ffn1_norm", "mix_norm", "fox_q_norm", "fox_k_norm", "fox_f_bias", "gdn_a_log", "gdn_dt_bias",
         "gdn_out_norm", "ffn2_norm")
WEIGHTS = ("ffn1_norm", "ffn1_w_in", "ffn1_w_out", "mix_norm", "w_in", "fox_q_norm", "fox_k_norm", "fox_f_bias",
           "gdn_conv", "gdn_a_log", "gdn_dt_bias", "gdn_out_norm", "w_out", "ffn2_norm", "ffn2_w_in", "ffn2_w_out")
IN_COLS = (("fq", 512), ("fk", 512), ("fv", 512), ("ff", 8), ("gq", 512), ("gk", 512), ("gv", 512),
           ("ga", 4), ("gb", 4), ("gg", 512))
MY_BIG = ("fq", "fk", "fv", "gq", "gk", "gv", "gg")
MY_SMALL = ("ff", "ga", "gb")
SMALL_ROWS = 8 * 128


def _in_cols_to_mine(w):
    off, parts = 0, {}
    for nm, wd in IN_COLS:
        parts[nm] = w[:, off:off + wd]
        off += wd
    small = jnp.concatenate([parts[nm] for nm in MY_SMALL], axis=1)
    small = jnp.pad(small, ((0, 0), (0, 128 - small.shape[1])))
    return jnp.concatenate([parts[nm] for nm in MY_BIG] + [small], axis=1)


def _in_cols_from_mine(g):
    parts = {nm: g[:, i * 512:(i + 1) * 512] for i, nm in enumerate(MY_BIG)}
    off = N_BIG
    for nm in MY_SMALL:
        wd = dict(IN_COLS)[nm]
        parts[nm] = g[:, off:off + wd]
        off += wd
    return jnp.concatenate([parts[nm] for nm, _ in IN_COLS], axis=1)


def _pack_small(vals):
    rows = []
    nl = vals[SMALL[0]].shape[0]
    for l in range(nl):
        for nm in SMALL:
            v = vals[nm][l].reshape(-1)
            pad = (-v.shape[0]) % SMALL_ROWS
            rows.append(jnp.pad(v, (0, pad)).reshape(-1, 128))
    return jnp.concatenate(rows, axis=0)


def _unpack_small(packed, like):
    out = {nm: [] for nm in SMALL}
    row = 0
    nl = like[SMALL[0]].shape[0]
    for l in range(nl):
        for nm in SMALL:
            n = like[nm].shape[1]
            nr = -(-n // SMALL_ROWS) * 8
            out[nm].append(packed[row:row + nr].reshape(-1)[:n])
            row += nr
    return {nm: jnp.stack(v) for nm, v in out.items()}


def kernel(x, ffn1_norm, ffn1_w_in, ffn1_w_out, mix_norm, w_in, fox_q_norm, fox_k_norm, fox_f_bias, gdn_conv, gdn_a_log, gdn_dt_bias, gdn_out_norm, w_out, ffn2_norm, ffn2_w_in, ffn2_w_out, loss_target, m_ffn1_norm, m_ffn1_w_in, m_ffn1_w_out, m_mix_norm, m_w_in, m_fox_q_norm, m_fox_k_norm, m_fox_f_bias, m_gdn_conv, m_gdn_a_log, m_gdn_dt_bias, m_gdn_out_norm, m_w_out, m_ffn2_norm, m_ffn2_w_in, m_ffn2_w_out, v_ffn1_norm, v_ffn1_w_in, v_ffn1_w_out, v_mix_norm, v_w_in, v_fox_q_norm, v_fox_k_norm, v_fox_f_bias, v_gdn_conv, v_gdn_a_log, v_gdn_dt_bias, v_gdn_out_norm, v_w_out, v_ffn2_norm, v_ffn2_w_in, v_ffn2_w_out):
    wts = dict(ffn1_norm=ffn1_norm, ffn1_w_in=ffn1_w_in, ffn1_w_out=ffn1_w_out, mix_norm=mix_norm, w_in=w_in,
               fox_q_norm=fox_q_norm, fox_k_norm=fox_k_norm, fox_f_bias=fox_f_bias, gdn_conv=gdn_conv,
               gdn_a_log=gdn_a_log, gdn_dt_bias=gdn_dt_bias, gdn_out_norm=gdn_out_norm, w_out=w_out,
               ffn2_norm=ffn2_norm, ffn2_w_in=ffn2_w_in, ffn2_w_out=ffn2_w_out)
    mom = dict(ffn1_norm=m_ffn1_norm, ffn1_w_in=m_ffn1_w_in, ffn1_w_out=m_ffn1_w_out, mix_norm=m_mix_norm, w_in=m_w_in,
               fox_q_norm=m_fox_q_norm, fox_k_norm=m_fox_k_norm, fox_f_bias=m_fox_f_bias, gdn_conv=m_gdn_conv,
               gdn_a_log=m_gdn_a_log, gdn_dt_bias=m_gdn_dt_bias, gdn_out_norm=m_gdn_out_norm, w_out=m_w_out,
               ffn2_norm=m_ffn2_norm, ffn2_w_in=m_ffn2_w_in, ffn2_w_out=m_ffn2_w_out)
    var = dict(ffn1_norm=v_ffn1_norm, ffn1_w_in=v_ffn1_w_in, ffn1_w_out=v_ffn1_w_out, mix_norm=v_mix_norm, w_in=v_w_in,
               fox_q_norm=v_fox_q_norm, fox_k_norm=v_fox_k_norm, fox_f_bias=v_fox_f_bias, gdn_conv=v_gdn_conv,
               gdn_a_log=v_gdn_a_log, gdn_dt_bias=v_gdn_dt_bias, gdn_out_norm=v_gdn_out_norm, w_out=v_w_out,
               ffn2_norm=v_ffn2_norm, ffn2_w_in=v_ffn2_w_in, ffn2_w_out=v_ffn2_w_out)
    nb, seq, d = x.shape
    t = nb * seq
    depth = ffn1_norm.shape[0]

    stages = [(l, gi) for l in range(depth) for gi in range(len(GROUPS))]

    def shards_of(l, gi):
        return [wts[nm][l] if nm == "gdn_conv" else wts[nm][l].astype(BF16) for nm in GROUPS[gi]]

    def behind(nw, token):
        return nw if token is None else nw + token[0:1, 0:1]

    def small_params(l):
        return dict(
            n1=ffn1_norm[l][None], nmix=mix_norm[l][None], n2=ffn2_norm[l][None],
            qw=fox_q_norm[l][None], kw=fox_k_norm[l][None], onw=gdn_out_norm[l][None],
            gp=jnp.concatenate([
                jnp.concatenate([fox_f_bias[l], gdn_dt_bias[l], jnp.zeros((116,), F32)])[None],
                jnp.concatenate([jnp.zeros((8,), F32), gdn_a_log[l], jnp.zeros((116,), F32)])[None],
                jnp.zeros((6, 128), F32)], axis=0))

    h = x.reshape(t, d)
    state, token = exchange_begin(shards_of(0, 0), [False] * len(GROUPS[0]), "gather_0", ffn1_norm)
    landed = exchange_end(state, token)
    saved = [dict(p=small_params(l)) for l in range(depth)]
    for k, (l, gi) in enumerate(stages):
        s, w, token = saved[l], landed, None
        p = s["p"]
        if k + 1 < len(stages):
            nl, ng = stages[k + 1]
            state, token = exchange_begin(shards_of(nl, ng), [False] * len(GROUPS[ng]), f"gather_{k + 1}", landed[0])
        if gi == 0:
            fb = w[0].shape[2]
            p["w1i"], p["w1o"] = w[0].reshape(2, 4, d, fb), w[1].reshape(4, fb, d)
            s["x0"] = h
            h, *s["ffn1"] = ffn_fwd(h, behind(p["n1"], token), p["w1i"], p["w1o"])
            s["x1"] = h
        elif gi == 1:
            p["wi"] = _in_cols_to_mine(w[0].transpose(1, 0, 2).reshape(d, -1))
            p["cw"] = w[1].transpose(1, 0, 2).reshape(CONV_W, -1)
            p["wo"] = w[2].reshape(d, d)
            proj, hn = inproj_fwd(h, behind(p["nmix"], token), p["wi"])
            gates = gates_fwd(proj, p["gp"], seq)
            yf, lse = attn_fwd(proj, gates, p["qw"], p["kw"], seq, tq=min(seq, ATTN_TQ_FWD))
            qh, kh, vh = gdn_pre_fwd(proj, p["cw"], seq)
            inv = gdn_inv(kh, gates, seq)
            yg, st = gdn_fwd(qh, kh, vh, gates, proj, p["onw"], inv, seq)
            h, ycat = outproj_fwd(h, yf, yg, p["wo"])
            s.update(x2=h, proj=proj, hn=hn, gates=gates, yf=yf, lse=lse, qh=qh, kh=kh, vh=vh, st=st, inv=inv, ycat=ycat)
        else:
            fb = w[0].shape[2]
            p["w2i"], p["w2o"] = w[0].reshape(2, 4, d, fb), w[1].reshape(4, fb, d)
            h, *s["ffn2"] = ffn_fwd(h, behind(p["n2"], token), p["w2i"], p["w2o"])
        if k + 1 < len(stages):
            landed = exchange_end(state, h)

    sq, dh = loss_head(h, loss_target.reshape(t, d))
    loss = lax.psum(0.5 * jnp.sum(sq) / d, ("x", "y", "c"))

    got = [None] * len(stages)
    pending, token = None, None
    gsmall = {nm: [None] * depth for nm in SMALL}
    for k in reversed(range(len(stages))):
        l, gi = stages[k]
        s = saved[l]
        p = s["p"]
        if gi != 1:
            nw, xin, wi_, wo_, nm_n, (xn, gu, hh) = (
                (p["n1"], s["x0"], p["w1i"], p["w1o"], "ffn1_norm", s["ffn1"]) if gi == 0 else
                (p["n2"], s["x2"], p["w2i"], p["w2o"], "ffn2_norm", s["ffn2"]))
            dh, dn, dgu, dyh = ffn_bwd(xin, dh, behind(nw, token), gu, wi_, wo_)
            g_in, g_out = wgrad_ffn_in(xn, dgu), wgrad_ffn_out(hh, dyh)
            send = [g_in.reshape(N_DEV, d, g_in.shape[3]), g_out.reshape(N_DEV, -1, d)]
            gsmall[nm_n][l] = dn[0]
        else:
            dyf, dyg, dyb = outproj_bwd(dh, p["wo"], token)
            g_wo = wgrad_2d(s["ycat"], dyb, 512, "wgrad_w_out")
            dq, dk, dv, dga, dqw, dkw = attn_bwd(s["proj"], s["gates"], p["qw"], p["kw"], s["yf"], s["lse"], dyf, seq,
                                                 tq=min(seq, ATTN_TQ_BWD))
            dqh, dkh, dvh, dgg, dgb, donw = gdn_bwd(s["qh"], s["kh"], s["vh"], s["gates"], s["proj"], p["onw"],
                                                     s["inv"], s["st"], dyg, seq)
            dxq, dxk, dxv, dwq, dwk, dwv = gdn_pre_bwd(s["proj"], p["cw"], dqh, dkh, dvh, seq)
            dsm, dgp = gates_bwd(s["proj"], p["gp"], dga, dgb, seq)
            dh, dnmix, dproj = inproj_bwd(s["x1"], dh, p["nmix"], p["wi"], [dq, dk, dv, dxq, dxk, dxv, dgg, dsm])
            g_wi = wgrad_2d(s["hn"], dproj, 512, "wgrad_w_in", F32)
            g_cw = jnp.concatenate([dwq, dwk, dwv], axis=1)
            send = [_in_cols_from_mine(g_wi).reshape(d, N_DEV, -1).transpose(1, 0, 2),
                    g_cw.reshape(CONV_W, N_DEV, -1).transpose(1, 0, 2), g_wo.reshape(N_DEV, -1, d)]
            for nm, val in (("mix_norm", dnmix[0]), ("fox_q_norm", dqw[0]), ("fox_k_norm", dkw[0]),
                            ("fox_f_bias", dgp[0, 0:8]), ("gdn_a_log", dgp[1, 8:12]), ("gdn_dt_bias", dgp[0, 8:12]),
                            ("gdn_out_norm", donw[0])):
                gsmall[nm][l] = val
        flags = [True] * len(send)
        if k == 0:
            send.append(_pack_small({nm: jnp.stack(v) for nm, v in gsmall.items()}))
            flags.append(False)
        prev = dh
        if pending is not None:
            got[pending[1]] = exchange_end(pending[0], dh)
            prev = got[pending[1]][0]
        state, token = exchange_begin(send, flags, f"exchange_grads_{k}", prev)
        pending = (state, k)
    grad_x = dh.reshape(nb, seq, d)

    res = {}

    def update_stage(k, slots, after):
        l, gi = stages[k]
        for i, nm in enumerate(GROUPS[gi]):
            r, c = wts[nm].shape[1:]
            res[nm] = adamw_reduce(slots[i].reshape(N_DEV, r, c), wts[nm], mom[nm], var[nm], l, f"adamw_{nm}_{l}",
                                   after, res.get(nm))
            if after is not None:
                after = res[nm][0]
        return after

    last = token
    for k in range(1, len(stages)):
        last = update_stage(k, got[k], last)
    got[0] = exchange_end(pending[0], last)
    update_stage(0, got[0], None)
    small_like = {nm: wts[nm] for nm in SMALL}
    sm = adamw_reduce(got[0][-1], _pack_small(small_like)[None], _pack_small({nm: mom[nm] for nm in SMALL})[None],
                      _pack_small({nm: var[nm] for nm in SMALL})[None], 0, "adamw_small")
    sm = [_unpack_small(a[0], small_like) for a in sm]
    for nm in SMALL:
        res[nm] = [sm[j][nm] for j in range(4)]
    return (loss, grad_x, *[res[nm][0] for nm in WEIGHTS], *[res[nm][1] for nm in WEIGHTS],
            *[res[nm][2] for nm in WEIGHTS], *[res[nm][3] for nm in WEIGHTS])
```

```python
import functools

import jax
import jax.numpy as jnp
from jax import lax
from jax.experimental import pallas as pl
from jax.experimental.pallas import tpu as pltpu

F32 = jnp.float32
BF16 = jnp.bfloat16
EPS = 1e-6
N_DEV = 8
MESH = pl.DeviceIdType.MESH
HIGHEST = lax.Precision.HIGHEST
VMEM_LIMIT = 56 * 1024 * 1024

FOX_HEADS, FOX_DH = 8, 64
GDN_HEADS, GDN_DH = 4, 128
CHUNK = 64
CONV_W = 4

ADAM_LR, ADAM_B1, ADAM_B2, ADAM_EPS, ADAM_WD, ADAM_STEP = 0.001, 0.9, 0.999, 1e-08, 0.01, 10


def _cp(*sem):
    return pltpu.CompilerParams(dimension_semantics=sem, vmem_limit_bytes=VMEM_LIMIT)


def _dot(a, b):
    return jnp.dot(a, b, preferred_element_type=F32)


def _dot_nt(a, b):
    return lax.dot_general(a, b, (((1,), (1,)), ((), ())), preferred_element_type=F32)


def _dot_tn(a, b):
    return lax.dot_general(a, b, (((0,), (0,)), ((), ())), preferred_element_type=F32)


def _rstd(xf):
    return lax.rsqrt(jnp.mean(xf * xf, axis=-1, keepdims=True) + EPS)


def _rms_bwd(xf, r, dyn):
    return r * dyn - xf * (r * r * r) * jnp.mean(dyn * xf, axis=-1, keepdims=True)


def ffn_fwd(x, nw, w_in, w_out, tm=1024, rc=1024):
    t, d = x.shape
    nj, fb = w_out.shape[0], w_out.shape[1]
    tm = min(tm, t)
    rc = min(rc, tm)

    def body(x_ref, nw_ref, wi_ref, wo_ref, o_ref, xn_ref, gu_ref, h_ref, acc_ref):
        j = pl.program_id(1)

        @pl.when(j == 0)
        def _():
            xf = x_ref[...]
            xn_ref[...] = (xf * _rstd(xf) * nw_ref[...]).astype(BF16)
            acc_ref[...] = jnp.zeros_like(acc_ref)

        rows = [slice(c * rc, (c + 1) * rc) for c in range(tm // rc)]
        gs = [_dot(xn_ref[r, :], wi_ref[0]) for r in rows]
        us = [_dot(xn_ref[r, :], wi_ref[1]) for r in rows]
        hs = []
        for g, u, r in zip(gs, us, rows):
            sg = jax.nn.sigmoid(g)
            silu = g * sg
            h = (silu * u).astype(BF16)
            gu_ref[0, r, :] = (u * (sg * (1.0 + g * (1.0 - sg)))).astype(BF16)
            gu_ref[1, r, :] = silu.astype(BF16)
            h_ref[r, :] = h
            hs.append(h)
        for h, r in zip(hs, rows):
            acc_ref[r, :] += _dot(h, wo_ref[...])

        @pl.when(j == nj - 1)
        def _():
            o_ref[...] = x_ref[...] + 0.5 * acc_ref[...]

    return pl.pallas_call(
        body, grid=(t // tm, nj),
        in_specs=[pl.BlockSpec((tm, d), lambda i, j: (i, 0)),
                  pl.BlockSpec((1, d), lambda i, j: (0, 0)),
                  pl.BlockSpec((2, None, d, fb), lambda i, j: (0, j, 0, 0)),
                  pl.BlockSpec((None, fb, d), lambda i, j: (j, 0, 0))],
        out_specs=[pl.BlockSpec((tm, d), lambda i, j: (i, 0)),
                   pl.BlockSpec((tm, d), lambda i, j: (i, 0)),
                   pl.BlockSpec((2, None, tm, fb), lambda i, j: (0, j, i, 0)),
                   pl.BlockSpec((None, tm, fb), lambda i, j: (j, i, 0))],
        out_shape=[jax.ShapeDtypeStruct((t, d), F32), jax.ShapeDtypeStruct((t, d), BF16),
                   jax.ShapeDtypeStruct((2, nj, t, fb), BF16), jax.ShapeDtypeStruct((nj, t, fb), BF16)],
        scratch_shapes=[pltpu.VMEM((tm, d), F32)],
        compiler_params=_cp("parallel", "arbitrary"), name="ffn_fwd")(x, nw, w_in, w_out)


def ffn_bwd(x, dy, nw, gu, w_in, w_out, tm=512, rc=256):
    t, d = x.shape
    nj, fb = w_out.shape[0], w_out.shape[1]
    tm = min(tm, t)
    rc = min(rc, tm)

    def body(x_ref, dy_ref, nw_ref, gu_ref, wi_ref, wo_ref,
             dx_ref, dnw_ref, dgu_ref, dyh_ref, acc_ref):
        i, j = pl.program_id(0), pl.program_id(1)

        @pl.when(j == 0)
        def _():
            dyh_ref[...] = (0.5 * dy_ref[...]).astype(BF16)
            acc_ref[...] = jnp.zeros_like(acc_ref)

        @pl.when((i == 0) & (j == 0))
        def _():
            dnw_ref[...] = jnp.zeros_like(dnw_ref)

        rows = [slice(c * rc, (c + 1) * rc) for c in range(tm // rc)]
        dhs = [_dot_nt(dyh_ref[r, :], wo_ref[...]) for r in rows]
        dgs = [(dh * gu_ref[0, r, :].astype(F32)).astype(BF16) for dh, r in zip(dhs, rows)]
        dus = [(dh * gu_ref[1, r, :].astype(F32)).astype(BF16) for dh, r in zip(dhs, rows)]
        for dg, du, r in zip(dgs, dus, rows):
            dgu_ref[0, r, :] = dg
            dgu_ref[1, r, :] = du
        for dg, du, r in zip(dgs, dus, rows):
            acc_ref[r, :] += _dot_nt(dg, wi_ref[0]) + _dot_nt(du, wi_ref[1])

        @pl.when(j == nj - 1)
        def _():
            xf = x_ref[...]
            r = _rstd(xf)
            dxn = acc_ref[...]
            dnw_ref[...] += jnp.sum(dxn * xf * r, axis=0, keepdims=True)
            dx_ref[...] = _rms_bwd(xf, r, dxn * nw_ref[...]) + dy_ref[...]

    return pl.pallas_call(
        body, grid=(t // tm, nj),
        in_specs=[pl.BlockSpec((tm, d), lambda i, j: (i, 0)),
                  pl.BlockSpec((tm, d), lambda i, j: (i, 0)),
                  pl.BlockSpec((1, d), lambda i, j: (0, 0)),
                  pl.BlockSpec((2, None, tm, fb), lambda i, j: (0, j, i, 0)),
                  pl.BlockSpec((2, None, d, fb), lambda i, j: (0, j, 0, 0)),
                  pl.BlockSpec((None, fb, d), lambda i, j: (j, 0, 0))],
        out_specs=[pl.BlockSpec((tm, d), lambda i, j: (i, 0)),
                   pl.BlockSpec((1, d), lambda i, j: (0, 0)),
                   pl.BlockSpec((2, None, tm, fb), lambda i, j: (0, j, i, 0)),
                   pl.BlockSpec((tm, d), lambda i, j: (i, 0))],
        out_shape=[jax.ShapeDtypeStruct((t, d), F32),
                   jax.ShapeDtypeStruct((1, d), F32),
                   jax.ShapeDtypeStruct((2, nj, t, fb), BF16),
                   jax.ShapeDtypeStruct((t, d), BF16)],
        scratch_shapes=[pltpu.VMEM((tm, d), F32)],
        compiler_params=_cp("arbitrary", "arbitrary"), name="ffn_bwd")(x, dy, nw, gu, w_in, w_out)


def _wgrad_call(a, b, a_spec, b_spec, out_shape, out_spec, grid, name, out_dtype=BF16):
    last = len(grid) - 1
    acc_shape = tuple(s for s in out_spec.block_shape if s is not None)

    def body(a_ref, b_ref, o_ref, acc_ref):
        @pl.when(pl.program_id(last) == 0)
        def _():
            acc_ref[...] = jnp.zeros_like(acc_ref)

        if len(acc_shape) == 3:
            a = a_ref[...]
            for s in range(acc_shape[0]):
                acc_ref[s] += _dot_tn(a, b_ref[s])
        else:
            acc_ref[...] += _dot_tn(a_ref[...], b_ref[...])

        @pl.when(pl.program_id(last) == grid[last] - 1)
        def _():
            o_ref[...] = acc_ref[...].astype(o_ref.dtype)

    sem = ("parallel",) * last + ("arbitrary",)
    return pl.pallas_call(body, grid=grid, in_specs=[a_spec, b_spec], out_specs=out_spec,
                          out_shape=jax.ShapeDtypeStruct(out_shape, out_dtype),
                          scratch_shapes=[pltpu.VMEM(acc_shape, F32)],
                          compiler_params=_cp(*sem), name=name)(a, b)


WGRAD_TM = 1024


def wgrad_ffn_in(xn, dgu, tm=WGRAD_TM):
    t, d = xn.shape
    _, nj, _, fb = dgu.shape
    tm = min(tm, t)
    return _wgrad_call(xn, dgu,
                       pl.BlockSpec((tm, d), lambda j, k: (k, 0)),
                       pl.BlockSpec((2, None, tm, fb), lambda j, k: (0, j, k, 0)),
                       (2, nj, d, fb), pl.BlockSpec((2, None, d, fb), lambda j, k: (0, j, 0, 0)),
                       (nj, t // tm), "wgrad_ffn_in")


def wgrad_ffn_out(h, dyh, tm=WGRAD_TM):
    nj, t, fb = h.shape
    d = dyh.shape[1]
    tm = min(tm, t)
    return _wgrad_call(h, dyh,
                       pl.BlockSpec((None, tm, fb), lambda j, k: (j, k, 0)),
                       pl.BlockSpec((tm, d), lambda j, k: (k, 0)),
                       (nj, fb, d), pl.BlockSpec((None, fb, d), lambda j, k: (j, 0, 0)),
                       (nj, t // tm), "wgrad_ffn_out")


def wgrad_2d(a, b, tk, name, out_dtype=BF16, tm=WGRAD_TM):
    t, k = a.shape
    n = b.shape[1]
    tm = min(tm, t)
    return _wgrad_call(a, b,
                       pl.BlockSpec((tm, tk), lambda c, s: (s, c)),
                       pl.BlockSpec((tm, n), lambda c, s: (s, 0)),
                       (k, n), pl.BlockSpec((tk, n), lambda c, s: (c, 0)),
                       (k // tk, t // tm), name, out_dtype)


N_BIG = 7 * 512
N_PROJ = N_BIG + 128
COL_SMALL = N_BIG // 128


def inproj_fwd(x, nw, w, tm=512):
    t, d = x.shape
    n = w.shape[1]

    def body(x_ref, nw_ref, w_ref, p_ref, hn_ref):
        xf = x_ref[...]
        hn = (xf * _rstd(xf) * nw_ref[...]).astype(BF16)
        hn_ref[...] = hn
        p_ref[...] = _dot(hn, w_ref[...])

    return pl.pallas_call(
        body, grid=(t // tm,),
        in_specs=[pl.BlockSpec((tm, d), lambda i: (i, 0)), pl.BlockSpec((1, d), lambda i: (0, 0)),
                  pl.BlockSpec((d, n), lambda i: (0, 0))],
        out_specs=[pl.BlockSpec((tm, n), lambda i: (i, 0)), pl.BlockSpec((tm, d), lambda i: (i, 0))],
        out_shape=[jax.ShapeDtypeStruct((t, n), F32), jax.ShapeDtypeStruct((t, d), BF16)],
        compiler_params=_cp("parallel"), name="inproj_fwd")(x, nw, w)


def inproj_bwd(x, dres, nw, w, dparts, tm=512):
    t, d = x.shape
    n = w.shape[1]
    widths = [p.shape[1] for p in dparts]
    assert sum(widths) == n

    def body(x_ref, dres_ref, nw_ref, w_ref, *rest):
        part_refs, (dx_ref, dnw_ref, dp_ref) = rest[:len(widths)], rest[len(widths):]

        @pl.when(pl.program_id(0) == 0)
        def _():
            dnw_ref[...] = jnp.zeros_like(dnw_ref)

        dp = jnp.concatenate([r[...].astype(BF16) for r in part_refs], axis=1)
        dp_ref[...] = dp
        dhn = _dot_nt(dp, w_ref[...])
        xf = x_ref[...]
        r = _rstd(xf)
        dnw_ref[...] += jnp.sum(dhn * xf * r, axis=0, keepdims=True)
        dx_ref[...] = _rms_bwd(xf, r, dhn * nw_ref[...]) + dres_ref[...]

    return pl.pallas_call(
        body, grid=(t // tm,),
        in_specs=[pl.BlockSpec((tm, d), lambda i: (i, 0)), pl.BlockSpec((tm, d), lambda i: (i, 0)),
                  pl.BlockSpec((1, d), lambda i: (0, 0)), pl.BlockSpec((d, n), lambda i: (0, 0))]
                 + [pl.BlockSpec((tm, wd), lambda i: (i, 0)) for wd in widths],
        out_specs=[pl.BlockSpec((tm, d), lambda i: (i, 0)), pl.BlockSpec((1, d), lambda i: (0, 0)),
                   pl.BlockSpec((tm, n), lambda i: (i, 0))],
        out_shape=[jax.ShapeDtypeStruct((t, d), F32), jax.ShapeDtypeStruct((1, d), F32),
                   jax.ShapeDtypeStruct((t, n), BF16)],
        compiler_params=_cp("arbitrary"), name="inproj_bwd")(x, dres, nw, w, *dparts)


def outproj_fwd(x, yf, yg, w, tm=512):
    t, d = x.shape
    hw = yf.shape[1]

    def body(x_ref, yf_ref, yg_ref, w_ref, o_ref, y_ref):
        y = jnp.concatenate([yf_ref[...], yg_ref[...]], axis=1).astype(BF16)
        y_ref[...] = y
        o_ref[...] = x_ref[...] + _dot(y, w_ref[...])

    return pl.pallas_call(
        body, grid=(t // tm,),
        in_specs=[pl.BlockSpec((tm, d), lambda i: (i, 0)), pl.BlockSpec((tm, hw), lambda i: (i, 0)),
                  pl.BlockSpec((tm, hw), lambda i: (i, 0)), pl.BlockSpec((2 * hw, d), lambda i: (0, 0))],
        out_specs=[pl.BlockSpec((tm, d), lambda i: (i, 0)), pl.BlockSpec((tm, 2 * hw), lambda i: (i, 0))],
        out_shape=[jax.ShapeDtypeStruct((t, d), F32), jax.ShapeDtypeStruct((t, 2 * hw), BF16)],
        compiler_params=_cp("parallel"), name="outproj_fwd")(x, yf, yg, w)


def outproj_bwd(dy, w, after=None, tm=512):
    t, d = dy.shape
    hw = w.shape[0] // 2
    extra = [] if after is None else [after]

    def body(dy_ref, w_ref, *rest):
        df_ref, dg_ref, dyb_ref = rest[-3:]
        dyb = dy_ref[...].astype(BF16)
        dyb_ref[...] = dyb
        dyy = _dot_nt(dyb, w_ref[...])
        df_ref[...] = dyy[:, :hw]
        dg_ref[...] = dyy[:, hw:]

    return pl.pallas_call(
        body, grid=(t // tm,),
        in_specs=[pl.BlockSpec((tm, d), lambda i: (i, 0)), pl.BlockSpec((2 * hw, d), lambda i: (0, 0))]
                 + [pl.BlockSpec(memory_space=pl.ANY)] * len(extra),
        out_specs=[pl.BlockSpec((tm, hw), lambda i: (i, 0)), pl.BlockSpec((tm, hw), lambda i: (i, 0)),
                   pl.BlockSpec((tm, d), lambda i: (i, 0))],
        out_shape=[jax.ShapeDtypeStruct((t, hw), F32), jax.ShapeDtypeStruct((t, hw), F32),
                   jax.ShapeDtypeStruct((t, d), BF16)],
        compiler_params=_cp("parallel"), name="outproj_bwd")(dy, w, *extra)


def _lane(shape):
    return lax.broadcasted_iota(jnp.int32, shape, 1)


def _row(shape):
    return lax.broadcasted_iota(jnp.int32, shape, 0)


def _gate_terms(val, gp_ref):
    z = val + gp_ref[0:1, :]
    sp = jnp.log(1.0 + jnp.exp(-jnp.abs(z)))
    return z, sp


def gates_fwd(proj, gp, seq, ts=512):
    t = proj.shape[0]
    nb, ns = t // seq, seq // ts

    def body(v_ref, gp_ref, o_ref, carry_ref):
        @pl.when(pl.program_id(1) == 0)
        def _():
            carry_ref[...] = jnp.zeros_like(carry_ref)

        z, sp = _gate_terms(v_ref[...], gp_ref)
        logsig = jnp.minimum(z, 0.0) - sp
        tri = (_row((ts, ts)) >= _lane((ts, ts))).astype(F32)
        cum = jnp.dot(tri, logsig, precision=HIGHEST, preferred_element_type=F32) + carry_ref[0:1, :]
        carry_ref[0:1, :] = cum[ts - 1:ts, :]
        g = -jnp.exp(gp_ref[1:2, :]) * (jnp.maximum(z, 0.0) + sp)
        beta = jax.nn.sigmoid(z)
        lane = _lane((ts, 128))
        o_ref[...] = jnp.where(lane < 8, cum, jnp.where(lane < 12, g, jnp.where(lane < 16, beta, 0.0)))

    return pl.pallas_call(
        body, grid=(nb, ns),
        in_specs=[pl.BlockSpec((ts, 128), lambda b, s: (b * ns + s, COL_SMALL)),
                  pl.BlockSpec((8, 128), lambda b, s: (0, 0))],
        out_specs=pl.BlockSpec((ts, 128), lambda b, s: (b * ns + s, 0)),
        out_shape=jax.ShapeDtypeStruct((t, 128), F32),
        scratch_shapes=[pltpu.VMEM((8, 128), F32)],
        compiler_params=_cp("parallel", "arbitrary"), name="gates_fwd")(proj, gp)


def gates_bwd(proj, gp, dga, dgb, seq, ts=512):
    t = proj.shape[0]
    nb, ns = t // seq, seq // ts

    def body(v_ref, gp_ref, da_ref, db_ref, ds_ref, dgp_ref, carry_ref):
        @pl.when(pl.program_id(1) == 0)
        def _():
            carry_ref[...] = jnp.zeros_like(carry_ref)

        @pl.when((pl.program_id(0) == 0) & (pl.program_id(1) == 0))
        def _():
            dgp_ref[...] = jnp.zeros_like(dgp_ref)

        lane = _lane((ts, 128))
        dgate = jnp.where(lane < 8, da_ref[...], jnp.where(lane < 16, db_ref[...], 0.0))
        z, sp = _gate_terms(v_ref[...], gp_ref)
        triu = (_row((ts, ts)) <= _lane((ts, ts))).astype(F32)
        dlog = jnp.dot(triu, dgate, precision=HIGHEST, preferred_element_type=F32) + carry_ref[0:1, :]
        carry_ref[0:1, :] = dlog[0:1, :]
        sig = jax.nn.sigmoid(z)
        nea = -jnp.exp(gp_ref[1:2, :])
        g = nea * (jnp.maximum(z, 0.0) + sp)
        dz = jnp.where(lane < 8, dlog * (1.0 - sig),
                       jnp.where(lane < 12, dgate * nea * sig, dgate * sig * (1.0 - sig)))
        dz = jnp.where(lane < 16, dz, 0.0)
        ds_ref[...] = dz.astype(BF16)
        dgp_ref[0:1, :] += jnp.where(lane[0:1] < 12, jnp.sum(dz, axis=0, keepdims=True), 0.0)
        dgp_ref[1:2, :] += jnp.where((lane[0:1] >= 8) & (lane[0:1] < 12), jnp.sum(dgate * g, axis=0, keepdims=True), 0.0)

    rev = lambda b, s: (b * ns + (ns - 1 - s), 0)
    return pl.pallas_call(
        body, grid=(nb, ns),
        in_specs=[pl.BlockSpec((ts, 128), lambda b, s: (b * ns + (ns - 1 - s), COL_SMALL)),
                  pl.BlockSpec((8, 128), lambda b, s: (0, 0)),
                  pl.BlockSpec((ts, 128), rev), pl.BlockSpec((ts, 128), rev)],
        out_specs=[pl.BlockSpec((ts, 128), rev), pl.BlockSpec((8, 128), lambda b, s: (0, 0))],
        out_shape=[jax.ShapeDtypeStruct((t, 128), BF16), jax.ShapeDtypeStruct((8, 128), F32)],
        scratch_shapes=[pltpu.VMEM((8, 128), F32)],
        compiler_params=_cp("arbitrary", "arbitrary"), name="gates_bwd")(proj, gp, dga, dgb)


NEG = -1e30
ATTN_TQ_FWD = 1024
ATTN_TQ_BWD = 512


def _pick_lane(tile, idx):
    return jnp.sum(jnp.where(_lane(tile.shape) == idx, tile, 0.0), axis=1, keepdims=True)


def _col_to_row(col, n):
    return jnp.sum(jnp.where(_row((n, n)) == _lane((n, n)), col, 0.0), axis=0, keepdims=True)


def _row_to_col(row, n):
    return jnp.sum(jnp.where(_row((n, n)) == _lane((n, n)), row, 0.0), axis=1, keepdims=True)


def _rows(i, n):
    return pl.ds(pl.multiple_of(i * n, n), n)


LOG2E = 1.4426950408889634
LN2 = 0.6931471805599453


def _rowsum(z, width, passes=2):
    ones = jnp.ones((z.shape[1], width), BF16)
    total, rest = None, z
    for _ in range(passes):
        part = rest.astype(BF16)
        rest = rest - part.astype(F32)
        total = _dot(part, ones) if total is None else total + _dot(part, ones)
    return total


def _split_dot(x, mat, passes):
    total, rest = None, x
    for _ in range(passes):
        part = rest.astype(BF16)
        rest = rest - part.astype(F32)
        total = _dot(part, mat) if total is None else total + _dot(part, mat)
    return total


def _pair_mats(p, dh):
    r, l = _row((128, 128)), _lane((128, 128))
    same = (r < dh) == (l < dh)
    upper = (l >= dh).astype(jnp.int32)
    as_bf16 = lambda m: m.astype(BF16)
    return dict(own=as_bf16(same), other=as_bf16(jnp.logical_not(same)), pick_other=as_bf16(r == 2 * p + 1 - upper),
                swap=as_bf16(((r == 0) & (l >= dh)) | ((r == dh) & (l < dh))))


def _pair_rstd(x2, sel, dh):
    return lax.rsqrt(_split_dot(x2 * x2, sel["own"], 2) * (1.0 / dh) + EPS)


def _pair_aug(cols, n, dh):
    lane = _lane((n, 128))
    li = jnp.where(lane >= dh, lane - dh, lane)
    out = jnp.zeros((n, 128), F32)
    for i, c in enumerate(cols):
        out = jnp.where(li == i, c, out)
    return out


def _rstd_mxu(xf):
    return lax.rsqrt(_rowsum(xf * xf, xf.shape[1]) * (1.0 / xf.shape[1]) + EPS)


def _rms_bwd_mxu(xf, r, dyn):
    return r * dyn - xf * (r * r * r) * (_rowsum(dyn * xf, xf.shape[1]) * (1.0 / xf.shape[1]))


def _pick_lane_mxu(tile, idx, width):
    onehot = (_row((tile.shape[1], width)) == idx).astype(BF16)
    total, rest = None, tile
    for _ in range(3):
        part = rest.astype(BF16)
        rest = rest - part.astype(F32)
        total = _dot(part, onehot) if total is None else total + _dot(part, onehot)
    return total


def _split3(x):
    hi = x.astype(BF16).astype(F32)
    mid = (x - hi).astype(BF16).astype(F32)
    return [hi, mid, (x - hi - mid).astype(BF16).astype(F32)]


def _aug_cols(cols, n, width):
    lane = _lane((n, width))
    out = jnp.zeros((n, width), F32)
    for i, c in enumerate(cols):
        out = jnp.where(lane == i, c, out)
    return out


def _once(shape, index_map):
    return pl.BlockSpec(shape, index_map, pipeline_mode=pl.Buffered(1))


def attn_fwd(proj, gates, qw, kw, seq, tq=256):
    t = proj.shape[0]
    nb, nq, dh = t // seq, seq // tq, FOX_DH
    scale = dh ** -0.5

    def body(q_ref, k_ref, v_ref, g_ref, qw_ref, kw_ref, y_ref, lse_ref, qs, ks, vs):
        p = pl.program_id(1)
        heads = range(2)
        low = _lane((tq, 128)) < dh
        sel = _pair_mats(p, dh)

        def prep(i, _):
            r = _rows(i, tq)
            q2, k2 = q_ref[r, :], k_ref[r, :]
            cc = _split_dot(g_ref[r, :], sel["pick_other"], 3) * LOG2E
            qn = q2 * _pair_rstd(q2, sel, dh) * qw_ref[...] * (scale * LOG2E)
            kn = k2 * _pair_rstd(k2, sel, dh) * kw_ref[...]
            qx = _pair_aug(_split3(cc) + [1.0, 1.0, 1.0], tq, dh)
            kx = _pair_aug([1.0, 1.0, 1.0] + _split3(-cc), tq, dh)
            for hh in heads:
                own = low if hh == 0 else jnp.logical_not(low)
                qs[hh, r, :] = jnp.where(own, qn, qx).astype(BF16)
                ks[hh, r, :] = jnp.where(own, kn, kx).astype(BF16)
            vs[r, :] = v_ref[r, :].astype(BF16)
            return 0

        lax.fori_loop(0, nq, prep, 0)

        def q_tile(i, _):
            r = _rows(i, tq)
            qt = [qs[hh, r, :] for hh in heads]

            def kv_step(j, carry, masked):
                kr = _rows(j, tq)
                vt = vs[kr, :]
                out = []
                for hh in heads:
                    m, l, acc = carry[hh]
                    s = _dot_nt(qt[hh], ks[hh, kr, :])
                    if masked:
                        s = jnp.where(_row((tq, tq)) >= _lane((tq, tq)), s, NEG)
                    m_new = jnp.maximum(m, jnp.max(s, axis=1, keepdims=True))
                    pe = jnp.exp2(s - m_new)
                    a = jnp.exp2(m - m_new)
                    out.append((m_new, a * l + jnp.sum(pe, axis=1, keepdims=True), a * acc + _dot(pe.astype(BF16), vt)))
                return tuple(out)

            one = (jnp.full((tq, 1), NEG, F32), jnp.zeros((tq, 1), F32), jnp.zeros((tq, 128), F32))
            carry = lax.fori_loop(0, i, lambda j, c: kv_step(j, c, False), (one, one))
            (m0, l0, acc0), (m1, l1, acc1) = kv_step(i, carry, True)
            y_ref[r, :] = jnp.where(low, acc0 / l0, acc1 / l1)
            lse_ref[r, :] = jnp.where(low, m0 + jnp.log2(l0), m1 + jnp.log2(l1))
            return 0

        lax.fori_loop(0, nq, q_tile, 0)

    blk = lambda off: _once((seq, 128), lambda b, p: (b, off + p))
    return pl.pallas_call(
        body, grid=(nb, 4),
        in_specs=[blk(0), blk(4), blk(8), _once((seq, 128), lambda b, p: (b, 0)),
                  pl.BlockSpec((1, 128), lambda b, p: (0, 0)), pl.BlockSpec((1, 128), lambda b, p: (0, 0))],
        out_specs=[pl.BlockSpec((seq, 128), lambda b, p: (b, p)), pl.BlockSpec((seq, 128), lambda b, p: (b, p))],
        out_shape=[jax.ShapeDtypeStruct((t, 512), F32), jax.ShapeDtypeStruct((t, 512), F32)],
        scratch_shapes=[pltpu.VMEM((2, seq, 128), BF16), pltpu.VMEM((2, seq, 128), BF16), pltpu.VMEM((seq, 128), BF16)],
        compiler_params=_cp("parallel", "arbitrary"),
        name="attn_fwd")(proj, proj, proj, gates, jnp.tile(qw, (1, 2)), jnp.tile(kw, (1, 2)))


def attn_bwd(proj, gates, qw, kw, y, lse, dy, seq, tq=256):
    t = proj.shape[0]
    nb, nq, dh = t // seq, seq // tq, FOX_DH
    scale = dh ** -0.5

    def body(q_ref, k_ref, v_ref, g_ref, qw_ref, kw_ref, y_ref, lse_ref, dy_ref,
             dq_ref, dk_ref, dv_ref, dg_ref, dqw_ref, dkw_ref,
             qs, ks, vs, dos, dsrow, dqa, dka):
        b, p = pl.program_id(0), pl.program_id(1)

        @pl.when((b == 0) & (p == 0))
        def _():
            dqw_ref[...] = jnp.zeros_like(dqw_ref)
            dkw_ref[...] = jnp.zeros_like(dkw_ref)

        @pl.when(p == 0)
        def _():
            dg_ref[...] = jnp.zeros_like(dg_ref)

        heads = range(2)
        low = _lane((tq, 128)) < dh
        sel = _pair_mats(p, dh)

        def prep(i, _):
            r = _rows(i, tq)
            q2, k2, dy2 = q_ref[r, :], k_ref[r, :], dy_ref[r, :]
            cc = _split_dot(g_ref[r, :], sel["pick_other"], 3) * LOG2E
            lse_x = _split_dot(lse_ref[r, :], sel["swap"], 3)
            delta_x = _split_dot(dy2 * y_ref[r, :], sel["other"], 2)
            qn = q2 * _pair_rstd(q2, sel, dh) * qw_ref[...] * (scale * LOG2E)
            kn = k2 * _pair_rstd(k2, sel, dh) * kw_ref[...]
            qx = _pair_aug(_split3(cc) + [1.0, 1.0, 1.0] + _split3(-lse_x), tq, dh)
            kx = _pair_aug([1.0, 1.0, 1.0] + _split3(-cc) + [1.0, 1.0, 1.0], tq, dh)
            vx = _pair_aug([1.0, 1.0, 1.0], tq, dh)
            dx = _pair_aug(_split3(-delta_x), tq, dh)
            for hh in heads:
                own = low if hh == 0 else jnp.logical_not(low)
                qs[hh, r, :] = jnp.where(own, qn, qx).astype(BF16)
                ks[hh, r, :] = jnp.where(own, kn, kx).astype(BF16)
                vs[hh, r, :] = jnp.where(own, v_ref[r, :], vx).astype(BF16)
                dos[hh, r, :] = jnp.where(own, dy2, dx).astype(BF16)
                dsrow[hh, r, :] = jnp.zeros((tq, 1), F32)
                dqa[hh, r, :] = jnp.zeros((tq, 128), F32)
            return 0

        lax.fori_loop(0, nq, prep, 0)

        def kv_tile(j, _):
            kr = _rows(j, tq)
            kt = [ks[hh, kr, :] for hh in heads]
            vt = [vs[hh, kr, :] for hh in heads]

            def q_step(i, carry, masked):
                r = _rows(i, tq)
                out = []
                for hh in heads:
                    dk, dv, dcr = carry[hh]
                    qt, dot = qs[hh, r, :], dos[hh, r, :]
                    s = _dot_nt(qt, kt[hh])
                    if masked:
                        s = jnp.where(_row((tq, tq)) >= _lane((tq, tq)), s, NEG)
                    pe = jnp.exp2(s)
                    ds = pe * _dot_nt(dot, vt[hh])
                    dsb = ds.astype(BF16)
                    dqa[hh, r, :] += _dot(dsb, kt[hh])
                    dsrow[hh, r, :] += jnp.sum(ds, axis=1, keepdims=True)
                    out.append((dk + _dot_tn(dsb, qt), dv + _dot_tn(pe.astype(BF16), dot),
                                dcr - jnp.sum(ds, axis=0, keepdims=True)))
                return tuple(out)

            one = (jnp.zeros((tq, 128), F32), jnp.zeros((tq, 128), F32), jnp.zeros((1, tq), F32))
            carry = q_step(j, (one, one), True)
            (dk0, dv0, dcr0), (dk1, dv1, dcr1) = lax.fori_loop(j + 1, nq, lambda i, c: q_step(i, c, False), carry)
            dka[kr, :] = jnp.where(low, dk0, dk1)
            dv_ref[kr, :] = jnp.where(low, dv0, dv1).astype(BF16)
            lane = _lane((tq, 128))
            dg_ref[kr, :] = jnp.where(lane == 2 * p, _row_to_col(dcr0, tq),
                                      jnp.where(lane == 2 * p + 1, _row_to_col(dcr1, tq), dg_ref[kr, :]))
            return 0

        lax.fori_loop(0, nq, kv_tile, 0)

        def post(i, _):
            r = _rows(i, tq)
            q2, k2 = q_ref[r, :], k_ref[r, :]
            rq, rk = _pair_rstd(q2, sel, dh), _pair_rstd(k2, sel, dh)
            dqn = jnp.where(low, dqa[0, r, :], dqa[1, r, :]) * scale
            dkn = dka[r, :] * LN2
            dqw_ref[...] += jnp.sum(dqn * q2 * rq, axis=0, keepdims=True)
            dkw_ref[...] += jnp.sum(dkn * k2 * rk, axis=0, keepdims=True)
            for x2, rr, dyn, o_ref in ((q2, rq, dqn * qw_ref[...], dq_ref), (k2, rk, dkn * kw_ref[...], dk_ref)):
                mean = _split_dot(dyn * x2, sel["own"], 2) * (1.0 / dh)
                o_ref[r, :] = (rr * dyn - x2 * (rr * rr * rr) * mean).astype(BF16)
            lane = _lane((tq, 128))
            dg_ref[r, :] += jnp.where(lane == 2 * p, dsrow[0, r, :], jnp.where(lane == 2 * p + 1, dsrow[1, r, :], 0.0))
            return 0

        lax.fori_loop(0, nq, post, 0)

    blk = lambda off: _once((seq, 128), lambda b, p: (b, off + p))
    own = lambda: _once((seq, 128), lambda b, p: (b, p))
    vec = lambda: pl.BlockSpec((1, 128), lambda b, p: (0, 0))
    res = pl.pallas_call(
        body, grid=(nb, 4),
        in_specs=[blk(0), blk(4), blk(8), _once((seq, 128), lambda b, p: (b, 0)), vec(), vec(), own(), own(), own()],
        out_specs=[own(), own(), own(), _once((seq, 128), lambda b, p: (b, 0)), vec(), vec()],
        out_shape=[jax.ShapeDtypeStruct((t, 512), BF16)] * 3
                  + [jax.ShapeDtypeStruct((t, 128), F32), jax.ShapeDtypeStruct((1, 128), F32), jax.ShapeDtypeStruct((1, 128), F32)],
        scratch_shapes=[pltpu.VMEM((2, seq, 128), BF16)] * 4
                       + [pltpu.VMEM((2, seq, 1), F32), pltpu.VMEM((2, seq, 128), F32), pltpu.VMEM((seq, 128), F32)],
        compiler_params=_cp("arbitrary", "arbitrary"),
        name="attn_bwd")(proj, proj, proj, gates, jnp.tile(qw, (1, 2)), jnp.tile(kw, (1, 2)), y, lse, dy)
    return list(res[:4]) + [res[4][:, :dh] + res[4][:, dh:], res[5][:, :dh] + res[5][:, dh:]]


def _silu_grad(c, sg):
    return sg * (1.0 + c * (1.0 - sg))


def _conv(x, w, n):
    row = _row(x.shape)
    c = x * w[CONV_W - 1:CONV_W, :]
    for k in range(CONV_W - 1):
        sh = CONV_W - 1 - k
        c = c + w[k:k + 1, :] * jnp.where(row >= sh, pltpu.roll(x, sh, 0), 0.0)
    return c


def gdn_pre_fwd(proj, cw, seq):
    t = proj.shape[0]
    nb = t // seq
    scale = GDN_DH ** -0.5

    def body(xq_ref, xk_ref, xv_ref, wq_ref, wk_ref, wv_ref, q_ref, k_ref, v_ref):
        def act(x_ref, w_ref):
            c = _conv(x_ref[...], w_ref[...], seq)
            return c * jax.nn.sigmoid(c)

        aq, ak = act(xq_ref, wq_ref), act(xk_ref, wk_ref)
        q_ref[...] = aq * lax.rsqrt(jnp.sum(aq * aq, axis=1, keepdims=True) + EPS) * scale
        k_ref[...] = ak * lax.rsqrt(jnp.sum(ak * ak, axis=1, keepdims=True) + EPS)
        v_ref[...] = act(xv_ref, wv_ref)

    xb = lambda off: pl.BlockSpec((seq, 128), lambda b, h: (b, off + h))
    wb = lambda off: pl.BlockSpec((CONV_W, 128), lambda b, h: (0, off + h))
    ob = lambda: pl.BlockSpec((seq, 128), lambda b, h: (b, h))
    return pl.pallas_call(
        body, grid=(nb, GDN_HEADS),
        in_specs=[xb(12), xb(16), xb(20), wb(0), wb(4), wb(8)],
        out_specs=[ob(), ob(), ob()],
        out_shape=[jax.ShapeDtypeStruct((t, 512), F32)] * 3,
        compiler_params=_cp("parallel", "parallel"), name="gdn_pre_fwd")(proj, proj, proj, cw, cw, cw)


def gdn_pre_bwd(proj, cw, dq, dk, dv, seq):
    t = proj.shape[0]
    nb = t // seq
    scale = GDN_DH ** -0.5

    def body(xq_ref, xk_ref, xv_ref, wq_ref, wk_ref, wv_ref, dq_ref, dk_ref, dv_ref,
             dxq_ref, dxk_ref, dxv_ref, dwq_ref, dwk_ref, dwv_ref):
        first = pl.program_id(1) == 0
        row = _row((seq, 128))

        def one(x_ref, w_ref, dy_ref, dx_ref, dw_ref, norm, sc):
            x, w = x_ref[...], w_ref[...]
            c = _conv(x, w, seq)
            sg = jax.nn.sigmoid(c)
            dy = dy_ref[...]
            if norm:
                a = c * sg
                rs = lax.rsqrt(jnp.sum(a * a, axis=1, keepdims=True) + EPS)
                dy = dy * sc
                da = rs * dy - a * (rs * rs * rs) * jnp.sum(dy * a, axis=1, keepdims=True)
            else:
                da = dy
            dc = da * _silu_grad(c, sg)
            dx = dc * w[CONV_W - 1:CONV_W, :]
            dws = [None] * CONV_W
            dws[CONV_W - 1] = jnp.sum(dc * x, axis=0, keepdims=True)
            for k in range(CONV_W - 1):
                sh = CONV_W - 1 - k
                dx = dx + w[k:k + 1, :] * jnp.where(row < seq - sh, pltpu.roll(dc, seq - sh, 0), 0.0)
                dws[k] = jnp.sum(dc * jnp.where(row >= sh, pltpu.roll(x, sh, 0), 0.0), axis=0, keepdims=True)
            dx_ref[...] = dx.astype(BF16)
            dwn = jnp.concatenate(dws, axis=0)

            @pl.when(first)
            def _():
                dw_ref[...] = dwn

            @pl.when(jnp.logical_not(first))
            def _():
                dw_ref[...] += dwn

        one(xq_ref, wq_ref, dq_ref, dxq_ref, dwq_ref, True, scale)
        one(xk_ref, wk_ref, dk_ref, dxk_ref, dwk_ref, True, 1.0)
        one(xv_ref, wv_ref, dv_ref, dxv_ref, dwv_ref, False, 1.0)

    xb = lambda off: pl.BlockSpec((seq, 128), lambda h, b: (b, off + h))
    wb = lambda off: pl.BlockSpec((CONV_W, 128), lambda h, b: (0, off + h))
    ob = lambda: pl.BlockSpec((seq, 128), lambda h, b: (b, h))
    return pl.pallas_call(
        body, grid=(GDN_HEADS, nb),
        in_specs=[xb(12), xb(16), xb(20), wb(0), wb(4), wb(8), ob(), ob(), ob()],
        out_specs=[ob(), ob(), ob()] + [pl.BlockSpec((CONV_W, 128), lambda h, b: (0, h))] * 3,
        out_shape=[jax.ShapeDtypeStruct((t, 512), BF16)] * 3 + [jax.ShapeDtypeStruct((CONV_W, 512), F32)] * 3,
        compiler_params=_cp("parallel", "arbitrary"), name="gdn_pre_bwd")(proj, proj, proj, cw, cw, cw, dq, dk, dv)


def _b16(x):
    return x.astype(BF16)


@jax.custom_vjp
def _mm(a, b):
    return _dot(_b16(a), _b16(b))


_mm.defvjp(lambda a, b: (_mm(a, b), (a, b)),
           lambda res, g: (_dot_nt(_b16(g), _b16(res[1])), _dot_tn(_b16(res[0]), _b16(g))))


@jax.custom_vjp
def _mm_nt(a, b):
    return _dot_nt(_b16(a), _b16(b))


_mm_nt.defvjp(lambda a, b: (_mm_nt(a, b), (a, b)),
              lambda res, g: (_dot(_b16(g), _b16(res[1])), _dot_tn(_b16(g), _b16(res[0]))))


@jax.custom_vjp
def _mm_tn(a, b):
    return _dot_tn(_b16(a), _b16(b))


_mm_tn.defvjp(lambda a, b: (_mm_tn(a, b), (a, b)),
              lambda res, g: (_dot_nt(_b16(res[1]), _b16(g)), _dot(_b16(res[0]), _b16(g))))


def _dot32(a, b, dims=(((1,), (0,)), ((), ()))):
    def split(x):
        hi = x.astype(BF16)
        return hi, (x - hi.astype(F32)).astype(BF16)

    (ah, al), (bh, bl) = split(a), split(b)
    d = lambda x, y: lax.dot_general(x, y, dims, preferred_element_type=F32)
    return d(ah, bh) + (d(ah, bl) + d(al, bh))


def _inv_fwd_many(mats):
    n = mats[0].shape[0]
    eye = (_row((n, n)) == _lane((n, n))).astype(F32)
    invs, pws = [eye - a for a in mats], list(mats)
    for _ in range(n.bit_length() - 2):
        pws = [_dot32(pw, pw) for pw in pws]
        invs = [inv + _dot32(inv, pw) for inv, pw in zip(invs, pws)]
    return invs


@jax.custom_vjp
def _inv_saved(a, inv):
    return inv


def _inv_saved_bwd(inv, g):
    tg = _dot32(inv, g, (((0,), (0,)), ((), ())))
    return -_dot32(tg, inv, (((1,), (1,)), ((), ()))), jnp.zeros_like(inv)


_inv_saved.defvjp(lambda a, inv: (inv, inv), _inv_saved_bwd)


def _gdn_decay(gcol):
    c = CHUNK
    ri, ci = _row((c, c)), _lane((c, c))
    incl, eye = ri >= ci, ri == ci
    grow = jnp.sum(jnp.where(eye, gcol, 0.0), axis=0, keepdims=True)
    gc = jnp.sum(jnp.where(incl, grow, 0.0), axis=1, keepdims=True)
    gcr = jnp.sum(jnp.where(eye, gc, 0.0), axis=0, keepdims=True)
    gl = jnp.sum(jnp.where(_row((c, 1)) == c - 1, gc, 0.0), axis=0, keepdims=True)
    return gc, gl, jnp.exp(jnp.where(incl, gc - gcr, NEG))


def _gdn_a(k, bcol, decay):
    c = CHUNK
    return jnp.where(_row((c, c)) > _lane((c, c)), _mm_nt(k * bcol, k) * decay, 0.0)


def _gdn_chunk(q, k, v, gcol, bcol, state, gg, nw, inv_saved):
    c = CHUNK
    incl = _row((c, c)) >= _lane((c, c))
    gc, gl, decay = _gdn_decay(gcol)
    kb, vb = k * bcol, v * bcol
    inv = _inv_saved(_gdn_a(k, bcol, decay), inv_saved)
    eg = jnp.exp(gc)
    u = _mm(inv, vb)
    w = _mm(inv, kb * eg)
    pm = jnp.where(incl, _mm_nt(q, k) * decay, 0.0)
    kd = k * jnp.exp(gl - gc)
    qd = q * eg
    v_new = u - _mm(w, state)
    o = _mm(qd, state) + _mm(pm, v_new)
    state_new = state * jnp.exp(gl) + _mm_tn(kd, v_new)
    y = o * _rstd(o) * nw * (gg * jax.nn.sigmoid(gg))
    return y, state_new


_gdn_chunks = jax.vmap(_gdn_chunk, in_axes=(0, 0, 0, 0, 0, 0, 0, None, 0))


def _gdn_chain_inputs(chains, p, r, c, q_ref, k_ref, v_ref, g_ref, gg_ref, inv_ref):
    cols = {nm: [] for nm in ("q", "k", "v", "g", "b", "gg", "inv")}
    for b, hh in chains:
        h = GDN_HPS * p + hh
        ln = slice(hh * 128, (hh + 1) * 128)
        gt = g_ref[b, r, :]
        cols["q"].append(q_ref[b, r, ln])
        cols["k"].append(k_ref[b, r, ln])
        cols["v"].append(v_ref[b, r, ln])
        cols["g"].append(_pick_lane(gt, 8 + h))
        cols["b"].append(_pick_lane(gt, 12 + h))
        cols["gg"].append(gg_ref[b, r, ln])
        cols["inv"].append(inv_ref[b, hh, c])
    return [jnp.stack(cols[nm]) for nm in ("q", "k", "v", "g", "b", "gg", "inv")]


GDN_CB = 8
GDN_INV_CB = 16
GDN_HPS = 4


def gdn_inv(k, gates, seq):
    t = k.shape[0]
    nb, nc = t // seq, seq // CHUNK
    cb = min(GDN_INV_CB, nc)
    rb = cb * CHUNK
    nsb = seq // rb

    def body(k_ref, g_ref, o_ref):
        h = pl.program_id(1)
        mats = []
        for c in range(cb):
            r = slice(c * CHUNK, (c + 1) * CHUNK)
            gt = g_ref[r, :]
            _, _, decay = _gdn_decay(_pick_lane(gt, 8 + h))
            mats.append(_gdn_a(k_ref[r, :], _pick_lane(gt, 12 + h), decay))
        for c, inv in enumerate(_inv_fwd_many(mats)):
            o_ref[c] = inv

    return pl.pallas_call(
        body, grid=(nb, GDN_HEADS, nsb),
        in_specs=[pl.BlockSpec((rb, 128), lambda b, h, s: (b * nsb + s, h)),
                  pl.BlockSpec((rb, 128), lambda b, h, s: (b * nsb + s, 0))],
        out_specs=pl.BlockSpec((None, None, cb, CHUNK, CHUNK), lambda b, h, s: (b, h, s, 0, 0)),
        out_shape=jax.ShapeDtypeStruct((nb, GDN_HEADS, nc, CHUNK, CHUNK), F32),
        compiler_params=_cp("parallel", "parallel", "parallel"), name="gdn_inv")(k, gates)


def _gdn_specs(nb, nsb, cb, rev):
    blk = (lambda s: nsb - 1 - s) if rev else (lambda s: s)
    rb = cb * CHUNK
    pair = lambda off=0: pl.BlockSpec((nb, rb, 128 * GDN_HPS), lambda s, p: (0, blk(s), off + p))
    gate = lambda: pl.BlockSpec((nb, rb, 128), lambda s, p: (0, blk(s), 0))
    mats = lambda n: pl.BlockSpec((nb, GDN_HPS, cb, n, n), lambda s, p: (0, p, blk(s), 0, 0))
    return pair, gate, mats


def gdn_fwd(q, k, v, gates, proj, nw, inv, seq):
    t = q.shape[0]
    nb, nc = t // seq, seq // CHUNK
    cb = GDN_CB
    nsb = nc // cb
    chains = [(b, hh) for b in range(nb) for hh in range(GDN_HPS)]
    nch = len(chains)
    pair, gate, mats = _gdn_specs(nb, nsb, cb, False)

    def body(q_ref, k_ref, v_ref, g_ref, gg_ref, inv_ref, nw_ref, y_ref, st_ref, carry):
        s, p = pl.program_id(0), pl.program_id(1)

        @pl.when(s == 0)
        def _():
            for ci in range(nch):
                carry[p * nch + ci] = jnp.zeros((GDN_DH, GDN_DH), F32)

        def step(c, states):
            r = _rows(c, CHUNK)
            for ci, (b, hh) in enumerate(chains):
                st_ref[b, hh, c] = states[ci]
            ins = _gdn_chain_inputs(chains, p, r, c, q_ref, k_ref, v_ref, g_ref, gg_ref, inv_ref)
            y, states = _gdn_chunks(*ins[:5], states, ins[5], nw_ref[...], ins[6])
            for ci, (b, hh) in enumerate(chains):
                y_ref[b, r, hh * 128:(hh + 1) * 128] = y[ci]
            return states

        states = lax.fori_loop(0, cb, step, jnp.stack([carry[p * nch + ci] for ci in range(nch)]))
        for ci in range(nch):
            carry[p * nch + ci] = states[ci]

    v3 = lambda a: a.reshape(nb, seq, a.shape[1])
    y, st = pl.pallas_call(
        body, grid=(nsb, GDN_HEADS // GDN_HPS),
        in_specs=[pair(), pair(), pair(), gate(), pair(24 // GDN_HPS), mats(CHUNK), pl.BlockSpec((1, 128), lambda s, p: (0, 0))],
        out_specs=[pair(), mats(GDN_DH)],
        out_shape=[jax.ShapeDtypeStruct((nb, seq, 512), F32),
                   jax.ShapeDtypeStruct((nb, GDN_HEADS, nc, GDN_DH, GDN_DH), F32)],
        scratch_shapes=[pltpu.VMEM((GDN_HEADS // GDN_HPS * nch, GDN_DH, GDN_DH), F32)],
        compiler_params=_cp("arbitrary", "arbitrary"), name="gdn_fwd")(v3(q), v3(k), v3(v), v3(gates), v3(proj), inv, nw)
    return y.reshape(t, 512), st


def gdn_bwd(q, k, v, gates, proj, nw, inv, states, dy, seq):
    t = q.shape[0]
    nb, nc = t // seq, seq // CHUNK
    cb = GDN_CB // 2
    nsb = nc // cb
    chains = [(b, hh) for b in range(nb) for hh in range(GDN_HPS)]
    nch = len(chains)
    pair, gate, mats = _gdn_specs(nb, nsb, cb, True)

    def body(q_ref, k_ref, v_ref, g_ref, gg_ref, inv_ref, st_ref, dy_ref, nw_ref,
             dq_ref, dk_ref, dv_ref, dgg_ref, dg_ref, dnw_ref, carry):
        s, p = pl.program_id(0), pl.program_id(1)

        @pl.when((s == 0) & (p == 0))
        def _():
            dnw_ref[...] = jnp.zeros_like(dnw_ref)

        @pl.when(p == 0)
        def _():
            dg_ref[...] = jnp.zeros_like(dg_ref)

        @pl.when(s == 0)
        def _():
            for ci in range(nch):
                carry[p * nch + ci] = jnp.zeros((GDN_DH, GDN_DH), F32)

        def step(idx, dstates):
            c = cb - 1 - idx
            r = _rows(c, CHUNK)
            ins = _gdn_chain_inputs(chains, p, r, c, q_ref, k_ref, v_ref, g_ref, gg_ref, inv_ref)
            st = jnp.stack([st_ref[b, hh, c] for b, hh in chains])
            dy = jnp.stack([dy_ref[b, r, hh * 128:(hh + 1) * 128] for b, hh in chains])
            _, vjp = jax.vjp(_gdn_chunks, *ins[:5], st, ins[5], nw_ref[...], ins[6])
            dq, dk, dv, dgc, dbc, dstates, dgg, dnw, _ = vjp((dy, dstates))
            dnw_ref[...] += dnw
            lane = _lane((CHUNK, 128))
            for ci, (b, hh) in enumerate(chains):
                h = GDN_HPS * p + hh
                ln = slice(hh * 128, (hh + 1) * 128)
                dq_ref[b, r, ln] = dq[ci]
                dk_ref[b, r, ln] = dk[ci]
                dv_ref[b, r, ln] = dv[ci]
                dgg_ref[b, r, ln] = dgg[ci].astype(BF16)
                dg_ref[b, r, :] = jnp.where(lane == 8 + h, dgc[ci], jnp.where(lane == 12 + h, dbc[ci], dg_ref[b, r, :]))
            return dstates

        dstates = lax.fori_loop(0, cb, step, jnp.stack([carry[p * nch + ci] for ci in range(nch)]))
        for ci in range(nch):
            carry[p * nch + ci] = dstates[ci]

    v3 = lambda a: a.reshape(nb, seq, a.shape[1])
    res = pl.pallas_call(
        body, grid=(nsb, GDN_HEADS // GDN_HPS),
        in_specs=[pair(), pair(), pair(), gate(), pair(24 // GDN_HPS), mats(CHUNK), mats(GDN_DH), pair(),
                  pl.BlockSpec((1, 128), lambda s, p: (0, 0))],
        out_specs=[pair(), pair(), pair(), pair(), gate(), pl.BlockSpec((1, 128), lambda s, p: (0, 0))],
        out_shape=[jax.ShapeDtypeStruct((nb, seq, 512), F32)] * 3 + [jax.ShapeDtypeStruct((nb, seq, 512), BF16)]
                  + [jax.ShapeDtypeStruct((nb, seq, 128), F32), jax.ShapeDtypeStruct((1, 128), F32)],
        scratch_shapes=[pltpu.VMEM((GDN_HEADS // GDN_HPS * nch, GDN_DH, GDN_DH), F32)],
        compiler_params=_cp("arbitrary", "arbitrary"),
        name="gdn_bwd")(v3(q), v3(k), v3(v), v3(gates), v3(proj), inv, states, v3(dy), nw)
    return [a.reshape(t, a.shape[2]) for a in res[:5]] + [res[5]]


def loss_head(y, target, tm=512):
    t, d = y.shape

    def body(y_ref, t_ref, s_ref, dy_ref):
        @pl.when(pl.program_id(0) == 0)
        def _():
            s_ref[...] = jnp.zeros_like(s_ref)

        err = y_ref[...] - t_ref[...]
        s_ref[...] += jnp.sum(err * err, axis=0, keepdims=True)
        dy_ref[...] = err * (1.0 / d)

    return pl.pallas_call(
        body, grid=(t // tm,),
        in_specs=[pl.BlockSpec((tm, d), lambda i: (i, 0)), pl.BlockSpec((tm, d), lambda i: (i, 0))],
        out_specs=[pl.BlockSpec((1, d), lambda i: (0, 0)), pl.BlockSpec((tm, d), lambda i: (i, 0))],
        out_shape=[jax.ShapeDtypeStruct((1, d), F32), jax.ShapeDtypeStruct((t, d), F32)],
        compiler_params=_cp("arbitrary"), name="loss_head")(y, target)


def _place():
    return lax.axis_index("x"), lax.axis_index("y"), lax.axis_index("c")


def _peer(k):
    x, y, c = _place()
    px = 1 - x if (k >> 2) & 1 else x
    py = 1 - y if (k >> 1) & 1 else y
    pc = 1 - c if k & 1 else c
    return (px, py, pc), 4 * px + 2 * py + pc


_ANY = pl.BlockSpec(memory_space=pl.ANY)
_SEM = pl.BlockSpec(memory_space=pltpu.SEMAPHORE)
_EFFECT = pltpu.SideEffectType.DATAFLOW_SIDE_EFFECTING


def _me():
    x, y, c = _place()
    return 4 * x + 2 * y + c


def _remote_copy(ins, lands, scatter, send_sems, recv_sems, a, k, arriving):
    pid, pidx = _peer(k)
    return pltpu.make_async_remote_copy(src_ref=ins[a].at[pidx] if scatter[a] else ins[a],
                                        dst_ref=lands[a].at[pidx if arriving else _me()],
                                        send_sem=send_sems.at[a * N_DEV + k], recv_sem=recv_sems.at[a * N_DEV + k],
                                        device_id=pid, device_id_type=MESH)


def _local_copy(ins, lands, scatter, loc_sems, a):
    me = _me()
    return pltpu.make_async_copy(ins[a].at[me] if scatter[a] else ins[a], lands[a].at[me], loc_sems.at[a])


def exchange_start(arrays, scatter, name, after):
    n = len(arrays)
    lands = [lax.empty(a.shape if s else (N_DEV,) + a.shape, a.dtype) for a, s in zip(arrays, scatter)]

    def body(*refs):
        ins, lds = refs[:n], refs[n:2 * n]
        send_sems, recv_sems, loc_sems = refs[2 * n + 1:2 * n + 4]
        token = refs[-1]
        for k in range(1, N_DEV):
            for a in range(n):
                _remote_copy(ins, lds, scatter, send_sems, recv_sems, a, k, False).start()
        for a in range(n):
            _local_copy(ins, lds, scatter, loc_sems, a).start()
        token[...] = jnp.zeros_like(token)

    hbm = lambda a: pltpu.HBM(a.shape, a.dtype)
    res = pl.pallas_call(
        body, name=name,
        in_specs=[_ANY] * (2 * n + 1),
        out_specs=[_SEM, _SEM, _SEM] + [_ANY] * (2 * n) + [pl.BlockSpec(memory_space=pltpu.VMEM)],
        out_shape=[pltpu.SemaphoreType.DMA((n * N_DEV,)), pltpu.SemaphoreType.DMA((n * N_DEV,)),
                   pltpu.SemaphoreType.DMA((n,))]
                  + [hbm(a) for a in arrays] + [hbm(a) for a in lands] + [jax.ShapeDtypeStruct((8, 128), F32)],
        input_output_aliases={i: 3 + i for i in range(2 * n)},
        compiler_params=pltpu.CompilerParams(has_side_effects=_EFFECT),
    )(*[pltpu.with_memory_space_constraint(a, pltpu.HBM) for a in list(arrays) + lands], after)
    return res[0:3], res[3:3 + n], res[3 + n:3 + 2 * n], res[-1]


def exchange_wait(sems, arrays, lands, scatter, after, name):
    n = len(arrays)

    def body(*refs):
        ins, lds = refs[:n], refs[n:2 * n]
        ssem, rsem, lsem = refs[2 * n:2 * n + 3]
        for a in range(n):
            _local_copy(ins, lds, scatter, lsem, a).wait()
        for k in range(1, N_DEV):
            for a in range(n):
                _remote_copy(ins, lds, scatter, ssem, rsem, a, k, True).wait_recv()
        for k in range(1, N_DEV):
            for a in range(n):
                _remote_copy(ins, lds, scatter, ssem, rsem, a, k, False).wait_send()

    hbm = lambda a: pltpu.HBM(a.shape, a.dtype)
    res = pl.pallas_call(
        body, name=name,
        in_specs=[_ANY] * (2 * n) + [_SEM, _SEM, _SEM, _ANY],
        out_specs=[_ANY] * (2 * n),
        out_shape=[hbm(a) for a in arrays] + [hbm(a) for a in lands],
        input_output_aliases={i: i for i in range(2 * n)},
        compiler_params=pltpu.CompilerParams(has_side_effects=_EFFECT),
    )(*arrays, *lands, *sems, after)
    return list(res[n:])


def exchange_begin(arrays, scatter, name, after):
    sems, arrays_thru, lands_thru, token = exchange_start(arrays, scatter, name + "_start", after)
    return (sems, arrays_thru, lands_thru, scatter, name), token


def exchange_end(state, after):
    sems, arrays_thru, lands_thru, scatter, name = state
    return exchange_wait(sems, arrays_thru, lands_thru, scatter, after, name + "_wait")


def adamw_reduce(slots, w, m, v, l, name, after=None, prev=None):
    nl, r, c = w.shape
    tr = r
    while tr * c * 4 > (1 << 20) and tr % 16 == 0:
        tr //= 2
    bc1 = 1.0 - ADAM_B1 ** ADAM_STEP
    bc2 = 1.0 - ADAM_B2 ** ADAM_STEP

    def body(s_ref, w_ref, m_ref, v_ref, *rest):
        g_ref, d_ref, nm_ref, nv_ref = rest[-4:]
        g = s_ref[0].astype(F32)
        for j in range(1, N_DEV):
            g = g + s_ref[j].astype(F32)
        nm = ADAM_B1 * m_ref[...] + (1.0 - ADAM_B1) * g
        nv = ADAM_B2 * v_ref[...] + (1.0 - ADAM_B2) * (g * g)
        g_ref[...] = g
        nm_ref[...] = nm
        nv_ref[...] = nv
        d_ref[...] = -ADAM_LR * ((nm / bc1) / (jnp.sqrt(nv / bc2) + ADAM_EPS) + ADAM_WD * w_ref[...])

    blk = lambda: pl.BlockSpec((None, tr, c), lambda i: (l, i, 0))
    extra = ([] if after is None else [after]) + ([] if prev is None else list(prev))
    first_prev = 4 + (after is not None)
    return pl.pallas_call(
        body, grid=(r // tr,),
        in_specs=[pl.BlockSpec((N_DEV, tr, c), lambda i: (0, i, 0)), blk(), blk(), blk()] + [_ANY] * len(extra),
        out_specs=[blk(), blk(), blk(), blk()],
        out_shape=[jax.ShapeDtypeStruct((nl, r, c), F32)] * 4,
        input_output_aliases={} if prev is None else {first_prev + j: j for j in range(4)},
        compiler_params=_cp("parallel"), name=name)(slots, w, m, v, *extra)


BIG = ("ffn1_w_in", "ffn1_w_out", "w_in", "gdn_conv", "w_out", "ffn2_w_in", "ffn2_w_out")
GROUPS = (BIG[0:2], BIG[2:5], BIG[5:7])
SMALL = ("ffn1_norm", "mix_norm", "fox_q_norm", "fox_k_norm", "fox_f_bias", "gdn_a_log", "gdn_dt_bias",
         "gdn_out_norm", "ffn2_norm")
WEIGHTS = ("ffn1_norm", "ffn1_w_in", "ffn1_w_out", "mix_norm", "w_in", "fox_q_norm", "fox_k_norm", "fox_f_bias",
           "gdn_conv", "gdn_a_log", "gdn_dt_bias", "gdn_out_norm", "w_out", "ffn2_norm", "ffn2_w_in", "ffn2_w_out")
IN_COLS = (("fq", 512), ("fk", 512), ("fv", 512), ("ff", 8), ("gq", 512), ("gk", 512), ("gv", 512),
           ("ga", 4), ("gb", 4), ("gg", 512))
MY_BIG = ("fq", "fk", "fv", "gq", "gk", "gv", "gg")
MY_SMALL = ("ff", "ga", "gb")
SMALL_ROWS = 8 * 128


def _in_cols_to_mine(w):
    off, parts = 0, {}
    for nm, wd in IN_COLS:
        parts[nm] = w[:, off:off + wd]
        off += wd
    small = jnp.concatenate([parts[nm] for nm in MY_SMALL], axis=1)
    small = jnp.pad(small, ((0, 0), (0, 128 - small.shape[1])))
    return jnp.concatenate([parts[nm] for nm in MY_BIG] + [small], axis=1)


def _in_cols_from_mine(g):
    parts = {nm: g[:, i * 512:(i + 1) * 512] for i, nm in enumerate(MY_BIG)}
    off = N_BIG
    for nm in MY_SMALL:
        wd = dict(IN_COLS)[nm]
        parts[nm] = g[:, off:off + wd]
        off += wd
    return jnp.concatenate([parts[nm] for nm, _ in IN_COLS], axis=1)


def _pack_small(vals):
    rows = []
    nl = vals[SMALL[0]].shape[0]
    for l in range(nl):
        for nm in SMALL:
            v = vals[nm][l].reshape(-1)
            pad = (-v.shape[0]) % SMALL_ROWS
            rows.append(jnp.pad(v, (0, pad)).reshape(-1, 128))
    return jnp.concatenate(rows, axis=0)


def _unpack_small(packed, like):
    out = {nm: [] for nm in SMALL}
    row = 0
    nl = like[SMALL[0]].shape[0]
    for l in range(nl):
        for nm in SMALL:
            n = like[nm].shape[1]
            nr = -(-n // SMALL_ROWS) * 8
            out[nm].append(packed[row:row + nr].reshape(-1)[:n])
            row += nr
    return {nm: jnp.stack(v) for nm, v in out.items()}


def kernel(x, ffn1_norm, ffn1_w_in, ffn1_w_out, mix_norm, w_in, fox_q_norm, fox_k_norm, fox_f_bias, gdn_conv, gdn_a_log, gdn_dt_bias, gdn_out_norm, w_out, ffn2_norm, ffn2_w_in, ffn2_w_out, loss_target, m_ffn1_norm, m_ffn1_w_in, m_ffn1_w_out, m_mix_norm, m_w_in, m_fox_q_norm, m_fox_k_norm, m_fox_f_bias, m_gdn_conv, m_gdn_a_log, m_gdn_dt_bias, m_gdn_out_norm, m_w_out, m_ffn2_norm, m_ffn2_w_in, m_ffn2_w_out, v_ffn1_norm, v_ffn1_w_in, v_ffn1_w_out, v_mix_norm, v_w_in, v_fox_q_norm, v_fox_k_norm, v_fox_f_bias, v_gdn_conv, v_gdn_a_log, v_gdn_dt_bias, v_gdn_out_norm, v_w_out, v_ffn2_norm, v_ffn2_w_in, v_ffn2_w_out):
    wts = dict(ffn1_norm=ffn1_norm, ffn1_w_in=ffn1_w_in, ffn1_w_out=ffn1_w_out, mix_norm=mix_norm, w_in=w_in,
               fox_q_norm=fox_q_norm, fox_k_norm=fox_k_norm, fox_f_bias=fox_f_bias, gdn_conv=gdn_conv,
               gdn_a_log=gdn_a_log, gdn_dt_bias=gdn_dt_bias, gdn_out_norm=gdn_out_norm, w_out=w_out,
               ffn2_norm=ffn2_norm, ffn2_w_in=ffn2_w_in, ffn2_w_out=ffn2_w_out)
    mom = dict(ffn1_norm=m_ffn1_norm, ffn1_w_in=m_ffn1_w_in, ffn1_w_out=m_ffn1_w_out, mix_norm=m_mix_norm, w_in=m_w_in,
               fox_q_norm=m_fox_q_norm, fox_k_norm=m_fox_k_norm, fox_f_bias=m_fox_f_bias, gdn_conv=m_gdn_conv,
               gdn_a_log=m_gdn_a_log, gdn_dt_bias=m_gdn_dt_bias, gdn_out_norm=m_gdn_out_norm, w_out=m_w_out,
               ffn2_norm=m_ffn2_norm, ffn2_w_in=m_ffn2_w_in, ffn2_w_out=m_ffn2_w_out)
    var = dict(ffn1_norm=v_ffn1_norm, ffn1_w_in=v_ffn1_w_in, ffn1_w_out=v_ffn1_w_out, mix_norm=v_mix_norm, w_in=v_w_in,
               fox_q_norm=v_fox_q_norm, fox_k_norm=v_fox_k_norm, fox_f_bias=v_fox_f_bias, gdn_conv=v_gdn_conv,
               gdn_a_log=v_gdn_a_log, gdn_dt_bias=v_gdn_dt_bias, gdn_out_norm=v_gdn_out_norm, w_out=v_w_out,
               ffn2_norm=v_ffn2_norm, ffn2_w_in=v_ffn2_w_in, ffn2_w_out=v_ffn2_w_out)
    nb, seq, d = x.shape
    t = nb * seq
    depth = ffn1_norm.shape[0]

    stages = [(l, gi) for l in range(depth) for gi in range(len(GROUPS))]

    def shards_of(l, gi):
        return [wts[nm][l] if nm == "gdn_conv" else wts[nm][l].astype(BF16) for nm in GROUPS[gi]]

    def behind(nw, token):
        return nw if token is None else nw + token[0:1, 0:1]

    def small_params(l):
        return dict(
            n1=ffn1_norm[l][None], nmix=mix_norm[l][None], n2=ffn2_norm[l][None],
            qw=fox_q_norm[l][None], kw=fox_k_norm[l][None], onw=gdn_out_norm[l][None],
            gp=jnp.concatenate([
                jnp.concatenate([fox_f_bias[l], gdn_dt_bias[l], jnp.zeros((116,), F32)])[None],
                jnp.concatenate([jnp.zeros((8,), F32), gdn_a_log[l], jnp.zeros((116,), F32)])[None],
                jnp.zeros((6, 128), F32)], axis=0))

    h = x.reshape(t, d)
    state, token = exchange_begin(shards_of(0, 0), [False] * len(GROUPS[0]), "gather_0", ffn1_norm)
    landed = exchange_end(state, token)
    saved = [dict(p=small_params(l)) for l in range(depth)]
    for k, (l, gi) in enumerate(stages):
        s, w, token = saved[l], landed, None
        p = s["p"]
        if k + 1 < len(stages):
            nl, ng = stages[k + 1]
            state, token = exchange_begin(shards_of(nl, ng), [False] * len(GROUPS[ng]), f"gather_{k + 1}", landed[0])
        if gi == 0:
            fb = w[0].shape[2]
            p["w1i"], p["w1o"] = w[0].reshape(2, 4, d, fb), w[1].reshape(4, fb, d)
            s["x0"] = h
            h, *s["ffn1"] = ffn_fwd(h, behind(p["n1"], token), p["w1i"], p["w1o"])
            s["x1"] = h
        elif gi == 1:
            p["wi"] = _in_cols_to_mine(w[0].transpose(1, 0, 2).reshape(d, -1))
            p["cw"] = w[1].transpose(1, 0, 2).reshape(CONV_W, -1)
            p["wo"] = w[2].reshape(d, d)
            proj, hn = inproj_fwd(h, behind(p["nmix"], token), p["wi"])
            gates = gates_fwd(proj, p["gp"], seq)
            yf, lse = attn_fwd(proj, gates, p["qw"], p["kw"], seq, tq=min(seq, ATTN_TQ_FWD))
            qh, kh, vh = gdn_pre_fwd(proj, p["cw"], seq)
            inv = gdn_inv(kh, gates, seq)
            yg, st = gdn_fwd(qh, kh, vh, gates, proj, p["onw"], inv, seq)
            h, ycat = outproj_fwd(h, yf, yg, p["wo"])
            s.update(x2=h, proj=proj, hn=hn, gates=gates, yf=yf, lse=lse, qh=qh, kh=kh, vh=vh, st=st, inv=inv, ycat=ycat)
        else:
            fb = w[0].shape[2]
            p["w2i"], p["w2o"] = w[0].reshape(2, 4, d, fb), w[1].reshape(4, fb, d)
            h, *s["ffn2"] = ffn_fwd(h, behind(p["n2"], token), p["w2i"], p["w2o"])
        if k + 1 < len(stages):
            landed = exchange_end(state, h)

    sq, dh = loss_head(h, loss_target.reshape(t, d))
    loss = lax.psum(0.5 * jnp.sum(sq) / d, ("x", "y", "c"))

    got = [None] * len(stages)
    pending, token = None, None
    gsmall = {nm: [None] * depth for nm in SMALL}
    for k in reversed(range(len(stages))):
        l, gi = stages[k]
        s = saved[l]
        p = s["p"]
        if gi != 1:
            nw, xin, wi_, wo_, nm_n, (xn, gu, hh) = (
                (p["n1"], s["x0"], p["w1i"], p["w1o"], "ffn1_norm", s["ffn1"]) if gi == 0 else
                (p["n2"], s["x2"], p["w2i"], p["w2o"], "ffn2_norm", s["ffn2"]))
            dh, dn, dgu, dyh = ffn_bwd(xin, dh, behind(nw, token), gu, wi_, wo_)
            g_in, g_out = wgrad_ffn_in(xn, dgu), wgrad_ffn_out(hh, dyh)
            send = [g_in.reshape(N_DEV, d, g_in.shape[3]), g_out.reshape(N_DEV, -1, d)]
            gsmall[nm_n][l] = dn[0]
        else:
            dyf, dyg, dyb = outproj_bwd(dh, p["wo"], token)
            g_wo = wgrad_2d(s["ycat"], dyb, 512, "wgrad_w_out")
            dq, dk, dv, dga, dqw, dkw = attn_bwd(s["proj"], s["gates"], p["qw"], p["kw"], s["yf"], s["lse"], dyf, seq,
                                                 tq=min(seq, ATTN_TQ_BWD))
            dqh, dkh, dvh, dgg, dgb, donw = gdn_bwd(s["qh"], s["kh"], s["vh"], s["gates"], s["proj"], p["onw"],
                                                     s["inv"], s["st"], dyg, seq)
            dxq, dxk, dxv, dwq, dwk, dwv = gdn_pre_bwd(s["proj"], p["cw"], dqh, dkh, dvh, seq)
            dsm, dgp = gates_bwd(s["proj"], p["gp"], dga, dgb, seq)
            dh, dnmix, dproj = inproj_bwd(s["x1"], dh, p["nmix"], p["wi"], [dq, dk, dv, dxq, dxk, dxv, dgg, dsm])
            g_wi = wgrad_2d(s["hn"], dproj, 512, "wgrad_w_in", F32)
            g_cw = jnp.concatenate([dwq, dwk, dwv], axis=1)
            send = [_in_cols_from_mine(g_wi).reshape(d, N_DEV, -1).transpose(1, 0, 2),
                    g_cw.reshape(CONV_W, N_DEV, -1).transpose(1, 0, 2), g_wo.reshape(N_DEV, -1, d)]
            for nm, val in (("mix_norm", dnmix[0]), ("fox_q_norm", dqw[0]), ("fox_k_norm", dkw[0]),
                            ("fox_f_bias", dgp[0, 0:8]), ("gdn_a_log", dgp[1, 8:12]), ("gdn_dt_bias", dgp[0, 8:12]),
                            ("gdn_out_norm", donw[0])):
                gsmall[nm][l] = val
        flags = [True] * len(send)
        if k == 0:
            send.append(_pack_small({nm: jnp.stack(v) for nm, v in gsmall.items()}))
            flags.append(False)
        prev = dh
        if pending is not None:
            got[pending[1]] = exchange_end(pending[0], dh)
            prev = got[pending[1]][0]
        state, token = exchange_begin(send, flags, f"exchange_grads_{k}", prev)
        pending = (state, k)
    grad_x = dh.reshape(nb, seq, d)

    res = {}

    def update_stage(k, slots, after):
        l, gi = stages[k]
        for i, nm in enumerate(GROUPS[gi]):
            r, c = wts[nm].shape[1:]
            res[nm] = adamw_reduce(slots[i].reshape(N_DEV, r, c), wts[nm], mom[nm], var[nm], l, f"adamw_{nm}_{l}",
                                   after, res.get(nm))
            if after is not None:
                after = res[nm][0]
        return after

    last = token
    for k in range(1, len(stages)):
        last = update_stage(k, got[k], last)
    got[0] = exchange_end(pending[0], last)
    update_stage(0, got[0], None)
    small_like = {nm: wts[nm] for nm in SMALL}
    sm = adamw_reduce(got[0][-1], _pack_small(small_like)[None], _pack_small({nm: mom[nm] for nm in SMALL})[None],
                      _pack_small({nm: var[nm] for nm in SMALL})[None], 0, "adamw_small")
    sm = [_unpack_small(a[0], small_like) for a in sm]
    for nm in SMALL:
        res[nm] = [sm[j][nm] for j in range(4)]
    return (loss, grad_x, *[res[nm][0] for nm in WEIGHTS], *[res[nm][1] for nm in WEIGHTS],
            *[res[nm][2] for nm in WEIGHTS], *[res[nm][3] for nm in WEIGHTS])
```

```python
import functools

import jax
import jax.numpy as jnp
from jax import lax
from jax.experimental import pallas as pl
from jax.experimental.pallas import tpu as pltpu

F32 = jnp.float32
BF16 = jnp.bfloat16
EPS = 1e-6
N_DEV = 8
MESH = pl.DeviceIdType.MESH
HIGHEST = lax.Precision.HIGHEST
VMEM_LIMIT = 56 * 1024 * 1024

FOX_HEADS, FOX_DH = 8, 64
GDN_HEADS, GDN_DH = 4, 128
CHUNK = 64
CONV_W = 4

ADAM_LR, ADAM_B1, ADAM_B2, ADAM_EPS, ADAM_WD, ADAM_STEP = 0.001, 0.9, 0.999, 1e-08, 0.01, 10


def _cp(*sem):
    return pltpu.CompilerParams(dimension_semantics=sem, vmem_limit_bytes=VMEM_LIMIT)


def _dot(a, b):
    return jnp.dot(a, b, preferred_element_type=F32)


def _dot_nt(a, b):
    return lax.dot_general(a, b, (((1,), (1,)), ((), ())), preferred_element_type=F32)


def _dot_tn(a, b):
    return lax.dot_general(a, b, (((0,), (0,)), ((), ())), preferred_element_type=F32)


def _rstd(xf):
    return lax.rsqrt(jnp.mean(xf * xf, axis=-1, keepdims=True) + EPS)


def _rms_bwd(xf, r, dyn):
    return r * dyn - xf * (r * r * r) * jnp.mean(dyn * xf, axis=-1, keepdims=True)


def ffn_fwd(x, nw, w_in, w_out, tm=1024, rc=1024):
    t, d = x.shape
    nj, fb = w_out.shape[0], w_out.shape[1]
    tm = min(tm, t)
    rc = min(rc, tm)

    def body(x_ref, nw_ref, wi_ref, wo_ref, o_ref, xn_ref, gu_ref, h_ref, acc_ref):
        j = pl.program_id(1)

        @pl.when(j == 0)
        def _():
            xf = x_ref[...]
            xn_ref[...] = (xf * _rstd(xf) * nw_ref[...]).astype(BF16)
            acc_ref[...] = jnp.zeros_like(acc_ref)

        rows = [slice(c * rc, (c + 1) * rc) for c in range(tm // rc)]
        gs = [_dot(xn_ref[r, :], wi_ref[0]) for r in rows]
        us = [_dot(xn_ref[r, :], wi_ref[1]) for r in rows]
        hs = []
        for g, u, r in zip(gs, us, rows):
            sg = jax.nn.sigmoid(g)
            silu = g * sg
            h = (silu * u).astype(BF16)
            gu_ref[0, r, :] = (u * (sg * (1.0 + g * (1.0 - sg)))).astype(BF16)
            gu_ref[1, r, :] = silu.astype(BF16)
            h_ref[r, :] = h
            hs.append(h)
        for h, r in zip(hs, rows):
            acc_ref[r, :] += _dot(h, wo_ref[...])

        @pl.when(j == nj - 1)
        def _():
            o_ref[...] = x_ref[...] + 0.5 * acc_ref[...]

    return pl.pallas_call(
        body, grid=(t // tm, nj),
        in_specs=[pl.BlockSpec((tm, d), lambda i, j: (i, 0)),
                  pl.BlockSpec((1, d), lambda i, j: (0, 0)),
                  pl.BlockSpec((2, None, d, fb), lambda i, j: (0, j, 0, 0)),
                  pl.BlockSpec((None, fb, d), lambda i, j: (j, 0, 0))],
        out_specs=[pl.BlockSpec((tm, d), lambda i, j: (i, 0)),
                   pl.BlockSpec((tm, d), lambda i, j: (i, 0)),
                   pl.BlockSpec((2, None, tm, fb), lambda i, j: (0, j, i, 0)),
                   pl.BlockSpec((None, tm, fb), lambda i, j: (j, i, 0))],
        out_shape=[jax.ShapeDtypeStruct((t, d), F32), jax.ShapeDtypeStruct((t, d), BF16),
                   jax.ShapeDtypeStruct((2, nj, t, fb), BF16), jax.ShapeDtypeStruct((nj, t, fb), BF16)],
        scratch_shapes=[pltpu.VMEM((tm, d), F32)],
        compiler_params=_cp("parallel", "arbitrary"), name="ffn_fwd")(x, nw, w_in, w_out)


def ffn_bwd(x, dy, nw, gu, w_in, w_out, tm=512, rc=256):
    t, d = x.shape
    nj, fb = w_out.shape[0], w_out.shape[1]
    tm = min(tm, t)
    rc = min(rc, tm)

    def body(x_ref, dy_ref, nw_ref, gu_ref, wi_ref, wo_ref,
             dx_ref, dnw_ref, dgu_ref, dyh_ref, acc_ref):
        i, j = pl.program_id(0), pl.program_id(1)

        @pl.when(j == 0)
        def _():
            dyh_ref[...] = (0.5 * dy_ref[...]).astype(BF16)
            acc_ref[...] = jnp.zeros_like(acc_ref)

        @pl.when((i == 0) & (j == 0))
        def _():
            dnw_ref[...] = jnp.zeros_like(dnw_ref)

        rows = [slice(c * rc, (c + 1) * rc) for c in range(tm // rc)]
        dhs = [_dot_nt(dyh_ref[r, :], wo_ref[...]) for r in rows]
        dgs = [(dh * gu_ref[0, r, :].astype(F32)).astype(BF16) for dh, r in zip(dhs, rows)]
        dus = [(dh * gu_ref[1, r, :].astype(F32)).astype(BF16) for dh, r in zip(dhs, rows)]
        for dg, du, r in zip(dgs, dus, rows):
            dgu_ref[0, r, :] = dg
            dgu_ref[1, r, :] = du
        for dg, du, r in zip(dgs, dus, rows):
            acc_ref[r, :] += _dot_nt(dg, wi_ref[0]) + _dot_nt(du, wi_ref[1])

        @pl.when(j == nj - 1)
        def _():
            xf = x_ref[...]
            r = _rstd(xf)
            dxn = acc_ref[...]
            dnw_ref[...] += jnp.sum(dxn * xf * r, axis=0, keepdims=True)
            dx_ref[...] = _rms_bwd(xf, r, dxn * nw_ref[...]) + dy_ref[...]

    return pl.pallas_call(
        body, grid=(t // tm, nj),
        in_specs=[pl.BlockSpec((tm, d), lambda i, j: (i, 0)),
                  pl.BlockSpec((tm, d), lambda i, j: (i, 0)),
                  pl.BlockSpec((1, d), lambda i, j: (0, 0)),
                  pl.BlockSpec((2, None, tm, fb), lambda i, j: (0, j, i, 0)),
                  pl.BlockSpec((2, None, d, fb), lambda i, j: (0, j, 0, 0)),
                  pl.BlockSpec((None, fb, d), lambda i, j: (j, 0, 0))],
        out_specs=[pl.BlockSpec((tm, d), lambda i, j: (i, 0)),
                   pl.BlockSpec((1, d), lambda i, j: (0, 0)),
                   pl.BlockSpec((2, None, tm, fb), lambda i, j: (0, j, i, 0)),
                   pl.BlockSpec((tm, d), lambda i, j: (i, 0))],
        out_shape=[jax.ShapeDtypeStruct((t, d), F32),
                   jax.ShapeDtypeStruct((1, d), F32),
                   jax.ShapeDtypeStruct((2, nj, t, fb), BF16),
                   jax.ShapeDtypeStruct((t, d), BF16)],
        scratch_shapes=[pltpu.VMEM((tm, d), F32)],
        compiler_params=_cp("arbitrary", "arbitrary"), name="ffn_bwd")(x, dy, nw, gu, w_in, w_out)


def _wgrad_call(a, b, a_spec, b_spec, out_shape, out_spec, grid, name, out_dtype=BF16):
    last = len(grid) - 1
    acc_shape = tuple(s for s in out_spec.block_shape if s is not None)

    def body(a_ref, b_ref, o_ref, acc_ref):
        @pl.when(pl.program_id(last) == 0)
        def _():
            acc_ref[...] = jnp.zeros_like(acc_ref)

        if len(acc_shape) == 3:
            shared_a = a_ref[...] if len(a_ref.shape) == 2 else None
            shared_b = b_ref[...] if len(b_ref.shape) == 2 else None
            for s in range(acc_shape[0]):
                acc_ref[s] += _dot_tn(a_ref[s] if shared_a is None else shared_a,
                                      b_ref[s] if shared_b is None else shared_b)
        else:
            acc_ref[...] += _dot_tn(a_ref[...], b_ref[...])

        @pl.when(pl.program_id(last) == grid[last] - 1)
        def _():
            o_ref[...] = acc_ref[...].astype(o_ref.dtype)

    sem = ("parallel",) * last + ("arbitrary",)
    return pl.pallas_call(body, grid=grid, in_specs=[a_spec, b_spec], out_specs=out_spec,
                          out_shape=jax.ShapeDtypeStruct(out_shape, out_dtype),
                          scratch_shapes=[pltpu.VMEM(acc_shape, F32)],
                          compiler_params=_cp(*sem), name=name)(a, b)


WGRAD_TM = 1024


def wgrad_ffn_in(xn, dgu, tm=WGRAD_TM):
    t, d = xn.shape
    _, nj, _, fb = dgu.shape
    tm = min(tm, t)
    return _wgrad_call(xn, dgu,
                       pl.BlockSpec((tm, d), lambda j, k: (k, 0)),
                       pl.BlockSpec((2, None, tm, fb), lambda j, k: (0, j, k, 0)),
                       (2, nj, d, fb), pl.BlockSpec((2, None, d, fb), lambda j, k: (0, j, 0, 0)),
                       (nj, t // tm), "wgrad_ffn_in")


def wgrad_ffn_out(h, dyh, tm=WGRAD_TM):
    nj, t, fb = h.shape
    d = dyh.shape[1]
    tm = min(tm, t)
    return _wgrad_call(h, dyh,
                       pl.BlockSpec((nj, tm, fb), lambda k: (0, k, 0)),
                       pl.BlockSpec((tm, d), lambda k: (k, 0)),
                       (nj, fb, d), pl.BlockSpec((nj, fb, d), lambda k: (0, 0, 0)),
                       (t // tm,), "wgrad_ffn_out")


def wgrad_2d(a, b, tk, name, out_dtype=BF16, tm=WGRAD_TM):
    t, k = a.shape
    n = b.shape[1]
    tm = min(tm, t)
    return _wgrad_call(a, b,
                       pl.BlockSpec((tm, tk), lambda c, s: (s, c)),
                       pl.BlockSpec((tm, n), lambda c, s: (s, 0)),
                       (k, n), pl.BlockSpec((tk, n), lambda c, s: (c, 0)),
                       (k // tk, t // tm), name, out_dtype)


N_BIG = 7 * 512
N_PROJ = N_BIG + 128
COL_SMALL = N_BIG // 128


def inproj_fwd(x, nw, w, tm=512):
    t, d = x.shape
    n = w.shape[1]

    def body(x_ref, nw_ref, w_ref, p_ref, hn_ref):
        xf = x_ref[...]
        hn = (xf * _rstd(xf) * nw_ref[...]).astype(BF16)
        hn_ref[...] = hn
        p_ref[...] = _dot(hn, w_ref[...])

    return pl.pallas_call(
        body, grid=(t // tm,),
        in_specs=[pl.BlockSpec((tm, d), lambda i: (i, 0)), pl.BlockSpec((1, d), lambda i: (0, 0)),
                  pl.BlockSpec((d, n), lambda i: (0, 0))],
        out_specs=[pl.BlockSpec((tm, n), lambda i: (i, 0)), pl.BlockSpec((tm, d), lambda i: (i, 0))],
        out_shape=[jax.ShapeDtypeStruct((t, n), F32), jax.ShapeDtypeStruct((t, d), BF16)],
        compiler_params=_cp("parallel"), name="inproj_fwd")(x, nw, w)


def inproj_bwd(x, dres, nw, w, dparts, tm=512):
    t, d = x.shape
    n = w.shape[1]
    widths = [p.shape[1] for p in dparts]
    assert sum(widths) == n

    def body(x_ref, dres_ref, nw_ref, w_ref, *rest):
        part_refs, (dx_ref, dnw_ref, dp_ref) = rest[:len(widths)], rest[len(widths):]

        @pl.when(pl.program_id(0) == 0)
        def _():
            dnw_ref[...] = jnp.zeros_like(dnw_ref)

        dp = jnp.concatenate([r[...].astype(BF16) for r in part_refs], axis=1)
        dp_ref[...] = dp
        dhn = _dot_nt(dp, w_ref[...])
        xf = x_ref[...]
        r = _rstd(xf)
        dnw_ref[...] += jnp.sum(dhn * xf * r, axis=0, keepdims=True)
        dx_ref[...] = _rms_bwd(xf, r, dhn * nw_ref[...]) + dres_ref[...]

    return pl.pallas_call(
        body, grid=(t // tm,),
        in_specs=[pl.BlockSpec((tm, d), lambda i: (i, 0)), pl.BlockSpec((tm, d), lambda i: (i, 0)),
                  pl.BlockSpec((1, d), lambda i: (0, 0)), pl.BlockSpec((d, n), lambda i: (0, 0))]
                 + [pl.BlockSpec((tm, wd), lambda i: (i, 0)) for wd in widths],
        out_specs=[pl.BlockSpec((tm, d), lambda i: (i, 0)), pl.BlockSpec((1, d), lambda i: (0, 0)),
                   pl.BlockSpec((tm, n), lambda i: (i, 0))],
        out_shape=[jax.ShapeDtypeStruct((t, d), F32), jax.ShapeDtypeStruct((1, d), F32),
                   jax.ShapeDtypeStruct((t, n), BF16)],
        compiler_params=_cp("arbitrary"), name="inproj_bwd")(x, dres, nw, w, *dparts)


def outproj_fwd(x, yf, yg, w, tm=512):
    t, d = x.shape
    hw = yf.shape[1]

    def body(x_ref, yf_ref, yg_ref, w_ref, o_ref, y_ref):
        y = jnp.concatenate([yf_ref[...], yg_ref[...]], axis=1).astype(BF16)
        y_ref[...] = y
        o_ref[...] = x_ref[...] + _dot(y, w_ref[...])

    return pl.pallas_call(
        body, grid=(t // tm,),
        in_specs=[pl.BlockSpec((tm, d), lambda i: (i, 0)), pl.BlockSpec((tm, hw), lambda i: (i, 0)),
                  pl.BlockSpec((tm, hw), lambda i: (i, 0)), pl.BlockSpec((2 * hw, d), lambda i: (0, 0))],
        out_specs=[pl.BlockSpec((tm, d), lambda i: (i, 0)), pl.BlockSpec((tm, 2 * hw), lambda i: (i, 0))],
        out_shape=[jax.ShapeDtypeStruct((t, d), F32), jax.ShapeDtypeStruct((t, 2 * hw), BF16)],
        compiler_params=_cp("parallel"), name="outproj_fwd")(x, yf, yg, w)


def outproj_bwd(dy, w, after=None, tm=512):
    t, d = dy.shape
    hw = w.shape[0] // 2
    extra = [] if after is None else [after]

    def body(dy_ref, w_ref, *rest):
        df_ref, dg_ref, dyb_ref = rest[-3:]
        dyb = dy_ref[...].astype(BF16)
        dyb_ref[...] = dyb
        dyy = _dot_nt(dyb, w_ref[...])
        df_ref[...] = dyy[:, :hw]
        dg_ref[...] = dyy[:, hw:]

    return pl.pallas_call(
        body, grid=(t // tm,),
        in_specs=[pl.BlockSpec((tm, d), lambda i: (i, 0)), pl.BlockSpec((2 * hw, d), lambda i: (0, 0))]
                 + [pl.BlockSpec(memory_space=pl.ANY)] * len(extra),
        out_specs=[pl.BlockSpec((tm, hw), lambda i: (i, 0)), pl.BlockSpec((tm, hw), lambda i: (i, 0)),
                   pl.BlockSpec((tm, d), lambda i: (i, 0))],
        out_shape=[jax.ShapeDtypeStruct((t, hw), F32), jax.ShapeDtypeStruct((t, hw), F32),
                   jax.ShapeDtypeStruct((t, d), BF16)],
        compiler_params=_cp("parallel"), name="outproj_bwd")(dy, w, *extra)


def _lane(shape):
    return lax.broadcasted_iota(jnp.int32, shape, 1)


def _row(shape):
    return lax.broadcasted_iota(jnp.int32, shape, 0)


def _gate_terms(val, gp_ref):
    z = val + gp_ref[0:1, :]
    sp = jnp.log(1.0 + jnp.exp(-jnp.abs(z)))
    return z, sp


def gates_fwd(proj, gp, seq, ts=512):
    t = proj.shape[0]
    nb, ns = t // seq, seq // ts

    def body(v_ref, gp_ref, o_ref, carry_ref):
        @pl.when(pl.program_id(1) == 0)
        def _():
            carry_ref[...] = jnp.zeros_like(carry_ref)

        z, sp = _gate_terms(v_ref[...], gp_ref)
        logsig = jnp.minimum(z, 0.0) - sp
        tri = (_row((ts, ts)) >= _lane((ts, ts))).astype(F32)
        cum = jnp.dot(tri, logsig, precision=HIGHEST, preferred_element_type=F32) + carry_ref[0:1, :]
        carry_ref[0:1, :] = cum[ts - 1:ts, :]
        g = -jnp.exp(gp_ref[1:2, :]) * (jnp.maximum(z, 0.0) + sp)
        beta = jax.nn.sigmoid(z)
        lane = _lane((ts, 128))
        o_ref[...] = jnp.where(lane < 8, cum, jnp.where(lane < 12, g, jnp.where(lane < 16, beta, 0.0)))

    return pl.pallas_call(
        body, grid=(nb, ns),
        in_specs=[pl.BlockSpec((ts, 128), lambda b, s: (b * ns + s, COL_SMALL)),
                  pl.BlockSpec((8, 128), lambda b, s: (0, 0))],
        out_specs=pl.BlockSpec((ts, 128), lambda b, s: (b * ns + s, 0)),
        out_shape=jax.ShapeDtypeStruct((t, 128), F32),
        scratch_shapes=[pltpu.VMEM((8, 128), F32)],
        compiler_params=_cp("parallel", "arbitrary"), name="gates_fwd")(proj, gp)


def gates_bwd(proj, gp, dga, dgb, seq, ts=512):
    t = proj.shape[0]
    nb, ns = t // seq, seq // ts

    def body(v_ref, gp_ref, da_ref, db_ref, ds_ref, dgp_ref, carry_ref):
        @pl.when(pl.program_id(1) == 0)
        def _():
            carry_ref[...] = jnp.zeros_like(carry_ref)

        @pl.when((pl.program_id(0) == 0) & (pl.program_id(1) == 0))
        def _():
            dgp_ref[...] = jnp.zeros_like(dgp_ref)

        lane = _lane((ts, 128))
        dgate = jnp.where(lane < 8, da_ref[...], jnp.where(lane < 16, db_ref[...], 0.0))
        z, sp = _gate_terms(v_ref[...], gp_ref)
        triu = (_row((ts, ts)) <= _lane((ts, ts))).astype(F32)
        dlog = jnp.dot(triu, dgate, precision=HIGHEST, preferred_element_type=F32) + carry_ref[0:1, :]
        carry_ref[0:1, :] = dlog[0:1, :]
        sig = jax.nn.sigmoid(z)
        nea = -jnp.exp(gp_ref[1:2, :])
        g = nea * (jnp.maximum(z, 0.0) + sp)
        dz = jnp.where(lane < 8, dlog * (1.0 - sig),
                       jnp.where(lane < 12, dgate * nea * sig, dgate * sig * (1.0 - sig)))
        dz = jnp.where(lane < 16, dz, 0.0)
        ds_ref[...] = dz.astype(BF16)
        dgp_ref[0:1, :] += jnp.where(lane[0:1] < 12, jnp.sum(dz, axis=0, keepdims=True), 0.0)
        dgp_ref[1:2, :] += jnp.where((lane[0:1] >= 8) & (lane[0:1] < 12), jnp.sum(dgate * g, axis=0, keepdims=True), 0.0)

    rev = lambda b, s: (b * ns + (ns - 1 - s), 0)
    return pl.pallas_call(
        body, grid=(nb, ns),
        in_specs=[pl.BlockSpec((ts, 128), lambda b, s: (b * ns + (ns - 1 - s), COL_SMALL)),
                  pl.BlockSpec((8, 128), lambda b, s: (0, 0)),
                  pl.BlockSpec((ts, 128), rev), pl.BlockSpec((ts, 128), rev)],
        out_specs=[pl.BlockSpec((ts, 128), rev), pl.BlockSpec((8, 128), lambda b, s: (0, 0))],
        out_shape=[jax.ShapeDtypeStruct((t, 128), BF16), jax.ShapeDtypeStruct((8, 128), F32)],
        scratch_shapes=[pltpu.VMEM((8, 128), F32)],
        compiler_params=_cp("arbitrary", "arbitrary"), name="gates_bwd")(proj, gp, dga, dgb)


NEG = -1e30
ATTN_TQ_FWD = 1024
ATTN_TQ_BWD = 512


def _pick_lane(tile, idx):
    return jnp.sum(jnp.where(_lane(tile.shape) == idx, tile, 0.0), axis=1, keepdims=True)


def _col_to_row(col, n):
    return jnp.sum(jnp.where(_row((n, n)) == _lane((n, n)), col, 0.0), axis=0, keepdims=True)


def _row_to_col(row, n):
    return jnp.sum(jnp.where(_row((n, n)) == _lane((n, n)), row, 0.0), axis=1, keepdims=True)


def _rows(i, n):
    return pl.ds(pl.multiple_of(i * n, n), n)


LOG2E = 1.4426950408889634
LN2 = 0.6931471805599453


def _rowsum(z, width, passes=2):
    ones = jnp.ones((z.shape[1], width), BF16)
    total, rest = None, z
    for _ in range(passes):
        part = rest.astype(BF16)
        rest = rest - part.astype(F32)
        total = _dot(part, ones) if total is None else total + _dot(part, ones)
    return total


def _split_dot(x, mat, passes):
    total, rest = None, x
    for _ in range(passes):
        part = rest.astype(BF16)
        rest = rest - part.astype(F32)
        total = _dot(part, mat) if total is None else total + _dot(part, mat)
    return total


def _pair_mats(p, dh):
    r, l = _row((128, 128)), _lane((128, 128))
    same = (r < dh) == (l < dh)
    upper = (l >= dh).astype(jnp.int32)
    as_bf16 = lambda m: m.astype(BF16)
    return dict(own=as_bf16(same), other=as_bf16(jnp.logical_not(same)), pick_other=as_bf16(r == 2 * p + 1 - upper),
                swap=as_bf16(((r == 0) & (l >= dh)) | ((r == dh) & (l < dh))))


def _pair_rstd(x2, sel, dh):
    return lax.rsqrt(_split_dot(x2 * x2, sel["own"], 2) * (1.0 / dh) + EPS)


def _pair_aug(cols, n, dh):
    lane = _lane((n, 128))
    li = jnp.where(lane >= dh, lane - dh, lane)
    out = jnp.zeros((n, 128), F32)
    for i, c in enumerate(cols):
        out = jnp.where(li == i, c, out)
    return out


def _rstd_mxu(xf):
    return lax.rsqrt(_rowsum(xf * xf, xf.shape[1]) * (1.0 / xf.shape[1]) + EPS)


def _rms_bwd_mxu(xf, r, dyn):
    return r * dyn - xf * (r * r * r) * (_rowsum(dyn * xf, xf.shape[1]) * (1.0 / xf.shape[1]))


def _pick_lane_mxu(tile, idx, width):
    onehot = (_row((tile.shape[1], width)) == idx).astype(BF16)
    total, rest = None, tile
    for _ in range(3):
        part = rest.astype(BF16)
        rest = rest - part.astype(F32)
        total = _dot(part, onehot) if total is None else total + _dot(part, onehot)
    return total


def _split3(x):
    hi = x.astype(BF16).astype(F32)
    mid = (x - hi).astype(BF16).astype(F32)
    return [hi, mid, (x - hi - mid).astype(BF16).astype(F32)]


def _aug_cols(cols, n, width):
    lane = _lane((n, width))
    out = jnp.zeros((n, width), F32)
    for i, c in enumerate(cols):
        out = jnp.where(lane == i, c, out)
    return out


def _once(shape, index_map):
    return pl.BlockSpec(shape, index_map, pipeline_mode=pl.Buffered(1))


def attn_fwd(proj, gates, qw, kw, seq, tq=256):
    t = proj.shape[0]
    nb, nq, dh = t // seq, seq // tq, FOX_DH
    scale = dh ** -0.5

    def body(q_ref, k_ref, v_ref, g_ref, qw_ref, kw_ref, y_ref, lse_ref, qs, ks, vs):
        p = pl.program_id(1)
        heads = range(2)
        low = _lane((tq, 128)) < dh
        sel = _pair_mats(p, dh)

        def prep(i, _):
            r = _rows(i, tq)
            q2, k2 = q_ref[r, :], k_ref[r, :]
            cc = _split_dot(g_ref[r, :], sel["pick_other"], 3) * LOG2E
            qn = q2 * _pair_rstd(q2, sel, dh) * qw_ref[...] * (scale * LOG2E)
            kn = k2 * _pair_rstd(k2, sel, dh) * kw_ref[...]
            qx = _pair_aug(_split3(cc) + [1.0, 1.0, 1.0], tq, dh)
            kx = _pair_aug([1.0, 1.0, 1.0] + _split3(-cc), tq, dh)
            for hh in heads:
                own = low if hh == 0 else jnp.logical_not(low)
                qs[hh, r, :] = jnp.where(own, qn, qx).astype(BF16)
                ks[hh, r, :] = jnp.where(own, kn, kx).astype(BF16)
            vs[r, :] = v_ref[r, :].astype(BF16)
            return 0

        lax.fori_loop(0, nq, prep, 0)

        def q_tile(i, _):
            r = _rows(i, tq)
            qt = [qs[hh, r, :] for hh in heads]

            def kv_step(j, carry, masked):
                kr = _rows(j, tq)
                vt = vs[kr, :]
                out = []
                for hh in heads:
                    m, l, acc = carry[hh]
                    s = _dot_nt(qt[hh], ks[hh, kr, :])
                    if masked:
                        s = jnp.where(_row((tq, tq)) >= _lane((tq, tq)), s, NEG)
                    m_new = jnp.maximum(m, jnp.max(s, axis=1, keepdims=True))
                    pe = jnp.exp2(s - m_new)
                    a = jnp.exp2(m - m_new)
                    out.append((m_new, a * l + jnp.sum(pe, axis=1, keepdims=True), a * acc + _dot(pe.astype(BF16), vt)))
                return tuple(out)

            one = (jnp.full((tq, 1), NEG, F32), jnp.zeros((tq, 1), F32), jnp.zeros((tq, 128), F32))
            carry = lax.fori_loop(0, i, lambda j, c: kv_step(j, c, False), (one, one))
            (m0, l0, acc0), (m1, l1, acc1) = kv_step(i, carry, True)
            y_ref[r, :] = jnp.where(low, acc0 / l0, acc1 / l1)
            lse_ref[r, :] = jnp.where(low, m0 + jnp.log2(l0), m1 + jnp.log2(l1))
            return 0

        lax.fori_loop(0, nq, q_tile, 0)

    blk = lambda off: _once((seq, 128), lambda b, p: (b, off + p))
    return pl.pallas_call(
        body, grid=(nb, 4),
        in_specs=[blk(0), blk(4), blk(8), _once((seq, 128), lambda b, p: (b, 0)),
                  pl.BlockSpec((1, 128), lambda b, p: (0, 0)), pl.BlockSpec((1, 128), lambda b, p: (0, 0))],
        out_specs=[pl.BlockSpec((seq, 128), lambda b, p: (b, p)), pl.BlockSpec((seq, 128), lambda b, p: (b, p))],
        out_shape=[jax.ShapeDtypeStruct((t, 512), F32), jax.ShapeDtypeStruct((t, 512), F32)],
        scratch_shapes=[pltpu.VMEM((2, seq, 128), BF16), pltpu.VMEM((2, seq, 128), BF16), pltpu.VMEM((seq, 128), BF16)],
        compiler_params=_cp("parallel", "arbitrary"),
        name="attn_fwd")(proj, proj, proj, gates, jnp.tile(qw, (1, 2)), jnp.tile(kw, (1, 2)))


def attn_bwd(proj, gates, qw, kw, y, lse, dy, seq, tq=256):
    t = proj.shape[0]
    nb, nq, dh = t // seq, seq // tq, FOX_DH
    scale = dh ** -0.5

    def body(q_ref, k_ref, v_ref, g_ref, qw_ref, kw_ref, y_ref, lse_ref, dy_ref,
             dq_ref, dk_ref, dv_ref, dg_ref, dqw_ref, dkw_ref,
             qs, ks, vs, dos, dsrow, dqa, dka):
        b, p = pl.program_id(0), pl.program_id(1)

        @pl.when((b == 0) & (p == 0))
        def _():
            dqw_ref[...] = jnp.zeros_like(dqw_ref)
            dkw_ref[...] = jnp.zeros_like(dkw_ref)

        @pl.when(p == 0)
        def _():
            dg_ref[...] = jnp.zeros_like(dg_ref)

        heads = range(2)
        low = _lane((tq, 128)) < dh
        sel = _pair_mats(p, dh)

        def prep(i, _):
            r = _rows(i, tq)
            q2, k2, dy2 = q_ref[r, :], k_ref[r, :], dy_ref[r, :]
            cc = _split_dot(g_ref[r, :], sel["pick_other"], 3) * LOG2E
            lse_x = _split_dot(lse_ref[r, :], sel["swap"], 3)
            delta_x = _split_dot(dy2 * y_ref[r, :], sel["other"], 2)
            qn = q2 * _pair_rstd(q2, sel, dh) * qw_ref[...] * (scale * LOG2E)
            kn = k2 * _pair_rstd(k2, sel, dh) * kw_ref[...]
            qx = _pair_aug(_split3(cc) + [1.0, 1.0, 1.0] + _split3(-lse_x), tq, dh)
            kx = _pair_aug([1.0, 1.0, 1.0] + _split3(-cc) + [1.0, 1.0, 1.0], tq, dh)
            vx = _pair_aug([1.0, 1.0, 1.0], tq, dh)
            dx = _pair_aug(_split3(-delta_x), tq, dh)
            for hh in heads:
                own = low if hh == 0 else jnp.logical_not(low)
                qs[hh, r, :] = jnp.where(own, qn, qx).astype(BF16)
                ks[hh, r, :] = jnp.where(own, kn, kx).astype(BF16)
                vs[hh, r, :] = jnp.where(own, v_ref[r, :], vx).astype(BF16)
                dos[hh, r, :] = jnp.where(own, dy2, dx).astype(BF16)
                dsrow[hh, r, :] = jnp.zeros((tq, 1), F32)
                dqa[hh, r, :] = jnp.zeros((tq, 128), F32)
            return 0

        lax.fori_loop(0, nq, prep, 0)

        def kv_tile(j, _):
            kr = _rows(j, tq)
            kt = [ks[hh, kr, :] for hh in heads]
            vt = [vs[hh, kr, :] for hh in heads]

            def q_step(i, carry, masked):
                r = _rows(i, tq)
                out = []
                for hh in heads:
                    dk, dv, dcr = carry[hh]
                    qt, dot = qs[hh, r, :], dos[hh, r, :]
                    s = _dot_nt(qt, kt[hh])
                    if masked:
                        s = jnp.where(_row((tq, tq)) >= _lane((tq, tq)), s, NEG)
                    pe = jnp.exp2(s)
                    ds = pe * _dot_nt(dot, vt[hh])
                    dsb = ds.astype(BF16)
                    dqa[hh, r, :] += _dot(dsb, kt[hh])
                    dsrow[hh, r, :] += jnp.sum(ds, axis=1, keepdims=True)
                    out.append((dk + _dot_tn(dsb, qt), dv + _dot_tn(pe.astype(BF16), dot),
                                dcr - jnp.sum(ds, axis=0, keepdims=True)))
                return tuple(out)

            one = (jnp.zeros((tq, 128), F32), jnp.zeros((tq, 128), F32), jnp.zeros((1, tq), F32))
            carry = q_step(j, (one, one), True)
            (dk0, dv0, dcr0), (dk1, dv1, dcr1) = lax.fori_loop(j + 1, nq, lambda i, c: q_step(i, c, False), carry)
            dka[kr, :] = jnp.where(low, dk0, dk1)
            dv_ref[kr, :] = jnp.where(low, dv0, dv1).astype(BF16)
            lane = _lane((tq, 128))
            dg_ref[kr, :] = jnp.where(lane == 2 * p, _row_to_col(dcr0, tq),
                                      jnp.where(lane == 2 * p + 1, _row_to_col(dcr1, tq), dg_ref[kr, :]))
            return 0

        lax.fori_loop(0, nq, kv_tile, 0)

        def post(i, _):
            r = _rows(i, tq)
            q2, k2 = q_ref[r, :], k_ref[r, :]
            rq, rk = _pair_rstd(q2, sel, dh), _pair_rstd(k2, sel, dh)
            dqn = jnp.where(low, dqa[0, r, :], dqa[1, r, :]) * scale
            dkn = dka[r, :] * LN2
            dqw_ref[...] += jnp.sum(dqn * q2 * rq, axis=0, keepdims=True)
            dkw_ref[...] += jnp.sum(dkn * k2 * rk, axis=0, keepdims=True)
            for x2, rr, dyn, o_ref in ((q2, rq, dqn * qw_ref[...], dq_ref), (k2, rk, dkn * kw_ref[...], dk_ref)):
                mean = _split_dot(dyn * x2, sel["own"], 2) * (1.0 / dh)
                o_ref[r, :] = (rr * dyn - x2 * (rr * rr * rr) * mean).astype(BF16)
            lane = _lane((tq, 128))
            dg_ref[r, :] += jnp.where(lane == 2 * p, dsrow[0, r, :], jnp.where(lane == 2 * p + 1, dsrow[1, r, :], 0.0))
            return 0

        lax.fori_loop(0, nq, post, 0)

    blk = lambda off: _once((seq, 128), lambda b, p: (b, off + p))
    own = lambda: _once((seq, 128), lambda b, p: (b, p))
    vec = lambda: pl.BlockSpec((1, 128), lambda b, p: (0, 0))
    res = pl.pallas_call(
        body, grid=(nb, 4),
        in_specs=[blk(0), blk(4), blk(8), _once((seq, 128), lambda b, p: (b, 0)), vec(), vec(), own(), own(), own()],
        out_specs=[own(), own(), own(), _once((seq, 128), lambda b, p: (b, 0)), vec(), vec()],
        out_shape=[jax.ShapeDtypeStruct((t, 512), BF16)] * 3
                  + [jax.ShapeDtypeStruct((t, 128), F32), jax.ShapeDtypeStruct((1, 128), F32), jax.ShapeDtypeStruct((1, 128), F32)],
        scratch_shapes=[pltpu.VMEM((2, seq, 128), BF16)] * 4
                       + [pltpu.VMEM((2, seq, 1), F32), pltpu.VMEM((2, seq, 128), F32), pltpu.VMEM((seq, 128), F32)],
        compiler_params=_cp("arbitrary", "arbitrary"),
        name="attn_bwd")(proj, proj, proj, gates, jnp.tile(qw, (1, 2)), jnp.tile(kw, (1, 2)), y, lse, dy)
    return list(res[:4]) + [res[4][:, :dh] + res[4][:, dh:], res[5][:, :dh] + res[5][:, dh:]]


def _silu_grad(c, sg):
    return sg * (1.0 + c * (1.0 - sg))


def _conv(x, w, n):
    row = _row(x.shape)
    c = x * w[CONV_W - 1:CONV_W, :]
    for k in range(CONV_W - 1):
        sh = CONV_W - 1 - k
        c = c + w[k:k + 1, :] * jnp.where(row >= sh, pltpu.roll(x, sh, 0), 0.0)
    return c


def gdn_pre_fwd(proj, cw, seq):
    t = proj.shape[0]
    nb = t // seq
    scale = GDN_DH ** -0.5

    def body(xq_ref, xk_ref, xv_ref, wq_ref, wk_ref, wv_ref, q_ref, k_ref, v_ref):
        def act(x_ref, w_ref):
            c = _conv(x_ref[...], w_ref[...], seq)
            return c * jax.nn.sigmoid(c)

        aq, ak = act(xq_ref, wq_ref), act(xk_ref, wk_ref)
        q_ref[...] = aq * lax.rsqrt(jnp.sum(aq * aq, axis=1, keepdims=True) + EPS) * scale
        k_ref[...] = ak * lax.rsqrt(jnp.sum(ak * ak, axis=1, keepdims=True) + EPS)
        v_ref[...] = act(xv_ref, wv_ref)

    xb = lambda off: pl.BlockSpec((seq, 128), lambda b, h: (b, off + h))
    wb = lambda off: pl.BlockSpec((CONV_W, 128), lambda b, h: (0, off + h))
    ob = lambda: pl.BlockSpec((seq, 128), lambda b, h: (b, h))
    return pl.pallas_call(
        body, grid=(nb, GDN_HEADS),
        in_specs=[xb(12), xb(16), xb(20), wb(0), wb(4), wb(8)],
        out_specs=[ob(), ob(), ob()],
        out_shape=[jax.ShapeDtypeStruct((t, 512), F32)] * 3,
        compiler_params=_cp("parallel", "parallel"), name="gdn_pre_fwd")(proj, proj, proj, cw, cw, cw)


def gdn_pre_bwd(proj, cw, dq, dk, dv, seq):
    t = proj.shape[0]
    nb = t // seq
    scale = GDN_DH ** -0.5

    def body(xq_ref, xk_ref, xv_ref, wq_ref, wk_ref, wv_ref, dq_ref, dk_ref, dv_ref,
             dxq_ref, dxk_ref, dxv_ref, dwq_ref, dwk_ref, dwv_ref):
        first = pl.program_id(1) == 0
        row = _row((seq, 128))

        def one(x_ref, w_ref, dy_ref, dx_ref, dw_ref, norm, sc):
            x, w = x_ref[...], w_ref[...]
            c = _conv(x, w, seq)
            sg = jax.nn.sigmoid(c)
            dy = dy_ref[...]
            if norm:
                a = c * sg
                rs = lax.rsqrt(jnp.sum(a * a, axis=1, keepdims=True) + EPS)
                dy = dy * sc
                da = rs * dy - a * (rs * rs * rs) * jnp.sum(dy * a, axis=1, keepdims=True)
            else:
                da = dy
            dc = da * _silu_grad(c, sg)
            dx = dc * w[CONV_W - 1:CONV_W, :]
            dws = [None] * CONV_W
            dws[CONV_W - 1] = jnp.sum(dc * x, axis=0, keepdims=True)
            for k in range(CONV_W - 1):
                sh = CONV_W - 1 - k
                dc_up = jnp.where(row < seq - sh, pltpu.roll(dc, seq - sh, 0), 0.0)
                dx = dx + w[k:k + 1, :] * dc_up
                dws[k] = jnp.sum(dc_up * x, axis=0, keepdims=True)
            dx_ref[...] = dx.astype(BF16)
            dwn = jnp.concatenate(dws, axis=0)

            @pl.when(first)
            def _():
                dw_ref[...] = dwn

            @pl.when(jnp.logical_not(first))
            def _():
                dw_ref[...] += dwn

        one(xq_ref, wq_ref, dq_ref, dxq_ref, dwq_ref, True, scale)
        one(xk_ref, wk_ref, dk_ref, dxk_ref, dwk_ref, True, 1.0)
        one(xv_ref, wv_ref, dv_ref, dxv_ref, dwv_ref, False, 1.0)

    xb = lambda off: pl.BlockSpec((seq, 128), lambda h, b: (b, off + h))
    wb = lambda off: pl.BlockSpec((CONV_W, 128), lambda h, b: (0, off + h))
    ob = lambda: pl.BlockSpec((seq, 128), lambda h, b: (b, h))
    return pl.pallas_call(
        body, grid=(GDN_HEADS, nb),
        in_specs=[xb(12), xb(16), xb(20), wb(0), wb(4), wb(8), ob(), ob(), ob()],
        out_specs=[ob(), ob(), ob()] + [pl.BlockSpec((CONV_W, 128), lambda h, b: (0, h))] * 3,
        out_shape=[jax.ShapeDtypeStruct((t, 512), BF16)] * 3 + [jax.ShapeDtypeStruct((CONV_W, 512), F32)] * 3,
        compiler_params=_cp("parallel", "arbitrary"), name="gdn_pre_bwd")(proj, proj, proj, cw, cw, cw, dq, dk, dv)


def _b16(x):
    return x.astype(BF16)


@jax.custom_vjp
def _mm(a, b):
    return _dot(_b16(a), _b16(b))


_mm.defvjp(lambda a, b: (_mm(a, b), (a, b)),
           lambda res, g: (_dot_nt(_b16(g), _b16(res[1])), _dot_tn(_b16(res[0]), _b16(g))))


@jax.custom_vjp
def _mm_nt(a, b):
    return _dot_nt(_b16(a), _b16(b))


_mm_nt.defvjp(lambda a, b: (_mm_nt(a, b), (a, b)),
              lambda res, g: (_dot(_b16(g), _b16(res[1])), _dot_tn(_b16(g), _b16(res[0]))))


@jax.custom_vjp
def _mm_tn(a, b):
    return _dot_tn(_b16(a), _b16(b))


_mm_tn.defvjp(lambda a, b: (_mm_tn(a, b), (a, b)),
              lambda res, g: (_dot_nt(_b16(res[1]), _b16(g)), _dot(_b16(res[0]), _b16(g))))


def _dot32(a, b, dims=(((1,), (0,)), ((), ()))):
    def split(x):
        hi = x.astype(BF16)
        return hi, (x - hi.astype(F32)).astype(BF16)

    (ah, al), (bh, bl) = split(a), split(b)
    d = lambda x, y: lax.dot_general(x, y, dims, preferred_element_type=F32)
    return d(ah, bh) + (d(ah, bl) + d(al, bh))


def _inv_fwd_many(mats):
    n = mats[0].shape[0]
    eye = (_row((n, n)) == _lane((n, n))).astype(F32)
    invs, pws = [eye - a for a in mats], list(mats)
    for _ in range(n.bit_length() - 2):
        pws = [_dot32(pw, pw) for pw in pws]
        invs = [inv + _dot32(inv, pw) for inv, pw in zip(invs, pws)]
    return invs


@jax.custom_vjp
def _inv_saved(a, inv):
    return inv


def _inv_saved_bwd(inv, g):
    tg = _dot32(inv, g, (((0,), (0,)), ((), ())))
    return -_dot32(tg, inv, (((1,), (1,)), ((), ()))), jnp.zeros_like(inv)


_inv_saved.defvjp(lambda a, inv: (inv, inv), _inv_saved_bwd)


def _gdn_decay(gcol):
    c = CHUNK
    ri, ci = _row((c, c)), _lane((c, c))
    incl, eye = ri >= ci, ri == ci
    grow = jnp.sum(jnp.where(eye, gcol, 0.0), axis=0, keepdims=True)
    gc = jnp.sum(jnp.where(incl, grow, 0.0), axis=1, keepdims=True)
    gcr = jnp.sum(jnp.where(eye, gc, 0.0), axis=0, keepdims=True)
    gl = jnp.sum(jnp.where(_row((c, 1)) == c - 1, gc, 0.0), axis=0, keepdims=True)
    return gc, gl, jnp.exp(jnp.where(incl, gc - gcr, NEG))


def _gdn_a(k, bcol, decay):
    c = CHUNK
    return jnp.where(_row((c, c)) > _lane((c, c)), _mm_nt(k * bcol, k) * decay, 0.0)


def _gdn_chunk(q, k, v, gcol, bcol, state, gg, nw, inv_saved):
    c = CHUNK
    incl = _row((c, c)) >= _lane((c, c))
    gc, gl, decay = _gdn_decay(gcol)
    kb, vb = k * bcol, v * bcol
    inv = _inv_saved(_gdn_a(k, bcol, decay), inv_saved)
    eg = jnp.exp(gc)
    u = _mm(inv, vb)
    w = _mm(inv, kb * eg)
    pm = jnp.where(incl, _mm_nt(q, k) * decay, 0.0)
    kd = k * jnp.exp(gl - gc)
    qd = q * eg
    v_new = u - _mm(w, state)
    o = _mm(qd, state) + _mm(pm, v_new)
    state_new = state * jnp.exp(gl) + _mm_tn(kd, v_new)
    y = o * _rstd(o) * nw * (gg * jax.nn.sigmoid(gg))
    return y, state_new


_gdn_chunks = jax.vmap(_gdn_chunk, in_axes=(0, 0, 0, 0, 0, 0, 0, None, 0))


def _gdn_chain_inputs(chains, p, r, c, q_ref, k_ref, v_ref, g_ref, gg_ref, inv_ref):
    cols = {nm: [] for nm in ("q", "k", "v", "g", "b", "gg", "inv")}
    for b, hh in chains:
        h = GDN_HPS * p + hh
        ln = slice(hh * 128, (hh + 1) * 128)
        gt = g_ref[b, r, :]
        cols["q"].append(q_ref[b, r, ln])
        cols["k"].append(k_ref[b, r, ln])
        cols["v"].append(v_ref[b, r, ln])
        cols["g"].append(_pick_lane(gt, 8 + h))
        cols["b"].append(_pick_lane(gt, 12 + h))
        cols["gg"].append(gg_ref[b, r, ln])
        cols["inv"].append(inv_ref[b, hh, c])
    return [jnp.stack(cols[nm]) for nm in ("q", "k", "v", "g", "b", "gg", "inv")]


GDN_CB = 8
GDN_INV_CB = 16
GDN_HPS = 4


def gdn_inv(k, gates, seq):
    t = k.shape[0]
    nb, nc = t // seq, seq // CHUNK
    cb = min(GDN_INV_CB, nc)
    rb = cb * CHUNK
    nsb = seq // rb

    def body(k_ref, g_ref, o_ref):
        h = pl.program_id(1)
        mats = []
        for c in range(cb):
            r = slice(c * CHUNK, (c + 1) * CHUNK)
            gt = g_ref[r, :]
            _, _, decay = _gdn_decay(_pick_lane(gt, 8 + h))
            mats.append(_gdn_a(k_ref[r, :], _pick_lane(gt, 12 + h), decay))
        for c, inv in enumerate(_inv_fwd_many(mats)):
            o_ref[c] = inv

    return pl.pallas_call(
        body, grid=(nb, GDN_HEADS, nsb),
        in_specs=[pl.BlockSpec((rb, 128), lambda b, h, s: (b * nsb + s, h)),
                  pl.BlockSpec((rb, 128), lambda b, h, s: (b * nsb + s, 0))],
        out_specs=pl.BlockSpec((None, None, cb, CHUNK, CHUNK), lambda b, h, s: (b, h, s, 0, 0)),
        out_shape=jax.ShapeDtypeStruct((nb, GDN_HEADS, nc, CHUNK, CHUNK), F32),
        compiler_params=_cp("parallel", "parallel", "parallel"), name="gdn_inv")(k, gates)


def _gdn_specs(nb, nsb, cb, rev):
    blk = (lambda s: nsb - 1 - s) if rev else (lambda s: s)
    rb = cb * CHUNK
    pair = lambda off=0: pl.BlockSpec((nb, rb, 128 * GDN_HPS), lambda s, p: (0, blk(s), off + p))
    gate = lambda: pl.BlockSpec((nb, rb, 128), lambda s, p: (0, blk(s), 0))
    mats = lambda n: pl.BlockSpec((nb, GDN_HPS, cb, n, n), lambda s, p: (0, p, blk(s), 0, 0))
    return pair, gate, mats


def gdn_fwd(q, k, v, gates, proj, nw, inv, seq):
    t = q.shape[0]
    nb, nc = t // seq, seq // CHUNK
    cb = GDN_CB
    nsb = nc // cb
    chains = [(b, hh) for b in range(nb) for hh in range(GDN_HPS)]
    nch = len(chains)
    pair, gate, mats = _gdn_specs(nb, nsb, cb, False)

    def body(q_ref, k_ref, v_ref, g_ref, gg_ref, inv_ref, nw_ref, y_ref, st_ref, carry):
        s, p = pl.program_id(0), pl.program_id(1)

        @pl.when(s == 0)
        def _():
            for ci in range(nch):
                carry[p * nch + ci] = jnp.zeros((GDN_DH, GDN_DH), F32)

        def step(c, states):
            r = _rows(c, CHUNK)
            for ci, (b, hh) in enumerate(chains):
                st_ref[b, hh, c] = states[ci]
            ins = _gdn_chain_inputs(chains, p, r, c, q_ref, k_ref, v_ref, g_ref, gg_ref, inv_ref)
            y, states = _gdn_chunks(*ins[:5], states, ins[5], nw_ref[...], ins[6])
            for ci, (b, hh) in enumerate(chains):
                y_ref[b, r, hh * 128:(hh + 1) * 128] = y[ci]
            return states

        states = lax.fori_loop(0, cb, step, jnp.stack([carry[p * nch + ci] for ci in range(nch)]))
        for ci in range(nch):
            carry[p * nch + ci] = states[ci]

    v3 = lambda a: a.reshape(nb, seq, a.shape[1])
    y, st = pl.pallas_call(
        body, grid=(nsb, GDN_HEADS // GDN_HPS),
        in_specs=[pair(), pair(), pair(), gate(), pair(24 // GDN_HPS), mats(CHUNK), pl.BlockSpec((1, 128), lambda s, p: (0, 0))],
        out_specs=[pair(), mats(GDN_DH)],
        out_shape=[jax.ShapeDtypeStruct((nb, seq, 512), F32),
                   jax.ShapeDtypeStruct((nb, GDN_HEADS, nc, GDN_DH, GDN_DH), F32)],
        scratch_shapes=[pltpu.VMEM((GDN_HEADS // GDN_HPS * nch, GDN_DH, GDN_DH), F32)],
        compiler_params=_cp("arbitrary", "arbitrary"), name="gdn_fwd")(v3(q), v3(k), v3(v), v3(gates), v3(proj), inv, nw)
    return y.reshape(t, 512), st


def gdn_bwd(q, k, v, gates, proj, nw, inv, states, dy, seq):
    t = q.shape[0]
    nb, nc = t // seq, seq // CHUNK
    cb = GDN_CB // 2
    nsb = nc // cb
    chains = [(b, hh) for b in range(nb) for hh in range(GDN_HPS)]
    nch = len(chains)
    pair, gate, mats = _gdn_specs(nb, nsb, cb, True)

    def body(q_ref, k_ref, v_ref, g_ref, gg_ref, inv_ref, st_ref, dy_ref, nw_ref,
             dq_ref, dk_ref, dv_ref, dgg_ref, dg_ref, dnw_ref, carry):
        s, p = pl.program_id(0), pl.program_id(1)

        @pl.when((s == 0) & (p == 0))
        def _():
            dnw_ref[...] = jnp.zeros_like(dnw_ref)

        @pl.when(p == 0)
        def _():
            dg_ref[...] = jnp.zeros_like(dg_ref)

        @pl.when(s == 0)
        def _():
            for ci in range(nch):
                carry[p * nch + ci] = jnp.zeros((GDN_DH, GDN_DH), F32)

        def step(idx, dstates):
            c = cb - 1 - idx
            r = _rows(c, CHUNK)
            ins = _gdn_chain_inputs(chains, p, r, c, q_ref, k_ref, v_ref, g_ref, gg_ref, inv_ref)
            st = jnp.stack([st_ref[b, hh, c] for b, hh in chains])
            dy = jnp.stack([dy_ref[b, r, hh * 128:(hh + 1) * 128] for b, hh in chains])
            _, vjp = jax.vjp(_gdn_chunks, *ins[:5], st, ins[5], nw_ref[...], ins[6])
            dq, dk, dv, dgc, dbc, dstates, dgg, dnw, _ = vjp((dy, dstates))
            dnw_ref[...] += dnw
            lane = _lane((CHUNK, 128))
            for ci, (b, hh) in enumerate(chains):
                h = GDN_HPS * p + hh
                ln = slice(hh * 128, (hh + 1) * 128)
                dq_ref[b, r, ln] = dq[ci]
                dk_ref[b, r, ln] = dk[ci]
                dv_ref[b, r, ln] = dv[ci]
                dgg_ref[b, r, ln] = dgg[ci].astype(BF16)
                dg_ref[b, r, :] = jnp.where(lane == 8 + h, dgc[ci], jnp.where(lane == 12 + h, dbc[ci], dg_ref[b, r, :]))
            return dstates

        dstates = lax.fori_loop(0, cb, step, jnp.stack([carry[p * nch + ci] for ci in range(nch)]))
        for ci in range(nch):
            carry[p * nch + ci] = dstates[ci]

    v3 = lambda a: a.reshape(nb, seq, a.shape[1])
    res = pl.pallas_call(
        body, grid=(nsb, GDN_HEADS // GDN_HPS),
        in_specs=[pair(), pair(), pair(), gate(), pair(24 // GDN_HPS), mats(CHUNK), mats(GDN_DH), pair(),
                  pl.BlockSpec((1, 128), lambda s, p: (0, 0))],
        out_specs=[pair(), pair(), pair(), pair(), gate(), pl.BlockSpec((1, 128), lambda s, p: (0, 0))],
        out_shape=[jax.ShapeDtypeStruct((nb, seq, 512), F32)] * 3 + [jax.ShapeDtypeStruct((nb, seq, 512), BF16)]
                  + [jax.ShapeDtypeStruct((nb, seq, 128), F32), jax.ShapeDtypeStruct((1, 128), F32)],
        scratch_shapes=[pltpu.VMEM((GDN_HEADS // GDN_HPS * nch, GDN_DH, GDN_DH), F32)],
        compiler_params=_cp("arbitrary", "arbitrary"),
        name="gdn_bwd")(v3(q), v3(k), v3(v), v3(gates), v3(proj), inv, states, v3(dy), nw)
    return [a.reshape(t, a.shape[2]) for a in res[:5]] + [res[5]]


def loss_head(y, target, tm=512):
    t, d = y.shape

    def body(y_ref, t_ref, s_ref, dy_ref):
        @pl.when(pl.program_id(0) == 0)
        def _():
            s_ref[...] = jnp.zeros_like(s_ref)

        err = y_ref[...] - t_ref[...]
        s_ref[...] += jnp.sum(err * err, axis=0, keepdims=True)
        dy_ref[...] = err * (1.0 / d)

    return pl.pallas_call(
        body, grid=(t // tm,),
        in_specs=[pl.BlockSpec((tm, d), lambda i: (i, 0)), pl.BlockSpec((tm, d), lambda i: (i, 0))],
        out_specs=[pl.BlockSpec((1, d), lambda i: (0, 0)), pl.BlockSpec((tm, d), lambda i: (i, 0))],
        out_shape=[jax.ShapeDtypeStruct((1, d), F32), jax.ShapeDtypeStruct((t, d), F32)],
        compiler_params=_cp("arbitrary"), name="loss_head")(y, target)


def _place():
    return lax.axis_index("x"), lax.axis_index("y"), lax.axis_index("c")


def _peer(k):
    x, y, c = _place()
    px = 1 - x if (k >> 2) & 1 else x
    py = 1 - y if (k >> 1) & 1 else y
    pc = 1 - c if k & 1 else c
    return (px, py, pc), 4 * px + 2 * py + pc


_ANY = pl.BlockSpec(memory_space=pl.ANY)
_SEM = pl.BlockSpec(memory_space=pltpu.SEMAPHORE)
_EFFECT = pltpu.SideEffectType.DATAFLOW_SIDE_EFFECTING


def _me():
    x, y, c = _place()
    return 4 * x + 2 * y + c


def _remote_copy(ins, lands, scatter, send_sems, recv_sems, a, k, arriving):
    pid, pidx = _peer(k)
    return pltpu.make_async_remote_copy(src_ref=ins[a].at[pidx] if scatter[a] else ins[a],
                                        dst_ref=lands[a].at[pidx if arriving else _me()],
                                        send_sem=send_sems.at[a * N_DEV + k], recv_sem=recv_sems.at[a * N_DEV + k],
                                        device_id=pid, device_id_type=MESH)


def _local_copy(ins, lands, scatter, loc_sems, a):
    me = _me()
    return pltpu.make_async_copy(ins[a].at[me] if scatter[a] else ins[a], lands[a].at[me], loc_sems.at[a])


def exchange_start(arrays, scatter, name, after):
    n = len(arrays)
    lands = [lax.empty(a.shape if s else (N_DEV,) + a.shape, a.dtype) for a, s in zip(arrays, scatter)]

    def body(*refs):
        ins, lds = refs[:n], refs[n:2 * n]
        send_sems, recv_sems, loc_sems = refs[2 * n + 1:2 * n + 4]
        token = refs[-1]
        for k in range(1, N_DEV):
            for a in range(n):
                _remote_copy(ins, lds, scatter, send_sems, recv_sems, a, k, False).start()
        for a in range(n):
            _local_copy(ins, lds, scatter, loc_sems, a).start()
        token[...] = jnp.zeros_like(token)

    hbm = lambda a: pltpu.HBM(a.shape, a.dtype)
    res = pl.pallas_call(
        body, name=name,
        in_specs=[_ANY] * (2 * n + 1),
        out_specs=[_SEM, _SEM, _SEM] + [_ANY] * (2 * n) + [pl.BlockSpec(memory_space=pltpu.VMEM)],
        out_shape=[pltpu.SemaphoreType.DMA((n * N_DEV,)), pltpu.SemaphoreType.DMA((n * N_DEV,)),
                   pltpu.SemaphoreType.DMA((n,))]
                  + [hbm(a) for a in arrays] + [hbm(a) for a in lands] + [jax.ShapeDtypeStruct((8, 128), F32)],
        input_output_aliases={i: 3 + i for i in range(2 * n)},
        compiler_params=pltpu.CompilerParams(has_side_effects=_EFFECT),
    )(*[pltpu.with_memory_space_constraint(a, pltpu.HBM) for a in list(arrays) + lands], after)
    return res[0:3], res[3:3 + n], res[3 + n:3 + 2 * n], res[-1]


def exchange_wait(sems, arrays, lands, scatter, after, name):
    n = len(arrays)

    def body(*refs):
        ins, lds = refs[:n], refs[n:2 * n]
        ssem, rsem, lsem = refs[2 * n:2 * n + 3]
        for a in range(n):
            _local_copy(ins, lds, scatter, lsem, a).wait()
        for k in range(1, N_DEV):
            for a in range(n):
                _remote_copy(ins, lds, scatter, ssem, rsem, a, k, True).wait_recv()
        for k in range(1, N_DEV):
            for a in range(n):
                _remote_copy(ins, lds, scatter, ssem, rsem, a, k, False).wait_send()

    hbm = lambda a: pltpu.HBM(a.shape, a.dtype)
    res = pl.pallas_call(
        body, name=name,
        in_specs=[_ANY] * (2 * n) + [_SEM, _SEM, _SEM, _ANY],
        out_specs=[_ANY] * (2 * n),
        out_shape=[hbm(a) for a in arrays] + [hbm(a) for a in lands],
        input_output_aliases={i: i for i in range(2 * n)},
        compiler_params=pltpu.CompilerParams(has_side_effects=_EFFECT),
    )(*arrays, *lands, *sems, after)
    return list(res[n:])


def exchange_begin(arrays, scatter, name, after):
    sems, arrays_thru, lands_thru, token = exchange_start(arrays, scatter, name + "_start", after)
    return (sems, arrays_thru, lands_thru, scatter, name), token


def exchange_end(state, after):
    sems, arrays_thru, lands_thru, scatter, name = state
    return exchange_wait(sems, arrays_thru, lands_thru, scatter, after, name + "_wait")


def adamw_reduce(slots, w, m, v, l, name, after=None, prev=None):
    nl, r, c = w.shape
    tr = r
    while tr * c * 4 > (1 << 20) and tr % 16 == 0:
        tr //= 2
    bc1 = 1.0 - ADAM_B1 ** ADAM_STEP
    bc2 = 1.0 - ADAM_B2 ** ADAM_STEP

    def body(s_ref, w_ref, m_ref, v_ref, *rest):
        g_ref, d_ref, nm_ref, nv_ref = rest[-4:]
        g = s_ref[0].astype(F32)
        for j in range(1, N_DEV):
            g = g + s_ref[j].astype(F32)
        nm = ADAM_B1 * m_ref[...] + (1.0 - ADAM_B1) * g
        nv = ADAM_B2 * v_ref[...] + (1.0 - ADAM_B2) * (g * g)
        g_ref[...] = g
        nm_ref[...] = nm
        nv_ref[...] = nv
        d_ref[...] = -ADAM_LR * ((nm / bc1) / (jnp.sqrt(nv / bc2) + ADAM_EPS) + ADAM_WD * w_ref[...])

    blk = lambda: pl.BlockSpec((None, tr, c), lambda i: (l, i, 0))
    extra = ([] if after is None else [after]) + ([] if prev is None else list(prev))
    first_prev = 4 + (after is not None)
    return pl.pallas_call(
        body, grid=(r // tr,),
        in_specs=[pl.BlockSpec((N_DEV, tr, c), lambda i: (0, i, 0)), blk(), blk(), blk()] + [_ANY] * len(extra),
        out_specs=[blk(), blk(), blk(), blk()],
        out_shape=[jax.ShapeDtypeStruct((nl, r, c), F32)] * 4,
        input_output_aliases={} if prev is None else {first_prev + j: j for j in range(4)},
        compiler_params=_cp("parallel"), name=name)(slots, w, m, v, *extra)


BIG = ("ffn1_w_in", "ffn1_w_out", "w_in", "gdn_conv", "w_out", "ffn2_w_in", "ffn2_w_out")
GROUPS = (BIG[0:2], BIG[2:5], BIG[5:7])
SMALL = ("ffn1_norm", "mix_norm", "fox_q_norm", "fox_k_norm", "fox_f_bias", "gdn_a_log", "gdn_dt_bias",
         "gdn_out_norm", "ffn2_norm")
WEIGHTS = ("ffn1_norm", "ffn1_w_in", "ffn1_w_out", "mix_norm", "w_in", "fox_q_norm", "fox_k_norm", "fox_f_bias",
           "gdn_conv", "gdn_a_log", "gdn_dt_bias", "gdn_out_norm", "w_out", "ffn2_norm", "ffn2_w_in", "ffn2_w_out")
IN_COLS = (("fq", 512), ("fk", 512), ("fv", 512), ("ff", 8), ("gq", 512), ("gk", 512), ("gv", 512),
           ("ga", 4), ("gb", 4), ("gg", 512))
MY_BIG = ("fq", "fk", "fv", "gq", "gk", "gv", "gg")
MY_SMALL = ("ff", "ga", "gb")
SMALL_ROWS = 8 * 128


def _in_cols_to_mine(w):
    off, parts = 0, {}
    for nm, wd in IN_COLS:
        parts[nm] = w[:, off:off + wd]
        off += wd
    small = jnp.concatenate([parts[nm] for nm in MY_SMALL], axis=1)
    small = jnp.pad(small, ((0, 0), (0, 128 - small.shape[1])))
    return jnp.concatenate([parts[nm] for nm in MY_BIG] + [small], axis=1)


def _in_cols_from_mine(g):
    parts = {nm: g[:, i * 512:(i + 1) * 512] for i, nm in enumerate(MY_BIG)}
    off = N_BIG
    for nm in MY_SMALL:
        wd = dict(IN_COLS)[nm]
        parts[nm] = g[:, off:off + wd]
        off += wd
    return jnp.concatenate([parts[nm] for nm, _ in IN_COLS], axis=1)


def _pack_small(vals):
    rows = []
    nl = vals[SMALL[0]].shape[0]
    for l in range(nl):
        for nm in SMALL:
            v = vals[nm][l].reshape(-1)
            pad = (-v.shape[0]) % SMALL_ROWS
            rows.append(jnp.pad(v, (0, pad)).reshape(-1, 128))
    return jnp.concatenate(rows, axis=0)


def _unpack_small(packed, like):
    out = {nm: [] for nm in SMALL}
    row = 0
    nl = like[SMALL[0]].shape[0]
    for l in range(nl):
        for nm in SMALL:
            n = like[nm].shape[1]
            nr = -(-n // SMALL_ROWS) * 8
            out[nm].append(packed[row:row + nr].reshape(-1)[:n])
            row += nr
    return {nm: jnp.stack(v) for nm, v in out.items()}


def kernel(x, ffn1_norm, ffn1_w_in, ffn1_w_out, mix_norm, w_in, fox_q_norm, fox_k_norm, fox_f_bias, gdn_conv, gdn_a_log, gdn_dt_bias, gdn_out_norm, w_out, ffn2_norm, ffn2_w_in, ffn2_w_out, loss_target, m_ffn1_norm, m_ffn1_w_in, m_ffn1_w_out, m_mix_norm, m_w_in, m_fox_q_norm, m_fox_k_norm, m_fox_f_bias, m_gdn_conv, m_gdn_a_log, m_gdn_dt_bias, m_gdn_out_norm, m_w_out, m_ffn2_norm, m_ffn2_w_in, m_ffn2_w_out, v_ffn1_norm, v_ffn1_w_in, v_ffn1_w_out, v_mix_norm, v_w_in, v_fox_q_norm, v_fox_k_norm, v_fox_f_bias, v_gdn_conv, v_gdn_a_log, v_gdn_dt_bias, v_gdn_out_norm, v_w_out, v_ffn2_norm, v_ffn2_w_in, v_ffn2_w_out):
    wts = dict(ffn1_norm=ffn1_norm, ffn1_w_in=ffn1_w_in, ffn1_w_out=ffn1_w_out, mix_norm=mix_norm, w_in=w_in,
               fox_q_norm=fox_q_norm, fox_k_norm=fox_k_norm, fox_f_bias=fox_f_bias, gdn_conv=gdn_conv,
               gdn_a_log=gdn_a_log, gdn_dt_bias=gdn_dt_bias, gdn_out_norm=gdn_out_norm, w_out=w_out,
               ffn2_norm=ffn2_norm, ffn2_w_in=ffn2_w_in, ffn2_w_out=ffn2_w_out)
    mom = dict(ffn1_norm=m_ffn1_norm, ffn1_w_in=m_ffn1_w_in, ffn1_w_out=m_ffn1_w_out, mix_norm=m_mix_norm, w_in=m_w_in,
               fox_q_norm=m_fox_q_norm, fox_k_norm=m_fox_k_norm, fox_f_bias=m_fox_f_bias, gdn_conv=m_gdn_conv,
               gdn_a_log=m_gdn_a_log, gdn_dt_bias=m_gdn_dt_bias, gdn_out_norm=m_gdn_out_norm, w_out=m_w_out,
               ffn2_norm=m_ffn2_norm, ffn2_w_in=m_ffn2_w_in, ffn2_w_out=m_ffn2_w_out)
    var = dict(ffn1_norm=v_ffn1_norm, ffn1_w_in=v_ffn1_w_in, ffn1_w_out=v_ffn1_w_out, mix_norm=v_mix_norm, w_in=v_w_in,
               fox_q_norm=v_fox_q_norm, fox_k_norm=v_fox_k_norm, fox_f_bias=v_fox_f_bias, gdn_conv=v_gdn_conv,
               gdn_a_log=v_gdn_a_log, gdn_dt_bias=v_gdn_dt_bias, gdn_out_norm=v_gdn_out_norm, w_out=v_w_out,
               ffn2_norm=v_ffn2_norm, ffn2_w_in=v_ffn2_w_in, ffn2_w_out=v_ffn2_w_out)
    nb, seq, d = x.shape
    t = nb * seq
    depth = ffn1_norm.shape[0]

    stages = [(l, gi) for l in range(depth) for gi in range(len(GROUPS))]

    def shards_of(l, gi):
        return [wts[nm][l] if nm == "gdn_conv" else wts[nm][l].astype(BF16) for nm in GROUPS[gi]]

    def behind(nw, token):
        return nw if token is None else nw + token[0:1, 0:1]

    def small_params(l):
        return dict(
            n1=ffn1_norm[l][None], nmix=mix_norm[l][None], n2=ffn2_norm[l][None],
            qw=fox_q_norm[l][None], kw=fox_k_norm[l][None], onw=gdn_out_norm[l][None],
            gp=jnp.concatenate([
                jnp.concatenate([fox_f_bias[l], gdn_dt_bias[l], jnp.zeros((116,), F32)])[None],
                jnp.concatenate([jnp.zeros((8,), F32), gdn_a_log[l], jnp.zeros((116,), F32)])[None],
                jnp.zeros((6, 128), F32)], axis=0))

    h = x.reshape(t, d)
    state, token = exchange_begin(shards_of(0, 0), [False] * len(GROUPS[0]), "gather_0", ffn1_norm)
    landed = exchange_end(state, token)
    saved = [dict(p=small_params(l)) for l in range(depth)]
    for k, (l, gi) in enumerate(stages):
        s, w, token = saved[l], landed, None
        p = s["p"]
        if k + 1 < len(stages):
            nl, ng = stages[k + 1]
            state, token = exchange_begin(shards_of(nl, ng), [False] * len(GROUPS[ng]), f"gather_{k + 1}", landed[0])
        if gi == 0:
            fb = w[0].shape[2]
            p["w1i"], p["w1o"] = w[0].reshape(2, 4, d, fb), w[1].reshape(4, fb, d)
            s["x0"] = h
            h, *s["ffn1"] = ffn_fwd(h, behind(p["n1"], token), p["w1i"], p["w1o"])
            s["x1"] = h
        elif gi == 1:
            p["wi"] = _in_cols_to_mine(w[0].transpose(1, 0, 2).reshape(d, -1))
            p["cw"] = w[1].transpose(1, 0, 2).reshape(CONV_W, -1)
            p["wo"] = w[2].reshape(d, d)
            proj, hn = inproj_fwd(h, behind(p["nmix"], token), p["wi"])
            gates = gates_fwd(proj, p["gp"], seq)
            yf, lse = attn_fwd(proj, gates, p["qw"], p["kw"], seq, tq=min(seq, ATTN_TQ_FWD))
            qh, kh, vh = gdn_pre_fwd(proj, p["cw"], seq)
            inv = gdn_inv(kh, gates, seq)
            yg, st = gdn_fwd(qh, kh, vh, gates, proj, p["onw"], inv, seq)
            h, ycat = outproj_fwd(h, yf, yg, p["wo"])
            s.update(x2=h, proj=proj, hn=hn, gates=gates, yf=yf, lse=lse, qh=qh, kh=kh, vh=vh, st=st, inv=inv, ycat=ycat)
        else:
            fb = w[0].shape[2]
            p["w2i"], p["w2o"] = w[0].reshape(2, 4, d, fb), w[1].reshape(4, fb, d)
            h, *s["ffn2"] = ffn_fwd(h, behind(p["n2"], token), p["w2i"], p["w2o"])
        if k + 1 < len(stages):
            landed = exchange_end(state, h)

    sq, dh = loss_head(h, loss_target.reshape(t, d))
    loss = lax.psum(0.5 * jnp.sum(sq) / d, ("x", "y", "c"))

    got = [None] * len(stages)
    pending, token = None, None
    gsmall = {nm: [None] * depth for nm in SMALL}
    for k in reversed(range(len(stages))):
        l, gi = stages[k]
        s = saved[l]
        p = s["p"]
        if gi != 1:
            nw, xin, wi_, wo_, nm_n, (xn, gu, hh) = (
                (p["n1"], s["x0"], p["w1i"], p["w1o"], "ffn1_norm", s["ffn1"]) if gi == 0 else
                (p["n2"], s["x2"], p["w2i"], p["w2o"], "ffn2_norm", s["ffn2"]))
            dh, dn, dgu, dyh = ffn_bwd(xin, dh, behind(nw, token), gu, wi_, wo_)
            g_in, g_out = wgrad_ffn_in(xn, dgu), wgrad_ffn_out(hh, dyh)
            send = [g_in.reshape(N_DEV, d, g_in.shape[3]), g_out.reshape(N_DEV, -1, d)]
            gsmall[nm_n][l] = dn[0]
        else:
            dyf, dyg, dyb = outproj_bwd(dh, p["wo"], token)
            g_wo = wgrad_2d(s["ycat"], dyb, 512, "wgrad_w_out")
            dq, dk, dv, dga, dqw, dkw = attn_bwd(s["proj"], s["gates"], p["qw"], p["kw"], s["yf"], s["lse"], dyf, seq,
                                                 tq=min(seq, ATTN_TQ_BWD))
            dqh, dkh, dvh, dgg, dgb, donw = gdn_bwd(s["qh"], s["kh"], s["vh"], s["gates"], s["proj"], p["onw"],
                                                     s["inv"], s["st"], dyg, seq)
            dxq, dxk, dxv, dwq, dwk, dwv = gdn_pre_bwd(s["proj"], p["cw"], dqh, dkh, dvh, seq)
            dsm, dgp = gates_bwd(s["proj"], p["gp"], dga, dgb, seq)
            dh, dnmix, dproj = inproj_bwd(s["x1"], dh, p["nmix"], p["wi"], [dq, dk, dv, dxq, dxk, dxv, dgg, dsm])
            g_wi = wgrad_2d(s["hn"], dproj, 512, "wgrad_w_in", F32)
            g_cw = jnp.concatenate([dwq, dwk, dwv], axis=1)
            send = [_in_cols_from_mine(g_wi).reshape(d, N_DEV, -1).transpose(1, 0, 2),
                    g_cw.reshape(CONV_W, N_DEV, -1).transpose(1, 0, 2), g_wo.reshape(N_DEV, -1, d)]
            for nm, val in (("mix_norm", dnmix[0]), ("fox_q_norm", dqw[0]), ("fox_k_norm", dkw[0]),
                            ("fox_f_bias", dgp[0, 0:8]), ("gdn_a_log", dgp[1, 8:12]), ("gdn_dt_bias", dgp[0, 8:12]),
                            ("gdn_out_norm", donw[0])):
                gsmall[nm][l] = val
        flags = [True] * len(send)
        if k == 0:
            send.append(_pack_small({nm: jnp.stack(v) for nm, v in gsmall.items()}))
            flags.append(False)
        prev = dh
        if pending is not None:
            got[pending[1]] = exchange_end(pending[0], dh)
            prev = got[pending[1]][0]
        state, token = exchange_begin(send, flags, f"exchange_grads_{k}", prev)
        pending = (state, k)
    grad_x = dh.reshape(nb, seq, d)

    res = {}

    def update_stage(k, slots, after):
        l, gi = stages[k]
        for i, nm in enumerate(GROUPS[gi]):
            r, c = wts[nm].shape[1:]
            res[nm] = adamw_reduce(slots[i].reshape(N_DEV, r, c), wts[nm], mom[nm], var[nm], l, f"adamw_{nm}_{l}",
                                   after, res.get(nm))
            if after is not None:
                after = res[nm][0]
        return after

    last = token
    for k in range(1, len(stages)):
        last = update_stage(k, got[k], last)
    got[0] = exchange_end(pending[0], last)
    update_stage(0, got[0], None)
    small_like = {nm: wts[nm] for nm in SMALL}
    sm = adamw_reduce(got[0][-1], _pack_small(small_like)[None], _pack_small({nm: mom[nm] for nm in SMALL})[None],
                      _pack_small({nm: var[nm] for nm in SMALL})[None], 0, "adamw_small")
    sm = [_unpack_small(a[0], small_like) for a in sm]
    for nm in SMALL:
        res[nm] = [sm[j][nm] for j in range(4)]
    return (loss, grad_x, *[res[nm][0] for nm in WEIGHTS], *[res[nm][1] for nm in WEIGHTS],
            *[res[nm][2] for nm in WEIGHTS], *[res[nm][3] for nm in WEIGHTS])
```

```python
import jax
import jax.numpy as jnp
from jax import lax
from jax.experimental import pallas as pl
from jax.experimental.pallas import tpu as pltpu

F32 = jnp.float32
BF16 = jnp.bfloat16
EPS = 1e-6
N_DEV = 8
MESH = pl.DeviceIdType.MESH
HIGHEST = lax.Precision.HIGHEST
VMEM_LIMIT = 56 * 1024 * 1024

FOX_HEADS, FOX_DH = 8, 64
GDN_HEADS, GDN_DH = 4, 128
CHUNK = 64
CONV_W = 4

ADAM_LR, ADAM_B1, ADAM_B2, ADAM_EPS, ADAM_WD, ADAM_STEP = 0.001, 0.9, 0.999, 1e-08, 0.01, 10


def _cp(*sem):
    return pltpu.CompilerParams(dimension_semantics=sem, vmem_limit_bytes=VMEM_LIMIT)


def _dot(a, b):
    return jnp.dot(a, b, preferred_element_type=F32)


def _dot_nt(a, b):
    return lax.dot_general(a, b, (((1,), (1,)), ((), ())), preferred_element_type=F32)


def _dot_tn(a, b):
    return lax.dot_general(a, b, (((0,), (0,)), ((), ())), preferred_element_type=F32)


def _rstd(xf):
    return lax.rsqrt(jnp.mean(xf * xf, axis=-1, keepdims=True) + EPS)


def _rms_bwd(xf, r, dyn):
    return r * dyn - xf * (r * r * r) * jnp.mean(dyn * xf, axis=-1, keepdims=True)


def ffn_fwd(x, nw, w_in, w_out, tm=1024, rc=1024):
    t, d = x.shape
    nj, fb = w_out.shape[0], w_out.shape[1]
    tm = min(tm, t)
    rc = min(rc, tm)

    def body(x_ref, nw_ref, wi_ref, wo_ref, o_ref, xn_ref, gu_ref, h_ref, acc_ref):
        j = pl.program_id(1)

        @pl.when(j == 0)
        def _():
            xf = x_ref[...]
            xn_ref[...] = (xf * _rstd(xf) * nw_ref[...]).astype(BF16)
            acc_ref[...] = jnp.zeros_like(acc_ref)

        rows = [slice(c * rc, (c + 1) * rc) for c in range(tm // rc)]
        gs = [_dot(xn_ref[r, :], wi_ref[0]) for r in rows]
        us = [_dot(xn_ref[r, :], wi_ref[1]) for r in rows]
        hs = []
        for g, u, r in zip(gs, us, rows):
            sg = jax.nn.sigmoid(g)
            silu = g * sg
            h = (silu * u).astype(BF16)
            gu_ref[0, r, :] = (u * (sg * (1.0 + g * (1.0 - sg)))).astype(BF16)
            gu_ref[1, r, :] = silu.astype(BF16)
            h_ref[r, :] = h
            hs.append(h)
        for h, r in zip(hs, rows):
            acc_ref[r, :] += _dot(h, wo_ref[...])

        @pl.when(j == nj - 1)
        def _():
            o_ref[...] = x_ref[...] + 0.5 * acc_ref[...]

    return pl.pallas_call(
        body, grid=(t // tm, nj),
        in_specs=[pl.BlockSpec((tm, d), lambda i, j: (i, 0)),
                  pl.BlockSpec((1, d), lambda i, j: (0, 0)),
                  pl.BlockSpec((2, None, d, fb), lambda i, j: (0, j, 0, 0)),
                  pl.BlockSpec((None, fb, d), lambda i, j: (j, 0, 0))],
        out_specs=[pl.BlockSpec((tm, d), lambda i, j: (i, 0)),
                   pl.BlockSpec((tm, d), lambda i, j: (i, 0)),
                   pl.BlockSpec((2, None, tm, fb), lambda i, j: (0, j, i, 0)),
                   pl.BlockSpec((None, tm, fb), lambda i, j: (j, i, 0))],
        out_shape=[jax.ShapeDtypeStruct((t, d), F32), jax.ShapeDtypeStruct((t, d), BF16),
                   jax.ShapeDtypeStruct((2, nj, t, fb), BF16), jax.ShapeDtypeStruct((nj, t, fb), BF16)],
        scratch_shapes=[pltpu.VMEM((tm, d), F32)],
        compiler_params=_cp("parallel", "arbitrary"), name="ffn_fwd")(x, nw, w_in, w_out)


def ffn_bwd(x, dy, nw, gu, w_in, w_out, tm=512, rc=256):
    t, d = x.shape
    nj, fb = w_out.shape[0], w_out.shape[1]
    tm = min(tm, t)
    rc = min(rc, tm)

    def body(x_ref, dy_ref, nw_ref, gu_ref, wi_ref, wo_ref,
             dx_ref, dnw_ref, dgu_ref, dyh_ref, acc_ref):
        i, j = pl.program_id(0), pl.program_id(1)

        @pl.when(j == 0)
        def _():
            dyh_ref[...] = (0.5 * dy_ref[...]).astype(BF16)
            acc_ref[...] = jnp.zeros_like(acc_ref)

        @pl.when((i == 0) & (j == 0))
        def _():
            dnw_ref[...] = jnp.zeros_like(dnw_ref)

        rows = [slice(c * rc, (c + 1) * rc) for c in range(tm // rc)]
        dhs = [_dot_nt(dyh_ref[r, :], wo_ref[...]) for r in rows]
        dgs = [(dh * gu_ref[0, r, :].astype(F32)).astype(BF16) for dh, r in zip(dhs, rows)]
        dus = [(dh * gu_ref[1, r, :].astype(F32)).astype(BF16) for dh, r in zip(dhs, rows)]
        for dg, du, r in zip(dgs, dus, rows):
            dgu_ref[0, r, :] = dg
            dgu_ref[1, r, :] = du
        for dg, du, r in zip(dgs, dus, rows):
            acc_ref[r, :] += _dot_nt(dg, wi_ref[0]) + _dot_nt(du, wi_ref[1])

        @pl.when(j == nj - 1)
        def _():
            xf = x_ref[...]
            r = _rstd(xf)
            dxn = acc_ref[...]
            dnw_ref[...] += jnp.sum(dxn * xf * r, axis=0, keepdims=True)
            dx_ref[...] = _rms_bwd(xf, r, dxn * nw_ref[...]) + dy_ref[...]

    return pl.pallas_call(
        body, grid=(t // tm, nj),
        in_specs=[pl.BlockSpec((tm, d), lambda i, j: (i, 0)),
                  pl.BlockSpec((tm, d), lambda i, j: (i, 0)),
                  pl.BlockSpec((1, d), lambda i, j: (0, 0)),
                  pl.BlockSpec((2, None, tm, fb), lambda i, j: (0, j, i, 0)),
                  pl.BlockSpec((2, None, d, fb), lambda i, j: (0, j, 0, 0)),
                  pl.BlockSpec((None, fb, d), lambda i, j: (j, 0, 0))],
        out_specs=[pl.BlockSpec((tm, d), lambda i, j: (i, 0)),
                   pl.BlockSpec((1, d), lambda i, j: (0, 0)),
                   pl.BlockSpec((2, None, tm, fb), lambda i, j: (0, j, i, 0)),
                   pl.BlockSpec((tm, d), lambda i, j: (i, 0))],
        out_shape=[jax.ShapeDtypeStruct((t, d), F32),
                   jax.ShapeDtypeStruct((1, d), F32),
                   jax.ShapeDtypeStruct((2, nj, t, fb), BF16),
                   jax.ShapeDtypeStruct((t, d), BF16)],
        scratch_shapes=[pltpu.VMEM((tm, d), F32)],
        compiler_params=_cp("arbitrary", "arbitrary"), name="ffn_bwd")(x, dy, nw, gu, w_in, w_out)


def _wgrad_call(a, b, a_spec, b_spec, out_shape, out_spec, grid, name, out_dtype=BF16):
    last = len(grid) - 1
    acc_shape = tuple(s for s in out_spec.block_shape if s is not None)

    def body(a_ref, b_ref, o_ref, acc_ref):
        @pl.when(pl.program_id(last) == 0)
        def _():
            acc_ref[...] = jnp.zeros_like(acc_ref)

        if len(acc_shape) == 3:
            shared_a = a_ref[...] if len(a_ref.shape) == 2 else None
            shared_b = b_ref[...] if len(b_ref.shape) == 2 else None
            for s in range(acc_shape[0]):
                acc_ref[s] += _dot_tn(a_ref[s] if shared_a is None else shared_a,
                                      b_ref[s] if shared_b is None else shared_b)
        else:
            acc_ref[...] += _dot_tn(a_ref[...], b_ref[...])

        @pl.when(pl.program_id(last) == grid[last] - 1)
        def _():
            o_ref[...] = acc_ref[...].astype(o_ref.dtype)

    sem = ("parallel",) * last + ("arbitrary",)
    return pl.pallas_call(body, grid=grid, in_specs=[a_spec, b_spec], out_specs=out_spec,
                          out_shape=jax.ShapeDtypeStruct(out_shape, out_dtype),
                          scratch_shapes=[pltpu.VMEM(acc_shape, F32)],
                          compiler_params=_cp(*sem), name=name)(a, b)


WGRAD_TM = 1024


def wgrad_ffn_in(xn, dgu, tm=WGRAD_TM):
    t, d = xn.shape
    _, nj, _, fb = dgu.shape
    tm = min(tm, t)
    return _wgrad_call(xn, dgu,
                       pl.BlockSpec((tm, d), lambda j, k: (k, 0)),
                       pl.BlockSpec((2, None, tm, fb), lambda j, k: (0, j, k, 0)),
                       (2, nj, d, fb), pl.BlockSpec((2, None, d, fb), lambda j, k: (0, j, 0, 0)),
                       (nj, t // tm), "wgrad_ffn_in")


def wgrad_ffn_out(h, dyh, tm=WGRAD_TM):
    nj, t, fb = h.shape
    d = dyh.shape[1]
    tm = min(tm, t)
    return _wgrad_call(h, dyh,
                       pl.BlockSpec((nj, tm, fb), lambda k: (0, k, 0)),
                       pl.BlockSpec((tm, d), lambda k: (k, 0)),
                       (nj, fb, d), pl.BlockSpec((nj, fb, d), lambda k: (0, 0, 0)),
                       (t // tm,), "wgrad_ffn_out")


def wgrad_2d(a, b, tk, name, out_dtype=BF16, tm=WGRAD_TM):
    t, k = a.shape
    n = b.shape[1]
    tm = min(tm, t)
    return _wgrad_call(a, b,
                       pl.BlockSpec((tm, tk), lambda c, s: (s, c)),
                       pl.BlockSpec((tm, n), lambda c, s: (s, 0)),
                       (k, n), pl.BlockSpec((tk, n), lambda c, s: (c, 0)),
                       (k // tk, t // tm), name, out_dtype)


N_BIG = 7 * 512
N_PROJ = N_BIG + 128
COL_SMALL = N_BIG // 128


def inproj_fwd(x, nw, w, tm=512):
    t, d = x.shape
    n = w.shape[1]

    def body(x_ref, nw_ref, w_ref, p_ref, hn_ref):
        xf = x_ref[...]
        hn = (xf * _rstd(xf) * nw_ref[...]).astype(BF16)
        hn_ref[...] = hn
        p_ref[...] = _dot(hn, w_ref[...])

    return pl.pallas_call(
        body, grid=(t // tm,),
        in_specs=[pl.BlockSpec((tm, d), lambda i: (i, 0)), pl.BlockSpec((1, d), lambda i: (0, 0)),
                  pl.BlockSpec((d, n), lambda i: (0, 0))],
        out_specs=[pl.BlockSpec((tm, n), lambda i: (i, 0)), pl.BlockSpec((tm, d), lambda i: (i, 0))],
        out_shape=[jax.ShapeDtypeStruct((t, n), F32), jax.ShapeDtypeStruct((t, d), BF16)],
        compiler_params=_cp("parallel"), name="inproj_fwd")(x, nw, w)


def inproj_bwd(x, dres, nw, w, dparts, tm=512):
    t, d = x.shape
    n = w.shape[1]
    widths = [p.shape[1] for p in dparts]
    assert sum(widths) == n

    def body(x_ref, dres_ref, nw_ref, w_ref, *rest):
        part_refs, (dx_ref, dnw_ref, dp_ref) = rest[:len(widths)], rest[len(widths):]

        @pl.when(pl.program_id(0) == 0)
        def _():
            dnw_ref[...] = jnp.zeros_like(dnw_ref)

        dp = jnp.concatenate([r[...].astype(BF16) for r in part_refs], axis=1)
        dp_ref[...] = dp
        dhn = _dot_nt(dp, w_ref[...])
        xf = x_ref[...]
        r = _rstd(xf)
        dnw_ref[...] += jnp.sum(dhn * xf * r, axis=0, keepdims=True)
        dx_ref[...] = _rms_bwd(xf, r, dhn * nw_ref[...]) + dres_ref[...]

    return pl.pallas_call(
        body, grid=(t // tm,),
        in_specs=[pl.BlockSpec((tm, d), lambda i: (i, 0)), pl.BlockSpec((tm, d), lambda i: (i, 0)),
                  pl.BlockSpec((1, d), lambda i: (0, 0)), pl.BlockSpec((d, n), lambda i: (0, 0))]
                 + [pl.BlockSpec((tm, wd), lambda i: (i, 0)) for wd in widths],
        out_specs=[pl.BlockSpec((tm, d), lambda i: (i, 0)), pl.BlockSpec((1, d), lambda i: (0, 0)),
                   pl.BlockSpec((tm, n), lambda i: (i, 0))],
        out_shape=[jax.ShapeDtypeStruct((t, d), F32), jax.ShapeDtypeStruct((1, d), F32),
                   jax.ShapeDtypeStruct((t, n), BF16)],
        compiler_params=_cp("arbitrary"), name="inproj_bwd")(x, dres, nw, w, *dparts)


def outproj_fwd(x, yf, yg, w, tm=1024):
    t, d = x.shape
    hw = yf.shape[1]
    tm = min(tm, t)

    def body(x_ref, yf_ref, yg_ref, w_ref, o_ref, y_ref):
        y = jnp.concatenate([yf_ref[...], yg_ref[...]], axis=1).astype(BF16)
        y_ref[...] = y
        o_ref[...] = x_ref[...] + _dot(y, w_ref[...])

    return pl.pallas_call(
        body, grid=(t // tm,),
        in_specs=[pl.BlockSpec((tm, d), lambda i: (i, 0)), pl.BlockSpec((tm, hw), lambda i: (i, 0)),
                  pl.BlockSpec((tm, hw), lambda i: (i, 0)), pl.BlockSpec((2 * hw, d), lambda i: (0, 0))],
        out_specs=[pl.BlockSpec((tm, d), lambda i: (i, 0)), pl.BlockSpec((tm, 2 * hw), lambda i: (i, 0))],
        out_shape=[jax.ShapeDtypeStruct((t, d), F32), jax.ShapeDtypeStruct((t, 2 * hw), BF16)],
        compiler_params=_cp("parallel"), name="outproj_fwd")(x, yf, yg, w)


def outproj_bwd(dy, w, after=None, tm=1024):
    t, d = dy.shape
    hw = w.shape[0] // 2
    tm = min(tm, t)
    extra = [] if after is None else [after]

    def body(dy_ref, w_ref, *rest):
        df_ref, dg_ref, dyb_ref = rest[-3:]
        dyb = dy_ref[...].astype(BF16)
        dyb_ref[...] = dyb
        dyy = _dot_nt(dyb, w_ref[...])
        df_ref[...] = dyy[:, :hw]
        dg_ref[...] = dyy[:, hw:]

    return pl.pallas_call(
        body, grid=(t // tm,),
        in_specs=[pl.BlockSpec((tm, d), lambda i: (i, 0)), pl.BlockSpec((2 * hw, d), lambda i: (0, 0))]
                 + [pl.BlockSpec(memory_space=pl.ANY)] * len(extra),
        out_specs=[pl.BlockSpec((tm, hw), lambda i: (i, 0)), pl.BlockSpec((tm, hw), lambda i: (i, 0)),
                   pl.BlockSpec((tm, d), lambda i: (i, 0))],
        out_shape=[jax.ShapeDtypeStruct((t, hw), F32), jax.ShapeDtypeStruct((t, hw), F32),
                   jax.ShapeDtypeStruct((t, d), BF16)],
        compiler_params=_cp("parallel"), name="outproj_bwd")(dy, w, *extra)


def _lane(shape):
    return lax.broadcasted_iota(jnp.int32, shape, 1)


def _row(shape):
    return lax.broadcasted_iota(jnp.int32, shape, 0)


def _gate_terms(val, gp_ref):
    z = val + gp_ref[0:1, :]
    sp = jnp.log(1.0 + jnp.exp(-jnp.abs(z)))
    return z, sp


def gates_fwd(proj, gp, seq, ts=512):
    t = proj.shape[0]
    nb, ns = t // seq, seq // ts

    def body(v_ref, gp_ref, o_ref, carry_ref):
        @pl.when(pl.program_id(1) == 0)
        def _():
            carry_ref[...] = jnp.zeros_like(carry_ref)

        z, sp = _gate_terms(v_ref[...], gp_ref)
        logsig = jnp.minimum(z, 0.0) - sp
        tri = (_row((ts, ts)) >= _lane((ts, ts))).astype(F32)
        cum = jnp.dot(tri, logsig, precision=HIGHEST, preferred_element_type=F32) + carry_ref[0:1, :]
        carry_ref[0:1, :] = cum[ts - 1:ts, :]
        g = -jnp.exp(gp_ref[1:2, :]) * (jnp.maximum(z, 0.0) + sp)
        beta = jax.nn.sigmoid(z)
        lane = _lane((ts, 128))
        o_ref[...] = jnp.where(lane < 8, cum, jnp.where(lane < 12, g, jnp.where(lane < 16, beta, 0.0)))

    return pl.pallas_call(
        body, grid=(nb, ns),
        in_specs=[pl.BlockSpec((ts, 128), lambda b, s: (b * ns + s, COL_SMALL)),
                  pl.BlockSpec((8, 128), lambda b, s: (0, 0))],
        out_specs=pl.BlockSpec((ts, 128), lambda b, s: (b * ns + s, 0)),
        out_shape=jax.ShapeDtypeStruct((t, 128), F32),
        scratch_shapes=[pltpu.VMEM((8, 128), F32)],
        compiler_params=_cp("parallel", "arbitrary"), name="gates_fwd")(proj, gp)


def gates_bwd(proj, gp, dga, dgb, seq, ts=512):
    t = proj.shape[0]
    nb, ns = t // seq, seq // ts

    def body(v_ref, gp_ref, da_ref, db_ref, ds_ref, dgp_ref, carry_ref):
        @pl.when(pl.program_id(1) == 0)
        def _():
            carry_ref[...] = jnp.zeros_like(carry_ref)

        @pl.when((pl.program_id(0) == 0) & (pl.program_id(1) == 0))
        def _():
            dgp_ref[...] = jnp.zeros_like(dgp_ref)

        lane = _lane((ts, 128))
        dgate = jnp.where(lane < 8, da_ref[...], jnp.where(lane < 16, db_ref[...], 0.0))
        z, sp = _gate_terms(v_ref[...], gp_ref)
        triu = (_row((ts, ts)) <= _lane((ts, ts))).astype(F32)
        dlog = jnp.dot(triu, dgate, precision=HIGHEST, preferred_element_type=F32) + carry_ref[0:1, :]
        carry_ref[0:1, :] = dlog[0:1, :]
        sig = jax.nn.sigmoid(z)
        nea = -jnp.exp(gp_ref[1:2, :])
        g = nea * (jnp.maximum(z, 0.0) + sp)
        dz = jnp.where(lane < 8, dlog * (1.0 - sig),
                       jnp.where(lane < 12, dgate * nea * sig, dgate * sig * (1.0 - sig)))
        dz = jnp.where(lane < 16, dz, 0.0)
        ds_ref[...] = dz.astype(BF16)
        dgp_ref[0:1, :] += jnp.where(lane[0:1] < 12, jnp.sum(dz, axis=0, keepdims=True), 0.0)
        dgp_ref[1:2, :] += jnp.where((lane[0:1] >= 8) & (lane[0:1] < 12), jnp.sum(dgate * g, axis=0, keepdims=True), 0.0)

    rev = lambda b, s: (b * ns + (ns - 1 - s), 0)
    return pl.pallas_call(
        body, grid=(nb, ns),
        in_specs=[pl.BlockSpec((ts, 128), lambda b, s: (b * ns + (ns - 1 - s), COL_SMALL)),
                  pl.BlockSpec((8, 128), lambda b, s: (0, 0)),
                  pl.BlockSpec((ts, 128), rev), pl.BlockSpec((ts, 128), rev)],
        out_specs=[pl.BlockSpec((ts, 128), rev), pl.BlockSpec((8, 128), lambda b, s: (0, 0))],
        out_shape=[jax.ShapeDtypeStruct((t, 128), BF16), jax.ShapeDtypeStruct((8, 128), F32)],
        scratch_shapes=[pltpu.VMEM((8, 128), F32)],
        compiler_params=_cp("arbitrary", "arbitrary"), name="gates_bwd")(proj, gp, dga, dgb)


NEG = -1e30
ATTN_TQ_FWD = 1024
ATTN_TQ_BWD = 512


def _pick_lane(tile, idx):
    return jnp.sum(jnp.where(_lane(tile.shape) == idx, tile, 0.0), axis=1, keepdims=True)


def _row_to_col(row, n):
    return jnp.sum(jnp.where(_row((n, n)) == _lane((n, n)), row, 0.0), axis=1, keepdims=True)


def _rows(i, n):
    return pl.ds(pl.multiple_of(i * n, n), n)


LOG2E = 1.4426950408889634
LN2 = 0.6931471805599453


def _split_dot(x, mat, passes):
    total, rest = None, x
    for _ in range(passes):
        part = rest.astype(BF16)
        rest = rest - part.astype(F32)
        total = _dot(part, mat) if total is None else total + _dot(part, mat)
    return total


def _pair_mats(p, dh):
    r, l = _row((128, 128)), _lane((128, 128))
    same = (r < dh) == (l < dh)
    upper = (l >= dh).astype(jnp.int32)
    as_bf16 = lambda m: m.astype(BF16)
    return dict(own=as_bf16(same), other=as_bf16(jnp.logical_not(same)), pick_other=as_bf16(r == 2 * p + 1 - upper),
                swap=as_bf16(((r == 0) & (l >= dh)) | ((r == dh) & (l < dh))))


def _pair_rstd(x2, sel, dh):
    return lax.rsqrt(_split_dot(x2 * x2, sel["own"], 2) * (1.0 / dh) + EPS)


def _pair_aug(cols, n, dh):
    lane = _lane((n, 128))
    li = jnp.where(lane >= dh, lane - dh, lane)
    out = jnp.zeros((n, 128), F32)
    for i, c in enumerate(cols):
        out = jnp.where(li == i, c, out)
    return out


def _split3(x):
    hi = x.astype(BF16).astype(F32)
    mid = (x - hi).astype(BF16).astype(F32)
    return [hi, mid, (x - hi - mid).astype(BF16).astype(F32)]


def _once(shape, index_map):
    return pl.BlockSpec(shape, index_map, pipeline_mode=pl.Buffered(1))


def attn_fwd(proj, gates, qw, kw, seq, tq=256):
    t = proj.shape[0]
    nb, nq, dh = t // seq, seq // tq, FOX_DH
    scale = dh ** -0.5

    def body(q_ref, k_ref, v_ref, g_ref, qw_ref, kw_ref, y_ref, lse_ref, qs, ks, vs):
        p = pl.program_id(1)
        heads = range(2)
        low = _lane((tq, 128)) < dh
        sel = _pair_mats(p, dh)

        def prep(i, _):
            r = _rows(i, tq)
            q2, k2 = q_ref[r, :], k_ref[r, :]
            cc = _split_dot(g_ref[r, :], sel["pick_other"], 3) * LOG2E
            qn = q2 * _pair_rstd(q2, sel, dh) * qw_ref[...] * (scale * LOG2E)
            kn = k2 * _pair_rstd(k2, sel, dh) * kw_ref[...]
            qx = _pair_aug(_split3(cc) + [1.0, 1.0, 1.0], tq, dh)
            kx = _pair_aug([1.0, 1.0, 1.0] + _split3(-cc), tq, dh)
            for hh in heads:
                own = low if hh == 0 else jnp.logical_not(low)
                qs[hh, r, :] = jnp.where(own, qn, qx).astype(BF16)
                ks[hh, r, :] = jnp.where(own, kn, kx).astype(BF16)
            vs[r, :] = v_ref[r, :].astype(BF16)
            return 0

        lax.fori_loop(0, nq, prep, 0)

        def q_tile(i, _):
            r = _rows(i, tq)
            qt = [qs[hh, r, :] for hh in heads]

            def kv_step(j, carry, masked):
                kr = _rows(j, tq)
                vt = vs[kr, :]
                out = []
                for hh in heads:
                    m, l, acc = carry[hh]
                    s = _dot_nt(qt[hh], ks[hh, kr, :])
                    if masked:
                        s = jnp.where(_row((tq, tq)) >= _lane((tq, tq)), s, NEG)
                    m_new = jnp.maximum(m, jnp.max(s, axis=1, keepdims=True))
                    pe = jnp.exp2(s - m_new)
                    a = jnp.exp2(m - m_new)
                    out.append((m_new, a * l + jnp.sum(pe, axis=1, keepdims=True), a * acc + _dot(pe.astype(BF16), vt)))
                return tuple(out)

            one = (jnp.full((tq, 1), NEG, F32), jnp.zeros((tq, 1), F32), jnp.zeros((tq, 128), F32))
            carry = lax.fori_loop(0, i, lambda j, c: kv_step(j, c, False), (one, one))
            (m0, l0, acc0), (m1, l1, acc1) = kv_step(i, carry, True)
            y_ref[r, :] = jnp.where(low, acc0 / l0, acc1 / l1)
            lse_ref[r, :] = jnp.where(low, m0 + jnp.log2(l0), m1 + jnp.log2(l1))
            return 0

        lax.fori_loop(0, nq, q_tile, 0)

    blk = lambda off: _once((seq, 128), lambda b, p: (b, off + p))
    return pl.pallas_call(
        body, grid=(nb, 4),
        in_specs=[blk(0), blk(4), blk(8), _once((seq, 128), lambda b, p: (b, 0)),
                  pl.BlockSpec((1, 128), lambda b, p: (0, 0)), pl.BlockSpec((1, 128), lambda b, p: (0, 0))],
        out_specs=[pl.BlockSpec((seq, 128), lambda b, p: (b, p)), pl.BlockSpec((seq, 128), lambda b, p: (b, p))],
        out_shape=[jax.ShapeDtypeStruct((t, 512), F32), jax.ShapeDtypeStruct((t, 512), F32)],
        scratch_shapes=[pltpu.VMEM((2, seq, 128), BF16), pltpu.VMEM((2, seq, 128), BF16), pltpu.VMEM((seq, 128), BF16)],
        compiler_params=_cp("parallel", "arbitrary"),
        name="attn_fwd")(proj, proj, proj, gates, jnp.tile(qw, (1, 2)), jnp.tile(kw, (1, 2)))


def attn_bwd(proj, gates, qw, kw, y, lse, dy, seq, tq=256):
    t = proj.shape[0]
    nb, nq, dh = t // seq, seq // tq, FOX_DH
    scale = dh ** -0.5

    def body(q_ref, k_ref, v_ref, g_ref, qw_ref, kw_ref, y_ref, lse_ref, dy_ref,
             dq_ref, dk_ref, dv_ref, dg_ref, dqw_ref, dkw_ref,
             qs, ks, vs, dos, dsrow, dqa, dka):
        b, p = pl.program_id(0), pl.program_id(1)

        @pl.when((b == 0) & (p == 0))
        def _():
            dqw_ref[...] = jnp.zeros_like(dqw_ref)
            dkw_ref[...] = jnp.zeros_like(dkw_ref)

        @pl.when(p == 0)
        def _():
            dg_ref[...] = jnp.zeros_like(dg_ref)

        heads = range(2)
        low = _lane((tq, 128)) < dh
        sel = _pair_mats(p, dh)

        def prep(i, _):
            r = _rows(i, tq)
            q2, k2, dy2 = q_ref[r, :], k_ref[r, :], dy_ref[r, :]
            cc = _split_dot(g_ref[r, :], sel["pick_other"], 3) * LOG2E
            lse_x = _split_dot(lse_ref[r, :], sel["swap"], 3)
            delta_x = _split_dot(dy2 * y_ref[r, :], sel["other"], 2)
            qn = q2 * _pair_rstd(q2, sel, dh) * qw_ref[...] * (scale * LOG2E)
            kn = k2 * _pair_rstd(k2, sel, dh) * kw_ref[...]
            qx = _pair_aug(_split3(cc) + [1.0, 1.0, 1.0] + _split3(-lse_x), tq, dh)
            kx = _pair_aug([1.0, 1.0, 1.0] + _split3(-cc) + [1.0, 1.0, 1.0], tq, dh)
            vx = _pair_aug([1.0, 1.0, 1.0], tq, dh)
            dx = _pair_aug(_split3(-delta_x), tq, dh)
            for hh in heads:
                own = low if hh == 0 else jnp.logical_not(low)
                qs[hh, r, :] = jnp.where(own, qn, qx).astype(BF16)
                ks[hh, r, :] = jnp.where(own, kn, kx).astype(BF16)
                vs[hh, r, :] = jnp.where(own, v_ref[r, :], vx).astype(BF16)
                dos[hh, r, :] = jnp.where(own, dy2, dx).astype(BF16)
                dsrow[hh, r, :] = jnp.zeros((tq, 1), F32)
                dqa[hh, r, :] = jnp.zeros((tq, 128), F32)
            return 0

        lax.fori_loop(0, nq, prep, 0)

        def kv_tile(j, _):
            kr = _rows(j, tq)
            kt = [ks[hh, kr, :] for hh in heads]
            vt = [vs[hh, kr, :] for hh in heads]

            def q_step(i, carry, masked):
                r = _rows(i, tq)
                out = []
                for hh in heads:
                    dk, dv, dcr = carry[hh]
                    qt, dot = qs[hh, r, :], dos[hh, r, :]
                    s = _dot_nt(qt, kt[hh])
                    if masked:
                        s = jnp.where(_row((tq, tq)) >= _lane((tq, tq)), s, NEG)
                    pe = jnp.exp2(s)
                    ds = pe * _dot_nt(dot, vt[hh])
                    dsb = ds.astype(BF16)
                    dqa[hh, r, :] += _dot(dsb, kt[hh])
                    dsrow[hh, r, :] += jnp.sum(ds, axis=1, keepdims=True)
                    out.append((dk + _dot_tn(dsb, qt), dv + _dot_tn(pe.astype(BF16), dot),
                                dcr - jnp.sum(ds, axis=0, keepdims=True)))
                return tuple(out)

            one = (jnp.zeros((tq, 128), F32), jnp.zeros((tq, 128), F32), jnp.zeros((1, tq), F32))
            carry = q_step(j, (one, one), True)
            (dk0, dv0, dcr0), (dk1, dv1, dcr1) = lax.fori_loop(j + 1, nq, lambda i, c: q_step(i, c, False), carry)
            dka[kr, :] = jnp.where(low, dk0, dk1)
            dv_ref[kr, :] = jnp.where(low, dv0, dv1).astype(BF16)
            lane = _lane((tq, 128))
            dg_ref[kr, :] = jnp.where(lane == 2 * p, _row_to_col(dcr0, tq),
                                      jnp.where(lane == 2 * p + 1, _row_to_col(dcr1, tq), dg_ref[kr, :]))
            return 0

        lax.fori_loop(0, nq, kv_tile, 0)

        def post(i, _):
            r = _rows(i, tq)
            q2, k2 = q_ref[r, :], k_ref[r, :]
            rq, rk = _pair_rstd(q2, sel, dh), _pair_rstd(k2, sel, dh)
            dqn = jnp.where(low, dqa[0, r, :], dqa[1, r, :]) * scale
            dkn = dka[r, :] * LN2
            dqw_ref[...] += jnp.sum(dqn * q2 * rq, axis=0, keepdims=True)
            dkw_ref[...] += jnp.sum(dkn * k2 * rk, axis=0, keepdims=True)
            for x2, rr, dyn, o_ref in ((q2, rq, dqn * qw_ref[...], dq_ref), (k2, rk, dkn * kw_ref[...], dk_ref)):
                mean = _split_dot(dyn * x2, sel["own"], 2) * (1.0 / dh)
                o_ref[r, :] = (rr * dyn - x2 * (rr * rr * rr) * mean).astype(BF16)
            lane = _lane((tq, 128))
            dg_ref[r, :] += jnp.where(lane == 2 * p, dsrow[0, r, :], jnp.where(lane == 2 * p + 1, dsrow[1, r, :], 0.0))
            return 0

        lax.fori_loop(0, nq, post, 0)

    blk = lambda off: _once((seq, 128), lambda b, p: (b, off + p))
    own = lambda: _once((seq, 128), lambda b, p: (b, p))
    vec = lambda: pl.BlockSpec((1, 128), lambda b, p: (0, 0))
    res = pl.pallas_call(
        body, grid=(nb, 4),
        in_specs=[blk(0), blk(4), blk(8), _once((seq, 128), lambda b, p: (b, 0)), vec(), vec(), own(), own(), own()],
        out_specs=[own(), own(), own(), _once((seq, 128), lambda b, p: (b, 0)), vec(), vec()],
        out_shape=[jax.ShapeDtypeStruct((t, 512), BF16)] * 3
                  + [jax.ShapeDtypeStruct((t, 128), F32), jax.ShapeDtypeStruct((1, 128), F32), jax.ShapeDtypeStruct((1, 128), F32)],
        scratch_shapes=[pltpu.VMEM((2, seq, 128), BF16)] * 4
                       + [pltpu.VMEM((2, seq, 1), F32), pltpu.VMEM((2, seq, 128), F32), pltpu.VMEM((seq, 128), F32)],
        compiler_params=_cp("arbitrary", "arbitrary"),
        name="attn_bwd")(proj, proj, proj, gates, jnp.tile(qw, (1, 2)), jnp.tile(kw, (1, 2)), y, lse, dy)
    return list(res[:4]) + [res[4][:, :dh] + res[4][:, dh:], res[5][:, :dh] + res[5][:, dh:]]


def _silu_grad(c, sg):
    return sg * (1.0 + c * (1.0 - sg))


def _conv(x, w, n):
    row = _row(x.shape)
    c = x * w[CONV_W - 1:CONV_W, :]
    for k in range(CONV_W - 1):
        sh = CONV_W - 1 - k
        c = c + w[k:k + 1, :] * jnp.where(row >= sh, pltpu.roll(x, sh, 0), 0.0)
    return c


def gdn_pre_fwd(proj, cw, seq):
    t = proj.shape[0]
    nb = t // seq
    scale = GDN_DH ** -0.5

    def body(xq_ref, xk_ref, xv_ref, wq_ref, wk_ref, wv_ref, q_ref, k_ref, v_ref):
        def act(x_ref, w_ref):
            c = _conv(x_ref[...], w_ref[...], seq)
            return c * jax.nn.sigmoid(c)

        aq, ak = act(xq_ref, wq_ref), act(xk_ref, wk_ref)
        q_ref[...] = aq * lax.rsqrt(jnp.sum(aq * aq, axis=1, keepdims=True) + EPS) * scale
        k_ref[...] = ak * lax.rsqrt(jnp.sum(ak * ak, axis=1, keepdims=True) + EPS)
        v_ref[...] = act(xv_ref, wv_ref)

    xb = lambda off: pl.BlockSpec((seq, 128), lambda b, h: (b, off + h))
    wb = lambda off: pl.BlockSpec((CONV_W, 128), lambda b, h: (0, off + h))
    ob = lambda: pl.BlockSpec((seq, 128), lambda b, h: (b, h))
    return pl.pallas_call(
        body, grid=(nb, GDN_HEADS),
        in_specs=[xb(12), xb(16), xb(20), wb(0), wb(4), wb(8)],
        out_specs=[ob(), ob(), ob()],
        out_shape=[jax.ShapeDtypeStruct((t, 512), F32)] * 3,
        compiler_params=_cp("parallel", "parallel"), name="gdn_pre_fwd")(proj, proj, proj, cw, cw, cw)


def gdn_pre_bwd(proj, cw, dq, dk, dv, seq):
    t = proj.shape[0]
    nb = t // seq
    scale = GDN_DH ** -0.5

    def body(xq_ref, xk_ref, xv_ref, wq_ref, wk_ref, wv_ref, dq_ref, dk_ref, dv_ref,
             dxq_ref, dxk_ref, dxv_ref, dwq_ref, dwk_ref, dwv_ref):
        first = pl.program_id(1) == 0
        row = _row((seq, 128))

        def one(x_ref, w_ref, dy_ref, dx_ref, dw_ref, norm, sc):
            x, w = x_ref[...], w_ref[...]
            c = _conv(x, w, seq)
            sg = jax.nn.sigmoid(c)
            dy = dy_ref[...]
            if norm:
                a = c * sg
                rs = lax.rsqrt(jnp.sum(a * a, axis=1, keepdims=True) + EPS)
                dy = dy * sc
                da = rs * dy - a * (rs * rs * rs) * jnp.sum(dy * a, axis=1, keepdims=True)
            else:
                da = dy
            dc = da * _silu_grad(c, sg)
            dx = dc * w[CONV_W - 1:CONV_W, :]
            dws = [None] * CONV_W
            dws[CONV_W - 1] = jnp.sum(dc * x, axis=0, keepdims=True)
            for k in range(CONV_W - 1):
                sh = CONV_W - 1 - k
                dc_up = jnp.where(row < seq - sh, pltpu.roll(dc, seq - sh, 0), 0.0)
                dx = dx + w[k:k + 1, :] * dc_up
                dws[k] = jnp.sum(dc_up * x, axis=0, keepdims=True)
            dx_ref[...] = dx.astype(BF16)
            dwn = jnp.concatenate(dws, axis=0)

            @pl.when(first)
            def _():
                dw_ref[...] = dwn

            @pl.when(jnp.logical_not(first))
            def _():
                dw_ref[...] += dwn

        one(xq_ref, wq_ref, dq_ref, dxq_ref, dwq_ref, True, scale)
        one(xk_ref, wk_ref, dk_ref, dxk_ref, dwk_ref, True, 1.0)
        one(xv_ref, wv_ref, dv_ref, dxv_ref, dwv_ref, False, 1.0)

    xb = lambda off: pl.BlockSpec((seq, 128), lambda h, b: (b, off + h))
    wb = lambda off: pl.BlockSpec((CONV_W, 128), lambda h, b: (0, off + h))
    ob = lambda: pl.BlockSpec((seq, 128), lambda h, b: (b, h))
    return pl.pallas_call(
        body, grid=(GDN_HEADS, nb),
        in_specs=[xb(12), xb(16), xb(20), wb(0), wb(4), wb(8), ob(), ob(), ob()],
        out_specs=[ob(), ob(), ob()] + [pl.BlockSpec((CONV_W, 128), lambda h, b: (0, h))] * 3,
        out_shape=[jax.ShapeDtypeStruct((t, 512), BF16)] * 3 + [jax.ShapeDtypeStruct((CONV_W, 512), F32)] * 3,
        compiler_params=_cp("parallel", "arbitrary"), name="gdn_pre_bwd")(proj, proj, proj, cw, cw, cw, dq, dk, dv)


def _b16(x):
    return x.astype(BF16)


@jax.custom_vjp
def _mm(a, b):
    return _dot(_b16(a), _b16(b))


_mm.defvjp(lambda a, b: (_mm(a, b), (a, b)),
           lambda res, g: (_dot_nt(_b16(g), _b16(res[1])), _dot_tn(_b16(res[0]), _b16(g))))


@jax.custom_vjp
def _mm_nt(a, b):
    return _dot_nt(_b16(a), _b16(b))


_mm_nt.defvjp(lambda a, b: (_mm_nt(a, b), (a, b)),
              lambda res, g: (_dot(_b16(g), _b16(res[1])), _dot_tn(_b16(g), _b16(res[0]))))


@jax.custom_vjp
def _mm_tn(a, b):
    return _dot_tn(_b16(a), _b16(b))


_mm_tn.defvjp(lambda a, b: (_mm_tn(a, b), (a, b)),
              lambda res, g: (_dot_nt(_b16(res[1]), _b16(g)), _dot(_b16(res[0]), _b16(g))))


def _dot32(a, b, dims=(((1,), (0,)), ((), ()))):
    def split(x):
        hi = x.astype(BF16)
        return hi, (x - hi.astype(F32)).astype(BF16)

    (ah, al), (bh, bl) = split(a), split(b)
    d = lambda x, y: lax.dot_general(x, y, dims, preferred_element_type=F32)
    return d(ah, bh) + (d(ah, bl) + d(al, bh))


def _inv_fwd_many(mats):
    n = mats[0].shape[0]
    eye = (_row((n, n)) == _lane((n, n))).astype(F32)
    invs, pws = [eye - a for a in mats], list(mats)
    for _ in range(n.bit_length() - 2):
        pws = [_dot32(pw, pw) for pw in pws]
        invs = [inv + _dot32(inv, pw) for inv, pw in zip(invs, pws)]
    return invs


@jax.custom_vjp
def _inv_saved(a, inv):
    return inv


def _inv_saved_bwd(inv, g):
    tg = _dot32(inv, g, (((0,), (0,)), ((), ())))
    return -_dot32(tg, inv, (((1,), (1,)), ((), ()))), jnp.zeros_like(inv)


_inv_saved.defvjp(lambda a, inv: (inv, inv), _inv_saved_bwd)


def _gdn_decay(gcol):
    c = CHUNK
    ri, ci = _row((c, c)), _lane((c, c))
    incl, eye = ri >= ci, ri == ci
    grow = jnp.sum(jnp.where(eye, gcol, 0.0), axis=0, keepdims=True)
    gc = jnp.sum(jnp.where(incl, grow, 0.0), axis=1, keepdims=True)
    gcr = jnp.sum(jnp.where(eye, gc, 0.0), axis=0, keepdims=True)
    gl = jnp.sum(jnp.where(_row((c, 1)) == c - 1, gc, 0.0), axis=0, keepdims=True)
    return gc, gl, jnp.exp(jnp.where(incl, gc - gcr, NEG))


def _gdn_a(k, bcol, decay):
    c = CHUNK
    return jnp.where(_row((c, c)) > _lane((c, c)), _mm_nt(k * bcol, k) * decay, 0.0)


def _gdn_chunk(q, k, v, gcol, bcol, state, gg, nw, inv_saved):
    c = CHUNK
    incl = _row((c, c)) >= _lane((c, c))
    gc, gl, decay = _gdn_decay(gcol)
    kb, vb = k * bcol, v * bcol
    inv = _inv_saved(_gdn_a(k, bcol, decay), inv_saved)
    eg = jnp.exp(gc)
    u = _mm(inv, vb)
    w = _mm(inv, kb * eg)
    pm = jnp.where(incl, _mm_nt(q, k) * decay, 0.0)
    kd = k * jnp.exp(gl - gc)
    qd = q * eg
    v_new = u - _mm(w, state)
    o = _mm(qd, state) + _mm(pm, v_new)
    state_new = state * jnp.exp(gl) + _mm_tn(kd, v_new)
    y = o * _rstd(o) * nw * (gg * jax.nn.sigmoid(gg))
    return y, state_new


_gdn_chunks = jax.vmap(_gdn_chunk, in_axes=(0, 0, 0, 0, 0, 0, 0, None, 0))


def _gdn_chain_inputs(chains, p, r, c, q_ref, k_ref, v_ref, g_ref, gg_ref, inv_ref):
    cols = {nm: [] for nm in ("q", "k", "v", "g", "b", "gg", "inv")}
    for b, hh in chains:
        h = GDN_HPS * p + hh
        ln = slice(hh * 128, (hh + 1) * 128)
        gt = g_ref[b, r, :]
        cols["q"].append(q_ref[b, r, ln])
        cols["k"].append(k_ref[b, r, ln])
        cols["v"].append(v_ref[b, r, ln])
        cols["g"].append(_pick_lane(gt, 8 + h))
        cols["b"].append(_pick_lane(gt, 12 + h))
        cols["gg"].append(gg_ref[b, r, ln])
        cols["inv"].append(inv_ref[b, hh, c])
    return [jnp.stack(cols[nm]) for nm in ("q", "k", "v", "g", "b", "gg", "inv")]


GDN_CB = 8
GDN_INV_CB = 16
GDN_HPS = 4


def gdn_inv(k, gates, seq):
    t = k.shape[0]
    nb, nc = t // seq, seq // CHUNK
    cb = min(GDN_INV_CB, nc)
    rb = cb * CHUNK
    nsb = seq // rb

    def body(k_ref, g_ref, o_ref):
        h = pl.program_id(1)
        mats = []
        for c in range(cb):
            r = slice(c * CHUNK, (c + 1) * CHUNK)
            gt = g_ref[r, :]
            _, _, decay = _gdn_decay(_pick_lane(gt, 8 + h))
            mats.append(_gdn_a(k_ref[r, :], _pick_lane(gt, 12 + h), decay))
        for c, inv in enumerate(_inv_fwd_many(mats)):
            o_ref[c] = inv

    return pl.pallas_call(
        body, grid=(nb, GDN_HEADS, nsb),
        in_specs=[pl.BlockSpec((rb, 128), lambda b, h, s: (b * nsb + s, h)),
                  pl.BlockSpec((rb, 128), lambda b, h, s: (b * nsb + s, 0))],
        out_specs=pl.BlockSpec((None, None, cb, CHUNK, CHUNK), lambda b, h, s: (b, h, s, 0, 0)),
        out_shape=jax.ShapeDtypeStruct((nb, GDN_HEADS, nc, CHUNK, CHUNK), F32),
        compiler_params=_cp("parallel", "parallel", "parallel"), name="gdn_inv")(k, gates)


def _gdn_specs(nb, nsb, cb, rev):
    blk = (lambda s: nsb - 1 - s) if rev else (lambda s: s)
    rb = cb * CHUNK
    pair = lambda off=0: pl.BlockSpec((nb, rb, 128 * GDN_HPS), lambda s, p: (0, blk(s), off + p))
    gate = lambda: pl.BlockSpec((nb, rb, 128), lambda s, p: (0, blk(s), 0))
    mats = lambda n: pl.BlockSpec((nb, GDN_HPS, cb, n, n), lambda s, p: (0, p, blk(s), 0, 0))
    return pair, gate, mats


def gdn_fwd(q, k, v, gates, proj, nw, inv, seq):
    t = q.shape[0]
    nb, nc = t // seq, seq // CHUNK
    cb = GDN_CB
    nsb = nc // cb
    chains = [(b, hh) for b in range(nb) for hh in range(GDN_HPS)]
    nch = len(chains)
    pair, gate, mats = _gdn_specs(nb, nsb, cb, False)

    def body(q_ref, k_ref, v_ref, g_ref, gg_ref, inv_ref, nw_ref, y_ref, st_ref, carry):
        s, p = pl.program_id(0), pl.program_id(1)

        @pl.when(s == 0)
        def _():
            for ci in range(nch):
                carry[p * nch + ci] = jnp.zeros((GDN_DH, GDN_DH), F32)

        def step(c, states):
            r = _rows(c, CHUNK)
            for ci, (b, hh) in enumerate(chains):
                st_ref[b, hh, c] = states[ci]
            ins = _gdn_chain_inputs(chains, p, r, c, q_ref, k_ref, v_ref, g_ref, gg_ref, inv_ref)
            y, states = _gdn_chunks(*ins[:5], states, ins[5], nw_ref[...], ins[6])
            for ci, (b, hh) in enumerate(chains):
                y_ref[b, r, hh * 128:(hh + 1) * 128] = y[ci]
            return states

        states = lax.fori_loop(0, cb, step, jnp.stack([carry[p * nch + ci] for ci in range(nch)]))
        for ci in range(nch):
            carry[p * nch + ci] = states[ci]

    v3 = lambda a: a.reshape(nb, seq, a.shape[1])
    y, st = pl.pallas_call(
        body, grid=(nsb, GDN_HEADS // GDN_HPS),
        in_specs=[pair(), pair(), pair(), gate(), pair(24 // GDN_HPS), mats(CHUNK), pl.BlockSpec((1, 128), lambda s, p: (0, 0))],
        out_specs=[pair(), mats(GDN_DH)],
        out_shape=[jax.ShapeDtypeStruct((nb, seq, 512), F32),
                   jax.ShapeDtypeStruct((nb, GDN_HEADS, nc, GDN_DH, GDN_DH), F32)],
        scratch_shapes=[pltpu.VMEM((GDN_HEADS // GDN_HPS * nch, GDN_DH, GDN_DH), F32)],
        compiler_params=_cp("arbitrary", "arbitrary"), name="gdn_fwd")(v3(q), v3(k), v3(v), v3(gates), v3(proj), inv, nw)
    return y.reshape(t, 512), st


def gdn_bwd(q, k, v, gates, proj, nw, inv, states, dy, seq):
    t = q.shape[0]
    nb, nc = t // seq, seq // CHUNK
    cb = GDN_CB // 2
    nsb = nc // cb
    chains = [(b, hh) for b in range(nb) for hh in range(GDN_HPS)]
    nch = len(chains)
    pair, gate, mats = _gdn_specs(nb, nsb, cb, True)

    def body(q_ref, k_ref, v_ref, g_ref, gg_ref, inv_ref, st_ref, dy_ref, nw_ref,
             dq_ref, dk_ref, dv_ref, dgg_ref, dg_ref, dnw_ref, carry):
        s, p = pl.program_id(0), pl.program_id(1)

        @pl.when((s == 0) & (p == 0))
        def _():
            dnw_ref[...] = jnp.zeros_like(dnw_ref)

        @pl.when(p == 0)
        def _():
            dg_ref[...] = jnp.zeros_like(dg_ref)

        @pl.when(s == 0)
        def _():
            for ci in range(nch):
                carry[p * nch + ci] = jnp.zeros((GDN_DH, GDN_DH), F32)

        def step(idx, dstates):
            c = cb - 1 - idx
            r = _rows(c, CHUNK)
            ins = _gdn_chain_inputs(chains, p, r, c, q_ref, k_ref, v_ref, g_ref, gg_ref, inv_ref)
            st = jnp.stack([st_ref[b, hh, c] for b, hh in chains])
            dy = jnp.stack([dy_ref[b, r, hh * 128:(hh + 1) * 128] for b, hh in chains])
            _, vjp = jax.vjp(_gdn_chunks, *ins[:5], st, ins[5], nw_ref[...], ins[6])
            dq, dk, dv, dgc, dbc, dstates, dgg, dnw, _ = vjp((dy, dstates))
            dnw_ref[...] += dnw
            lane = _lane((CHUNK, 128))
            for ci, (b, hh) in enumerate(chains):
                h = GDN_HPS * p + hh
                ln = slice(hh * 128, (hh + 1) * 128)
                dq_ref[b, r, ln] = dq[ci]
                dk_ref[b, r, ln] = dk[ci]
                dv_ref[b, r, ln] = dv[ci]
                dgg_ref[b, r, ln] = dgg[ci].astype(BF16)
                dg_ref[b, r, :] = jnp.where(lane == 8 + h, dgc[ci], jnp.where(lane == 12 + h, dbc[ci], dg_ref[b, r, :]))
            return dstates

        dstates = lax.fori_loop(0, cb, step, jnp.stack([carry[p * nch + ci] for ci in range(nch)]))
        for ci in range(nch):
            carry[p * nch + ci] = dstates[ci]

    v3 = lambda a: a.reshape(nb, seq, a.shape[1])
    res = pl.pallas_call(
        body, grid=(nsb, GDN_HEADS // GDN_HPS),
        in_specs=[pair(), pair(), pair(), gate(), pair(24 // GDN_HPS), mats(CHUNK), mats(GDN_DH), pair(),
                  pl.BlockSpec((1, 128), lambda s, p: (0, 0))],
        out_specs=[pair(), pair(), pair(), pair(), gate(), pl.BlockSpec((1, 128), lambda s, p: (0, 0))],
        out_shape=[jax.ShapeDtypeStruct((nb, seq, 512), F32)] * 3 + [jax.ShapeDtypeStruct((nb, seq, 512), BF16)]
                  + [jax.ShapeDtypeStruct((nb, seq, 128), F32), jax.ShapeDtypeStruct((1, 128), F32)],
        scratch_shapes=[pltpu.VMEM((GDN_HEADS // GDN_HPS * nch, GDN_DH, GDN_DH), F32)],
        compiler_params=_cp("arbitrary", "arbitrary"),
        name="gdn_bwd")(v3(q), v3(k), v3(v), v3(gates), v3(proj), inv, states, v3(dy), nw)
    return [a.reshape(t, a.shape[2]) for a in res[:5]] + [res[5]]


def loss_head(y, target, tm=512):
    t, d = y.shape

    def body(y_ref, t_ref, s_ref, dy_ref):
        @pl.when(pl.program_id(0) == 0)
        def _():
            s_ref[...] = jnp.zeros_like(s_ref)

        err = y_ref[...] - t_ref[...]
        s_ref[...] += jnp.sum(err * err, axis=0, keepdims=True)
        dy_ref[...] = err * (1.0 / d)

    return pl.pallas_call(
        body, grid=(t // tm,),
        in_specs=[pl.BlockSpec((tm, d), lambda i: (i, 0)), pl.BlockSpec((tm, d), lambda i: (i, 0))],
        out_specs=[pl.BlockSpec((1, d), lambda i: (0, 0)), pl.BlockSpec((tm, d), lambda i: (i, 0))],
        out_shape=[jax.ShapeDtypeStruct((1, d), F32), jax.ShapeDtypeStruct((t, d), F32)],
        compiler_params=_cp("arbitrary"), name="loss_head")(y, target)


def _place():
    return lax.axis_index("x"), lax.axis_index("y"), lax.axis_index("c")


def _peer(k):
    x, y, c = _place()
    px = 1 - x if (k >> 2) & 1 else x
    py = 1 - y if (k >> 1) & 1 else y
    pc = 1 - c if k & 1 else c
    return (px, py, pc), 4 * px + 2 * py + pc


_ANY = pl.BlockSpec(memory_space=pl.ANY)
_SEM = pl.BlockSpec(memory_space=pltpu.SEMAPHORE)
_EFFECT = pltpu.SideEffectType.DATAFLOW_SIDE_EFFECTING


def _me():
    x, y, c = _place()
    return 4 * x + 2 * y + c


def _remote_copy(ins, lands, scatter, send_sems, recv_sems, a, k, arriving):
    pid, pidx = _peer(k)
    return pltpu.make_async_remote_copy(src_ref=ins[a].at[pidx] if scatter[a] else ins[a],
                                        dst_ref=lands[a].at[pidx if arriving else _me()],
                                        send_sem=send_sems.at[a * N_DEV + k], recv_sem=recv_sems.at[a * N_DEV + k],
                                        device_id=pid, device_id_type=MESH)


def _local_copy(ins, lands, scatter, loc_sems, a):
    me = _me()
    return pltpu.make_async_copy(ins[a].at[me] if scatter[a] else ins[a], lands[a].at[me], loc_sems.at[a])


def exchange_start(arrays, scatter, name, after):
    n = len(arrays)
    lands = [lax.empty(a.shape if s else (N_DEV,) + a.shape, a.dtype) for a, s in zip(arrays, scatter)]

    def body(*refs):
        ins, lds = refs[:n], refs[n:2 * n]
        send_sems, recv_sems, loc_sems = refs[2 * n + 1:2 * n + 4]
        token = refs[-1]
        for k in range(1, N_DEV):
            for a in range(n):
                _remote_copy(ins, lds, scatter, send_sems, recv_sems, a, k, False).start()
        for a in range(n):
            _local_copy(ins, lds, scatter, loc_sems, a).start()
        token[...] = jnp.zeros_like(token)

    hbm = lambda a: pltpu.HBM(a.shape, a.dtype)
    res = pl.pallas_call(
        body, name=name,
        in_specs=[_ANY] * (2 * n + 1),
        out_specs=[_SEM, _SEM, _SEM] + [_ANY] * (2 * n) + [pl.BlockSpec(memory_space=pltpu.VMEM)],
        out_shape=[pltpu.SemaphoreType.DMA((n * N_DEV,)), pltpu.SemaphoreType.DMA((n * N_DEV,)),
                   pltpu.SemaphoreType.DMA((n,))]
                  + [hbm(a) for a in arrays] + [hbm(a) for a in lands] + [jax.ShapeDtypeStruct((8, 128), F32)],
        input_output_aliases={i: 3 + i for i in range(2 * n)},
        compiler_params=pltpu.CompilerParams(has_side_effects=_EFFECT),
    )(*[pltpu.with_memory_space_constraint(a, pltpu.HBM) for a in list(arrays) + lands], after)
    return res[0:3], res[3:3 + n], res[3 + n:3 + 2 * n], res[-1]


def exchange_wait(sems, arrays, lands, scatter, after, name):
    n = len(arrays)

    def body(*refs):
        ins, lds = refs[:n], refs[n:2 * n]
        ssem, rsem, lsem = refs[2 * n:2 * n + 3]
        for a in range(n):
            _local_copy(ins, lds, scatter, lsem, a).wait()
        for k in range(1, N_DEV):
            for a in range(n):
                _remote_copy(ins, lds, scatter, ssem, rsem, a, k, True).wait_recv()
        for k in range(1, N_DEV):
            for a in range(n):
                _remote_copy(ins, lds, scatter, ssem, rsem, a, k, False).wait_send()

    hbm = lambda a: pltpu.HBM(a.shape, a.dtype)
    res = pl.pallas_call(
        body, name=name,
        in_specs=[_ANY] * (2 * n) + [_SEM, _SEM, _SEM, _ANY],
        out_specs=[_ANY] * (2 * n),
        out_shape=[hbm(a) for a in arrays] + [hbm(a) for a in lands],
        input_output_aliases={i: i for i in range(2 * n)},
        compiler_params=pltpu.CompilerParams(has_side_effects=_EFFECT),
    )(*arrays, *lands, *sems, after)
    return list(res[n:])


def exchange_begin(arrays, scatter, name, after):
    sems, arrays_thru, lands_thru, token = exchange_start(arrays, scatter, name + "_start", after)
    return (sems, arrays_thru, lands_thru, scatter, name), token


def exchange_end(state, after):
    sems, arrays_thru, lands_thru, scatter, name = state
    return exchange_wait(sems, arrays_thru, lands_thru, scatter, after, name + "_wait")


def adamw_reduce(slots, w, m, v, l, name, after=None, prev=None):
    nl, r, c = w.shape
    tr = r
    while tr * c * 4 > (1 << 20) and tr % 16 == 0:
        tr //= 2
    bc1 = 1.0 - ADAM_B1 ** ADAM_STEP
    bc2 = 1.0 - ADAM_B2 ** ADAM_STEP

    def body(s_ref, w_ref, m_ref, v_ref, *rest):
        g_ref, d_ref, nm_ref, nv_ref = rest[-4:]
        g = s_ref[0].astype(F32)
        for j in range(1, N_DEV):
            g = g + s_ref[j].astype(F32)
        nm = ADAM_B1 * m_ref[...] + (1.0 - ADAM_B1) * g
        nv = ADAM_B2 * v_ref[...] + (1.0 - ADAM_B2) * (g * g)
        g_ref[...] = g
        nm_ref[...] = nm
        nv_ref[...] = nv
        d_ref[...] = -ADAM_LR * ((nm / bc1) / (jnp.sqrt(nv / bc2) + ADAM_EPS) + ADAM_WD * w_ref[...])

    blk = lambda: pl.BlockSpec((None, tr, c), lambda i: (l, i, 0))
    extra = ([] if after is None else [after]) + ([] if prev is None else list(prev))
    first_prev = 4 + (after is not None)
    return pl.pallas_call(
        body, grid=(r // tr,),
        in_specs=[pl.BlockSpec((N_DEV, tr, c), lambda i: (0, i, 0)), blk(), blk(), blk()] + [_ANY] * len(extra),
        out_specs=[blk(), blk(), blk(), blk()],
        out_shape=[jax.ShapeDtypeStruct((nl, r, c), F32)] * 4,
        input_output_aliases={} if prev is None else {first_prev + j: j for j in range(4)},
        compiler_params=_cp("parallel"), name=name)(slots, w, m, v, *extra)


BIG = ("ffn1_w_in", "ffn1_w_out", "w_in", "gdn_conv", "w_out", "ffn2_w_in", "ffn2_w_out")
GROUPS = (BIG[0:2], BIG[2:5], BIG[5:7])
SMALL = ("ffn1_norm", "mix_norm", "fox_q_norm", "fox_k_norm", "fox_f_bias", "gdn_a_log", "gdn_dt_bias",
         "gdn_out_norm", "ffn2_norm")
WEIGHTS = ("ffn1_norm", "ffn1_w_in", "ffn1_w_out", "mix_norm", "w_in", "fox_q_norm", "fox_k_norm", "fox_f_bias",
           "gdn_conv", "gdn_a_log", "gdn_dt_bias", "gdn_out_norm", "w_out", "ffn2_norm", "ffn2_w_in", "ffn2_w_out")
IN_COLS = (("fq", 512), ("fk", 512), ("fv", 512), ("ff", 8), ("gq", 512), ("gk", 512), ("gv", 512),
           ("ga", 4), ("gb", 4), ("gg", 512))
MY_BIG = ("fq", "fk", "fv", "gq", "gk", "gv", "gg")
MY_SMALL = ("ff", "ga", "gb")
SMALL_ROWS = 8 * 128


def _in_cols_to_mine(w):
    off, parts = 0, {}
    for nm, wd in IN_COLS:
        parts[nm] = w[:, off:off + wd]
        off += wd
    small = jnp.concatenate([parts[nm] for nm in MY_SMALL], axis=1)
    small = jnp.pad(small, ((0, 0), (0, 128 - small.shape[1])))
    return jnp.concatenate([parts[nm] for nm in MY_BIG] + [small], axis=1)


def _in_cols_from_mine(g):
    parts = {nm: g[:, i * 512:(i + 1) * 512] for i, nm in enumerate(MY_BIG)}
    off = N_BIG
    for nm in MY_SMALL:
        wd = dict(IN_COLS)[nm]
        parts[nm] = g[:, off:off + wd]
        off += wd
    return jnp.concatenate([parts[nm] for nm, _ in IN_COLS], axis=1)


def _pack_small(vals):
    rows = []
    nl = vals[SMALL[0]].shape[0]
    for l in range(nl):
        for nm in SMALL:
            v = vals[nm][l].reshape(-1)
            pad = (-v.shape[0]) % SMALL_ROWS
            rows.append(jnp.pad(v, (0, pad)).reshape(-1, 128))
    return jnp.concatenate(rows, axis=0)


def _unpack_small(packed, like):
    out = {nm: [] for nm in SMALL}
    row = 0
    nl = like[SMALL[0]].shape[0]
    for l in range(nl):
        for nm in SMALL:
            n = like[nm].shape[1]
            nr = -(-n // SMALL_ROWS) * 8
            out[nm].append(packed[row:row + nr].reshape(-1)[:n])
            row += nr
    return {nm: jnp.stack(v) for nm, v in out.items()}


def kernel(x, ffn1_norm, ffn1_w_in, ffn1_w_out, mix_norm, w_in, fox_q_norm, fox_k_norm, fox_f_bias, gdn_conv, gdn_a_log, gdn_dt_bias, gdn_out_norm, w_out, ffn2_norm, ffn2_w_in, ffn2_w_out, loss_target, m_ffn1_norm, m_ffn1_w_in, m_ffn1_w_out, m_mix_norm, m_w_in, m_fox_q_norm, m_fox_k_norm, m_fox_f_bias, m_gdn_conv, m_gdn_a_log, m_gdn_dt_bias, m_gdn_out_norm, m_w_out, m_ffn2_norm, m_ffn2_w_in, m_ffn2_w_out, v_ffn1_norm, v_ffn1_w_in, v_ffn1_w_out, v_mix_norm, v_w_in, v_fox_q_norm, v_fox_k_norm, v_fox_f_bias, v_gdn_conv, v_gdn_a_log, v_gdn_dt_bias, v_gdn_out_norm, v_w_out, v_ffn2_norm, v_ffn2_w_in, v_ffn2_w_out):
    wts = dict(ffn1_norm=ffn1_norm, ffn1_w_in=ffn1_w_in, ffn1_w_out=ffn1_w_out, mix_norm=mix_norm, w_in=w_in,
               fox_q_norm=fox_q_norm, fox_k_norm=fox_k_norm, fox_f_bias=fox_f_bias, gdn_conv=gdn_conv,
               gdn_a_log=gdn_a_log, gdn_dt_bias=gdn_dt_bias, gdn_out_norm=gdn_out_norm, w_out=w_out,
               ffn2_norm=ffn2_norm, ffn2_w_in=ffn2_w_in, ffn2_w_out=ffn2_w_out)
    mom = dict(ffn1_norm=m_ffn1_norm, ffn1_w_in=m_ffn1_w_in, ffn1_w_out=m_ffn1_w_out, mix_norm=m_mix_norm, w_in=m_w_in,
               fox_q_norm=m_fox_q_norm, fox_k_norm=m_fox_k_norm, fox_f_bias=m_fox_f_bias, gdn_conv=m_gdn_conv,
               gdn_a_log=m_gdn_a_log, gdn_dt_bias=m_gdn_dt_bias, gdn_out_norm=m_gdn_out_norm, w_out=m_w_out,
               ffn2_norm=m_ffn2_norm, ffn2_w_in=m_ffn2_w_in, ffn2_w_out=m_ffn2_w_out)
    var = dict(ffn1_norm=v_ffn1_norm, ffn1_w_in=v_ffn1_w_in, ffn1_w_out=v_ffn1_w_out, mix_norm=v_mix_norm, w_in=v_w_in,
               fox_q_norm=v_fox_q_norm, fox_k_norm=v_fox_k_norm, fox_f_bias=v_fox_f_bias, gdn_conv=v_gdn_conv,
               gdn_a_log=v_gdn_a_log, gdn_dt_bias=v_gdn_dt_bias, gdn_out_norm=v_gdn_out_norm, w_out=v_w_out,
               ffn2_norm=v_ffn2_norm, ffn2_w_in=v_ffn2_w_in, ffn2_w_out=v_ffn2_w_out)
    nb, seq, d = x.shape
    t = nb * seq
    depth = ffn1_norm.shape[0]

    stages = [(l, gi) for l in range(depth) for gi in range(len(GROUPS))]

    def shards_of(l, gi):
        return [wts[nm][l] if nm == "gdn_conv" else wts[nm][l].astype(BF16) for nm in GROUPS[gi]]

    def behind(nw, token):
        return nw if token is None else nw + token[0:1, 0:1]

    def small_params(l):
        return dict(
            n1=ffn1_norm[l][None], nmix=mix_norm[l][None], n2=ffn2_norm[l][None],
            qw=fox_q_norm[l][None], kw=fox_k_norm[l][None], onw=gdn_out_norm[l][None],
            gp=jnp.concatenate([
                jnp.concatenate([fox_f_bias[l], gdn_dt_bias[l], jnp.zeros((116,), F32)])[None],
                jnp.concatenate([jnp.zeros((8,), F32), gdn_a_log[l], jnp.zeros((116,), F32)])[None],
                jnp.zeros((6, 128), F32)], axis=0))

    h = x.reshape(t, d)
    state, token = exchange_begin(shards_of(0, 0), [False] * len(GROUPS[0]), "gather_0", ffn1_norm)
    landed = exchange_end(state, token)
    saved = [dict(p=small_params(l)) for l in range(depth)]
    for k, (l, gi) in enumerate(stages):
        s, w, token = saved[l], landed, None
        p = s["p"]
        if k + 1 < len(stages):
            nl, ng = stages[k + 1]
            state, token = exchange_begin(shards_of(nl, ng), [False] * len(GROUPS[ng]), f"gather_{k + 1}", landed[0])
        if gi == 0:
            fb = w[0].shape[2]
            p["w1i"], p["w1o"] = w[0].reshape(2, 4, d, fb), w[1].reshape(4, fb, d)
            s["x0"] = h
            h, *s["ffn1"] = ffn_fwd(h, behind(p["n1"], token), p["w1i"], p["w1o"])
            s["x1"] = h
        elif gi == 1:
            p["wi"] = _in_cols_to_mine(w[0].transpose(1, 0, 2).reshape(d, -1))
            p["cw"] = w[1].transpose(1, 0, 2).reshape(CONV_W, -1)
            p["wo"] = w[2].reshape(d, d)
            proj, hn = inproj_fwd(h, behind(p["nmix"], token), p["wi"])
            gates = gates_fwd(proj, p["gp"], seq)
            yf, lse = attn_fwd(proj, gates, p["qw"], p["kw"], seq, tq=min(seq, ATTN_TQ_FWD))
            qh, kh, vh = gdn_pre_fwd(proj, p["cw"], seq)
            inv = gdn_inv(kh, gates, seq)
            yg, st = gdn_fwd(qh, kh, vh, gates, proj, p["onw"], inv, seq)
            h, ycat = outproj_fwd(h, yf, yg, p["wo"])
            s.update(x2=h, proj=proj, hn=hn, gates=gates, yf=yf, lse=lse, qh=qh, kh=kh, vh=vh, st=st, inv=inv, ycat=ycat)
        else:
            fb = w[0].shape[2]
            p["w2i"], p["w2o"] = w[0].reshape(2, 4, d, fb), w[1].reshape(4, fb, d)
            h, *s["ffn2"] = ffn_fwd(h, behind(p["n2"], token), p["w2i"], p["w2o"])
        if k + 1 < len(stages):
            landed = exchange_end(state, h)

    sq, dh = loss_head(h, loss_target.reshape(t, d))
    loss = lax.psum(0.5 * jnp.sum(sq) / d, ("x", "y", "c"))

    got = [None] * len(stages)
    pending, token = None, None
    gsmall = {nm: [None] * depth for nm in SMALL}
    for k in reversed(range(len(stages))):
        l, gi = stages[k]
        s = saved[l]
        p = s["p"]
        if gi != 1:
            nw, xin, wi_, wo_, nm_n, (xn, gu, hh) = (
                (p["n1"], s["x0"], p["w1i"], p["w1o"], "ffn1_norm", s["ffn1"]) if gi == 0 else
                (p["n2"], s["x2"], p["w2i"], p["w2o"], "ffn2_norm", s["ffn2"]))
            dh, dn, dgu, dyh = ffn_bwd(xin, dh, behind(nw, token), gu, wi_, wo_)
            g_in, g_out = wgrad_ffn_in(xn, dgu), wgrad_ffn_out(hh, dyh)
            send = [g_in.reshape(N_DEV, d, g_in.shape[3]), g_out.reshape(N_DEV, -1, d)]
            gsmall[nm_n][l] = dn[0]
        else:
            dyf, dyg, dyb = outproj_bwd(dh, p["wo"], token)
            g_wo = wgrad_2d(s["ycat"], dyb, 512, "wgrad_w_out")
            dq, dk, dv, dga, dqw, dkw = attn_bwd(s["proj"], s["gates"], p["qw"], p["kw"], s["yf"], s["lse"], dyf, seq,
                                                 tq=min(seq, ATTN_TQ_BWD))
            dqh, dkh, dvh, dgg, dgb, donw = gdn_bwd(s["qh"], s["kh"], s["vh"], s["gates"], s["proj"], p["onw"],
                                                     s["inv"], s["st"], dyg, seq)
            dxq, dxk, dxv, dwq, dwk, dwv = gdn_pre_bwd(s["proj"], p["cw"], dqh, dkh, dvh, seq)
            dsm, dgp = gates_bwd(s["proj"], p["gp"], dga, dgb, seq)
            dh, dnmix, dproj = inproj_bwd(s["x1"], dh, p["nmix"], p["wi"], [dq, dk, dv, dxq, dxk, dxv, dgg, dsm])
            g_wi = wgrad_2d(s["hn"], dproj, 512, "wgrad_w_in", F32)
            g_cw = jnp.concatenate([dwq, dwk, dwv], axis=1)
            send = [_in_cols_from_mine(g_wi).reshape(d, N_DEV, -1).transpose(1, 0, 2),
                    g_cw.reshape(CONV_W, N_DEV, -1).transpose(1, 0, 2), g_wo.reshape(N_DEV, -1, d)]
            for nm, val in (("mix_norm", dnmix[0]), ("fox_q_norm", dqw[0]), ("fox_k_norm", dkw[0]),
                            ("fox_f_bias", dgp[0, 0:8]), ("gdn_a_log", dgp[1, 8:12]), ("gdn_dt_bias", dgp[0, 8:12]),
                            ("gdn_out_norm", donw[0])):
                gsmall[nm][l] = val
        flags = [True] * len(send)
        if k == 0:
            send.append(_pack_small({nm: jnp.stack(v) for nm, v in gsmall.items()}))
            flags.append(False)
        prev = dh
        if pending is not None:
            got[pending[1]] = exchange_end(pending[0], dh)
            prev = got[pending[1]][0]
        state, token = exchange_begin(send, flags, f"exchange_grads_{k}", prev)
        pending = (state, k)
    grad_x = dh.reshape(nb, seq, d)

    res = {}

    def update_stage(k, slots, after):
        l, gi = stages[k]
        for i, nm in enumerate(GROUPS[gi]):
            r, c = wts[nm].shape[1:]
            res[nm] = adamw_reduce(slots[i].reshape(N_DEV, r, c), wts[nm], mom[nm], var[nm], l, f"adamw_{nm}_{l}",
                                   after, res.get(nm))
            if after is not None:
                after = res[nm][0]
        return after

    last = token
    for k in range(1, len(stages)):
        last = update_stage(k, got[k], last)
    got[0] = exchange_end(pending[0], last)
    update_stage(0, got[0], None)
    small_like = {nm: wts[nm] for nm in SMALL}
    sm = adamw_reduce(got[0][-1], _pack_small(small_like)[None], _pack_small({nm: mom[nm] for nm in SMALL})[None],
                      _pack_small({nm: var[nm] for nm in SMALL})[None], 0, "adamw_small")
    sm = [_unpack_small(a[0], small_like) for a in sm]
    for nm in SMALL:
        res[nm] = [sm[j][nm] for j in range(4)]
    return (loss, grad_x, *[res[nm][0] for nm in WEIGHTS], *[res[nm][1] for nm in WEIGHTS],
            *[res[nm][2] for nm in WEIGHTS], *[res[nm][3] for nm in WEIGHTS])
```

```python
import jax
import jax.numpy as jnp
from jax import lax
from jax.experimental import pallas as pl
from jax.experimental.pallas import tpu as pltpu

F32 = jnp.float32
BF16 = jnp.bfloat16
EPS = 1e-6
N_DEV = 8
MESH = pl.DeviceIdType.MESH
HIGHEST = lax.Precision.HIGHEST
VMEM_LIMIT = 56 * 1024 * 1024

FOX_HEADS, FOX_DH = 8, 64
GDN_HEADS, GDN_DH = 4, 128
CHUNK = 64
CONV_W = 4

ADAM_LR, ADAM_B1, ADAM_B2, ADAM_EPS, ADAM_WD, ADAM_STEP = 0.001, 0.9, 0.999, 1e-08, 0.01, 10


def _cp(*sem):
    return pltpu.CompilerParams(dimension_semantics=sem, vmem_limit_bytes=VMEM_LIMIT)


def _dot(a, b):
    return jnp.dot(a, b, preferred_element_type=F32)


def _dot_nt(a, b):
    return lax.dot_general(a, b, (((1,), (1,)), ((), ())), preferred_element_type=F32)


def _dot_tn(a, b):
    return lax.dot_general(a, b, (((0,), (0,)), ((), ())), preferred_element_type=F32)


def _rstd(xf):
    return lax.rsqrt(jnp.mean(xf * xf, axis=-1, keepdims=True) + EPS)


def _rms_bwd(xf, r, dyn):
    return r * dyn - xf * (r * r * r) * jnp.mean(dyn * xf, axis=-1, keepdims=True)


def ffn_fwd(x, nw, w_in, w_out, tm=1024, rc=1024):
    t, d = x.shape
    nj, fb = w_out.shape[0], w_out.shape[1]
    tm = min(tm, t)
    rc = min(rc, tm)

    def body(x_ref, nw_ref, wi_ref, wo_ref, o_ref, xn_ref, gu_ref, h_ref, acc_ref):
        j = pl.program_id(1)

        @pl.when(j == 0)
        def _():
            xf = x_ref[...]
            xn_ref[...] = (xf * _rstd(xf) * nw_ref[...]).astype(BF16)
            acc_ref[...] = jnp.zeros_like(acc_ref)

        rows = [slice(c * rc, (c + 1) * rc) for c in range(tm // rc)]
        gs = [_dot(xn_ref[r, :], wi_ref[0]) for r in rows]
        us = [_dot(xn_ref[r, :], wi_ref[1]) for r in rows]
        hs = []
        for g, u, r in zip(gs, us, rows):
            sg = jax.nn.sigmoid(g)
            silu = g * sg
            h = (silu * u).astype(BF16)
            gu_ref[0, r, :] = (u * (sg * (1.0 + g * (1.0 - sg)))).astype(BF16)
            gu_ref[1, r, :] = silu.astype(BF16)
            h_ref[r, :] = h
            hs.append(h)
        for h, r in zip(hs, rows):
            acc_ref[r, :] += _dot(h, wo_ref[...])

        @pl.when(j == nj - 1)
        def _():
            o_ref[...] = x_ref[...] + 0.5 * acc_ref[...]

    return pl.pallas_call(
        body, grid=(t // tm, nj),
        in_specs=[pl.BlockSpec((tm, d), lambda i, j: (i, 0)),
                  pl.BlockSpec((1, d), lambda i, j: (0, 0)),
                  pl.BlockSpec((2, None, d, fb), lambda i, j: (0, j, 0, 0)),
                  pl.BlockSpec((None, fb, d), lambda i, j: (j, 0, 0))],
        out_specs=[pl.BlockSpec((tm, d), lambda i, j: (i, 0)),
                   pl.BlockSpec((tm, d), lambda i, j: (i, 0)),
                   pl.BlockSpec((2, None, tm, fb), lambda i, j: (0, j, i, 0)),
                   pl.BlockSpec((None, tm, fb), lambda i, j: (j, i, 0))],
        out_shape=[jax.ShapeDtypeStruct((t, d), F32), jax.ShapeDtypeStruct((t, d), BF16),
                   jax.ShapeDtypeStruct((2, nj, t, fb), BF16), jax.ShapeDtypeStruct((nj, t, fb), BF16)],
        scratch_shapes=[pltpu.VMEM((tm, d), F32)],
        compiler_params=_cp("parallel", "arbitrary"), name="ffn_fwd")(x, nw, w_in, w_out)


def ffn_bwd(x, dy, nw, gu, w_in, w_out, tm=512, rc=256):
    t, d = x.shape
    nj, fb = w_out.shape[0], w_out.shape[1]
    tm = min(tm, t)
    rc = min(rc, tm)

    def body(x_ref, dy_ref, nw_ref, gu_ref, wi_ref, wo_ref,
             dx_ref, dnw_ref, dgu_ref, dyh_ref, acc_ref):
        i, j = pl.program_id(0), pl.program_id(1)

        @pl.when(j == 0)
        def _():
            dyh_ref[...] = (0.5 * dy_ref[...]).astype(BF16)
            acc_ref[...] = jnp.zeros_like(acc_ref)

        @pl.when((i == 0) & (j == 0))
        def _():
            dnw_ref[...] = jnp.zeros_like(dnw_ref)

        rows = [slice(c * rc, (c + 1) * rc) for c in range(tm // rc)]
        dhs = [_dot_nt(dyh_ref[r, :], wo_ref[...]) for r in rows]
        dgs = [(dh * gu_ref[0, r, :].astype(F32)).astype(BF16) for dh, r in zip(dhs, rows)]
        dus = [(dh * gu_ref[1, r, :].astype(F32)).astype(BF16) for dh, r in zip(dhs, rows)]
        for dg, du, r in zip(dgs, dus, rows):
            dgu_ref[0, r, :] = dg
            dgu_ref[1, r, :] = du
        for dg, du, r in zip(dgs, dus, rows):
            acc_ref[r, :] += _dot_nt(dg, wi_ref[0]) + _dot_nt(du, wi_ref[1])

        @pl.when(j == nj - 1)
        def _():
            xf = x_ref[...]
            r = _rstd(xf)
            dxn = acc_ref[...]
            dnw_ref[...] += jnp.sum(dxn * xf * r, axis=0, keepdims=True)
            dx_ref[...] = _rms_bwd(xf, r, dxn * nw_ref[...]) + dy_ref[...]

    return pl.pallas_call(
        body, grid=(t // tm, nj),
        in_specs=[pl.BlockSpec((tm, d), lambda i, j: (i, 0)),
                  pl.BlockSpec((tm, d), lambda i, j: (i, 0)),
                  pl.BlockSpec((1, d), lambda i, j: (0, 0)),
                  pl.BlockSpec((2, None, tm, fb), lambda i, j: (0, j, i, 0)),
                  pl.BlockSpec((2, None, d, fb), lambda i, j: (0, j, 0, 0)),
                  pl.BlockSpec((None, fb, d), lambda i, j: (j, 0, 0))],
        out_specs=[pl.BlockSpec((tm, d), lambda i, j: (i, 0)),
                   pl.BlockSpec((1, d), lambda i, j: (0, 0)),
                   pl.BlockSpec((2, None, tm, fb), lambda i, j: (0, j, i, 0)),
                   pl.BlockSpec((tm, d), lambda i, j: (i, 0))],
        out_shape=[jax.ShapeDtypeStruct((t, d), F32),
                   jax.ShapeDtypeStruct((1, d), F32),
                   jax.ShapeDtypeStruct((2, nj, t, fb), BF16),
                   jax.ShapeDtypeStruct((t, d), BF16)],
        scratch_shapes=[pltpu.VMEM((tm, d), F32)],
        compiler_params=_cp("arbitrary", "arbitrary"), name="ffn_bwd")(x, dy, nw, gu, w_in, w_out)


def _wgrad_call(a, b, a_spec, b_spec, out_shape, out_spec, grid, name, out_dtype=BF16):
    last = len(grid) - 1
    acc_shape = tuple(s for s in out_spec.block_shape if s is not None)

    def body(a_ref, b_ref, o_ref, acc_ref):
        @pl.when(pl.program_id(last) == 0)
        def _():
            acc_ref[...] = jnp.zeros_like(acc_ref)

        if len(acc_shape) == 3:
            shared_a = a_ref[...] if len(a_ref.shape) == 2 else None
            shared_b = b_ref[...] if len(b_ref.shape) == 2 else None
            for s in range(acc_shape[0]):
                acc_ref[s] += _dot_tn(a_ref[s] if shared_a is None else shared_a,
                                      b_ref[s] if shared_b is None else shared_b)
        else:
            acc_ref[...] += _dot_tn(a_ref[...], b_ref[...])

        @pl.when(pl.program_id(last) == grid[last] - 1)
        def _():
            o_ref[...] = acc_ref[...].astype(o_ref.dtype)

    sem = ("parallel",) * last + ("arbitrary",)
    return pl.pallas_call(body, grid=grid, in_specs=[a_spec, b_spec], out_specs=out_spec,
                          out_shape=jax.ShapeDtypeStruct(out_shape, out_dtype),
                          scratch_shapes=[pltpu.VMEM(acc_shape, F32)],
                          compiler_params=_cp(*sem), name=name)(a, b)


WGRAD_TM = 1024


def wgrad_ffn_in(xn, dgu, tm=WGRAD_TM):
    t, d = xn.shape
    _, nj, _, fb = dgu.shape
    tm = min(tm, t)
    return _wgrad_call(xn, dgu,
                       pl.BlockSpec((tm, d), lambda j, k: (k, 0)),
                       pl.BlockSpec((2, None, tm, fb), lambda j, k: (0, j, k, 0)),
                       (2, nj, d, fb), pl.BlockSpec((2, None, d, fb), lambda j, k: (0, j, 0, 0)),
                       (nj, t // tm), "wgrad_ffn_in")


def wgrad_ffn_out(h, dyh, tm=WGRAD_TM):
    nj, t, fb = h.shape
    d = dyh.shape[1]
    tm = min(tm, t)
    return _wgrad_call(h, dyh,
                       pl.BlockSpec((nj, tm, fb), lambda k: (0, k, 0)),
                       pl.BlockSpec((tm, d), lambda k: (k, 0)),
                       (nj, fb, d), pl.BlockSpec((nj, fb, d), lambda k: (0, 0, 0)),
                       (t // tm,), "wgrad_ffn_out")


def wgrad_2d(a, b, tk, name, out_dtype=BF16, tm=WGRAD_TM):
    t, k = a.shape
    n = b.shape[1]
    tm = min(tm, t)
    return _wgrad_call(a, b,
                       pl.BlockSpec((tm, tk), lambda c, s: (s, c)),
                       pl.BlockSpec((tm, n), lambda c, s: (s, 0)),
                       (k, n), pl.BlockSpec((tk, n), lambda c, s: (c, 0)),
                       (k // tk, t // tm), name, out_dtype)


N_BIG = 7 * 512
N_PROJ = N_BIG + 128
COL_SMALL = N_BIG // 128


def inproj_fwd(x, nw, w, tm=512):
    t, d = x.shape
    n = w.shape[1]

    def body(x_ref, nw_ref, w_ref, p_ref, hn_ref):
        xf = x_ref[...]
        hn = (xf * _rstd(xf) * nw_ref[...]).astype(BF16)
        hn_ref[...] = hn
        p_ref[...] = _dot(hn, w_ref[...])

    return pl.pallas_call(
        body, grid=(t // tm,),
        in_specs=[pl.BlockSpec((tm, d), lambda i: (i, 0)), pl.BlockSpec((1, d), lambda i: (0, 0)),
                  pl.BlockSpec((d, n), lambda i: (0, 0))],
        out_specs=[pl.BlockSpec((tm, n), lambda i: (i, 0)), pl.BlockSpec((tm, d), lambda i: (i, 0))],
        out_shape=[jax.ShapeDtypeStruct((t, n), F32), jax.ShapeDtypeStruct((t, d), BF16)],
        compiler_params=_cp("parallel"), name="inproj_fwd")(x, nw, w)


def inproj_bwd(x, dres, nw, w, dparts, tm=512):
    t, d = x.shape
    n = w.shape[1]
    widths = [p.shape[1] for p in dparts]
    assert sum(widths) == n

    def body(x_ref, dres_ref, nw_ref, w_ref, *rest):
        part_refs, (dx_ref, dnw_ref, dp_ref) = rest[:len(widths)], rest[len(widths):]

        @pl.when(pl.program_id(0) == 0)
        def _():
            dnw_ref[...] = jnp.zeros_like(dnw_ref)

        dp = jnp.concatenate([r[...].astype(BF16) for r in part_refs], axis=1)
        dp_ref[...] = dp
        dhn = _dot_nt(dp, w_ref[...])
        xf = x_ref[...]
        r = _rstd(xf)
        dnw_ref[...] += jnp.sum(dhn * xf * r, axis=0, keepdims=True)
        dx_ref[...] = _rms_bwd(xf, r, dhn * nw_ref[...]) + dres_ref[...]

    return pl.pallas_call(
        body, grid=(t // tm,),
        in_specs=[pl.BlockSpec((tm, d), lambda i: (i, 0)), pl.BlockSpec((tm, d), lambda i: (i, 0)),
                  pl.BlockSpec((1, d), lambda i: (0, 0)), pl.BlockSpec((d, n), lambda i: (0, 0))]
                 + [pl.BlockSpec((tm, wd), lambda i: (i, 0)) for wd in widths],
        out_specs=[pl.BlockSpec((tm, d), lambda i: (i, 0)), pl.BlockSpec((1, d), lambda i: (0, 0)),
                   pl.BlockSpec((tm, n), lambda i: (i, 0))],
        out_shape=[jax.ShapeDtypeStruct((t, d), F32), jax.ShapeDtypeStruct((1, d), F32),
                   jax.ShapeDtypeStruct((t, n), BF16)],
        compiler_params=_cp("arbitrary"), name="inproj_bwd")(x, dres, nw, w, *dparts)


def outproj_fwd(x, yf, yg, w, tm=1024):
    t, d = x.shape
    hw = yf.shape[1]
    tm = min(tm, t)

    def body(x_ref, yf_ref, yg_ref, w_ref, o_ref, y_ref):
        y = jnp.concatenate([yf_ref[...], yg_ref[...]], axis=1).astype(BF16)
        y_ref[...] = y
        o_ref[...] = x_ref[...] + _dot(y, w_ref[...])

    return pl.pallas_call(
        body, grid=(t // tm,),
        in_specs=[pl.BlockSpec((tm, d), lambda i: (i, 0)), pl.BlockSpec((tm, hw), lambda i: (i, 0)),
                  pl.BlockSpec((tm, hw), lambda i: (i, 0)), pl.BlockSpec((2 * hw, d), lambda i: (0, 0))],
        out_specs=[pl.BlockSpec((tm, d), lambda i: (i, 0)), pl.BlockSpec((tm, 2 * hw), lambda i: (i, 0))],
        out_shape=[jax.ShapeDtypeStruct((t, d), F32), jax.ShapeDtypeStruct((t, 2 * hw), BF16)],
        compiler_params=_cp("parallel"), name="outproj_fwd")(x, yf, yg, w)


def outproj_bwd(dy, w, after=None, tm=1024):
    t, d = dy.shape
    hw = w.shape[0] // 2
    tm = min(tm, t)
    extra = [] if after is None else [after]

    def body(dy_ref, w_ref, *rest):
        df_ref, dg_ref, dyb_ref = rest[-3:]
        dyb = dy_ref[...].astype(BF16)
        dyb_ref[...] = dyb
        dyy = _dot_nt(dyb, w_ref[...])
        df_ref[...] = dyy[:, :hw]
        dg_ref[...] = dyy[:, hw:]

    return pl.pallas_call(
        body, grid=(t // tm,),
        in_specs=[pl.BlockSpec((tm, d), lambda i: (i, 0)), pl.BlockSpec((2 * hw, d), lambda i: (0, 0))]
                 + [pl.BlockSpec(memory_space=pl.ANY)] * len(extra),
        out_specs=[pl.BlockSpec((tm, hw), lambda i: (i, 0)), pl.BlockSpec((tm, hw), lambda i: (i, 0)),
                   pl.BlockSpec((tm, d), lambda i: (i, 0))],
        out_shape=[jax.ShapeDtypeStruct((t, hw), F32), jax.ShapeDtypeStruct((t, hw), F32),
                   jax.ShapeDtypeStruct((t, d), BF16)],
        compiler_params=_cp("parallel"), name="outproj_bwd")(dy, w, *extra)


def _lane(shape):
    return lax.broadcasted_iota(jnp.int32, shape, 1)


def _row(shape):
    return lax.broadcasted_iota(jnp.int32, shape, 0)


def _gate_terms(val, gp_ref):
    z = val + gp_ref[0:1, :]
    sp = jnp.log(1.0 + jnp.exp(-jnp.abs(z)))
    return z, sp


def gates_fwd(proj, gp, seq, ts=512):
    t = proj.shape[0]
    nb, ns = t // seq, seq // ts

    def body(v_ref, gp_ref, o_ref, carry_ref):
        @pl.when(pl.program_id(1) == 0)
        def _():
            carry_ref[...] = jnp.zeros_like(carry_ref)

        z, sp = _gate_terms(v_ref[...], gp_ref)
        logsig = jnp.minimum(z, 0.0) - sp
        tri = (_row((ts, ts)) >= _lane((ts, ts))).astype(F32)
        cum = jnp.dot(tri, logsig, precision=HIGHEST, preferred_element_type=F32) + carry_ref[0:1, :]
        carry_ref[0:1, :] = cum[ts - 1:ts, :]
        g = -jnp.exp(gp_ref[1:2, :]) * (jnp.maximum(z, 0.0) + sp)
        beta = jax.nn.sigmoid(z)
        lane = _lane((ts, 128))
        o_ref[...] = jnp.where(lane < 8, cum, jnp.where(lane < 12, g, jnp.where(lane < 16, beta, 0.0)))

    return pl.pallas_call(
        body, grid=(nb, ns),
        in_specs=[pl.BlockSpec((ts, 128), lambda b, s: (b * ns + s, COL_SMALL)),
                  pl.BlockSpec((8, 128), lambda b, s: (0, 0))],
        out_specs=pl.BlockSpec((ts, 128), lambda b, s: (b * ns + s, 0)),
        out_shape=jax.ShapeDtypeStruct((t, 128), F32),
        scratch_shapes=[pltpu.VMEM((8, 128), F32)],
        compiler_params=_cp("parallel", "arbitrary"), name="gates_fwd")(proj, gp)


def gates_bwd(proj, gp, dga, dgb, seq, ts=512):
    t = proj.shape[0]
    nb, ns = t // seq, seq // ts

    def body(v_ref, gp_ref, da_ref, db_ref, ds_ref, dgp_ref, carry_ref):
        @pl.when(pl.program_id(1) == 0)
        def _():
            carry_ref[...] = jnp.zeros_like(carry_ref)

        @pl.when((pl.program_id(0) == 0) & (pl.program_id(1) == 0))
        def _():
            dgp_ref[...] = jnp.zeros_like(dgp_ref)

        lane = _lane((ts, 128))
        dgate = jnp.where(lane < 8, da_ref[...], jnp.where(lane < 16, db_ref[...], 0.0))
        z, sp = _gate_terms(v_ref[...], gp_ref)
        triu = (_row((ts, ts)) <= _lane((ts, ts))).astype(F32)
        dlog = jnp.dot(triu, dgate, precision=HIGHEST, preferred_element_type=F32) + carry_ref[0:1, :]
        carry_ref[0:1, :] = dlog[0:1, :]
        sig = jax.nn.sigmoid(z)
        nea = -jnp.exp(gp_ref[1:2, :])
        g = nea * (jnp.maximum(z, 0.0) + sp)
        dz = jnp.where(lane < 8, dlog * (1.0 - sig),
                       jnp.where(lane < 12, dgate * nea * sig, dgate * sig * (1.0 - sig)))
        dz = jnp.where(lane < 16, dz, 0.0)
        ds_ref[...] = dz.astype(BF16)
        dgp_ref[0:1, :] += jnp.where(lane[0:1] < 12, jnp.sum(dz, axis=0, keepdims=True), 0.0)
        dgp_ref[1:2, :] += jnp.where((lane[0:1] >= 8) & (lane[0:1] < 12), jnp.sum(dgate * g, axis=0, keepdims=True), 0.0)

    rev = lambda b, s: (b * ns + (ns - 1 - s), 0)
    return pl.pallas_call(
        body, grid=(nb, ns),
        in_specs=[pl.BlockSpec((ts, 128), lambda b, s: (b * ns + (ns - 1 - s), COL_SMALL)),
                  pl.BlockSpec((8, 128), lambda b, s: (0, 0)),
                  pl.BlockSpec((ts, 128), rev), pl.BlockSpec((ts, 128), rev)],
        out_specs=[pl.BlockSpec((ts, 128), rev), pl.BlockSpec((8, 128), lambda b, s: (0, 0))],
        out_shape=[jax.ShapeDtypeStruct((t, 128), BF16), jax.ShapeDtypeStruct((8, 128), F32)],
        scratch_shapes=[pltpu.VMEM((8, 128), F32)],
        compiler_params=_cp("arbitrary", "arbitrary"), name="gates_bwd")(proj, gp, dga, dgb)


NEG = -1e30
ATTN_TQ_FWD = 1024
ATTN_TQ_BWD = 512


def _pick_lane(tile, idx):
    return jnp.sum(jnp.where(_lane(tile.shape) == idx, tile, 0.0), axis=1, keepdims=True)


def _row_to_col(row, n):
    return jnp.sum(jnp.where(_row((n, n)) == _lane((n, n)), row, 0.0), axis=1, keepdims=True)


def _rows(i, n):
    return pl.ds(pl.multiple_of(i * n, n), n)


LOG2E = 1.4426950408889634
LN2 = 0.6931471805599453


def _split_dot(x, mat, passes):
    total, rest = None, x
    for _ in range(passes):
        part = rest.astype(BF16)
        rest = rest - part.astype(F32)
        total = _dot(part, mat) if total is None else total + _dot(part, mat)
    return total


def _pair_mats(p, dh):
    r, l = _row((128, 128)), _lane((128, 128))
    same = (r < dh) == (l < dh)
    upper = (l >= dh).astype(jnp.int32)
    as_bf16 = lambda m: m.astype(BF16)
    return dict(own=as_bf16(same), other=as_bf16(jnp.logical_not(same)), pick_other=as_bf16(r == 2 * p + 1 - upper),
                swap=as_bf16(((r == 0) & (l >= dh)) | ((r == dh) & (l < dh))))


def _pair_rstd(x2, sel, dh):
    return lax.rsqrt(_split_dot(x2 * x2, sel["own"], 2) * (1.0 / dh) + EPS)


def _pair_aug(cols, n, dh):
    lane = _lane((n, 128))
    li = jnp.where(lane >= dh, lane - dh, lane)
    out = jnp.zeros((n, 128), F32)
    for i, c in enumerate(cols):
        out = jnp.where(li == i, c, out)
    return out


def _split3(x):
    hi = x.astype(BF16).astype(F32)
    mid = (x - hi).astype(BF16).astype(F32)
    return [hi, mid, (x - hi - mid).astype(BF16).astype(F32)]


def _once(shape, index_map):
    return pl.BlockSpec(shape, index_map, pipeline_mode=pl.Buffered(1))


def attn_fwd(proj, gates, qw, kw, seq, tq=256):
    t = proj.shape[0]
    nb, nq, dh = t // seq, seq // tq, FOX_DH
    scale = dh ** -0.5

    def body(q_ref, k_ref, v_ref, g_ref, qw_ref, kw_ref, y_ref, lse_ref, qs, ks, vs):
        p = pl.program_id(1)
        heads = range(2)
        low = _lane((tq, 128)) < dh
        sel = _pair_mats(p, dh)

        def prep(i, _):
            r = _rows(i, tq)
            q2, k2 = q_ref[r, :], k_ref[r, :]
            cc = _split_dot(g_ref[r, :], sel["pick_other"], 3) * LOG2E
            qn = q2 * _pair_rstd(q2, sel, dh) * qw_ref[...] * (scale * LOG2E)
            kn = k2 * _pair_rstd(k2, sel, dh) * kw_ref[...]
            qx = _pair_aug(_split3(cc) + [1.0, 1.0, 1.0], tq, dh)
            kx = _pair_aug([1.0, 1.0, 1.0] + _split3(-cc), tq, dh)
            for hh in heads:
                own = low if hh == 0 else jnp.logical_not(low)
                qs[hh, r, :] = jnp.where(own, qn, qx).astype(BF16)
                ks[hh, r, :] = jnp.where(own, kn, kx).astype(BF16)
            vs[r, :] = v_ref[r, :].astype(BF16)
            return 0

        lax.fori_loop(0, nq, prep, 0)

        def q_tile(i, _):
            r = _rows(i, tq)
            qt = [qs[hh, r, :] for hh in heads]

            def kv_step(j, carry, masked):
                kr = _rows(j, tq)
                vt = vs[kr, :]
                out = []
                for hh in heads:
                    m, l, acc = carry[hh]
                    s = _dot_nt(qt[hh], ks[hh, kr, :])
                    if masked:
                        s = jnp.where(_row((tq, tq)) >= _lane((tq, tq)), s, NEG)
                    m_new = jnp.maximum(m, jnp.max(s, axis=1, keepdims=True))
                    pe = jnp.exp2(s - m_new)
                    a = jnp.exp2(m - m_new)
                    out.append((m_new, a * l + jnp.sum(pe, axis=1, keepdims=True), a * acc + _dot(pe.astype(BF16), vt)))
                return tuple(out)

            one = (jnp.full((tq, 1), NEG, F32), jnp.zeros((tq, 1), F32), jnp.zeros((tq, 128), F32))
            carry = lax.fori_loop(0, i, lambda j, c: kv_step(j, c, False), (one, one))
            (m0, l0, acc0), (m1, l1, acc1) = kv_step(i, carry, True)
            y_ref[r, :] = jnp.where(low, acc0 / l0, acc1 / l1)
            lse_ref[r, :] = jnp.where(low, m0 + jnp.log2(l0), m1 + jnp.log2(l1))
            return 0

        lax.fori_loop(0, nq, q_tile, 0)

    blk = lambda off: _once((seq, 128), lambda b, p: (b, off + p))
    return pl.pallas_call(
        body, grid=(nb, 4),
        in_specs=[blk(0), blk(4), blk(8), _once((seq, 128), lambda b, p: (b, 0)),
                  pl.BlockSpec((1, 128), lambda b, p: (0, 0)), pl.BlockSpec((1, 128), lambda b, p: (0, 0))],
        out_specs=[pl.BlockSpec((seq, 128), lambda b, p: (b, p)), pl.BlockSpec((seq, 128), lambda b, p: (b, p))],
        out_shape=[jax.ShapeDtypeStruct((t, 512), F32), jax.ShapeDtypeStruct((t, 512), F32)],
        scratch_shapes=[pltpu.VMEM((2, seq, 128), BF16), pltpu.VMEM((2, seq, 128), BF16), pltpu.VMEM((seq, 128), BF16)],
        compiler_params=_cp("parallel", "arbitrary"),
        name="attn_fwd")(proj, proj, proj, gates, jnp.tile(qw, (1, 2)), jnp.tile(kw, (1, 2)))


def attn_bwd(proj, gates, qw, kw, y, lse, dy, seq, tq=256):
    t = proj.shape[0]
    nb, nq, dh = t // seq, seq // tq, FOX_DH
    scale = dh ** -0.5

    def body(q_ref, k_ref, v_ref, g_ref, qw_ref, kw_ref, y_ref, lse_ref, dy_ref,
             dq_ref, dk_ref, dv_ref, dg_ref, dqw_ref, dkw_ref,
             qs, ks, vs, dos, dsrow, dqa, dka):
        b, p = pl.program_id(0), pl.program_id(1)

        @pl.when((b == 0) & (p == 0))
        def _():
            dqw_ref[...] = jnp.zeros_like(dqw_ref)
            dkw_ref[...] = jnp.zeros_like(dkw_ref)

        @pl.when(p == 0)
        def _():
            dg_ref[...] = jnp.zeros_like(dg_ref)

        heads = range(2)
        low = _lane((tq, 128)) < dh
        sel = _pair_mats(p, dh)

        def prep(i, _):
            r = _rows(i, tq)
            q2, k2, dy2 = q_ref[r, :], k_ref[r, :], dy_ref[r, :]
            cc = _split_dot(g_ref[r, :], sel["pick_other"], 3) * LOG2E
            lse_x = _split_dot(lse_ref[r, :], sel["swap"], 3)
            delta_x = _split_dot(dy2 * y_ref[r, :], sel["other"], 2)
            qn = q2 * _pair_rstd(q2, sel, dh) * qw_ref[...] * (scale * LOG2E)
            kn = k2 * _pair_rstd(k2, sel, dh) * kw_ref[...]
            qx = _pair_aug(_split3(cc) + [1.0, 1.0, 1.0] + _split3(-lse_x), tq, dh)
            kx = _pair_aug([1.0, 1.0, 1.0] + _split3(-cc) + [1.0, 1.0, 1.0], tq, dh)
            vx = _pair_aug([1.0, 1.0, 1.0], tq, dh)
            dx = _pair_aug(_split3(-delta_x), tq, dh)
            for hh in heads:
                own = low if hh == 0 else jnp.logical_not(low)
                qs[hh, r, :] = jnp.where(own, qn, qx).astype(BF16)
                ks[hh, r, :] = jnp.where(own, kn, kx).astype(BF16)
                vs[hh, r, :] = jnp.where(own, v_ref[r, :], vx).astype(BF16)
                dos[hh, r, :] = jnp.where(own, dy2, dx).astype(BF16)
                dsrow[hh, r, :] = jnp.zeros((tq, 1), F32)
                dqa[hh, r, :] = jnp.zeros((tq, 128), F32)
            return 0

        lax.fori_loop(0, nq, prep, 0)

        def kv_tile(j, _):
            kr = _rows(j, tq)
            kt = [ks[hh, kr, :] for hh in heads]
            vt = [vs[hh, kr, :] for hh in heads]

            def q_step(i, carry, masked):
                r = _rows(i, tq)
                out = []
                for hh in heads:
                    dk, dv, dcr = carry[hh]
                    qt, dot = qs[hh, r, :], dos[hh, r, :]
                    s = _dot_nt(qt, kt[hh])
                    if masked:
                        s = jnp.where(_row((tq, tq)) >= _lane((tq, tq)), s, NEG)
                    pe = jnp.exp2(s)
                    ds = pe * _dot_nt(dot, vt[hh])
                    dsb = ds.astype(BF16)
                    dqa[hh, r, :] += _dot(dsb, kt[hh])
                    dsrow[hh, r, :] += jnp.sum(ds, axis=1, keepdims=True)
                    out.append((dk + _dot_tn(dsb, qt), dv + _dot_tn(pe.astype(BF16), dot),
                                dcr - jnp.sum(ds, axis=0, keepdims=True)))
                return tuple(out)

            one = (jnp.zeros((tq, 128), F32), jnp.zeros((tq, 128), F32), jnp.zeros((1, tq), F32))
            carry = q_step(j, (one, one), True)
            (dk0, dv0, dcr0), (dk1, dv1, dcr1) = lax.fori_loop(j + 1, nq, lambda i, c: q_step(i, c, False), carry)
            dka[kr, :] = jnp.where(low, dk0, dk1)
            dv_ref[kr, :] = jnp.where(low, dv0, dv1).astype(BF16)
            lane = _lane((tq, 128))
            dg_ref[kr, :] = jnp.where(lane == 2 * p, _row_to_col(dcr0, tq),
                                      jnp.where(lane == 2 * p + 1, _row_to_col(dcr1, tq), dg_ref[kr, :]))
            return 0

        lax.fori_loop(0, nq, kv_tile, 0)

        def post(i, _):
            r = _rows(i, tq)
            q2, k2 = q_ref[r, :], k_ref[r, :]
            rq, rk = _pair_rstd(q2, sel, dh), _pair_rstd(k2, sel, dh)
            dqn = jnp.where(low, dqa[0, r, :], dqa[1, r, :]) * scale
            dkn = dka[r, :] * LN2
            dqw_ref[...] += jnp.sum(dqn * q2 * rq, axis=0, keepdims=True)
            dkw_ref[...] += jnp.sum(dkn * k2 * rk, axis=0, keepdims=True)
            for x2, rr, dyn, o_ref in ((q2, rq, dqn * qw_ref[...], dq_ref), (k2, rk, dkn * kw_ref[...], dk_ref)):
                mean = _split_dot(dyn * x2, sel["own"], 2) * (1.0 / dh)
                o_ref[r, :] = (rr * dyn - x2 * (rr * rr * rr) * mean).astype(BF16)
            lane = _lane((tq, 128))
            dg_ref[r, :] += jnp.where(lane == 2 * p, dsrow[0, r, :], jnp.where(lane == 2 * p + 1, dsrow[1, r, :], 0.0))
            return 0

        lax.fori_loop(0, nq, post, 0)

    blk = lambda off: _once((seq, 128), lambda b, p: (b, off + p))
    own = lambda: _once((seq, 128), lambda b, p: (b, p))
    vec = lambda: pl.BlockSpec((1, 128), lambda b, p: (0, 0))
    res = pl.pallas_call(
        body, grid=(nb, 4),
        in_specs=[blk(0), blk(4), blk(8), _once((seq, 128), lambda b, p: (b, 0)), vec(), vec(), own(), own(), own()],
        out_specs=[own(), own(), own(), _once((seq, 128), lambda b, p: (b, 0)), vec(), vec()],
        out_shape=[jax.ShapeDtypeStruct((t, 512), BF16)] * 3
                  + [jax.ShapeDtypeStruct((t, 128), F32), jax.ShapeDtypeStruct((1, 128), F32), jax.ShapeDtypeStruct((1, 128), F32)],
        scratch_shapes=[pltpu.VMEM((2, seq, 128), BF16)] * 4
                       + [pltpu.VMEM((2, seq, 1), F32), pltpu.VMEM((2, seq, 128), F32), pltpu.VMEM((seq, 128), F32)],
        compiler_params=_cp("arbitrary", "arbitrary"),
        name="attn_bwd")(proj, proj, proj, gates, jnp.tile(qw, (1, 2)), jnp.tile(kw, (1, 2)), y, lse, dy)
    return list(res[:4]) + [res[4][:, :dh] + res[4][:, dh:], res[5][:, :dh] + res[5][:, dh:]]


def _silu_grad(c, sg):
    return sg * (1.0 + c * (1.0 - sg))


def _conv(x, w, n):
    row = _row(x.shape)
    c = x * w[CONV_W - 1:CONV_W, :]
    for k in range(CONV_W - 1):
        sh = CONV_W - 1 - k
        c = c + w[k:k + 1, :] * jnp.where(row >= sh, pltpu.roll(x, sh, 0), 0.0)
    return c


def gdn_pre_fwd(proj, cw, seq):
    t = proj.shape[0]
    nb = t // seq
    scale = GDN_DH ** -0.5

    def body(xq_ref, xk_ref, xv_ref, wq_ref, wk_ref, wv_ref, q_ref, k_ref, v_ref):
        def act(x_ref, w_ref):
            c = _conv(x_ref[...], w_ref[...], seq)
            return c * jax.nn.sigmoid(c)

        aq, ak = act(xq_ref, wq_ref), act(xk_ref, wk_ref)
        q_ref[...] = aq * lax.rsqrt(jnp.sum(aq * aq, axis=1, keepdims=True) + EPS) * scale
        k_ref[...] = ak * lax.rsqrt(jnp.sum(ak * ak, axis=1, keepdims=True) + EPS)
        v_ref[...] = act(xv_ref, wv_ref)

    xb = lambda off: pl.BlockSpec((seq, 128), lambda b, h: (b, off + h))
    wb = lambda off: pl.BlockSpec((CONV_W, 128), lambda b, h: (0, off + h))
    ob = lambda: pl.BlockSpec((seq, 128), lambda b, h: (b, h))
    return pl.pallas_call(
        body, grid=(nb, GDN_HEADS),
        in_specs=[xb(12), xb(16), xb(20), wb(0), wb(4), wb(8)],
        out_specs=[ob(), ob(), ob()],
        out_shape=[jax.ShapeDtypeStruct((t, 512), F32)] * 3,
        compiler_params=_cp("parallel", "parallel"), name="gdn_pre_fwd")(proj, proj, proj, cw, cw, cw)


def gdn_pre_bwd(proj, cw, dq, dk, dv, seq):
    t = proj.shape[0]
    nb = t // seq
    scale = GDN_DH ** -0.5

    def body(xq_ref, xk_ref, xv_ref, wq_ref, wk_ref, wv_ref, dq_ref, dk_ref, dv_ref,
             dxq_ref, dxk_ref, dxv_ref, dwq_ref, dwk_ref, dwv_ref):
        first = pl.program_id(1) == 0
        row = _row((seq, 128))

        def one(x_ref, w_ref, dy_ref, dx_ref, dw_ref, norm, sc):
            x, w = x_ref[...], w_ref[...]
            c = _conv(x, w, seq)
            sg = jax.nn.sigmoid(c)
            dy = dy_ref[...]
            if norm:
                a = c * sg
                rs = lax.rsqrt(jnp.sum(a * a, axis=1, keepdims=True) + EPS)
                dy = dy * sc
                da = rs * dy - a * (rs * rs * rs) * jnp.sum(dy * a, axis=1, keepdims=True)
            else:
                da = dy
            dc = da * _silu_grad(c, sg)
            dx = dc * w[CONV_W - 1:CONV_W, :]
            dws = [None] * CONV_W
            dws[CONV_W - 1] = jnp.sum(dc * x, axis=0, keepdims=True)
            for k in range(CONV_W - 1):
                sh = CONV_W - 1 - k
                dc_up = jnp.where(row < seq - sh, pltpu.roll(dc, seq - sh, 0), 0.0)
                dx = dx + w[k:k + 1, :] * dc_up
                dws[k] = jnp.sum(dc_up * x, axis=0, keepdims=True)
            dx_ref[...] = dx.astype(BF16)
            dwn = jnp.concatenate(dws, axis=0)

            @pl.when(first)
            def _():
                dw_ref[...] = dwn

            @pl.when(jnp.logical_not(first))
            def _():
                dw_ref[...] += dwn

        one(xq_ref, wq_ref, dq_ref, dxq_ref, dwq_ref, True, scale)
        one(xk_ref, wk_ref, dk_ref, dxk_ref, dwk_ref, True, 1.0)
        one(xv_ref, wv_ref, dv_ref, dxv_ref, dwv_ref, False, 1.0)

    xb = lambda off: pl.BlockSpec((seq, 128), lambda h, b: (b, off + h))
    wb = lambda off: pl.BlockSpec((CONV_W, 128), lambda h, b: (0, off + h))
    ob = lambda: pl.BlockSpec((seq, 128), lambda h, b: (b, h))
    return pl.pallas_call(
        body, grid=(GDN_HEADS, nb),
        in_specs=[xb(12), xb(16), xb(20), wb(0), wb(4), wb(8), ob(), ob(), ob()],
        out_specs=[ob(), ob(), ob()] + [pl.BlockSpec((CONV_W, 128), lambda h, b: (0, h))] * 3,
        out_shape=[jax.ShapeDtypeStruct((t, 512), BF16)] * 3 + [jax.ShapeDtypeStruct((CONV_W, 512), F32)] * 3,
        compiler_params=_cp("parallel", "arbitrary"), name="gdn_pre_bwd")(proj, proj, proj, cw, cw, cw, dq, dk, dv)


def _b16(x):
    return x.astype(BF16)


@jax.custom_vjp
def _mm(a, b):
    return _dot(_b16(a), _b16(b))


_mm.defvjp(lambda a, b: (_mm(a, b), (a, b)),
           lambda res, g: (_dot_nt(_b16(g), _b16(res[1])), _dot_tn(_b16(res[0]), _b16(g))))


@jax.custom_vjp
def _mm_nt(a, b):
    return _dot_nt(_b16(a), _b16(b))


_mm_nt.defvjp(lambda a, b: (_mm_nt(a, b), (a, b)),
              lambda res, g: (_dot(_b16(g), _b16(res[1])), _dot_tn(_b16(g), _b16(res[0]))))


@jax.custom_vjp
def _mm_tn(a, b):
    return _dot_tn(_b16(a), _b16(b))


_mm_tn.defvjp(lambda a, b: (_mm_tn(a, b), (a, b)),
              lambda res, g: (_dot_nt(_b16(res[1]), _b16(g)), _dot(_b16(res[0]), _b16(g))))


def _dot32(a, b, dims=(((1,), (0,)), ((), ()))):
    def split(x):
        hi = x.astype(BF16)
        return hi, (x - hi.astype(F32)).astype(BF16)

    (ah, al), (bh, bl) = split(a), split(b)
    d = lambda x, y: lax.dot_general(x, y, dims, preferred_element_type=F32)
    return d(ah, bh) + (d(ah, bl) + d(al, bh))


def _inv_fwd_many(mats):
    n = mats[0].shape[0]
    eye = (_row((n, n)) == _lane((n, n))).astype(F32)
    invs, pws = [eye - a for a in mats], list(mats)
    for _ in range(n.bit_length() - 2):
        pws = [_dot32(pw, pw) for pw in pws]
        invs = [inv + _dot32(inv, pw) for inv, pw in zip(invs, pws)]
    return invs


@jax.custom_vjp
def _inv_saved(a, inv):
    return inv


def _inv_saved_bwd(inv, g):
    tg = _dot32(inv, g, (((0,), (0,)), ((), ())))
    return -_dot32(tg, inv, (((1,), (1,)), ((), ()))), jnp.zeros_like(inv)


_inv_saved.defvjp(lambda a, inv: (inv, inv), _inv_saved_bwd)


def _gdn_decay(gcol):
    c = CHUNK
    ri, ci = _row((c, c)), _lane((c, c))
    incl, eye = ri >= ci, ri == ci
    grow = jnp.sum(jnp.where(eye, gcol, 0.0), axis=0, keepdims=True)
    gc = jnp.sum(jnp.where(incl, grow, 0.0), axis=1, keepdims=True)
    gcr = jnp.sum(jnp.where(eye, gc, 0.0), axis=0, keepdims=True)
    gl = jnp.sum(jnp.where(_row((c, 1)) == c - 1, gc, 0.0), axis=0, keepdims=True)
    return gc, gl, jnp.exp(jnp.where(incl, gc - gcr, NEG))


def _gdn_a(k, bcol, decay):
    c = CHUNK
    return jnp.where(_row((c, c)) > _lane((c, c)), _mm_nt(k * bcol, k) * decay, 0.0)


def _gdn_chunk(q, k, v, gcol, bcol, state, gg, nw, inv_saved):
    c = CHUNK
    incl = _row((c, c)) >= _lane((c, c))
    gc, gl, decay = _gdn_decay(gcol)
    kb, vb = k * bcol, v * bcol
    inv = _inv_saved(_gdn_a(k, bcol, decay), inv_saved)
    eg = jnp.exp(gc)
    u = _mm(inv, vb)
    w = _mm(inv, kb * eg)
    pm = jnp.where(incl, _mm_nt(q, k) * decay, 0.0)
    kd = k * jnp.exp(gl - gc)
    qd = q * eg
    v_new = u - _mm(w, state)
    o = _mm(qd, state) + _mm(pm, v_new)
    state_new = state * jnp.exp(gl) + _mm_tn(kd, v_new)
    y = o * _rstd(o) * nw * (gg * jax.nn.sigmoid(gg))
    return y, state_new


_gdn_chunks = jax.vmap(_gdn_chunk, in_axes=(0, 0, 0, 0, 0, 0, 0, None, 0))


def _gdn_chain_inputs(chains, p, r, c, q_ref, k_ref, v_ref, g_ref, gg_ref, inv_ref):
    cols = {nm: [] for nm in ("q", "k", "v", "g", "b", "gg", "inv")}
    for b, hh in chains:
        h = GDN_HPS * p + hh
        ln = slice(hh * 128, (hh + 1) * 128)
        gt = g_ref[b, r, :]
        cols["q"].append(q_ref[b, r, ln])
        cols["k"].append(k_ref[b, r, ln])
        cols["v"].append(v_ref[b, r, ln])
        cols["g"].append(_pick_lane(gt, 8 + h))
        cols["b"].append(_pick_lane(gt, 12 + h))
        cols["gg"].append(gg_ref[b, r, ln])
        cols["inv"].append(inv_ref[b, hh, c])
    return [jnp.stack(cols[nm]) for nm in ("q", "k", "v", "g", "b", "gg", "inv")]


GDN_CB = 8
GDN_INV_CB = 16
GDN_HPS = 4


def gdn_inv(k, gates, seq):
    t = k.shape[0]
    nb, nc = t // seq, seq // CHUNK
    cb = min(GDN_INV_CB, nc)
    rb = cb * CHUNK
    nsb = seq // rb

    def body(k_ref, g_ref, o_ref):
        h = pl.program_id(1)
        mats = []
        for c in range(cb):
            r = slice(c * CHUNK, (c + 1) * CHUNK)
            gt = g_ref[r, :]
            _, _, decay = _gdn_decay(_pick_lane(gt, 8 + h))
            mats.append(_gdn_a(k_ref[r, :], _pick_lane(gt, 12 + h), decay))
        for c, inv in enumerate(_inv_fwd_many(mats)):
            o_ref[c] = inv

    return pl.pallas_call(
        body, grid=(nb, GDN_HEADS, nsb),
        in_specs=[pl.BlockSpec((rb, 128), lambda b, h, s: (b * nsb + s, h)),
                  pl.BlockSpec((rb, 128), lambda b, h, s: (b * nsb + s, 0))],
        out_specs=pl.BlockSpec((None, None, cb, CHUNK, CHUNK), lambda b, h, s: (b, h, s, 0, 0)),
        out_shape=jax.ShapeDtypeStruct((nb, GDN_HEADS, nc, CHUNK, CHUNK), F32),
        compiler_params=_cp("parallel", "parallel", "parallel"), name="gdn_inv")(k, gates)


def _gdn_specs(nb, nsb, cb, rev):
    blk = (lambda s: nsb - 1 - s) if rev else (lambda s: s)
    rb = cb * CHUNK
    pair = lambda off=0: pl.BlockSpec((nb, rb, 128 * GDN_HPS), lambda s, p: (0, blk(s), off + p))
    gate = lambda: pl.BlockSpec((nb, rb, 128), lambda s, p: (0, blk(s), 0))
    mats = lambda n: pl.BlockSpec((nb, GDN_HPS, cb, n, n), lambda s, p: (0, p, blk(s), 0, 0))
    return pair, gate, mats


def gdn_fwd(q, k, v, gates, proj, nw, inv, seq):
    t = q.shape[0]
    nb, nc = t // seq, seq // CHUNK
    cb = GDN_CB
    nsb = nc // cb
    chains = [(b, hh) for b in range(nb) for hh in range(GDN_HPS)]
    nch = len(chains)
    pair, gate, mats = _gdn_specs(nb, nsb, cb, False)

    def body(q_ref, k_ref, v_ref, g_ref, gg_ref, inv_ref, nw_ref, y_ref, st_ref, carry):
        s, p = pl.program_id(0), pl.program_id(1)

        @pl.when(s == 0)
        def _():
            for ci in range(nch):
                carry[p * nch + ci] = jnp.zeros((GDN_DH, GDN_DH), F32)

        def step(c, states):
            r = _rows(c, CHUNK)
            for ci, (b, hh) in enumerate(chains):
                st_ref[b, hh, c] = states[ci]
            ins = _gdn_chain_inputs(chains, p, r, c, q_ref, k_ref, v_ref, g_ref, gg_ref, inv_ref)
            y, states = _gdn_chunks(*ins[:5], states, ins[5], nw_ref[...], ins[6])
            for ci, (b, hh) in enumerate(chains):
                y_ref[b, r, hh * 128:(hh + 1) * 128] = y[ci]
            return states

        states = lax.fori_loop(0, cb, step, jnp.stack([carry[p * nch + ci] for ci in range(nch)]))
        for ci in range(nch):
            carry[p * nch + ci] = states[ci]

    v3 = lambda a: a.reshape(nb, seq, a.shape[1])
    y, st = pl.pallas_call(
        body, grid=(nsb, GDN_HEADS // GDN_HPS),
        in_specs=[pair(), pair(), pair(), gate(), pair(24 // GDN_HPS), mats(CHUNK), pl.BlockSpec((1, 128), lambda s, p: (0, 0))],
        out_specs=[pair(), mats(GDN_DH)],
        out_shape=[jax.ShapeDtypeStruct((nb, seq, 512), F32),
                   jax.ShapeDtypeStruct((nb, GDN_HEADS, nc, GDN_DH, GDN_DH), F32)],
        scratch_shapes=[pltpu.VMEM((GDN_HEADS // GDN_HPS * nch, GDN_DH, GDN_DH), F32)],
        compiler_params=_cp("arbitrary", "arbitrary"), name="gdn_fwd")(v3(q), v3(k), v3(v), v3(gates), v3(proj), inv, nw)
    return y.reshape(t, 512), st


def gdn_bwd(q, k, v, gates, proj, nw, inv, states, dy, seq):
    t = q.shape[0]
    nb, nc = t // seq, seq // CHUNK
    cb = GDN_CB // 2
    nsb = nc // cb
    chains = [(b, hh) for b in range(nb) for hh in range(GDN_HPS)]
    nch = len(chains)
    pair, gate, mats = _gdn_specs(nb, nsb, cb, True)

    def body(q_ref, k_ref, v_ref, g_ref, gg_ref, inv_ref, st_ref, dy_ref, nw_ref,
             dq_ref, dk_ref, dv_ref, dgg_ref, dg_ref, dnw_ref, carry):
        s, p = pl.program_id(0), pl.program_id(1)

        @pl.when((s == 0) & (p == 0))
        def _():
            dnw_ref[...] = jnp.zeros_like(dnw_ref)

        @pl.when(p == 0)
        def _():
            dg_ref[...] = jnp.zeros_like(dg_ref)

        @pl.when(s == 0)
        def _():
            for ci in range(nch):
                carry[p * nch + ci] = jnp.zeros((GDN_DH, GDN_DH), F32)

        def step(idx, dstates):
            c = cb - 1 - idx
            r = _rows(c, CHUNK)
            ins = _gdn_chain_inputs(chains, p, r, c, q_ref, k_ref, v_ref, g_ref, gg_ref, inv_ref)
            st = jnp.stack([st_ref[b, hh, c] for b, hh in chains])
            dy = jnp.stack([dy_ref[b, r, hh * 128:(hh + 1) * 128] for b, hh in chains])
            _, vjp = jax.vjp(_gdn_chunks, *ins[:5], st, ins[5], nw_ref[...], ins[6])
            dq, dk, dv, dgc, dbc, dstates, dgg, dnw, _ = vjp((dy, dstates))
            dnw_ref[...] += dnw
            lane = _lane((CHUNK, 128))
            for ci, (b, hh) in enumerate(chains):
                h = GDN_HPS * p + hh
                ln = slice(hh * 128, (hh + 1) * 128)
                dq_ref[b, r, ln] = dq[ci]
                dk_ref[b, r, ln] = dk[ci]
                dv_ref[b, r, ln] = dv[ci]
                dgg_ref[b, r, ln] = dgg[ci].astype(BF16)
                dg_ref[b, r, :] = jnp.where(lane == 8 + h, dgc[ci], jnp.where(lane == 12 + h, dbc[ci], dg_ref[b, r, :]))
            return dstates

        dstates = lax.fori_loop(0, cb, step, jnp.stack([carry[p * nch + ci] for ci in range(nch)]))
        for ci in range(nch):
            carry[p * nch + ci] = dstates[ci]

    v3 = lambda a: a.reshape(nb, seq, a.shape[1])
    res = pl.pallas_call(
        body, grid=(nsb, GDN_HEADS // GDN_HPS),
        in_specs=[pair(), pair(), pair(), gate(), pair(24 // GDN_HPS), mats(CHUNK), mats(GDN_DH), pair(),
                  pl.BlockSpec((1, 128), lambda s, p: (0, 0))],
        out_specs=[pair(), pair(), pair(), pair(), gate(), pl.BlockSpec((1, 128), lambda s, p: (0, 0))],
        out_shape=[jax.ShapeDtypeStruct((nb, seq, 512), F32)] * 3 + [jax.ShapeDtypeStruct((nb, seq, 512), BF16)]
                  + [jax.ShapeDtypeStruct((nb, seq, 128), F32), jax.ShapeDtypeStruct((1, 128), F32)],
        scratch_shapes=[pltpu.VMEM((GDN_HEADS // GDN_HPS * nch, GDN_DH, GDN_DH), F32)],
        compiler_params=_cp("arbitrary", "arbitrary"),
        name="gdn_bwd")(v3(q), v3(k), v3(v), v3(gates), v3(proj), inv, states, v3(dy), nw)
    return [a.reshape(t, a.shape[2]) for a in res[:5]] + [res[5]]


def loss_head(y, target, tm=512):
    t, d = y.shape

    def body(y_ref, t_ref, s_ref, dy_ref):
        @pl.when(pl.program_id(0) == 0)
        def _():
            s_ref[...] = jnp.zeros_like(s_ref)

        err = y_ref[...] - t_ref[...]
        s_ref[...] += jnp.sum(err * err, axis=0, keepdims=True)
        dy_ref[...] = err * (1.0 / d)

    return pl.pallas_call(
        body, grid=(t // tm,),
        in_specs=[pl.BlockSpec((tm, d), lambda i: (i, 0)), pl.BlockSpec((tm, d), lambda i: (i, 0))],
        out_specs=[pl.BlockSpec((1, d), lambda i: (0, 0)), pl.BlockSpec((tm, d), lambda i: (i, 0))],
        out_shape=[jax.ShapeDtypeStruct((1, d), F32), jax.ShapeDtypeStruct((t, d), F32)],
        compiler_params=_cp("arbitrary"), name="loss_head")(y, target)


def _place():
    return lax.axis_index("x"), lax.axis_index("y"), lax.axis_index("c")


def _peer(k):
    x, y, c = _place()
    px = 1 - x if (k >> 2) & 1 else x
    py = 1 - y if (k >> 1) & 1 else y
    pc = 1 - c if k & 1 else c
    return (px, py, pc), 4 * px + 2 * py + pc


_ANY = pl.BlockSpec(memory_space=pl.ANY)
_SEM = pl.BlockSpec(memory_space=pltpu.SEMAPHORE)
_EFFECT = pltpu.SideEffectType.DATAFLOW_SIDE_EFFECTING


def _me():
    x, y, c = _place()
    return 4 * x + 2 * y + c


def _remote_copy(ins, lands, scatter, send_sems, recv_sems, a, k, arriving):
    pid, pidx = _peer(k)
    return pltpu.make_async_remote_copy(src_ref=ins[a].at[pidx] if scatter[a] else ins[a],
                                        dst_ref=lands[a].at[pidx if arriving else _me()],
                                        send_sem=send_sems.at[a * N_DEV + k], recv_sem=recv_sems.at[a * N_DEV + k],
                                        device_id=pid, device_id_type=MESH)


def _local_copy(ins, lands, scatter, loc_sems, a):
    me = _me()
    return pltpu.make_async_copy(ins[a].at[me] if scatter[a] else ins[a], lands[a].at[me], loc_sems.at[a])


def exchange_start(arrays, scatter, name, after):
    n = len(arrays)
    lands = [lax.empty(a.shape if s else (N_DEV,) + a.shape, a.dtype) for a, s in zip(arrays, scatter)]

    def body(*refs):
        ins, lds = refs[:n], refs[n:2 * n]
        send_sems, recv_sems, loc_sems = refs[2 * n + 1:2 * n + 4]
        token = refs[-1]
        for k in range(1, N_DEV):
            for a in range(n):
                _remote_copy(ins, lds, scatter, send_sems, recv_sems, a, k, False).start()
        for a in range(n):
            _local_copy(ins, lds, scatter, loc_sems, a).start()
        token[...] = jnp.zeros_like(token)

    hbm = lambda a: pltpu.HBM(a.shape, a.dtype)
    res = pl.pallas_call(
        body, name=name,
        in_specs=[_ANY] * (2 * n + 1),
        out_specs=[_SEM, _SEM, _SEM] + [_ANY] * (2 * n) + [pl.BlockSpec(memory_space=pltpu.VMEM)],
        out_shape=[pltpu.SemaphoreType.DMA((n * N_DEV,)), pltpu.SemaphoreType.DMA((n * N_DEV,)),
                   pltpu.SemaphoreType.DMA((n,))]
                  + [hbm(a) for a in arrays] + [hbm(a) for a in lands] + [jax.ShapeDtypeStruct((8, 128), F32)],
        input_output_aliases={i: 3 + i for i in range(2 * n)},
        compiler_params=pltpu.CompilerParams(has_side_effects=_EFFECT),
    )(*[pltpu.with_memory_space_constraint(a, pltpu.HBM) for a in list(arrays) + lands], after)
    return res[0:3], res[3:3 + n], res[3 + n:3 + 2 * n], res[-1]


def exchange_wait(sems, arrays, lands, scatter, after, name):
    n = len(arrays)

    def body(*refs):
        ins, lds = refs[:n], refs[n:2 * n]
        ssem, rsem, lsem = refs[2 * n:2 * n + 3]
        for a in range(n):
            _local_copy(ins, lds, scatter, lsem, a).wait()
        for k in range(1, N_DEV):
            for a in range(n):
                _remote_copy(ins, lds, scatter, ssem, rsem, a, k, True).wait_recv()
        for k in range(1, N_DEV):
            for a in range(n):
                _remote_copy(ins, lds, scatter, ssem, rsem, a, k, False).wait_send()

    hbm = lambda a: pltpu.HBM(a.shape, a.dtype)
    res = pl.pallas_call(
        body, name=name,
        in_specs=[_ANY] * (2 * n) + [_SEM, _SEM, _SEM, _ANY],
        out_specs=[_ANY] * (2 * n),
        out_shape=[hbm(a) for a in arrays] + [hbm(a) for a in lands],
        input_output_aliases={i: i for i in range(2 * n)},
        compiler_params=pltpu.CompilerParams(has_side_effects=_EFFECT),
    )(*arrays, *lands, *sems, after)
    return list(res[n:])


def gather_two_level(arrays, name):
    n = len(arrays)

    def body(*refs):
        ins, outs = refs[:n], refs[n:2 * n]
        send_sems, recv_sems, loc_sems = refs[2 * n:]
        x, y, c = _place()
        sibling = (x, y, 1 - c)
        chips = [(1 - x, y), (x, 1 - y), (1 - x, 1 - y)]

        def slot(pos):
            return 4 * pos[0] + 2 * pos[1] + pos[2]

        def copy(a, k, block, to, from_input=False):
            return pltpu.make_async_remote_copy(
                src_ref=ins[a] if from_input else outs[a].at[slot(block)], dst_ref=outs[a].at[slot(block)],
                send_sem=send_sems.at[a, k], recv_sem=recv_sems.at[a, k], device_id=to, device_id_type=MESH)

        me = (x, y, c)
        mine = [pltpu.make_async_copy(ins[a], outs[a].at[slot(me)], loc_sems.at[a]) for a in range(n)]
        for cp in mine:
            cp.start()
        first = [copy(a, 0, me, sibling, True) for a in range(n)]
        first += [copy(a, 1 + j, me, (*chip, c), True) for j, chip in enumerate(chips) for a in range(n)]
        for cp in first:
            cp.start()
        passed = []
        for j, chip in enumerate(chips):
            for a in range(n):
                copy(a, 1 + j, (*chip, c), me).wait_recv()
                passed.append(copy(a, 4 + j, (*chip, c), sibling))
                passed[-1].start()
        for a in range(n):
            copy(a, 0, sibling, me).wait_recv()
            for j, chip in enumerate(chips):
                copy(a, 4 + j, (*chip, 1 - c), me).wait_recv()
        for cp in first + passed:
            cp.wait_send()
        for cp in mine:
            cp.wait()

    return pl.pallas_call(
        body, in_specs=[_ANY] * n, out_specs=[_ANY] * n,
        out_shape=[jax.ShapeDtypeStruct((N_DEV,) + a.shape, a.dtype) for a in arrays],
        scratch_shapes=[pltpu.SemaphoreType.DMA((n, 7)), pltpu.SemaphoreType.DMA((n, 7)), pltpu.SemaphoreType.DMA((n,))],
        name=name)(*arrays)


def exchange_begin(arrays, scatter, name, after):
    sems, arrays_thru, lands_thru, token = exchange_start(arrays, scatter, name + "_start", after)
    return (sems, arrays_thru, lands_thru, scatter, name), token


def exchange_end(state, after):
    sems, arrays_thru, lands_thru, scatter, name = state
    return exchange_wait(sems, arrays_thru, lands_thru, scatter, after, name + "_wait")


def adamw_reduce(slots, w, m, v, l, name, after=None, prev=None):
    nl, r, c = w.shape
    tr = r
    while tr * c * 4 > (1 << 20) and tr % 16 == 0:
        tr //= 2
    bc1 = 1.0 - ADAM_B1 ** ADAM_STEP
    bc2 = 1.0 - ADAM_B2 ** ADAM_STEP

    def body(s_ref, w_ref, m_ref, v_ref, *rest):
        g_ref, d_ref, nm_ref, nv_ref = rest[-4:]
        g = s_ref[0].astype(F32)
        for j in range(1, N_DEV):
            g = g + s_ref[j].astype(F32)
        nm = ADAM_B1 * m_ref[...] + (1.0 - ADAM_B1) * g
        nv = ADAM_B2 * v_ref[...] + (1.0 - ADAM_B2) * (g * g)
        g_ref[...] = g
        nm_ref[...] = nm
        nv_ref[...] = nv
        d_ref[...] = -ADAM_LR * ((nm / bc1) / (jnp.sqrt(nv / bc2) + ADAM_EPS) + ADAM_WD * w_ref[...])

    blk = lambda: pl.BlockSpec((None, tr, c), lambda i: (l, i, 0))
    extra = ([] if after is None else [after]) + ([] if prev is None else list(prev))
    first_prev = 4 + (after is not None)
    return pl.pallas_call(
        body, grid=(r // tr,),
        in_specs=[pl.BlockSpec((N_DEV, tr, c), lambda i: (0, i, 0)), blk(), blk(), blk()] + [_ANY] * len(extra),
        out_specs=[blk(), blk(), blk(), blk()],
        out_shape=[jax.ShapeDtypeStruct((nl, r, c), F32)] * 4,
        input_output_aliases={} if prev is None else {first_prev + j: j for j in range(4)},
        compiler_params=_cp("parallel"), name=name)(slots, w, m, v, *extra)


BIG = ("ffn1_w_in", "ffn1_w_out", "w_in", "gdn_conv", "w_out", "ffn2_w_in", "ffn2_w_out")
GROUPS = (BIG[0:2], BIG[2:5], BIG[5:7])
SMALL = ("ffn1_norm", "mix_norm", "fox_q_norm", "fox_k_norm", "fox_f_bias", "gdn_a_log", "gdn_dt_bias",
         "gdn_out_norm", "ffn2_norm")
WEIGHTS = ("ffn1_norm", "ffn1_w_in", "ffn1_w_out", "mix_norm", "w_in", "fox_q_norm", "fox_k_norm", "fox_f_bias",
           "gdn_conv", "gdn_a_log", "gdn_dt_bias", "gdn_out_norm", "w_out", "ffn2_norm", "ffn2_w_in", "ffn2_w_out")
IN_COLS = (("fq", 512), ("fk", 512), ("fv", 512), ("ff", 8), ("gq", 512), ("gk", 512), ("gv", 512),
           ("ga", 4), ("gb", 4), ("gg", 512))
MY_BIG = ("fq", "fk", "fv", "gq", "gk", "gv", "gg")
MY_SMALL = ("ff", "ga", "gb")
SMALL_ROWS = 8 * 128


def _in_cols_to_mine(w):
    off, parts = 0, {}
    for nm, wd in IN_COLS:
        parts[nm] = w[:, off:off + wd]
        off += wd
    small = jnp.concatenate([parts[nm] for nm in MY_SMALL], axis=1)
    small = jnp.pad(small, ((0, 0), (0, 128 - small.shape[1])))
    return jnp.concatenate([parts[nm] for nm in MY_BIG] + [small], axis=1)


def _in_cols_from_mine(g):
    parts = {nm: g[:, i * 512:(i + 1) * 512] for i, nm in enumerate(MY_BIG)}
    off = N_BIG
    for nm in MY_SMALL:
        wd = dict(IN_COLS)[nm]
        parts[nm] = g[:, off:off + wd]
        off += wd
    return jnp.concatenate([parts[nm] for nm, _ in IN_COLS], axis=1)


def _pack_small(vals):
    rows = []
    nl = vals[SMALL[0]].shape[0]
    for l in range(nl):
        for nm in SMALL:
            v = vals[nm][l].reshape(-1)
            pad = (-v.shape[0]) % SMALL_ROWS
            rows.append(jnp.pad(v, (0, pad)).reshape(-1, 128))
    return jnp.concatenate(rows, axis=0)


def _unpack_small(packed, like):
    out = {nm: [] for nm in SMALL}
    row = 0
    nl = like[SMALL[0]].shape[0]
    for l in range(nl):
        for nm in SMALL:
            n = like[nm].shape[1]
            nr = -(-n // SMALL_ROWS) * 8
            out[nm].append(packed[row:row + nr].reshape(-1)[:n])
            row += nr
    return {nm: jnp.stack(v) for nm, v in out.items()}


def kernel(x, ffn1_norm, ffn1_w_in, ffn1_w_out, mix_norm, w_in, fox_q_norm, fox_k_norm, fox_f_bias, gdn_conv, gdn_a_log, gdn_dt_bias, gdn_out_norm, w_out, ffn2_norm, ffn2_w_in, ffn2_w_out, loss_target, m_ffn1_norm, m_ffn1_w_in, m_ffn1_w_out, m_mix_norm, m_w_in, m_fox_q_norm, m_fox_k_norm, m_fox_f_bias, m_gdn_conv, m_gdn_a_log, m_gdn_dt_bias, m_gdn_out_norm, m_w_out, m_ffn2_norm, m_ffn2_w_in, m_ffn2_w_out, v_ffn1_norm, v_ffn1_w_in, v_ffn1_w_out, v_mix_norm, v_w_in, v_fox_q_norm, v_fox_k_norm, v_fox_f_bias, v_gdn_conv, v_gdn_a_log, v_gdn_dt_bias, v_gdn_out_norm, v_w_out, v_ffn2_norm, v_ffn2_w_in, v_ffn2_w_out):
    wts = dict(ffn1_norm=ffn1_norm, ffn1_w_in=ffn1_w_in, ffn1_w_out=ffn1_w_out, mix_norm=mix_norm, w_in=w_in,
               fox_q_norm=fox_q_norm, fox_k_norm=fox_k_norm, fox_f_bias=fox_f_bias, gdn_conv=gdn_conv,
               gdn_a_log=gdn_a_log, gdn_dt_bias=gdn_dt_bias, gdn_out_norm=gdn_out_norm, w_out=w_out,
               ffn2_norm=ffn2_norm, ffn2_w_in=ffn2_w_in, ffn2_w_out=ffn2_w_out)
    mom = dict(ffn1_norm=m_ffn1_norm, ffn1_w_in=m_ffn1_w_in, ffn1_w_out=m_ffn1_w_out, mix_norm=m_mix_norm, w_in=m_w_in,
               fox_q_norm=m_fox_q_norm, fox_k_norm=m_fox_k_norm, fox_f_bias=m_fox_f_bias, gdn_conv=m_gdn_conv,
               gdn_a_log=m_gdn_a_log, gdn_dt_bias=m_gdn_dt_bias, gdn_out_norm=m_gdn_out_norm, w_out=m_w_out,
               ffn2_norm=m_ffn2_norm, ffn2_w_in=m_ffn2_w_in, ffn2_w_out=m_ffn2_w_out)
    var = dict(ffn1_norm=v_ffn1_norm, ffn1_w_in=v_ffn1_w_in, ffn1_w_out=v_ffn1_w_out, mix_norm=v_mix_norm, w_in=v_w_in,
               fox_q_norm=v_fox_q_norm, fox_k_norm=v_fox_k_norm, fox_f_bias=v_fox_f_bias, gdn_conv=v_gdn_conv,
               gdn_a_log=v_gdn_a_log, gdn_dt_bias=v_gdn_dt_bias, gdn_out_norm=v_gdn_out_norm, w_out=v_w_out,
               ffn2_norm=v_ffn2_norm, ffn2_w_in=v_ffn2_w_in, ffn2_w_out=v_ffn2_w_out)
    nb, seq, d = x.shape
    t = nb * seq
    depth = ffn1_norm.shape[0]

    stages = [(l, gi) for l in range(depth) for gi in range(len(GROUPS))]

    def shards_of(l, gi):
        return [wts[nm][l] if nm == "gdn_conv" else wts[nm][l].astype(BF16) for nm in GROUPS[gi]]

    def behind(nw, token):
        return nw if token is None else nw + token[0:1, 0:1]

    def small_params(l):
        return dict(
            n1=ffn1_norm[l][None], nmix=mix_norm[l][None], n2=ffn2_norm[l][None],
            qw=fox_q_norm[l][None], kw=fox_k_norm[l][None], onw=gdn_out_norm[l][None],
            gp=jnp.concatenate([
                jnp.concatenate([fox_f_bias[l], gdn_dt_bias[l], jnp.zeros((116,), F32)])[None],
                jnp.concatenate([jnp.zeros((8,), F32), gdn_a_log[l], jnp.zeros((116,), F32)])[None],
                jnp.zeros((6, 128), F32)], axis=0))

    h = x.reshape(t, d)
    landed = gather_two_level(shards_of(0, 0), "gather_0")
    saved = [dict(p=small_params(l)) for l in range(depth)]
    for k, (l, gi) in enumerate(stages):
        s, w, token = saved[l], landed, None
        p = s["p"]
        if k + 1 < len(stages):
            nl, ng = stages[k + 1]
            state, token = exchange_begin(shards_of(nl, ng), [False] * len(GROUPS[ng]), f"gather_{k + 1}", landed[0])
        if gi == 0:
            fb = w[0].shape[2]
            p["w1i"], p["w1o"] = w[0].reshape(2, 4, d, fb), w[1].reshape(4, fb, d)
            s["x0"] = h
            h, *s["ffn1"] = ffn_fwd(h, behind(p["n1"], token), p["w1i"], p["w1o"])
            s["x1"] = h
        elif gi == 1:
            p["wi"] = _in_cols_to_mine(w[0].transpose(1, 0, 2).reshape(d, -1))
            p["cw"] = w[1].transpose(1, 0, 2).reshape(CONV_W, -1)
            p["wo"] = w[2].reshape(d, d)
            proj, hn = inproj_fwd(h, behind(p["nmix"], token), p["wi"])
            gates = gates_fwd(proj, p["gp"], seq)
            yf, lse = attn_fwd(proj, gates, p["qw"], p["kw"], seq, tq=min(seq, ATTN_TQ_FWD))
            qh, kh, vh = gdn_pre_fwd(proj, p["cw"], seq)
            inv = gdn_inv(kh, gates, seq)
            yg, st = gdn_fwd(qh, kh, vh, gates, proj, p["onw"], inv, seq)
            h, ycat = outproj_fwd(h, yf, yg, p["wo"])
            s.update(x2=h, proj=proj, hn=hn, gates=gates, yf=yf, lse=lse, qh=qh, kh=kh, vh=vh, st=st, inv=inv, ycat=ycat)
        else:
            fb = w[0].shape[2]
            p["w2i"], p["w2o"] = w[0].reshape(2, 4, d, fb), w[1].reshape(4, fb, d)
            h, *s["ffn2"] = ffn_fwd(h, behind(p["n2"], token), p["w2i"], p["w2o"])
        if k + 1 < len(stages):
            landed = exchange_end(state, h)

    sq, dh = loss_head(h, loss_target.reshape(t, d))
    loss = lax.psum(0.5 * jnp.sum(sq) / d, ("x", "y", "c"))

    got = [None] * len(stages)
    pending, token = None, None
    gsmall = {nm: [None] * depth for nm in SMALL}
    for k in reversed(range(len(stages))):
        l, gi = stages[k]
        s = saved[l]
        p = s["p"]
        if gi != 1:
            nw, xin, wi_, wo_, nm_n, (xn, gu, hh) = (
                (p["n1"], s["x0"], p["w1i"], p["w1o"], "ffn1_norm", s["ffn1"]) if gi == 0 else
                (p["n2"], s["x2"], p["w2i"], p["w2o"], "ffn2_norm", s["ffn2"]))
            dh, dn, dgu, dyh = ffn_bwd(xin, dh, behind(nw, token), gu, wi_, wo_)
            g_in, g_out = wgrad_ffn_in(xn, dgu), wgrad_ffn_out(hh, dyh)
            send = [g_in.reshape(N_DEV, d, g_in.shape[3]), g_out.reshape(N_DEV, -1, d)]
            gsmall[nm_n][l] = dn[0]
        else:
            dyf, dyg, dyb = outproj_bwd(dh, p["wo"], token)
            g_wo = wgrad_2d(s["ycat"], dyb, 512, "wgrad_w_out")
            dq, dk, dv, dga, dqw, dkw = attn_bwd(s["proj"], s["gates"], p["qw"], p["kw"], s["yf"], s["lse"], dyf, seq,
                                                 tq=min(seq, ATTN_TQ_BWD))
            dqh, dkh, dvh, dgg, dgb, donw = gdn_bwd(s["qh"], s["kh"], s["vh"], s["gates"], s["proj"], p["onw"],
                                                     s["inv"], s["st"], dyg, seq)
            dxq, dxk, dxv, dwq, dwk, dwv = gdn_pre_bwd(s["proj"], p["cw"], dqh, dkh, dvh, seq)
            dsm, dgp = gates_bwd(s["proj"], p["gp"], dga, dgb, seq)
            dh, dnmix, dproj = inproj_bwd(s["x1"], dh, p["nmix"], p["wi"], [dq, dk, dv, dxq, dxk, dxv, dgg, dsm])
            g_wi = wgrad_2d(s["hn"], dproj, 512, "wgrad_w_in", F32)
            g_cw = jnp.concatenate([dwq, dwk, dwv], axis=1)
            send = [_in_cols_from_mine(g_wi).reshape(d, N_DEV, -1).transpose(1, 0, 2),
                    g_cw.reshape(CONV_W, N_DEV, -1).transpose(1, 0, 2), g_wo.reshape(N_DEV, -1, d)]
            for nm, val in (("mix_norm", dnmix[0]), ("fox_q_norm", dqw[0]), ("fox_k_norm", dkw[0]),
                            ("fox_f_bias", dgp[0, 0:8]), ("gdn_a_log", dgp[1, 8:12]), ("gdn_dt_bias", dgp[0, 8:12]),
                            ("gdn_out_norm", donw[0])):
                gsmall[nm][l] = val
        flags = [True] * len(send)
        if k == 0:
            send.append(_pack_small({nm: jnp.stack(v) for nm, v in gsmall.items()}))
            flags.append(False)
        prev = dh
        if pending is not None:
            got[pending[1]] = exchange_end(pending[0], dh)
            prev = got[pending[1]][0]
        state, token = exchange_begin(send, flags, f"exchange_grads_{k}", prev)
        pending = (state, k)
    grad_x = dh.reshape(nb, seq, d)

    res = {}

    def update_stage(k, slots, after):
        l, gi = stages[k]
        for i, nm in enumerate(GROUPS[gi]):
            r, c = wts[nm].shape[1:]
            res[nm] = adamw_reduce(slots[i].reshape(N_DEV, r, c), wts[nm], mom[nm], var[nm], l, f"adamw_{nm}_{l}",
                                   after, res.get(nm))
            if after is not None:
                after = res[nm][0]
        return after

    last = token
    for k in range(1, len(stages)):
        last = update_stage(k, got[k], last)
    got[0] = exchange_end(pending[0], last)
    update_stage(0, got[0], None)
    small_like = {nm: wts[nm] for nm in SMALL}
    sm = adamw_reduce(got[0][-1], _pack_small(small_like)[None], _pack_small({nm: mom[nm] for nm in SMALL})[None],
                      _pack_small({nm: var[nm] for nm in SMALL})[None], 0, "adamw_small")
    sm = [_unpack_small(a[0], small_like) for a in sm]
    for nm in SMALL:
        res[nm] = [sm[j][nm] for j in range(4)]
    return (loss, grad_x, *[res[nm][0] for nm in WEIGHTS], *[res[nm][1] for nm in WEIGHTS],
            *[res[nm][2] for nm in WEIGHTS], *[res[nm][3] for nm in WEIGHTS])
```

```python
import jax
import jax.numpy as jnp
from jax import lax
from jax.experimental import pallas as pl
from jax.experimental.pallas import tpu as pltpu

F32 = jnp.float32
BF16 = jnp.bfloat16
EPS = 1e-6
N_DEV = 8
MESH = pl.DeviceIdType.MESH
HIGHEST = lax.Precision.HIGHEST
VMEM_LIMIT = 56 * 1024 * 1024

FOX_HEADS, FOX_DH = 8, 64
GDN_HEADS, GDN_DH = 4, 128
CHUNK = 64
CONV_W = 4

ADAM_LR, ADAM_B1, ADAM_B2, ADAM_EPS, ADAM_WD, ADAM_STEP = 0.001, 0.9, 0.999, 1e-08, 0.01, 10


def _cp(*sem):
    return pltpu.CompilerParams(dimension_semantics=sem, vmem_limit_bytes=VMEM_LIMIT)


def _dot(a, b):
    return jnp.dot(a, b, preferred_element_type=F32)


def _dot_nt(a, b):
    return lax.dot_general(a, b, (((1,), (1,)), ((), ())), preferred_element_type=F32)


def _dot_tn(a, b):
    return lax.dot_general(a, b, (((0,), (0,)), ((), ())), preferred_element_type=F32)


def _rstd(xf):
    return lax.rsqrt(jnp.mean(xf * xf, axis=-1, keepdims=True) + EPS)


def _rms_bwd(xf, r, dyn):
    return r * dyn - xf * (r * r * r) * jnp.mean(dyn * xf, axis=-1, keepdims=True)


def ffn_fwd(x, nw, w_in, w_out, tm=1024, rc=1024):
    t, d = x.shape
    nj, fb = w_out.shape[0], w_out.shape[1]
    tm = min(tm, t)
    rc = min(rc, tm)

    def body(x_ref, nw_ref, wi_ref, wo_ref, o_ref, xn_ref, gu_ref, h_ref, acc_ref):
        j = pl.program_id(1)

        @pl.when(j == 0)
        def _():
            xf = x_ref[...]
            xn_ref[...] = (xf * _rstd(xf) * nw_ref[...]).astype(BF16)
            acc_ref[...] = jnp.zeros_like(acc_ref)

        rows = [slice(c * rc, (c + 1) * rc) for c in range(tm // rc)]
        gs = [_dot(xn_ref[r, :], wi_ref[0]) for r in rows]
        us = [_dot(xn_ref[r, :], wi_ref[1]) for r in rows]
        hs = []
        for g, u, r in zip(gs, us, rows):
            sg = jax.nn.sigmoid(g)
            silu = g * sg
            h = (silu * u).astype(BF16)
            gu_ref[0, r, :] = (u * (sg * (1.0 + g * (1.0 - sg)))).astype(BF16)
            gu_ref[1, r, :] = silu.astype(BF16)
            h_ref[r, :] = h
            hs.append(h)
        for h, r in zip(hs, rows):
            acc_ref[r, :] += _dot(h, wo_ref[...])

        @pl.when(j == nj - 1)
        def _():
            o_ref[...] = x_ref[...] + 0.5 * acc_ref[...]

    return pl.pallas_call(
        body, grid=(t // tm, nj),
        in_specs=[pl.BlockSpec((tm, d), lambda i, j: (i, 0)),
                  pl.BlockSpec((1, d), lambda i, j: (0, 0)),
                  pl.BlockSpec((2, None, d, fb), lambda i, j: (0, j, 0, 0)),
                  pl.BlockSpec((None, fb, d), lambda i, j: (j, 0, 0))],
        out_specs=[pl.BlockSpec((tm, d), lambda i, j: (i, 0)),
                   pl.BlockSpec((tm, d), lambda i, j: (i, 0)),
                   pl.BlockSpec((2, None, tm, fb), lambda i, j: (0, j, i, 0)),
                   pl.BlockSpec((None, tm, fb), lambda i, j: (j, i, 0))],
        out_shape=[jax.ShapeDtypeStruct((t, d), F32), jax.ShapeDtypeStruct((t, d), BF16),
                   jax.ShapeDtypeStruct((2, nj, t, fb), BF16), jax.ShapeDtypeStruct((nj, t, fb), BF16)],
        scratch_shapes=[pltpu.VMEM((tm, d), F32)],
        compiler_params=_cp("parallel", "arbitrary"), name="ffn_fwd")(x, nw, w_in, w_out)


def ffn_bwd(x, dy, nw, gu, w_in, w_out, tm=512, rc=256):
    t, d = x.shape
    nj, fb = w_out.shape[0], w_out.shape[1]
    tm = min(tm, t)
    rc = min(rc, tm)

    def body(x_ref, dy_ref, nw_ref, gu_ref, wi_ref, wo_ref,
             dx_ref, dnw_ref, dgu_ref, dyh_ref, acc_ref):
        i, j = pl.program_id(0), pl.program_id(1)

        @pl.when(j == 0)
        def _():
            dyh_ref[...] = (0.5 * dy_ref[...]).astype(BF16)
            acc_ref[...] = jnp.zeros_like(acc_ref)

        @pl.when((i == 0) & (j == 0))
        def _():
            dnw_ref[...] = jnp.zeros_like(dnw_ref)

        rows = [slice(c * rc, (c + 1) * rc) for c in range(tm // rc)]
        dhs = [_dot_nt(dyh_ref[r, :], wo_ref[...]) for r in rows]
        dgs = [(dh * gu_ref[0, r, :].astype(F32)).astype(BF16) for dh, r in zip(dhs, rows)]
        dus = [(dh * gu_ref[1, r, :].astype(F32)).astype(BF16) for dh, r in zip(dhs, rows)]
        for dg, du, r in zip(dgs, dus, rows):
            dgu_ref[0, r, :] = dg
            dgu_ref[1, r, :] = du
        for dg, du, r in zip(dgs, dus, rows):
            acc_ref[r, :] += _dot_nt(dg, wi_ref[0]) + _dot_nt(du, wi_ref[1])

        @pl.when(j == nj - 1)
        def _():
            xf = x_ref[...]
            r = _rstd(xf)
            dxn = acc_ref[...]
            dnw_ref[...] += jnp.sum(dxn * xf * r, axis=0, keepdims=True)
            dx_ref[...] = _rms_bwd(xf, r, dxn * nw_ref[...]) + dy_ref[...]

    return pl.pallas_call(
        body, grid=(t // tm, nj),
        in_specs=[pl.BlockSpec((tm, d), lambda i, j: (i, 0)),
                  pl.BlockSpec((tm, d), lambda i, j: (i, 0)),
                  pl.BlockSpec((1, d), lambda i, j: (0, 0)),
                  pl.BlockSpec((2, None, tm, fb), lambda i, j: (0, j, i, 0)),
                  pl.BlockSpec((2, None, d, fb), lambda i, j: (0, j, 0, 0)),
                  pl.BlockSpec((None, fb, d), lambda i, j: (j, 0, 0))],
        out_specs=[pl.BlockSpec((tm, d), lambda i, j: (i, 0)),
                   pl.BlockSpec((1, d), lambda i, j: (0, 0)),
                   pl.BlockSpec((2, None, tm, fb), lambda i, j: (0, j, i, 0)),
                   pl.BlockSpec((tm, d), lambda i, j: (i, 0))],
        out_shape=[jax.ShapeDtypeStruct((t, d), F32),
                   jax.ShapeDtypeStruct((1, d), F32),
                   jax.ShapeDtypeStruct((2, nj, t, fb), BF16),
                   jax.ShapeDtypeStruct((t, d), BF16)],
        scratch_shapes=[pltpu.VMEM((tm, d), F32)],
        compiler_params=_cp("arbitrary", "arbitrary"), name="ffn_bwd")(x, dy, nw, gu, w_in, w_out)


def _wgrad_call(a, b, a_spec, b_spec, out_shape, out_spec, grid, name, out_dtype=BF16):
    last = len(grid) - 1
    acc_shape = tuple(s for s in out_spec.block_shape if s is not None)

    def body(a_ref, b_ref, o_ref, acc_ref):
        @pl.when(pl.program_id(last) == 0)
        def _():
            acc_ref[...] = jnp.zeros_like(acc_ref)

        if len(acc_shape) == 3:
            shared_a = a_ref[...] if len(a_ref.shape) == 2 else None
            shared_b = b_ref[...] if len(b_ref.shape) == 2 else None
            for s in range(acc_shape[0]):
                acc_ref[s] += _dot_tn(a_ref[s] if shared_a is None else shared_a,
                                      b_ref[s] if shared_b is None else shared_b)
        else:
            acc_ref[...] += _dot_tn(a_ref[...], b_ref[...])

        @pl.when(pl.program_id(last) == grid[last] - 1)
        def _():
            o_ref[...] = acc_ref[...].astype(o_ref.dtype)

    sem = ("parallel",) * last + ("arbitrary",)
    return pl.pallas_call(body, grid=grid, in_specs=[a_spec, b_spec], out_specs=out_spec,
                          out_shape=jax.ShapeDtypeStruct(out_shape, out_dtype),
                          scratch_shapes=[pltpu.VMEM(acc_shape, F32)],
                          compiler_params=_cp(*sem), name=name)(a, b)


WGRAD_TM = 2048
WGRAD_TM_MIXER = 1024


def wgrad_ffn_in(xn, dgu, tm=WGRAD_TM):
    t, d = xn.shape
    _, nj, _, fb = dgu.shape
    tm = min(tm, t)
    return _wgrad_call(xn, dgu,
                       pl.BlockSpec((tm, d), lambda j, k: (k, 0)),
                       pl.BlockSpec((2, None, tm, fb), lambda j, k: (0, j, k, 0)),
                       (2, nj, d, fb), pl.BlockSpec((2, None, d, fb), lambda j, k: (0, j, 0, 0)),
                       (nj, t // tm), "wgrad_ffn_in")


def wgrad_ffn_out(h, dyh, tm=WGRAD_TM):
    nj, t, fb = h.shape
    d = dyh.shape[1]
    tm = min(tm, t)
    return _wgrad_call(h, dyh,
                       pl.BlockSpec((nj, tm, fb), lambda k: (0, k, 0)),
                       pl.BlockSpec((tm, d), lambda k: (k, 0)),
                       (nj, fb, d), pl.BlockSpec((nj, fb, d), lambda k: (0, 0, 0)),
                       (t // tm,), "wgrad_ffn_out")


def wgrad_2d(a, b, tk, name, out_dtype=BF16, tm=WGRAD_TM_MIXER):
    t, k = a.shape
    n = b.shape[1]
    tm = min(tm, t)
    return _wgrad_call(a, b,
                       pl.BlockSpec((tm, tk), lambda c, s: (s, c)),
                       pl.BlockSpec((tm, n), lambda c, s: (s, 0)),
                       (k, n), pl.BlockSpec((tk, n), lambda c, s: (c, 0)),
                       (k // tk, t // tm), name, out_dtype)


N_BIG = 7 * 512
N_PROJ = N_BIG + 128
COL_SMALL = N_BIG // 128


def inproj_fwd(x, nw, w, tm=512):
    t, d = x.shape
    n = w.shape[1]

    def body(x_ref, nw_ref, w_ref, p_ref, hn_ref):
        xf = x_ref[...]
        hn = (xf * _rstd(xf) * nw_ref[...]).astype(BF16)
        hn_ref[...] = hn
        p_ref[...] = _dot(hn, w_ref[...])

    return pl.pallas_call(
        body, grid=(t // tm,),
        in_specs=[pl.BlockSpec((tm, d), lambda i: (i, 0)), pl.BlockSpec((1, d), lambda i: (0, 0)),
                  pl.BlockSpec((d, n), lambda i: (0, 0))],
        out_specs=[pl.BlockSpec((tm, n), lambda i: (i, 0)), pl.BlockSpec((tm, d), lambda i: (i, 0))],
        out_shape=[jax.ShapeDtypeStruct((t, n), F32), jax.ShapeDtypeStruct((t, d), BF16)],
        compiler_params=_cp("parallel"), name="inproj_fwd")(x, nw, w)


def inproj_bwd(x, dres, nw, w, dparts, tm=512):
    t, d = x.shape
    n = w.shape[1]
    widths = [p.shape[1] for p in dparts]
    assert sum(widths) == n

    def body(x_ref, dres_ref, nw_ref, w_ref, *rest):
        part_refs, (dx_ref, dnw_ref, dp_ref) = rest[:len(widths)], rest[len(widths):]

        @pl.when(pl.program_id(0) == 0)
        def _():
            dnw_ref[...] = jnp.zeros_like(dnw_ref)

        dp = jnp.concatenate([r[...].astype(BF16) for r in part_refs], axis=1)
        dp_ref[...] = dp
        dhn = _dot_nt(dp, w_ref[...])
        xf = x_ref[...]
        r = _rstd(xf)
        dnw_ref[...] += jnp.sum(dhn * xf * r, axis=0, keepdims=True)
        dx_ref[...] = _rms_bwd(xf, r, dhn * nw_ref[...]) + dres_ref[...]

    return pl.pallas_call(
        body, grid=(t // tm,),
        in_specs=[pl.BlockSpec((tm, d), lambda i: (i, 0)), pl.BlockSpec((tm, d), lambda i: (i, 0)),
                  pl.BlockSpec((1, d), lambda i: (0, 0)), pl.BlockSpec((d, n), lambda i: (0, 0))]
                 + [pl.BlockSpec((tm, wd), lambda i: (i, 0)) for wd in widths],
        out_specs=[pl.BlockSpec((tm, d), lambda i: (i, 0)), pl.BlockSpec((1, d), lambda i: (0, 0)),
                   pl.BlockSpec((tm, n), lambda i: (i, 0))],
        out_shape=[jax.ShapeDtypeStruct((t, d), F32), jax.ShapeDtypeStruct((1, d), F32),
                   jax.ShapeDtypeStruct((t, n), BF16)],
        compiler_params=_cp("arbitrary"), name="inproj_bwd")(x, dres, nw, w, *dparts)


def outproj_fwd(x, yf, yg, w, tm=1024):
    t, d = x.shape
    hw = yf.shape[1]
    tm = min(tm, t)

    def body(x_ref, yf_ref, yg_ref, w_ref, o_ref, y_ref):
        y = jnp.concatenate([yf_ref[...], yg_ref[...]], axis=1).astype(BF16)
        y_ref[...] = y
        o_ref[...] = x_ref[...] + _dot(y, w_ref[...])

    return pl.pallas_call(
        body, grid=(t // tm,),
        in_specs=[pl.BlockSpec((tm, d), lambda i: (i, 0)), pl.BlockSpec((tm, hw), lambda i: (i, 0)),
                  pl.BlockSpec((tm, hw), lambda i: (i, 0)), pl.BlockSpec((2 * hw, d), lambda i: (0, 0))],
        out_specs=[pl.BlockSpec((tm, d), lambda i: (i, 0)), pl.BlockSpec((tm, 2 * hw), lambda i: (i, 0))],
        out_shape=[jax.ShapeDtypeStruct((t, d), F32), jax.ShapeDtypeStruct((t, 2 * hw), BF16)],
        compiler_params=_cp("parallel"), name="outproj_fwd")(x, yf, yg, w)


def outproj_bwd(dy, w, after=None, tm=1024):
    t, d = dy.shape
    hw = w.shape[0] // 2
    tm = min(tm, t)
    extra = [] if after is None else [after]

    def body(dy_ref, w_ref, *rest):
        df_ref, dg_ref, dyb_ref = rest[-3:]
        dyb = dy_ref[...].astype(BF16)
        dyb_ref[...] = dyb
        dyy = _dot_nt(dyb, w_ref[...])
        df_ref[...] = dyy[:, :hw]
        dg_ref[...] = dyy[:, hw:]

    return pl.pallas_call(
        body, grid=(t // tm,),
        in_specs=[pl.BlockSpec((tm, d), lambda i: (i, 0)), pl.BlockSpec((2 * hw, d), lambda i: (0, 0))]
                 + [pl.BlockSpec(memory_space=pl.ANY)] * len(extra),
        out_specs=[pl.BlockSpec((tm, hw), lambda i: (i, 0)), pl.BlockSpec((tm, hw), lambda i: (i, 0)),
                   pl.BlockSpec((tm, d), lambda i: (i, 0))],
        out_shape=[jax.ShapeDtypeStruct((t, hw), F32), jax.ShapeDtypeStruct((t, hw), F32),
                   jax.ShapeDtypeStruct((t, d), BF16)],
        compiler_params=_cp("parallel"), name="outproj_bwd")(dy, w, *extra)


def _lane(shape):
    return lax.broadcasted_iota(jnp.int32, shape, 1)


def _row(shape):
    return lax.broadcasted_iota(jnp.int32, shape, 0)


def _gate_terms(val, gp_ref):
    z = val + gp_ref[0:1, :]
    sp = jnp.log(1.0 + jnp.exp(-jnp.abs(z)))
    return z, sp


def gates_fwd(proj, gp, seq, ts=512):
    t = proj.shape[0]
    nb, ns = t // seq, seq // ts

    def body(v_ref, gp_ref, o_ref, carry_ref):
        @pl.when(pl.program_id(1) == 0)
        def _():
            carry_ref[...] = jnp.zeros_like(carry_ref)

        z, sp = _gate_terms(v_ref[...], gp_ref)
        logsig = jnp.minimum(z, 0.0) - sp
        tri = (_row((ts, ts)) >= _lane((ts, ts))).astype(F32)
        cum = jnp.dot(tri, logsig, precision=HIGHEST, preferred_element_type=F32) + carry_ref[0:1, :]
        carry_ref[0:1, :] = cum[ts - 1:ts, :]
        g = -jnp.exp(gp_ref[1:2, :]) * (jnp.maximum(z, 0.0) + sp)
        beta = jax.nn.sigmoid(z)
        lane = _lane((ts, 128))
        o_ref[...] = jnp.where(lane < 8, cum, jnp.where(lane < 12, g, jnp.where(lane < 16, beta, 0.0)))

    return pl.pallas_call(
        body, grid=(nb, ns),
        in_specs=[pl.BlockSpec((ts, 128), lambda b, s: (b * ns + s, COL_SMALL)),
                  pl.BlockSpec((8, 128), lambda b, s: (0, 0))],
        out_specs=pl.BlockSpec((ts, 128), lambda b, s: (b * ns + s, 0)),
        out_shape=jax.ShapeDtypeStruct((t, 128), F32),
        scratch_shapes=[pltpu.VMEM((8, 128), F32)],
        compiler_params=_cp("parallel", "arbitrary"), name="gates_fwd")(proj, gp)


def gates_bwd(proj, gp, dga, dgb, seq, ts=512):
    t = proj.shape[0]
    nb, ns = t // seq, seq // ts

    def body(v_ref, gp_ref, da_ref, db_ref, ds_ref, dgp_ref, carry_ref):
        @pl.when(pl.program_id(1) == 0)
        def _():
            carry_ref[...] = jnp.zeros_like(carry_ref)

        @pl.when((pl.program_id(0) == 0) & (pl.program_id(1) == 0))
        def _():
            dgp_ref[...] = jnp.zeros_like(dgp_ref)

        lane = _lane((ts, 128))
        dgate = jnp.where(lane < 8, da_ref[...], jnp.where(lane < 16, db_ref[...], 0.0))
        z, sp = _gate_terms(v_ref[...], gp_ref)
        triu = (_row((ts, ts)) <= _lane((ts, ts))).astype(F32)
        dlog = jnp.dot(triu, dgate, precision=HIGHEST, preferred_element_type=F32) + carry_ref[0:1, :]
        carry_ref[0:1, :] = dlog[0:1, :]
        sig = jax.nn.sigmoid(z)
        nea = -jnp.exp(gp_ref[1:2, :])
        g = nea * (jnp.maximum(z, 0.0) + sp)
        dz = jnp.where(lane < 8, dlog * (1.0 - sig),
                       jnp.where(lane < 12, dgate * nea * sig, dgate * sig * (1.0 - sig)))
        dz = jnp.where(lane < 16, dz, 0.0)
        ds_ref[...] = dz.astype(BF16)
        dgp_ref[0:1, :] += jnp.where(lane[0:1] < 12, jnp.sum(dz, axis=0, keepdims=True), 0.0)
        dgp_ref[1:2, :] += jnp.where((lane[0:1] >= 8) & (lane[0:1] < 12), jnp.sum(dgate * g, axis=0, keepdims=True), 0.0)

    rev = lambda b, s: (b * ns + (ns - 1 - s), 0)
    return pl.pallas_call(
        body, grid=(nb, ns),
        in_specs=[pl.BlockSpec((ts, 128), lambda b, s: (b * ns + (ns - 1 - s), COL_SMALL)),
                  pl.BlockSpec((8, 128), lambda b, s: (0, 0)),
                  pl.BlockSpec((ts, 128), rev), pl.BlockSpec((ts, 128), rev)],
        out_specs=[pl.BlockSpec((ts, 128), rev), pl.BlockSpec((8, 128), lambda b, s: (0, 0))],
        out_shape=[jax.ShapeDtypeStruct((t, 128), BF16), jax.ShapeDtypeStruct((8, 128), F32)],
        scratch_shapes=[pltpu.VMEM((8, 128), F32)],
        compiler_params=_cp("arbitrary", "arbitrary"), name="gates_bwd")(proj, gp, dga, dgb)


NEG = -1e30
ATTN_TQ_FWD = 1024
ATTN_TQ_BWD = 512


def _pick_lane(tile, idx):
    return jnp.sum(jnp.where(_lane(tile.shape) == idx, tile, 0.0), axis=1, keepdims=True)


def _row_to_col(row, n):
    return jnp.sum(jnp.where(_row((n, n)) == _lane((n, n)), row, 0.0), axis=1, keepdims=True)


def _rows(i, n):
    return pl.ds(pl.multiple_of(i * n, n), n)


LOG2E = 1.4426950408889634
LN2 = 0.6931471805599453


def _split_dot(x, mat, passes):
    total, rest = None, x
    for _ in range(passes):
        part = rest.astype(BF16)
        rest = rest - part.astype(F32)
        total = _dot(part, mat) if total is None else total + _dot(part, mat)
    return total


def _pair_mats(p, dh):
    r, l = _row((128, 128)), _lane((128, 128))
    same = (r < dh) == (l < dh)
    upper = (l >= dh).astype(jnp.int32)
    as_bf16 = lambda m: m.astype(BF16)
    return dict(own=as_bf16(same), other=as_bf16(jnp.logical_not(same)), pick_other=as_bf16(r == 2 * p + 1 - upper),
                swap=as_bf16(((r == 0) & (l >= dh)) | ((r == dh) & (l < dh))))


def _pair_rstd(x2, sel, dh):
    return lax.rsqrt(_split_dot(x2 * x2, sel["own"], 2) * (1.0 / dh) + EPS)


def _pair_aug(cols, n, dh):
    lane = _lane((n, 128))
    li = jnp.where(lane >= dh, lane - dh, lane)
    out = jnp.zeros((n, 128), F32)
    for i, c in enumerate(cols):
        out = jnp.where(li == i, c, out)
    return out


def _split3(x):
    hi = x.astype(BF16).astype(F32)
    mid = (x - hi).astype(BF16).astype(F32)
    return [hi, mid, (x - hi - mid).astype(BF16).astype(F32)]


def _once(shape, index_map):
    return pl.BlockSpec(shape, index_map, pipeline_mode=pl.Buffered(1))


def attn_fwd(proj, gates, qw, kw, seq, tq=256):
    t = proj.shape[0]
    nb, nq, dh = t // seq, seq // tq, FOX_DH
    scale = dh ** -0.5

    def body(q_ref, k_ref, v_ref, g_ref, qw_ref, kw_ref, y_ref, lse_ref, qs, ks, vs):
        p = pl.program_id(1)
        heads = range(2)
        low = _lane((tq, 128)) < dh
        sel = _pair_mats(p, dh)

        def prep(i, _):
            r = _rows(i, tq)
            q2, k2 = q_ref[r, :], k_ref[r, :]
            cc = _split_dot(g_ref[r, :], sel["pick_other"], 3) * LOG2E
            qn = q2 * _pair_rstd(q2, sel, dh) * qw_ref[...] * (scale * LOG2E)
            kn = k2 * _pair_rstd(k2, sel, dh) * kw_ref[...]
            qx = _pair_aug(_split3(cc) + [1.0, 1.0, 1.0], tq, dh)
            kx = _pair_aug([1.0, 1.0, 1.0] + _split3(-cc), tq, dh)
            for hh in heads:
                own = low if hh == 0 else jnp.logical_not(low)
                qs[hh, r, :] = jnp.where(own, qn, qx).astype(BF16)
                ks[hh, r, :] = jnp.where(own, kn, kx).astype(BF16)
            vs[r, :] = v_ref[r, :].astype(BF16)
            return 0

        lax.fori_loop(0, nq, prep, 0)

        def q_tile(i, _):
            r = _rows(i, tq)
            qt = [qs[hh, r, :] for hh in heads]

            def kv_step(j, carry, masked):
                kr = _rows(j, tq)
                vt = vs[kr, :]
                out = []
                for hh in heads:
                    m, l, acc = carry[hh]
                    s = _dot_nt(qt[hh], ks[hh, kr, :])
                    if masked:
                        s = jnp.where(_row((tq, tq)) >= _lane((tq, tq)), s, NEG)
                    m_new = jnp.maximum(m, jnp.max(s, axis=1, keepdims=True))
                    pe = jnp.exp2(s - m_new)
                    a = jnp.exp2(m - m_new)
                    out.append((m_new, a * l + jnp.sum(pe, axis=1, keepdims=True), a * acc + _dot(pe.astype(BF16), vt)))
                return tuple(out)

            one = (jnp.full((tq, 1), NEG, F32), jnp.zeros((tq, 1), F32), jnp.zeros((tq, 128), F32))
            carry = lax.fori_loop(0, i, lambda j, c: kv_step(j, c, False), (one, one))
            (m0, l0, acc0), (m1, l1, acc1) = kv_step(i, carry, True)
            y_ref[r, :] = jnp.where(low, acc0 / l0, acc1 / l1)
            lse_ref[r, :] = jnp.where(low, m0 + jnp.log2(l0), m1 + jnp.log2(l1))
            return 0

        lax.fori_loop(0, nq, q_tile, 0)

    blk = lambda off: _once((seq, 128), lambda b, p: (b, off + p))
    return pl.pallas_call(
        body, grid=(nb, 4),
        in_specs=[blk(0), blk(4), blk(8), _once((seq, 128), lambda b, p: (b, 0)),
                  pl.BlockSpec((1, 128), lambda b, p: (0, 0)), pl.BlockSpec((1, 128), lambda b, p: (0, 0))],
        out_specs=[pl.BlockSpec((seq, 128), lambda b, p: (b, p)), pl.BlockSpec((seq, 128), lambda b, p: (b, p))],
        out_shape=[jax.ShapeDtypeStruct((t, 512), F32), jax.ShapeDtypeStruct((t, 512), F32)],
        scratch_shapes=[pltpu.VMEM((2, seq, 128), BF16), pltpu.VMEM((2, seq, 128), BF16), pltpu.VMEM((seq, 128), BF16)],
        compiler_params=_cp("parallel", "arbitrary"),
        name="attn_fwd")(proj, proj, proj, gates, jnp.tile(qw, (1, 2)), jnp.tile(kw, (1, 2)))


def attn_bwd(proj, gates, qw, kw, y, lse, dy, seq, tq=256):
    t = proj.shape[0]
    nb, nq, dh = t // seq, seq // tq, FOX_DH
    scale = dh ** -0.5

    def body(q_ref, k_ref, v_ref, g_ref, qw_ref, kw_ref, y_ref, lse_ref, dy_ref,
             dq_ref, dk_ref, dv_ref, dg_ref, dqw_ref, dkw_ref,
             qs, ks, vs, dos, dsrow, dqa, dka):
        b, p = pl.program_id(0), pl.program_id(1)

        @pl.when((b == 0) & (p == 0))
        def _():
            dqw_ref[...] = jnp.zeros_like(dqw_ref)
            dkw_ref[...] = jnp.zeros_like(dkw_ref)

        @pl.when(p == 0)
        def _():
            dg_ref[...] = jnp.zeros_like(dg_ref)

        heads = range(2)
        low = _lane((tq, 128)) < dh
        sel = _pair_mats(p, dh)

        def prep(i, _):
            r = _rows(i, tq)
            q2, k2, dy2 = q_ref[r, :], k_ref[r, :], dy_ref[r, :]
            cc = _split_dot(g_ref[r, :], sel["pick_other"], 3) * LOG2E
            lse_x = _split_dot(lse_ref[r, :], sel["swap"], 3)
            delta_x = _split_dot(dy2 * y_ref[r, :], sel["other"], 2)
            qn = q2 * _pair_rstd(q2, sel, dh) * qw_ref[...] * (scale * LOG2E)
            kn = k2 * _pair_rstd(k2, sel, dh) * kw_ref[...]
            qx = _pair_aug(_split3(cc) + [1.0, 1.0, 1.0] + _split3(-lse_x), tq, dh)
            kx = _pair_aug([1.0, 1.0, 1.0] + _split3(-cc) + [1.0, 1.0, 1.0], tq, dh)
            vx = _pair_aug([1.0, 1.0, 1.0], tq, dh)
            dx = _pair_aug(_split3(-delta_x), tq, dh)
            for hh in heads:
                own = low if hh == 0 else jnp.logical_not(low)
                qs[hh, r, :] = jnp.where(own, qn, qx).astype(BF16)
                ks[hh, r, :] = jnp.where(own, kn, kx).astype(BF16)
                vs[hh, r, :] = jnp.where(own, v_ref[r, :], vx).astype(BF16)
                dos[hh, r, :] = jnp.where(own, dy2, dx).astype(BF16)
                dsrow[hh, r, :] = jnp.zeros((tq, 1), F32)
                dqa[hh, r, :] = jnp.zeros((tq, 128), F32)
            return 0

        lax.fori_loop(0, nq, prep, 0)

        def kv_tile(j, _):
            kr = _rows(j, tq)
            kt = [ks[hh, kr, :] for hh in heads]
            vt = [vs[hh, kr, :] for hh in heads]

            def q_step(i, carry, masked):
                r = _rows(i, tq)
                out = []
                for hh in heads:
                    dk, dv, dcr = carry[hh]
                    qt, dot = qs[hh, r, :], dos[hh, r, :]
                    s = _dot_nt(qt, kt[hh])
                    if masked:
                        s = jnp.where(_row((tq, tq)) >= _lane((tq, tq)), s, NEG)
                    pe = jnp.exp2(s)
                    ds = pe * _dot_nt(dot, vt[hh])
                    dsb = ds.astype(BF16)
                    dqa[hh, r, :] += _dot(dsb, kt[hh])
                    dsrow[hh, r, :] += jnp.sum(ds, axis=1, keepdims=True)
                    out.append((dk + _dot_tn(dsb, qt), dv + _dot_tn(pe.astype(BF16), dot),
                                dcr - jnp.sum(ds, axis=0, keepdims=True)))
                return tuple(out)

            one = (jnp.zeros((tq, 128), F32), jnp.zeros((tq, 128), F32), jnp.zeros((1, tq), F32))
            carry = q_step(j, (one, one), True)
            (dk0, dv0, dcr0), (dk1, dv1, dcr1) = lax.fori_loop(j + 1, nq, lambda i, c: q_step(i, c, False), carry)
            dka[kr, :] = jnp.where(low, dk0, dk1)
            dv_ref[kr, :] = jnp.where(low, dv0, dv1).astype(BF16)
            lane = _lane((tq, 128))
            dg_ref[kr, :] = jnp.where(lane == 2 * p, _row_to_col(dcr0, tq),
                                      jnp.where(lane == 2 * p + 1, _row_to_col(dcr1, tq), dg_ref[kr, :]))
            return 0

        lax.fori_loop(0, nq, kv_tile, 0)

        def post(i, _):
            r = _rows(i, tq)
            q2, k2 = q_ref[r, :], k_ref[r, :]
            rq, rk = _pair_rstd(q2, sel, dh), _pair_rstd(k2, sel, dh)
            dqn = jnp.where(low, dqa[0, r, :], dqa[1, r, :]) * scale
            dkn = dka[r, :] * LN2
            dqw_ref[...] += jnp.sum(dqn * q2 * rq, axis=0, keepdims=True)
            dkw_ref[...] += jnp.sum(dkn * k2 * rk, axis=0, keepdims=True)
            for x2, rr, dyn, o_ref in ((q2, rq, dqn * qw_ref[...], dq_ref), (k2, rk, dkn * kw_ref[...], dk_ref)):
                mean = _split_dot(dyn * x2, sel["own"], 2) * (1.0 / dh)
                o_ref[r, :] = (rr * dyn - x2 * (rr * rr * rr) * mean).astype(BF16)
            lane = _lane((tq, 128))
            dg_ref[r, :] += jnp.where(lane == 2 * p, dsrow[0, r, :], jnp.where(lane == 2 * p + 1, dsrow[1, r, :], 0.0))
            return 0

        lax.fori_loop(0, nq, post, 0)

    blk = lambda off: _once((seq, 128), lambda b, p: (b, off + p))
    own = lambda: _once((seq, 128), lambda b, p: (b, p))
    vec = lambda: pl.BlockSpec((1, 128), lambda b, p: (0, 0))
    res = pl.pallas_call(
        body, grid=(nb, 4),
        in_specs=[blk(0), blk(4), blk(8), _once((seq, 128), lambda b, p: (b, 0)), vec(), vec(), own(), own(), own()],
        out_specs=[own(), own(), own(), _once((seq, 128), lambda b, p: (b, 0)), vec(), vec()],
        out_shape=[jax.ShapeDtypeStruct((t, 512), BF16)] * 3
                  + [jax.ShapeDtypeStruct((t, 128), F32), jax.ShapeDtypeStruct((1, 128), F32), jax.ShapeDtypeStruct((1, 128), F32)],
        scratch_shapes=[pltpu.VMEM((2, seq, 128), BF16)] * 4
                       + [pltpu.VMEM((2, seq, 1), F32), pltpu.VMEM((2, seq, 128), F32), pltpu.VMEM((seq, 128), F32)],
        compiler_params=_cp("arbitrary", "arbitrary"),
        name="attn_bwd")(proj, proj, proj, gates, jnp.tile(qw, (1, 2)), jnp.tile(kw, (1, 2)), y, lse, dy)
    return list(res[:4]) + [res[4][:, :dh] + res[4][:, dh:], res[5][:, :dh] + res[5][:, dh:]]


def _silu_grad(c, sg):
    return sg * (1.0 + c * (1.0 - sg))


def _conv(x, w, n):
    row = _row(x.shape)
    c = x * w[CONV_W - 1:CONV_W, :]
    for k in range(CONV_W - 1):
        sh = CONV_W - 1 - k
        c = c + w[k:k + 1, :] * jnp.where(row >= sh, pltpu.roll(x, sh, 0), 0.0)
    return c


def gdn_pre_fwd(proj, cw, seq):
    t = proj.shape[0]
    nb = t // seq
    scale = GDN_DH ** -0.5

    def body(xq_ref, xk_ref, xv_ref, wq_ref, wk_ref, wv_ref, q_ref, k_ref, v_ref):
        def act(x_ref, w_ref):
            c = _conv(x_ref[...], w_ref[...], seq)
            return c * jax.nn.sigmoid(c)

        aq, ak = act(xq_ref, wq_ref), act(xk_ref, wk_ref)
        q_ref[...] = aq * lax.rsqrt(jnp.sum(aq * aq, axis=1, keepdims=True) + EPS) * scale
        k_ref[...] = ak * lax.rsqrt(jnp.sum(ak * ak, axis=1, keepdims=True) + EPS)
        v_ref[...] = act(xv_ref, wv_ref)

    xb = lambda off: pl.BlockSpec((seq, 128), lambda b, h: (b, off + h))
    wb = lambda off: pl.BlockSpec((CONV_W, 128), lambda b, h: (0, off + h))
    ob = lambda: pl.BlockSpec((seq, 128), lambda b, h: (b, h))
    return pl.pallas_call(
        body, grid=(nb, GDN_HEADS),
        in_specs=[xb(12), xb(16), xb(20), wb(0), wb(4), wb(8)],
        out_specs=[ob(), ob(), ob()],
        out_shape=[jax.ShapeDtypeStruct((t, 512), F32)] * 3,
        compiler_params=_cp("parallel", "parallel"), name="gdn_pre_fwd")(proj, proj, proj, cw, cw, cw)


def gdn_pre_bwd(proj, cw, dq, dk, dv, seq):
    t = proj.shape[0]
    nb = t // seq
    scale = GDN_DH ** -0.5

    def body(xq_ref, xk_ref, xv_ref, wq_ref, wk_ref, wv_ref, dq_ref, dk_ref, dv_ref,
             dxq_ref, dxk_ref, dxv_ref, dwq_ref, dwk_ref, dwv_ref):
        first = pl.program_id(1) == 0
        row = _row((seq, 128))

        def one(x_ref, w_ref, dy_ref, dx_ref, dw_ref, norm, sc):
            x, w = x_ref[...], w_ref[...]
            c = _conv(x, w, seq)
            sg = jax.nn.sigmoid(c)
            dy = dy_ref[...]
            if norm:
                a = c * sg
                rs = lax.rsqrt(jnp.sum(a * a, axis=1, keepdims=True) + EPS)
                dy = dy * sc
                da = rs * dy - a * (rs * rs * rs) * jnp.sum(dy * a, axis=1, keepdims=True)
            else:
                da = dy
            dc = da * _silu_grad(c, sg)
            dx = dc * w[CONV_W - 1:CONV_W, :]
            dws = [None] * CONV_W
            dws[CONV_W - 1] = jnp.sum(dc * x, axis=0, keepdims=True)
            for k in range(CONV_W - 1):
                sh = CONV_W - 1 - k
                dc_up = jnp.where(row < seq - sh, pltpu.roll(dc, seq - sh, 0), 0.0)
                dx = dx + w[k:k + 1, :] * dc_up
                dws[k] = jnp.sum(dc_up * x, axis=0, keepdims=True)
            dx_ref[...] = dx.astype(BF16)
            dwn = jnp.concatenate(dws, axis=0)

            @pl.when(first)
            def _():
                dw_ref[...] = dwn

            @pl.when(jnp.logical_not(first))
            def _():
                dw_ref[...] += dwn

        one(xq_ref, wq_ref, dq_ref, dxq_ref, dwq_ref, True, scale)
        one(xk_ref, wk_ref, dk_ref, dxk_ref, dwk_ref, True, 1.0)
        one(xv_ref, wv_ref, dv_ref, dxv_ref, dwv_ref, False, 1.0)

    xb = lambda off: pl.BlockSpec((seq, 128), lambda h, b: (b, off + h))
    wb = lambda off: pl.BlockSpec((CONV_W, 128), lambda h, b: (0, off + h))
    ob = lambda: pl.BlockSpec((seq, 128), lambda h, b: (b, h))
    return pl.pallas_call(
        body, grid=(GDN_HEADS, nb),
        in_specs=[xb(12), xb(16), xb(20), wb(0), wb(4), wb(8), ob(), ob(), ob()],
        out_specs=[ob(), ob(), ob()] + [pl.BlockSpec((CONV_W, 128), lambda h, b: (0, h))] * 3,
        out_shape=[jax.ShapeDtypeStruct((t, 512), BF16)] * 3 + [jax.ShapeDtypeStruct((CONV_W, 512), F32)] * 3,
        compiler_params=_cp("parallel", "arbitrary"), name="gdn_pre_bwd")(proj, proj, proj, cw, cw, cw, dq, dk, dv)


def _b16(x):
    return x.astype(BF16)


@jax.custom_vjp
def _mm(a, b):
    return _dot(_b16(a), _b16(b))


_mm.defvjp(lambda a, b: (_mm(a, b), (a, b)),
           lambda res, g: (_dot_nt(_b16(g), _b16(res[1])), _dot_tn(_b16(res[0]), _b16(g))))


@jax.custom_vjp
def _mm_nt(a, b):
    return _dot_nt(_b16(a), _b16(b))


_mm_nt.defvjp(lambda a, b: (_mm_nt(a, b), (a, b)),
              lambda res, g: (_dot(_b16(g), _b16(res[1])), _dot_tn(_b16(g), _b16(res[0]))))


@jax.custom_vjp
def _mm_tn(a, b):
    return _dot_tn(_b16(a), _b16(b))


_mm_tn.defvjp(lambda a, b: (_mm_tn(a, b), (a, b)),
              lambda res, g: (_dot_nt(_b16(res[1]), _b16(g)), _dot(_b16(res[0]), _b16(g))))


def _dot32(a, b, dims=(((1,), (0,)), ((), ()))):
    def split(x):
        hi = x.astype(BF16)
        return hi, (x - hi.astype(F32)).astype(BF16)

    (ah, al), (bh, bl) = split(a), split(b)
    d = lambda x, y: lax.dot_general(x, y, dims, preferred_element_type=F32)
    return d(ah, bh) + (d(ah, bl) + d(al, bh))


def _inv_fwd_many(mats):
    n = mats[0].shape[0]
    eye = (_row((n, n)) == _lane((n, n))).astype(F32)
    invs, pws = [eye - a for a in mats], list(mats)
    for _ in range(n.bit_length() - 2):
        pws = [_dot32(pw, pw) for pw in pws]
        invs = [inv + _dot32(inv, pw) for inv, pw in zip(invs, pws)]
    return invs


@jax.custom_vjp
def _inv_saved(a, inv):
    return inv


def _inv_saved_bwd(inv, g):
    tg = _dot32(inv, g, (((0,), (0,)), ((), ())))
    return -_dot32(tg, inv, (((1,), (1,)), ((), ()))), jnp.zeros_like(inv)


_inv_saved.defvjp(lambda a, inv: (inv, inv), _inv_saved_bwd)


def _gdn_decay(gcol):
    c = CHUNK
    ri, ci = _row((c, c)), _lane((c, c))
    incl, eye = ri >= ci, ri == ci
    grow = jnp.sum(jnp.where(eye, gcol, 0.0), axis=0, keepdims=True)
    gc = jnp.sum(jnp.where(incl, grow, 0.0), axis=1, keepdims=True)
    gcr = jnp.sum(jnp.where(eye, gc, 0.0), axis=0, keepdims=True)
    gl = jnp.sum(jnp.where(_row((c, 1)) == c - 1, gc, 0.0), axis=0, keepdims=True)
    return gc, gl, jnp.exp(jnp.where(incl, gc - gcr, NEG))


def _gdn_a(k, bcol, decay):
    c = CHUNK
    return jnp.where(_row((c, c)) > _lane((c, c)), _mm_nt(k * bcol, k) * decay, 0.0)


def _gdn_chunk(q, k, v, gcol, bcol, state, gg, nw, inv_saved):
    c = CHUNK
    incl = _row((c, c)) >= _lane((c, c))
    gc, gl, decay = _gdn_decay(gcol)
    kb, vb = k * bcol, v * bcol
    inv = _inv_saved(_gdn_a(k, bcol, decay), inv_saved)
    eg = jnp.exp(gc)
    u = _mm(inv, vb)
    w = _mm(inv, kb * eg)
    pm = jnp.where(incl, _mm_nt(q, k) * decay, 0.0)
    kd = k * jnp.exp(gl - gc)
    qd = q * eg
    v_new = u - _mm(w, state)
    o = _mm(qd, state) + _mm(pm, v_new)
    state_new = state * jnp.exp(gl) + _mm_tn(kd, v_new)
    y = o * _rstd(o) * nw * (gg * jax.nn.sigmoid(gg))
    return y, state_new


_gdn_chunks = jax.vmap(_gdn_chunk, in_axes=(0, 0, 0, 0, 0, 0, 0, None, 0))


def _gdn_chain_inputs(chains, p, r, c, q_ref, k_ref, v_ref, g_ref, gg_ref, inv_ref):
    cols = {nm: [] for nm in ("q", "k", "v", "g", "b", "gg", "inv")}
    for b, hh in chains:
        h = GDN_HPS * p + hh
        ln = slice(hh * 128, (hh + 1) * 128)
        gt = g_ref[b, r, :]
        cols["q"].append(q_ref[b, r, ln])
        cols["k"].append(k_ref[b, r, ln])
        cols["v"].append(v_ref[b, r, ln])
        cols["g"].append(_pick_lane(gt, 8 + h))
        cols["b"].append(_pick_lane(gt, 12 + h))
        cols["gg"].append(gg_ref[b, r, ln])
        cols["inv"].append(inv_ref[b, hh, c])
    return [jnp.stack(cols[nm]) for nm in ("q", "k", "v", "g", "b", "gg", "inv")]


GDN_CB = 8
GDN_INV_CB = 16
GDN_HPS = 4


def gdn_inv(k, gates, seq):
    t = k.shape[0]
    nb, nc = t // seq, seq // CHUNK
    cb = min(GDN_INV_CB, nc)
    rb = cb * CHUNK
    nsb = seq // rb

    def body(k_ref, g_ref, o_ref):
        h = pl.program_id(1)
        mats = []
        for c in range(cb):
            r = slice(c * CHUNK, (c + 1) * CHUNK)
            gt = g_ref[r, :]
            _, _, decay = _gdn_decay(_pick_lane(gt, 8 + h))
            mats.append(_gdn_a(k_ref[r, :], _pick_lane(gt, 12 + h), decay))
        for c, inv in enumerate(_inv_fwd_many(mats)):
            o_ref[c] = inv

    return pl.pallas_call(
        body, grid=(nb, GDN_HEADS, nsb),
        in_specs=[pl.BlockSpec((rb, 128), lambda b, h, s: (b * nsb + s, h)),
                  pl.BlockSpec((rb, 128), lambda b, h, s: (b * nsb + s, 0))],
        out_specs=pl.BlockSpec((None, None, cb, CHUNK, CHUNK), lambda b, h, s: (b, h, s, 0, 0)),
        out_shape=jax.ShapeDtypeStruct((nb, GDN_HEADS, nc, CHUNK, CHUNK), F32),
        compiler_params=_cp("parallel", "parallel", "parallel"), name="gdn_inv")(k, gates)


def _gdn_specs(nb, nsb, cb, rev):
    blk = (lambda s: nsb - 1 - s) if rev else (lambda s: s)
    rb = cb * CHUNK
    pair = lambda off=0: pl.BlockSpec((nb, rb, 128 * GDN_HPS), lambda s, p: (0, blk(s), off + p))
    gate = lambda: pl.BlockSpec((nb, rb, 128), lambda s, p: (0, blk(s), 0))
    mats = lambda n: pl.BlockSpec((nb, GDN_HPS, cb, n, n), lambda s, p: (0, p, blk(s), 0, 0))
    return pair, gate, mats


def gdn_fwd(q, k, v, gates, proj, nw, inv, seq):
    t = q.shape[0]
    nb, nc = t // seq, seq // CHUNK
    cb = GDN_CB
    nsb = nc // cb
    chains = [(b, hh) for b in range(nb) for hh in range(GDN_HPS)]
    nch = len(chains)
    pair, gate, mats = _gdn_specs(nb, nsb, cb, False)

    def body(q_ref, k_ref, v_ref, g_ref, gg_ref, inv_ref, nw_ref, y_ref, st_ref, carry):
        s, p = pl.program_id(0), pl.program_id(1)

        @pl.when(s == 0)
        def _():
            for ci in range(nch):
                carry[p * nch + ci] = jnp.zeros((GDN_DH, GDN_DH), F32)

        def step(c, states):
            r = _rows(c, CHUNK)
            for ci, (b, hh) in enumerate(chains):
                st_ref[b, hh, c] = states[ci]
            ins = _gdn_chain_inputs(chains, p, r, c, q_ref, k_ref, v_ref, g_ref, gg_ref, inv_ref)
            y, states = _gdn_chunks(*ins[:5], states, ins[5], nw_ref[...], ins[6])
            for ci, (b, hh) in enumerate(chains):
                y_ref[b, r, hh * 128:(hh + 1) * 128] = y[ci]
            return states

        states = lax.fori_loop(0, cb, step, jnp.stack([carry[p * nch + ci] for ci in range(nch)]))
        for ci in range(nch):
            carry[p * nch + ci] = states[ci]

    v3 = lambda a: a.reshape(nb, seq, a.shape[1])
    y, st = pl.pallas_call(
        body, grid=(nsb, GDN_HEADS // GDN_HPS),
        in_specs=[pair(), pair(), pair(), gate(), pair(24 // GDN_HPS), mats(CHUNK), pl.BlockSpec((1, 128), lambda s, p: (0, 0))],
        out_specs=[pair(), mats(GDN_DH)],
        out_shape=[jax.ShapeDtypeStruct((nb, seq, 512), F32),
                   jax.ShapeDtypeStruct((nb, GDN_HEADS, nc, GDN_DH, GDN_DH), F32)],
        scratch_shapes=[pltpu.VMEM((GDN_HEADS // GDN_HPS * nch, GDN_DH, GDN_DH), F32)],
        compiler_params=_cp("arbitrary", "arbitrary"), name="gdn_fwd")(v3(q), v3(k), v3(v), v3(gates), v3(proj), inv, nw)
    return y.reshape(t, 512), st


def gdn_bwd(q, k, v, gates, proj, nw, inv, states, dy, seq):
    t = q.shape[0]
    nb, nc = t // seq, seq // CHUNK
    cb = GDN_CB // 2
    nsb = nc // cb
    chains = [(b, hh) for b in range(nb) for hh in range(GDN_HPS)]
    nch = len(chains)
    pair, gate, mats = _gdn_specs(nb, nsb, cb, True)

    def body(q_ref, k_ref, v_ref, g_ref, gg_ref, inv_ref, st_ref, dy_ref, nw_ref,
             dq_ref, dk_ref, dv_ref, dgg_ref, dg_ref, dnw_ref, carry):
        s, p = pl.program_id(0), pl.program_id(1)

        @pl.when((s == 0) & (p == 0))
        def _():
            dnw_ref[...] = jnp.zeros_like(dnw_ref)

        @pl.when(p == 0)
        def _():
            dg_ref[...] = jnp.zeros_like(dg_ref)

        @pl.when(s == 0)
        def _():
            for ci in range(nch):
                carry[p * nch + ci] = jnp.zeros((GDN_DH, GDN_DH), F32)

        def step(idx, dstates):
            c = cb - 1 - idx
            r = _rows(c, CHUNK)
            ins = _gdn_chain_inputs(chains, p, r, c, q_ref, k_ref, v_ref, g_ref, gg_ref, inv_ref)
            st = jnp.stack([st_ref[b, hh, c] for b, hh in chains])
            dy = jnp.stack([dy_ref[b, r, hh * 128:(hh + 1) * 128] for b, hh in chains])
            _, vjp = jax.vjp(_gdn_chunks, *ins[:5], st, ins[5], nw_ref[...], ins[6])
            dq, dk, dv, dgc, dbc, dstates, dgg, dnw, _ = vjp((dy, dstates))
            dnw_ref[...] += dnw
            lane = _lane((CHUNK, 128))
            for ci, (b, hh) in enumerate(chains):
                h = GDN_HPS * p + hh
                ln = slice(hh * 128, (hh + 1) * 128)
                dq_ref[b, r, ln] = dq[ci]
                dk_ref[b, r, ln] = dk[ci]
                dv_ref[b, r, ln] = dv[ci]
                dgg_ref[b, r, ln] = dgg[ci].astype(BF16)
                dg_ref[b, r, :] = jnp.where(lane == 8 + h, dgc[ci], jnp.where(lane == 12 + h, dbc[ci], dg_ref[b, r, :]))
            return dstates

        dstates = lax.fori_loop(0, cb, step, jnp.stack([carry[p * nch + ci] for ci in range(nch)]))
        for ci in range(nch):
            carry[p * nch + ci] = dstates[ci]

    v3 = lambda a: a.reshape(nb, seq, a.shape[1])
    res = pl.pallas_call(
        body, grid=(nsb, GDN_HEADS // GDN_HPS),
        in_specs=[pair(), pair(), pair(), gate(), pair(24 // GDN_HPS), mats(CHUNK), mats(GDN_DH), pair(),
                  pl.BlockSpec((1, 128), lambda s, p: (0, 0))],
        out_specs=[pair(), pair(), pair(), pair(), gate(), pl.BlockSpec((1, 128), lambda s, p: (0, 0))],
        out_shape=[jax.ShapeDtypeStruct((nb, seq, 512), F32)] * 3 + [jax.ShapeDtypeStruct((nb, seq, 512), BF16)]
                  + [jax.ShapeDtypeStruct((nb, seq, 128), F32), jax.ShapeDtypeStruct((1, 128), F32)],
        scratch_shapes=[pltpu.VMEM((GDN_HEADS // GDN_HPS * nch, GDN_DH, GDN_DH), F32)],
        compiler_params=_cp("arbitrary", "arbitrary"),
        name="gdn_bwd")(v3(q), v3(k), v3(v), v3(gates), v3(proj), inv, states, v3(dy), nw)
    return [a.reshape(t, a.shape[2]) for a in res[:5]] + [res[5]]


def loss_head(y, target, tm=512):
    t, d = y.shape

    def body(y_ref, t_ref, s_ref, dy_ref):
        @pl.when(pl.program_id(0) == 0)
        def _():
            s_ref[...] = jnp.zeros_like(s_ref)

        err = y_ref[...] - t_ref[...]
        s_ref[...] += jnp.sum(err * err, axis=0, keepdims=True)
        dy_ref[...] = err * (1.0 / d)

    return pl.pallas_call(
        body, grid=(t // tm,),
        in_specs=[pl.BlockSpec((tm, d), lambda i: (i, 0)), pl.BlockSpec((tm, d), lambda i: (i, 0))],
        out_specs=[pl.BlockSpec((1, d), lambda i: (0, 0)), pl.BlockSpec((tm, d), lambda i: (i, 0))],
        out_shape=[jax.ShapeDtypeStruct((1, d), F32), jax.ShapeDtypeStruct((t, d), F32)],
        compiler_params=_cp("arbitrary"), name="loss_head")(y, target)


def _place():
    return lax.axis_index("x"), lax.axis_index("y"), lax.axis_index("c")


def _peer(k):
    x, y, c = _place()
    px = 1 - x if (k >> 2) & 1 else x
    py = 1 - y if (k >> 1) & 1 else y
    pc = 1 - c if k & 1 else c
    return (px, py, pc), 4 * px + 2 * py + pc


_ANY = pl.BlockSpec(memory_space=pl.ANY)
_SEM = pl.BlockSpec(memory_space=pltpu.SEMAPHORE)
_EFFECT = pltpu.SideEffectType.DATAFLOW_SIDE_EFFECTING


def _me():
    x, y, c = _place()
    return 4 * x + 2 * y + c


def _remote_copy(ins, lands, scatter, send_sems, recv_sems, a, k, arriving):
    pid, pidx = _peer(k)
    return pltpu.make_async_remote_copy(src_ref=ins[a].at[pidx] if scatter[a] else ins[a],
                                        dst_ref=lands[a].at[pidx if arriving else _me()],
                                        send_sem=send_sems.at[a * N_DEV + k], recv_sem=recv_sems.at[a * N_DEV + k],
                                        device_id=pid, device_id_type=MESH)


def _local_copy(ins, lands, scatter, loc_sems, a):
    me = _me()
    return pltpu.make_async_copy(ins[a].at[me] if scatter[a] else ins[a], lands[a].at[me], loc_sems.at[a])


def exchange_start(arrays, scatter, name, after):
    n = len(arrays)
    lands = [lax.empty(a.shape if s else (N_DEV,) + a.shape, a.dtype) for a, s in zip(arrays, scatter)]

    def body(*refs):
        ins, lds = refs[:n], refs[n:2 * n]
        send_sems, recv_sems, loc_sems = refs[2 * n + 1:2 * n + 4]
        token = refs[-1]
        for k in range(1, N_DEV):
            for a in range(n):
                _remote_copy(ins, lds, scatter, send_sems, recv_sems, a, k, False).start()
        for a in range(n):
            _local_copy(ins, lds, scatter, loc_sems, a).start()
        token[...] = jnp.zeros_like(token)

    hbm = lambda a: pltpu.HBM(a.shape, a.dtype)
    res = pl.pallas_call(
        body, name=name,
        in_specs=[_ANY] * (2 * n + 1),
        out_specs=[_SEM, _SEM, _SEM] + [_ANY] * (2 * n) + [pl.BlockSpec(memory_space=pltpu.VMEM)],
        out_shape=[pltpu.SemaphoreType.DMA((n * N_DEV,)), pltpu.SemaphoreType.DMA((n * N_DEV,)),
                   pltpu.SemaphoreType.DMA((n,))]
                  + [hbm(a) for a in arrays] + [hbm(a) for a in lands] + [jax.ShapeDtypeStruct((8, 128), F32)],
        input_output_aliases={i: 3 + i for i in range(2 * n)},
        compiler_params=pltpu.CompilerParams(has_side_effects=_EFFECT),
    )(*[pltpu.with_memory_space_constraint(a, pltpu.HBM) for a in list(arrays) + lands], after)
    return res[0:3], res[3:3 + n], res[3 + n:3 + 2 * n], res[-1]


def exchange_wait(sems, arrays, lands, scatter, after, name):
    n = len(arrays)

    def body(*refs):
        ins, lds = refs[:n], refs[n:2 * n]
        ssem, rsem, lsem = refs[2 * n:2 * n + 3]
        for a in range(n):
            _local_copy(ins, lds, scatter, lsem, a).wait()
        for k in range(1, N_DEV):
            for a in range(n):
                _remote_copy(ins, lds, scatter, ssem, rsem, a, k, True).wait_recv()
        for k in range(1, N_DEV):
            for a in range(n):
                _remote_copy(ins, lds, scatter, ssem, rsem, a, k, False).wait_send()

    hbm = lambda a: pltpu.HBM(a.shape, a.dtype)
    res = pl.pallas_call(
        body, name=name,
        in_specs=[_ANY] * (2 * n) + [_SEM, _SEM, _SEM, _ANY],
        out_specs=[_ANY] * (2 * n),
        out_shape=[hbm(a) for a in arrays] + [hbm(a) for a in lands],
        input_output_aliases={i: i for i in range(2 * n)},
        compiler_params=pltpu.CompilerParams(has_side_effects=_EFFECT),
    )(*arrays, *lands, *sems, after)
    return list(res[n:])


def gather_two_level(arrays, name):
    n = len(arrays)

    def body(*refs):
        ins, outs = refs[:n], refs[n:2 * n]
        send_sems, recv_sems, loc_sems = refs[2 * n:]
        x, y, c = _place()
        sibling = (x, y, 1 - c)
        chips = [(1 - x, y), (x, 1 - y), (1 - x, 1 - y)]

        def slot(pos):
            return 4 * pos[0] + 2 * pos[1] + pos[2]

        def copy(a, k, block, to, from_input=False):
            return pltpu.make_async_remote_copy(
                src_ref=ins[a] if from_input else outs[a].at[slot(block)], dst_ref=outs[a].at[slot(block)],
                send_sem=send_sems.at[a, k], recv_sem=recv_sems.at[a, k], device_id=to, device_id_type=MESH)

        me = (x, y, c)
        mine = [pltpu.make_async_copy(ins[a], outs[a].at[slot(me)], loc_sems.at[a]) for a in range(n)]
        for cp in mine:
            cp.start()
        first = [copy(a, 0, me, sibling, True) for a in range(n)]
        first += [copy(a, 1 + j, me, (*chip, c), True) for j, chip in enumerate(chips) for a in range(n)]
        for cp in first:
            cp.start()
        passed = []
        for j, chip in enumerate(chips):
            for a in range(n):
                copy(a, 1 + j, (*chip, c), me).wait_recv()
                passed.append(copy(a, 4 + j, (*chip, c), sibling))
                passed[-1].start()
        for a in range(n):
            copy(a, 0, sibling, me).wait_recv()
            for j, chip in enumerate(chips):
                copy(a, 4 + j, (*chip, 1 - c), me).wait_recv()
        for cp in first + passed:
            cp.wait_send()
        for cp in mine:
            cp.wait()

    return pl.pallas_call(
        body, in_specs=[_ANY] * n, out_specs=[_ANY] * n,
        out_shape=[jax.ShapeDtypeStruct((N_DEV,) + a.shape, a.dtype) for a in arrays],
        scratch_shapes=[pltpu.SemaphoreType.DMA((n, 7)), pltpu.SemaphoreType.DMA((n, 7)), pltpu.SemaphoreType.DMA((n,))],
        name=name)(*arrays)


def exchange_begin(arrays, scatter, name, after):
    sems, arrays_thru, lands_thru, token = exchange_start(arrays, scatter, name + "_start", after)
    return (sems, arrays_thru, lands_thru, scatter, name), token


def exchange_end(state, after):
    sems, arrays_thru, lands_thru, scatter, name = state
    return exchange_wait(sems, arrays_thru, lands_thru, scatter, after, name + "_wait")


def adamw_reduce(slots, w, m, v, l, name, after=None, prev=None):
    nl, r, c = w.shape
    tr = r
    while tr * c * 4 > (1 << 20) and tr % 16 == 0:
        tr //= 2
    bc1 = 1.0 - ADAM_B1 ** ADAM_STEP
    bc2 = 1.0 - ADAM_B2 ** ADAM_STEP

    def body(s_ref, w_ref, m_ref, v_ref, *rest):
        g_ref, d_ref, nm_ref, nv_ref = rest[-4:]
        g = s_ref[0].astype(F32)
        for j in range(1, N_DEV):
            g = g + s_ref[j].astype(F32)
        nm = ADAM_B1 * m_ref[...] + (1.0 - ADAM_B1) * g
        nv = ADAM_B2 * v_ref[...] + (1.0 - ADAM_B2) * (g * g)
        g_ref[...] = g
        nm_ref[...] = nm
        nv_ref[...] = nv
        d_ref[...] = -ADAM_LR * ((nm / bc1) / (jnp.sqrt(nv / bc2) + ADAM_EPS) + ADAM_WD * w_ref[...])

    blk = lambda: pl.BlockSpec((None, tr, c), lambda i: (l, i, 0))
    extra = ([] if after is None else [after]) + ([] if prev is None else list(prev))
    first_prev = 4 + (after is not None)
    return pl.pallas_call(
        body, grid=(r // tr,),
        in_specs=[pl.BlockSpec((N_DEV, tr, c), lambda i: (0, i, 0)), blk(), blk(), blk()] + [_ANY] * len(extra),
        out_specs=[blk(), blk(), blk(), blk()],
        out_shape=[jax.ShapeDtypeStruct((nl, r, c), F32)] * 4,
        input_output_aliases={} if prev is None else {first_prev + j: j for j in range(4)},
        compiler_params=_cp("parallel"), name=name)(slots, w, m, v, *extra)


BIG = ("ffn1_w_in", "ffn1_w_out", "w_in", "gdn_conv", "w_out", "ffn2_w_in", "ffn2_w_out")
GROUPS = (BIG[0:2], BIG[2:5], BIG[5:7])
SMALL = ("ffn1_norm", "mix_norm", "fox_q_norm", "fox_k_norm", "fox_f_bias", "gdn_a_log", "gdn_dt_bias",
         "gdn_out_norm", "ffn2_norm")
WEIGHTS = ("ffn1_norm", "ffn1_w_in", "ffn1_w_out", "mix_norm", "w_in", "fox_q_norm", "fox_k_norm", "fox_f_bias",
           "gdn_conv", "gdn_a_log", "gdn_dt_bias", "gdn_out_norm", "w_out", "ffn2_norm", "ffn2_w_in", "ffn2_w_out")
IN_COLS = (("fq", 512), ("fk", 512), ("fv", 512), ("ff", 8), ("gq", 512), ("gk", 512), ("gv", 512),
           ("ga", 4), ("gb", 4), ("gg", 512))
MY_BIG = ("fq", "fk", "fv", "gq", "gk", "gv", "gg")
MY_SMALL = ("ff", "ga", "gb")
SMALL_ROWS = 8 * 128


def _in_cols_to_mine(w):
    off, parts = 0, {}
    for nm, wd in IN_COLS:
        parts[nm] = w[:, off:off + wd]
        off += wd
    small = jnp.concatenate([parts[nm] for nm in MY_SMALL], axis=1)
    small = jnp.pad(small, ((0, 0), (0, 128 - small.shape[1])))
    return jnp.concatenate([parts[nm] for nm in MY_BIG] + [small], axis=1)


def _in_cols_from_mine(g):
    parts = {nm: g[:, i * 512:(i + 1) * 512] for i, nm in enumerate(MY_BIG)}
    off = N_BIG
    for nm in MY_SMALL:
        wd = dict(IN_COLS)[nm]
        parts[nm] = g[:, off:off + wd]
        off += wd
    return jnp.concatenate([parts[nm] for nm, _ in IN_COLS], axis=1)


def _pack_small(vals):
    rows = []
    nl = vals[SMALL[0]].shape[0]
    for l in range(nl):
        for nm in SMALL:
            v = vals[nm][l].reshape(-1)
            pad = (-v.shape[0]) % SMALL_ROWS
            rows.append(jnp.pad(v, (0, pad)).reshape(-1, 128))
    return jnp.concatenate(rows, axis=0)


def _unpack_small(packed, like):
    out = {nm: [] for nm in SMALL}
    row = 0
    nl = like[SMALL[0]].shape[0]
    for l in range(nl):
        for nm in SMALL:
            n = like[nm].shape[1]
            nr = -(-n // SMALL_ROWS) * 8
            out[nm].append(packed[row:row + nr].reshape(-1)[:n])
            row += nr
    return {nm: jnp.stack(v) for nm, v in out.items()}


def kernel(x, ffn1_norm, ffn1_w_in, ffn1_w_out, mix_norm, w_in, fox_q_norm, fox_k_norm, fox_f_bias, gdn_conv, gdn_a_log, gdn_dt_bias, gdn_out_norm, w_out, ffn2_norm, ffn2_w_in, ffn2_w_out, loss_target, m_ffn1_norm, m_ffn1_w_in, m_ffn1_w_out, m_mix_norm, m_w_in, m_fox_q_norm, m_fox_k_norm, m_fox_f_bias, m_gdn_conv, m_gdn_a_log, m_gdn_dt_bias, m_gdn_out_norm, m_w_out, m_ffn2_norm, m_ffn2_w_in, m_ffn2_w_out, v_ffn1_norm, v_ffn1_w_in, v_ffn1_w_out, v_mix_norm, v_w_in, v_fox_q_norm, v_fox_k_norm, v_fox_f_bias, v_gdn_conv, v_gdn_a_log, v_gdn_dt_bias, v_gdn_out_norm, v_w_out, v_ffn2_norm, v_ffn2_w_in, v_ffn2_w_out):
    wts = dict(ffn1_norm=ffn1_norm, ffn1_w_in=ffn1_w_in, ffn1_w_out=ffn1_w_out, mix_norm=mix_norm, w_in=w_in,
               fox_q_norm=fox_q_norm, fox_k_norm=fox_k_norm, fox_f_bias=fox_f_bias, gdn_conv=gdn_conv,
               gdn_a_log=gdn_a_log, gdn_dt_bias=gdn_dt_bias, gdn_out_norm=gdn_out_norm, w_out=w_out,
               ffn2_norm=ffn2_norm, ffn2_w_in=ffn2_w_in, ffn2_w_out=ffn2_w_out)
    mom = dict(ffn1_norm=m_ffn1_norm, ffn1_w_in=m_ffn1_w_in, ffn1_w_out=m_ffn1_w_out, mix_norm=m_mix_norm, w_in=m_w_in,
               fox_q_norm=m_fox_q_norm, fox_k_norm=m_fox_k_norm, fox_f_bias=m_fox_f_bias, gdn_conv=m_gdn_conv,
               gdn_a_log=m_gdn_a_log, gdn_dt_bias=m_gdn_dt_bias, gdn_out_norm=m_gdn_out_norm, w_out=m_w_out,
               ffn2_norm=m_ffn2_norm, ffn2_w_in=m_ffn2_w_in, ffn2_w_out=m_ffn2_w_out)
    var = dict(ffn1_norm=v_ffn1_norm, ffn1_w_in=v_ffn1_w_in, ffn1_w_out=v_ffn1_w_out, mix_norm=v_mix_norm, w_in=v_w_in,
               fox_q_norm=v_fox_q_norm, fox_k_norm=v_fox_k_norm, fox_f_bias=v_fox_f_bias, gdn_conv=v_gdn_conv,
               gdn_a_log=v_gdn_a_log, gdn_dt_bias=v_gdn_dt_bias, gdn_out_norm=v_gdn_out_norm, w_out=v_w_out,
               ffn2_norm=v_ffn2_norm, ffn2_w_in=v_ffn2_w_in, ffn2_w_out=v_ffn2_w_out)
    nb, seq, d = x.shape
    t = nb * seq
    depth = ffn1_norm.shape[0]

    stages = [(l, gi) for l in range(depth) for gi in range(len(GROUPS))]

    def shards_of(l, gi):
        return [wts[nm][l] if nm == "gdn_conv" else wts[nm][l].astype(BF16) for nm in GROUPS[gi]]

    def behind(nw, token):
        return nw if token is None else nw + token[0:1, 0:1]

    def small_params(l):
        return dict(
            n1=ffn1_norm[l][None], nmix=mix_norm[l][None], n2=ffn2_norm[l][None],
            qw=fox_q_norm[l][None], kw=fox_k_norm[l][None], onw=gdn_out_norm[l][None],
            gp=jnp.concatenate([
                jnp.concatenate([fox_f_bias[l], gdn_dt_bias[l], jnp.zeros((116,), F32)])[None],
                jnp.concatenate([jnp.zeros((8,), F32), gdn_a_log[l], jnp.zeros((116,), F32)])[None],
                jnp.zeros((6, 128), F32)], axis=0))

    h = x.reshape(t, d)
    landed = gather_two_level(shards_of(0, 0), "gather_0")
    saved = [dict(p=small_params(l)) for l in range(depth)]
    for k, (l, gi) in enumerate(stages):
        s, w, token = saved[l], landed, None
        p = s["p"]
        if k + 1 < len(stages):
            nl, ng = stages[k + 1]
            state, token = exchange_begin(shards_of(nl, ng), [False] * len(GROUPS[ng]), f"gather_{k + 1}", landed[0])
        if gi == 0:
            fb = w[0].shape[2]
            p["w1i"], p["w1o"] = w[0].reshape(2, 4, d, fb), w[1].reshape(4, fb, d)
            s["x0"] = h
            h, *s["ffn1"] = ffn_fwd(h, behind(p["n1"], token), p["w1i"], p["w1o"])
            s["x1"] = h
        elif gi == 1:
            p["wi"] = _in_cols_to_mine(w[0].transpose(1, 0, 2).reshape(d, -1))
            p["cw"] = w[1].transpose(1, 0, 2).reshape(CONV_W, -1)
            p["wo"] = w[2].reshape(d, d)
            proj, hn = inproj_fwd(h, behind(p["nmix"], token), p["wi"])
            gates = gates_fwd(proj, p["gp"], seq)
            yf, lse = attn_fwd(proj, gates, p["qw"], p["kw"], seq, tq=min(seq, ATTN_TQ_FWD))
            qh, kh, vh = gdn_pre_fwd(proj, p["cw"], seq)
            inv = gdn_inv(kh, gates, seq)
            yg, st = gdn_fwd(qh, kh, vh, gates, proj, p["onw"], inv, seq)
            h, ycat = outproj_fwd(h, yf, yg, p["wo"])
            s.update(x2=h, proj=proj, hn=hn, gates=gates, yf=yf, lse=lse, qh=qh, kh=kh, vh=vh, st=st, inv=inv, ycat=ycat)
        else:
            fb = w[0].shape[2]
            p["w2i"], p["w2o"] = w[0].reshape(2, 4, d, fb), w[1].reshape(4, fb, d)
            h, *s["ffn2"] = ffn_fwd(h, behind(p["n2"], token), p["w2i"], p["w2o"])
        if k + 1 < len(stages):
            landed = exchange_end(state, h)

    sq, dh = loss_head(h, loss_target.reshape(t, d))
    loss = lax.psum(0.5 * jnp.sum(sq) / d, ("x", "y", "c"))

    got = [None] * len(stages)
    pending, token = None, None
    gsmall = {nm: [None] * depth for nm in SMALL}
    for k in reversed(range(len(stages))):
        l, gi = stages[k]
        s = saved[l]
        p = s["p"]
        if gi != 1:
            nw, xin, wi_, wo_, nm_n, (xn, gu, hh) = (
                (p["n1"], s["x0"], p["w1i"], p["w1o"], "ffn1_norm", s["ffn1"]) if gi == 0 else
                (p["n2"], s["x2"], p["w2i"], p["w2o"], "ffn2_norm", s["ffn2"]))
            dh, dn, dgu, dyh = ffn_bwd(xin, dh, behind(nw, token), gu, wi_, wo_)
            g_in, g_out = wgrad_ffn_in(xn, dgu), wgrad_ffn_out(hh, dyh)
            send = [g_in.reshape(N_DEV, d, g_in.shape[3]), g_out.reshape(N_DEV, -1, d)]
            gsmall[nm_n][l] = dn[0]
        else:
            dyf, dyg, dyb = outproj_bwd(dh, p["wo"], token)
            g_wo = wgrad_2d(s["ycat"], dyb, 512, "wgrad_w_out")
            dq, dk, dv, dga, dqw, dkw = attn_bwd(s["proj"], s["gates"], p["qw"], p["kw"], s["yf"], s["lse"], dyf, seq,
                                                 tq=min(seq, ATTN_TQ_BWD))
            dqh, dkh, dvh, dgg, dgb, donw = gdn_bwd(s["qh"], s["kh"], s["vh"], s["gates"], s["proj"], p["onw"],
                                                     s["inv"], s["st"], dyg, seq)
            dxq, dxk, dxv, dwq, dwk, dwv = gdn_pre_bwd(s["proj"], p["cw"], dqh, dkh, dvh, seq)
            dsm, dgp = gates_bwd(s["proj"], p["gp"], dga, dgb, seq)
            dh, dnmix, dproj = inproj_bwd(s["x1"], dh, p["nmix"], p["wi"], [dq, dk, dv, dxq, dxk, dxv, dgg, dsm])
            g_wi = wgrad_2d(s["hn"], dproj, 512, "wgrad_w_in", F32)
            g_cw = jnp.concatenate([dwq, dwk, dwv], axis=1)
            send = [_in_cols_from_mine(g_wi).reshape(d, N_DEV, -1).transpose(1, 0, 2),
                    g_cw.reshape(CONV_W, N_DEV, -1).transpose(1, 0, 2), g_wo.reshape(N_DEV, -1, d)]
            for nm, val in (("mix_norm", dnmix[0]), ("fox_q_norm", dqw[0]), ("fox_k_norm", dkw[0]),
                            ("fox_f_bias", dgp[0, 0:8]), ("gdn_a_log", dgp[1, 8:12]), ("gdn_dt_bias", dgp[0, 8:12]),
                            ("gdn_out_norm", donw[0])):
                gsmall[nm][l] = val
        flags = [True] * len(send)
        if k == 0:
            send.append(_pack_small({nm: jnp.stack(v) for nm, v in gsmall.items()}))
            flags.append(False)
        prev = dh
        if pending is not None:
            got[pending[1]] = exchange_end(pending[0], dh)
            prev = got[pending[1]][0]
        state, token = exchange_begin(send, flags, f"exchange_grads_{k}", prev)
        pending = (state, k)
    grad_x = dh.reshape(nb, seq, d)

    res = {}

    def update_stage(k, slots, after):
        l, gi = stages[k]
        for i, nm in enumerate(GROUPS[gi]):
            r, c = wts[nm].shape[1:]
            res[nm] = adamw_reduce(slots[i].reshape(N_DEV, r, c), wts[nm], mom[nm], var[nm], l, f"adamw_{nm}_{l}",
                                   after, res.get(nm))
            if after is not None:
                after = res[nm][0]
        return after

    last = token
    for k in range(1, len(stages)):
        last = update_stage(k, got[k], last)
    got[0] = exchange_end(pending[0], last)
    update_stage(0, got[0], None)
    small_like = {nm: wts[nm] for nm in SMALL}
    sm = adamw_reduce(got[0][-1], _pack_small(small_like)[None], _pack_small({nm: mom[nm] for nm in SMALL})[None],
                      _pack_small({nm: var[nm] for nm in SMALL})[None], 0, "adamw_small")
    sm = [_unpack_small(a[0], small_like) for a in sm]
    for nm in SMALL:
        res[nm] = [sm[j][nm] for j in range(4)]
    return (loss, grad_x, *[res[nm][0] for nm in WEIGHTS], *[res[nm][1] for nm in WEIGHTS],
            *[res[nm][2] for nm in WEIGHTS], *[res[nm][3] for nm in WEIGHTS])
```

```python
import jax
import jax.numpy as jnp
from jax import lax
from jax.experimental import pallas as pl
from jax.experimental.pallas import tpu as pltpu

F32 = jnp.float32
BF16 = jnp.bfloat16
EPS = 1e-6
N_DEV = 8
MESH = pl.DeviceIdType.MESH
HIGHEST = lax.Precision.HIGHEST
VMEM_LIMIT = 56 * 1024 * 1024

FOX_HEADS, FOX_DH = 8, 64
GDN_HEADS, GDN_DH = 4, 128
CHUNK = 64
CONV_W = 4

ADAM_LR, ADAM_B1, ADAM_B2, ADAM_EPS, ADAM_WD, ADAM_STEP = 0.001, 0.9, 0.999, 1e-08, 0.01, 10


def _cp(*sem):
    return pltpu.CompilerParams(dimension_semantics=sem, vmem_limit_bytes=VMEM_LIMIT)


def _dot(a, b):
    return jnp.dot(a, b, preferred_element_type=F32)


def _dot_nt(a, b):
    return lax.dot_general(a, b, (((1,), (1,)), ((), ())), preferred_element_type=F32)


def _dot_tn(a, b):
    return lax.dot_general(a, b, (((0,), (0,)), ((), ())), preferred_element_type=F32)


def _rstd(xf):
    return lax.rsqrt(jnp.mean(xf * xf, axis=-1, keepdims=True) + EPS)


def _rms_bwd(xf, r, dyn):
    return r * dyn - xf * (r * r * r) * jnp.mean(dyn * xf, axis=-1, keepdims=True)


def ffn_fwd(x, nw, w_in, w_out, tm=1024, rc=1024):
    t, d = x.shape
    nj, fb = w_out.shape[0], w_out.shape[1]
    tm = min(tm, t)
    rc = min(rc, tm)

    def body(x_ref, nw_ref, wi_ref, wo_ref, o_ref, xn_ref, gu_ref, h_ref, acc_ref):
        j = pl.program_id(1)

        @pl.when(j == 0)
        def _():
            xf = x_ref[...]
            xn_ref[...] = (xf * _rstd(xf) * nw_ref[...]).astype(BF16)
            acc_ref[...] = jnp.zeros_like(acc_ref)

        rows = [slice(c * rc, (c + 1) * rc) for c in range(tm // rc)]
        gs = [_dot(xn_ref[r, :], wi_ref[0]) for r in rows]
        us = [_dot(xn_ref[r, :], wi_ref[1]) for r in rows]
        hs = []
        for g, u, r in zip(gs, us, rows):
            sg = jax.nn.sigmoid(g)
            silu = g * sg
            h = (silu * u).astype(BF16)
            gu_ref[0, r, :] = (u * (sg * (1.0 + g * (1.0 - sg)))).astype(BF16)
            gu_ref[1, r, :] = silu.astype(BF16)
            h_ref[r, :] = h
            hs.append(h)
        for h, r in zip(hs, rows):
            acc_ref[r, :] += _dot(h, wo_ref[...])

        @pl.when(j == nj - 1)
        def _():
            o_ref[...] = x_ref[...] + 0.5 * acc_ref[...]

    return pl.pallas_call(
        body, grid=(t // tm, nj),
        in_specs=[pl.BlockSpec((tm, d), lambda i, j: (i, 0)),
                  pl.BlockSpec((1, d), lambda i, j: (0, 0)),
                  pl.BlockSpec((2, None, d, fb), lambda i, j: (0, j, 0, 0)),
                  pl.BlockSpec((None, fb, d), lambda i, j: (j, 0, 0))],
        out_specs=[pl.BlockSpec((tm, d), lambda i, j: (i, 0)),
                   pl.BlockSpec((tm, d), lambda i, j: (i, 0)),
                   pl.BlockSpec((2, None, tm, fb), lambda i, j: (0, j, i, 0)),
                   pl.BlockSpec((None, tm, fb), lambda i, j: (j, i, 0))],
        out_shape=[jax.ShapeDtypeStruct((t, d), F32), jax.ShapeDtypeStruct((t, d), BF16),
                   jax.ShapeDtypeStruct((2, nj, t, fb), BF16), jax.ShapeDtypeStruct((nj, t, fb), BF16)],
        scratch_shapes=[pltpu.VMEM((tm, d), F32)],
        compiler_params=_cp("parallel", "arbitrary"), name="ffn_fwd")(x, nw, w_in, w_out)


def ffn_bwd(x, dy, nw, gu, w_in, w_out, tm=512, rc=256):
    t, d = x.shape
    nj, fb = w_out.shape[0], w_out.shape[1]
    tm = min(tm, t)
    rc = min(rc, tm)

    def body(x_ref, dy_ref, nw_ref, gu_ref, wi_ref, wo_ref,
             dx_ref, dnw_ref, dgu_ref, dyh_ref, acc_ref):
        i, j = pl.program_id(0), pl.program_id(1)

        @pl.when(j == 0)
        def _():
            dyh_ref[...] = (0.5 * dy_ref[...]).astype(BF16)
            acc_ref[...] = jnp.zeros_like(acc_ref)

        @pl.when((i == 0) & (j == 0))
        def _():
            dnw_ref[...] = jnp.zeros_like(dnw_ref)

        rows = [slice(c * rc, (c + 1) * rc) for c in range(tm // rc)]
        dhs = [_dot_nt(dyh_ref[r, :], wo_ref[...]) for r in rows]
        dgs = [(dh * gu_ref[0, r, :].astype(F32)).astype(BF16) for dh, r in zip(dhs, rows)]
        dus = [(dh * gu_ref[1, r, :].astype(F32)).astype(BF16) for dh, r in zip(dhs, rows)]
        for dg, du, r in zip(dgs, dus, rows):
            dgu_ref[0, r, :] = dg
            dgu_ref[1, r, :] = du
        for dg, du, r in zip(dgs, dus, rows):
            acc_ref[r, :] += _dot_nt(dg, wi_ref[0]) + _dot_nt(du, wi_ref[1])

        @pl.when(j == nj - 1)
        def _():
            xf = x_ref[...]
            r = _rstd(xf)
            dxn = acc_ref[...]
            dnw_ref[...] += jnp.sum(dxn * xf * r, axis=0, keepdims=True)
            dx_ref[...] = _rms_bwd(xf, r, dxn * nw_ref[...]) + dy_ref[...]

    return pl.pallas_call(
        body, grid=(t // tm, nj),
        in_specs=[pl.BlockSpec((tm, d), lambda i, j: (i, 0)),
                  pl.BlockSpec((tm, d), lambda i, j: (i, 0)),
                  pl.BlockSpec((1, d), lambda i, j: (0, 0)),
                  pl.BlockSpec((2, None, tm, fb), lambda i, j: (0, j, i, 0)),
                  pl.BlockSpec((2, None, d, fb), lambda i, j: (0, j, 0, 0)),
                  pl.BlockSpec((None, fb, d), lambda i, j: (j, 0, 0))],
        out_specs=[pl.BlockSpec((tm, d), lambda i, j: (i, 0)),
                   pl.BlockSpec((1, d), lambda i, j: (0, 0)),
                   pl.BlockSpec((2, None, tm, fb), lambda i, j: (0, j, i, 0)),
                   pl.BlockSpec((tm, d), lambda i, j: (i, 0))],
        out_shape=[jax.ShapeDtypeStruct((t, d), F32),
                   jax.ShapeDtypeStruct((1, d), F32),
                   jax.ShapeDtypeStruct((2, nj, t, fb), BF16),
                   jax.ShapeDtypeStruct((t, d), BF16)],
        scratch_shapes=[pltpu.VMEM((tm, d), F32)],
        compiler_params=_cp("arbitrary", "arbitrary"), name="ffn_bwd")(x, dy, nw, gu, w_in, w_out)


def _wgrad_call(a, b, a_spec, b_spec, out_shape, out_spec, grid, name, out_dtype=BF16):
    last = len(grid) - 1
    acc_shape = tuple(s for s in out_spec.block_shape if s is not None)

    def body(a_ref, b_ref, o_ref, acc_ref):
        @pl.when(pl.program_id(last) == 0)
        def _():
            acc_ref[...] = jnp.zeros_like(acc_ref)

        if len(acc_shape) == 3:
            shared_a = a_ref[...] if len(a_ref.shape) == 2 else None
            shared_b = b_ref[...] if len(b_ref.shape) == 2 else None
            for s in range(acc_shape[0]):
                acc_ref[s] += _dot_tn(a_ref[s] if shared_a is None else shared_a,
                                      b_ref[s] if shared_b is None else shared_b)
        else:
            acc_ref[...] += _dot_tn(a_ref[...], b_ref[...])

        @pl.when(pl.program_id(last) == grid[last] - 1)
        def _():
            o_ref[...] = acc_ref[...].astype(o_ref.dtype)

    sem = ("parallel",) * last + ("arbitrary",)
    return pl.pallas_call(body, grid=grid, in_specs=[a_spec, b_spec], out_specs=out_spec,
                          out_shape=jax.ShapeDtypeStruct(out_shape, out_dtype),
                          scratch_shapes=[pltpu.VMEM(acc_shape, F32)],
                          compiler_params=_cp(*sem), name=name)(a, b)


WGRAD_TM = 2048
WGRAD_TM_MIXER = 1024


def wgrad_ffn_in(xn, dgu, tm=WGRAD_TM):
    t, d = xn.shape
    _, nj, _, fb = dgu.shape
    tm = min(tm, t)
    return _wgrad_call(xn, dgu,
                       pl.BlockSpec((tm, d), lambda j, k: (k, 0)),
                       pl.BlockSpec((2, None, tm, fb), lambda j, k: (0, j, k, 0)),
                       (2, nj, d, fb), pl.BlockSpec((2, None, d, fb), lambda j, k: (0, j, 0, 0)),
                       (nj, t // tm), "wgrad_ffn_in")


def wgrad_ffn_out(h, dyh, tm=WGRAD_TM):
    nj, t, fb = h.shape
    d = dyh.shape[1]
    tm = min(tm, t)
    return _wgrad_call(h, dyh,
                       pl.BlockSpec((nj, tm, fb), lambda k: (0, k, 0)),
                       pl.BlockSpec((tm, d), lambda k: (k, 0)),
                       (nj, fb, d), pl.BlockSpec((nj, fb, d), lambda k: (0, 0, 0)),
                       (t // tm,), "wgrad_ffn_out")


def wgrad_2d(a, b, tk, name, out_dtype=BF16, tm=WGRAD_TM_MIXER):
    t, k = a.shape
    n = b.shape[1]
    tm = min(tm, t)
    return _wgrad_call(a, b,
                       pl.BlockSpec((tm, tk), lambda c, s: (s, c)),
                       pl.BlockSpec((tm, n), lambda c, s: (s, 0)),
                       (k, n), pl.BlockSpec((tk, n), lambda c, s: (c, 0)),
                       (k // tk, t // tm), name, out_dtype)


N_BIG = 7 * 512
N_PROJ = N_BIG + 128
COL_SMALL = N_BIG // 128


def inproj_fwd(x, nw, w, tm=512):
    t, d = x.shape
    n = w.shape[1]

    def body(x_ref, nw_ref, w_ref, p_ref, hn_ref):
        xf = x_ref[...]
        hn = (xf * _rstd(xf) * nw_ref[...]).astype(BF16)
        hn_ref[...] = hn
        p_ref[...] = _dot(hn, w_ref[...])

    return pl.pallas_call(
        body, grid=(t // tm,),
        in_specs=[pl.BlockSpec((tm, d), lambda i: (i, 0)), pl.BlockSpec((1, d), lambda i: (0, 0)),
                  pl.BlockSpec((d, n), lambda i: (0, 0))],
        out_specs=[pl.BlockSpec((tm, n), lambda i: (i, 0)), pl.BlockSpec((tm, d), lambda i: (i, 0))],
        out_shape=[jax.ShapeDtypeStruct((t, n), F32), jax.ShapeDtypeStruct((t, d), BF16)],
        compiler_params=_cp("parallel"), name="inproj_fwd")(x, nw, w)


def inproj_bwd(x, dres, nw, w, dparts, tm=512):
    t, d = x.shape
    n = w.shape[1]
    widths = [p.shape[1] for p in dparts]
    assert sum(widths) == n

    def body(x_ref, dres_ref, nw_ref, w_ref, *rest):
        part_refs, (dx_ref, dnw_ref, dp_ref) = rest[:len(widths)], rest[len(widths):]

        @pl.when(pl.program_id(0) == 0)
        def _():
            dnw_ref[...] = jnp.zeros_like(dnw_ref)

        dp = jnp.concatenate([r[...].astype(BF16) for r in part_refs], axis=1)
        dp_ref[...] = dp
        dhn = _dot_nt(dp, w_ref[...])
        xf = x_ref[...]
        r = _rstd(xf)
        dnw_ref[...] += jnp.sum(dhn * xf * r, axis=0, keepdims=True)
        dx_ref[...] = _rms_bwd(xf, r, dhn * nw_ref[...]) + dres_ref[...]

    return pl.pallas_call(
        body, grid=(t // tm,),
        in_specs=[pl.BlockSpec((tm, d), lambda i: (i, 0)), pl.BlockSpec((tm, d), lambda i: (i, 0)),
                  pl.BlockSpec((1, d), lambda i: (0, 0)), pl.BlockSpec((d, n), lambda i: (0, 0))]
                 + [pl.BlockSpec((tm, wd), lambda i: (i, 0)) for wd in widths],
        out_specs=[pl.BlockSpec((tm, d), lambda i: (i, 0)), pl.BlockSpec((1, d), lambda i: (0, 0)),
                   pl.BlockSpec((tm, n), lambda i: (i, 0))],
        out_shape=[jax.ShapeDtypeStruct((t, d), F32), jax.ShapeDtypeStruct((1, d), F32),
                   jax.ShapeDtypeStruct((t, n), BF16)],
        compiler_params=_cp("arbitrary"), name="inproj_bwd")(x, dres, nw, w, *dparts)


def outproj_fwd(x, yf, yg, w, tm=1024):
    t, d = x.shape
    hw = yf.shape[1]
    tm = min(tm, t)

    def body(x_ref, yf_ref, yg_ref, w_ref, o_ref, y_ref):
        y = jnp.concatenate([yf_ref[...], yg_ref[...]], axis=1).astype(BF16)
        y_ref[...] = y
        o_ref[...] = x_ref[...] + _dot(y, w_ref[...])

    return pl.pallas_call(
        body, grid=(t // tm,),
        in_specs=[pl.BlockSpec((tm, d), lambda i: (i, 0)), pl.BlockSpec((tm, hw), lambda i: (i, 0)),
                  pl.BlockSpec((tm, hw), lambda i: (i, 0)), pl.BlockSpec((2 * hw, d), lambda i: (0, 0))],
        out_specs=[pl.BlockSpec((tm, d), lambda i: (i, 0)), pl.BlockSpec((tm, 2 * hw), lambda i: (i, 0))],
        out_shape=[jax.ShapeDtypeStruct((t, d), F32), jax.ShapeDtypeStruct((t, 2 * hw), BF16)],
        compiler_params=_cp("parallel"), name="outproj_fwd")(x, yf, yg, w)


def outproj_bwd(dy, w, after=None, tm=1024):
    t, d = dy.shape
    hw = w.shape[0] // 2
    tm = min(tm, t)
    extra = [] if after is None else [after]

    def body(dy_ref, w_ref, *rest):
        df_ref, dg_ref, dyb_ref = rest[-3:]
        dyb = dy_ref[...].astype(BF16)
        dyb_ref[...] = dyb
        dyy = _dot_nt(dyb, w_ref[...])
        df_ref[...] = dyy[:, :hw]
        dg_ref[...] = dyy[:, hw:]

    return pl.pallas_call(
        body, grid=(t // tm,),
        in_specs=[pl.BlockSpec((tm, d), lambda i: (i, 0)), pl.BlockSpec((2 * hw, d), lambda i: (0, 0))]
                 + [pl.BlockSpec(memory_space=pl.ANY)] * len(extra),
        out_specs=[pl.BlockSpec((tm, hw), lambda i: (i, 0)), pl.BlockSpec((tm, hw), lambda i: (i, 0)),
                   pl.BlockSpec((tm, d), lambda i: (i, 0))],
        out_shape=[jax.ShapeDtypeStruct((t, hw), F32), jax.ShapeDtypeStruct((t, hw), F32),
                   jax.ShapeDtypeStruct((t, d), BF16)],
        compiler_params=_cp("parallel"), name="outproj_bwd")(dy, w, *extra)


def _lane(shape):
    return lax.broadcasted_iota(jnp.int32, shape, 1)


def _row(shape):
    return lax.broadcasted_iota(jnp.int32, shape, 0)


def _gate_terms(val, gp_ref):
    z = val + gp_ref[0:1, :]
    sp = jnp.log(1.0 + jnp.exp(-jnp.abs(z)))
    return z, sp


def gates_fwd(proj, gp, seq, ts=512):
    t = proj.shape[0]
    nb, ns = t // seq, seq // ts

    def body(v_ref, gp_ref, o_ref, carry_ref):
        @pl.when(pl.program_id(1) == 0)
        def _():
            carry_ref[...] = jnp.zeros_like(carry_ref)

        z, sp = _gate_terms(v_ref[...], gp_ref)
        logsig = jnp.minimum(z, 0.0) - sp
        tri = (_row((ts, ts)) >= _lane((ts, ts))).astype(F32)
        cum = jnp.dot(tri, logsig, precision=HIGHEST, preferred_element_type=F32) + carry_ref[0:1, :]
        carry_ref[0:1, :] = cum[ts - 1:ts, :]
        g = -jnp.exp(gp_ref[1:2, :]) * (jnp.maximum(z, 0.0) + sp)
        beta = jax.nn.sigmoid(z)
        lane = _lane((ts, 128))
        o_ref[...] = jnp.where(lane < 8, cum, jnp.where(lane < 12, g, jnp.where(lane < 16, beta, 0.0)))

    return pl.pallas_call(
        body, grid=(nb, ns),
        in_specs=[pl.BlockSpec((ts, 128), lambda b, s: (b * ns + s, COL_SMALL)),
                  pl.BlockSpec((8, 128), lambda b, s: (0, 0))],
        out_specs=pl.BlockSpec((ts, 128), lambda b, s: (b * ns + s, 0)),
        out_shape=jax.ShapeDtypeStruct((t, 128), F32),
        scratch_shapes=[pltpu.VMEM((8, 128), F32)],
        compiler_params=_cp("parallel", "arbitrary"), name="gates_fwd")(proj, gp)


def gates_bwd(proj, gp, dga, dgb, seq, ts=512):
    t = proj.shape[0]
    nb, ns = t // seq, seq // ts

    def body(v_ref, gp_ref, da_ref, db_ref, ds_ref, dgp_ref, carry_ref):
        @pl.when(pl.program_id(1) == 0)
        def _():
            carry_ref[...] = jnp.zeros_like(carry_ref)

        @pl.when((pl.program_id(0) == 0) & (pl.program_id(1) == 0))
        def _():
            dgp_ref[...] = jnp.zeros_like(dgp_ref)

        lane = _lane((ts, 128))
        dgate = jnp.where(lane < 8, da_ref[...], jnp.where(lane < 16, db_ref[...], 0.0))
        z, sp = _gate_terms(v_ref[...], gp_ref)
        triu = (_row((ts, ts)) <= _lane((ts, ts))).astype(F32)
        dlog = jnp.dot(triu, dgate, precision=HIGHEST, preferred_element_type=F32) + carry_ref[0:1, :]
        carry_ref[0:1, :] = dlog[0:1, :]
        sig = jax.nn.sigmoid(z)
        nea = -jnp.exp(gp_ref[1:2, :])
        g = nea * (jnp.maximum(z, 0.0) + sp)
        dz = jnp.where(lane < 8, dlog * (1.0 - sig),
                       jnp.where(lane < 12, dgate * nea * sig, dgate * sig * (1.0 - sig)))
        dz = jnp.where(lane < 16, dz, 0.0)
        ds_ref[...] = dz.astype(BF16)
        dgp_ref[0:1, :] += jnp.where(lane[0:1] < 12, jnp.sum(dz, axis=0, keepdims=True), 0.0)
        dgp_ref[1:2, :] += jnp.where((lane[0:1] >= 8) & (lane[0:1] < 12), jnp.sum(dgate * g, axis=0, keepdims=True), 0.0)

    rev = lambda b, s: (b * ns + (ns - 1 - s), 0)
    return pl.pallas_call(
        body, grid=(nb, ns),
        in_specs=[pl.BlockSpec((ts, 128), lambda b, s: (b * ns + (ns - 1 - s), COL_SMALL)),
                  pl.BlockSpec((8, 128), lambda b, s: (0, 0)),
                  pl.BlockSpec((ts, 128), rev), pl.BlockSpec((ts, 128), rev)],
        out_specs=[pl.BlockSpec((ts, 128), rev), pl.BlockSpec((8, 128), lambda b, s: (0, 0))],
        out_shape=[jax.ShapeDtypeStruct((t, 128), BF16), jax.ShapeDtypeStruct((8, 128), F32)],
        scratch_shapes=[pltpu.VMEM((8, 128), F32)],
        compiler_params=_cp("arbitrary", "arbitrary"), name="gates_bwd")(proj, gp, dga, dgb)


NEG = -1e30
ATTN_TQ_FWD = 1024
ATTN_TQ_BWD = 512


def _pick_lane(tile, idx):
    return jnp.sum(jnp.where(_lane(tile.shape) == idx, tile, 0.0), axis=1, keepdims=True)


def _row_to_col(row, n):
    return jnp.sum(jnp.where(_row((n, n)) == _lane((n, n)), row, 0.0), axis=1, keepdims=True)


def _rows(i, n):
    return pl.ds(pl.multiple_of(i * n, n), n)


LOG2E = 1.4426950408889634
LN2 = 0.6931471805599453


def _split_dot(x, mat, passes):
    total, rest = None, x
    for _ in range(passes):
        part = rest.astype(BF16)
        rest = rest - part.astype(F32)
        total = _dot(part, mat) if total is None else total + _dot(part, mat)
    return total


def _pair_mats(p, dh):
    r, l = _row((128, 128)), _lane((128, 128))
    same = (r < dh) == (l < dh)
    upper = (l >= dh).astype(jnp.int32)
    as_bf16 = lambda m: m.astype(BF16)
    return dict(own=as_bf16(same), other=as_bf16(jnp.logical_not(same)), pick_other=as_bf16(r == 2 * p + 1 - upper),
                swap=as_bf16(((r == 0) & (l >= dh)) | ((r == dh) & (l < dh))))


def _pair_rstd(x2, sel, dh):
    return lax.rsqrt(_split_dot(x2 * x2, sel["own"], 2) * (1.0 / dh) + EPS)


def _pair_aug(cols, n, dh):
    lane = _lane((n, 128))
    li = jnp.where(lane >= dh, lane - dh, lane)
    out = jnp.zeros((n, 128), F32)
    for i, c in enumerate(cols):
        out = jnp.where(li == i, c, out)
    return out


def _split3(x):
    hi = x.astype(BF16).astype(F32)
    mid = (x - hi).astype(BF16).astype(F32)
    return [hi, mid, (x - hi - mid).astype(BF16).astype(F32)]


def _once(shape, index_map):
    return pl.BlockSpec(shape, index_map, pipeline_mode=pl.Buffered(1))


def attn_fwd(proj, gates, qw, kw, seq, tq=256):
    t = proj.shape[0]
    nb, nq, dh = t // seq, seq // tq, FOX_DH
    scale = dh ** -0.5

    def body(q_ref, k_ref, v_ref, g_ref, qw_ref, kw_ref, y_ref, lse_ref, qs, ks, vs):
        p = pl.program_id(1)
        heads = range(2)
        low = _lane((tq, 128)) < dh
        sel = _pair_mats(p, dh)

        def prep(i, _):
            r = _rows(i, tq)
            q2, k2 = q_ref[r, :], k_ref[r, :]
            cc = _split_dot(g_ref[r, :], sel["pick_other"], 3) * LOG2E
            qn = q2 * _pair_rstd(q2, sel, dh) * qw_ref[...] * (scale * LOG2E)
            kn = k2 * _pair_rstd(k2, sel, dh) * kw_ref[...]
            qx = _pair_aug(_split3(cc) + [1.0, 1.0, 1.0], tq, dh)
            kx = _pair_aug([1.0, 1.0, 1.0] + _split3(-cc), tq, dh)
            for hh in heads:
                own = low if hh == 0 else jnp.logical_not(low)
                qs[hh, r, :] = jnp.where(own, qn, qx).astype(BF16)
                ks[hh, r, :] = jnp.where(own, kn, kx).astype(BF16)
            vs[r, :] = v_ref[r, :].astype(BF16)
            return 0

        lax.fori_loop(0, nq, prep, 0)

        def q_tile(i, _):
            r = _rows(i, tq)
            qt = [qs[hh, r, :] for hh in heads]

            def kv_step(j, carry, masked):
                kr = _rows(j, tq)
                vt = vs[kr, :]
                out = []
                for hh in heads:
                    m, l, acc = carry[hh]
                    s = _dot_nt(qt[hh], ks[hh, kr, :])
                    if masked:
                        s = jnp.where(_row((tq, tq)) >= _lane((tq, tq)), s, NEG)
                    m_new = jnp.maximum(m, jnp.max(s, axis=1, keepdims=True))
                    pe = jnp.exp2(s - m_new)
                    a = jnp.exp2(m - m_new)
                    out.append((m_new, a * l + jnp.sum(pe, axis=1, keepdims=True), a * acc + _dot(pe.astype(BF16), vt)))
                return tuple(out)

            one = (jnp.full((tq, 1), NEG, F32), jnp.zeros((tq, 1), F32), jnp.zeros((tq, 128), F32))
            carry = lax.fori_loop(0, i, lambda j, c: kv_step(j, c, False), (one, one))
            (m0, l0, acc0), (m1, l1, acc1) = kv_step(i, carry, True)
            y_ref[r, :] = jnp.where(low, acc0 / l0, acc1 / l1)
            lse_ref[r, :] = jnp.where(low, m0 + jnp.log2(l0), m1 + jnp.log2(l1))
            return 0

        lax.fori_loop(0, nq, q_tile, 0)

    blk = lambda off: _once((seq, 128), lambda b, p: (b, off + p))
    return pl.pallas_call(
        body, grid=(nb, 4),
        in_specs=[blk(0), blk(4), blk(8), _once((seq, 128), lambda b, p: (b, 0)),
                  pl.BlockSpec((1, 128), lambda b, p: (0, 0)), pl.BlockSpec((1, 128), lambda b, p: (0, 0))],
        out_specs=[pl.BlockSpec((seq, 128), lambda b, p: (b, p)), pl.BlockSpec((seq, 128), lambda b, p: (b, p))],
        out_shape=[jax.ShapeDtypeStruct((t, 512), F32), jax.ShapeDtypeStruct((t, 512), F32)],
        scratch_shapes=[pltpu.VMEM((2, seq, 128), BF16), pltpu.VMEM((2, seq, 128), BF16), pltpu.VMEM((seq, 128), BF16)],
        compiler_params=_cp("parallel", "arbitrary"),
        name="attn_fwd")(proj, proj, proj, gates, jnp.tile(qw, (1, 2)), jnp.tile(kw, (1, 2)))


def attn_bwd(proj, gates, qw, kw, y, lse, dy, seq, tq=256):
    t = proj.shape[0]
    nb, nq, dh = t // seq, seq // tq, FOX_DH
    scale = dh ** -0.5

    def body(q_ref, k_ref, v_ref, g_ref, qw_ref, kw_ref, y_ref, lse_ref, dy_ref,
             dq_ref, dk_ref, dv_ref, dg_ref, dqw_ref, dkw_ref,
             qs, ks, vs, dos, dsrow, dqa, dka):
        b, p = pl.program_id(0), pl.program_id(1)

        @pl.when((b == 0) & (p == 0))
        def _():
            dqw_ref[...] = jnp.zeros_like(dqw_ref)
            dkw_ref[...] = jnp.zeros_like(dkw_ref)

        @pl.when(p == 0)
        def _():
            dg_ref[...] = jnp.zeros_like(dg_ref)

        heads = range(2)
        low = _lane((tq, 128)) < dh
        sel = _pair_mats(p, dh)

        def prep(i, _):
            r = _rows(i, tq)
            q2, k2, dy2 = q_ref[r, :], k_ref[r, :], dy_ref[r, :]
            cc = _split_dot(g_ref[r, :], sel["pick_other"], 3) * LOG2E
            lse_x = _split_dot(lse_ref[r, :], sel["swap"], 3)
            delta_x = _split_dot(dy2 * y_ref[r, :], sel["other"], 2)
            qn = q2 * _pair_rstd(q2, sel, dh) * qw_ref[...] * (scale * LOG2E)
            kn = k2 * _pair_rstd(k2, sel, dh) * kw_ref[...]
            qx = _pair_aug(_split3(cc) + [1.0, 1.0, 1.0] + _split3(-lse_x), tq, dh)
            kx = _pair_aug([1.0, 1.0, 1.0] + _split3(-cc) + [1.0, 1.0, 1.0], tq, dh)
            vx = _pair_aug([1.0, 1.0, 1.0], tq, dh)
            dx = _pair_aug(_split3(-delta_x), tq, dh)
            for hh in heads:
                own = low if hh == 0 else jnp.logical_not(low)
                qs[hh, r, :] = jnp.where(own, qn, qx).astype(BF16)
                ks[hh, r, :] = jnp.where(own, kn, kx).astype(BF16)
                vs[hh, r, :] = jnp.where(own, v_ref[r, :], vx).astype(BF16)
                dos[hh, r, :] = jnp.where(own, dy2, dx).astype(BF16)
                dsrow[hh, r, :] = jnp.zeros((tq, 1), F32)
                dqa[hh, r, :] = jnp.zeros((tq, 128), F32)
            return 0

        lax.fori_loop(0, nq, prep, 0)

        def kv_tile(j, _):
            kr = _rows(j, tq)
            kt = [ks[hh, kr, :] for hh in heads]
            vt = [vs[hh, kr, :] for hh in heads]

            def q_step(i, carry, masked):
                r = _rows(i, tq)
                out = []
                for hh in heads:
                    dk, dv, dcr = carry[hh]
                    qt, dot = qs[hh, r, :], dos[hh, r, :]
                    s = _dot_nt(qt, kt[hh])
                    if masked:
                        s = jnp.where(_row((tq, tq)) >= _lane((tq, tq)), s, NEG)
                    pe = jnp.exp2(s)
                    ds = pe * _dot_nt(dot, vt[hh])
                    dsb = ds.astype(BF16)
                    dqa[hh, r, :] += _dot(dsb, kt[hh])
                    dsrow[hh, r, :] += jnp.sum(ds, axis=1, keepdims=True)
                    out.append((dk + _dot_tn(dsb, qt), dv + _dot_tn(pe.astype(BF16), dot),
                                dcr - jnp.sum(ds, axis=0, keepdims=True)))
                return tuple(out)

            one = (jnp.zeros((tq, 128), F32), jnp.zeros((tq, 128), F32), jnp.zeros((1, tq), F32))
            carry = q_step(j, (one, one), True)
            (dk0, dv0, dcr0), (dk1, dv1, dcr1) = lax.fori_loop(j + 1, nq, lambda i, c: q_step(i, c, False), carry)
            dka[kr, :] = jnp.where(low, dk0, dk1)
            dv_ref[kr, :] = jnp.where(low, dv0, dv1).astype(BF16)
            lane = _lane((tq, 128))
            dg_ref[kr, :] = jnp.where(lane == 2 * p, _row_to_col(dcr0, tq),
                                      jnp.where(lane == 2 * p + 1, _row_to_col(dcr1, tq), dg_ref[kr, :]))
            return 0

        lax.fori_loop(0, nq, kv_tile, 0)

        def post(i, _):
            r = _rows(i, tq)
            q2, k2 = q_ref[r, :], k_ref[r, :]
            rq, rk = _pair_rstd(q2, sel, dh), _pair_rstd(k2, sel, dh)
            dqn = jnp.where(low, dqa[0, r, :], dqa[1, r, :]) * scale
            dkn = dka[r, :] * LN2
            dqw_ref[...] += jnp.sum(dqn * q2 * rq, axis=0, keepdims=True)
            dkw_ref[...] += jnp.sum(dkn * k2 * rk, axis=0, keepdims=True)
            for x2, rr, dyn, o_ref in ((q2, rq, dqn * qw_ref[...], dq_ref), (k2, rk, dkn * kw_ref[...], dk_ref)):
                mean = _split_dot(dyn * x2, sel["own"], 2) * (1.0 / dh)
                o_ref[r, :] = (rr * dyn - x2 * (rr * rr * rr) * mean).astype(BF16)
            lane = _lane((tq, 128))
            dg_ref[r, :] += jnp.where(lane == 2 * p, dsrow[0, r, :], jnp.where(lane == 2 * p + 1, dsrow[1, r, :], 0.0))
            return 0

        lax.fori_loop(0, nq, post, 0)

    blk = lambda off: _once((seq, 128), lambda b, p: (b, off + p))
    own = lambda: _once((seq, 128), lambda b, p: (b, p))
    vec = lambda: pl.BlockSpec((1, 128), lambda b, p: (0, 0))
    res = pl.pallas_call(
        body, grid=(nb, 4),
        in_specs=[blk(0), blk(4), blk(8), _once((seq, 128), lambda b, p: (b, 0)), vec(), vec(), own(), own(), own()],
        out_specs=[own(), own(), own(), _once((seq, 128), lambda b, p: (b, 0)), vec(), vec()],
        out_shape=[jax.ShapeDtypeStruct((t, 512), BF16)] * 3
                  + [jax.ShapeDtypeStruct((t, 128), F32), jax.ShapeDtypeStruct((1, 128), F32), jax.ShapeDtypeStruct((1, 128), F32)],
        scratch_shapes=[pltpu.VMEM((2, seq, 128), BF16)] * 4
                       + [pltpu.VMEM((2, seq, 1), F32), pltpu.VMEM((2, seq, 128), F32), pltpu.VMEM((seq, 128), F32)],
        compiler_params=_cp("arbitrary", "arbitrary"),
        name="attn_bwd")(proj, proj, proj, gates, jnp.tile(qw, (1, 2)), jnp.tile(kw, (1, 2)), y, lse, dy)
    return list(res[:4]) + [res[4][:, :dh] + res[4][:, dh:], res[5][:, :dh] + res[5][:, dh:]]


def _silu_grad(c, sg):
    return sg * (1.0 + c * (1.0 - sg))


def _conv(x, w, n):
    row = _row(x.shape)
    c = x * w[CONV_W - 1:CONV_W, :]
    for k in range(CONV_W - 1):
        sh = CONV_W - 1 - k
        c = c + w[k:k + 1, :] * jnp.where(row >= sh, pltpu.roll(x, sh, 0), 0.0)
    return c


def gdn_pre_fwd(proj, cw, seq):
    t = proj.shape[0]
    nb = t // seq
    scale = GDN_DH ** -0.5

    def body(xq_ref, xk_ref, xv_ref, wq_ref, wk_ref, wv_ref, q_ref, k_ref, v_ref):
        def act(x_ref, w_ref):
            c = _conv(x_ref[...], w_ref[...], seq)
            return c * jax.nn.sigmoid(c)

        aq, ak = act(xq_ref, wq_ref), act(xk_ref, wk_ref)
        q_ref[...] = aq * lax.rsqrt(jnp.sum(aq * aq, axis=1, keepdims=True) + EPS) * scale
        k_ref[...] = ak * lax.rsqrt(jnp.sum(ak * ak, axis=1, keepdims=True) + EPS)
        v_ref[...] = act(xv_ref, wv_ref)

    xb = lambda off: pl.BlockSpec((seq, 128), lambda b, h: (b, off + h))
    wb = lambda off: pl.BlockSpec((CONV_W, 128), lambda b, h: (0, off + h))
    ob = lambda: pl.BlockSpec((seq, 128), lambda b, h: (b, h))
    return pl.pallas_call(
        body, grid=(nb, GDN_HEADS),
        in_specs=[xb(12), xb(16), xb(20), wb(0), wb(4), wb(8)],
        out_specs=[ob(), ob(), ob()],
        out_shape=[jax.ShapeDtypeStruct((t, 512), F32)] * 3,
        compiler_params=_cp("parallel", "parallel"), name="gdn_pre_fwd")(proj, proj, proj, cw, cw, cw)


def gdn_pre_bwd(proj, cw, dq, dk, dv, seq):
    t = proj.shape[0]
    nb = t // seq
    scale = GDN_DH ** -0.5

    def body(xq_ref, xk_ref, xv_ref, wq_ref, wk_ref, wv_ref, dq_ref, dk_ref, dv_ref,
             dxq_ref, dxk_ref, dxv_ref, dwq_ref, dwk_ref, dwv_ref):
        first = pl.program_id(1) == 0
        row = _row((seq, 128))

        def one(x_ref, w_ref, dy_ref, dx_ref, dw_ref, norm, sc):
            x, w = x_ref[...], w_ref[...]
            c = _conv(x, w, seq)
            sg = jax.nn.sigmoid(c)
            dy = dy_ref[...]
            if norm:
                a = c * sg
                rs = lax.rsqrt(jnp.sum(a * a, axis=1, keepdims=True) + EPS)
                dy = dy * sc
                da = rs * dy - a * (rs * rs * rs) * jnp.sum(dy * a, axis=1, keepdims=True)
            else:
                da = dy
            dc = da * _silu_grad(c, sg)
            dx = dc * w[CONV_W - 1:CONV_W, :]
            dws = [None] * CONV_W
            dws[CONV_W - 1] = jnp.sum(dc * x, axis=0, keepdims=True)
            for k in range(CONV_W - 1):
                sh = CONV_W - 1 - k
                dc_up = jnp.where(row < seq - sh, pltpu.roll(dc, seq - sh, 0), 0.0)
                dx = dx + w[k:k + 1, :] * dc_up
                dws[k] = jnp.sum(dc_up * x, axis=0, keepdims=True)
            dx_ref[...] = dx.astype(BF16)
            dwn = jnp.concatenate(dws, axis=0)

            @pl.when(first)
            def _():
                dw_ref[...] = dwn

            @pl.when(jnp.logical_not(first))
            def _():
                dw_ref[...] += dwn

        one(xq_ref, wq_ref, dq_ref, dxq_ref, dwq_ref, True, scale)
        one(xk_ref, wk_ref, dk_ref, dxk_ref, dwk_ref, True, 1.0)
        one(xv_ref, wv_ref, dv_ref, dxv_ref, dwv_ref, False, 1.0)

    xb = lambda off: pl.BlockSpec((seq, 128), lambda h, b: (b, off + h))
    wb = lambda off: pl.BlockSpec((CONV_W, 128), lambda h, b: (0, off + h))
    ob = lambda: pl.BlockSpec((seq, 128), lambda h, b: (b, h))
    return pl.pallas_call(
        body, grid=(GDN_HEADS, nb),
        in_specs=[xb(12), xb(16), xb(20), wb(0), wb(4), wb(8), ob(), ob(), ob()],
        out_specs=[ob(), ob(), ob()] + [pl.BlockSpec((CONV_W, 128), lambda h, b: (0, h))] * 3,
        out_shape=[jax.ShapeDtypeStruct((t, 512), BF16)] * 3 + [jax.ShapeDtypeStruct((CONV_W, 512), F32)] * 3,
        compiler_params=_cp("parallel", "arbitrary"), name="gdn_pre_bwd")(proj, proj, proj, cw, cw, cw, dq, dk, dv)


def _b16(x):
    return x.astype(BF16)


@jax.custom_vjp
def _mm(a, b):
    return _dot(_b16(a), _b16(b))


_mm.defvjp(lambda a, b: (_mm(a, b), (a, b)),
           lambda res, g: (_dot_nt(_b16(g), _b16(res[1])), _dot_tn(_b16(res[0]), _b16(g))))


@jax.custom_vjp
def _mm_nt(a, b):
    return _dot_nt(_b16(a), _b16(b))


_mm_nt.defvjp(lambda a, b: (_mm_nt(a, b), (a, b)),
              lambda res, g: (_dot(_b16(g), _b16(res[1])), _dot_tn(_b16(g), _b16(res[0]))))


@jax.custom_vjp
def _mm_tn(a, b):
    return _dot_tn(_b16(a), _b16(b))


_mm_tn.defvjp(lambda a, b: (_mm_tn(a, b), (a, b)),
              lambda res, g: (_dot_nt(_b16(res[1]), _b16(g)), _dot(_b16(res[0]), _b16(g))))


def _dot32(a, b, dims=(((1,), (0,)), ((), ()))):
    def split(x):
        hi = x.astype(BF16)
        return hi, (x - hi.astype(F32)).astype(BF16)

    (ah, al), (bh, bl) = split(a), split(b)
    d = lambda x, y: lax.dot_general(x, y, dims, preferred_element_type=F32)
    return d(ah, bh) + (d(ah, bl) + d(al, bh))


INV_BLOCK = 4


def _inv_fwd_many(mats):
    n = mats[0].shape[0]
    r, c = _row((n, n)), _lane((n, n))
    eye = (r == c).astype(F32)

    def same_block(width):
        bits = jnp.int32(width.bit_length() - 1)
        return lax.shift_right_logical(r, bits) == lax.shift_right_logical(c, bits)

    diag = [jnp.where(same_block(INV_BLOCK), a, 0.0) for a in mats]
    invs, pws = [eye - d for d in diag], diag
    for _ in range(INV_BLOCK.bit_length() - 2):
        pws = [_dot32(pw, pw) for pw in pws]
        invs = [inv + _dot32(inv, pw) for inv, pw in zip(invs, pws)]
    width = INV_BLOCK
    while width < n:
        off = jnp.logical_and(same_block(2 * width), jnp.logical_not(same_block(width)))
        invs = [inv - _dot32(_dot32(inv, jnp.where(off, a, 0.0)), inv) for inv, a in zip(invs, mats)]
        width *= 2
    return invs


@jax.custom_vjp
def _inv_saved(a, inv):
    return inv


def _inv_saved_bwd(inv, g):
    tg = _dot32(inv, g, (((0,), (0,)), ((), ())))
    return -_dot32(tg, inv, (((1,), (1,)), ((), ()))), jnp.zeros_like(inv)


_inv_saved.defvjp(lambda a, inv: (inv, inv), _inv_saved_bwd)


def _gdn_decay(gcol):
    c = CHUNK
    ri, ci = _row((c, c)), _lane((c, c))
    incl, eye = ri >= ci, ri == ci
    grow = jnp.sum(jnp.where(eye, gcol, 0.0), axis=0, keepdims=True)
    gc = jnp.sum(jnp.where(incl, grow, 0.0), axis=1, keepdims=True)
    gcr = jnp.sum(jnp.where(eye, gc, 0.0), axis=0, keepdims=True)
    gl = jnp.sum(jnp.where(_row((c, 1)) == c - 1, gc, 0.0), axis=0, keepdims=True)
    return gc, gl, jnp.exp(jnp.where(incl, gc - gcr, NEG))


def _gdn_a(k, bcol, decay):
    c = CHUNK
    return jnp.where(_row((c, c)) > _lane((c, c)), _mm_nt(k * bcol, k) * decay, 0.0)


def _gdn_chunk(q, k, v, gcol, bcol, state, gg, nw, inv_saved):
    c = CHUNK
    incl = _row((c, c)) >= _lane((c, c))
    gc, gl, decay = _gdn_decay(gcol)
    kb, vb = k * bcol, v * bcol
    inv = _inv_saved(_gdn_a(k, bcol, decay), inv_saved)
    eg = jnp.exp(gc)
    u = _mm(inv, vb)
    w = _mm(inv, kb * eg)
    pm = jnp.where(incl, _mm_nt(q, k) * decay, 0.0)
    kd = k * jnp.exp(gl - gc)
    qd = q * eg
    v_new = u - _mm(w, state)
    o = _mm(qd, state) + _mm(pm, v_new)
    state_new = state * jnp.exp(gl) + _mm_tn(kd, v_new)
    y = o * _rstd(o) * nw * (gg * jax.nn.sigmoid(gg))
    return y, state_new


_gdn_chunks = jax.vmap(_gdn_chunk, in_axes=(0, 0, 0, 0, 0, 0, 0, None, 0))


def _gdn_chain_inputs(chains, p, r, c, q_ref, k_ref, v_ref, g_ref, gg_ref, inv_ref):
    cols = {nm: [] for nm in ("q", "k", "v", "g", "b", "gg", "inv")}
    for b, hh in chains:
        h = GDN_HPS * p + hh
        ln = slice(hh * 128, (hh + 1) * 128)
        gt = g_ref[b, r, :]
        cols["q"].append(q_ref[b, r, ln])
        cols["k"].append(k_ref[b, r, ln])
        cols["v"].append(v_ref[b, r, ln])
        cols["g"].append(_pick_lane(gt, 8 + h))
        cols["b"].append(_pick_lane(gt, 12 + h))
        cols["gg"].append(gg_ref[b, r, ln])
        cols["inv"].append(inv_ref[b, hh, c])
    return [jnp.stack(cols[nm]) for nm in ("q", "k", "v", "g", "b", "gg", "inv")]


GDN_CB = 8
GDN_INV_CB = 16
GDN_HPS = 4


def gdn_inv(k, gates, seq):
    t = k.shape[0]
    nb, nc = t // seq, seq // CHUNK
    cb = min(GDN_INV_CB, nc)
    rb = cb * CHUNK
    nsb = seq // rb

    def body(k_ref, g_ref, o_ref):
        h = pl.program_id(1)
        mats = []
        for c in range(cb):
            r = slice(c * CHUNK, (c + 1) * CHUNK)
            gt = g_ref[r, :]
            _, _, decay = _gdn_decay(_pick_lane(gt, 8 + h))
            mats.append(_gdn_a(k_ref[r, :], _pick_lane(gt, 12 + h), decay))
        for c, inv in enumerate(_inv_fwd_many(mats)):
            o_ref[c] = inv

    return pl.pallas_call(
        body, grid=(nb, GDN_HEADS, nsb),
        in_specs=[pl.BlockSpec((rb, 128), lambda b, h, s: (b * nsb + s, h)),
                  pl.BlockSpec((rb, 128), lambda b, h, s: (b * nsb + s, 0))],
        out_specs=pl.BlockSpec((None, None, cb, CHUNK, CHUNK), lambda b, h, s: (b, h, s, 0, 0)),
        out_shape=jax.ShapeDtypeStruct((nb, GDN_HEADS, nc, CHUNK, CHUNK), F32),
        compiler_params=_cp("parallel", "parallel", "parallel"), name="gdn_inv")(k, gates)


def _gdn_specs(nb, nsb, cb, rev):
    blk = (lambda s: nsb - 1 - s) if rev else (lambda s: s)
    rb = cb * CHUNK
    pair = lambda off=0: pl.BlockSpec((nb, rb, 128 * GDN_HPS), lambda s, p: (0, blk(s), off + p))
    gate = lambda: pl.BlockSpec((nb, rb, 128), lambda s, p: (0, blk(s), 0))
    mats = lambda n: pl.BlockSpec((nb, GDN_HPS, cb, n, n), lambda s, p: (0, p, blk(s), 0, 0))
    return pair, gate, mats


def gdn_fwd(q, k, v, gates, proj, nw, inv, seq):
    t = q.shape[0]
    nb, nc = t // seq, seq // CHUNK
    cb = GDN_CB
    nsb = nc // cb
    chains = [(b, hh) for b in range(nb) for hh in range(GDN_HPS)]
    nch = len(chains)
    pair, gate, mats = _gdn_specs(nb, nsb, cb, False)

    def body(q_ref, k_ref, v_ref, g_ref, gg_ref, inv_ref, nw_ref, y_ref, st_ref, carry):
        s, p = pl.program_id(0), pl.program_id(1)

        @pl.when(s == 0)
        def _():
            for ci in range(nch):
                carry[p * nch + ci] = jnp.zeros((GDN_DH, GDN_DH), F32)

        def step(c, states):
            r = _rows(c, CHUNK)
            for ci, (b, hh) in enumerate(chains):
                st_ref[b, hh, c] = states[ci]
            ins = _gdn_chain_inputs(chains, p, r, c, q_ref, k_ref, v_ref, g_ref, gg_ref, inv_ref)
            y, states = _gdn_chunks(*ins[:5], states, ins[5], nw_ref[...], ins[6])
            for ci, (b, hh) in enumerate(chains):
                y_ref[b, r, hh * 128:(hh + 1) * 128] = y[ci]
            return states

        states = lax.fori_loop(0, cb, step, jnp.stack([carry[p * nch + ci] for ci in range(nch)]))
        for ci in range(nch):
            carry[p * nch + ci] = states[ci]

    v3 = lambda a: a.reshape(nb, seq, a.shape[1])
    y, st = pl.pallas_call(
        body, grid=(nsb, GDN_HEADS // GDN_HPS),
        in_specs=[pair(), pair(), pair(), gate(), pair(24 // GDN_HPS), mats(CHUNK), pl.BlockSpec((1, 128), lambda s, p: (0, 0))],
        out_specs=[pair(), mats(GDN_DH)],
        out_shape=[jax.ShapeDtypeStruct((nb, seq, 512), F32),
                   jax.ShapeDtypeStruct((nb, GDN_HEADS, nc, GDN_DH, GDN_DH), F32)],
        scratch_shapes=[pltpu.VMEM((GDN_HEADS // GDN_HPS * nch, GDN_DH, GDN_DH), F32)],
        compiler_params=_cp("arbitrary", "arbitrary"), name="gdn_fwd")(v3(q), v3(k), v3(v), v3(gates), v3(proj), inv, nw)
    return y.reshape(t, 512), st


def gdn_bwd(q, k, v, gates, proj, nw, inv, states, dy, seq):
    t = q.shape[0]
    nb, nc = t // seq, seq // CHUNK
    cb = GDN_CB // 2
    nsb = nc // cb
    chains = [(b, hh) for b in range(nb) for hh in range(GDN_HPS)]
    nch = len(chains)
    pair, gate, mats = _gdn_specs(nb, nsb, cb, True)

    def body(q_ref, k_ref, v_ref, g_ref, gg_ref, inv_ref, st_ref, dy_ref, nw_ref,
             dq_ref, dk_ref, dv_ref, dgg_ref, dg_ref, dnw_ref, carry):
        s, p = pl.program_id(0), pl.program_id(1)

        @pl.when((s == 0) & (p == 0))
        def _():
            dnw_ref[...] = jnp.zeros_like(dnw_ref)

        @pl.when(p == 0)
        def _():
            dg_ref[...] = jnp.zeros_like(dg_ref)

        @pl.when(s == 0)
        def _():
            for ci in range(nch):
                carry[p * nch + ci] = jnp.zeros((GDN_DH, GDN_DH), F32)

        def step(idx, dstates):
            c = cb - 1 - idx
            r = _rows(c, CHUNK)
            ins = _gdn_chain_inputs(chains, p, r, c, q_ref, k_ref, v_ref, g_ref, gg_ref, inv_ref)
            st = jnp.stack([st_ref[b, hh, c] for b, hh in chains])
            dy = jnp.stack([dy_ref[b, r, hh * 128:(hh + 1) * 128] for b, hh in chains])
            _, vjp = jax.vjp(_gdn_chunks, *ins[:5], st, ins[5], nw_ref[...], ins[6])
            dq, dk, dv, dgc, dbc, dstates, dgg, dnw, _ = vjp((dy, dstates))
            dnw_ref[...] += dnw
            lane = _lane((CHUNK, 128))
            for ci, (b, hh) in enumerate(chains):
                h = GDN_HPS * p + hh
                ln = slice(hh * 128, (hh + 1) * 128)
                dq_ref[b, r, ln] = dq[ci]
                dk_ref[b, r, ln] = dk[ci]
                dv_ref[b, r, ln] = dv[ci]
                dgg_ref[b, r, ln] = dgg[ci].astype(BF16)
                dg_ref[b, r, :] = jnp.where(lane == 8 + h, dgc[ci], jnp.where(lane == 12 + h, dbc[ci], dg_ref[b, r, :]))
            return dstates

        dstates = lax.fori_loop(0, cb, step, jnp.stack([carry[p * nch + ci] for ci in range(nch)]))
        for ci in range(nch):
            carry[p * nch + ci] = dstates[ci]

    v3 = lambda a: a.reshape(nb, seq, a.shape[1])
    res = pl.pallas_call(
        body, grid=(nsb, GDN_HEADS // GDN_HPS),
        in_specs=[pair(), pair(), pair(), gate(), pair(24 // GDN_HPS), mats(CHUNK), mats(GDN_DH), pair(),
                  pl.BlockSpec((1, 128), lambda s, p: (0, 0))],
        out_specs=[pair(), pair(), pair(), pair(), gate(), pl.BlockSpec((1, 128), lambda s, p: (0, 0))],
        out_shape=[jax.ShapeDtypeStruct((nb, seq, 512), F32)] * 3 + [jax.ShapeDtypeStruct((nb, seq, 512), BF16)]
                  + [jax.ShapeDtypeStruct((nb, seq, 128), F32), jax.ShapeDtypeStruct((1, 128), F32)],
        scratch_shapes=[pltpu.VMEM((GDN_HEADS // GDN_HPS * nch, GDN_DH, GDN_DH), F32)],
        compiler_params=_cp("arbitrary", "arbitrary"),
        name="gdn_bwd")(v3(q), v3(k), v3(v), v3(gates), v3(proj), inv, states, v3(dy), nw)
    return [a.reshape(t, a.shape[2]) for a in res[:5]] + [res[5]]


def loss_head(y, target, tm=512):
    t, d = y.shape

    def body(y_ref, t_ref, s_ref, dy_ref):
        @pl.when(pl.program_id(0) == 0)
        def _():
            s_ref[...] = jnp.zeros_like(s_ref)

        err = y_ref[...] - t_ref[...]
        s_ref[...] += jnp.sum(err * err, axis=0, keepdims=True)
        dy_ref[...] = err * (1.0 / d)

    return pl.pallas_call(
        body, grid=(t // tm,),
        in_specs=[pl.BlockSpec((tm, d), lambda i: (i, 0)), pl.BlockSpec((tm, d), lambda i: (i, 0))],
        out_specs=[pl.BlockSpec((1, d), lambda i: (0, 0)), pl.BlockSpec((tm, d), lambda i: (i, 0))],
        out_shape=[jax.ShapeDtypeStruct((1, d), F32), jax.ShapeDtypeStruct((t, d), F32)],
        compiler_params=_cp("arbitrary"), name="loss_head")(y, target)


def _place():
    return lax.axis_index("x"), lax.axis_index("y"), lax.axis_index("c")


def _peer(k):
    x, y, c = _place()
    px = 1 - x if (k >> 2) & 1 else x
    py = 1 - y if (k >> 1) & 1 else y
    pc = 1 - c if k & 1 else c
    return (px, py, pc), 4 * px + 2 * py + pc


_ANY = pl.BlockSpec(memory_space=pl.ANY)
_SEM = pl.BlockSpec(memory_space=pltpu.SEMAPHORE)
_EFFECT = pltpu.SideEffectType.DATAFLOW_SIDE_EFFECTING


def _me():
    x, y, c = _place()
    return 4 * x + 2 * y + c


def _remote_copy(ins, lands, scatter, send_sems, recv_sems, a, k, arriving):
    pid, pidx = _peer(k)
    return pltpu.make_async_remote_copy(src_ref=ins[a].at[pidx] if scatter[a] else ins[a],
                                        dst_ref=lands[a].at[pidx if arriving else _me()],
                                        send_sem=send_sems.at[a * N_DEV + k], recv_sem=recv_sems.at[a * N_DEV + k],
                                        device_id=pid, device_id_type=MESH)


def _local_copy(ins, lands, scatter, loc_sems, a):
    me = _me()
    return pltpu.make_async_copy(ins[a].at[me] if scatter[a] else ins[a], lands[a].at[me], loc_sems.at[a])


def exchange_start(arrays, scatter, name, after):
    n = len(arrays)
    lands = [lax.empty(a.shape if s else (N_DEV,) + a.shape, a.dtype) for a, s in zip(arrays, scatter)]

    def body(*refs):
        ins, lds = refs[:n], refs[n:2 * n]
        send_sems, recv_sems, loc_sems = refs[2 * n + 1:2 * n + 4]
        token = refs[-1]
        for k in range(1, N_DEV):
            for a in range(n):
                _remote_copy(ins, lds, scatter, send_sems, recv_sems, a, k, False).start()
        for a in range(n):
            _local_copy(ins, lds, scatter, loc_sems, a).start()
        token[...] = jnp.zeros_like(token)

    hbm = lambda a: pltpu.HBM(a.shape, a.dtype)
    res = pl.pallas_call(
        body, name=name,
        in_specs=[_ANY] * (2 * n + 1),
        out_specs=[_SEM, _SEM, _SEM] + [_ANY] * (2 * n) + [pl.BlockSpec(memory_space=pltpu.VMEM)],
        out_shape=[pltpu.SemaphoreType.DMA((n * N_DEV,)), pltpu.SemaphoreType.DMA((n * N_DEV,)),
                   pltpu.SemaphoreType.DMA((n,))]
                  + [hbm(a) for a in arrays] + [hbm(a) for a in lands] + [jax.ShapeDtypeStruct((8, 128), F32)],
        input_output_aliases={i: 3 + i for i in range(2 * n)},
        compiler_params=pltpu.CompilerParams(has_side_effects=_EFFECT),
    )(*[pltpu.with_memory_space_constraint(a, pltpu.HBM) for a in list(arrays) + lands], after)
    return res[0:3], res[3:3 + n], res[3 + n:3 + 2 * n], res[-1]


def exchange_wait(sems, arrays, lands, scatter, after, name):
    n = len(arrays)

    def body(*refs):
        ins, lds = refs[:n], refs[n:2 * n]
        ssem, rsem, lsem = refs[2 * n:2 * n + 3]
        for a in range(n):
            _local_copy(ins, lds, scatter, lsem, a).wait()
        for k in range(1, N_DEV):
            for a in range(n):
                _remote_copy(ins, lds, scatter, ssem, rsem, a, k, True).wait_recv()
        for k in range(1, N_DEV):
            for a in range(n):
                _remote_copy(ins, lds, scatter, ssem, rsem, a, k, False).wait_send()

    hbm = lambda a: pltpu.HBM(a.shape, a.dtype)
    res = pl.pallas_call(
        body, name=name,
        in_specs=[_ANY] * (2 * n) + [_SEM, _SEM, _SEM, _ANY],
        out_specs=[_ANY] * (2 * n),
        out_shape=[hbm(a) for a in arrays] + [hbm(a) for a in lands],
        input_output_aliases={i: i for i in range(2 * n)},
        compiler_params=pltpu.CompilerParams(has_side_effects=_EFFECT),
    )(*arrays, *lands, *sems, after)
    return list(res[n:])


def gather_two_level(arrays, name):
    n = len(arrays)

    def body(*refs):
        ins, outs = refs[:n], refs[n:2 * n]
        send_sems, recv_sems, loc_sems = refs[2 * n:]
        x, y, c = _place()
        sibling = (x, y, 1 - c)
        chips = [(1 - x, y), (x, 1 - y), (1 - x, 1 - y)]

        def slot(pos):
            return 4 * pos[0] + 2 * pos[1] + pos[2]

        def copy(a, k, block, to, from_input=False):
            return pltpu.make_async_remote_copy(
                src_ref=ins[a] if from_input else outs[a].at[slot(block)], dst_ref=outs[a].at[slot(block)],
                send_sem=send_sems.at[a, k], recv_sem=recv_sems.at[a, k], device_id=to, device_id_type=MESH)

        me = (x, y, c)
        mine = [pltpu.make_async_copy(ins[a], outs[a].at[slot(me)], loc_sems.at[a]) for a in range(n)]
        for cp in mine:
            cp.start()
        first = [copy(a, 0, me, sibling, True) for a in range(n)]
        first += [copy(a, 1 + j, me, (*chip, c), True) for j, chip in enumerate(chips) for a in range(n)]
        for cp in first:
            cp.start()
        passed = []
        for j, chip in enumerate(chips):
            for a in range(n):
                copy(a, 1 + j, (*chip, c), me).wait_recv()
                passed.append(copy(a, 4 + j, (*chip, c), sibling))
                passed[-1].start()
        for a in range(n):
            copy(a, 0, sibling, me).wait_recv()
            for j, chip in enumerate(chips):
                copy(a, 4 + j, (*chip, 1 - c), me).wait_recv()
        for cp in first + passed:
            cp.wait_send()
        for cp in mine:
            cp.wait()

    return pl.pallas_call(
        body, in_specs=[_ANY] * n, out_specs=[_ANY] * n,
        out_shape=[jax.ShapeDtypeStruct((N_DEV,) + a.shape, a.dtype) for a in arrays],
        scratch_shapes=[pltpu.SemaphoreType.DMA((n, 7)), pltpu.SemaphoreType.DMA((n, 7)), pltpu.SemaphoreType.DMA((n,))],
        name=name)(*arrays)


def exchange_begin(arrays, scatter, name, after):
    sems, arrays_thru, lands_thru, token = exchange_start(arrays, scatter, name + "_start", after)
    return (sems, arrays_thru, lands_thru, scatter, name), token


def exchange_end(state, after):
    sems, arrays_thru, lands_thru, scatter, name = state
    return exchange_wait(sems, arrays_thru, lands_thru, scatter, after, name + "_wait")


def adamw_reduce(slots, w, m, v, l, name, after=None, prev=None):
    nl, r, c = w.shape
    tr = r
    while tr * c * 4 > (1 << 20) and tr % 16 == 0:
        tr //= 2
    bc1 = 1.0 - ADAM_B1 ** ADAM_STEP
    bc2 = 1.0 - ADAM_B2 ** ADAM_STEP

    def body(s_ref, w_ref, m_ref, v_ref, *rest):
        g_ref, d_ref, nm_ref, nv_ref = rest[-4:]
        g = s_ref[0].astype(F32)
        for j in range(1, N_DEV):
            g = g + s_ref[j].astype(F32)
        nm = ADAM_B1 * m_ref[...] + (1.0 - ADAM_B1) * g
        nv = ADAM_B2 * v_ref[...] + (1.0 - ADAM_B2) * (g * g)
        g_ref[...] = g
        nm_ref[...] = nm
        nv_ref[...] = nv
        d_ref[...] = -ADAM_LR * ((nm / bc1) / (jnp.sqrt(nv / bc2) + ADAM_EPS) + ADAM_WD * w_ref[...])

    blk = lambda: pl.BlockSpec((None, tr, c), lambda i: (l, i, 0))
    extra = ([] if after is None else [after]) + ([] if prev is None else list(prev))
    first_prev = 4 + (after is not None)
    return pl.pallas_call(
        body, grid=(r // tr,),
        in_specs=[pl.BlockSpec((N_DEV, tr, c), lambda i: (0, i, 0)), blk(), blk(), blk()] + [_ANY] * len(extra),
        out_specs=[blk(), blk(), blk(), blk()],
        out_shape=[jax.ShapeDtypeStruct((nl, r, c), F32)] * 4,
        input_output_aliases={} if prev is None else {first_prev + j: j for j in range(4)},
        compiler_params=_cp("parallel"), name=name)(slots, w, m, v, *extra)


BIG = ("ffn1_w_in", "ffn1_w_out", "w_in", "gdn_conv", "w_out", "ffn2_w_in", "ffn2_w_out")
GROUPS = (BIG[0:2], BIG[2:5], BIG[5:7])
SMALL = ("ffn1_norm", "mix_norm", "fox_q_norm", "fox_k_norm", "fox_f_bias", "gdn_a_log", "gdn_dt_bias",
         "gdn_out_norm", "ffn2_norm")
WEIGHTS = ("ffn1_norm", "ffn1_w_in", "ffn1_w_out", "mix_norm", "w_in", "fox_q_norm", "fox_k_norm", "fox_f_bias",
           "gdn_conv", "gdn_a_log", "gdn_dt_bias", "gdn_out_norm", "w_out", "ffn2_norm", "ffn2_w_in", "ffn2_w_out")
IN_COLS = (("fq", 512), ("fk", 512), ("fv", 512), ("ff", 8), ("gq", 512), ("gk", 512), ("gv", 512),
           ("ga", 4), ("gb", 4), ("gg", 512))
MY_BIG = ("fq", "fk", "fv", "gq", "gk", "gv", "gg")
MY_SMALL = ("ff", "ga", "gb")
SMALL_ROWS = 8 * 128


def _in_cols_to_mine(w):
    off, parts = 0, {}
    for nm, wd in IN_COLS:
        parts[nm] = w[:, off:off + wd]
        off += wd
    small = jnp.concatenate([parts[nm] for nm in MY_SMALL], axis=1)
    small = jnp.pad(small, ((0, 0), (0, 128 - small.shape[1])))
    return jnp.concatenate([parts[nm] for nm in MY_BIG] + [small], axis=1)


def _in_cols_from_mine(g):
    parts = {nm: g[:, i * 512:(i + 1) * 512] for i, nm in enumerate(MY_BIG)}
    off = N_BIG
    for nm in MY_SMALL:
        wd = dict(IN_COLS)[nm]
        parts[nm] = g[:, off:off + wd]
        off += wd
    return jnp.concatenate([parts[nm] for nm, _ in IN_COLS], axis=1)


def _pack_small(vals):
    rows = []
    nl = vals[SMALL[0]].shape[0]
    for l in range(nl):
        for nm in SMALL:
            v = vals[nm][l].reshape(-1)
            pad = (-v.shape[0]) % SMALL_ROWS
            rows.append(jnp.pad(v, (0, pad)).reshape(-1, 128))
    return jnp.concatenate(rows, axis=0)


def _unpack_small(packed, like):
    out = {nm: [] for nm in SMALL}
    row = 0
    nl = like[SMALL[0]].shape[0]
    for l in range(nl):
        for nm in SMALL:
            n = like[nm].shape[1]
            nr = -(-n // SMALL_ROWS) * 8
            out[nm].append(packed[row:row + nr].reshape(-1)[:n])
            row += nr
    return {nm: jnp.stack(v) for nm, v in out.items()}


def kernel(x, ffn1_norm, ffn1_w_in, ffn1_w_out, mix_norm, w_in, fox_q_norm, fox_k_norm, fox_f_bias, gdn_conv, gdn_a_log, gdn_dt_bias, gdn_out_norm, w_out, ffn2_norm, ffn2_w_in, ffn2_w_out, loss_target, m_ffn1_norm, m_ffn1_w_in, m_ffn1_w_out, m_mix_norm, m_w_in, m_fox_q_norm, m_fox_k_norm, m_fox_f_bias, m_gdn_conv, m_gdn_a_log, m_gdn_dt_bias, m_gdn_out_norm, m_w_out, m_ffn2_norm, m_ffn2_w_in, m_ffn2_w_out, v_ffn1_norm, v_ffn1_w_in, v_ffn1_w_out, v_mix_norm, v_w_in, v_fox_q_norm, v_fox_k_norm, v_fox_f_bias, v_gdn_conv, v_gdn_a_log, v_gdn_dt_bias, v_gdn_out_norm, v_w_out, v_ffn2_norm, v_ffn2_w_in, v_ffn2_w_out):
    wts = dict(ffn1_norm=ffn1_norm, ffn1_w_in=ffn1_w_in, ffn1_w_out=ffn1_w_out, mix_norm=mix_norm, w_in=w_in,
               fox_q_norm=fox_q_norm, fox_k_norm=fox_k_norm, fox_f_bias=fox_f_bias, gdn_conv=gdn_conv,
               gdn_a_log=gdn_a_log, gdn_dt_bias=gdn_dt_bias, gdn_out_norm=gdn_out_norm, w_out=w_out,
               ffn2_norm=ffn2_norm, ffn2_w_in=ffn2_w_in, ffn2_w_out=ffn2_w_out)
    mom = dict(ffn1_norm=m_ffn1_norm, ffn1_w_in=m_ffn1_w_in, ffn1_w_out=m_ffn1_w_out, mix_norm=m_mix_norm, w_in=m_w_in,
               fox_q_norm=m_fox_q_norm, fox_k_norm=m_fox_k_norm, fox_f_bias=m_fox_f_bias, gdn_conv=m_gdn_conv,
               gdn_a_log=m_gdn_a_log, gdn_dt_bias=m_gdn_dt_bias, gdn_out_norm=m_gdn_out_norm, w_out=m_w_out,
               ffn2_norm=m_ffn2_norm, ffn2_w_in=m_ffn2_w_in, ffn2_w_out=m_ffn2_w_out)
    var = dict(ffn1_norm=v_ffn1_norm, ffn1_w_in=v_ffn1_w_in, ffn1_w_out=v_ffn1_w_out, mix_norm=v_mix_norm, w_in=v_w_in,
               fox_q_norm=v_fox_q_norm, fox_k_norm=v_fox_k_norm, fox_f_bias=v_fox_f_bias, gdn_conv=v_gdn_conv,
               gdn_a_log=v_gdn_a_log, gdn_dt_bias=v_gdn_dt_bias, gdn_out_norm=v_gdn_out_norm, w_out=v_w_out,
               ffn2_norm=v_ffn2_norm, ffn2_w_in=v_ffn2_w_in, ffn2_w_out=v_ffn2_w_out)
    nb, seq, d = x.shape
    t = nb * seq
    depth = ffn1_norm.shape[0]

    stages = [(l, gi) for l in range(depth) for gi in range(len(GROUPS))]

    def shards_of(l, gi):
        return [wts[nm][l] if nm == "gdn_conv" else wts[nm][l].astype(BF16) for nm in GROUPS[gi]]

    def behind(nw, token):
        return nw if token is None else nw + token[0:1, 0:1]

    def small_params(l):
        return dict(
            n1=ffn1_norm[l][None], nmix=mix_norm[l][None], n2=ffn2_norm[l][None],
            qw=fox_q_norm[l][None], kw=fox_k_norm[l][None], onw=gdn_out_norm[l][None],
            gp=jnp.concatenate([
                jnp.concatenate([fox_f_bias[l], gdn_dt_bias[l], jnp.zeros((116,), F32)])[None],
                jnp.concatenate([jnp.zeros((8,), F32), gdn_a_log[l], jnp.zeros((116,), F32)])[None],
                jnp.zeros((6, 128), F32)], axis=0))

    h = x.reshape(t, d)
    landed = gather_two_level(shards_of(0, 0), "gather_0")
    saved = [dict(p=small_params(l)) for l in range(depth)]
    for k, (l, gi) in enumerate(stages):
        s, w, token = saved[l], landed, None
        p = s["p"]
        if k + 1 < len(stages):
            nl, ng = stages[k + 1]
            state, token = exchange_begin(shards_of(nl, ng), [False] * len(GROUPS[ng]), f"gather_{k + 1}", landed[0])
        if gi == 0:
            fb = w[0].shape[2]
            p["w1i"], p["w1o"] = w[0].reshape(2, 4, d, fb), w[1].reshape(4, fb, d)
            s["x0"] = h
            h, *s["ffn1"] = ffn_fwd(h, behind(p["n1"], token), p["w1i"], p["w1o"])
            s["x1"] = h
        elif gi == 1:
            p["wi"] = _in_cols_to_mine(w[0].transpose(1, 0, 2).reshape(d, -1))
            p["cw"] = w[1].transpose(1, 0, 2).reshape(CONV_W, -1)
            p["wo"] = w[2].reshape(d, d)
            proj, hn = inproj_fwd(h, behind(p["nmix"], token), p["wi"])
            gates = gates_fwd(proj, p["gp"], seq)
            yf, lse = attn_fwd(proj, gates, p["qw"], p["kw"], seq, tq=min(seq, ATTN_TQ_FWD))
            qh, kh, vh = gdn_pre_fwd(proj, p["cw"], seq)
            inv = gdn_inv(kh, gates, seq)
            yg, st = gdn_fwd(qh, kh, vh, gates, proj, p["onw"], inv, seq)
            h, ycat = outproj_fwd(h, yf, yg, p["wo"])
            s.update(x2=h, proj=proj, hn=hn, gates=gates, yf=yf, lse=lse, qh=qh, kh=kh, vh=vh, st=st, inv=inv, ycat=ycat)
        else:
            fb = w[0].shape[2]
            p["w2i"], p["w2o"] = w[0].reshape(2, 4, d, fb), w[1].reshape(4, fb, d)
            h, *s["ffn2"] = ffn_fwd(h, behind(p["n2"], token), p["w2i"], p["w2o"])
        if k + 1 < len(stages):
            landed = exchange_end(state, h)

    sq, dh = loss_head(h, loss_target.reshape(t, d))
    loss = lax.psum(0.5 * jnp.sum(sq) / d, ("x", "y", "c"))

    got = [None] * len(stages)
    pending, token = None, None
    gsmall = {nm: [None] * depth for nm in SMALL}
    for k in reversed(range(len(stages))):
        l, gi = stages[k]
        s = saved[l]
        p = s["p"]
        if gi != 1:
            nw, xin, wi_, wo_, nm_n, (xn, gu, hh) = (
                (p["n1"], s["x0"], p["w1i"], p["w1o"], "ffn1_norm", s["ffn1"]) if gi == 0 else
                (p["n2"], s["x2"], p["w2i"], p["w2o"], "ffn2_norm", s["ffn2"]))
            dh, dn, dgu, dyh = ffn_bwd(xin, dh, behind(nw, token), gu, wi_, wo_)
            g_in, g_out = wgrad_ffn_in(xn, dgu), wgrad_ffn_out(hh, dyh)
            send = [g_in.reshape(N_DEV, d, g_in.shape[3]), g_out.reshape(N_DEV, -1, d)]
            gsmall[nm_n][l] = dn[0]
        else:
            dyf, dyg, dyb = outproj_bwd(dh, p["wo"], token)
            g_wo = wgrad_2d(s["ycat"], dyb, 512, "wgrad_w_out")
            dq, dk, dv, dga, dqw, dkw = attn_bwd(s["proj"], s["gates"], p["qw"], p["kw"], s["yf"], s["lse"], dyf, seq,
                                                 tq=min(seq, ATTN_TQ_BWD))
            dqh, dkh, dvh, dgg, dgb, donw = gdn_bwd(s["qh"], s["kh"], s["vh"], s["gates"], s["proj"], p["onw"],
                                                     s["inv"], s["st"], dyg, seq)
            dxq, dxk, dxv, dwq, dwk, dwv = gdn_pre_bwd(s["proj"], p["cw"], dqh, dkh, dvh, seq)
            dsm, dgp = gates_bwd(s["proj"], p["gp"], dga, dgb, seq)
            dh, dnmix, dproj = inproj_bwd(s["x1"], dh, p["nmix"], p["wi"], [dq, dk, dv, dxq, dxk, dxv, dgg, dsm])
            g_wi = wgrad_2d(s["hn"], dproj, 512, "wgrad_w_in", F32)
            g_cw = jnp.concatenate([dwq, dwk, dwv], axis=1)
            send = [_in_cols_from_mine(g_wi).reshape(d, N_DEV, -1).transpose(1, 0, 2),
                    g_cw.reshape(CONV_W, N_DEV, -1).transpose(1, 0, 2), g_wo.reshape(N_DEV, -1, d)]
            for nm, val in (("mix_norm", dnmix[0]), ("fox_q_norm", dqw[0]), ("fox_k_norm", dkw[0]),
                            ("fox_f_bias", dgp[0, 0:8]), ("gdn_a_log", dgp[1, 8:12]), ("gdn_dt_bias", dgp[0, 8:12]),
                            ("gdn_out_norm", donw[0])):
                gsmall[nm][l] = val
        flags = [True] * len(send)
        if k == 0:
            send.append(_pack_small({nm: jnp.stack(v) for nm, v in gsmall.items()}))
            flags.append(False)
        prev = dh
        if pending is not None:
            got[pending[1]] = exchange_end(pending[0], dh)
            prev = got[pending[1]][0]
        state, token = exchange_begin(send, flags, f"exchange_grads_{k}", prev)
        pending = (state, k)
    grad_x = dh.reshape(nb, seq, d)

    res = {}

    def update_stage(k, slots, after):
        l, gi = stages[k]
        for i, nm in enumerate(GROUPS[gi]):
            r, c = wts[nm].shape[1:]
            res[nm] = adamw_reduce(slots[i].reshape(N_DEV, r, c), wts[nm], mom[nm], var[nm], l, f"adamw_{nm}_{l}",
                                   after, res.get(nm))
            if after is not None:
                after = res[nm][0]
        return after

    last = token
    for k in range(1, len(stages)):
        last = update_stage(k, got[k], last)
    got[0] = exchange_end(pending[0], last)
    update_stage(0, got[0], None)
    small_like = {nm: wts[nm] for nm in SMALL}
    sm = adamw_reduce(got[0][-1], _pack_small(small_like)[None], _pack_small({nm: mom[nm] for nm in SMALL})[None],
                      _pack_small({nm: var[nm] for nm in SMALL})[None], 0, "adamw_small")
    sm = [_unpack_small(a[0], small_like) for a in sm]
    for nm in SMALL:
        res[nm] = [sm[j][nm] for j in range(4)]
    return (loss, grad_x, *[res[nm][0] for nm in WEIGHTS], *[res[nm][1] for nm in WEIGHTS],
            *[res[nm][2] for nm in WEIGHTS], *[res[nm][3] for nm in WEIGHTS])
```

```python
import jax
import jax.numpy as jnp
from jax import lax
from jax.experimental import pallas as pl
from jax.experimental.pallas import tpu as pltpu

F32 = jnp.float32
BF16 = jnp.bfloat16
EPS = 1e-6
N_DEV = 8
MESH = pl.DeviceIdType.MESH
HIGHEST = lax.Precision.HIGHEST
VMEM_LIMIT = 56 * 1024 * 1024

FOX_HEADS, FOX_DH = 8, 64
GDN_HEADS, GDN_DH = 4, 128
CHUNK = 64
CONV_W = 4

ADAM_LR, ADAM_B1, ADAM_B2, ADAM_EPS, ADAM_WD, ADAM_STEP = 0.001, 0.9, 0.999, 1e-08, 0.01, 10


def _cp(*sem):
    return pltpu.CompilerParams(dimension_semantics=sem, vmem_limit_bytes=VMEM_LIMIT)


def _dot(a, b):
    return jnp.dot(a, b, preferred_element_type=F32)


def _dot_nt(a, b):
    return lax.dot_general(a, b, (((1,), (1,)), ((), ())), preferred_element_type=F32)


def _dot_tn(a, b):
    return lax.dot_general(a, b, (((0,), (0,)), ((), ())), preferred_element_type=F32)


def _rstd(xf):
    return lax.rsqrt(jnp.mean(xf * xf, axis=-1, keepdims=True) + EPS)


def _rms_bwd(xf, r, dyn):
    return r * dyn - xf * (r * r * r) * jnp.mean(dyn * xf, axis=-1, keepdims=True)


def ffn_fwd(x, nw, w_in, w_out, tm=1024, rc=1024):
    t, d = x.shape
    nj, fb = w_out.shape[0], w_out.shape[1]
    tm = min(tm, t)
    rc = min(rc, tm)

    def body(x_ref, nw_ref, wi_ref, wo_ref, o_ref, xn_ref, gu_ref, h_ref, acc_ref):
        j = pl.program_id(1)

        @pl.when(j == 0)
        def _():
            xf = x_ref[...]
            xn_ref[...] = (xf * _rstd(xf) * nw_ref[...]).astype(BF16)
            acc_ref[...] = jnp.zeros_like(acc_ref)

        rows = [slice(c * rc, (c + 1) * rc) for c in range(tm // rc)]
        gs = [_dot(xn_ref[r, :], wi_ref[0]) for r in rows]
        us = [_dot(xn_ref[r, :], wi_ref[1]) for r in rows]
        hs = []
        for g, u, r in zip(gs, us, rows):
            sg = jax.nn.sigmoid(g)
            silu = g * sg
            h = (silu * u).astype(BF16)
            gu_ref[0, r, :] = (u * (sg * (1.0 + g * (1.0 - sg)))).astype(BF16)
            gu_ref[1, r, :] = silu.astype(BF16)
            h_ref[r, :] = h
            hs.append(h)
        for h, r in zip(hs, rows):
            acc_ref[r, :] += _dot(h, wo_ref[...])

        @pl.when(j == nj - 1)
        def _():
            o_ref[...] = x_ref[...] + 0.5 * acc_ref[...]

    return pl.pallas_call(
        body, grid=(t // tm, nj),
        in_specs=[pl.BlockSpec((tm, d), lambda i, j: (i, 0)),
                  pl.BlockSpec((1, d), lambda i, j: (0, 0)),
                  pl.BlockSpec((2, None, d, fb), lambda i, j: (0, j, 0, 0)),
                  pl.BlockSpec((None, fb, d), lambda i, j: (j, 0, 0))],
        out_specs=[pl.BlockSpec((tm, d), lambda i, j: (i, 0)),
                   pl.BlockSpec((tm, d), lambda i, j: (i, 0)),
                   pl.BlockSpec((2, None, tm, fb), lambda i, j: (0, j, i, 0)),
                   pl.BlockSpec((None, tm, fb), lambda i, j: (j, i, 0))],
        out_shape=[jax.ShapeDtypeStruct((t, d), F32), jax.ShapeDtypeStruct((t, d), BF16),
                   jax.ShapeDtypeStruct((2, nj, t, fb), BF16), jax.ShapeDtypeStruct((nj, t, fb), BF16)],
        scratch_shapes=[pltpu.VMEM((tm, d), F32)],
        compiler_params=_cp("parallel", "arbitrary"), name="ffn_fwd")(x, nw, w_in, w_out)


def ffn_bwd(x, dy, nw, gu, w_in, w_out, tm=512, rc=256):
    t, d = x.shape
    nj, fb = w_out.shape[0], w_out.shape[1]
    tm = min(tm, t)
    rc = min(rc, tm)

    def body(x_ref, dy_ref, nw_ref, gu_ref, wi_ref, wo_ref,
             dx_ref, dnw_ref, dgu_ref, dyh_ref, acc_ref):
        i, j = pl.program_id(0), pl.program_id(1)

        @pl.when(j == 0)
        def _():
            dyh_ref[...] = (0.5 * dy_ref[...]).astype(BF16)
            acc_ref[...] = jnp.zeros_like(acc_ref)

        @pl.when((i == 0) & (j == 0))
        def _():
            dnw_ref[...] = jnp.zeros_like(dnw_ref)

        rows = [slice(c * rc, (c + 1) * rc) for c in range(tm // rc)]
        dhs = [_dot_nt(dyh_ref[r, :], wo_ref[...]) for r in rows]
        dgs = [(dh * gu_ref[0, r, :].astype(F32)).astype(BF16) for dh, r in zip(dhs, rows)]
        dus = [(dh * gu_ref[1, r, :].astype(F32)).astype(BF16) for dh, r in zip(dhs, rows)]
        for dg, du, r in zip(dgs, dus, rows):
            dgu_ref[0, r, :] = dg
            dgu_ref[1, r, :] = du
        for dg, du, r in zip(dgs, dus, rows):
            acc_ref[r, :] += _dot_nt(dg, wi_ref[0]) + _dot_nt(du, wi_ref[1])

        @pl.when(j == nj - 1)
        def _():
            xf = x_ref[...]
            r = _rstd(xf)
            dxn = acc_ref[...]
            dnw_ref[...] += jnp.sum(dxn * xf * r, axis=0, keepdims=True)
            dx_ref[...] = _rms_bwd(xf, r, dxn * nw_ref[...]) + dy_ref[...]

    return pl.pallas_call(
        body, grid=(t // tm, nj),
        in_specs=[pl.BlockSpec((tm, d), lambda i, j: (i, 0)),
                  pl.BlockSpec((tm, d), lambda i, j: (i, 0)),
                  pl.BlockSpec((1, d), lambda i, j: (0, 0)),
                  pl.BlockSpec((2, None, tm, fb), lambda i, j: (0, j, i, 0)),
                  pl.BlockSpec((2, None, d, fb), lambda i, j: (0, j, 0, 0)),
                  pl.BlockSpec((None, fb, d), lambda i, j: (j, 0, 0))],
        out_specs=[pl.BlockSpec((tm, d), lambda i, j: (i, 0)),
                   pl.BlockSpec((1, d), lambda i, j: (0, 0)),
                   pl.BlockSpec((2, None, tm, fb), lambda i, j: (0, j, i, 0)),
                   pl.BlockSpec((tm, d), lambda i, j: (i, 0))],
        out_shape=[jax.ShapeDtypeStruct((t, d), F32),
                   jax.ShapeDtypeStruct((1, d), F32),
                   jax.ShapeDtypeStruct((2, nj, t, fb), BF16),
                   jax.ShapeDtypeStruct((t, d), BF16)],
        scratch_shapes=[pltpu.VMEM((tm, d), F32)],
        compiler_params=_cp("arbitrary", "arbitrary"), name="ffn_bwd")(x, dy, nw, gu, w_in, w_out)


def _wgrad_call(a, b, a_spec, b_spec, out_shape, out_spec, grid, name, out_dtype=BF16):
    last = len(grid) - 1
    acc_shape = tuple(s for s in out_spec.block_shape if s is not None)

    def body(a_ref, b_ref, o_ref, acc_ref):
        @pl.when(pl.program_id(last) == 0)
        def _():
            acc_ref[...] = jnp.zeros_like(acc_ref)

        if len(acc_shape) == 3:
            shared_a = a_ref[...] if len(a_ref.shape) == 2 else None
            shared_b = b_ref[...] if len(b_ref.shape) == 2 else None
            for s in range(acc_shape[0]):
                acc_ref[s] += _dot_tn(a_ref[s] if shared_a is None else shared_a,
                                      b_ref[s] if shared_b is None else shared_b)
        else:
            acc_ref[...] += _dot_tn(a_ref[...], b_ref[...])

        @pl.when(pl.program_id(last) == grid[last] - 1)
        def _():
            o_ref[...] = acc_ref[...].astype(o_ref.dtype)

    sem = ("parallel",) * last + ("arbitrary",)
    return pl.pallas_call(body, grid=grid, in_specs=[a_spec, b_spec], out_specs=out_spec,
                          out_shape=jax.ShapeDtypeStruct(out_shape, out_dtype),
                          scratch_shapes=[pltpu.VMEM(acc_shape, F32)],
                          compiler_params=_cp(*sem), name=name)(a, b)


WGRAD_TM = 2048
WGRAD_TM_MIXER = 1024


def wgrad_ffn_in(xn, dgu, tm=WGRAD_TM):
    t, d = xn.shape
    _, nj, _, fb = dgu.shape
    tm = min(tm, t)
    return _wgrad_call(xn, dgu,
                       pl.BlockSpec((tm, d), lambda j, k: (k, 0)),
                       pl.BlockSpec((2, None, tm, fb), lambda j, k: (0, j, k, 0)),
                       (2, nj, d, fb), pl.BlockSpec((2, None, d, fb), lambda j, k: (0, j, 0, 0)),
                       (nj, t // tm), "wgrad_ffn_in")


def wgrad_ffn_out(h, dyh, tm=WGRAD_TM):
    nj, t, fb = h.shape
    d = dyh.shape[1]
    tm = min(tm, t)
    return _wgrad_call(h, dyh,
                       pl.BlockSpec((nj, tm, fb), lambda k: (0, k, 0)),
                       pl.BlockSpec((tm, d), lambda k: (k, 0)),
                       (nj, fb, d), pl.BlockSpec((nj, fb, d), lambda k: (0, 0, 0)),
                       (t // tm,), "wgrad_ffn_out")


def wgrad_2d(a, b, tk, name, out_dtype=BF16, tm=WGRAD_TM_MIXER):
    t, k = a.shape
    n = b.shape[1]
    tm = min(tm, t)
    return _wgrad_call(a, b,
                       pl.BlockSpec((tm, tk), lambda c, s: (s, c)),
                       pl.BlockSpec((tm, n), lambda c, s: (s, 0)),
                       (k, n), pl.BlockSpec((tk, n), lambda c, s: (c, 0)),
                       (k // tk, t // tm), name, out_dtype)


N_BIG = 7 * 512
N_PROJ = N_BIG + 128
COL_SMALL = N_BIG // 128


def inproj_fwd(x, nw, w, tm=512):
    t, d = x.shape
    n = w.shape[1]

    def body(x_ref, nw_ref, w_ref, p_ref, hn_ref):
        xf = x_ref[...]
        hn = (xf * _rstd(xf) * nw_ref[...]).astype(BF16)
        hn_ref[...] = hn
        p_ref[...] = _dot(hn, w_ref[...])

    return pl.pallas_call(
        body, grid=(t // tm,),
        in_specs=[pl.BlockSpec((tm, d), lambda i: (i, 0)), pl.BlockSpec((1, d), lambda i: (0, 0)),
                  pl.BlockSpec((d, n), lambda i: (0, 0))],
        out_specs=[pl.BlockSpec((tm, n), lambda i: (i, 0)), pl.BlockSpec((tm, d), lambda i: (i, 0))],
        out_shape=[jax.ShapeDtypeStruct((t, n), F32), jax.ShapeDtypeStruct((t, d), BF16)],
        compiler_params=_cp("parallel"), name="inproj_fwd")(x, nw, w)


def inproj_bwd(x, dres, nw, w, dparts, tm=512):
    t, d = x.shape
    n = w.shape[1]
    widths = [p.shape[1] for p in dparts]
    assert sum(widths) == n

    def body(x_ref, dres_ref, nw_ref, w_ref, *rest):
        part_refs, (dx_ref, dnw_ref, dp_ref) = rest[:len(widths)], rest[len(widths):]

        @pl.when(pl.program_id(0) == 0)
        def _():
            dnw_ref[...] = jnp.zeros_like(dnw_ref)

        dp = jnp.concatenate([r[...].astype(BF16) for r in part_refs], axis=1)
        dp_ref[...] = dp
        dhn = _dot_nt(dp, w_ref[...])
        xf = x_ref[...]
        r = _rstd(xf)
        dnw_ref[...] += jnp.sum(dhn * xf * r, axis=0, keepdims=True)
        dx_ref[...] = _rms_bwd(xf, r, dhn * nw_ref[...]) + dres_ref[...]

    return pl.pallas_call(
        body, grid=(t // tm,),
        in_specs=[pl.BlockSpec((tm, d), lambda i: (i, 0)), pl.BlockSpec((tm, d), lambda i: (i, 0)),
                  pl.BlockSpec((1, d), lambda i: (0, 0)), pl.BlockSpec((d, n), lambda i: (0, 0))]
                 + [pl.BlockSpec((tm, wd), lambda i: (i, 0)) for wd in widths],
        out_specs=[pl.BlockSpec((tm, d), lambda i: (i, 0)), pl.BlockSpec((1, d), lambda i: (0, 0)),
                   pl.BlockSpec((tm, n), lambda i: (i, 0))],
        out_shape=[jax.ShapeDtypeStruct((t, d), F32), jax.ShapeDtypeStruct((1, d), F32),
                   jax.ShapeDtypeStruct((t, n), BF16)],
        compiler_params=_cp("arbitrary"), name="inproj_bwd")(x, dres, nw, w, *dparts)


def outproj_fwd(x, yf, yg, w, tm=1024):
    t, d = x.shape
    hw = yf.shape[1]
    tm = min(tm, t)

    def body(x_ref, yf_ref, yg_ref, w_ref, o_ref, y_ref):
        y = jnp.concatenate([yf_ref[...], yg_ref[...]], axis=1).astype(BF16)
        y_ref[...] = y
        o_ref[...] = x_ref[...] + _dot(y, w_ref[...])

    return pl.pallas_call(
        body, grid=(t // tm,),
        in_specs=[pl.BlockSpec((tm, d), lambda i: (i, 0)), pl.BlockSpec((tm, hw), lambda i: (i, 0)),
                  pl.BlockSpec((tm, hw), lambda i: (i, 0)), pl.BlockSpec((2 * hw, d), lambda i: (0, 0))],
        out_specs=[pl.BlockSpec((tm, d), lambda i: (i, 0)), pl.BlockSpec((tm, 2 * hw), lambda i: (i, 0))],
        out_shape=[jax.ShapeDtypeStruct((t, d), F32), jax.ShapeDtypeStruct((t, 2 * hw), BF16)],
        compiler_params=_cp("parallel"), name="outproj_fwd")(x, yf, yg, w)


def outproj_bwd(dy, w, after=None, tm=1024):
    t, d = dy.shape
    hw = w.shape[0] // 2
    tm = min(tm, t)
    extra = [] if after is None else [after]

    def body(dy_ref, w_ref, *rest):
        df_ref, dg_ref, dyb_ref = rest[-3:]
        dyb = dy_ref[...].astype(BF16)
        dyb_ref[...] = dyb
        dyy = _dot_nt(dyb, w_ref[...])
        df_ref[...] = dyy[:, :hw]
        dg_ref[...] = dyy[:, hw:]

    return pl.pallas_call(
        body, grid=(t // tm,),
        in_specs=[pl.BlockSpec((tm, d), lambda i: (i, 0)), pl.BlockSpec((2 * hw, d), lambda i: (0, 0))]
                 + [pl.BlockSpec(memory_space=pl.ANY)] * len(extra),
        out_specs=[pl.BlockSpec((tm, hw), lambda i: (i, 0)), pl.BlockSpec((tm, hw), lambda i: (i, 0)),
                   pl.BlockSpec((tm, d), lambda i: (i, 0))],
        out_shape=[jax.ShapeDtypeStruct((t, hw), F32), jax.ShapeDtypeStruct((t, hw), F32),
                   jax.ShapeDtypeStruct((t, d), BF16)],
        compiler_params=_cp("parallel"), name="outproj_bwd")(dy, w, *extra)


def _lane(shape):
    return lax.broadcasted_iota(jnp.int32, shape, 1)


def _row(shape):
    return lax.broadcasted_iota(jnp.int32, shape, 0)


def _gate_terms(val, gp_ref):
    z = val + gp_ref[0:1, :]
    sp = jnp.log(1.0 + jnp.exp(-jnp.abs(z)))
    return z, sp


def gates_fwd(proj, gp, seq, ts=512):
    t = proj.shape[0]
    nb, ns = t // seq, seq // ts

    def body(v_ref, gp_ref, o_ref, carry_ref):
        @pl.when(pl.program_id(1) == 0)
        def _():
            carry_ref[...] = jnp.zeros_like(carry_ref)

        z, sp = _gate_terms(v_ref[...], gp_ref)
        logsig = jnp.minimum(z, 0.0) - sp
        tri = (_row((ts, ts)) >= _lane((ts, ts))).astype(F32)
        cum = jnp.dot(tri, logsig, precision=HIGHEST, preferred_element_type=F32) + carry_ref[0:1, :]
        carry_ref[0:1, :] = cum[ts - 1:ts, :]
        g = -jnp.exp(gp_ref[1:2, :]) * (jnp.maximum(z, 0.0) + sp)
        beta = jax.nn.sigmoid(z)
        lane = _lane((ts, 128))
        o_ref[...] = jnp.where(lane < 8, cum, jnp.where(lane < 12, g, jnp.where(lane < 16, beta, 0.0)))

    return pl.pallas_call(
        body, grid=(nb, ns),
        in_specs=[pl.BlockSpec((ts, 128), lambda b, s: (b * ns + s, COL_SMALL)),
                  pl.BlockSpec((8, 128), lambda b, s: (0, 0))],
        out_specs=pl.BlockSpec((ts, 128), lambda b, s: (b * ns + s, 0)),
        out_shape=jax.ShapeDtypeStruct((t, 128), F32),
        scratch_shapes=[pltpu.VMEM((8, 128), F32)],
        compiler_params=_cp("parallel", "arbitrary"), name="gates_fwd")(proj, gp)


def gates_bwd(proj, gp, dga, dgb, seq, ts=512):
    t = proj.shape[0]
    nb, ns = t // seq, seq // ts

    def body(v_ref, gp_ref, da_ref, db_ref, ds_ref, dgp_ref, carry_ref):
        @pl.when(pl.program_id(1) == 0)
        def _():
            carry_ref[...] = jnp.zeros_like(carry_ref)

        @pl.when((pl.program_id(0) == 0) & (pl.program_id(1) == 0))
        def _():
            dgp_ref[...] = jnp.zeros_like(dgp_ref)

        lane = _lane((ts, 128))
        dgate = jnp.where(lane < 8, da_ref[...], jnp.where(lane < 16, db_ref[...], 0.0))
        z, sp = _gate_terms(v_ref[...], gp_ref)
        triu = (_row((ts, ts)) <= _lane((ts, ts))).astype(F32)
        dlog = jnp.dot(triu, dgate, precision=HIGHEST, preferred_element_type=F32) + carry_ref[0:1, :]
        carry_ref[0:1, :] = dlog[0:1, :]
        sig = jax.nn.sigmoid(z)
        nea = -jnp.exp(gp_ref[1:2, :])
        g = nea * (jnp.maximum(z, 0.0) + sp)
        dz = jnp.where(lane < 8, dlog * (1.0 - sig),
                       jnp.where(lane < 12, dgate * nea * sig, dgate * sig * (1.0 - sig)))
        dz = jnp.where(lane < 16, dz, 0.0)
        ds_ref[...] = dz.astype(BF16)
        dgp_ref[0:1, :] += jnp.where(lane[0:1] < 12, jnp.sum(dz, axis=0, keepdims=True), 0.0)
        dgp_ref[1:2, :] += jnp.where((lane[0:1] >= 8) & (lane[0:1] < 12), jnp.sum(dgate * g, axis=0, keepdims=True), 0.0)

    rev = lambda b, s: (b * ns + (ns - 1 - s), 0)
    return pl.pallas_call(
        body, grid=(nb, ns),
        in_specs=[pl.BlockSpec((ts, 128), lambda b, s: (b * ns + (ns - 1 - s), COL_SMALL)),
                  pl.BlockSpec((8, 128), lambda b, s: (0, 0)),
                  pl.BlockSpec((ts, 128), rev), pl.BlockSpec((ts, 128), rev)],
        out_specs=[pl.BlockSpec((ts, 128), rev), pl.BlockSpec((8, 128), lambda b, s: (0, 0))],
        out_shape=[jax.ShapeDtypeStruct((t, 128), BF16), jax.ShapeDtypeStruct((8, 128), F32)],
        scratch_shapes=[pltpu.VMEM((8, 128), F32)],
        compiler_params=_cp("arbitrary", "arbitrary"), name="gates_bwd")(proj, gp, dga, dgb)


NEG = -1e30
ATTN_TQ_FWD = 1024
ATTN_TQ_BWD = 512


def _pick_lane(tile, idx):
    return jnp.sum(jnp.where(_lane(tile.shape) == idx, tile, 0.0), axis=1, keepdims=True)


def _row_to_col(row, n):
    return jnp.sum(jnp.where(_row((n, n)) == _lane((n, n)), row, 0.0), axis=1, keepdims=True)


def _rows(i, n):
    return pl.ds(pl.multiple_of(i * n, n), n)


LOG2E = 1.4426950408889634
LN2 = 0.6931471805599453


def _split_dot(x, mat, passes):
    total, rest = None, x
    for _ in range(passes):
        part = rest.astype(BF16)
        rest = rest - part.astype(F32)
        total = _dot(part, mat) if total is None else total + _dot(part, mat)
    return total


def _pair_mats(p, dh):
    r, l = _row((128, 128)), _lane((128, 128))
    same = (r < dh) == (l < dh)
    upper = (l >= dh).astype(jnp.int32)
    as_bf16 = lambda m: m.astype(BF16)
    return dict(own=as_bf16(same), other=as_bf16(jnp.logical_not(same)), pick_other=as_bf16(r == 2 * p + 1 - upper),
                swap=as_bf16(((r == 0) & (l >= dh)) | ((r == dh) & (l < dh))))


def _pair_rstd(x2, sel, dh):
    return lax.rsqrt(_split_dot(x2 * x2, sel["own"], 2) * (1.0 / dh) + EPS)


def _pair_aug(cols, n, dh):
    lane = _lane((n, 128))
    li = jnp.where(lane >= dh, lane - dh, lane)
    out = jnp.zeros((n, 128), F32)
    for i, c in enumerate(cols):
        out = jnp.where(li == i, c, out)
    return out


def _split3(x):
    hi = x.astype(BF16).astype(F32)
    mid = (x - hi).astype(BF16).astype(F32)
    return [hi, mid, (x - hi - mid).astype(BF16).astype(F32)]


def _once(shape, index_map):
    return pl.BlockSpec(shape, index_map, pipeline_mode=pl.Buffered(1))


def attn_fwd(proj, gates, qw, kw, seq, tq=256):
    t = proj.shape[0]
    nb, nq, dh = t // seq, seq // tq, FOX_DH
    scale = dh ** -0.5

    def body(q_ref, k_ref, v_ref, g_ref, qw_ref, kw_ref, y_ref, lse_ref, qs, ks, vs):
        p = pl.program_id(1)
        heads = range(2)
        low = _lane((tq, 128)) < dh
        sel = _pair_mats(p, dh)

        def prep(i, _):
            r = _rows(i, tq)
            q2, k2 = q_ref[r, :], k_ref[r, :]
            cc = _split_dot(g_ref[r, :], sel["pick_other"], 3) * LOG2E
            qn = q2 * _pair_rstd(q2, sel, dh) * qw_ref[...] * (scale * LOG2E)
            kn = k2 * _pair_rstd(k2, sel, dh) * kw_ref[...]
            qx = _pair_aug(_split3(cc) + [1.0, 1.0, 1.0], tq, dh)
            kx = _pair_aug([1.0, 1.0, 1.0] + _split3(-cc), tq, dh)
            for hh in heads:
                own = low if hh == 0 else jnp.logical_not(low)
                qs[hh, r, :] = jnp.where(own, qn, qx).astype(BF16)
                ks[hh, r, :] = jnp.where(own, kn, kx).astype(BF16)
            vs[r, :] = v_ref[r, :].astype(BF16)
            return 0

        lax.fori_loop(0, nq, prep, 0)

        def q_tile(i, _):
            r = _rows(i, tq)
            qt = [qs[hh, r, :] for hh in heads]

            def kv_step(j, carry, masked):
                kr = _rows(j, tq)
                vt = vs[kr, :]
                out = []
                for hh in heads:
                    m, l, acc = carry[hh]
                    s = _dot_nt(qt[hh], ks[hh, kr, :])
                    if masked:
                        s = jnp.where(_row((tq, tq)) >= _lane((tq, tq)), s, NEG)
                    m_new = jnp.maximum(m, jnp.max(s, axis=1, keepdims=True))
                    pe = jnp.exp2(s - m_new)
                    a = jnp.exp2(m - m_new)
                    out.append((m_new, a * l + jnp.sum(pe, axis=1, keepdims=True), a * acc + _dot(pe.astype(BF16), vt)))
                return tuple(out)

            one = (jnp.full((tq, 1), NEG, F32), jnp.zeros((tq, 1), F32), jnp.zeros((tq, 128), F32))
            carry = lax.fori_loop(0, i, lambda j, c: kv_step(j, c, False), (one, one))
            (m0, l0, acc0), (m1, l1, acc1) = kv_step(i, carry, True)
            y_ref[r, :] = jnp.where(low, acc0 / l0, acc1 / l1)
            lse_ref[r, :] = jnp.where(low, m0 + jnp.log2(l0), m1 + jnp.log2(l1))
            return 0

        lax.fori_loop(0, nq, q_tile, 0)

    blk = lambda off: _once((seq, 128), lambda b, p: (b, off + p))
    return pl.pallas_call(
        body, grid=(nb, 4),
        in_specs=[blk(0), blk(4), blk(8), _once((seq, 128), lambda b, p: (b, 0)),
                  pl.BlockSpec((1, 128), lambda b, p: (0, 0)), pl.BlockSpec((1, 128), lambda b, p: (0, 0))],
        out_specs=[pl.BlockSpec((seq, 128), lambda b, p: (b, p)), pl.BlockSpec((seq, 128), lambda b, p: (b, p))],
        out_shape=[jax.ShapeDtypeStruct((t, 512), F32), jax.ShapeDtypeStruct((t, 512), F32)],
        scratch_shapes=[pltpu.VMEM((2, seq, 128), BF16), pltpu.VMEM((2, seq, 128), BF16), pltpu.VMEM((seq, 128), BF16)],
        compiler_params=_cp("parallel", "arbitrary"),
        name="attn_fwd")(proj, proj, proj, gates, jnp.tile(qw, (1, 2)), jnp.tile(kw, (1, 2)))


def attn_bwd(proj, gates, qw, kw, y, lse, dy, seq, tq=256):
    t = proj.shape[0]
    nb, nq, dh = t // seq, seq // tq, FOX_DH
    scale = dh ** -0.5

    def body(q_ref, k_ref, v_ref, g_ref, qw_ref, kw_ref, y_ref, lse_ref, dy_ref,
             dq_ref, dk_ref, dv_ref, dg_ref, dqw_ref, dkw_ref,
             qs, ks, vs, dos, dsrow, dqa, dka):
        b, p = pl.program_id(0), pl.program_id(1)

        @pl.when((b == 0) & (p == 0))
        def _():
            dqw_ref[...] = jnp.zeros_like(dqw_ref)
            dkw_ref[...] = jnp.zeros_like(dkw_ref)

        @pl.when(p == 0)
        def _():
            dg_ref[...] = jnp.zeros_like(dg_ref)

        heads = range(2)
        low = _lane((tq, 128)) < dh
        sel = _pair_mats(p, dh)

        def prep(i, _):
            r = _rows(i, tq)
            q2, k2, dy2 = q_ref[r, :], k_ref[r, :], dy_ref[r, :]
            cc = _split_dot(g_ref[r, :], sel["pick_other"], 3) * LOG2E
            lse_x = _split_dot(lse_ref[r, :], sel["swap"], 3)
            delta_x = _split_dot(dy2 * y_ref[r, :], sel["other"], 2)
            qn = q2 * _pair_rstd(q2, sel, dh) * qw_ref[...] * (scale * LOG2E)
            kn = k2 * _pair_rstd(k2, sel, dh) * kw_ref[...]
            qx = _pair_aug(_split3(cc) + [1.0, 1.0, 1.0] + _split3(-lse_x), tq, dh)
            kx = _pair_aug([1.0, 1.0, 1.0] + _split3(-cc) + [1.0, 1.0, 1.0], tq, dh)
            vx = _pair_aug([1.0, 1.0, 1.0], tq, dh)
            dx = _pair_aug(_split3(-delta_x), tq, dh)
            for hh in heads:
                own = low if hh == 0 else jnp.logical_not(low)
                qs[hh, r, :] = jnp.where(own, qn, qx).astype(BF16)
                ks[hh, r, :] = jnp.where(own, kn, kx).astype(BF16)
                vs[hh, r, :] = jnp.where(own, v_ref[r, :], vx).astype(BF16)
                dos[hh, r, :] = jnp.where(own, dy2, dx).astype(BF16)
                dsrow[hh, r, :] = jnp.zeros((tq, 1), F32)
                dqa[hh, r, :] = jnp.zeros((tq, 128), F32)
            return 0

        lax.fori_loop(0, nq, prep, 0)

        def kv_tile(j, _):
            kr = _rows(j, tq)
            kt = [ks[hh, kr, :] for hh in heads]
            vt = [vs[hh, kr, :] for hh in heads]

            def q_step(i, carry, masked):
                r = _rows(i, tq)
                out = []
                for hh in heads:
                    dk, dv, dcr = carry[hh]
                    qt, dot = qs[hh, r, :], dos[hh, r, :]
                    s = _dot_nt(qt, kt[hh])
                    if masked:
                        s = jnp.where(_row((tq, tq)) >= _lane((tq, tq)), s, NEG)
                    pe = jnp.exp2(s)
                    ds = pe * _dot_nt(dot, vt[hh])
                    dsb = ds.astype(BF16)
                    dqa[hh, r, :] += _dot(dsb, kt[hh])
                    dsrow[hh, r, :] += jnp.sum(ds, axis=1, keepdims=True)
                    out.append((dk + _dot_tn(dsb, qt), dv + _dot_tn(pe.astype(BF16), dot),
                                dcr - jnp.sum(ds, axis=0, keepdims=True)))
                return tuple(out)

            one = (jnp.zeros((tq, 128), F32), jnp.zeros((tq, 128), F32), jnp.zeros((1, tq), F32))
            carry = q_step(j, (one, one), True)
            (dk0, dv0, dcr0), (dk1, dv1, dcr1) = lax.fori_loop(j + 1, nq, lambda i, c: q_step(i, c, False), carry)
            dka[kr, :] = jnp.where(low, dk0, dk1)
            dv_ref[kr, :] = jnp.where(low, dv0, dv1).astype(BF16)
            lane = _lane((tq, 128))
            dg_ref[kr, :] = jnp.where(lane == 2 * p, _row_to_col(dcr0, tq),
                                      jnp.where(lane == 2 * p + 1, _row_to_col(dcr1, tq), dg_ref[kr, :]))
            return 0

        lax.fori_loop(0, nq, kv_tile, 0)

        def post(i, _):
            r = _rows(i, tq)
            q2, k2 = q_ref[r, :], k_ref[r, :]
            rq, rk = _pair_rstd(q2, sel, dh), _pair_rstd(k2, sel, dh)
            dqn = jnp.where(low, dqa[0, r, :], dqa[1, r, :]) * scale
            dkn = dka[r, :] * LN2
            dqw_ref[...] += jnp.sum(dqn * q2 * rq, axis=0, keepdims=True)
            dkw_ref[...] += jnp.sum(dkn * k2 * rk, axis=0, keepdims=True)
            for x2, rr, dyn, o_ref in ((q2, rq, dqn * qw_ref[...], dq_ref), (k2, rk, dkn * kw_ref[...], dk_ref)):
                mean = _split_dot(dyn * x2, sel["own"], 2) * (1.0 / dh)
                o_ref[r, :] = (rr * dyn - x2 * (rr * rr * rr) * mean).astype(BF16)
            lane = _lane((tq, 128))
            dg_ref[r, :] += jnp.where(lane == 2 * p, dsrow[0, r, :], jnp.where(lane == 2 * p + 1, dsrow[1, r, :], 0.0))
            return 0

        lax.fori_loop(0, nq, post, 0)

    blk = lambda off: _once((seq, 128), lambda b, p: (b, off + p))
    own = lambda: _once((seq, 128), lambda b, p: (b, p))
    vec = lambda: pl.BlockSpec((1, 128), lambda b, p: (0, 0))
    res = pl.pallas_call(
        body, grid=(nb, 4),
        in_specs=[blk(0), blk(4), blk(8), _once((seq, 128), lambda b, p: (b, 0)), vec(), vec(), own(), own(), own()],
        out_specs=[own(), own(), own(), _once((seq, 128), lambda b, p: (b, 0)), vec(), vec()],
        out_shape=[jax.ShapeDtypeStruct((t, 512), BF16)] * 3
                  + [jax.ShapeDtypeStruct((t, 128), F32), jax.ShapeDtypeStruct((1, 128), F32), jax.ShapeDtypeStruct((1, 128), F32)],
        scratch_shapes=[pltpu.VMEM((2, seq, 128), BF16)] * 4
                       + [pltpu.VMEM((2, seq, 1), F32), pltpu.VMEM((2, seq, 128), F32), pltpu.VMEM((seq, 128), F32)],
        compiler_params=_cp("arbitrary", "arbitrary"),
        name="attn_bwd")(proj, proj, proj, gates, jnp.tile(qw, (1, 2)), jnp.tile(kw, (1, 2)), y, lse, dy)
    return list(res[:4]) + [res[4][:, :dh] + res[4][:, dh:], res[5][:, :dh] + res[5][:, dh:]]


def _silu_grad(c, sg):
    return sg * (1.0 + c * (1.0 - sg))


def _conv(x, w, n):
    row = _row(x.shape)
    c = x * w[CONV_W - 1:CONV_W, :]
    for k in range(CONV_W - 1):
        sh = CONV_W - 1 - k
        c = c + w[k:k + 1, :] * jnp.where(row >= sh, pltpu.roll(x, sh, 0), 0.0)
    return c


def gdn_pre_fwd(proj, cw, seq):
    t = proj.shape[0]
    nb = t // seq
    scale = GDN_DH ** -0.5

    rc = min(512, seq)
    halo = 8

    def body(xq_ref, xk_ref, xv_ref, wq_ref, wk_ref, wv_ref, q_ref, k_ref, v_ref):
        def act(x_ext, w):
            c = x_ext * w[CONV_W - 1:CONV_W, :]
            for k in range(CONV_W - 1):
                c = c + w[k:k + 1, :] * pltpu.roll(x_ext, CONV_W - 1 - k, 0)
            c = c[halo:, :]
            return c * jax.nn.sigmoid(c)

        def chunk(r, ext):
            aq, ak = act(ext(xq_ref), wq_ref[...]), act(ext(xk_ref), wk_ref[...])
            q_ref[r, :] = aq * lax.rsqrt(jnp.sum(aq * aq, axis=1, keepdims=True) + EPS) * scale
            k_ref[r, :] = ak * lax.rsqrt(jnp.sum(ak * ak, axis=1, keepdims=True) + EPS)
            v_ref[r, :] = act(ext(xv_ref), wv_ref[...])

        chunk(slice(0, rc), lambda ref: jnp.concatenate([jnp.zeros((halo, 128), F32), ref[0:rc, :]], axis=0))

        def later(i, _):
            start = pl.multiple_of(i * rc - halo, halo)
            chunk(_rows(i, rc), lambda ref: ref[pl.ds(start, rc + halo), :])
            return 0

        lax.fori_loop(1, seq // rc, later, 0)

    xb = lambda off: pl.BlockSpec((seq, 128), lambda b, h: (b, off + h))
    wb = lambda off: pl.BlockSpec((CONV_W, 128), lambda b, h: (0, off + h))
    ob = lambda: pl.BlockSpec((seq, 128), lambda b, h: (b, h))
    return pl.pallas_call(
        body, grid=(nb, GDN_HEADS),
        in_specs=[xb(12), xb(16), xb(20), wb(0), wb(4), wb(8)],
        out_specs=[ob(), ob(), ob()],
        out_shape=[jax.ShapeDtypeStruct((t, 512), F32)] * 3,
        compiler_params=_cp("parallel", "parallel"), name="gdn_pre_fwd")(proj, proj, proj, cw, cw, cw)


def gdn_pre_bwd(proj, cw, dq, dk, dv, seq):
    t = proj.shape[0]
    nb = t // seq
    scale = GDN_DH ** -0.5

    def body(xq_ref, xk_ref, xv_ref, wq_ref, wk_ref, wv_ref, dq_ref, dk_ref, dv_ref,
             dxq_ref, dxk_ref, dxv_ref, dwq_ref, dwk_ref, dwv_ref):
        first = pl.program_id(1) == 0
        row = _row((seq, 128))

        def one(x_ref, w_ref, dy_ref, dx_ref, dw_ref, norm, sc):
            x, w = x_ref[...], w_ref[...]
            c = _conv(x, w, seq)
            sg = jax.nn.sigmoid(c)
            dy = dy_ref[...]
            if norm:
                a = c * sg
                rs = lax.rsqrt(jnp.sum(a * a, axis=1, keepdims=True) + EPS)
                dy = dy * sc
                da = rs * dy - a * (rs * rs * rs) * jnp.sum(dy * a, axis=1, keepdims=True)
            else:
                da = dy
            dc = da * _silu_grad(c, sg)
            dx = dc * w[CONV_W - 1:CONV_W, :]
            dws = [None] * CONV_W
            dws[CONV_W - 1] = jnp.sum(dc * x, axis=0, keepdims=True)
            for k in range(CONV_W - 1):
                sh = CONV_W - 1 - k
                dc_up = jnp.where(row < seq - sh, pltpu.roll(dc, seq - sh, 0), 0.0)
                dx = dx + w[k:k + 1, :] * dc_up
                dws[k] = jnp.sum(dc_up * x, axis=0, keepdims=True)
            dx_ref[...] = dx.astype(BF16)
            dwn = jnp.concatenate(dws, axis=0)

            @pl.when(first)
            def _():
                dw_ref[...] = dwn

            @pl.when(jnp.logical_not(first))
            def _():
                dw_ref[...] += dwn

        one(xq_ref, wq_ref, dq_ref, dxq_ref, dwq_ref, True, scale)
        one(xk_ref, wk_ref, dk_ref, dxk_ref, dwk_ref, True, 1.0)
        one(xv_ref, wv_ref, dv_ref, dxv_ref, dwv_ref, False, 1.0)

    xb = lambda off: pl.BlockSpec((seq, 128), lambda h, b: (b, off + h))
    wb = lambda off: pl.BlockSpec((CONV_W, 128), lambda h, b: (0, off + h))
    ob = lambda: pl.BlockSpec((seq, 128), lambda h, b: (b, h))
    return pl.pallas_call(
        body, grid=(GDN_HEADS, nb),
        in_specs=[xb(12), xb(16), xb(20), wb(0), wb(4), wb(8), ob(), ob(), ob()],
        out_specs=[ob(), ob(), ob()] + [pl.BlockSpec((CONV_W, 128), lambda h, b: (0, h))] * 3,
        out_shape=[jax.ShapeDtypeStruct((t, 512), BF16)] * 3 + [jax.ShapeDtypeStruct((CONV_W, 512), F32)] * 3,
        compiler_params=_cp("parallel", "arbitrary"), name="gdn_pre_bwd")(proj, proj, proj, cw, cw, cw, dq, dk, dv)


def _b16(x):
    return x.astype(BF16)


@jax.custom_vjp
def _mm(a, b):
    return _dot(_b16(a), _b16(b))


_mm.defvjp(lambda a, b: (_mm(a, b), (a, b)),
           lambda res, g: (_dot_nt(_b16(g), _b16(res[1])), _dot_tn(_b16(res[0]), _b16(g))))


@jax.custom_vjp
def _mm_nt(a, b):
    return _dot_nt(_b16(a), _b16(b))


_mm_nt.defvjp(lambda a, b: (_mm_nt(a, b), (a, b)),
              lambda res, g: (_dot(_b16(g), _b16(res[1])), _dot_tn(_b16(g), _b16(res[0]))))


@jax.custom_vjp
def _mm_tn(a, b):
    return _dot_tn(_b16(a), _b16(b))


_mm_tn.defvjp(lambda a, b: (_mm_tn(a, b), (a, b)),
              lambda res, g: (_dot_nt(_b16(res[1]), _b16(g)), _dot(_b16(res[0]), _b16(g))))


def _dot32(a, b, dims=(((1,), (0,)), ((), ()))):
    def split(x):
        hi = x.astype(BF16)
        return hi, (x - hi.astype(F32)).astype(BF16)

    (ah, al), (bh, bl) = split(a), split(b)
    d = lambda x, y: lax.dot_general(x, y, dims, preferred_element_type=F32)
    return d(ah, bh) + (d(ah, bl) + d(al, bh))


INV_BLOCK = 4


def _inv_fwd_many(mats):
    n = mats[0].shape[0]
    r, c = _row((n, n)), _lane((n, n))
    eye = (r == c).astype(F32)

    def same_block(width):
        bits = jnp.int32(width.bit_length() - 1)
        return lax.shift_right_logical(r, bits) == lax.shift_right_logical(c, bits)

    diag = [jnp.where(same_block(INV_BLOCK), a, 0.0) for a in mats]
    invs, pws = [eye - d for d in diag], diag
    for _ in range(INV_BLOCK.bit_length() - 2):
        pws = [_dot32(pw, pw) for pw in pws]
        invs = [inv + _dot32(inv, pw) for inv, pw in zip(invs, pws)]
    width = INV_BLOCK
    while width < n:
        off = jnp.logical_and(same_block(2 * width), jnp.logical_not(same_block(width)))
        invs = [inv - _dot32(_dot32(inv, jnp.where(off, a, 0.0)), inv) for inv, a in zip(invs, mats)]
        width *= 2
    return invs


@jax.custom_vjp
def _inv_saved(a, inv):
    return inv


def _inv_saved_bwd(inv, g):
    tg = _dot32(inv, g, (((0,), (0,)), ((), ())))
    return -_dot32(tg, inv, (((1,), (1,)), ((), ()))), jnp.zeros_like(inv)


_inv_saved.defvjp(lambda a, inv: (inv, inv), _inv_saved_bwd)


def _gdn_decay(gcol):
    c = CHUNK
    ri, ci = _row((c, c)), _lane((c, c))
    incl, eye = ri >= ci, ri == ci
    grow = jnp.sum(jnp.where(eye, gcol, 0.0), axis=0, keepdims=True)
    gc = jnp.sum(jnp.where(incl, grow, 0.0), axis=1, keepdims=True)
    gcr = jnp.sum(jnp.where(eye, gc, 0.0), axis=0, keepdims=True)
    gl = jnp.sum(jnp.where(_row((c, 1)) == c - 1, gc, 0.0), axis=0, keepdims=True)
    return gc, gl, jnp.exp(jnp.where(incl, gc - gcr, NEG))


def _gdn_a(k, bcol, decay):
    c = CHUNK
    return jnp.where(_row((c, c)) > _lane((c, c)), _mm_nt(k * bcol, k) * decay, 0.0)


def _gdn_chunk(q, k, v, gcol, bcol, state, gg, nw, inv_saved):
    c = CHUNK
    incl = _row((c, c)) >= _lane((c, c))
    gc, gl, decay = _gdn_decay(gcol)
    kb, vb = k * bcol, v * bcol
    inv = _inv_saved(_gdn_a(k, bcol, decay), inv_saved)
    eg = jnp.exp(gc)
    u = _mm(inv, vb)
    w = _mm(inv, kb * eg)
    pm = jnp.where(incl, _mm_nt(q, k) * decay, 0.0)
    kd = k * jnp.exp(gl - gc)
    qd = q * eg
    v_new = u - _mm(w, state)
    o = _mm(qd, state) + _mm(pm, v_new)
    state_new = state * jnp.exp(gl) + _mm_tn(kd, v_new)
    y = o * _rstd(o) * nw * (gg * jax.nn.sigmoid(gg))
    return y, state_new


_gdn_chunks = jax.vmap(_gdn_chunk, in_axes=(0, 0, 0, 0, 0, 0, 0, None, 0))


def _gdn_chain_inputs(chains, p, r, c, q_ref, k_ref, v_ref, g_ref, gg_ref, inv_ref):
    cols = {nm: [] for nm in ("q", "k", "v", "g", "b", "gg", "inv")}
    for b, hh in chains:
        h = GDN_HPS * p + hh
        ln = slice(hh * 128, (hh + 1) * 128)
        gt = g_ref[b, r, :]
        cols["q"].append(q_ref[b, r, ln])
        cols["k"].append(k_ref[b, r, ln])
        cols["v"].append(v_ref[b, r, ln])
        cols["g"].append(_pick_lane(gt, 8 + h))
        cols["b"].append(_pick_lane(gt, 12 + h))
        cols["gg"].append(gg_ref[b, r, ln])
        cols["inv"].append(inv_ref[b, hh, c])
    return [jnp.stack(cols[nm]) for nm in ("q", "k", "v", "g", "b", "gg", "inv")]


GDN_CB = 8
GDN_INV_CB = 16
GDN_HPS = 4


def gdn_inv(k, gates, seq):
    t = k.shape[0]
    nb, nc = t // seq, seq // CHUNK
    cb = min(GDN_INV_CB, nc)
    rb = cb * CHUNK
    nsb = seq // rb

    def body(k_ref, g_ref, o_ref):
        h = pl.program_id(1)
        mats = []
        for c in range(cb):
            r = slice(c * CHUNK, (c + 1) * CHUNK)
            gt = g_ref[r, :]
            _, _, decay = _gdn_decay(_pick_lane(gt, 8 + h))
            mats.append(_gdn_a(k_ref[r, :], _pick_lane(gt, 12 + h), decay))
        for c, inv in enumerate(_inv_fwd_many(mats)):
            o_ref[c] = inv

    return pl.pallas_call(
        body, grid=(nb, GDN_HEADS, nsb),
        in_specs=[pl.BlockSpec((rb, 128), lambda b, h, s: (b * nsb + s, h)),
                  pl.BlockSpec((rb, 128), lambda b, h, s: (b * nsb + s, 0))],
        out_specs=pl.BlockSpec((None, None, cb, CHUNK, CHUNK), lambda b, h, s: (b, h, s, 0, 0)),
        out_shape=jax.ShapeDtypeStruct((nb, GDN_HEADS, nc, CHUNK, CHUNK), F32),
        compiler_params=_cp("parallel", "parallel", "parallel"), name="gdn_inv")(k, gates)


def _gdn_specs(nb, nsb, cb, rev):
    blk = (lambda s: nsb - 1 - s) if rev else (lambda s: s)
    rb = cb * CHUNK
    pair = lambda off=0: pl.BlockSpec((nb, rb, 128 * GDN_HPS), lambda s, p: (0, blk(s), off + p))
    gate = lambda: pl.BlockSpec((nb, rb, 128), lambda s, p: (0, blk(s), 0))
    mats = lambda n: pl.BlockSpec((nb, GDN_HPS, cb, n, n), lambda s, p: (0, p, blk(s), 0, 0))
    return pair, gate, mats


def gdn_fwd(q, k, v, gates, proj, nw, inv, seq):
    t = q.shape[0]
    nb, nc = t // seq, seq // CHUNK
    cb = GDN_CB
    nsb = nc // cb
    chains = [(b, hh) for b in range(nb) for hh in range(GDN_HPS)]
    nch = len(chains)
    pair, gate, mats = _gdn_specs(nb, nsb, cb, False)

    def body(q_ref, k_ref, v_ref, g_ref, gg_ref, inv_ref, nw_ref, y_ref, st_ref, carry):
        s, p = pl.program_id(0), pl.program_id(1)

        @pl.when(s == 0)
        def _():
            for ci in range(nch):
                carry[p * nch + ci] = jnp.zeros((GDN_DH, GDN_DH), F32)

        def step(c, states):
            r = _rows(c, CHUNK)
            for ci, (b, hh) in enumerate(chains):
                st_ref[b, hh, c] = states[ci]
            ins = _gdn_chain_inputs(chains, p, r, c, q_ref, k_ref, v_ref, g_ref, gg_ref, inv_ref)
            y, states = _gdn_chunks(*ins[:5], states, ins[5], nw_ref[...], ins[6])
            for ci, (b, hh) in enumerate(chains):
                y_ref[b, r, hh * 128:(hh + 1) * 128] = y[ci]
            return states

        states = lax.fori_loop(0, cb, step, jnp.stack([carry[p * nch + ci] for ci in range(nch)]))
        for ci in range(nch):
            carry[p * nch + ci] = states[ci]

    v3 = lambda a: a.reshape(nb, seq, a.shape[1])
    y, st = pl.pallas_call(
        body, grid=(nsb, GDN_HEADS // GDN_HPS),
        in_specs=[pair(), pair(), pair(), gate(), pair(24 // GDN_HPS), mats(CHUNK), pl.BlockSpec((1, 128), lambda s, p: (0, 0))],
        out_specs=[pair(), mats(GDN_DH)],
        out_shape=[jax.ShapeDtypeStruct((nb, seq, 512), F32),
                   jax.ShapeDtypeStruct((nb, GDN_HEADS, nc, GDN_DH, GDN_DH), F32)],
        scratch_shapes=[pltpu.VMEM((GDN_HEADS // GDN_HPS * nch, GDN_DH, GDN_DH), F32)],
        compiler_params=_cp("arbitrary", "arbitrary"), name="gdn_fwd")(v3(q), v3(k), v3(v), v3(gates), v3(proj), inv, nw)
    return y.reshape(t, 512), st


def gdn_bwd(q, k, v, gates, proj, nw, inv, states, dy, seq):
    t = q.shape[0]
    nb, nc = t // seq, seq // CHUNK
    cb = GDN_CB // 2
    nsb = nc // cb
    chains = [(b, hh) for b in range(nb) for hh in range(GDN_HPS)]
    nch = len(chains)
    pair, gate, mats = _gdn_specs(nb, nsb, cb, True)

    def body(q_ref, k_ref, v_ref, g_ref, gg_ref, inv_ref, st_ref, dy_ref, nw_ref,
             dq_ref, dk_ref, dv_ref, dgg_ref, dg_ref, dnw_ref, carry):
        s, p = pl.program_id(0), pl.program_id(1)

        @pl.when((s == 0) & (p == 0))
        def _():
            dnw_ref[...] = jnp.zeros_like(dnw_ref)

        @pl.when(p == 0)
        def _():
            dg_ref[...] = jnp.zeros_like(dg_ref)

        @pl.when(s == 0)
        def _():
            for ci in range(nch):
                carry[p * nch + ci] = jnp.zeros((GDN_DH, GDN_DH), F32)

        def step(idx, dstates):
            c = cb - 1 - idx
            r = _rows(c, CHUNK)
            ins = _gdn_chain_inputs(chains, p, r, c, q_ref, k_ref, v_ref, g_ref, gg_ref, inv_ref)
            st = jnp.stack([st_ref[b, hh, c] for b, hh in chains])
            dy = jnp.stack([dy_ref[b, r, hh * 128:(hh + 1) * 128] for b, hh in chains])
            _, vjp = jax.vjp(_gdn_chunks, *ins[:5], st, ins[5], nw_ref[...], ins[6])
            dq, dk, dv, dgc, dbc, dstates, dgg, dnw, _ = vjp((dy, dstates))
            dnw_ref[...] += dnw
            lane = _lane((CHUNK, 128))
            for ci, (b, hh) in enumerate(chains):
                h = GDN_HPS * p + hh
                ln = slice(hh * 128, (hh + 1) * 128)
                dq_ref[b, r, ln] = dq[ci]
                dk_ref[b, r, ln] = dk[ci]
                dv_ref[b, r, ln] = dv[ci]
                dgg_ref[b, r, ln] = dgg[ci].astype(BF16)
                dg_ref[b, r, :] = jnp.where(lane == 8 + h, dgc[ci], jnp.where(lane == 12 + h, dbc[ci], dg_ref[b, r, :]))
            return dstates

        dstates = lax.fori_loop(0, cb, step, jnp.stack([carry[p * nch + ci] for ci in range(nch)]))
        for ci in range(nch):
            carry[p * nch + ci] = dstates[ci]

    v3 = lambda a: a.reshape(nb, seq, a.shape[1])
    res = pl.pallas_call(
        body, grid=(nsb, GDN_HEADS // GDN_HPS),
        in_specs=[pair(), pair(), pair(), gate(), pair(24 // GDN_HPS), mats(CHUNK), mats(GDN_DH), pair(),
                  pl.BlockSpec((1, 128), lambda s, p: (0, 0))],
        out_specs=[pair(), pair(), pair(), pair(), gate(), pl.BlockSpec((1, 128), lambda s, p: (0, 0))],
        out_shape=[jax.ShapeDtypeStruct((nb, seq, 512), F32)] * 3 + [jax.ShapeDtypeStruct((nb, seq, 512), BF16)]
                  + [jax.ShapeDtypeStruct((nb, seq, 128), F32), jax.ShapeDtypeStruct((1, 128), F32)],
        scratch_shapes=[pltpu.VMEM((GDN_HEADS // GDN_HPS * nch, GDN_DH, GDN_DH), F32)],
        compiler_params=_cp("arbitrary", "arbitrary"),
        name="gdn_bwd")(v3(q), v3(k), v3(v), v3(gates), v3(proj), inv, states, v3(dy), nw)
    return [a.reshape(t, a.shape[2]) for a in res[:5]] + [res[5]]


def loss_head(y, target, tm=512):
    t, d = y.shape

    def body(y_ref, t_ref, s_ref, dy_ref):
        @pl.when(pl.program_id(0) == 0)
        def _():
            s_ref[...] = jnp.zeros_like(s_ref)

        err = y_ref[...] - t_ref[...]
        s_ref[...] += jnp.sum(err * err, axis=0, keepdims=True)
        dy_ref[...] = err * (1.0 / d)

    return pl.pallas_call(
        body, grid=(t // tm,),
        in_specs=[pl.BlockSpec((tm, d), lambda i: (i, 0)), pl.BlockSpec((tm, d), lambda i: (i, 0))],
        out_specs=[pl.BlockSpec((1, d), lambda i: (0, 0)), pl.BlockSpec((tm, d), lambda i: (i, 0))],
        out_shape=[jax.ShapeDtypeStruct((1, d), F32), jax.ShapeDtypeStruct((t, d), F32)],
        compiler_params=_cp("arbitrary"), name="loss_head")(y, target)


def _place():
    return lax.axis_index("x"), lax.axis_index("y"), lax.axis_index("c")


def _peer(k):
    x, y, c = _place()
    px = 1 - x if (k >> 2) & 1 else x
    py = 1 - y if (k >> 1) & 1 else y
    pc = 1 - c if k & 1 else c
    return (px, py, pc), 4 * px + 2 * py + pc


_ANY = pl.BlockSpec(memory_space=pl.ANY)
_SEM = pl.BlockSpec(memory_space=pltpu.SEMAPHORE)
_EFFECT = pltpu.SideEffectType.DATAFLOW_SIDE_EFFECTING


def _me():
    x, y, c = _place()
    return 4 * x + 2 * y + c


def _remote_copy(ins, lands, scatter, send_sems, recv_sems, a, k, arriving):
    pid, pidx = _peer(k)
    return pltpu.make_async_remote_copy(src_ref=ins[a].at[pidx] if scatter[a] else ins[a],
                                        dst_ref=lands[a].at[pidx if arriving else _me()],
                                        send_sem=send_sems.at[a * N_DEV + k], recv_sem=recv_sems.at[a * N_DEV + k],
                                        device_id=pid, device_id_type=MESH)


def _local_copy(ins, lands, scatter, loc_sems, a):
    me = _me()
    return pltpu.make_async_copy(ins[a].at[me] if scatter[a] else ins[a], lands[a].at[me], loc_sems.at[a])


def exchange_start(arrays, scatter, name, after):
    n = len(arrays)
    lands = [lax.empty(a.shape if s else (N_DEV,) + a.shape, a.dtype) for a, s in zip(arrays, scatter)]

    def body(*refs):
        ins, lds = refs[:n], refs[n:2 * n]
        send_sems, recv_sems, loc_sems = refs[2 * n + 1:2 * n + 4]
        token = refs[-1]
        for k in range(1, N_DEV):
            for a in range(n):
                _remote_copy(ins, lds, scatter, send_sems, recv_sems, a, k, False).start()
        for a in range(n):
            _local_copy(ins, lds, scatter, loc_sems, a).start()
        token[...] = jnp.zeros_like(token)

    hbm = lambda a: pltpu.HBM(a.shape, a.dtype)
    res = pl.pallas_call(
        body, name=name,
        in_specs=[_ANY] * (2 * n + 1),
        out_specs=[_SEM, _SEM, _SEM] + [_ANY] * (2 * n) + [pl.BlockSpec(memory_space=pltpu.VMEM)],
        out_shape=[pltpu.SemaphoreType.DMA((n * N_DEV,)), pltpu.SemaphoreType.DMA((n * N_DEV,)),
                   pltpu.SemaphoreType.DMA((n,))]
                  + [hbm(a) for a in arrays] + [hbm(a) for a in lands] + [jax.ShapeDtypeStruct((8, 128), F32)],
        input_output_aliases={i: 3 + i for i in range(2 * n)},
        compiler_params=pltpu.CompilerParams(has_side_effects=_EFFECT),
    )(*[pltpu.with_memory_space_constraint(a, pltpu.HBM) for a in list(arrays) + lands], after)
    return res[0:3], res[3:3 + n], res[3 + n:3 + 2 * n], res[-1]


def exchange_wait(sems, arrays, lands, scatter, after, name):
    n = len(arrays)

    def body(*refs):
        ins, lds = refs[:n], refs[n:2 * n]
        ssem, rsem, lsem = refs[2 * n:2 * n + 3]
        for a in range(n):
            _local_copy(ins, lds, scatter, lsem, a).wait()
        for k in range(1, N_DEV):
            for a in range(n):
                _remote_copy(ins, lds, scatter, ssem, rsem, a, k, True).wait_recv()
        for k in range(1, N_DEV):
            for a in range(n):
                _remote_copy(ins, lds, scatter, ssem, rsem, a, k, False).wait_send()

    hbm = lambda a: pltpu.HBM(a.shape, a.dtype)
    res = pl.pallas_call(
        body, name=name,
        in_specs=[_ANY] * (2 * n) + [_SEM, _SEM, _SEM, _ANY],
        out_specs=[_ANY] * (2 * n),
        out_shape=[hbm(a) for a in arrays] + [hbm(a) for a in lands],
        input_output_aliases={i: i for i in range(2 * n)},
        compiler_params=pltpu.CompilerParams(has_side_effects=_EFFECT),
    )(*arrays, *lands, *sems, after)
    return list(res[n:])


def gather_two_level(arrays, name):
    n = len(arrays)

    def body(*refs):
        ins, outs = refs[:n], refs[n:2 * n]
        send_sems, recv_sems, loc_sems = refs[2 * n:]
        x, y, c = _place()
        sibling = (x, y, 1 - c)
        chips = [(1 - x, y), (x, 1 - y), (1 - x, 1 - y)]

        def slot(pos):
            return 4 * pos[0] + 2 * pos[1] + pos[2]

        def copy(a, k, block, to, from_input=False):
            return pltpu.make_async_remote_copy(
                src_ref=ins[a] if from_input else outs[a].at[slot(block)], dst_ref=outs[a].at[slot(block)],
                send_sem=send_sems.at[a, k], recv_sem=recv_sems.at[a, k], device_id=to, device_id_type=MESH)

        me = (x, y, c)
        mine = [pltpu.make_async_copy(ins[a], outs[a].at[slot(me)], loc_sems.at[a]) for a in range(n)]
        for cp in mine:
            cp.start()
        first = [copy(a, 0, me, sibling, True) for a in range(n)]
        first += [copy(a, 1 + j, me, (*chip, c), True) for j, chip in enumerate(chips) for a in range(n)]
        for cp in first:
            cp.start()
        passed = []
        for j, chip in enumerate(chips):
            for a in range(n):
                copy(a, 1 + j, (*chip, c), me).wait_recv()
                passed.append(copy(a, 4 + j, (*chip, c), sibling))
                passed[-1].start()
        for a in range(n):
            copy(a, 0, sibling, me).wait_recv()
            for j, chip in enumerate(chips):
                copy(a, 4 + j, (*chip, 1 - c), me).wait_recv()
        for cp in first + passed:
            cp.wait_send()
        for cp in mine:
            cp.wait()

    return pl.pallas_call(
        body, in_specs=[_ANY] * n, out_specs=[_ANY] * n,
        out_shape=[jax.ShapeDtypeStruct((N_DEV,) + a.shape, a.dtype) for a in arrays],
        scratch_shapes=[pltpu.SemaphoreType.DMA((n, 7)), pltpu.SemaphoreType.DMA((n, 7)), pltpu.SemaphoreType.DMA((n,))],
        name=name)(*arrays)


def exchange_begin(arrays, scatter, name, after):
    sems, arrays_thru, lands_thru, token = exchange_start(arrays, scatter, name + "_start", after)
    return (sems, arrays_thru, lands_thru, scatter, name), token


def exchange_end(state, after):
    sems, arrays_thru, lands_thru, scatter, name = state
    return exchange_wait(sems, arrays_thru, lands_thru, scatter, after, name + "_wait")


def adamw_reduce(slots, w, m, v, l, name, after=None, prev=None):
    nl, r, c = w.shape
    tr = r
    while tr * c * 4 > (1 << 20) and tr % 16 == 0:
        tr //= 2
    bc1 = 1.0 - ADAM_B1 ** ADAM_STEP
    bc2 = 1.0 - ADAM_B2 ** ADAM_STEP

    def body(s_ref, w_ref, m_ref, v_ref, *rest):
        g_ref, d_ref, nm_ref, nv_ref = rest[-4:]
        g = s_ref[0].astype(F32)
        for j in range(1, N_DEV):
            g = g + s_ref[j].astype(F32)
        nm = ADAM_B1 * m_ref[...] + (1.0 - ADAM_B1) * g
        nv = ADAM_B2 * v_ref[...] + (1.0 - ADAM_B2) * (g * g)
        g_ref[...] = g
        nm_ref[...] = nm
        nv_ref[...] = nv
        d_ref[...] = -ADAM_LR * ((nm / bc1) / (jnp.sqrt(nv / bc2) + ADAM_EPS) + ADAM_WD * w_ref[...])

    blk = lambda: pl.BlockSpec((None, tr, c), lambda i: (l, i, 0))
    extra = ([] if after is None else [after]) + ([] if prev is None else list(prev))
    first_prev = 4 + (after is not None)
    return pl.pallas_call(
        body, grid=(r // tr,),
        in_specs=[pl.BlockSpec((N_DEV, tr, c), lambda i: (0, i, 0)), blk(), blk(), blk()] + [_ANY] * len(extra),
        out_specs=[blk(), blk(), blk(), blk()],
        out_shape=[jax.ShapeDtypeStruct((nl, r, c), F32)] * 4,
        input_output_aliases={} if prev is None else {first_prev + j: j for j in range(4)},
        compiler_params=_cp("parallel"), name=name)(slots, w, m, v, *extra)


BIG = ("ffn1_w_in", "ffn1_w_out", "w_in", "gdn_conv", "w_out", "ffn2_w_in", "ffn2_w_out")
GROUPS = (BIG[0:2], BIG[2:5], BIG[5:7])
SMALL = ("ffn1_norm", "mix_norm", "fox_q_norm", "fox_k_norm", "fox_f_bias", "gdn_a_log", "gdn_dt_bias",
         "gdn_out_norm", "ffn2_norm")
WEIGHTS = ("ffn1_norm", "ffn1_w_in", "ffn1_w_out", "mix_norm", "w_in", "fox_q_norm", "fox_k_norm", "fox_f_bias",
           "gdn_conv", "gdn_a_log", "gdn_dt_bias", "gdn_out_norm", "w_out", "ffn2_norm", "ffn2_w_in", "ffn2_w_out")
IN_COLS = (("fq", 512), ("fk", 512), ("fv", 512), ("ff", 8), ("gq", 512), ("gk", 512), ("gv", 512),
           ("ga", 4), ("gb", 4), ("gg", 512))
MY_BIG = ("fq", "fk", "fv", "gq", "gk", "gv", "gg")
MY_SMALL = ("ff", "ga", "gb")
SMALL_ROWS = 8 * 128


def _in_cols_to_mine(w):
    off, parts = 0, {}
    for nm, wd in IN_COLS:
        parts[nm] = w[:, off:off + wd]
        off += wd
    small = jnp.concatenate([parts[nm] for nm in MY_SMALL], axis=1)
    small = jnp.pad(small, ((0, 0), (0, 128 - small.shape[1])))
    return jnp.concatenate([parts[nm] for nm in MY_BIG] + [small], axis=1)


def _in_cols_from_mine(g):
    parts = {nm: g[:, i * 512:(i + 1) * 512] for i, nm in enumerate(MY_BIG)}
    off = N_BIG
    for nm in MY_SMALL:
        wd = dict(IN_COLS)[nm]
        parts[nm] = g[:, off:off + wd]
        off += wd
    return jnp.concatenate([parts[nm] for nm, _ in IN_COLS], axis=1)


def _pack_small(vals):
    rows = []
    nl = vals[SMALL[0]].shape[0]
    for l in range(nl):
        for nm in SMALL:
            v = vals[nm][l].reshape(-1)
            pad = (-v.shape[0]) % SMALL_ROWS
            rows.append(jnp.pad(v, (0, pad)).reshape(-1, 128))
    return jnp.concatenate(rows, axis=0)


def _unpack_small(packed, like):
    out = {nm: [] for nm in SMALL}
    row = 0
    nl = like[SMALL[0]].shape[0]
    for l in range(nl):
        for nm in SMALL:
            n = like[nm].shape[1]
            nr = -(-n // SMALL_ROWS) * 8
            out[nm].append(packed[row:row + nr].reshape(-1)[:n])
            row += nr
    return {nm: jnp.stack(v) for nm, v in out.items()}


def kernel(x, ffn1_norm, ffn1_w_in, ffn1_w_out, mix_norm, w_in, fox_q_norm, fox_k_norm, fox_f_bias, gdn_conv, gdn_a_log, gdn_dt_bias, gdn_out_norm, w_out, ffn2_norm, ffn2_w_in, ffn2_w_out, loss_target, m_ffn1_norm, m_ffn1_w_in, m_ffn1_w_out, m_mix_norm, m_w_in, m_fox_q_norm, m_fox_k_norm, m_fox_f_bias, m_gdn_conv, m_gdn_a_log, m_gdn_dt_bias, m_gdn_out_norm, m_w_out, m_ffn2_norm, m_ffn2_w_in, m_ffn2_w_out, v_ffn1_norm, v_ffn1_w_in, v_ffn1_w_out, v_mix_norm, v_w_in, v_fox_q_norm, v_fox_k_norm, v_fox_f_bias, v_gdn_conv, v_gdn_a_log, v_gdn_dt_bias, v_gdn_out_norm, v_w_out, v_ffn2_norm, v_ffn2_w_in, v_ffn2_w_out):
    wts = dict(ffn1_norm=ffn1_norm, ffn1_w_in=ffn1_w_in, ffn1_w_out=ffn1_w_out, mix_norm=mix_norm, w_in=w_in,
               fox_q_norm=fox_q_norm, fox_k_norm=fox_k_norm, fox_f_bias=fox_f_bias, gdn_conv=gdn_conv,
               gdn_a_log=gdn_a_log, gdn_dt_bias=gdn_dt_bias, gdn_out_norm=gdn_out_norm, w_out=w_out,
               ffn2_norm=ffn2_norm, ffn2_w_in=ffn2_w_in, ffn2_w_out=ffn2_w_out)
    mom = dict(ffn1_norm=m_ffn1_norm, ffn1_w_in=m_ffn1_w_in, ffn1_w_out=m_ffn1_w_out, mix_norm=m_mix_norm, w_in=m_w_in,
               fox_q_norm=m_fox_q_norm, fox_k_norm=m_fox_k_norm, fox_f_bias=m_fox_f_bias, gdn_conv=m_gdn_conv,
               gdn_a_log=m_gdn_a_log, gdn_dt_bias=m_gdn_dt_bias, gdn_out_norm=m_gdn_out_norm, w_out=m_w_out,
               ffn2_norm=m_ffn2_norm, ffn2_w_in=m_ffn2_w_in, ffn2_w_out=m_ffn2_w_out)
    var = dict(ffn1_norm=v_ffn1_norm, ffn1_w_in=v_ffn1_w_in, ffn1_w_out=v_ffn1_w_out, mix_norm=v_mix_norm, w_in=v_w_in,
               fox_q_norm=v_fox_q_norm, fox_k_norm=v_fox_k_norm, fox_f_bias=v_fox_f_bias, gdn_conv=v_gdn_conv,
               gdn_a_log=v_gdn_a_log, gdn_dt_bias=v_gdn_dt_bias, gdn_out_norm=v_gdn_out_norm, w_out=v_w_out,
               ffn2_norm=v_ffn2_norm, ffn2_w_in=v_ffn2_w_in, ffn2_w_out=v_ffn2_w_out)
    nb, seq, d = x.shape
    t = nb * seq
    depth = ffn1_norm.shape[0]

    stages = [(l, gi) for l in range(depth) for gi in range(len(GROUPS))]

    def shards_of(l, gi):
        return [wts[nm][l] if nm == "gdn_conv" else wts[nm][l].astype(BF16) for nm in GROUPS[gi]]

    def behind(nw, token):
        return nw if token is None else nw + token[0:1, 0:1]

    def small_params(l):
        return dict(
            n1=ffn1_norm[l][None], nmix=mix_norm[l][None], n2=ffn2_norm[l][None],
            qw=fox_q_norm[l][None], kw=fox_k_norm[l][None], onw=gdn_out_norm[l][None],
            gp=jnp.concatenate([
                jnp.concatenate([fox_f_bias[l], gdn_dt_bias[l], jnp.zeros((116,), F32)])[None],
                jnp.concatenate([jnp.zeros((8,), F32), gdn_a_log[l], jnp.zeros((116,), F32)])[None],
                jnp.zeros((6, 128), F32)], axis=0))

    h = x.reshape(t, d)
    landed = gather_two_level(shards_of(0, 0), "gather_0")
    saved = [dict(p=small_params(l)) for l in range(depth)]
    for k, (l, gi) in enumerate(stages):
        s, w, token = saved[l], landed, None
        p = s["p"]
        if k + 1 < len(stages):
            nl, ng = stages[k + 1]
            state, token = exchange_begin(shards_of(nl, ng), [False] * len(GROUPS[ng]), f"gather_{k + 1}", landed[0])
        if gi == 0:
            fb = w[0].shape[2]
            p["w1i"], p["w1o"] = w[0].reshape(2, 4, d, fb), w[1].reshape(4, fb, d)
            s["x0"] = h
            h, *s["ffn1"] = ffn_fwd(h, behind(p["n1"], token), p["w1i"], p["w1o"])
            s["x1"] = h
        elif gi == 1:
            p["wi"] = _in_cols_to_mine(w[0].transpose(1, 0, 2).reshape(d, -1))
            p["cw"] = w[1].transpose(1, 0, 2).reshape(CONV_W, -1)
            p["wo"] = w[2].reshape(d, d)
            proj, hn = inproj_fwd(h, behind(p["nmix"], token), p["wi"])
            gates = gates_fwd(proj, p["gp"], seq)
            yf, lse = attn_fwd(proj, gates, p["qw"], p["kw"], seq, tq=min(seq, ATTN_TQ_FWD))
            qh, kh, vh = gdn_pre_fwd(proj, p["cw"], seq)
            inv = gdn_inv(kh, gates, seq)
            yg, st = gdn_fwd(qh, kh, vh, gates, proj, p["onw"], inv, seq)
            h, ycat = outproj_fwd(h, yf, yg, p["wo"])
            s.update(x2=h, proj=proj, hn=hn, gates=gates, yf=yf, lse=lse, qh=qh, kh=kh, vh=vh, st=st, inv=inv, ycat=ycat)
        else:
            fb = w[0].shape[2]
            p["w2i"], p["w2o"] = w[0].reshape(2, 4, d, fb), w[1].reshape(4, fb, d)
            h, *s["ffn2"] = ffn_fwd(h, behind(p["n2"], token), p["w2i"], p["w2o"])
        if k + 1 < len(stages):
            landed = exchange_end(state, h)

    sq, dh = loss_head(h, loss_target.reshape(t, d))
    loss = lax.psum(0.5 * jnp.sum(sq) / d, ("x", "y", "c"))

    got = [None] * len(stages)
    pending, token = None, None
    gsmall = {nm: [None] * depth for nm in SMALL}
    for k in reversed(range(len(stages))):
        l, gi = stages[k]
        s = saved[l]
        p = s["p"]
        if gi != 1:
            nw, xin, wi_, wo_, nm_n, (xn, gu, hh) = (
                (p["n1"], s["x0"], p["w1i"], p["w1o"], "ffn1_norm", s["ffn1"]) if gi == 0 else
                (p["n2"], s["x2"], p["w2i"], p["w2o"], "ffn2_norm", s["ffn2"]))
            dh, dn, dgu, dyh = ffn_bwd(xin, dh, behind(nw, token), gu, wi_, wo_)
            g_in, g_out = wgrad_ffn_in(xn, dgu), wgrad_ffn_out(hh, dyh)
            send = [g_in.reshape(N_DEV, d, g_in.shape[3]), g_out.reshape(N_DEV, -1, d)]
            gsmall[nm_n][l] = dn[0]
        else:
            dyf, dyg, dyb = outproj_bwd(dh, p["wo"], token)
            g_wo = wgrad_2d(s["ycat"], dyb, 512, "wgrad_w_out")
            dq, dk, dv, dga, dqw, dkw = attn_bwd(s["proj"], s["gates"], p["qw"], p["kw"], s["yf"], s["lse"], dyf, seq,
                                                 tq=min(seq, ATTN_TQ_BWD))
            dqh, dkh, dvh, dgg, dgb, donw = gdn_bwd(s["qh"], s["kh"], s["vh"], s["gates"], s["proj"], p["onw"],
                                                     s["inv"], s["st"], dyg, seq)
            dxq, dxk, dxv, dwq, dwk, dwv = gdn_pre_bwd(s["proj"], p["cw"], dqh, dkh, dvh, seq)
            dsm, dgp = gates_bwd(s["proj"], p["gp"], dga, dgb, seq)
            dh, dnmix, dproj = inproj_bwd(s["x1"], dh, p["nmix"], p["wi"], [dq, dk, dv, dxq, dxk, dxv, dgg, dsm])
            g_wi = wgrad_2d(s["hn"], dproj, 512, "wgrad_w_in", F32)
            g_cw = jnp.concatenate([dwq, dwk, dwv], axis=1)
            send = [_in_cols_from_mine(g_wi).reshape(d, N_DEV, -1).transpose(1, 0, 2),
                    g_cw.reshape(CONV_W, N_DEV, -1).transpose(1, 0, 2), g_wo.reshape(N_DEV, -1, d)]
            for nm, val in (("mix_norm", dnmix[0]), ("fox_q_norm", dqw[0]), ("fox_k_norm", dkw[0]),
                            ("fox_f_bias", dgp[0, 0:8]), ("gdn_a_log", dgp[1, 8:12]), ("gdn_dt_bias", dgp[0, 8:12]),
                            ("gdn_out_norm", donw[0])):
                gsmall[nm][l] = val
        flags = [True] * len(send)
        if k == 0:
            send.append(_pack_small({nm: jnp.stack(v) for nm, v in gsmall.items()}))
            flags.append(False)
        prev = dh
        if pending is not None:
            got[pending[1]] = exchange_end(pending[0], dh)
            prev = got[pending[1]][0]
        state, token = exchange_begin(send, flags, f"exchange_grads_{k}", prev)
        pending = (state, k)
    grad_x = dh.reshape(nb, seq, d)

    res = {}

    def update_stage(k, slots, after):
        l, gi = stages[k]
        for i, nm in enumerate(GROUPS[gi]):
            r, c = wts[nm].shape[1:]
            res[nm] = adamw_reduce(slots[i].reshape(N_DEV, r, c), wts[nm], mom[nm], var[nm], l, f"adamw_{nm}_{l}",
                                   after, res.get(nm))
            if after is not None:
                after = res[nm][0]
        return after

    last = token
    for k in range(1, len(stages)):
        last = update_stage(k, got[k], last)
    got[0] = exchange_end(pending[0], last)
    update_stage(0, got[0], None)
    small_like = {nm: wts[nm] for nm in SMALL}
    sm = adamw_reduce(got[0][-1], _pack_small(small_like)[None], _pack_small({nm: mom[nm] for nm in SMALL})[None],
                      _pack_small({nm: var[nm] for nm in SMALL})[None], 0, "adamw_small")
    sm = [_unpack_small(a[0], small_like) for a in sm]
    for nm in SMALL:
        res[nm] = [sm[j][nm] for j in range(4)]
    return (loss, grad_x, *[res[nm][0] for nm in WEIGHTS], *[res[nm][1] for nm in WEIGHTS],
            *[res[nm][2] for nm in WEIGHTS], *[res[nm][3] for nm in WEIGHTS])
```

```python
import jax
import jax.numpy as jnp
from jax import lax
from jax.experimental import pallas as pl
from jax.experimental.pallas import tpu as pltpu

F32 = jnp.float32
BF16 = jnp.bfloat16
EPS = 1e-6
N_DEV = 8
MESH = pl.DeviceIdType.MESH
HIGHEST = lax.Precision.HIGHEST
VMEM_LIMIT = 56 * 1024 * 1024

FOX_HEADS, FOX_DH = 8, 64
GDN_HEADS, GDN_DH = 4, 128
CHUNK = 64
CONV_W = 4

ADAM_LR, ADAM_B1, ADAM_B2, ADAM_EPS, ADAM_WD, ADAM_STEP = 0.001, 0.9, 0.999, 1e-08, 0.01, 10


def _cp(*sem):
    return pltpu.CompilerParams(dimension_semantics=sem, vmem_limit_bytes=VMEM_LIMIT)


def _dot(a, b):
    return jnp.dot(a, b, preferred_element_type=F32)


def _dot_nt(a, b):
    return lax.dot_general(a, b, (((1,), (1,)), ((), ())), preferred_element_type=F32)


def _dot_tn(a, b):
    return lax.dot_general(a, b, (((0,), (0,)), ((), ())), preferred_element_type=F32)


def _rstd(xf):
    return lax.rsqrt(jnp.mean(xf * xf, axis=-1, keepdims=True) + EPS)


def _rms_bwd(xf, r, dyn):
    return r * dyn - xf * (r * r * r) * jnp.mean(dyn * xf, axis=-1, keepdims=True)


def ffn_fwd(x, nw, w_in, w_out, tm=1024, rc=1024):
    t, d = x.shape
    nj, fb = w_out.shape[0], w_out.shape[1]
    tm = min(tm, t)
    rc = min(rc, tm)

    def body(x_ref, nw_ref, wi_ref, wo_ref, o_ref, xn_ref, gu_ref, h_ref, acc_ref):
        j = pl.program_id(1)

        @pl.when(j == 0)
        def _():
            xf = x_ref[...]
            xn_ref[...] = (xf * _rstd(xf) * nw_ref[...]).astype(BF16)
            acc_ref[...] = jnp.zeros_like(acc_ref)

        rows = [slice(c * rc, (c + 1) * rc) for c in range(tm // rc)]
        gs = [_dot(xn_ref[r, :], wi_ref[0]) for r in rows]
        us = [_dot(xn_ref[r, :], wi_ref[1]) for r in rows]
        hs = []
        for g, u, r in zip(gs, us, rows):
            sg = jax.nn.sigmoid(g)
            silu = g * sg
            h = (silu * u).astype(BF16)
            gu_ref[0, r, :] = (u * (sg * (1.0 + g * (1.0 - sg)))).astype(BF16)
            gu_ref[1, r, :] = silu.astype(BF16)
            h_ref[r, :] = h
            hs.append(h)
        for h, r in zip(hs, rows):
            acc_ref[r, :] += _dot(h, wo_ref[...])

        @pl.when(j == nj - 1)
        def _():
            o_ref[...] = x_ref[...] + 0.5 * acc_ref[...]

    return pl.pallas_call(
        body, grid=(t // tm, nj),
        in_specs=[pl.BlockSpec((tm, d), lambda i, j: (i, 0)),
                  pl.BlockSpec((1, d), lambda i, j: (0, 0)),
                  pl.BlockSpec((2, None, d, fb), lambda i, j: (0, j, 0, 0)),
                  pl.BlockSpec((None, fb, d), lambda i, j: (j, 0, 0))],
        out_specs=[pl.BlockSpec((tm, d), lambda i, j: (i, 0)),
                   pl.BlockSpec((tm, d), lambda i, j: (i, 0)),
                   pl.BlockSpec((2, None, tm, fb), lambda i, j: (0, j, i, 0)),
                   pl.BlockSpec((None, tm, fb), lambda i, j: (j, i, 0))],
        out_shape=[jax.ShapeDtypeStruct((t, d), F32), jax.ShapeDtypeStruct((t, d), BF16),
                   jax.ShapeDtypeStruct((2, nj, t, fb), BF16), jax.ShapeDtypeStruct((nj, t, fb), BF16)],
        scratch_shapes=[pltpu.VMEM((tm, d), F32)],
        compiler_params=_cp("parallel", "arbitrary"), name="ffn_fwd")(x, nw, w_in, w_out)


def ffn_bwd(x, dy, nw, gu, w_in, w_out, tm=512, rc=256):
    t, d = x.shape
    nj, fb = w_out.shape[0], w_out.shape[1]
    tm = min(tm, t)
    rc = min(rc, tm)

    def body(x_ref, dy_ref, nw_ref, gu_ref, wi_ref, wo_ref,
             dx_ref, dnw_ref, dgu_ref, dyh_ref, acc_ref):
        i, j = pl.program_id(0), pl.program_id(1)

        @pl.when(j == 0)
        def _():
            dyh_ref[...] = (0.5 * dy_ref[...]).astype(BF16)
            acc_ref[...] = jnp.zeros_like(acc_ref)

        @pl.when((i == 0) & (j == 0))
        def _():
            dnw_ref[...] = jnp.zeros_like(dnw_ref)

        rows = [slice(c * rc, (c + 1) * rc) for c in range(tm // rc)]
        dhs = [_dot_nt(dyh_ref[r, :], wo_ref[...]) for r in rows]
        dgs = [(dh * gu_ref[0, r, :].astype(F32)).astype(BF16) for dh, r in zip(dhs, rows)]
        dus = [(dh * gu_ref[1, r, :].astype(F32)).astype(BF16) for dh, r in zip(dhs, rows)]
        for dg, du, r in zip(dgs, dus, rows):
            dgu_ref[0, r, :] = dg
            dgu_ref[1, r, :] = du
        for dg, du, r in zip(dgs, dus, rows):
            acc_ref[r, :] += _dot_nt(dg, wi_ref[0]) + _dot_nt(du, wi_ref[1])

        @pl.when(j == nj - 1)
        def _():
            xf = x_ref[...]
            r = _rstd(xf)
            dxn = acc_ref[...]
            dnw_ref[...] += jnp.sum(dxn * xf * r, axis=0, keepdims=True)
            dx_ref[...] = _rms_bwd(xf, r, dxn * nw_ref[...]) + dy_ref[...]

    return pl.pallas_call(
        body, grid=(t // tm, nj),
        in_specs=[pl.BlockSpec((tm, d), lambda i, j: (i, 0)),
                  pl.BlockSpec((tm, d), lambda i, j: (i, 0)),
                  pl.BlockSpec((1, d), lambda i, j: (0, 0)),
                  pl.BlockSpec((2, None, tm, fb), lambda i, j: (0, j, i, 0)),
                  pl.BlockSpec((2, None, d, fb), lambda i, j: (0, j, 0, 0)),
                  pl.BlockSpec((None, fb, d), lambda i, j: (j, 0, 0))],
        out_specs=[pl.BlockSpec((tm, d), lambda i, j: (i, 0)),
                   pl.BlockSpec((1, d), lambda i, j: (0, 0)),
                   pl.BlockSpec((2, None, tm, fb), lambda i, j: (0, j, i, 0)),
                   pl.BlockSpec((tm, d), lambda i, j: (i, 0))],
        out_shape=[jax.ShapeDtypeStruct((t, d), F32),
                   jax.ShapeDtypeStruct((1, d), F32),
                   jax.ShapeDtypeStruct((2, nj, t, fb), BF16),
                   jax.ShapeDtypeStruct((t, d), BF16)],
        scratch_shapes=[pltpu.VMEM((tm, d), F32)],
        compiler_params=_cp("arbitrary", "arbitrary"), name="ffn_bwd")(x, dy, nw, gu, w_in, w_out)


def _wgrad_call(a, b, a_spec, b_spec, out_shape, out_spec, grid, name, out_dtype=BF16):
    last = len(grid) - 1
    acc_shape = tuple(s for s in out_spec.block_shape if s is not None)

    def body(a_ref, b_ref, o_ref, acc_ref):
        @pl.when(pl.program_id(last) == 0)
        def _():
            acc_ref[...] = jnp.zeros_like(acc_ref)

        if len(acc_shape) == 3:
            shared_a = a_ref[...] if len(a_ref.shape) == 2 else None
            shared_b = b_ref[...] if len(b_ref.shape) == 2 else None
            for s in range(acc_shape[0]):
                acc_ref[s] += _dot_tn(a_ref[s] if shared_a is None else shared_a,
                                      b_ref[s] if shared_b is None else shared_b)
        else:
            acc_ref[...] += _dot_tn(a_ref[...], b_ref[...])

        @pl.when(pl.program_id(last) == grid[last] - 1)
        def _():
            o_ref[...] = acc_ref[...].astype(o_ref.dtype)

    sem = ("parallel",) * last + ("arbitrary",)
    return pl.pallas_call(body, grid=grid, in_specs=[a_spec, b_spec], out_specs=out_spec,
                          out_shape=jax.ShapeDtypeStruct(out_shape, out_dtype),
                          scratch_shapes=[pltpu.VMEM(acc_shape, F32)],
                          compiler_params=_cp(*sem), name=name)(a, b)


WGRAD_TM = 2048
WGRAD_TM_MIXER = 1024


def wgrad_ffn_in(xn, dgu, tm=WGRAD_TM):
    t, d = xn.shape
    _, nj, _, fb = dgu.shape
    tm = min(tm, t)
    return _wgrad_call(xn, dgu,
                       pl.BlockSpec((tm, d), lambda j, k: (k, 0)),
                       pl.BlockSpec((2, None, tm, fb), lambda j, k: (0, j, k, 0)),
                       (2, nj, d, fb), pl.BlockSpec((2, None, d, fb), lambda j, k: (0, j, 0, 0)),
                       (nj, t // tm), "wgrad_ffn_in")


def wgrad_ffn_out(h, dyh, tm=WGRAD_TM):
    nj, t, fb = h.shape
    d = dyh.shape[1]
    tm = min(tm, t)
    return _wgrad_call(h, dyh,
                       pl.BlockSpec((nj, tm, fb), lambda k: (0, k, 0)),
                       pl.BlockSpec((tm, d), lambda k: (k, 0)),
                       (nj, fb, d), pl.BlockSpec((nj, fb, d), lambda k: (0, 0, 0)),
                       (t // tm,), "wgrad_ffn_out")


def wgrad_2d(a, b, tk, name, out_dtype=BF16, tm=WGRAD_TM_MIXER):
    t, k = a.shape
    n = b.shape[1]
    tm = min(tm, t)
    return _wgrad_call(a, b,
                       pl.BlockSpec((tm, tk), lambda c, s: (s, c)),
                       pl.BlockSpec((tm, n), lambda c, s: (s, 0)),
                       (k, n), pl.BlockSpec((tk, n), lambda c, s: (c, 0)),
                       (k // tk, t // tm), name, out_dtype)


N_BIG = 7 * 512
N_PROJ = N_BIG + 128
COL_SMALL = N_BIG // 128


def inproj_fwd(x, nw, w, tm=512):
    t, d = x.shape
    n = w.shape[1]

    def body(x_ref, nw_ref, w_ref, p_ref, hn_ref):
        xf = x_ref[...]
        hn = (xf * _rstd(xf) * nw_ref[...]).astype(BF16)
        hn_ref[...] = hn
        p_ref[...] = _dot(hn, w_ref[...])

    return pl.pallas_call(
        body, grid=(t // tm,),
        in_specs=[pl.BlockSpec((tm, d), lambda i: (i, 0)), pl.BlockSpec((1, d), lambda i: (0, 0)),
                  pl.BlockSpec((d, n), lambda i: (0, 0))],
        out_specs=[pl.BlockSpec((tm, n), lambda i: (i, 0)), pl.BlockSpec((tm, d), lambda i: (i, 0))],
        out_shape=[jax.ShapeDtypeStruct((t, n), F32), jax.ShapeDtypeStruct((t, d), BF16)],
        compiler_params=_cp("parallel"), name="inproj_fwd")(x, nw, w)


def inproj_bwd(x, dres, nw, w, dparts, tm=512):
    t, d = x.shape
    n = w.shape[1]
    widths = [p.shape[1] for p in dparts]
    assert sum(widths) == n

    def body(x_ref, dres_ref, nw_ref, w_ref, *rest):
        part_refs, (dx_ref, dnw_ref, dp_ref) = rest[:len(widths)], rest[len(widths):]

        @pl.when(pl.program_id(0) == 0)
        def _():
            dnw_ref[...] = jnp.zeros_like(dnw_ref)

        half = tm // 2
        rows = [slice(0, half), slice(half, tm)]
        dps = [jnp.concatenate([r[rw, :].astype(BF16) for r in part_refs], axis=1) for rw in rows]
        for dp, rw in zip(dps, rows):
            dp_ref[rw, :] = dp
        dhns = [_dot_nt(dp, w_ref[...]) for dp in dps]
        for dhn, rw in zip(dhns, rows):
            xf = x_ref[rw, :]
            r = _rstd(xf)
            dnw_ref[...] += jnp.sum(dhn * xf * r, axis=0, keepdims=True)
            dx_ref[rw, :] = _rms_bwd(xf, r, dhn * nw_ref[...]) + dres_ref[rw, :]

    return pl.pallas_call(
        body, grid=(t // tm,),
        in_specs=[pl.BlockSpec((tm, d), lambda i: (i, 0)), pl.BlockSpec((tm, d), lambda i: (i, 0)),
                  pl.BlockSpec((1, d), lambda i: (0, 0)), pl.BlockSpec((d, n), lambda i: (0, 0))]
                 + [pl.BlockSpec((tm, wd), lambda i: (i, 0)) for wd in widths],
        out_specs=[pl.BlockSpec((tm, d), lambda i: (i, 0)), pl.BlockSpec((1, d), lambda i: (0, 0)),
                   pl.BlockSpec((tm, n), lambda i: (i, 0))],
        out_shape=[jax.ShapeDtypeStruct((t, d), F32), jax.ShapeDtypeStruct((1, d), F32),
                   jax.ShapeDtypeStruct((t, n), BF16)],
        compiler_params=_cp("arbitrary"), name="inproj_bwd")(x, dres, nw, w, *dparts)


def outproj_fwd(x, yf, yg, w, tm=1024):
    t, d = x.shape
    hw = yf.shape[1]
    tm = min(tm, t)

    def body(x_ref, yf_ref, yg_ref, w_ref, o_ref, y_ref):
        y = jnp.concatenate([yf_ref[...], yg_ref[...]], axis=1).astype(BF16)
        y_ref[...] = y
        o_ref[...] = x_ref[...] + _dot(y, w_ref[...])

    return pl.pallas_call(
        body, grid=(t // tm,),
        in_specs=[pl.BlockSpec((tm, d), lambda i: (i, 0)), pl.BlockSpec((tm, hw), lambda i: (i, 0)),
                  pl.BlockSpec((tm, hw), lambda i: (i, 0)), pl.BlockSpec((2 * hw, d), lambda i: (0, 0))],
        out_specs=[pl.BlockSpec((tm, d), lambda i: (i, 0)), pl.BlockSpec((tm, 2 * hw), lambda i: (i, 0))],
        out_shape=[jax.ShapeDtypeStruct((t, d), F32), jax.ShapeDtypeStruct((t, 2 * hw), BF16)],
        compiler_params=_cp("parallel"), name="outproj_fwd")(x, yf, yg, w)


def outproj_bwd(dy, w, after=None, tm=1024):
    t, d = dy.shape
    hw = w.shape[0] // 2
    tm = min(tm, t)
    extra = [] if after is None else [after]

    def body(dy_ref, w_ref, *rest):
        df_ref, dg_ref, dyb_ref = rest[-3:]
        dyb = dy_ref[...].astype(BF16)
        dyb_ref[...] = dyb
        dyy = _dot_nt(dyb, w_ref[...])
        df_ref[...] = dyy[:, :hw]
        dg_ref[...] = dyy[:, hw:]

    return pl.pallas_call(
        body, grid=(t // tm,),
        in_specs=[pl.BlockSpec((tm, d), lambda i: (i, 0)), pl.BlockSpec((2 * hw, d), lambda i: (0, 0))]
                 + [pl.BlockSpec(memory_space=pl.ANY)] * len(extra),
        out_specs=[pl.BlockSpec((tm, hw), lambda i: (i, 0)), pl.BlockSpec((tm, hw), lambda i: (i, 0)),
                   pl.BlockSpec((tm, d), lambda i: (i, 0))],
        out_shape=[jax.ShapeDtypeStruct((t, hw), F32), jax.ShapeDtypeStruct((t, hw), F32),
                   jax.ShapeDtypeStruct((t, d), BF16)],
        compiler_params=_cp("parallel"), name="outproj_bwd")(dy, w, *extra)


def _lane(shape):
    return lax.broadcasted_iota(jnp.int32, shape, 1)


def _row(shape):
    return lax.broadcasted_iota(jnp.int32, shape, 0)


def _gate_terms(val, gp_ref):
    z = val + gp_ref[0:1, :]
    sp = jnp.log(1.0 + jnp.exp(-jnp.abs(z)))
    return z, sp


def gates_fwd(proj, gp, seq, ts=512):
    t = proj.shape[0]
    nb, ns = t // seq, seq // ts

    def body(v_ref, gp_ref, o_ref, carry_ref):
        @pl.when(pl.program_id(1) == 0)
        def _():
            carry_ref[...] = jnp.zeros_like(carry_ref)

        z, sp = _gate_terms(v_ref[...], gp_ref)
        logsig = jnp.minimum(z, 0.0) - sp
        tri = (_row((ts, ts)) >= _lane((ts, ts))).astype(F32)
        cum = jnp.dot(tri, logsig, precision=HIGHEST, preferred_element_type=F32) + carry_ref[0:1, :]
        carry_ref[0:1, :] = cum[ts - 1:ts, :]
        g = -jnp.exp(gp_ref[1:2, :]) * (jnp.maximum(z, 0.0) + sp)
        beta = jax.nn.sigmoid(z)
        lane = _lane((ts, 128))
        o_ref[...] = jnp.where(lane < 8, cum, jnp.where(lane < 12, g, jnp.where(lane < 16, beta, 0.0)))

    return pl.pallas_call(
        body, grid=(nb, ns),
        in_specs=[pl.BlockSpec((ts, 128), lambda b, s: (b * ns + s, COL_SMALL)),
                  pl.BlockSpec((8, 128), lambda b, s: (0, 0))],
        out_specs=pl.BlockSpec((ts, 128), lambda b, s: (b * ns + s, 0)),
        out_shape=jax.ShapeDtypeStruct((t, 128), F32),
        scratch_shapes=[pltpu.VMEM((8, 128), F32)],
        compiler_params=_cp("parallel", "arbitrary"), name="gates_fwd")(proj, gp)


def gates_bwd(proj, gp, dga, dgb, seq, ts=512):
    t = proj.shape[0]
    nb, ns = t // seq, seq // ts

    def body(v_ref, gp_ref, da_ref, db_ref, ds_ref, dgp_ref, carry_ref):
        @pl.when(pl.program_id(1) == 0)
        def _():
            carry_ref[...] = jnp.zeros_like(carry_ref)

        @pl.when((pl.program_id(0) == 0) & (pl.program_id(1) == 0))
        def _():
            dgp_ref[...] = jnp.zeros_like(dgp_ref)

        lane = _lane((ts, 128))
        dgate = jnp.where(lane < 8, da_ref[...], jnp.where(lane < 16, db_ref[...], 0.0))
        z, sp = _gate_terms(v_ref[...], gp_ref)
        triu = (_row((ts, ts)) <= _lane((ts, ts))).astype(F32)
        dlog = jnp.dot(triu, dgate, precision=HIGHEST, preferred_element_type=F32) + carry_ref[0:1, :]
        carry_ref[0:1, :] = dlog[0:1, :]
        sig = jax.nn.sigmoid(z)
        nea = -jnp.exp(gp_ref[1:2, :])
        g = nea * (jnp.maximum(z, 0.0) + sp)
        dz = jnp.where(lane < 8, dlog * (1.0 - sig),
                       jnp.where(lane < 12, dgate * nea * sig, dgate * sig * (1.0 - sig)))
        dz = jnp.where(lane < 16, dz, 0.0)
        ds_ref[...] = dz.astype(BF16)
        dgp_ref[0:1, :] += jnp.where(lane[0:1] < 12, jnp.sum(dz, axis=0, keepdims=True), 0.0)
        dgp_ref[1:2, :] += jnp.where((lane[0:1] >= 8) & (lane[0:1] < 12), jnp.sum(dgate * g, axis=0, keepdims=True), 0.0)

    rev = lambda b, s: (b * ns + (ns - 1 - s), 0)
    return pl.pallas_call(
        body, grid=(nb, ns),
        in_specs=[pl.BlockSpec((ts, 128), lambda b, s: (b * ns + (ns - 1 - s), COL_SMALL)),
                  pl.BlockSpec((8, 128), lambda b, s: (0, 0)),
                  pl.BlockSpec((ts, 128), rev), pl.BlockSpec((ts, 128), rev)],
        out_specs=[pl.BlockSpec((ts, 128), rev), pl.BlockSpec((8, 128), lambda b, s: (0, 0))],
        out_shape=[jax.ShapeDtypeStruct((t, 128), BF16), jax.ShapeDtypeStruct((8, 128), F32)],
        scratch_shapes=[pltpu.VMEM((8, 128), F32)],
        compiler_params=_cp("arbitrary", "arbitrary"), name="gates_bwd")(proj, gp, dga, dgb)


NEG = -1e30
ATTN_TQ_FWD = 1024
ATTN_TQ_BWD = 512


def _pick_lane(tile, idx):
    return jnp.sum(jnp.where(_lane(tile.shape) == idx, tile, 0.0), axis=1, keepdims=True)


def _row_to_col(row, n):
    return jnp.sum(jnp.where(_row((n, n)) == _lane((n, n)), row, 0.0), axis=1, keepdims=True)


def _rows(i, n):
    return pl.ds(pl.multiple_of(i * n, n), n)


LOG2E = 1.4426950408889634
LN2 = 0.6931471805599453


def _split_dot(x, mat, passes):
    total, rest = None, x
    for _ in range(passes):
        part = rest.astype(BF16)
        rest = rest - part.astype(F32)
        total = _dot(part, mat) if total is None else total + _dot(part, mat)
    return total


def _pair_mats(p, dh):
    r, l = _row((128, 128)), _lane((128, 128))
    same = (r < dh) == (l < dh)
    upper = (l >= dh).astype(jnp.int32)
    as_bf16 = lambda m: m.astype(BF16)
    return dict(own=as_bf16(same), other=as_bf16(jnp.logical_not(same)), pick_other=as_bf16(r == 2 * p + 1 - upper),
                swap=as_bf16(((r == 0) & (l >= dh)) | ((r == dh) & (l < dh))))


def _pair_rstd(x2, sel, dh):
    return lax.rsqrt(_split_dot(x2 * x2, sel["own"], 2) * (1.0 / dh) + EPS)


def _pair_aug(cols, n, dh):
    lane = _lane((n, 128))
    li = jnp.where(lane >= dh, lane - dh, lane)
    out = jnp.zeros((n, 128), F32)
    for i, c in enumerate(cols):
        out = jnp.where(li == i, c, out)
    return out


def _split3(x):
    hi = x.astype(BF16).astype(F32)
    mid = (x - hi).astype(BF16).astype(F32)
    return [hi, mid, (x - hi - mid).astype(BF16).astype(F32)]


def _once(shape, index_map):
    return pl.BlockSpec(shape, index_map, pipeline_mode=pl.Buffered(1))


def attn_fwd(proj, gates, qw, kw, seq, tq=256):
    t = proj.shape[0]
    nb, nq, dh = t // seq, seq // tq, FOX_DH
    scale = dh ** -0.5

    def body(q_ref, k_ref, v_ref, g_ref, qw_ref, kw_ref, y_ref, lse_ref, qs, ks, vs):
        p = pl.program_id(1)
        heads = range(2)
        low = _lane((tq, 128)) < dh
        sel = _pair_mats(p, dh)

        def prep(i, _):
            r = _rows(i, tq)
            q2, k2 = q_ref[r, :], k_ref[r, :]
            cc = _split_dot(g_ref[r, :], sel["pick_other"], 3) * LOG2E
            qn = q2 * _pair_rstd(q2, sel, dh) * qw_ref[...] * (scale * LOG2E)
            kn = k2 * _pair_rstd(k2, sel, dh) * kw_ref[...]
            qx = _pair_aug(_split3(cc) + [1.0, 1.0, 1.0], tq, dh)
            kx = _pair_aug([1.0, 1.0, 1.0] + _split3(-cc), tq, dh)
            for hh in heads:
                own = low if hh == 0 else jnp.logical_not(low)
                qs[hh, r, :] = jnp.where(own, qn, qx).astype(BF16)
                ks[hh, r, :] = jnp.where(own, kn, kx).astype(BF16)
            vs[r, :] = v_ref[r, :].astype(BF16)
            return 0

        lax.fori_loop(0, nq, prep, 0)

        def q_tile(i, _):
            r = _rows(i, tq)
            qt = [qs[hh, r, :] for hh in heads]

            def kv_step(j, carry, masked):
                kr = _rows(j, tq)
                vt = vs[kr, :]
                out = []
                for hh in heads:
                    m, l, acc = carry[hh]
                    s = _dot_nt(qt[hh], ks[hh, kr, :])
                    if masked:
                        s = jnp.where(_row((tq, tq)) >= _lane((tq, tq)), s, NEG)
                    m_new = jnp.maximum(m, jnp.max(s, axis=1, keepdims=True))
                    pe = jnp.exp2(s - m_new)
                    a = jnp.exp2(m - m_new)
                    out.append((m_new, a * l + jnp.sum(pe, axis=1, keepdims=True), a * acc + _dot(pe.astype(BF16), vt)))
                return tuple(out)

            one = (jnp.full((tq, 1), NEG, F32), jnp.zeros((tq, 1), F32), jnp.zeros((tq, 128), F32))
            carry = lax.fori_loop(0, i, lambda j, c: kv_step(j, c, False), (one, one))
            (m0, l0, acc0), (m1, l1, acc1) = kv_step(i, carry, True)
            y_ref[r, :] = jnp.where(low, acc0 / l0, acc1 / l1)
            lse_ref[r, :] = jnp.where(low, m0 + jnp.log2(l0), m1 + jnp.log2(l1))
            return 0

        lax.fori_loop(0, nq, q_tile, 0)

    blk = lambda off: _once((seq, 128), lambda b, p: (b, off + p))
    return pl.pallas_call(
        body, grid=(nb, 4),
        in_specs=[blk(0), blk(4), blk(8), _once((seq, 128), lambda b, p: (b, 0)),
                  pl.BlockSpec((1, 128), lambda b, p: (0, 0)), pl.BlockSpec((1, 128), lambda b, p: (0, 0))],
        out_specs=[pl.BlockSpec((seq, 128), lambda b, p: (b, p)), pl.BlockSpec((seq, 128), lambda b, p: (b, p))],
        out_shape=[jax.ShapeDtypeStruct((t, 512), F32), jax.ShapeDtypeStruct((t, 512), F32)],
        scratch_shapes=[pltpu.VMEM((2, seq, 128), BF16), pltpu.VMEM((2, seq, 128), BF16), pltpu.VMEM((seq, 128), BF16)],
        compiler_params=_cp("parallel", "arbitrary"),
        name="attn_fwd")(proj, proj, proj, gates, jnp.tile(qw, (1, 2)), jnp.tile(kw, (1, 2)))


def attn_bwd(proj, gates, qw, kw, y, lse, dy, seq, tq=256):
    t = proj.shape[0]
    nb, nq, dh = t // seq, seq // tq, FOX_DH
    scale = dh ** -0.5

    def body(q_ref, k_ref, v_ref, g_ref, qw_ref, kw_ref, y_ref, lse_ref, dy_ref,
             dq_ref, dk_ref, dv_ref, dg_ref, dqw_ref, dkw_ref,
             qs, ks, vs, dos, dsrow, dqa, dka):
        b, p = pl.program_id(0), pl.program_id(1)

        @pl.when((b == 0) & (p == 0))
        def _():
            dqw_ref[...] = jnp.zeros_like(dqw_ref)
            dkw_ref[...] = jnp.zeros_like(dkw_ref)

        @pl.when(p == 0)
        def _():
            dg_ref[...] = jnp.zeros_like(dg_ref)

        heads = range(2)
        low = _lane((tq, 128)) < dh
        sel = _pair_mats(p, dh)

        def prep(i, _):
            r = _rows(i, tq)
            q2, k2, dy2 = q_ref[r, :], k_ref[r, :], dy_ref[r, :]
            cc = _split_dot(g_ref[r, :], sel["pick_other"], 3) * LOG2E
            lse_x = _split_dot(lse_ref[r, :], sel["swap"], 3)
            delta_x = _split_dot(dy2 * y_ref[r, :], sel["other"], 2)
            qn = q2 * _pair_rstd(q2, sel, dh) * qw_ref[...] * (scale * LOG2E)
            kn = k2 * _pair_rstd(k2, sel, dh) * kw_ref[...]
            qx = _pair_aug(_split3(cc) + [1.0, 1.0, 1.0] + _split3(-lse_x), tq, dh)
            kx = _pair_aug([1.0, 1.0, 1.0] + _split3(-cc) + [1.0, 1.0, 1.0], tq, dh)
            vx = _pair_aug([1.0, 1.0, 1.0], tq, dh)
            dx = _pair_aug(_split3(-delta_x), tq, dh)
            for hh in heads:
                own = low if hh == 0 else jnp.logical_not(low)
                qs[hh, r, :] = jnp.where(own, qn, qx).astype(BF16)
                ks[hh, r, :] = jnp.where(own, kn, kx).astype(BF16)
                vs[hh, r, :] = jnp.where(own, v_ref[r, :], vx).astype(BF16)
                dos[hh, r, :] = jnp.where(own, dy2, dx).astype(BF16)
                dsrow[hh, r, :] = jnp.zeros((tq, 1), F32)
                dqa[hh, r, :] = jnp.zeros((tq, 128), F32)
            return 0

        lax.fori_loop(0, nq, prep, 0)

        def kv_tile(j, _):
            kr = _rows(j, tq)
            kt = [ks[hh, kr, :] for hh in heads]
            vt = [vs[hh, kr, :] for hh in heads]

            def q_step(i, carry, masked):
                r = _rows(i, tq)
                out = []
                for hh in heads:
                    dk, dv, dcr = carry[hh]
                    qt, dot = qs[hh, r, :], dos[hh, r, :]
                    s = _dot_nt(qt, kt[hh])
                    if masked:
                        s = jnp.where(_row((tq, tq)) >= _lane((tq, tq)), s, NEG)
                    pe = jnp.exp2(s)
                    ds = pe * _dot_nt(dot, vt[hh])
                    dsb = ds.astype(BF16)
                    dqa[hh, r, :] += _dot(dsb, kt[hh])
                    dsrow[hh, r, :] += jnp.sum(ds, axis=1, keepdims=True)
                    out.append((dk + _dot_tn(dsb, qt), dv + _dot_tn(pe.astype(BF16), dot),
                                dcr - jnp.sum(ds, axis=0, keepdims=True)))
                return tuple(out)

            one = (jnp.zeros((tq, 128), F32), jnp.zeros((tq, 128), F32), jnp.zeros((1, tq), F32))
            carry = q_step(j, (one, one), True)
            (dk0, dv0, dcr0), (dk1, dv1, dcr1) = lax.fori_loop(j + 1, nq, lambda i, c: q_step(i, c, False), carry)
            dka[kr, :] = jnp.where(low, dk0, dk1)
            dv_ref[kr, :] = jnp.where(low, dv0, dv1).astype(BF16)
            lane = _lane((tq, 128))
            dg_ref[kr, :] = jnp.where(lane == 2 * p, _row_to_col(dcr0, tq),
                                      jnp.where(lane == 2 * p + 1, _row_to_col(dcr1, tq), dg_ref[kr, :]))
            return 0

        lax.fori_loop(0, nq, kv_tile, 0)

        def post(i, _):
            r = _rows(i, tq)
            q2, k2 = q_ref[r, :], k_ref[r, :]
            rq, rk = _pair_rstd(q2, sel, dh), _pair_rstd(k2, sel, dh)
            dqn = jnp.where(low, dqa[0, r, :], dqa[1, r, :]) * scale
            dkn = dka[r, :] * LN2
            dqw_ref[...] += jnp.sum(dqn * q2 * rq, axis=0, keepdims=True)
            dkw_ref[...] += jnp.sum(dkn * k2 * rk, axis=0, keepdims=True)
            for x2, rr, dyn, o_ref in ((q2, rq, dqn * qw_ref[...], dq_ref), (k2, rk, dkn * kw_ref[...], dk_ref)):
                mean = _split_dot(dyn * x2, sel["own"], 2) * (1.0 / dh)
                o_ref[r, :] = (rr * dyn - x2 * (rr * rr * rr) * mean).astype(BF16)
            lane = _lane((tq, 128))
            dg_ref[r, :] += jnp.where(lane == 2 * p, dsrow[0, r, :], jnp.where(lane == 2 * p + 1, dsrow[1, r, :], 0.0))
            return 0

        lax.fori_loop(0, nq, post, 0)

    blk = lambda off: _once((seq, 128), lambda b, p: (b, off + p))
    own = lambda: _once((seq, 128), lambda b, p: (b, p))
    vec = lambda: pl.BlockSpec((1, 128), lambda b, p: (0, 0))
    res = pl.pallas_call(
        body, grid=(nb, 4),
        in_specs=[blk(0), blk(4), blk(8), _once((seq, 128), lambda b, p: (b, 0)), vec(), vec(), own(), own(), own()],
        out_specs=[own(), own(), own(), _once((seq, 128), lambda b, p: (b, 0)), vec(), vec()],
        out_shape=[jax.ShapeDtypeStruct((t, 512), BF16)] * 3
                  + [jax.ShapeDtypeStruct((t, 128), F32), jax.ShapeDtypeStruct((1, 128), F32), jax.ShapeDtypeStruct((1, 128), F32)],
        scratch_shapes=[pltpu.VMEM((2, seq, 128), BF16)] * 4
                       + [pltpu.VMEM((2, seq, 1), F32), pltpu.VMEM((2, seq, 128), F32), pltpu.VMEM((seq, 128), F32)],
        compiler_params=_cp("arbitrary", "arbitrary"),
        name="attn_bwd")(proj, proj, proj, gates, jnp.tile(qw, (1, 2)), jnp.tile(kw, (1, 2)), y, lse, dy)
    return list(res[:4]) + [res[4][:, :dh] + res[4][:, dh:], res[5][:, :dh] + res[5][:, dh:]]


def _silu_grad(c, sg):
    return sg * (1.0 + c * (1.0 - sg))


def _conv(x, w, n):
    row = _row(x.shape)
    c = x * w[CONV_W - 1:CONV_W, :]
    for k in range(CONV_W - 1):
        sh = CONV_W - 1 - k
        c = c + w[k:k + 1, :] * jnp.where(row >= sh, pltpu.roll(x, sh, 0), 0.0)
    return c


def gdn_pre_fwd(proj, cw, seq):
    t = proj.shape[0]
    nb = t // seq
    scale = GDN_DH ** -0.5

    def body(xq_ref, xk_ref, xv_ref, wq_ref, wk_ref, wv_ref, q_ref, k_ref, v_ref):
        def act(x_ref, w_ref):
            c = _conv(x_ref[...], w_ref[...], seq)
            return c * jax.nn.sigmoid(c)

        aq, ak = act(xq_ref, wq_ref), act(xk_ref, wk_ref)
        q_ref[...] = aq * lax.rsqrt(jnp.sum(aq * aq, axis=1, keepdims=True) + EPS) * scale
        k_ref[...] = ak * lax.rsqrt(jnp.sum(ak * ak, axis=1, keepdims=True) + EPS)
        v_ref[...] = act(xv_ref, wv_ref)

    xb = lambda off: pl.BlockSpec((seq, 128), lambda b, h: (b, off + h))
    wb = lambda off: pl.BlockSpec((CONV_W, 128), lambda b, h: (0, off + h))
    ob = lambda: pl.BlockSpec((seq, 128), lambda b, h: (b, h))
    return pl.pallas_call(
        body, grid=(nb, GDN_HEADS),
        in_specs=[xb(12), xb(16), xb(20), wb(0), wb(4), wb(8)],
        out_specs=[ob(), ob(), ob()],
        out_shape=[jax.ShapeDtypeStruct((t, 512), F32)] * 3,
        compiler_params=_cp("parallel", "parallel"), name="gdn_pre_fwd")(proj, proj, proj, cw, cw, cw)


def gdn_pre_bwd(proj, cw, dq, dk, dv, seq):
    t = proj.shape[0]
    nb = t // seq
    scale = GDN_DH ** -0.5

    def body(xq_ref, xk_ref, xv_ref, wq_ref, wk_ref, wv_ref, dq_ref, dk_ref, dv_ref,
             dxq_ref, dxk_ref, dxv_ref, dwq_ref, dwk_ref, dwv_ref):
        first = pl.program_id(1) == 0
        row = _row((seq, 128))

        def one(x_ref, w_ref, dy_ref, dx_ref, dw_ref, norm, sc):
            x, w = x_ref[...], w_ref[...]
            c = _conv(x, w, seq)
            sg = jax.nn.sigmoid(c)
            dy = dy_ref[...]
            if norm:
                a = c * sg
                rs = lax.rsqrt(jnp.sum(a * a, axis=1, keepdims=True) + EPS)
                dy = dy * sc
                da = rs * dy - a * (rs * rs * rs) * jnp.sum(dy * a, axis=1, keepdims=True)
            else:
                da = dy
            dc = da * _silu_grad(c, sg)
            dx = dc * w[CONV_W - 1:CONV_W, :]
            dws = [None] * CONV_W
            dws[CONV_W - 1] = jnp.sum(dc * x, axis=0, keepdims=True)
            for k in range(CONV_W - 1):
                sh = CONV_W - 1 - k
                dc_up = jnp.where(row < seq - sh, pltpu.roll(dc, seq - sh, 0), 0.0)
                dx = dx + w[k:k + 1, :] * dc_up
                dws[k] = jnp.sum(dc_up * x, axis=0, keepdims=True)
            dx_ref[...] = dx.astype(BF16)
            dwn = jnp.concatenate(dws, axis=0)

            @pl.when(first)
            def _():
                dw_ref[...] = dwn

            @pl.when(jnp.logical_not(first))
            def _():
                dw_ref[...] += dwn

        one(xq_ref, wq_ref, dq_ref, dxq_ref, dwq_ref, True, scale)
        one(xk_ref, wk_ref, dk_ref, dxk_ref, dwk_ref, True, 1.0)
        one(xv_ref, wv_ref, dv_ref, dxv_ref, dwv_ref, False, 1.0)

    xb = lambda off: pl.BlockSpec((seq, 128), lambda h, b: (b, off + h))
    wb = lambda off: pl.BlockSpec((CONV_W, 128), lambda h, b: (0, off + h))
    ob = lambda: pl.BlockSpec((seq, 128), lambda h, b: (b, h))
    return pl.pallas_call(
        body, grid=(GDN_HEADS, nb),
        in_specs=[xb(12), xb(16), xb(20), wb(0), wb(4), wb(8), ob(), ob(), ob()],
        out_specs=[ob(), ob(), ob()] + [pl.BlockSpec((CONV_W, 128), lambda h, b: (0, h))] * 3,
        out_shape=[jax.ShapeDtypeStruct((t, 512), BF16)] * 3 + [jax.ShapeDtypeStruct((CONV_W, 512), F32)] * 3,
        compiler_params=_cp("parallel", "arbitrary"), name="gdn_pre_bwd")(proj, proj, proj, cw, cw, cw, dq, dk, dv)


def _b16(x):
    return x.astype(BF16)


@jax.custom_vjp
def _mm(a, b):
    return _dot(_b16(a), _b16(b))


_mm.defvjp(lambda a, b: (_mm(a, b), (a, b)),
           lambda res, g: (_dot_nt(_b16(g), _b16(res[1])), _dot_tn(_b16(res[0]), _b16(g))))


@jax.custom_vjp
def _mm_nt(a, b):
    return _dot_nt(_b16(a), _b16(b))


_mm_nt.defvjp(lambda a, b: (_mm_nt(a, b), (a, b)),
              lambda res, g: (_dot(_b16(g), _b16(res[1])), _dot_tn(_b16(g), _b16(res[0]))))


@jax.custom_vjp
def _mm_tn(a, b):
    return _dot_tn(_b16(a), _b16(b))


_mm_tn.defvjp(lambda a, b: (_mm_tn(a, b), (a, b)),
              lambda res, g: (_dot_nt(_b16(res[1]), _b16(g)), _dot(_b16(res[0]), _b16(g))))


def _dot32(a, b, dims=(((1,), (0,)), ((), ()))):
    def split(x):
        hi = x.astype(BF16)
        return hi, (x - hi.astype(F32)).astype(BF16)

    (ah, al), (bh, bl) = split(a), split(b)
    d = lambda x, y: lax.dot_general(x, y, dims, preferred_element_type=F32)
    return d(ah, bh) + (d(ah, bl) + d(al, bh))


INV_BLOCK = 4


def _inv_fwd_many(mats):
    n = mats[0].shape[0]
    r, c = _row((n, n)), _lane((n, n))
    eye = (r == c).astype(F32)

    def same_block(width):
        bits = jnp.int32(width.bit_length() - 1)
        return lax.shift_right_logical(r, bits) == lax.shift_right_logical(c, bits)

    diag = [jnp.where(same_block(INV_BLOCK), a, 0.0) for a in mats]
    invs, pws = [eye - d for d in diag], diag
    for _ in range(INV_BLOCK.bit_length() - 2):
        pws = [_dot32(pw, pw) for pw in pws]
        invs = [inv + _dot32(inv, pw) for inv, pw in zip(invs, pws)]
    width = INV_BLOCK
    while width < n:
        off = jnp.logical_and(same_block(2 * width), jnp.logical_not(same_block(width)))
        invs = [inv - _dot32(_dot32(inv, jnp.where(off, a, 0.0)), inv) for inv, a in zip(invs, mats)]
        width *= 2
    return invs


@jax.custom_vjp
def _inv_saved(a, inv):
    return inv


def _inv_saved_bwd(inv, g):
    tg = _dot32(inv, g, (((0,), (0,)), ((), ())))
    return -_dot32(tg, inv, (((1,), (1,)), ((), ()))), jnp.zeros_like(inv)


_inv_saved.defvjp(lambda a, inv: (inv, inv), _inv_saved_bwd)


def _gdn_decay(gcol):
    c = CHUNK
    ri, ci = _row((c, c)), _lane((c, c))
    incl, eye = ri >= ci, ri == ci
    grow = jnp.sum(jnp.where(eye, gcol, 0.0), axis=0, keepdims=True)
    gc = jnp.sum(jnp.where(incl, grow, 0.0), axis=1, keepdims=True)
    gcr = jnp.sum(jnp.where(eye, gc, 0.0), axis=0, keepdims=True)
    gl = jnp.sum(jnp.where(_row((c, 1)) == c - 1, gc, 0.0), axis=0, keepdims=True)
    return gc, gl, jnp.exp(jnp.where(incl, gc - gcr, NEG))


def _gdn_a(k, bcol, decay):
    c = CHUNK
    return jnp.where(_row((c, c)) > _lane((c, c)), _mm_nt(k * bcol, k) * decay, 0.0)


def _gdn_chunk(q, k, v, gcol, bcol, state, gg, nw, inv_saved):
    c = CHUNK
    incl = _row((c, c)) >= _lane((c, c))
    gc, gl, decay = _gdn_decay(gcol)
    kb, vb = k * bcol, v * bcol
    inv = _inv_saved(_gdn_a(k, bcol, decay), inv_saved)
    eg = jnp.exp(gc)
    u = _mm(inv, vb)
    w = _mm(inv, kb * eg)
    pm = jnp.where(incl, _mm_nt(q, k) * decay, 0.0)
    kd = k * jnp.exp(gl - gc)
    qd = q * eg
    v_new = u - _mm(w, state)
    o = _mm(qd, state) + _mm(pm, v_new)
    state_new = state * jnp.exp(gl) + _mm_tn(kd, v_new)
    y = o * _rstd(o) * nw * (gg * jax.nn.sigmoid(gg))
    return y, state_new


_gdn_chunks = jax.vmap(_gdn_chunk, in_axes=(0, 0, 0, 0, 0, 0, 0, None, 0))


def _gdn_chain_inputs(chains, p, r, c, q_ref, k_ref, v_ref, g_ref, gg_ref, inv_ref):
    cols = {nm: [] for nm in ("q", "k", "v", "g", "b", "gg", "inv")}
    for b, hh in chains:
        h = GDN_HPS * p + hh
        ln = slice(hh * 128, (hh + 1) * 128)
        gt = g_ref[b, r, :]
        cols["q"].append(q_ref[b, r, ln])
        cols["k"].append(k_ref[b, r, ln])
        cols["v"].append(v_ref[b, r, ln])
        cols["g"].append(_pick_lane(gt, 8 + h))
        cols["b"].append(_pick_lane(gt, 12 + h))
        cols["gg"].append(gg_ref[b, r, ln])
        cols["inv"].append(inv_ref[b, hh, c])
    return [jnp.stack(cols[nm]) for nm in ("q", "k", "v", "g", "b", "gg", "inv")]


GDN_CB = 8
GDN_INV_CB = 16
GDN_HPS = 4


def gdn_inv(k, gates, seq):
    t = k.shape[0]
    nb, nc = t // seq, seq // CHUNK
    cb = min(GDN_INV_CB, nc)
    rb = cb * CHUNK
    nsb = seq // rb

    def body(k_ref, g_ref, o_ref):
        h = pl.program_id(1)
        mats = []
        for c in range(cb):
            r = slice(c * CHUNK, (c + 1) * CHUNK)
            gt = g_ref[r, :]
            _, _, decay = _gdn_decay(_pick_lane(gt, 8 + h))
            mats.append(_gdn_a(k_ref[r, :], _pick_lane(gt, 12 + h), decay))
        for c, inv in enumerate(_inv_fwd_many(mats)):
            o_ref[c] = inv

    return pl.pallas_call(
        body, grid=(nb, GDN_HEADS, nsb),
        in_specs=[pl.BlockSpec((rb, 128), lambda b, h, s: (b * nsb + s, h)),
                  pl.BlockSpec((rb, 128), lambda b, h, s: (b * nsb + s, 0))],
        out_specs=pl.BlockSpec((None, None, cb, CHUNK, CHUNK), lambda b, h, s: (b, h, s, 0, 0)),
        out_shape=jax.ShapeDtypeStruct((nb, GDN_HEADS, nc, CHUNK, CHUNK), F32),
        compiler_params=_cp("parallel", "parallel", "parallel"), name="gdn_inv")(k, gates)


def _gdn_specs(nb, nsb, cb, rev):
    blk = (lambda s: nsb - 1 - s) if rev else (lambda s: s)
    rb = cb * CHUNK
    pair = lambda off=0: pl.BlockSpec((nb, rb, 128 * GDN_HPS), lambda s, p: (0, blk(s), off + p))
    gate = lambda: pl.BlockSpec((nb, rb, 128), lambda s, p: (0, blk(s), 0))
    mats = lambda n: pl.BlockSpec((nb, GDN_HPS, cb, n, n), lambda s, p: (0, p, blk(s), 0, 0))
    return pair, gate, mats


def gdn_fwd(q, k, v, gates, proj, nw, inv, seq):
    t = q.shape[0]
    nb, nc = t // seq, seq // CHUNK
    cb = GDN_CB
    nsb = nc // cb
    chains = [(b, hh) for b in range(nb) for hh in range(GDN_HPS)]
    nch = len(chains)
    pair, gate, mats = _gdn_specs(nb, nsb, cb, False)

    def body(q_ref, k_ref, v_ref, g_ref, gg_ref, inv_ref, nw_ref, y_ref, st_ref, carry):
        s, p = pl.program_id(0), pl.program_id(1)

        @pl.when(s == 0)
        def _():
            for ci in range(nch):
                carry[p * nch + ci] = jnp.zeros((GDN_DH, GDN_DH), F32)

        def step(c, states):
            r = _rows(c, CHUNK)
            for ci, (b, hh) in enumerate(chains):
                st_ref[b, hh, c] = states[ci]
            ins = _gdn_chain_inputs(chains, p, r, c, q_ref, k_ref, v_ref, g_ref, gg_ref, inv_ref)
            y, states = _gdn_chunks(*ins[:5], states, ins[5], nw_ref[...], ins[6])
            for ci, (b, hh) in enumerate(chains):
                y_ref[b, r, hh * 128:(hh + 1) * 128] = y[ci]
            return states

        states = lax.fori_loop(0, cb, step, jnp.stack([carry[p * nch + ci] for ci in range(nch)]))
        for ci in range(nch):
            carry[p * nch + ci] = states[ci]

    v3 = lambda a: a.reshape(nb, seq, a.shape[1])
    y, st = pl.pallas_call(
        body, grid=(nsb, GDN_HEADS // GDN_HPS),
        in_specs=[pair(), pair(), pair(), gate(), pair(24 // GDN_HPS), mats(CHUNK), pl.BlockSpec((1, 128), lambda s, p: (0, 0))],
        out_specs=[pair(), mats(GDN_DH)],
        out_shape=[jax.ShapeDtypeStruct((nb, seq, 512), F32),
                   jax.ShapeDtypeStruct((nb, GDN_HEADS, nc, GDN_DH, GDN_DH), F32)],
        scratch_shapes=[pltpu.VMEM((GDN_HEADS // GDN_HPS * nch, GDN_DH, GDN_DH), F32)],
        compiler_params=_cp("arbitrary", "arbitrary"), name="gdn_fwd")(v3(q), v3(k), v3(v), v3(gates), v3(proj), inv, nw)
    return y.reshape(t, 512), st


def gdn_bwd(q, k, v, gates, proj, nw, inv, states, dy, seq):
    t = q.shape[0]
    nb, nc = t // seq, seq // CHUNK
    cb = GDN_CB // 2
    nsb = nc // cb
    chains = [(b, hh) for b in range(nb) for hh in range(GDN_HPS)]
    nch = len(chains)
    pair, gate, mats = _gdn_specs(nb, nsb, cb, True)

    def body(q_ref, k_ref, v_ref, g_ref, gg_ref, inv_ref, st_ref, dy_ref, nw_ref,
             dq_ref, dk_ref, dv_ref, dgg_ref, dg_ref, dnw_ref, carry):
        s, p = pl.program_id(0), pl.program_id(1)

        @pl.when((s == 0) & (p == 0))
        def _():
            dnw_ref[...] = jnp.zeros_like(dnw_ref)

        @pl.when(p == 0)
        def _():
            dg_ref[...] = jnp.zeros_like(dg_ref)

        @pl.when(s == 0)
        def _():
            for ci in range(nch):
                carry[p * nch + ci] = jnp.zeros((GDN_DH, GDN_DH), F32)

        def step(idx, dstates):
            c = cb - 1 - idx
            r = _rows(c, CHUNK)
            ins = _gdn_chain_inputs(chains, p, r, c, q_ref, k_ref, v_ref, g_ref, gg_ref, inv_ref)
            st = jnp.stack([st_ref[b, hh, c] for b, hh in chains])
            dy = jnp.stack([dy_ref[b, r, hh * 128:(hh + 1) * 128] for b, hh in chains])
            _, vjp = jax.vjp(_gdn_chunks, *ins[:5], st, ins[5], nw_ref[...], ins[6])
            dq, dk, dv, dgc, dbc, dstates, dgg, dnw, _ = vjp((dy, dstates))
            dnw_ref[...] += dnw
            lane = _lane((CHUNK, 128))
            for ci, (b, hh) in enumerate(chains):
                h = GDN_HPS * p + hh
                ln = slice(hh * 128, (hh + 1) * 128)
                dq_ref[b, r, ln] = dq[ci]
                dk_ref[b, r, ln] = dk[ci]
                dv_ref[b, r, ln] = dv[ci]
                dgg_ref[b, r, ln] = dgg[ci].astype(BF16)
                dg_ref[b, r, :] = jnp.where(lane == 8 + h, dgc[ci], jnp.where(lane == 12 + h, dbc[ci], dg_ref[b, r, :]))
            return dstates

        dstates = lax.fori_loop(0, cb, step, jnp.stack([carry[p * nch + ci] for ci in range(nch)]))
        for ci in range(nch):
            carry[p * nch + ci] = dstates[ci]

    v3 = lambda a: a.reshape(nb, seq, a.shape[1])
    res = pl.pallas_call(
        body, grid=(nsb, GDN_HEADS // GDN_HPS),
        in_specs=[pair(), pair(), pair(), gate(), pair(24 // GDN_HPS), mats(CHUNK), mats(GDN_DH), pair(),
                  pl.BlockSpec((1, 128), lambda s, p: (0, 0))],
        out_specs=[pair(), pair(), pair(), pair(), gate(), pl.BlockSpec((1, 128), lambda s, p: (0, 0))],
        out_shape=[jax.ShapeDtypeStruct((nb, seq, 512), F32)] * 3 + [jax.ShapeDtypeStruct((nb, seq, 512), BF16)]
                  + [jax.ShapeDtypeStruct((nb, seq, 128), F32), jax.ShapeDtypeStruct((1, 128), F32)],
        scratch_shapes=[pltpu.VMEM((GDN_HEADS // GDN_HPS * nch, GDN_DH, GDN_DH), F32)],
        compiler_params=_cp("arbitrary", "arbitrary"),
        name="gdn_bwd")(v3(q), v3(k), v3(v), v3(gates), v3(proj), inv, states, v3(dy), nw)
    return [a.reshape(t, a.shape[2]) for a in res[:5]] + [res[5]]


def loss_head(y, target, tm=512):
    t, d = y.shape

    def body(y_ref, t_ref, s_ref, dy_ref):
        @pl.when(pl.program_id(0) == 0)
        def _():
            s_ref[...] = jnp.zeros_like(s_ref)

        err = y_ref[...] - t_ref[...]
        s_ref[...] += jnp.sum(err * err, axis=0, keepdims=True)
        dy_ref[...] = err * (1.0 / d)

    return pl.pallas_call(
        body, grid=(t // tm,),
        in_specs=[pl.BlockSpec((tm, d), lambda i: (i, 0)), pl.BlockSpec((tm, d), lambda i: (i, 0))],
        out_specs=[pl.BlockSpec((1, d), lambda i: (0, 0)), pl.BlockSpec((tm, d), lambda i: (i, 0))],
        out_shape=[jax.ShapeDtypeStruct((1, d), F32), jax.ShapeDtypeStruct((t, d), F32)],
        compiler_params=_cp("arbitrary"), name="loss_head")(y, target)


def _place():
    return lax.axis_index("x"), lax.axis_index("y"), lax.axis_index("c")


def _peer(k):
    x, y, c = _place()
    px = 1 - x if (k >> 2) & 1 else x
    py = 1 - y if (k >> 1) & 1 else y
    pc = 1 - c if k & 1 else c
    return (px, py, pc), 4 * px + 2 * py + pc


_ANY = pl.BlockSpec(memory_space=pl.ANY)
_SEM = pl.BlockSpec(memory_space=pltpu.SEMAPHORE)
_EFFECT = pltpu.SideEffectType.DATAFLOW_SIDE_EFFECTING


def _me():
    x, y, c = _place()
    return 4 * x + 2 * y + c


def _remote_copy(ins, lands, scatter, send_sems, recv_sems, a, k, arriving):
    pid, pidx = _peer(k)
    return pltpu.make_async_remote_copy(src_ref=ins[a].at[pidx] if scatter[a] else ins[a],
                                        dst_ref=lands[a].at[pidx if arriving else _me()],
                                        send_sem=send_sems.at[a * N_DEV + k], recv_sem=recv_sems.at[a * N_DEV + k],
                                        device_id=pid, device_id_type=MESH)


def _local_copy(ins, lands, scatter, loc_sems, a):
    me = _me()
    return pltpu.make_async_copy(ins[a].at[me] if scatter[a] else ins[a], lands[a].at[me], loc_sems.at[a])


def exchange_start(arrays, scatter, name, after):
    n = len(arrays)
    lands = [lax.empty(a.shape if s else (N_DEV,) + a.shape, a.dtype) for a, s in zip(arrays, scatter)]

    def body(*refs):
        ins, lds = refs[:n], refs[n:2 * n]
        send_sems, recv_sems, loc_sems = refs[2 * n + 1:2 * n + 4]
        token = refs[-1]
        for k in range(1, N_DEV):
            for a in range(n):
                _remote_copy(ins, lds, scatter, send_sems, recv_sems, a, k, False).start()
        for a in range(n):
            _local_copy(ins, lds, scatter, loc_sems, a).start()
        token[...] = jnp.zeros_like(token)

    hbm = lambda a: pltpu.HBM(a.shape, a.dtype)
    res = pl.pallas_call(
        body, name=name,
        in_specs=[_ANY] * (2 * n + 1),
        out_specs=[_SEM, _SEM, _SEM] + [_ANY] * (2 * n) + [pl.BlockSpec(memory_space=pltpu.VMEM)],
        out_shape=[pltpu.SemaphoreType.DMA((n * N_DEV,)), pltpu.SemaphoreType.DMA((n * N_DEV,)),
                   pltpu.SemaphoreType.DMA((n,))]
                  + [hbm(a) for a in arrays] + [hbm(a) for a in lands] + [jax.ShapeDtypeStruct((8, 128), F32)],
        input_output_aliases={i: 3 + i for i in range(2 * n)},
        compiler_params=pltpu.CompilerParams(has_side_effects=_EFFECT),
    )(*[pltpu.with_memory_space_constraint(a, pltpu.HBM) for a in list(arrays) + lands], after)
    return res[0:3], res[3:3 + n], res[3 + n:3 + 2 * n], res[-1]


def exchange_wait(sems, arrays, lands, scatter, after, name):
    n = len(arrays)

    def body(*refs):
        ins, lds = refs[:n], refs[n:2 * n]
        ssem, rsem, lsem = refs[2 * n:2 * n + 3]
        for a in range(n):
            _local_copy(ins, lds, scatter, lsem, a).wait()
        for k in range(1, N_DEV):
            for a in range(n):
                _remote_copy(ins, lds, scatter, ssem, rsem, a, k, True).wait_recv()
        for k in range(1, N_DEV):
            for a in range(n):
                _remote_copy(ins, lds, scatter, ssem, rsem, a, k, False).wait_send()

    hbm = lambda a: pltpu.HBM(a.shape, a.dtype)
    res = pl.pallas_call(
        body, name=name,
        in_specs=[_ANY] * (2 * n) + [_SEM, _SEM, _SEM, _ANY],
        out_specs=[_ANY] * (2 * n),
        out_shape=[hbm(a) for a in arrays] + [hbm(a) for a in lands],
        input_output_aliases={i: i for i in range(2 * n)},
        compiler_params=pltpu.CompilerParams(has_side_effects=_EFFECT),
    )(*arrays, *lands, *sems, after)
    return list(res[n:])


def gather_two_level(arrays, name):
    n = len(arrays)

    def body(*refs):
        ins, outs = refs[:n], refs[n:2 * n]
        send_sems, recv_sems, loc_sems = refs[2 * n:]
        x, y, c = _place()
        sibling = (x, y, 1 - c)
        chips = [(1 - x, y), (x, 1 - y), (1 - x, 1 - y)]

        def slot(pos):
            return 4 * pos[0] + 2 * pos[1] + pos[2]

        def copy(a, k, block, to, from_input=False):
            return pltpu.make_async_remote_copy(
                src_ref=ins[a] if from_input else outs[a].at[slot(block)], dst_ref=outs[a].at[slot(block)],
                send_sem=send_sems.at[a, k], recv_sem=recv_sems.at[a, k], device_id=to, device_id_type=MESH)

        me = (x, y, c)
        mine = [pltpu.make_async_copy(ins[a], outs[a].at[slot(me)], loc_sems.at[a]) for a in range(n)]
        for cp in mine:
            cp.start()
        first = [copy(a, 0, me, sibling, True) for a in range(n)]
        first += [copy(a, 1 + j, me, (*chip, c), True) for j, chip in enumerate(chips) for a in range(n)]
        for cp in first:
            cp.start()
        passed = []
        for j, chip in enumerate(chips):
            for a in range(n):
                copy(a, 1 + j, (*chip, c), me).wait_recv()
                passed.append(copy(a, 4 + j, (*chip, c), sibling))
                passed[-1].start()
        for a in range(n):
            copy(a, 0, sibling, me).wait_recv()
            for j, chip in enumerate(chips):
                copy(a, 4 + j, (*chip, 1 - c), me).wait_recv()
        for cp in first + passed:
            cp.wait_send()
        for cp in mine:
            cp.wait()

    return pl.pallas_call(
        body, in_specs=[_ANY] * n, out_specs=[_ANY] * n,
        out_shape=[jax.ShapeDtypeStruct((N_DEV,) + a.shape, a.dtype) for a in arrays],
        scratch_shapes=[pltpu.SemaphoreType.DMA((n, 7)), pltpu.SemaphoreType.DMA((n, 7)), pltpu.SemaphoreType.DMA((n,))],
        name=name)(*arrays)


def exchange_begin(arrays, scatter, name, after):
    sems, arrays_thru, lands_thru, token = exchange_start(arrays, scatter, name + "_start", after)
    return (sems, arrays_thru, lands_thru, scatter, name), token


def exchange_end(state, after):
    sems, arrays_thru, lands_thru, scatter, name = state
    return exchange_wait(sems, arrays_thru, lands_thru, scatter, after, name + "_wait")


def adamw_reduce(slots, w, m, v, l, name, after=None, prev=None):
    nl, r, c = w.shape
    tr = r
    while tr * c * 4 > (1 << 20) and tr % 16 == 0:
        tr //= 2
    bc1 = 1.0 - ADAM_B1 ** ADAM_STEP
    bc2 = 1.0 - ADAM_B2 ** ADAM_STEP

    def body(s_ref, w_ref, m_ref, v_ref, *rest):
        g_ref, d_ref, nm_ref, nv_ref = rest[-4:]
        g = s_ref[0].astype(F32)
        for j in range(1, N_DEV):
            g = g + s_ref[j].astype(F32)
        nm = ADAM_B1 * m_ref[...] + (1.0 - ADAM_B1) * g
        nv = ADAM_B2 * v_ref[...] + (1.0 - ADAM_B2) * (g * g)
        g_ref[...] = g
        nm_ref[...] = nm
        nv_ref[...] = nv
        d_ref[...] = -ADAM_LR * ((nm / bc1) / (jnp.sqrt(nv / bc2) + ADAM_EPS) + ADAM_WD * w_ref[...])

    blk = lambda: pl.BlockSpec((None, tr, c), lambda i: (l, i, 0))
    extra = ([] if after is None else [after]) + ([] if prev is None else list(prev))
    first_prev = 4 + (after is not None)
    return pl.pallas_call(
        body, grid=(r // tr,),
        in_specs=[pl.BlockSpec((N_DEV, tr, c), lambda i: (0, i, 0)), blk(), blk(), blk()] + [_ANY] * len(extra),
        out_specs=[blk(), blk(), blk(), blk()],
        out_shape=[jax.ShapeDtypeStruct((nl, r, c), F32)] * 4,
        input_output_aliases={} if prev is None else {first_prev + j: j for j in range(4)},
        compiler_params=_cp("parallel"), name=name)(slots, w, m, v, *extra)


BIG = ("ffn1_w_in", "ffn1_w_out", "w_in", "gdn_conv", "w_out", "ffn2_w_in", "ffn2_w_out")
GROUPS = (BIG[0:2], BIG[2:5], BIG[5:7])
SMALL = ("ffn1_norm", "mix_norm", "fox_q_norm", "fox_k_norm", "fox_f_bias", "gdn_a_log", "gdn_dt_bias",
         "gdn_out_norm", "ffn2_norm")
WEIGHTS = ("ffn1_norm", "ffn1_w_in", "ffn1_w_out", "mix_norm", "w_in", "fox_q_norm", "fox_k_norm", "fox_f_bias",
           "gdn_conv", "gdn_a_log", "gdn_dt_bias", "gdn_out_norm", "w_out", "ffn2_norm", "ffn2_w_in", "ffn2_w_out")
IN_COLS = (("fq", 512), ("fk", 512), ("fv", 512), ("ff", 8), ("gq", 512), ("gk", 512), ("gv", 512),
           ("ga", 4), ("gb", 4), ("gg", 512))
MY_BIG = ("fq", "fk", "fv", "gq", "gk", "gv", "gg")
MY_SMALL = ("ff", "ga", "gb")
SMALL_ROWS = 8 * 128


def _in_cols_to_mine(w):
    off, parts = 0, {}
    for nm, wd in IN_COLS:
        parts[nm] = w[:, off:off + wd]
        off += wd
    small = jnp.concatenate([parts[nm] for nm in MY_SMALL], axis=1)
    small = jnp.pad(small, ((0, 0), (0, 128 - small.shape[1])))
    return jnp.concatenate([parts[nm] for nm in MY_BIG] + [small], axis=1)


def _in_cols_from_mine(g):
    parts = {nm: g[:, i * 512:(i + 1) * 512] for i, nm in enumerate(MY_BIG)}
    off = N_BIG
    for nm in MY_SMALL:
        wd = dict(IN_COLS)[nm]
        parts[nm] = g[:, off:off + wd]
        off += wd
    return jnp.concatenate([parts[nm] for nm, _ in IN_COLS], axis=1)


def _pack_small(vals):
    rows = []
    nl = vals[SMALL[0]].shape[0]
    for l in range(nl):
        for nm in SMALL:
            v = vals[nm][l].reshape(-1)
            pad = (-v.shape[0]) % SMALL_ROWS
            rows.append(jnp.pad(v, (0, pad)).reshape(-1, 128))
    return jnp.concatenate(rows, axis=0)


def _unpack_small(packed, like):
    out = {nm: [] for nm in SMALL}
    row = 0
    nl = like[SMALL[0]].shape[0]
    for l in range(nl):
        for nm in SMALL:
            n = like[nm].shape[1]
            nr = -(-n // SMALL_ROWS) * 8
            out[nm].append(packed[row:row + nr].reshape(-1)[:n])
            row += nr
    return {nm: jnp.stack(v) for nm, v in out.items()}


def kernel(x, ffn1_norm, ffn1_w_in, ffn1_w_out, mix_norm, w_in, fox_q_norm, fox_k_norm, fox_f_bias, gdn_conv, gdn_a_log, gdn_dt_bias, gdn_out_norm, w_out, ffn2_norm, ffn2_w_in, ffn2_w_out, loss_target, m_ffn1_norm, m_ffn1_w_in, m_ffn1_w_out, m_mix_norm, m_w_in, m_fox_q_norm, m_fox_k_norm, m_fox_f_bias, m_gdn_conv, m_gdn_a_log, m_gdn_dt_bias, m_gdn_out_norm, m_w_out, m_ffn2_norm, m_ffn2_w_in, m_ffn2_w_out, v_ffn1_norm, v_ffn1_w_in, v_ffn1_w_out, v_mix_norm, v_w_in, v_fox_q_norm, v_fox_k_norm, v_fox_f_bias, v_gdn_conv, v_gdn_a_log, v_gdn_dt_bias, v_gdn_out_norm, v_w_out, v_ffn2_norm, v_ffn2_w_in, v_ffn2_w_out):
    wts = dict(ffn1_norm=ffn1_norm, ffn1_w_in=ffn1_w_in, ffn1_w_out=ffn1_w_out, mix_norm=mix_norm, w_in=w_in,
               fox_q_norm=fox_q_norm, fox_k_norm=fox_k_norm, fox_f_bias=fox_f_bias, gdn_conv=gdn_conv,
               gdn_a_log=gdn_a_log, gdn_dt_bias=gdn_dt_bias, gdn_out_norm=gdn_out_norm, w_out=w_out,
               ffn2_norm=ffn2_norm, ffn2_w_in=ffn2_w_in, ffn2_w_out=ffn2_w_out)
    mom = dict(ffn1_norm=m_ffn1_norm, ffn1_w_in=m_ffn1_w_in, ffn1_w_out=m_ffn1_w_out, mix_norm=m_mix_norm, w_in=m_w_in,
               fox_q_norm=m_fox_q_norm, fox_k_norm=m_fox_k_norm, fox_f_bias=m_fox_f_bias, gdn_conv=m_gdn_conv,
               gdn_a_log=m_gdn_a_log, gdn_dt_bias=m_gdn_dt_bias, gdn_out_norm=m_gdn_out_norm, w_out=m_w_out,
               ffn2_norm=m_ffn2_norm, ffn2_w_in=m_ffn2_w_in, ffn2_w_out=m_ffn2_w_out)
    var = dict(ffn1_norm=v_ffn1_norm, ffn1_w_in=v_ffn1_w_in, ffn1_w_out=v_ffn1_w_out, mix_norm=v_mix_norm, w_in=v_w_in,
               fox_q_norm=v_fox_q_norm, fox_k_norm=v_fox_k_norm, fox_f_bias=v_fox_f_bias, gdn_conv=v_gdn_conv,
               gdn_a_log=v_gdn_a_log, gdn_dt_bias=v_gdn_dt_bias, gdn_out_norm=v_gdn_out_norm, w_out=v_w_out,
               ffn2_norm=v_ffn2_norm, ffn2_w_in=v_ffn2_w_in, ffn2_w_out=v_ffn2_w_out)
    nb, seq, d = x.shape
    t = nb * seq
    depth = ffn1_norm.shape[0]

    stages = [(l, gi) for l in range(depth) for gi in range(len(GROUPS))]

    def shards_of(l, gi):
        return [wts[nm][l] if nm == "gdn_conv" else wts[nm][l].astype(BF16) for nm in GROUPS[gi]]

    def behind(nw, token):
        return nw if token is None else nw + token[0:1, 0:1]

    def small_params(l):
        return dict(
            n1=ffn1_norm[l][None], nmix=mix_norm[l][None], n2=ffn2_norm[l][None],
            qw=fox_q_norm[l][None], kw=fox_k_norm[l][None], onw=gdn_out_norm[l][None],
            gp=jnp.concatenate([
                jnp.concatenate([fox_f_bias[l], gdn_dt_bias[l], jnp.zeros((116,), F32)])[None],
                jnp.concatenate([jnp.zeros((8,), F32), gdn_a_log[l], jnp.zeros((116,), F32)])[None],
                jnp.zeros((6, 128), F32)], axis=0))

    h = x.reshape(t, d)
    landed = gather_two_level(shards_of(0, 0), "gather_0")
    saved = [dict(p=small_params(l)) for l in range(depth)]
    for k, (l, gi) in enumerate(stages):
        s, w, token = saved[l], landed, None
        p = s["p"]
        if k + 1 < len(stages):
            nl, ng = stages[k + 1]
            state, token = exchange_begin(shards_of(nl, ng), [False] * len(GROUPS[ng]), f"gather_{k + 1}", landed[0])
        if gi == 0:
            fb = w[0].shape[2]
            p["w1i"], p["w1o"] = w[0].reshape(2, 4, d, fb), w[1].reshape(4, fb, d)
            s["x0"] = h
            h, *s["ffn1"] = ffn_fwd(h, behind(p["n1"], token), p["w1i"], p["w1o"])
            s["x1"] = h
        elif gi == 1:
            p["wi"] = _in_cols_to_mine(w[0].transpose(1, 0, 2).reshape(d, -1))
            p["cw"] = w[1].transpose(1, 0, 2).reshape(CONV_W, -1)
            p["wo"] = w[2].reshape(d, d)
            proj, hn = inproj_fwd(h, behind(p["nmix"], token), p["wi"])
            gates = gates_fwd(proj, p["gp"], seq)
            yf, lse = attn_fwd(proj, gates, p["qw"], p["kw"], seq, tq=min(seq, ATTN_TQ_FWD))
            qh, kh, vh = gdn_pre_fwd(proj, p["cw"], seq)
            inv = gdn_inv(kh, gates, seq)
            yg, st = gdn_fwd(qh, kh, vh, gates, proj, p["onw"], inv, seq)
            h, ycat = outproj_fwd(h, yf, yg, p["wo"])
            s.update(x2=h, proj=proj, hn=hn, gates=gates, yf=yf, lse=lse, qh=qh, kh=kh, vh=vh, st=st, inv=inv, ycat=ycat)
        else:
            fb = w[0].shape[2]
            p["w2i"], p["w2o"] = w[0].reshape(2, 4, d, fb), w[1].reshape(4, fb, d)
            h, *s["ffn2"] = ffn_fwd(h, behind(p["n2"], token), p["w2i"], p["w2o"])
        if k + 1 < len(stages):
            landed = exchange_end(state, h)

    sq, dh = loss_head(h, loss_target.reshape(t, d))
    loss = lax.psum(0.5 * jnp.sum(sq) / d, ("x", "y", "c"))

    got = [None] * len(stages)
    pending, token = None, None
    gsmall = {nm: [None] * depth for nm in SMALL}
    for k in reversed(range(len(stages))):
        l, gi = stages[k]
        s = saved[l]
        p = s["p"]
        if gi != 1:
            nw, xin, wi_, wo_, nm_n, (xn, gu, hh) = (
                (p["n1"], s["x0"], p["w1i"], p["w1o"], "ffn1_norm", s["ffn1"]) if gi == 0 else
                (p["n2"], s["x2"], p["w2i"], p["w2o"], "ffn2_norm", s["ffn2"]))
            dh, dn, dgu, dyh = ffn_bwd(xin, dh, behind(nw, token), gu, wi_, wo_)
            g_in, g_out = wgrad_ffn_in(xn, dgu), wgrad_ffn_out(hh, dyh)
            send = [g_in.reshape(N_DEV, d, g_in.shape[3]), g_out.reshape(N_DEV, -1, d)]
            gsmall[nm_n][l] = dn[0]
        else:
            dyf, dyg, dyb = outproj_bwd(dh, p["wo"], token)
            g_wo = wgrad_2d(s["ycat"], dyb, 512, "wgrad_w_out")
            dq, dk, dv, dga, dqw, dkw = attn_bwd(s["proj"], s["gates"], p["qw"], p["kw"], s["yf"], s["lse"], dyf, seq,
                                                 tq=min(seq, ATTN_TQ_BWD))
            dqh, dkh, dvh, dgg, dgb, donw = gdn_bwd(s["qh"], s["kh"], s["vh"], s["gates"], s["proj"], p["onw"],
                                                     s["inv"], s["st"], dyg, seq)
            dxq, dxk, dxv, dwq, dwk, dwv = gdn_pre_bwd(s["proj"], p["cw"], dqh, dkh, dvh, seq)
            dsm, dgp = gates_bwd(s["proj"], p["gp"], dga, dgb, seq)
            dh, dnmix, dproj = inproj_bwd(s["x1"], dh, p["nmix"], p["wi"], [dq, dk, dv, dxq, dxk, dxv, dgg, dsm])
            g_wi = wgrad_2d(s["hn"], dproj, 512, "wgrad_w_in", F32)
            g_cw = jnp.concatenate([dwq, dwk, dwv], axis=1)
            send = [_in_cols_from_mine(g_wi).reshape(d, N_DEV, -1).transpose(1, 0, 2),
                    g_cw.reshape(CONV_W, N_DEV, -1).transpose(1, 0, 2), g_wo.reshape(N_DEV, -1, d)]
            for nm, val in (("mix_norm", dnmix[0]), ("fox_q_norm", dqw[0]), ("fox_k_norm", dkw[0]),
                            ("fox_f_bias", dgp[0, 0:8]), ("gdn_a_log", dgp[1, 8:12]), ("gdn_dt_bias", dgp[0, 8:12]),
                            ("gdn_out_norm", donw[0])):
                gsmall[nm][l] = val
        flags = [True] * len(send)
        if k == 0:
            send.append(_pack_small({nm: jnp.stack(v) for nm, v in gsmall.items()}))
            flags.append(False)
        prev = dh
        if pending is not None:
            got[pending[1]] = exchange_end(pending[0], dh)
            prev = got[pending[1]][0]
        state, token = exchange_begin(send, flags, f"exchange_grads_{k}", prev)
        pending = (state, k)
    grad_x = dh.reshape(nb, seq, d)

    res = {}

    def update_stage(k, slots, after):
        l, gi = stages[k]
        for i, nm in enumerate(GROUPS[gi]):
            r, c = wts[nm].shape[1:]
            res[nm] = adamw_reduce(slots[i].reshape(N_DEV, r, c), wts[nm], mom[nm], var[nm], l, f"adamw_{nm}_{l}",
                                   after, res.get(nm))
            if after is not None:
                after = res[nm][0]
        return after

    last = token
    for k in range(1, len(stages)):
        last = update_stage(k, got[k], last)
    got[0] = exchange_end(pending[0], last)
    update_stage(0, got[0], None)
    small_like = {nm: wts[nm] for nm in SMALL}
    sm = adamw_reduce(got[0][-1], _pack_small(small_like)[None], _pack_small({nm: mom[nm] for nm in SMALL})[None],
                      _pack_small({nm: var[nm] for nm in SMALL})[None], 0, "adamw_small")
    sm = [_unpack_small(a[0], small_like) for a in sm]
    for nm in SMALL:
        res[nm] = [sm[j][nm] for j in range(4)]
    return (loss, grad_x, *[res[nm][0] for nm in WEIGHTS], *[res[nm][1] for nm in WEIGHTS],
            *[res[nm][2] for nm in WEIGHTS], *[res[nm][3] for nm in WEIGHTS])
```
